```python
import jax, jax.numpy as jnp
from jax import lax
import numpy as np

D_MODEL = 1024
BATCH = 8
SEQ = 8192
DEPTH = 1

CHUNK = 64
N_LEFT_CHUNKS = 8
BAND = (N_LEFT_CHUNKS + 1) * CHUNK
ATT_HEADS = 8
ATT_HEAD_DIM = 64
D_ATT = ATT_HEADS * ATT_HEAD_DIM
D_CONV = D_MODEL // 2
CONV_WIDTH = 3
MAX_REL = 128
N_REL = 2 * MAX_REL + 1
EPS = 1e-6
NEG_BIG = -1e30
IN_SIZES = (D_ATT, D_ATT, D_ATT, D_ATT, D_CONV, D_CONV, D_CONV, D_CONV, D_MODEL, D_MODEL)
IN_COLS = sum(IN_SIZES)
IN_SPLITS = tuple(int(s) for s in np.cumsum(IN_SIZES)[:-1])

kernel_name = "hybrid_chunk_attn_shortconv_gated_block"


def rmsnorm(x, g):
    xf = x.astype(jnp.float32)
    r = lax.rsqrt(jnp.mean(xf * xf, axis=-1, keepdims=True) + EPS)
    return (xf * r).astype(x.dtype) * g


def chunked_rel_attention(q, k, v, rel_bias):
    b, s, h, dh = q.shape
    nc = s // CHUNK
    qc = q.reshape(b, nc, CHUNK, h, dh)
    kc = k.reshape(b, nc, CHUNK, h, dh)
    vc = v.reshape(b, nc, CHUNK, h, dh)
    pad = ((0, 0), (N_LEFT_CHUNKS, 0), (0, 0), (0, 0), (0, 0))
    kp = jnp.pad(kc, pad)
    vp = jnp.pad(vc, pad)
    kb = jnp.concatenate([kp[:, j:j + nc] for j in range(N_LEFT_CHUNKS + 1)], axis=2)
    vb = jnp.concatenate([vp[:, j:j + nc] for j in range(N_LEFT_CHUNKS + 1)], axis=2)
    scale = ATT_HEAD_DIM ** -0.5
    scores = jnp.einsum('bcqhd,bckhd->bhcqk', qc, kb).astype(jnp.float32) * scale
    rel = np.arange(CHUNK)[:, None] + N_LEFT_CHUNKS * CHUNK - np.arange(BAND)[None, :]
    idx = np.clip(rel, -MAX_REL, MAX_REL) + MAX_REL
    bias = rel_bias[:, idx].astype(jnp.float32)
    kpos = np.arange(nc)[:, None] * CHUNK - N_LEFT_CHUNKS * CHUNK + np.arange(BAND)[None, :]
    valid = jnp.asarray(kpos >= 0)
    scores = jnp.where(valid[None, None, :, None, :], scores + bias[None, :, None], NEG_BIG)
    p = jax.nn.softmax(scores, axis=-1).astype(v.dtype)
    out = jnp.einsum('bhcqk,bckhd->bcqhd', p, vb)
    return out.reshape(b, s, h * dh)


def causal_depthwise_conv(u, w, bias):
    s = u.shape[1]
    up = jnp.pad(u, ((0, 0), (CONV_WIDTH - 1, 0), (0, 0)))
    return sum(w[t] * up[:, t:t + s] for t in range(CONV_WIDTH)) + bias


def _fwd_setup_inputs(seed: int = 0) -> dict:
    key = jax.random.key(seed)
    ks = jax.random.split(key, 12)
    f32 = jnp.float32
    x = jax.random.normal(ks[0], (BATCH, SEQ, D_MODEL), f32)
    norm_g = 1.0 + 0.05 * jax.random.normal(ks[1], (DEPTH, D_MODEL), f32)
    w_in = jax.random.normal(ks[2], (DEPTH, D_MODEL, IN_COLS), f32) * D_MODEL ** -0.5
    rel_bias = 0.2 * jax.random.normal(ks[3], (DEPTH, ATT_HEADS, N_REL), f32)
    w_att_out = jax.random.normal(ks[4], (DEPTH, D_ATT, D_MODEL), f32) * D_ATT ** -0.5
    conv_w = jax.random.normal(ks[5], (DEPTH, CONV_WIDTH, D_CONV), f32) * CONV_WIDTH ** -0.5
    conv_b = 0.02 * jax.random.normal(ks[6], (DEPTH, D_CONV), f32)
    w_conv_out = jax.random.normal(ks[7], (DEPTH, D_CONV, D_MODEL), f32) * D_CONV ** -0.5
    w_out = jax.random.normal(ks[8], (DEPTH, D_MODEL, D_MODEL), f32) * D_MODEL ** -0.5
    final_norm_g = 1.0 + 0.05 * jax.random.normal(ks[9], (D_MODEL,), f32)
    return {"x": x, "norm_g": norm_g, "w_in": w_in, "rel_bias": rel_bias,
            "w_att_out": w_att_out, "conv_w": conv_w, "conv_b": conv_b,
            "w_conv_out": w_conv_out, "w_out": w_out, "final_norm_g": final_norm_g}


def _fwd_reference(x, norm_g, w_in, rel_bias, w_att_out, conv_w, conv_b, w_conv_out, w_out, final_norm_g):
    b, s, _ = x.shape
    for l in range(DEPTH):
        h = rmsnorm(x, norm_g[l])
        proj = jnp.einsum('bsd,de->bse', h, w_in[l])
        q, k, v, z_att, gb, gc, u, z_conv, g_att, g_conv = jnp.split(proj, IN_SPLITS, axis=-1)
        att = chunked_rel_attention(q.reshape(b, s, ATT_HEADS, ATT_HEAD_DIM),
                                    k.reshape(b, s, ATT_HEADS, ATT_HEAD_DIM),
                                    v.reshape(b, s, ATT_HEADS, ATT_HEAD_DIM),
                                    rel_bias[l])
        y_att = jnp.einsum('bsc,cd->bsd', att * jax.nn.silu(z_att), w_att_out[l])
        vconv = causal_depthwise_conv(gc * u, conv_w[l], conv_b[l])
        y_conv = jnp.einsum('bsc,cd->bsd', gb * vconv * jax.nn.silu(z_conv), w_conv_out[l])
        m = jax.nn.sigmoid(g_att) * y_att + jax.nn.sigmoid(g_conv) * y_conv
        x = x + jnp.einsum('bsd,de->bse', m, w_out[l])
    return rmsnorm(x, final_norm_g)


import jax as _jax
import jax.numpy as _jnp

TWIN_FORMAT = 'train_step'
FWD_PARAMS = ['x', 'norm_g', 'w_in', 'rel_bias', 'w_att_out', 'conv_w', 'conv_b', 'w_conv_out', 'w_out', 'final_norm_g']
TWIN_WEIGHTS = ['norm_g', 'w_in', 'rel_bias', 'w_att_out', 'conv_w', 'conv_b', 'w_conv_out', 'w_out', 'final_norm_g']
TWIN_DIFF_INPUT = 'x'
TWIN_INPUTS = ['x', 'norm_g', 'w_in', 'rel_bias', 'w_att_out', 'conv_w', 'conv_b', 'w_conv_out', 'w_out', 'final_norm_g', 'loss_target', 'm_norm_g', 'm_w_in', 'm_rel_bias', 'm_w_att_out', 'm_conv_w', 'm_conv_b', 'm_w_conv_out', 'm_w_out', 'm_final_norm_g', 'v_norm_g', 'v_w_in', 'v_rel_bias', 'v_w_att_out', 'v_conv_w', 'v_conv_b', 'v_w_conv_out', 'v_w_out', 'v_final_norm_g']
TWIN_OUTPUTS = ['loss', 'grad_x', 'grad_norm_g', 'grad_w_in', 'grad_rel_bias', 'grad_w_att_out', 'grad_conv_w', 'grad_conv_b', 'grad_w_conv_out', 'grad_w_out', 'grad_final_norm_g', 'delta_norm_g', 'delta_w_in', 'delta_rel_bias', 'delta_w_att_out', 'delta_conv_w', 'delta_conv_b', 'delta_w_conv_out', 'delta_w_out', 'delta_final_norm_g', 'new_m_norm_g', 'new_m_w_in', 'new_m_rel_bias', 'new_m_w_att_out', 'new_m_conv_w', 'new_m_conv_b', 'new_m_w_conv_out', 'new_m_w_out', 'new_m_final_norm_g', 'new_v_norm_g', 'new_v_w_in', 'new_v_rel_bias', 'new_v_w_att_out', 'new_v_conv_w', 'new_v_conv_b', 'new_v_w_conv_out', 'new_v_w_out', 'new_v_final_norm_g']
TWIN_LEAF_KINDS = {'loss': 'loss', 'grad_x': 'grad_x', 'grad_norm_g': 'grad_w', 'grad_w_in': 'grad_w', 'grad_rel_bias': 'grad_w', 'grad_w_att_out': 'grad_w', 'grad_conv_w': 'grad_w', 'grad_conv_b': 'grad_w', 'grad_w_conv_out': 'grad_w', 'grad_w_out': 'grad_w', 'grad_final_norm_g': 'grad_w', 'delta_norm_g': 'delta_w', 'delta_w_in': 'delta_w', 'delta_rel_bias': 'delta_w', 'delta_w_att_out': 'delta_w', 'delta_conv_w': 'delta_w', 'delta_conv_b': 'delta_w', 'delta_w_conv_out': 'delta_w', 'delta_w_out': 'delta_w', 'delta_final_norm_g': 'delta_w', 'new_m_norm_g': 'new_m', 'new_m_w_in': 'new_m', 'new_m_rel_bias': 'new_m', 'new_m_w_att_out': 'new_m', 'new_m_conv_w': 'new_m', 'new_m_conv_b': 'new_m', 'new_m_w_conv_out': 'new_m', 'new_m_w_out': 'new_m', 'new_m_final_norm_g': 'new_m', 'new_v_norm_g': 'new_v', 'new_v_w_in': 'new_v', 'new_v_rel_bias': 'new_v', 'new_v_w_att_out': 'new_v', 'new_v_conv_w': 'new_v', 'new_v_conv_b': 'new_v', 'new_v_w_conv_out': 'new_v', 'new_v_w_out': 'new_v', 'new_v_final_norm_g': 'new_v'}


def _forward(args):
    return _fwd_reference(*[args[k] for k in FWD_PARAMS])


def _output_shape():
    def fwd():
        inp = _fwd_setup_inputs(0)
        return _fwd_reference(*[inp[k] for k in FWD_PARAMS])
    out = _jax.eval_shape(fwd)
    return out.shape, out.dtype

N_MICROBATCH = 1
ADAM_LR = 0.001
ADAM_B1 = 0.9
ADAM_B2 = 0.999
ADAM_EPS = 1e-08
ADAM_WD = 0.01
ADAM_STEP = 10
PER_EXAMPLE_BATCH_AXIS = {'x': 0, 'loss_target': 0}
SHARED_INPUTS = []
_WEIGHT_DTYPES = {'norm_g': _jnp.float32, 'w_in': _jnp.float32, 'rel_bias': _jnp.float32, 'w_att_out': _jnp.float32, 'conv_w': _jnp.float32, 'conv_b': _jnp.float32, 'w_conv_out': _jnp.float32, 'w_out': _jnp.float32, 'final_norm_g': _jnp.float32}
MOMENT_SCALE = {'norm_g': 1.695038e-01, 'w_in': 6.544743e-02, 'rel_bias': 7.437402e-03, 'w_att_out': 1.151508e-02, 'conv_w': 1.070367e-01, 'conv_b': 1.112377e-01, 'w_conv_out': 7.895804e-02, 'w_out': 7.950127e-02, 'final_norm_g': 6.405586e+01}


def _to_microbatches(a, axis):
    t = _jnp.moveaxis(a, axis, 0)
    t = t.reshape((N_MICROBATCH, t.shape[0] // N_MICROBATCH) + t.shape[1:])
    return _jnp.moveaxis(t, 1, axis + 1)


def setup_inputs(seed: int = 0) -> dict:
    inp = _fwd_setup_inputs(seed)
    key = _jax.random.fold_in(_jax.random.key(seed), 7919)
    shape, _ = _output_shape()
    out = dict(inp)
    out["loss_target"] = _jax.random.normal(_jax.random.fold_in(key, 0), shape, _jnp.float32)
    for i, name in enumerate(TWIN_WEIGHTS):
        w = inp[name].astype(_jnp.float32)
        if MOMENT_SCALE is None:
            s = _jnp.sqrt(_jnp.mean(_jnp.square(w)) + 1e-30)
        else:
            s = MOMENT_SCALE[name]
        km, kv = _jax.random.split(_jax.random.fold_in(key, i + 1))
        out[name] = w
        out["m_" + name] = s * _jax.random.normal(km, w.shape, _jnp.float32)
        out["v_" + name] = (s * s) * _jax.random.uniform(kv, w.shape, _jnp.float32, 0.5, 1.5)
    if N_MICROBATCH > 1:
        for name, axis in PER_EXAMPLE_BATCH_AXIS.items():
            out[name] = _to_microbatches(out[name], axis)
    return {'x': out['x'], 'norm_g': out['norm_g'], 'w_in': out['w_in'], 'rel_bias': out['rel_bias'], 'w_att_out': out['w_att_out'], 'conv_w': out['conv_w'], 'conv_b': out['conv_b'], 'w_conv_out': out['w_conv_out'], 'w_out': out['w_out'], 'final_norm_g': out['final_norm_g'], 'loss_target': out['loss_target'], 'm_norm_g': out['m_norm_g'], 'm_w_in': out['m_w_in'], 'm_rel_bias': out['m_rel_bias'], 'm_w_att_out': out['m_w_att_out'], 'm_conv_w': out['m_conv_w'], 'm_conv_b': out['m_conv_b'], 'm_w_conv_out': out['m_w_conv_out'], 'm_w_out': out['m_w_out'], 'm_final_norm_g': out['m_final_norm_g'], 'v_norm_g': out['v_norm_g'], 'v_w_in': out['v_w_in'], 'v_rel_bias': out['v_rel_bias'], 'v_w_att_out': out['v_w_att_out'], 'v_conv_w': out['v_conv_w'], 'v_conv_b': out['v_conv_b'], 'v_w_conv_out': out['v_w_conv_out'], 'v_w_out': out['v_w_out'], 'v_final_norm_g': out['v_final_norm_g']}


def _loss(weights, diff, rest, loss_target):
    with _jax.named_scope("forward"):
        args = {**rest, TWIN_DIFF_INPUT: diff, **{k: w.astype(_WEIGHT_DTYPES[k]) for k, w in weights.items()}}
        y = _forward(args)
    with _jax.named_scope("loss_head"):
        err = _jnp.square(y.astype(_jnp.float32) - loss_target)
        return 0.5 * _jnp.sum(_jnp.mean(err, axis=-1)) if err.ndim else 0.5 * err


def _adamw(w, g, m, v):
    m = ADAM_B1 * m + (1.0 - ADAM_B1) * g
    v = ADAM_B2 * v + (1.0 - ADAM_B2) * _jnp.square(g)
    m_hat = m / (1.0 - ADAM_B1 ** ADAM_STEP)
    v_hat = v / (1.0 - ADAM_B2 ** ADAM_STEP)
    delta = -ADAM_LR * (m_hat / (_jnp.sqrt(v_hat) + ADAM_EPS) + ADAM_WD * w)
    return delta, m, v


def reference(x, norm_g, w_in, rel_bias, w_att_out, conv_w, conv_b, w_conv_out, w_out, final_norm_g, loss_target, m_norm_g, m_w_in, m_rel_bias, m_w_att_out, m_conv_w, m_conv_b, m_w_conv_out, m_w_out, m_final_norm_g, v_norm_g, v_w_in, v_rel_bias, v_w_att_out, v_conv_w, v_conv_b, v_w_conv_out, v_w_out, v_final_norm_g):
    given = dict(x=x, norm_g=norm_g, w_in=w_in, rel_bias=rel_bias, w_att_out=w_att_out, conv_w=conv_w, conv_b=conv_b, w_conv_out=w_conv_out, w_out=w_out, final_norm_g=final_norm_g, loss_target=loss_target, m_norm_g=m_norm_g, m_w_in=m_w_in, m_rel_bias=m_rel_bias, m_w_att_out=m_w_att_out, m_conv_w=m_conv_w, m_conv_b=m_conv_b, m_w_conv_out=m_w_conv_out, m_w_out=m_w_out, m_final_norm_g=m_final_norm_g, v_norm_g=v_norm_g, v_w_in=v_w_in, v_rel_bias=v_rel_bias, v_w_att_out=v_w_att_out, v_conv_w=v_conv_w, v_conv_b=v_conv_b, v_w_conv_out=v_w_conv_out, v_w_out=v_w_out, v_final_norm_g=v_final_norm_g)
    weights = {n: given[n] for n in TWIN_WEIGHTS}
    shared = {n: given[n] for n in SHARED_INPUTS}
    per_example = {n: given[n] for n in ['x']}
    grad_fn = _jax.value_and_grad(_loss, argnums=(0, 1))

    def one_microbatch(ex, loss_target):
        ex = dict(ex)
        diff = ex.pop(TWIN_DIFF_INPUT)
        return grad_fn(weights, diff, {**shared, **ex}, loss_target)

    if N_MICROBATCH == 1:
        loss, (grad_w, grad_x) = one_microbatch(per_example, given["loss_target"])
    else:
        def body(carry, xs):
            loss_sum, grad_sum = carry
            l_k, (gw_k, gx_k) = one_microbatch(xs[0], xs[1])
            with _jax.named_scope("update"):
                return (loss_sum + l_k, _jax.tree.map(_jnp.add, grad_sum, gw_k)), gx_k

        init = (_jnp.zeros((), _jnp.float32), _jax.tree.map(_jnp.zeros_like, weights))
        (loss, grad_w), grad_x = _jax.lax.scan(body, init, (per_example, given["loss_target"]))
    with _jax.named_scope("update"):
        delta_w, new_m, new_v = {}, {}, {}
        for n in TWIN_WEIGHTS:
            delta_w[n], new_m[n], new_v[n] = _adamw(weights[n], grad_w[n], given["m_" + n], given["v_" + n])
    return (loss, grad_x, *[grad_w[n] for n in TWIN_WEIGHTS], *[delta_w[n] for n in TWIN_WEIGHTS],
            *[new_m[n] for n in TWIN_WEIGHTS], *[new_v[n] for n in TWIN_WEIGHTS])
```

```python
import functools

import numpy as np
import jax
import jax.numpy as jnp
from jax import lax
from jax.experimental import pallas as pl
from jax.experimental.pallas import tpu as pltpu

F32 = jnp.float32
BF16 = jnp.bfloat16

D_MODEL = 1024
CHUNK = 64
N_LEFT = 8
HEADS = 8
D_ATT = 512
D_CONV = 512
MAX_REL = 128
N_REL = 2 * MAX_REL + 1
IN_COLS = 6144
EPS = 1e-6
NEG_BIG = -1e30
N_DEV = 8
W_BLK = IN_COLS // N_DEV
QB = 4 * CHUNK
KB = QB + N_LEFT * CHUNK
PADK = N_LEFT * CHUNK
SCALE = 64 ** -0.5
GATE_COLS = IN_COLS - 3 * D_ATT

ADAM_LR = 0.001
ADAM_B1 = 0.9
ADAM_B2 = 0.999
ADAM_EPS = 1e-08
ADAM_WD = 0.01
ADAM_STEP = 10

VMEM_LIMIT = 56 * 1024 * 1024

MESH = pl.DeviceIdType.MESH
ANY = pl.BlockSpec(memory_space=pl.ANY)


def _params(n_grid):
    return pltpu.CompilerParams(dimension_semantics=("arbitrary",) * n_grid,
                                vmem_limit_bytes=VMEM_LIMIT)


def _dot(a, b):
    return jnp.dot(a, b, preferred_element_type=F32)


def _dot_nt(a, b):
    return lax.dot_general(a, b, (((1,), (1,)), ((), ())), preferred_element_type=F32)


def _dot_tn(a, b):
    return lax.dot_general(a, b, (((0,), (0,)), ((), ())), preferred_element_type=F32)


def _sigmoid(z):
    return 1.0 / (1.0 + jnp.exp(-z))


def _mesh_pos():
    return lax.axis_index("x"), lax.axis_index("y"), lax.axis_index("c")


def _flat(p):
    return 4 * p[0] + 2 * p[1] + p[2]


def _all_gather(shards, name):
    n = len(shards)

    def body(*refs):
        ins, outs = refs[:n], refs[n:2 * n]
        send_sems, recv_sems, local_sems = refs[2 * n:]
        x, y, c = _mesh_pos()
        me, sibling = (x, y, c), (x, y, 1 - c)
        chips = [(1 - x, y), (x, 1 - y), (1 - x, 1 - y)]

        def copy(k, a, block, to, from_input=False):
            rows = outs[a].at[_flat(block)]
            return pltpu.make_async_remote_copy(
                src_ref=ins[a] if from_input else rows, dst_ref=rows,
                send_sem=send_sems.at[k, a], recv_sem=recv_sems.at[k, a],
                device_id=to, device_id_type=MESH)

        mine = [pltpu.make_async_copy(ins[a], outs[a].at[_flat(me)], local_sems.at[a]) for a in range(n)]
        for cp in mine:
            cp.start()
        first = [copy(0, a, me, sibling, True) for a in range(n)]
        for j, chip in enumerate(chips):
            first += [copy(1 + j, a, me, (*chip, c), True) for a in range(n)]
        for cp in first:
            cp.start()
        passed = []
        for j, chip in enumerate(chips):
            for a in range(n):
                copy(1 + j, a, (*chip, c), me).wait_recv()
            fwd = [copy(4 + j, a, (*chip, c), sibling) for a in range(n)]
            for cp in fwd:
                cp.start()
            passed += fwd
        for a in range(n):
            copy(0, a, sibling, me).wait_recv()
        for j, chip in enumerate(chips):
            for a in range(n):
                copy(4 + j, a, (*chip, 1 - c), me).wait_recv()
        for cp in first + passed:
            cp.wait_send()
        for cp in mine:
            cp.wait()

    return pl.pallas_call(
        body, name=name,
        out_shape=[jax.ShapeDtypeStruct((N_DEV,) + s.shape, s.dtype) for s in shards],
        in_specs=[ANY] * n, out_specs=[ANY] * n,
        scratch_shapes=[pltpu.SemaphoreType.DMA((7, n)), pltpu.SemaphoreType.DMA((7, n)),
                        pltpu.SemaphoreType.DMA((n,))],
    )(*shards)


def _exchange(blocks, name):
    n = len(blocks)

    def body(*refs):
        ins, outs = refs[:n], refs[n:2 * n]
        send_sems, recv_sems, local_sems = refs[2 * n:]
        x, y, c = _mesh_pos()
        me = (x, y, c)
        peers = [(x, y, 1 - c), (1 - x, y, c), (x, 1 - y, c), (1 - x, 1 - y, c),
                 (1 - x, y, 1 - c), (x, 1 - y, 1 - c), (1 - x, 1 - y, 1 - c)]

        def copy(k, a, src_block, dst_slot, to):
            return pltpu.make_async_remote_copy(
                src_ref=ins[a].at[_flat(src_block)], dst_ref=outs[a].at[_flat(dst_slot)],
                send_sem=send_sems.at[k, a], recv_sem=recv_sems.at[k, a],
                device_id=to, device_id_type=MESH)

        mine = [pltpu.make_async_copy(ins[a].at[_flat(me)], outs[a].at[_flat(me)], local_sems.at[a])
                for a in range(n)]
        for cp in mine:
            cp.start()
        sends = [copy(k, a, peer, me, peer) for k, peer in enumerate(peers) for a in range(n)]
        for cp in sends:
            cp.start()
        for k, peer in enumerate(peers):
            for a in range(n):
                copy(k, a, me, peer, me).wait_recv()
        for cp in sends:
            cp.wait_send()
        for cp in mine:
            cp.wait()

    return pl.pallas_call(
        body, name=name,
        out_shape=[jax.ShapeDtypeStruct(b.shape, b.dtype) for b in blocks],
        in_specs=[ANY] * n, out_specs=[ANY] * n,
        scratch_shapes=[pltpu.SemaphoreType.DMA((7, n)), pltpu.SemaphoreType.DMA((7, n)),
                        pltpu.SemaphoreType.DMA((n,))],
    )(*blocks)


def _in_proj(x, norm_g, w_in_g):
    S = x.shape[0]
    ts = 256

    def body(x_ref, g_ref, w_ref, p_ref, h_ref):
        xf = x_ref[...]
        r = lax.rsqrt(jnp.mean(xf * xf, axis=-1, keepdims=True) + EPS)
        h = ((xf * r) * g_ref[...]).astype(BF16)
        h_ref[...] = h
        for j in range(N_DEV):
            p_ref[:, j * W_BLK:(j + 1) * W_BLK] = _dot(h, w_ref[j]).astype(BF16)

    return pl.pallas_call(
        body, name="in_proj", grid=(S // ts,),
        out_shape=[jax.ShapeDtypeStruct((S, IN_COLS), BF16), jax.ShapeDtypeStruct((S, D_MODEL), BF16)],
        in_specs=[pl.BlockSpec((ts, D_MODEL), lambda i: (i, 0)),
                  pl.BlockSpec((1, D_MODEL), lambda i: (0, 0)),
                  pl.BlockSpec((N_DEV, D_MODEL, W_BLK), lambda i: (0, 0, 0))],
        out_specs=[pl.BlockSpec((ts, IN_COLS), lambda i: (i, 0)),
                   pl.BlockSpec((ts, D_MODEL), lambda i: (i, 0))],
        compiler_params=_params(1),
    )(x, norm_g, w_in_g)


def _bias_table(rel_bias):
    wide = 1024

    def body(r_ref, o_ref):
        col = lax.broadcasted_iota(jnp.int32, (1, wide), 1)
        k_minus_q = jnp.where(col < KB, col, col - wide)
        idx = jnp.clip(PADK - k_minus_q, -MAX_REL, MAX_REL) + MAX_REL
        f = jnp.zeros((HEADS, wide), F32)
        for r in range(MAX_REL - CHUNK + 1, N_REL):
            f = jnp.where(idx == r, r_ref[:, r:r + 1], f)
        rowi = lax.broadcasted_iota(jnp.int32, (QB, 1), 0)
        kc = lax.broadcasted_iota(jnp.int32, (1, KB), 1) >> 6
        qc = rowi >> 6
        band = (kc >= qc) & (kc <= qc + N_LEFT)
        for h in range(HEADS):
            a = jnp.broadcast_to(f[h:h + 1, :], (QB, wide))
            for b in range(8):
                a = jnp.where((rowi >> b) & 1 == 1, pltpu.roll(a, 1 << b, 1), a)
            o_ref[h] = jnp.where(band, a[:, 0:KB], NEG_BIG)

    return pl.pallas_call(
        body, name="bias_table",
        out_shape=jax.ShapeDtypeStruct((HEADS, QB, KB), F32),
        compiler_params=pltpu.CompilerParams(vmem_limit_bytes=VMEM_LIMIT),
    )(rel_bias)


def _load_keys(p_hbm, kp, vp, sem):
    kp[0:PADK, :] = jnp.zeros((PADK, D_ATT), BF16)
    vp[0:PADK, :] = jnp.zeros((PADK, D_ATT), BF16)
    S = p_hbm.shape[0]
    ck = pltpu.make_async_copy(p_hbm.at[:, D_ATT:2 * D_ATT], kp.at[PADK:PADK + S, :], sem.at[0])
    cv = pltpu.make_async_copy(p_hbm.at[:, 2 * D_ATT:3 * D_ATT], vp.at[PADK:PADK + S, :], sem.at[1])
    ck.start()
    cv.start()
    ck.wait()
    cv.wait()


def _start_mask(g):
    kidx = lax.broadcasted_iota(jnp.int32, (1, KB), 1)
    return jnp.where(kidx + g * QB >= PADK, 0.0, NEG_BIG).astype(F32)


def _attn_fwd(P, bias_tab):
    S = P.shape[0]
    nb = S // QB

    def body(q_ref, p_hbm, bias_ref, o_ref, lse_ref, kp, vp, sem):
        g = pl.program_id(0)

        @pl.when(g == 0)
        def _():
            _load_keys(p_hbm, kp, vp, sem)

        start = pl.multiple_of(g * QB, QB)
        lane = lax.broadcasted_iota(jnp.int32, (1, 128), 1)
        smask = _start_mask(g)
        for p in range(HEADS // 2):
            cols = slice(128 * p, 128 * (p + 1))
            qp = q_ref[:, cols]
            kpair = kp[pl.ds(start, KB), cols]
            vpair = vp[pl.ds(start, KB), cols]
            outs = []
            for e in range(2):
                h = 2 * p + e
                lm = (lane < 64) if e == 0 else (lane >= 64)
                qm = jnp.where(lm, qp, jnp.zeros_like(qp))
                s = _dot_nt(qm, kpair) * SCALE + bias_ref[h] + smask
                mx = jnp.max(s, axis=-1, keepdims=True)
                ex = jnp.exp(s - mx)
                sm = jnp.sum(ex, axis=-1, keepdims=True)
                pr = (ex * (1.0 / sm)).astype(BF16)
                outs.append(_dot(pr, vpair))
                lse_ref[:, h:h + 1] = mx + jnp.log(sm)
            o_ref[:, cols] = jnp.where(lane < 64, outs[0], outs[1]).astype(BF16)

    return pl.pallas_call(
        body, name="attn_fwd", grid=(nb,),
        out_shape=[jax.ShapeDtypeStruct((S, D_ATT), BF16), jax.ShapeDtypeStruct((S, HEADS), F32)],
        in_specs=[pl.BlockSpec((QB, D_ATT), lambda g: (g, 0)), ANY,
                  pl.BlockSpec((HEADS, QB, KB), lambda g: (0, 0, 0))],
        out_specs=[pl.BlockSpec((QB, D_ATT), lambda g: (g, 0)),
                   pl.BlockSpec((QB, HEADS), lambda g: (g, 0))],
        scratch_shapes=[pltpu.VMEM((S + PADK, D_ATT), BF16), pltpu.VMEM((S + PADK, D_ATT), BF16),
                        pltpu.SemaphoreType.DMA((2,))],
        compiler_params=_params(1),
    )(P, P, bias_tab)


def _token_local(x, tgt, P, att, w_att_out, w_conv_out, w_out, conv_w, conv_b, final_g):
    S = x.shape[0]
    ts = 256
    nt = S // ts
    hb = 16

    def body(x_ref, t_ref, s1_ref, s2_ref, s3_ref, h1_ref, h2_ref, att_ref,
             wao_ref, wco_ref, wo_ref, cw_ref, cb_ref, g2_ref,
             dx2_ref, dg_ref, datt_ref, dwo_ref, dwao_ref, dwco_ref, sm1_ref, sm2_ref, carry):
        i = pl.program_id(0)
        t = nt - 1 - i

        @pl.when(i == 0)
        def _():
            dwo_ref[...] = jnp.zeros_like(dwo_ref)
            dwao_ref[...] = jnp.zeros_like(dwao_ref)
            dwco_ref[...] = jnp.zeros_like(dwco_ref)
            sm1_ref[...] = jnp.zeros_like(sm1_ref)
            sm2_ref[...] = jnp.zeros_like(sm2_ref)
            carry[...] = jnp.zeros_like(carry)

        za = s1_ref[:, 0:512].astype(F32)
        gb = s1_ref[:, 512:1024].astype(F32)
        gc = s1_ref[:, 1024:1536].astype(F32)
        u = s2_ref[:, 0:512].astype(F32)
        zc = s2_ref[:, 512:1024].astype(F32)
        ga = jnp.concatenate([s2_ref[:, 1024:1536], s3_ref[:, 0:512]], axis=1).astype(F32)
        gv = s3_ref[:, 512:1536].astype(F32)
        att = att_ref[...].astype(F32)
        row = lax.broadcasted_iota(jnp.int32, (ts, 1), 0)

        sa = _sigmoid(za)
        silu_a = za * sa
        att_g = (att * silu_a).astype(BF16)
        y_att = _dot(att_g, wao_ref[...])

        cu = gc * u
        keep = jnp.where(t > 0, 1.0, 0.0).astype(F32)
        hcu = (h1_ref[:, 1024:1536].astype(F32) * h2_ref[:, 0:512].astype(F32)) * keep
        cu_m1 = jnp.where(row == 0, hcu[hb - 1:hb, :], pltpu.roll(cu, 1, 0))
        cu_m2 = jnp.where(row == 0, hcu[hb - 2:hb - 1, :],
                          jnp.where(row == 1, hcu[hb - 1:hb, :], pltpu.roll(cu, 2, 0)))
        w0, w1, w2 = cw_ref[0:1, :], cw_ref[1:2, :], cw_ref[2:3, :]
        vconv = w0 * cu_m2 + w1 * cu_m1 + w2 * cu + cb_ref[...]
        sc = _sigmoid(zc)
        silu_c = zc * sc
        cg = (gb * vconv * silu_c).astype(BF16)
        y_conv = _dot(cg, wco_ref[...])

        sga = _sigmoid(ga)
        sgv = _sigmoid(gv)
        m = (sga * y_att + sgv * y_conv).astype(BF16)
        x2 = x_ref[...] + _dot(m, wo_ref[...])
        r2 = lax.rsqrt(jnp.mean(x2 * x2, axis=-1, keepdims=True) + EPS)
        xn2 = x2 * r2
        g2 = g2_ref[...]
        err = xn2 * g2 - t_ref[...]
        sm1_ref[1:2, :] += jnp.sum(err * err, axis=0, keepdims=True) * (0.5 / D_MODEL)

        dy = err * (1.0 / D_MODEL)
        sm1_ref[0:1, :] += jnp.sum(dy * xn2, axis=0, keepdims=True)
        dxn = dy * g2
        dx2 = r2 * (dxn - xn2 * jnp.mean(dxn * xn2, axis=-1, keepdims=True))
        dx2_ref[...] = dx2
        dx2b = dx2.astype(BF16)
        dwo_ref[...] += _dot_tn(m, dx2b)
        dm = _dot_nt(dx2b, wo_ref[...])
        dya = (dm * sga).astype(BF16)
        dyc = (dm * sgv).astype(BF16)
        dg_ref[:, 2560:3584] = (dm * y_att * (sga * (1.0 - sga))).astype(BF16)
        dg_ref[:, 3584:4608] = (dm * y_conv * (sgv * (1.0 - sgv))).astype(BF16)
        dwao_ref[...] += _dot_tn(att_g, dya)
        dwco_ref[...] += _dot_tn(cg, dyc)
        datt_g = _dot_nt(dya, wao_ref[...])
        dcg = _dot_nt(dyc, wco_ref[...])
        datt_ref[...] = (datt_g * silu_a).astype(BF16)
        dg_ref[:, 0:512] = (datt_g * att * (sa * (1.0 + za * (1.0 - sa)))).astype(BF16)
        dg_ref[:, 512:1024] = (dcg * vconv * silu_c).astype(BF16)
        dg_ref[:, 2048:2560] = (dcg * gb * vconv * (sc * (1.0 + zc * (1.0 - sc)))).astype(BF16)
        dv = dcg * gb * silu_c
        sm2_ref[3:4, :] += jnp.sum(dv, axis=0, keepdims=True)
        sm2_ref[0:1, :] += jnp.sum(dv * cu_m2, axis=0, keepdims=True)
        sm2_ref[1:2, :] += jnp.sum(dv * cu_m1, axis=0, keepdims=True)
        sm2_ref[2:3, :] += jnp.sum(dv * cu, axis=0, keepdims=True)
        nxt = carry[...]
        dv_p1 = jnp.where(row == ts - 1, nxt[0:1, :], pltpu.roll(dv, ts - 1, 0))
        dv_p2 = jnp.where(row == ts - 1, nxt[1:2, :],
                          jnp.where(row == ts - 2, nxt[0:1, :], pltpu.roll(dv, ts - 2, 0)))
        dcu = w2 * dv + w1 * dv_p1 + w0 * dv_p2
        carry[...] = dv[0:8, :]
        dg_ref[:, 1024:1536] = (dcu * u).astype(BF16)
        dg_ref[:, 1536:2048] = (dcu * gc).astype(BF16)

    tile = lambda w: pl.BlockSpec((ts, w), lambda i: (nt - 1 - i, 0))
    seg = lambda c: pl.BlockSpec((ts, 1536), lambda i: (nt - 1 - i, c))
    halo = lambda c: pl.BlockSpec((hb, 1536), lambda i: (jnp.maximum((nt - 1 - i) * (ts // hb) - 1, 0), c))
    full = lambda a: pl.BlockSpec(a.shape, lambda i: (0,) * a.ndim)
    acc = lambda r, c: pl.BlockSpec((r, c), lambda i: (0, 0))
    return pl.pallas_call(
        body, name="token_local", grid=(nt,),
        out_shape=[jax.ShapeDtypeStruct((S, D_MODEL), F32), jax.ShapeDtypeStruct((S, GATE_COLS), BF16),
                   jax.ShapeDtypeStruct((S, D_ATT), BF16), jax.ShapeDtypeStruct((D_MODEL, D_MODEL), F32),
                   jax.ShapeDtypeStruct((D_ATT, D_MODEL), F32), jax.ShapeDtypeStruct((D_CONV, D_MODEL), F32),
                   jax.ShapeDtypeStruct((8, D_MODEL), F32), jax.ShapeDtypeStruct((8, D_CONV), F32)],
        in_specs=[tile(D_MODEL), tile(D_MODEL), seg(1), seg(2), seg(3), halo(1), halo(2), tile(D_ATT),
                  full(w_att_out), full(w_conv_out), full(w_out), full(conv_w), full(conv_b), full(final_g)],
        out_specs=[tile(D_MODEL), tile(GATE_COLS), tile(D_ATT), acc(D_MODEL, D_MODEL), acc(D_ATT, D_MODEL),
                   acc(D_CONV, D_MODEL), acc(8, D_MODEL), acc(8, D_CONV)],
        scratch_shapes=[pltpu.VMEM((8, D_CONV), F32)],
        compiler_params=_params(1),
    )(x, tgt, P, P, P, P, P, att, w_att_out, w_conv_out, w_out, conv_w, conv_b, final_g)


def _attn_bwd(P, att, datt, lse, bias_tab):
    S = P.shape[0]
    nb = S // QB

    def body(q_ref, att_ref, datt_ref, lse_ref, p_hbm, bias_ref, out_ref, db_ref,
             kp, vp, dq_ring, dk_ring, dv_ring, sem):
        g = pl.program_id(0)

        @pl.when(g == 0)
        def _():
            _load_keys(p_hbm, kp, vp, sem)
            db_ref[...] = jnp.zeros_like(db_ref)
            dk_ring[...] = jnp.zeros_like(dk_ring)
            dv_ring[...] = jnp.zeros_like(dv_ring)

        s_new = g % 3
        s_mid = (g + 2) % 3
        s_old = (g + 1) % 3

        @pl.when(g < nb)
        def _():
            start = pl.multiple_of(g * QB, QB)
            lane = lax.broadcasted_iota(jnp.int32, (1, 128), 1)
            smask = _start_mask(g)
            for p in range(HEADS // 2):
                cols = slice(128 * p, 128 * (p + 1))
                qp = q_ref[:, cols]
                op = att_ref[:, cols].astype(F32)
                dop = datt_ref[:, cols]
                kpair = kp[pl.ds(start, KB), cols]
                vpair = vp[pl.ds(start, KB), cols]
                dqs = []
                dk_acc = jnp.zeros((KB, 128), F32)
                dv_acc = jnp.zeros((KB, 128), F32)
                for e in range(2):
                    h = 2 * p + e
                    lm = (lane < 64) if e == 0 else (lane >= 64)
                    qm = jnp.where(lm, qp, jnp.zeros_like(qp))
                    dom = jnp.where(lm, dop, jnp.zeros_like(dop))
                    s = _dot_nt(qm, kpair) * SCALE + bias_ref[h] + smask
                    pr = jnp.exp(s - lse_ref[:, h:h + 1])
                    dp = _dot_nt(dom, vpair)
                    delta = jnp.sum(dom.astype(F32) * op, axis=-1, keepdims=True)
                    ds = pr * (dp - delta)
                    db_ref[h] += ds
                    dsb = (ds * SCALE).astype(BF16)
                    prb = pr.astype(BF16)
                    dqs.append(_dot(dsb, kpair))
                    dk_acc = dk_acc + _dot_tn(dsb, qm)
                    dv_acc = dv_acc + _dot_tn(prb, dom)
                dq_ring[s_new, :, cols] = jnp.where(lane < 64, dqs[0], dqs[1])
                dk_ring[s_old, :, cols] += dk_acc[0:QB]
                dk_ring[s_mid, :, cols] += dk_acc[QB:2 * QB]
                dk_ring[s_new, :, cols] = dk_acc[2 * QB:3 * QB]
                dv_ring[s_old, :, cols] += dv_acc[0:QB]
                dv_ring[s_mid, :, cols] += dv_acc[QB:2 * QB]
                dv_ring[s_new, :, cols] = dv_acc[2 * QB:3 * QB]

        @pl.when(g >= 2)
        def _():
            out_ref[:, 0:D_ATT] = dq_ring[s_old].astype(BF16)
            out_ref[:, D_ATT:2 * D_ATT] = dk_ring[s_old].astype(BF16)
            out_ref[:, 2 * D_ATT:3 * D_ATT] = dv_ring[s_old].astype(BF16)

    qblk = lambda w: pl.BlockSpec((QB, w), lambda g: (jnp.minimum(g, nb - 1), 0))
    return pl.pallas_call(
        body, name="attn_bwd", grid=(nb + 2,),
        out_shape=[jax.ShapeDtypeStruct((S, 3 * D_ATT), BF16), jax.ShapeDtypeStruct((HEADS, QB, KB), F32)],
        in_specs=[qblk(D_ATT), qblk(D_ATT), qblk(D_ATT), qblk(HEADS), ANY,
                  pl.BlockSpec((HEADS, QB, KB), lambda g: (0, 0, 0))],
        out_specs=[pl.BlockSpec((QB, 3 * D_ATT), lambda g: (jnp.maximum(g - 2, 0), 0)),
                   pl.BlockSpec((HEADS, QB, KB), lambda g: (0, 0, 0))],
        scratch_shapes=[pltpu.VMEM((S + PADK, D_ATT), BF16), pltpu.VMEM((S + PADK, D_ATT), BF16),
                        pltpu.VMEM((3, QB, D_ATT), F32), pltpu.VMEM((3, QB, D_ATT), F32),
                        pltpu.VMEM((3, QB, D_ATT), F32), pltpu.SemaphoreType.DMA((2,))],
        compiler_params=_params(1),
    )(P, att, datt, lse, P, bias_tab)


def _bias_fold(dscore):
    wide = 1024

    def body(d_ref, o_ref):
        rowi = lax.broadcasted_iota(jnp.int32, (QB, 1), 0)
        col = lax.broadcasted_iota(jnp.int32, (1, wide), 1)
        for h in range(HEADS):
            a = jnp.concatenate([d_ref[h], jnp.zeros((QB, wide - KB), F32)], axis=1)
            for b in range(8):
                a = jnp.where((rowi >> b) & 1 == 1, pltpu.roll(a, wide - (1 << b), 1), a)
            vec = jnp.sum(a, axis=0, keepdims=True)
            far = (col <= PADK - MAX_REL) | (col > KB)
            tail = jnp.sum(jnp.where(far, vec, 0.0), axis=-1, keepdims=True)
            o_ref[h:h + 1, :] = jnp.where(col == wide - 1, tail, vec)

    return pl.pallas_call(
        body, name="bias_fold",
        out_shape=jax.ShapeDtypeStruct((HEADS, wide), F32),
        compiler_params=pltpu.CompilerParams(vmem_limit_bytes=VMEM_LIMIT),
    )(dscore)


def _in_proj_bwd(x, norm_g, dx2, dqkv, dgate, w_in_g):
    S = x.shape[0]
    ts = 256

    def body(x_ref, g_ref, dx2_ref, dq_ref, dg_ref, w_ref, gx_ref, dn_ref):
        @pl.when(pl.program_id(0) == 0)
        def _():
            dn_ref[...] = jnp.zeros_like(dn_ref)

        dh = jnp.zeros((ts, D_MODEL), F32)
        for j in range(N_DEV):
            if j < 2:
                d = dq_ref[:, j * W_BLK:(j + 1) * W_BLK]
            else:
                d = dg_ref[:, (j - 2) * W_BLK:(j - 1) * W_BLK]
            dh = dh + _dot_nt(d, w_ref[j])
        xf = x_ref[...]
        r = lax.rsqrt(jnp.mean(xf * xf, axis=-1, keepdims=True) + EPS)
        xn = xf * r
        dn_ref[0:1, :] += jnp.sum(dh * xn, axis=0, keepdims=True)
        dhg = dh * g_ref[...]
        gx_ref[...] = dx2_ref[...] + r * (dhg - xn * jnp.mean(dhg * xn, axis=-1, keepdims=True))

    tile = lambda w: pl.BlockSpec((ts, w), lambda i: (i, 0))
    return pl.pallas_call(
        body, name="in_proj_bwd", grid=(S // ts,),
        out_shape=[jax.ShapeDtypeStruct((S, D_MODEL), F32), jax.ShapeDtypeStruct((8, D_MODEL), F32)],
        in_specs=[tile(D_MODEL), pl.BlockSpec((1, D_MODEL), lambda i: (0, 0)), tile(D_MODEL),
                  tile(3 * D_ATT), tile(GATE_COLS),
                  pl.BlockSpec((N_DEV, D_MODEL, W_BLK), lambda i: (0, 0, 0))],
        out_specs=[tile(D_MODEL), pl.BlockSpec((8, D_MODEL), lambda i: (0, 0))],
        compiler_params=_params(1),
    )(x, norm_g, dx2, dqkv, dgate, w_in_g)


def _w_in_grad(h, dqkv, dgate):
    S = h.shape[0]
    ts = 512
    nt = S // ts

    def body(h_ref, dq_ref, dg_ref, o_ref):
        j, i = pl.program_id(0), pl.program_id(1)

        @pl.when(i == 0)
        def _():
            o_ref[...] = jnp.zeros_like(o_ref)

        @pl.when(j < 2)
        def _():
            o_ref[0] += _dot_tn(h_ref[...], dq_ref[...])

        @pl.when(j >= 2)
        def _():
            o_ref[0] += _dot_tn(h_ref[...], dg_ref[...])

    return pl.pallas_call(
        body, name="w_in_grad", grid=(N_DEV, nt),
        out_shape=jax.ShapeDtypeStruct((N_DEV, D_MODEL, W_BLK), F32),
        in_specs=[pl.BlockSpec((ts, D_MODEL), lambda j, i: (i, 0)),
                  pl.BlockSpec((ts, W_BLK), lambda j, i: (i, jnp.minimum(j, 1))),
                  pl.BlockSpec((ts, W_BLK), lambda j, i: (i, jnp.maximum(j - 2, 0)))],
        out_specs=pl.BlockSpec((1, D_MODEL, W_BLK), lambda j, i: (j, 0, 0)),
        compiler_params=_params(2),
    )(h, dqkv, dgate)


def _adamw(w, g, m, v):
    m = ADAM_B1 * m + (1.0 - ADAM_B1) * g
    v = ADAM_B2 * v + (1.0 - ADAM_B2) * (g * g)
    m_hat = m / (1.0 - ADAM_B1 ** ADAM_STEP)
    v_hat = v / (1.0 - ADAM_B2 ** ADAM_STEP)
    delta = -ADAM_LR * (m_hat / (jnp.sqrt(v_hat) + ADAM_EPS) + ADAM_WD * w)
    return delta, m, v


def _sum_adamw(parts, w, m, v, name):
    R, C = w.shape
    tr = min(R, 256)

    def body(p_ref, w_ref, m_ref, v_ref, g_ref, d_ref, nm_ref, nv_ref):
        g = p_ref[0].astype(F32)
        for s in range(1, N_DEV):
            g = g + p_ref[s].astype(F32)
        g_ref[...] = g
        d_ref[...], nm_ref[...], nv_ref[...] = _adamw(w_ref[...], g, m_ref[...], v_ref[...])

    tile = pl.BlockSpec((tr, C), lambda i: (i, 0))
    return pl.pallas_call(
        body, name=name, grid=(R // tr,),
        out_shape=[jax.ShapeDtypeStruct((R, C), F32)] * 4,
        in_specs=[pl.BlockSpec((N_DEV, tr, C), lambda i: (0, i, 0)), tile, tile, tile],
        out_specs=[tile] * 4,
        compiler_params=_params(1),
    )(parts, w, m, v)


def _sum_parts(parts):
    def body(p_ref, o_ref):
        g = p_ref[0]
        for s in range(1, N_DEV):
            g = g + p_ref[s]
        o_ref[...] = g

    return pl.pallas_call(body, name="sum_small",
                          out_shape=jax.ShapeDtypeStruct(parts.shape[1:], F32))(parts)


def _adamw_small(w, g, m, v):
    def body(w_ref, g_ref, m_ref, v_ref, d_ref, nm_ref, nv_ref):
        d_ref[...], nm_ref[...], nv_ref[...] = _adamw(w_ref[...], g_ref[...], m_ref[...], v_ref[...])

    return pl.pallas_call(body, name="adamw_small",
                          out_shape=[jax.ShapeDtypeStruct(w.shape, F32)] * 3)(w, g, m, v)


def _pad_row(a, width=D_MODEL):
    a = a.reshape(-1, a.shape[-1])
    return jnp.pad(a, ((0, 0), (0, width - a.shape[-1])))


def _owner_cols(a, blk):
    r = a.shape[0]
    return a.reshape(r, N_DEV, blk).transpose(1, 0, 2)


def kernel(x, norm_g, w_in, rel_bias, w_att_out, conv_w, conv_b, w_conv_out, w_out, final_norm_g, loss_target, m_norm_g, m_w_in, m_rel_bias, m_w_att_out, m_conv_w, m_conv_b, m_w_conv_out, m_w_out, m_final_norm_g, v_norm_g, v_w_in, v_rel_bias, v_w_att_out, v_conv_w, v_conv_b, v_w_conv_out, v_w_out, v_final_norm_g):
    S = x.shape[1]
    x2d = x.reshape(S, D_MODEL)
    tgt = loss_target.reshape(S, D_MODEL)
    me = 4 * lax.axis_index("x") + 2 * lax.axis_index("y") + lax.axis_index("c")

    proj_sh = jnp.concatenate([w_att_out[0], w_conv_out[0]], axis=1).astype(BF16)
    cw_sh = jnp.pad(conv_w[0], ((0, 5), (0, 64)))
    w_in_g, proj_g, w_out_g, cw_g = _all_gather(
        [w_in[0].astype(BF16), proj_sh, w_out[0].astype(BF16), cw_sh], "gather_weights")
    wao = proj_g[:, :, 0:128].transpose(1, 0, 2).reshape(D_ATT, D_MODEL)
    wco = proj_g[:, :, 128:256].transpose(1, 0, 2).reshape(D_CONV, D_MODEL)
    wo = w_out_g.reshape(D_MODEL, D_MODEL)
    cw = cw_g[:, 0:3, 0:64].transpose(1, 0, 2).reshape(3, D_CONV)

    P, h = _in_proj(x2d, norm_g, w_in_g)
    bias_tab = _bias_table(rel_bias[0])
    att, lse = _attn_fwd(P, bias_tab)
    dx2, dgate, datt, d_wo, d_wao, d_wco, sm1, sm2 = _token_local(
        x2d, tgt, P, att, wao, wco, wo, cw, conv_b, final_norm_g.reshape(1, D_MODEL))
    dqkv, dscore = _attn_bwd(P, att, datt, lse, bias_tab)
    dbias = _bias_fold(dscore)
    grad_x, dnorm = _in_proj_bwd(x2d, norm_g, dx2, dqkv, dgate, w_in_g)
    d_win = _w_in_grad(h, dqkv, dgate)

    d_proj = jnp.concatenate([_owner_cols(d_wao, 128), _owner_cols(d_wco, 128)], axis=2).astype(BF16)
    r_win, r_proj, r_wo = _exchange(
        [d_win.astype(BF16), d_proj, d_wo.reshape(N_DEV, 128, D_MODEL).astype(BF16)], "scatter_grads")
    small = jnp.concatenate([dnorm[0:1], sm1[0:2], _pad_row(sm2[0:4]), jnp.zeros((1, D_MODEL), F32), dbias],
                            axis=0)
    (small_g,) = _all_gather([small], "gather_small")
    tot = _sum_parts(small_g)

    g_win, d_win_, nm_win, nv_win = _sum_adamw(r_win, w_in[0], m_w_in[0], v_w_in[0], "adamw_w_in")
    w_proj = jnp.concatenate([w_att_out[0], w_conv_out[0]], axis=1)
    m_proj = jnp.concatenate([m_w_att_out[0], m_w_conv_out[0]], axis=1)
    v_proj = jnp.concatenate([v_w_att_out[0], v_w_conv_out[0]], axis=1)
    g_proj, d_proj_, nm_proj, nv_proj = _sum_adamw(r_proj, w_proj, m_proj, v_proj, "adamw_proj")
    g_wo, d_wo_, nm_wo, nv_wo = _sum_adamw(r_wo, w_out[0], m_w_out[0], v_w_out[0], "adamw_w_out")

    loss = jnp.sum(tot[2])
    g_norm, g_final = tot[0:1], tot[1]
    g_cw = lax.dynamic_slice(tot[3:6, 0:D_CONV], (0, me * 64), (3, 64))
    g_cb = tot[6:7, 0:D_CONV]
    vec = tot[8:16]
    g_rel = jnp.concatenate([jnp.zeros((HEADS, MAX_REL - CHUNK + 1), F32),
                             vec[:, PADK - MAX_REL + 1:PADK + CHUNK][:, ::-1], vec[:, 1023:1024]], axis=1)

    def pack(norm, final, cwv, cbv, rel):
        return jnp.concatenate([norm, final.reshape(1, D_MODEL), _pad_row(cwv), _pad_row(cbv),
                                jnp.zeros((2, D_MODEL), F32), _pad_row(rel)], axis=0)

    w_s = pack(norm_g, final_norm_g, conv_w[0], conv_b, rel_bias[0])
    g_s = pack(g_norm, g_final, g_cw, g_cb, g_rel)
    m_s = pack(m_norm_g, m_final_norm_g, m_conv_w[0], m_conv_b, m_rel_bias[0])
    v_s = pack(v_norm_g, v_final_norm_g, v_conv_w[0], v_conv_b, v_rel_bias[0])
    d_s, nm_s, nv_s = _adamw_small(w_s, g_s, m_s, v_s)

    def unpack(p):
        return (p[0:1], p[1], p[2:5, 0:64][None], p[5:6, 0:D_CONV], p[8:16, 0:N_REL][None])

    def sharded(a_in, a_proj, a_wo):
        return a_in[None], a_proj[:, 0:128][None], a_proj[:, 128:256][None], a_wo[None]

    outs = []
    for small_pack, big in ((g_s, (g_win, g_proj, g_wo)), (d_s, (d_win_, d_proj_, d_wo_)),
                            (nm_s, (nm_win, nm_proj, nm_wo)), (nv_s, (nv_win, nv_proj, nv_wo))):
        norm, final, cwv, cbv, rel = unpack(small_pack)
        b_in, b_ao, b_co, b_wo = sharded(*big)
        outs += [norm, b_in, rel, b_ao, cwv, cbv, b_co, b_wo, final]
    return (loss, grad_x.reshape(1, S, D_MODEL), *outs)
```

```python
import functools

import numpy as np
import jax
import jax.numpy as jnp
from jax import lax
from jax.experimental import pallas as pl
from jax.experimental.pallas import tpu as pltpu

F32 = jnp.float32
BF16 = jnp.bfloat16

D_MODEL = 1024
CHUNK = 64
N_LEFT = 8
HEADS = 8
D_ATT = 512
D_CONV = 512
MAX_REL = 128
N_REL = 2 * MAX_REL + 1
IN_COLS = 6144
EPS = 1e-6
NEG_BIG = -1e30
N_DEV = 8
W_BLK = IN_COLS // N_DEV
QB = 4 * CHUNK
KB = QB + N_LEFT * CHUNK
PADK = N_LEFT * CHUNK
SCALE = 64 ** -0.5
LOG2E = 1.4426950408889634
GATE_COLS = IN_COLS - 3 * D_ATT

ADAM_LR = 0.001
ADAM_B1 = 0.9
ADAM_B2 = 0.999
ADAM_EPS = 1e-08
ADAM_WD = 0.01
ADAM_STEP = 10

VMEM_LIMIT = 56 * 1024 * 1024

MESH = pl.DeviceIdType.MESH
ANY = pl.BlockSpec(memory_space=pl.ANY)


def _params(n_grid):
    return pltpu.CompilerParams(dimension_semantics=("arbitrary",) * n_grid,
                                vmem_limit_bytes=VMEM_LIMIT)


def _dot(a, b):
    return jnp.dot(a, b, preferred_element_type=F32)


def _dot_nt(a, b):
    return lax.dot_general(a, b, (((1,), (1,)), ((), ())), preferred_element_type=F32)


def _dot_tn(a, b):
    return lax.dot_general(a, b, (((0,), (0,)), ((), ())), preferred_element_type=F32)


def _sigmoid(z):
    return 1.0 / (1.0 + jnp.exp(-z))


def _mesh_pos():
    return lax.axis_index("x"), lax.axis_index("y"), lax.axis_index("c")


def _flat(p):
    return 4 * p[0] + 2 * p[1] + p[2]


def _all_gather(shards, name):
    n = len(shards)

    def body(*refs):
        ins, outs = refs[:n], refs[n:2 * n]
        send_sems, recv_sems, local_sems = refs[2 * n:]
        x, y, c = _mesh_pos()
        me, sibling = (x, y, c), (x, y, 1 - c)
        chips = [(1 - x, y), (x, 1 - y), (1 - x, 1 - y)]

        def copy(k, a, block, to, from_input=False):
            rows = outs[a].at[_flat(block)]
            return pltpu.make_async_remote_copy(
                src_ref=ins[a] if from_input else rows, dst_ref=rows,
                send_sem=send_sems.at[k, a], recv_sem=recv_sems.at[k, a],
                device_id=to, device_id_type=MESH)

        mine = [pltpu.make_async_copy(ins[a], outs[a].at[_flat(me)], local_sems.at[a]) for a in range(n)]
        for cp in mine:
            cp.start()
        first = [copy(0, a, me, sibling, True) for a in range(n)]
        for j, chip in enumerate(chips):
            first += [copy(1 + j, a, me, (*chip, c), True) for a in range(n)]
        for cp in first:
            cp.start()
        passed = []
        for j, chip in enumerate(chips):
            for a in range(n):
                copy(1 + j, a, (*chip, c), me).wait_recv()
            fwd = [copy(4 + j, a, (*chip, c), sibling) for a in range(n)]
            for cp in fwd:
                cp.start()
            passed += fwd
        for a in range(n):
            copy(0, a, sibling, me).wait_recv()
        for j, chip in enumerate(chips):
            for a in range(n):
                copy(4 + j, a, (*chip, 1 - c), me).wait_recv()
        for cp in first + passed:
            cp.wait_send()
        for cp in mine:
            cp.wait()

    return pl.pallas_call(
        body, name=name,
        out_shape=[jax.ShapeDtypeStruct((N_DEV,) + s.shape, s.dtype) for s in shards],
        in_specs=[ANY] * n, out_specs=[ANY] * n,
        scratch_shapes=[pltpu.SemaphoreType.DMA((7, n)), pltpu.SemaphoreType.DMA((7, n)),
                        pltpu.SemaphoreType.DMA((n,))],
    )(*shards)


def _in_proj(x, norm_g, w_in_g):
    S = x.shape[0]
    ts = 256

    def body(x_ref, g_ref, w_ref, p_ref, ht_ref):
        xf = x_ref[...]
        r = lax.rsqrt(jnp.mean(xf * xf, axis=-1, keepdims=True) + EPS)
        hf = (xf * r) * g_ref[...]
        h = hf.astype(BF16)
        ht_ref[...] = hf.T.astype(BF16)
        for j in range(N_DEV):
            p_ref[:, j * W_BLK:(j + 1) * W_BLK] = _dot(h, w_ref[j]).astype(BF16)

    return pl.pallas_call(
        body, name="in_proj", grid=(S // ts,),
        out_shape=[jax.ShapeDtypeStruct((S, IN_COLS), BF16), jax.ShapeDtypeStruct((D_MODEL, S), BF16)],
        in_specs=[pl.BlockSpec((ts, D_MODEL), lambda i: (i, 0)),
                  pl.BlockSpec((1, D_MODEL), lambda i: (0, 0)),
                  pl.BlockSpec((N_DEV, D_MODEL, W_BLK), lambda i: (0, 0, 0))],
        out_specs=[pl.BlockSpec((ts, IN_COLS), lambda i: (i, 0)),
                   pl.BlockSpec((D_MODEL, ts), lambda i: (0, i))],
        compiler_params=_params(1),
    )(x, norm_g, w_in_g)


def _bias_table(rel_bias):
    wide = 1024

    def body(r_ref, o_ref):
        h = pl.program_id(0)
        col = lax.broadcasted_iota(jnp.int32, (1, wide), 1)
        k_minus_q = jnp.where(col < KB, col, col - wide)
        idx = jnp.clip(PADK - k_minus_q, -MAX_REL, MAX_REL) + MAX_REL
        f = jnp.zeros((1, wide), F32)
        for r in range(MAX_REL - CHUNK + 1, N_REL):
            f = jnp.where(idx == r, r_ref[h, r], f)
        rowi = lax.broadcasted_iota(jnp.int32, (QB, 1), 0)
        kcol = lax.broadcasted_iota(jnp.int32, (1, KB), 1)
        kc, qc = kcol >> 6, rowi >> 6
        band = (kc >= qc) & (kc <= qc + N_LEFT)
        a = jnp.broadcast_to(f * LOG2E, (QB, wide))
        for b in range(8):
            a = jnp.where((rowi >> b) & 1 == 1, pltpu.roll(a, 1 << b, 1), a)
        a = a[:, 0:KB]
        for t in range(3):
            o_ref[t, 0] = jnp.where(band & (kcol >= PADK - t * QB), a, NEG_BIG)

    return pl.pallas_call(
        body, name="bias_table", grid=(HEADS,),
        out_shape=jax.ShapeDtypeStruct((3, HEADS, QB, KB), F32),
        in_specs=[pl.BlockSpec(memory_space=pltpu.SMEM)],
        out_specs=pl.BlockSpec((3, 1, QB, KB), lambda h: (0, h, 0, 0)),
        compiler_params=_params(1),
    )(rel_bias)


def _load_keys(p_hbm, kp, vp, sem):
    kp[0:PADK, :] = jnp.zeros((PADK, D_ATT), BF16)
    vp[0:PADK, :] = jnp.zeros((PADK, D_ATT), BF16)
    S = p_hbm.shape[0]
    ck = pltpu.make_async_copy(p_hbm.at[:, D_ATT:2 * D_ATT], kp.at[PADK:PADK + S, :], sem.at[0])
    cv = pltpu.make_async_copy(p_hbm.at[:, 2 * D_ATT:3 * D_ATT], vp.at[PADK:PADK + S, :], sem.at[1])
    ck.start()
    cv.start()
    ck.wait()
    cv.wait()


def _attn_fwd(P, bias_tab):
    S = P.shape[0]
    nb = S // QB

    def body(q_ref, p_hbm, bias_ref, o_ref, lse_ref, kp, vp, sem):
        g = pl.program_id(0)

        @pl.when(g == 0)
        def _():
            _load_keys(p_hbm, kp, vp, sem)

        start = pl.multiple_of(g * QB, QB)
        lane = lax.broadcasted_iota(jnp.int32, (1, 128), 1)
        for p in range(HEADS // 2):
            cols = slice(128 * p, 128 * (p + 1))
            qp = q_ref[:, cols] * SCALE
            kpair = kp[pl.ds(start, KB), cols]
            vpair = vp[pl.ds(start, KB), cols]
            outs = []
            for e in range(2):
                h = 2 * p + e
                lm = (lane < 64) if e == 0 else (lane >= 64)
                qm = jnp.where(lm, qp, jnp.zeros_like(qp))
                s = _dot_nt(qm, kpair) * LOG2E + bias_ref[0, h]
                mx = jnp.max(s, axis=-1, keepdims=True)
                ex = jnp.exp2(s - mx)
                sm = jnp.sum(ex, axis=-1, keepdims=True)
                outs.append(_dot(ex.astype(BF16), vpair) * (1.0 / sm))
                lse_ref[:, h:h + 1] = mx + jnp.log2(sm)
            o_ref[:, cols] = jnp.where(lane < 64, outs[0], outs[1]).astype(BF16)

    return pl.pallas_call(
        body, name="attn_fwd", grid=(nb,),
        out_shape=[jax.ShapeDtypeStruct((S, D_ATT), BF16), jax.ShapeDtypeStruct((S, HEADS), F32)],
        in_specs=[pl.BlockSpec((QB, D_ATT), lambda g: (g, 0)), ANY,
                  pl.BlockSpec((1, HEADS, QB, KB), lambda g: (jnp.minimum(g, 2), 0, 0, 0))],
        out_specs=[pl.BlockSpec((QB, D_ATT), lambda g: (g, 0)),
                   pl.BlockSpec((QB, HEADS), lambda g: (g, 0))],
        scratch_shapes=[pltpu.VMEM((S + PADK, D_ATT), BF16), pltpu.VMEM((S + PADK, D_ATT), BF16),
                        pltpu.SemaphoreType.DMA((2,))],
        compiler_params=_params(1),
    )(P, P, bias_tab)


def _token_local(x, tgt, P, att, w_att_out, w_conv_out, w_out, conv_w, conv_b, final_g):
    S = x.shape[0]
    ts = 256
    nt = S // ts
    hb = 16

    def body(x_ref, t_ref, s1_ref, s2_ref, s3_ref, h1_ref, h2_ref, att_ref,
             wao_ref, wco_ref, wo_ref, cw_ref, cb_ref, g2_ref,
             dx2_ref, dg_ref, datt_ref, dwo_ref, dwao_ref, dwco_ref, sm1_ref, sm2_ref, carry):
        i = pl.program_id(0)
        t = nt - 1 - i

        @pl.when(i == 0)
        def _():
            dwo_ref[...] = jnp.zeros_like(dwo_ref)
            dwao_ref[...] = jnp.zeros_like(dwao_ref)
            dwco_ref[...] = jnp.zeros_like(dwco_ref)
            sm1_ref[...] = jnp.zeros_like(sm1_ref)
            sm2_ref[...] = jnp.zeros_like(sm2_ref)
            carry[...] = jnp.zeros_like(carry)

        za = s1_ref[:, 0:512].astype(F32)
        gb = s1_ref[:, 512:1024].astype(F32)
        gc = s1_ref[:, 1024:1536].astype(F32)
        u = s2_ref[:, 0:512].astype(F32)
        zc = s2_ref[:, 512:1024].astype(F32)
        ga = jnp.concatenate([s2_ref[:, 1024:1536], s3_ref[:, 0:512]], axis=1).astype(F32)
        gv = s3_ref[:, 512:1536].astype(F32)
        att = att_ref[...].astype(F32)
        row = lax.broadcasted_iota(jnp.int32, (ts, 1), 0)

        sa = _sigmoid(za)
        silu_a = za * sa
        att_g = (att * silu_a).astype(BF16)
        y_att = _dot(att_g, wao_ref[...])

        cu = gc * u
        keep = jnp.where(t > 0, 1.0, 0.0).astype(F32)
        hcu = (h1_ref[:, 1024:1536].astype(F32) * h2_ref[:, 0:512].astype(F32)) * keep
        cu_m1 = jnp.where(row == 0, hcu[hb - 1:hb, :], pltpu.roll(cu, 1, 0))
        cu_m2 = jnp.where(row == 0, hcu[hb - 2:hb - 1, :],
                          jnp.where(row == 1, hcu[hb - 1:hb, :], pltpu.roll(cu, 2, 0)))
        w0, w1, w2 = cw_ref[0:1, :], cw_ref[1:2, :], cw_ref[2:3, :]
        vconv = w0 * cu_m2 + w1 * cu_m1 + w2 * cu + cb_ref[...]
        sc = _sigmoid(zc)
        silu_c = zc * sc
        cg = (gb * vconv * silu_c).astype(BF16)
        y_conv = _dot(cg, wco_ref[...])

        sga = _sigmoid(ga)
        sgv = _sigmoid(gv)
        m = (sga * y_att + sgv * y_conv).astype(BF16)
        x2 = x_ref[...] + _dot(m, wo_ref[...])
        r2 = lax.rsqrt(jnp.mean(x2 * x2, axis=-1, keepdims=True) + EPS)
        xn2 = x2 * r2
        g2 = g2_ref[...]
        err = xn2 * g2 - t_ref[...]
        sm1_ref[1:2, :] += jnp.sum(err * err, axis=0, keepdims=True) * (0.5 / D_MODEL)

        dy = err * (1.0 / D_MODEL)
        sm1_ref[0:1, :] += jnp.sum(dy * xn2, axis=0, keepdims=True)
        dxn = dy * g2
        dx2 = r2 * (dxn - xn2 * jnp.mean(dxn * xn2, axis=-1, keepdims=True))
        dx2_ref[...] = dx2
        dx2b = dx2.astype(BF16)
        dwo_ref[...] += _dot_tn(m, dx2b)
        dm = _dot_nt(dx2b, wo_ref[...])
        dya = (dm * sga).astype(BF16)
        dyc = (dm * sgv).astype(BF16)
        dg_ref[:, 2560:3584] = (dm * y_att * (sga * (1.0 - sga))).astype(BF16)
        dg_ref[:, 3584:4608] = (dm * y_conv * (sgv * (1.0 - sgv))).astype(BF16)
        dwao_ref[...] += _dot_tn(att_g, dya)
        dwco_ref[...] += _dot_tn(cg, dyc)
        datt_g = _dot_nt(dya, wao_ref[...])
        dcg = _dot_nt(dyc, wco_ref[...])
        datt_ref[...] = (datt_g * silu_a).astype(BF16)
        dg_ref[:, 0:512] = (datt_g * att * (sa * (1.0 + za * (1.0 - sa)))).astype(BF16)
        dg_ref[:, 512:1024] = (dcg * vconv * silu_c).astype(BF16)
        dg_ref[:, 2048:2560] = (dcg * gb * vconv * (sc * (1.0 + zc * (1.0 - sc)))).astype(BF16)
        dv = dcg * gb * silu_c
        sm2_ref[3:4, :] += jnp.sum(dv, axis=0, keepdims=True)
        sm2_ref[0:1, :] += jnp.sum(dv * cu_m2, axis=0, keepdims=True)
        sm2_ref[1:2, :] += jnp.sum(dv * cu_m1, axis=0, keepdims=True)
        sm2_ref[2:3, :] += jnp.sum(dv * cu, axis=0, keepdims=True)
        nxt = carry[...]
        dv_p1 = jnp.where(row == ts - 1, nxt[0:1, :], pltpu.roll(dv, ts - 1, 0))
        dv_p2 = jnp.where(row == ts - 1, nxt[1:2, :],
                          jnp.where(row == ts - 2, nxt[0:1, :], pltpu.roll(dv, ts - 2, 0)))
        dcu = w2 * dv + w1 * dv_p1 + w0 * dv_p2
        carry[...] = dv[0:8, :]
        dg_ref[:, 1024:1536] = (dcu * u).astype(BF16)
        dg_ref[:, 1536:2048] = (dcu * gc).astype(BF16)

    tile = lambda w: pl.BlockSpec((ts, w), lambda i: (nt - 1 - i, 0))
    seg = lambda c: pl.BlockSpec((ts, 1536), lambda i: (nt - 1 - i, c))
    halo = lambda c: pl.BlockSpec((hb, 1536), lambda i: (jnp.maximum((nt - 1 - i) * (ts // hb) - 1, 0), c))
    full = lambda a: pl.BlockSpec(a.shape, lambda i: (0,) * a.ndim)
    acc = lambda r, c: pl.BlockSpec((r, c), lambda i: (0, 0))
    return pl.pallas_call(
        body, name="token_local", grid=(nt,),
        out_shape=[jax.ShapeDtypeStruct((S, D_MODEL), F32), jax.ShapeDtypeStruct((S, GATE_COLS), BF16),
                   jax.ShapeDtypeStruct((S, D_ATT), BF16), jax.ShapeDtypeStruct((D_MODEL, D_MODEL), F32),
                   jax.ShapeDtypeStruct((D_ATT, D_MODEL), F32), jax.ShapeDtypeStruct((D_CONV, D_MODEL), F32),
                   jax.ShapeDtypeStruct((8, D_MODEL), F32), jax.ShapeDtypeStruct((8, D_CONV), F32)],
        in_specs=[tile(D_MODEL), tile(D_MODEL), seg(1), seg(2), seg(3), halo(1), halo(2), tile(D_ATT),
                  full(w_att_out), full(w_conv_out), full(w_out), full(conv_w), full(conv_b), full(final_g)],
        out_specs=[tile(D_MODEL), tile(GATE_COLS), tile(D_ATT), acc(D_MODEL, D_MODEL), acc(D_ATT, D_MODEL),
                   acc(D_CONV, D_MODEL), acc(8, D_MODEL), acc(8, D_CONV)],
        scratch_shapes=[pltpu.VMEM((8, D_CONV), F32)],
        compiler_params=_params(1),
    )(x, tgt, P, P, P, P, P, att, w_att_out, w_conv_out, w_out, conv_w, conv_b, final_g)


def _attn_bwd(P, att, datt, lse, bias_tab):
    S = P.shape[0]
    nb = S // QB

    def body(q_ref, att_ref, datt_ref, lse_ref, p_hbm, bias_ref, out_ref, db_ref,
             kp, vp, dq_ring, dk_ring, dv_ring, sem):
        g = pl.program_id(0)

        @pl.when(g == 0)
        def _():
            _load_keys(p_hbm, kp, vp, sem)
            db_ref[...] = jnp.zeros_like(db_ref)
            dk_ring[...] = jnp.zeros_like(dk_ring)
            dv_ring[...] = jnp.zeros_like(dv_ring)

        s_new = g % 3
        s_mid = (g + 2) % 3
        s_old = (g + 1) % 3

        @pl.when(g < nb)
        def _():
            start = pl.multiple_of(g * QB, QB)
            lane = lax.broadcasted_iota(jnp.int32, (1, 128), 1)
            for p in range(HEADS // 2):
                cols = slice(128 * p, 128 * (p + 1))
                qp = q_ref[:, cols] * SCALE
                op = att_ref[:, cols].astype(F32)
                dop = datt_ref[:, cols]
                kpair = kp[pl.ds(start, KB), cols]
                vpair = vp[pl.ds(start, KB), cols]
                dqs = []
                dk_acc = jnp.zeros((KB, 128), F32)
                dv_acc = jnp.zeros((KB, 128), F32)
                for e in range(2):
                    h = 2 * p + e
                    lm = (lane < 64) if e == 0 else (lane >= 64)
                    qm = jnp.where(lm, qp, jnp.zeros_like(qp))
                    dom = jnp.where(lm, dop, jnp.zeros_like(dop))
                    s = _dot_nt(qm, kpair) * LOG2E + bias_ref[0, h]
                    pr = jnp.exp2(s - lse_ref[:, h:h + 1])
                    dp = _dot_nt(dom, vpair)
                    delta = jnp.sum(dom.astype(F32) * op, axis=-1, keepdims=True)
                    ds = pr * (dp - delta)
                    db_ref[h] += ds
                    dsb = ds.astype(BF16)
                    prb = pr.astype(BF16)
                    dqs.append(_dot(dsb, kpair) * SCALE)
                    dk_acc = dk_acc + _dot_tn(dsb, qm)
                    dv_acc = dv_acc + _dot_tn(prb, dom)
                dq_ring[s_new, :, cols] = jnp.where(lane < 64, dqs[0], dqs[1])
                dk_ring[s_old, :, cols] += dk_acc[0:QB]
                dk_ring[s_mid, :, cols] += dk_acc[QB:2 * QB]
                dk_ring[s_new, :, cols] = dk_acc[2 * QB:3 * QB]
                dv_ring[s_old, :, cols] += dv_acc[0:QB]
                dv_ring[s_mid, :, cols] += dv_acc[QB:2 * QB]
                dv_ring[s_new, :, cols] = dv_acc[2 * QB:3 * QB]

        @pl.when(g >= 2)
        def _():
            out_ref[:, 0:D_ATT] = dq_ring[s_old].astype(BF16)
            out_ref[:, D_ATT:2 * D_ATT] = dk_ring[s_old].astype(BF16)
            out_ref[:, 2 * D_ATT:3 * D_ATT] = dv_ring[s_old].astype(BF16)

    qblk = lambda w: pl.BlockSpec((QB, w), lambda g: (jnp.minimum(g, nb - 1), 0))
    return pl.pallas_call(
        body, name="attn_bwd", grid=(nb + 2,),
        out_shape=[jax.ShapeDtypeStruct((S, 3 * D_ATT), BF16), jax.ShapeDtypeStruct((HEADS, QB, KB), F32)],
        in_specs=[qblk(D_ATT), qblk(D_ATT), qblk(D_ATT), qblk(HEADS), ANY,
                  pl.BlockSpec((1, HEADS, QB, KB), lambda g: (jnp.minimum(g, 2), 0, 0, 0))],
        out_specs=[pl.BlockSpec((QB, 3 * D_ATT), lambda g: (jnp.maximum(g - 2, 0), 0)),
                   pl.BlockSpec((HEADS, QB, KB), lambda g: (0, 0, 0))],
        scratch_shapes=[pltpu.VMEM((S + PADK, D_ATT), BF16), pltpu.VMEM((S + PADK, D_ATT), BF16),
                        pltpu.VMEM((3, QB, D_ATT), F32), pltpu.VMEM((3, QB, D_ATT), F32),
                        pltpu.VMEM((3, QB, D_ATT), F32), pltpu.SemaphoreType.DMA((2,))],
        compiler_params=_params(1),
    )(P, att, datt, lse, P, bias_tab)


def _bias_fold(dscore):
    wide = 1024

    def body(d_ref, o_ref):
        rowi = lax.broadcasted_iota(jnp.int32, (QB, 1), 0)
        col = lax.broadcasted_iota(jnp.int32, (1, wide), 1)
        for h in range(HEADS):
            a = jnp.concatenate([d_ref[h], jnp.zeros((QB, wide - KB), F32)], axis=1)
            for b in range(8):
                a = jnp.where((rowi >> b) & 1 == 1, pltpu.roll(a, wide - (1 << b), 1), a)
            vec = jnp.sum(a, axis=0, keepdims=True)
            far = (col <= PADK - MAX_REL) | (col > KB)
            tail = jnp.sum(jnp.where(far, vec, 0.0), axis=-1, keepdims=True)
            o_ref[h:h + 1, :] = jnp.where(col == wide - 1, tail, vec)

    return pl.pallas_call(
        body, name="bias_fold",
        out_shape=jax.ShapeDtypeStruct((HEADS, wide), F32),
        compiler_params=pltpu.CompilerParams(vmem_limit_bytes=VMEM_LIMIT),
    )(dscore)


def _in_proj_bwd(x, norm_g, dx2, dqkv, dgate, w_in_g):
    S = x.shape[0]
    ts = 256

    def body(x_ref, g_ref, dx2_ref, dq_ref, dg_ref, w_ref, gx_ref, dn_ref):
        @pl.when(pl.program_id(0) == 0)
        def _():
            dn_ref[...] = jnp.zeros_like(dn_ref)

        dh = jnp.zeros((ts, D_MODEL), F32)
        for j in range(N_DEV):
            if j < 2:
                d = dq_ref[:, j * W_BLK:(j + 1) * W_BLK]
            else:
                d = dg_ref[:, (j - 2) * W_BLK:(j - 1) * W_BLK]
            dh = dh + _dot_nt(d, w_ref[j])
        xf = x_ref[...]
        r = lax.rsqrt(jnp.mean(xf * xf, axis=-1, keepdims=True) + EPS)
        xn = xf * r
        dn_ref[0:1, :] += jnp.sum(dh * xn, axis=0, keepdims=True)
        dhg = dh * g_ref[...]
        gx_ref[...] = dx2_ref[...] + r * (dhg - xn * jnp.mean(dhg * xn, axis=-1, keepdims=True))

    tile = lambda w: pl.BlockSpec((ts, w), lambda i: (i, 0))
    return pl.pallas_call(
        body, name="in_proj_bwd", grid=(S // ts,),
        out_shape=[jax.ShapeDtypeStruct((S, D_MODEL), F32), jax.ShapeDtypeStruct((8, D_MODEL), F32)],
        in_specs=[tile(D_MODEL), pl.BlockSpec((1, D_MODEL), lambda i: (0, 0)), tile(D_MODEL),
                  tile(3 * D_ATT), tile(GATE_COLS),
                  pl.BlockSpec((N_DEV, D_MODEL, W_BLK), lambda i: (0, 0, 0))],
        out_specs=[tile(D_MODEL), pl.BlockSpec((8, D_MODEL), lambda i: (0, 0))],
        compiler_params=_params(1),
    )(x, norm_g, dx2, dqkv, dgate, w_in_g)


PEER_MASKS = (1, 4, 2, 6, 5, 3, 7, 0)


def _w_in_grad_scatter(ht, dqkv, dgate, d_proj, d_wo, order):
    S = ht.shape[1]
    ts = 1024
    nt = S // ts
    n_steps = len(PEER_MASKS)

    def body(order_ref, ht_ref, dq_ref, dg_ref, proj_hbm, wo_hbm, rwin, rproj, rwo,
             acc, stage, send_sems, recv_sems, small_send, small_recv, local_sems):
        k, i = pl.program_id(0), pl.program_id(1)
        x, y, c = _mesh_pos()
        my = _flat((x, y, c))
        peers = [(x, y, 1 - c), (1 - x, y, c), (x, 1 - y, c), (1 - x, 1 - y, c),
                 (1 - x, y, 1 - c), (x, 1 - y, 1 - c), (1 - x, 1 - y, 1 - c)]
        small = ((proj_hbm, rproj), (wo_hbm, rwo))

        def small_copy(kk, a, receive=False):
            src, dst = small[a]
            slot = _flat(peers[kk]) if receive else my
            return pltpu.make_async_remote_copy(
                src_ref=src.at[_flat(peers[kk])], dst_ref=dst.at[slot],
                send_sem=small_send.at[kk, a], recv_sem=small_recv.at[kk, a],
                device_id=peers[kk], device_id_type=MESH)

        def win_copy(kk, receive=False):
            slot = _flat(peers[kk]) if receive else my
            return pltpu.make_async_remote_copy(
                src_ref=stage.at[kk % 2], dst_ref=rwin.at[slot],
                send_sem=send_sems.at[kk], recv_sem=recv_sems.at[kk],
                device_id=peers[kk], device_id_type=MESH)

        own_small = [pltpu.make_async_copy(src.at[my], dst.at[my], local_sems.at[a])
                     for a, (src, dst) in enumerate(small)]

        @pl.when((k == 0) & (i == 0))
        def _():
            for cp in own_small:
                cp.start()
            for kk in range(n_steps - 1):
                for a in range(2):
                    small_copy(kk, a).start()

        def accumulate(d_ref):
            prod = _dot(ht_ref[...], d_ref[...])

            @pl.when(i == 0)
            def _():
                acc[...] = prod

            @pl.when(i > 0)
            def _():
                acc[...] += prod

        @pl.when(order_ref[k] < 2)
        def _():
            accumulate(dq_ref)

        @pl.when(order_ref[k] >= 2)
        def _():
            accumulate(dg_ref)

        @pl.when(i == nt - 1)
        def _():
            for kk in range(n_steps):
                @pl.when(k == kk)
                def _():
                    if kk >= 2:
                        win_copy(kk - 2).wait_send()
                    stage[kk % 2] = acc[...].astype(BF16)
                    if kk < n_steps - 1:
                        win_copy(kk).start()
                    else:
                        own = pltpu.make_async_copy(stage.at[kk % 2], rwin.at[my], local_sems.at[2])
                        own.start()
                        win_copy(kk - 1).wait_send()
                        for q in range(n_steps - 1):
                            win_copy(q, receive=True).wait_recv()
                            for a in range(2):
                                small_copy(q, a).wait_send()
                                small_copy(q, a, receive=True).wait_recv()
                        for cp in own_small:
                            cp.wait()
                        own.wait()

    grid_spec = pltpu.PrefetchScalarGridSpec(
        num_scalar_prefetch=1, grid=(n_steps, nt),
        in_specs=[pl.BlockSpec((D_MODEL, ts), lambda k, i, o: (0, i)),
                  pl.BlockSpec((ts, W_BLK), lambda k, i, o: (i, jnp.minimum(o[k], 1))),
                  pl.BlockSpec((ts, W_BLK), lambda k, i, o: (i, jnp.maximum(o[k] - 2, 0))),
                  ANY, ANY],
        out_specs=[ANY, ANY, ANY],
        scratch_shapes=[pltpu.VMEM((D_MODEL, W_BLK), F32), pltpu.VMEM((2, D_MODEL, W_BLK), BF16),
                        pltpu.SemaphoreType.DMA((n_steps - 1,)), pltpu.SemaphoreType.DMA((n_steps - 1,)),
                        pltpu.SemaphoreType.DMA((n_steps - 1, 2)), pltpu.SemaphoreType.DMA((n_steps - 1, 2)),
                        pltpu.SemaphoreType.DMA((3,))])
    return pl.pallas_call(
        body, name="w_in_grad_scatter", grid_spec=grid_spec,
        out_shape=[jax.ShapeDtypeStruct((N_DEV, D_MODEL, W_BLK), BF16),
                   jax.ShapeDtypeStruct(d_proj.shape, BF16), jax.ShapeDtypeStruct(d_wo.shape, BF16)],
        compiler_params=_params(2),
    )(order, ht, dqkv, dgate, d_proj, d_wo)


def _adamw(w, g, m, v):
    m = ADAM_B1 * m + (1.0 - ADAM_B1) * g
    v = ADAM_B2 * v + (1.0 - ADAM_B2) * (g * g)
    m_hat = m / (1.0 - ADAM_B1 ** ADAM_STEP)
    v_hat = v / (1.0 - ADAM_B2 ** ADAM_STEP)
    delta = -ADAM_LR * (m_hat / (jnp.sqrt(v_hat) + ADAM_EPS) + ADAM_WD * w)
    return delta, m, v


def _sum_adamw(parts, w, m, v, name):
    R, C = w.shape
    tr = min(R, 256)

    def body(p_ref, w_ref, m_ref, v_ref, g_ref, d_ref, nm_ref, nv_ref):
        g = p_ref[0].astype(F32)
        for s in range(1, N_DEV):
            g = g + p_ref[s].astype(F32)
        g_ref[...] = g
        d_ref[...], nm_ref[...], nv_ref[...] = _adamw(w_ref[...], g, m_ref[...], v_ref[...])

    tile = pl.BlockSpec((tr, C), lambda i: (i, 0))
    return pl.pallas_call(
        body, name=name, grid=(R // tr,),
        out_shape=[jax.ShapeDtypeStruct((R, C), F32)] * 4,
        in_specs=[pl.BlockSpec((N_DEV, tr, C), lambda i: (0, i, 0)), tile, tile, tile],
        out_specs=[tile] * 4,
        compiler_params=_params(1),
    )(parts, w, m, v)


def _sum_parts(parts):
    def body(p_ref, o_ref):
        g = p_ref[0]
        for s in range(1, N_DEV):
            g = g + p_ref[s]
        o_ref[...] = g

    return pl.pallas_call(body, name="sum_small",
                          out_shape=jax.ShapeDtypeStruct(parts.shape[1:], F32))(parts)


def _adamw_small(w, g, m, v):
    def body(w_ref, g_ref, m_ref, v_ref, d_ref, nm_ref, nv_ref):
        d_ref[...], nm_ref[...], nv_ref[...] = _adamw(w_ref[...], g_ref[...], m_ref[...], v_ref[...])

    return pl.pallas_call(body, name="adamw_small",
                          out_shape=[jax.ShapeDtypeStruct(w.shape, F32)] * 3)(w, g, m, v)


def _pad_row(a, width=D_MODEL):
    a = a.reshape(-1, a.shape[-1])
    return jnp.pad(a, ((0, 0), (0, width - a.shape[-1])))


def _owner_cols(a, blk):
    r = a.shape[0]
    return a.reshape(r, N_DEV, blk).transpose(1, 0, 2)


def kernel(x, norm_g, w_in, rel_bias, w_att_out, conv_w, conv_b, w_conv_out, w_out, final_norm_g, loss_target, m_norm_g, m_w_in, m_rel_bias, m_w_att_out, m_conv_w, m_conv_b, m_w_conv_out, m_w_out, m_final_norm_g, v_norm_g, v_w_in, v_rel_bias, v_w_att_out, v_conv_w, v_conv_b, v_w_conv_out, v_w_out, v_final_norm_g):
    S = x.shape[1]
    x2d = x.reshape(S, D_MODEL)
    tgt = loss_target.reshape(S, D_MODEL)
    me = 4 * lax.axis_index("x") + 2 * lax.axis_index("y") + lax.axis_index("c")

    proj_sh = jnp.concatenate([w_att_out[0], w_conv_out[0]], axis=1).astype(BF16)
    cw_sh = jnp.pad(conv_w[0], ((0, 5), (0, 64)))
    w_in_g, proj_g, w_out_g, cw_g = _all_gather(
        [w_in[0].astype(BF16), proj_sh, w_out[0].astype(BF16), cw_sh], "gather_weights")
    wao = proj_g[:, :, 0:128].transpose(1, 0, 2).reshape(D_ATT, D_MODEL)
    wco = proj_g[:, :, 128:256].transpose(1, 0, 2).reshape(D_CONV, D_MODEL)
    wo = w_out_g.reshape(D_MODEL, D_MODEL)
    cw = cw_g[:, 0:3, 0:64].transpose(1, 0, 2).reshape(3, D_CONV)

    P, ht = _in_proj(x2d, norm_g, w_in_g)
    bias_tab = _bias_table(rel_bias[0])
    att, lse = _attn_fwd(P, bias_tab)
    dx2, dgate, datt, d_wo, d_wao, d_wco, sm1, sm2 = _token_local(
        x2d, tgt, P, att, wao, wco, wo, cw, conv_b, final_norm_g.reshape(1, D_MODEL))
    dqkv, dscore = _attn_bwd(P, att, datt, lse, bias_tab)
    dbias = _bias_fold(dscore)
    grad_x, dnorm = _in_proj_bwd(x2d, norm_g, dx2, dqkv, dgate, w_in_g)

    d_proj = jnp.concatenate([_owner_cols(d_wao, 128), _owner_cols(d_wco, 128)], axis=2).astype(BF16)
    order = me ^ jnp.array(PEER_MASKS, jnp.int32)
    r_win, r_proj, r_wo = _w_in_grad_scatter(
        ht, dqkv, dgate, d_proj, d_wo.reshape(N_DEV, 128, D_MODEL).astype(BF16), order)
    small = jnp.concatenate([dnorm[0:1], sm1[0:2], _pad_row(sm2[0:4]), jnp.zeros((1, D_MODEL), F32), dbias],
                            axis=0)
    (small_g,) = _all_gather([small], "gather_small")
    tot = _sum_parts(small_g)

    g_win, d_win_, nm_win, nv_win = _sum_adamw(r_win, w_in[0], m_w_in[0], v_w_in[0], "adamw_w_in")
    w_proj = jnp.concatenate([w_att_out[0], w_conv_out[0]], axis=1)
    m_proj = jnp.concatenate([m_w_att_out[0], m_w_conv_out[0]], axis=1)
    v_proj = jnp.concatenate([v_w_att_out[0], v_w_conv_out[0]], axis=1)
    g_proj, d_proj_, nm_proj, nv_proj = _sum_adamw(r_proj, w_proj, m_proj, v_proj, "adamw_proj")
    g_wo, d_wo_, nm_wo, nv_wo = _sum_adamw(r_wo, w_out[0], m_w_out[0], v_w_out[0], "adamw_w_out")

    loss = jnp.sum(tot[2])
    g_norm, g_final = tot[0:1], tot[1]
    g_cw = lax.dynamic_slice(tot[3:6, 0:D_CONV], (0, me * 64), (3, 64))
    g_cb = tot[6:7, 0:D_CONV]
    vec = tot[8:16]
    g_rel = jnp.concatenate([jnp.zeros((HEADS, MAX_REL - CHUNK + 1), F32),
                             vec[:, PADK - MAX_REL + 1:PADK + CHUNK][:, ::-1], vec[:, 1023:1024]], axis=1)

    def pack(norm, final, cwv, cbv, rel):
        return jnp.concatenate([norm, final.reshape(1, D_MODEL), _pad_row(cwv), _pad_row(cbv),
                                jnp.zeros((2, D_MODEL), F32), _pad_row(rel)], axis=0)

    w_s = pack(norm_g, final_norm_g, conv_w[0], conv_b, rel_bias[0])
    g_s = pack(g_norm, g_final, g_cw, g_cb, g_rel)
    m_s = pack(m_norm_g, m_final_norm_g, m_conv_w[0], m_conv_b, m_rel_bias[0])
    v_s = pack(v_norm_g, v_final_norm_g, v_conv_w[0], v_conv_b, v_rel_bias[0])
    d_s, nm_s, nv_s = _adamw_small(w_s, g_s, m_s, v_s)

    def unpack(p):
        return (p[0:1], p[1], p[2:5, 0:64][None], p[5:6, 0:D_CONV], p[8:16, 0:N_REL][None])

    def sharded(a_in, a_proj, a_wo):
        return a_in[None], a_proj[:, 0:128][None], a_proj[:, 128:256][None], a_wo[None]

    outs = []
    for small_pack, big in ((g_s, (g_win, g_proj, g_wo)), (d_s, (d_win_, d_proj_, d_wo_)),
                            (nm_s, (nm_win, nm_proj, nm_wo)), (nv_s, (nv_win, nv_proj, nv_wo))):
        norm, final, cwv, cbv, rel = unpack(small_pack)
        b_in, b_ao, b_co, b_wo = sharded(*big)
        outs += [norm, b_in, rel, b_ao, cwv, cbv, b_co, b_wo, final]
    return (loss, grad_x.reshape(1, S, D_MODEL), *outs)
```

```python
import functools

import numpy as np
import jax
import jax.numpy as jnp
from jax import lax
from jax.experimental import pallas as pl
from jax.experimental.pallas import tpu as pltpu

F32 = jnp.float32
BF16 = jnp.bfloat16

D_MODEL = 1024
CHUNK = 64
N_LEFT = 8
HEADS = 8
D_ATT = 512
D_CONV = 512
MAX_REL = 128
N_REL = 2 * MAX_REL + 1
IN_COLS = 6144
EPS = 1e-6
NEG_BIG = -1e30
N_DEV = 8
W_BLK = IN_COLS // N_DEV
QB = 4 * CHUNK
KB = QB + N_LEFT * CHUNK
PADK = N_LEFT * CHUNK
SCALE = 64 ** -0.5
LOG2E = 1.4426950408889634
GATE_COLS = IN_COLS - 3 * D_ATT

ADAM_LR = 0.001
ADAM_B1 = 0.9
ADAM_B2 = 0.999
ADAM_EPS = 1e-08
ADAM_WD = 0.01
ADAM_STEP = 10

VMEM_LIMIT = 56 * 1024 * 1024

MESH = pl.DeviceIdType.MESH
ANY = pl.BlockSpec(memory_space=pl.ANY)


def _params(n_grid):
    return pltpu.CompilerParams(dimension_semantics=("arbitrary",) * n_grid,
                                vmem_limit_bytes=VMEM_LIMIT)


def _dot(a, b):
    return jnp.dot(a, b, preferred_element_type=F32)


def _dot_nt(a, b):
    return lax.dot_general(a, b, (((1,), (1,)), ((), ())), preferred_element_type=F32)


def _dot_tn(a, b):
    return lax.dot_general(a, b, (((0,), (0,)), ((), ())), preferred_element_type=F32)


def _sigmoid(z):
    return 1.0 / (1.0 + jnp.exp(-z))


def _mesh_pos():
    return lax.axis_index("x"), lax.axis_index("y"), lax.axis_index("c")


def _flat(p):
    return 4 * p[0] + 2 * p[1] + p[2]


def _all_gather(shards, name):
    n = len(shards)

    def body(*refs):
        ins, outs = refs[:n], refs[n:2 * n]
        send_sems, recv_sems, local_sems = refs[2 * n:]
        x, y, c = _mesh_pos()
        me, sibling = (x, y, c), (x, y, 1 - c)
        chips = [(1 - x, y), (x, 1 - y), (1 - x, 1 - y)]

        def copy(k, a, block, to, from_input=False):
            rows = outs[a].at[_flat(block)]
            return pltpu.make_async_remote_copy(
                src_ref=ins[a] if from_input else rows, dst_ref=rows,
                send_sem=send_sems.at[k, a], recv_sem=recv_sems.at[k, a],
                device_id=to, device_id_type=MESH)

        mine = [pltpu.make_async_copy(ins[a], outs[a].at[_flat(me)], local_sems.at[a]) for a in range(n)]
        for cp in mine:
            cp.start()
        first = [copy(0, a, me, sibling, True) for a in range(n)]
        for j, chip in enumerate(chips):
            first += [copy(1 + j, a, me, (*chip, c), True) for a in range(n)]
        for cp in first:
            cp.start()
        passed = []
        for j, chip in enumerate(chips):
            for a in range(n):
                copy(1 + j, a, (*chip, c), me).wait_recv()
            fwd = [copy(4 + j, a, (*chip, c), sibling) for a in range(n)]
            for cp in fwd:
                cp.start()
            passed += fwd
        for a in range(n):
            copy(0, a, sibling, me).wait_recv()
        for j, chip in enumerate(chips):
            for a in range(n):
                copy(4 + j, a, (*chip, 1 - c), me).wait_recv()
        for cp in first + passed:
            cp.wait_send()
        for cp in mine:
            cp.wait()

    return pl.pallas_call(
        body, name=name,
        out_shape=[jax.ShapeDtypeStruct((N_DEV,) + s.shape, s.dtype) for s in shards],
        in_specs=[ANY] * n, out_specs=[ANY] * n,
        scratch_shapes=[pltpu.SemaphoreType.DMA((7, n)), pltpu.SemaphoreType.DMA((7, n)),
                        pltpu.SemaphoreType.DMA((n,))],
    )(*shards)


GATHER_MASKS = (0, 1, 4, 5, 2, 3, 6, 7)


def _gather_in_proj(x, norm_g, w_sh, smalls, order):
    S = x.shape[0]
    ts = 512
    nt = S // ts
    n_small = len(smalls)
    n_steps = len(GATHER_MASKS)

    def body(order_ref, x_ref, g_ref, w_hbm, *rest):
        small_in = rest[:n_small]
        p_ref, ht_ref, wg_hbm = rest[n_small:n_small + 3]
        small_out = rest[n_small + 3:2 * n_small + 3]
        (wbuf, hbuf, own_sem, send_sems, recv_sems, out_sems,
         small_send, small_recv, small_local) = rest[2 * n_small + 3:]
        k, i = pl.program_id(0), pl.program_id(1)
        x_, y_, c_ = _mesh_pos()
        me, sibling = (x_, y_, c_), (x_, y_, 1 - c_)
        my = _flat(me)
        chips = [(1 - x_, y_), (x_, 1 - y_), (1 - x_, 1 - y_)]
        peers = [sibling] + [(*chip, c_) for chip in chips] + [(*chip, 1 - c_) for chip in chips]

        def wcopy(sem, block, to, from_input=False):
            dst = wbuf.at[_flat(block)]
            return pltpu.make_async_remote_copy(
                src_ref=w_hbm if from_input else dst, dst_ref=dst,
                send_sem=send_sems.at[sem], recv_sem=recv_sems.at[sem], device_id=to, device_id_type=MESH)

        def small_copy(q, a, receive=False):
            slot = _flat(peers[q]) if receive else my
            return pltpu.make_async_remote_copy(
                src_ref=small_in[a], dst_ref=small_out[a].at[slot],
                send_sem=small_send.at[q, a], recv_sem=small_recv.at[q, a],
                device_id=peers[q], device_id_type=MESH)

        def keep(step, block):
            return pltpu.make_async_copy(wbuf.at[_flat(block)], wg_hbm.at[_flat(block)], out_sems.at[step])

        own = pltpu.make_async_copy(w_hbm, wbuf.at[my], own_sem)
        small_own = [pltpu.make_async_copy(small_in[a], small_out[a].at[my], small_local.at[a])
                     for a in range(n_small)]
        arrivals = [me, sibling]
        for chip in chips:
            arrivals += [(*chip, c_), (*chip, 1 - c_)]

        @pl.when(i == 0)
        def _():
            for kk in range(n_steps):
                @pl.when(k == kk)
                def _():
                    if kk == 0:
                        own.start()
                        wcopy(0, me, sibling, True).start()
                        for j, chip in enumerate(chips):
                            wcopy(1 + j, me, (*chip, c_), True).start()
                        for cp in small_own:
                            cp.start()
                        for q in range(len(peers)):
                            for a in range(n_small):
                                small_copy(q, a).start()
                        own.wait()
                    elif kk == 1:
                        wcopy(0, sibling, me).wait_recv()
                    elif kk % 2 == 0:
                        j = kk // 2 - 1
                        wcopy(1 + j, (*chips[j], c_), me).wait_recv()
                        wcopy(4 + j, (*chips[j], c_), sibling).start()
                    else:
                        j = kk // 2 - 1
                        wcopy(4 + j, (*chips[j], 1 - c_), me).wait_recv()
                    keep(kk, arrivals[kk]).start()

        row0 = pl.multiple_of(i * ts, ts)

        @pl.when(k == 0)
        def _():
            xf = x_ref[...]
            r = lax.rsqrt(jnp.mean(xf * xf, axis=-1, keepdims=True) + EPS)
            hf = (xf * r) * g_ref[...]
            hbuf[pl.ds(row0, ts), :] = hf.astype(BF16)
            ht_ref[...] = hf.T.astype(BF16)

        p_ref[...] = _dot(hbuf[pl.ds(row0, ts), :], wbuf[order_ref[k]]).astype(BF16)

        @pl.when((k == n_steps - 1) & (i == nt - 1))
        def _():
            wcopy(0, me, sibling, True).wait_send()
            for j, chip in enumerate(chips):
                wcopy(1 + j, me, (*chip, c_), True).wait_send()
                wcopy(4 + j, (*chip, c_), sibling).wait_send()
            for kk in range(n_steps):
                keep(kk, arrivals[kk]).wait()
            for cp in small_own:
                cp.wait()
            for q in range(len(peers)):
                for a in range(n_small):
                    small_copy(q, a).wait_send()
                    small_copy(q, a, receive=True).wait_recv()

    first_pass = lambda k, i: jnp.where(k == 0, i, nt - 1)
    grid_spec = pltpu.PrefetchScalarGridSpec(
        num_scalar_prefetch=1, grid=(n_steps, nt),
        in_specs=[pl.BlockSpec((ts, D_MODEL), lambda k, i, o: (first_pass(k, i), 0)),
                  pl.BlockSpec((1, D_MODEL), lambda k, i, o: (0, 0)), ANY] + [ANY] * n_small,
        out_specs=[pl.BlockSpec((ts, W_BLK), lambda k, i, o: (i, o[k])),
                   pl.BlockSpec((D_MODEL, ts), lambda k, i, o: (0, first_pass(k, i))), ANY] + [ANY] * n_small,
        scratch_shapes=[pltpu.VMEM((N_DEV, D_MODEL, W_BLK), BF16), pltpu.VMEM((S, D_MODEL), BF16),
                        pltpu.SemaphoreType.DMA, pltpu.SemaphoreType.DMA((7,)), pltpu.SemaphoreType.DMA((7,)),
                        pltpu.SemaphoreType.DMA((n_steps,)),
                        pltpu.SemaphoreType.DMA((7, n_small)), pltpu.SemaphoreType.DMA((7, n_small)),
                        pltpu.SemaphoreType.DMA((n_small,))])
    return pl.pallas_call(
        body, name="gather_in_proj", grid_spec=grid_spec,
        out_shape=[jax.ShapeDtypeStruct((S, IN_COLS), BF16), jax.ShapeDtypeStruct((D_MODEL, S), BF16),
                   jax.ShapeDtypeStruct((N_DEV,) + w_sh.shape, BF16)]
        + [jax.ShapeDtypeStruct((N_DEV,) + s.shape, s.dtype) for s in smalls],
        compiler_params=_params(2),
    )(order, x, norm_g, w_sh, *smalls)


def _bias_table(rel_bias):
    wide = 1024

    def body(r_ref, o_ref):
        h = pl.program_id(0)
        col = lax.broadcasted_iota(jnp.int32, (1, wide), 1)
        k_minus_q = jnp.where(col < KB, col, col - wide)
        idx = jnp.clip(PADK - k_minus_q, -MAX_REL, MAX_REL) + MAX_REL
        f = jnp.zeros((1, wide), F32)
        for r in range(MAX_REL - CHUNK + 1, N_REL):
            f = jnp.where(idx == r, r_ref[h, r], f)
        kcol = lax.broadcasted_iota(jnp.int32, (1, KB), 1)
        kc = kcol >> 6
        sub = lax.broadcasted_iota(jnp.int32, (8, 1), 0)
        f8 = jnp.broadcast_to(f * LOG2E, (8, wide))
        base = f8
        for r in range(1, 8):
            base = jnp.where(sub == r, pltpu.roll(f8, r, 1), base)
        for qh in range(QB // 8):
            rows = (pltpu.roll(base, 8 * qh, 1) if qh else base)[:, 0:KB]
            qc = (8 * qh) // CHUNK
            band = (kc >= qc) & (kc <= qc + N_LEFT)
            for t in range(3):
                o_ref[t, 0, 8 * qh:8 * qh + 8, :] = jnp.where(band & (kcol >= PADK - t * QB), rows, NEG_BIG)

    return pl.pallas_call(
        body, name="bias_table", grid=(HEADS,),
        out_shape=jax.ShapeDtypeStruct((3, HEADS, QB, KB), F32),
        in_specs=[pl.BlockSpec(memory_space=pltpu.SMEM)],
        out_specs=pl.BlockSpec((3, 1, QB, KB), lambda h: (0, h, 0, 0)),
        compiler_params=_params(1),
    )(rel_bias)


def _load_keys(p_hbm, kp, vp, sem):
    kp[0:PADK, :] = jnp.zeros((PADK, D_ATT), BF16)
    vp[0:PADK, :] = jnp.zeros((PADK, D_ATT), BF16)
    S = p_hbm.shape[0]
    ck = pltpu.make_async_copy(p_hbm.at[:, D_ATT:2 * D_ATT], kp.at[PADK:PADK + S, :], sem.at[0])
    cv = pltpu.make_async_copy(p_hbm.at[:, 2 * D_ATT:3 * D_ATT], vp.at[PADK:PADK + S, :], sem.at[1])
    ck.start()
    cv.start()
    ck.wait()
    cv.wait()


def _attn_fwd(P, bias_tab):
    S = P.shape[0]
    nb = S // QB

    def body(q_ref, p_hbm, bias_ref, o_ref, lse_ref, kp, vp, sem):
        g = pl.program_id(0)

        @pl.when(g == 0)
        def _():
            _load_keys(p_hbm, kp, vp, sem)

        start = pl.multiple_of(g * QB, QB)
        lane = lax.broadcasted_iota(jnp.int32, (1, 128), 1)
        for p in range(HEADS // 2):
            cols = slice(128 * p, 128 * (p + 1))
            qp = q_ref[:, cols] * SCALE
            kpair = kp[pl.ds(start, KB), cols]
            vpair = vp[pl.ds(start, KB), cols]
            outs = []
            for e in range(2):
                h = 2 * p + e
                lm = (lane < 64) if e == 0 else (lane >= 64)
                qm = jnp.where(lm, qp, jnp.zeros_like(qp))
                s = _dot_nt(qm, kpair) * LOG2E + bias_ref[0, h]
                mx = jnp.max(s, axis=-1, keepdims=True)
                ex = jnp.exp2(s - mx)
                sm = jnp.sum(ex, axis=-1, keepdims=True)
                outs.append(_dot(ex.astype(BF16), vpair) * (1.0 / sm))
                lse_ref[:, h:h + 1] = mx + jnp.log2(sm)
            o_ref[:, cols] = jnp.where(lane < 64, outs[0], outs[1]).astype(BF16)

    return pl.pallas_call(
        body, name="attn_fwd", grid=(nb,),
        out_shape=[jax.ShapeDtypeStruct((S, D_ATT), BF16), jax.ShapeDtypeStruct((S, HEADS), F32)],
        in_specs=[pl.BlockSpec((QB, D_ATT), lambda g: (g, 0)), ANY,
                  pl.BlockSpec((1, HEADS, QB, KB), lambda g: (jnp.minimum(g, 2), 0, 0, 0))],
        out_specs=[pl.BlockSpec((QB, D_ATT), lambda g: (g, 0)),
                   pl.BlockSpec((QB, HEADS), lambda g: (g, 0))],
        scratch_shapes=[pltpu.VMEM((S + PADK, D_ATT), BF16), pltpu.VMEM((S + PADK, D_ATT), BF16),
                        pltpu.SemaphoreType.DMA((2,))],
        compiler_params=_params(1),
    )(P, P, bias_tab)


def _token_local(x, tgt, P, att, w_att_out, w_conv_out, w_out, conv_w, conv_b, final_g):
    S = x.shape[0]
    ts = 256
    nt = S // ts
    hb = 16

    def body(x_ref, t_ref, s1_ref, s2_ref, s3_ref, h1_ref, h2_ref, att_ref,
             wao_ref, wco_ref, wo_ref, cw_ref, cb_ref, g2_ref,
             dx2_ref, dg_ref, datt_ref, dwo_ref, dwao_ref, dwco_ref, sm1_ref, sm2_ref, carry):
        i = pl.program_id(0)
        t = nt - 1 - i

        @pl.when(i == 0)
        def _():
            dwo_ref[...] = jnp.zeros_like(dwo_ref)
            dwao_ref[...] = jnp.zeros_like(dwao_ref)
            dwco_ref[...] = jnp.zeros_like(dwco_ref)
            sm1_ref[...] = jnp.zeros_like(sm1_ref)
            sm2_ref[...] = jnp.zeros_like(sm2_ref)
            carry[...] = jnp.zeros_like(carry)

        za = s1_ref[:, 0:512].astype(F32)
        gb = s1_ref[:, 512:1024].astype(F32)
        gc = s1_ref[:, 1024:1536].astype(F32)
        u = s2_ref[:, 0:512].astype(F32)
        zc = s2_ref[:, 512:1024].astype(F32)
        ga = jnp.concatenate([s2_ref[:, 1024:1536], s3_ref[:, 0:512]], axis=1).astype(F32)
        gv = s3_ref[:, 512:1536].astype(F32)
        att = att_ref[...].astype(F32)
        row = lax.broadcasted_iota(jnp.int32, (ts, 1), 0)

        sa = _sigmoid(za)
        silu_a = za * sa
        att_g = (att * silu_a).astype(BF16)
        y_att = _dot(att_g, wao_ref[...])

        cu = gc * u
        keep = jnp.where(t > 0, 1.0, 0.0).astype(F32)
        hcu = (h1_ref[:, 1024:1536].astype(F32) * h2_ref[:, 0:512].astype(F32)) * keep
        cu_m1 = jnp.where(row == 0, hcu[hb - 1:hb, :], pltpu.roll(cu, 1, 0))
        cu_m2 = jnp.where(row == 0, hcu[hb - 2:hb - 1, :],
                          jnp.where(row == 1, hcu[hb - 1:hb, :], pltpu.roll(cu, 2, 0)))
        w0, w1, w2 = cw_ref[0:1, :], cw_ref[1:2, :], cw_ref[2:3, :]
        vconv = w0 * cu_m2 + w1 * cu_m1 + w2 * cu + cb_ref[...]
        sc = _sigmoid(zc)
        silu_c = zc * sc
        cg = (gb * vconv * silu_c).astype(BF16)
        y_conv = _dot(cg, wco_ref[...])

        sga = _sigmoid(ga)
        sgv = _sigmoid(gv)
        m = (sga * y_att + sgv * y_conv).astype(BF16)
        x2 = x_ref[...] + _dot(m, wo_ref[...])
        r2 = lax.rsqrt(jnp.mean(x2 * x2, axis=-1, keepdims=True) + EPS)
        xn2 = x2 * r2
        g2 = g2_ref[...]
        err = xn2 * g2 - t_ref[...]
        sm1_ref[1:2, :] += jnp.sum(err * err, axis=0, keepdims=True) * (0.5 / D_MODEL)

        dy = err * (1.0 / D_MODEL)
        sm1_ref[0:1, :] += jnp.sum(dy * xn2, axis=0, keepdims=True)
        dxn = dy * g2
        dx2 = r2 * (dxn - xn2 * jnp.mean(dxn * xn2, axis=-1, keepdims=True))
        dx2_ref[...] = dx2
        dx2b = dx2.astype(BF16)
        dwo_ref[...] += _dot_tn(m, dx2b)
        dm = _dot_nt(dx2b, wo_ref[...])
        dya = (dm * sga).astype(BF16)
        dyc = (dm * sgv).astype(BF16)
        dg_ref[:, 2560:3584] = (dm * y_att * (sga * (1.0 - sga))).astype(BF16)
        dg_ref[:, 3584:4608] = (dm * y_conv * (sgv * (1.0 - sgv))).astype(BF16)
        dwao_ref[...] += _dot_tn(att_g, dya)
        dwco_ref[...] += _dot_tn(cg, dyc)
        datt_g = _dot_nt(dya, wao_ref[...])
        dcg = _dot_nt(dyc, wco_ref[...])
        datt_ref[...] = (datt_g * silu_a).astype(BF16)
        dg_ref[:, 0:512] = (datt_g * att * (sa * (1.0 + za * (1.0 - sa)))).astype(BF16)
        dg_ref[:, 512:1024] = (dcg * vconv * silu_c).astype(BF16)
        dg_ref[:, 2048:2560] = (dcg * gb * vconv * (sc * (1.0 + zc * (1.0 - sc)))).astype(BF16)
        dv = dcg * gb * silu_c
        sm2_ref[3:4, :] += jnp.sum(dv, axis=0, keepdims=True)
        sm2_ref[0:1, :] += jnp.sum(dv * cu_m2, axis=0, keepdims=True)
        sm2_ref[1:2, :] += jnp.sum(dv * cu_m1, axis=0, keepdims=True)
        sm2_ref[2:3, :] += jnp.sum(dv * cu, axis=0, keepdims=True)
        nxt = carry[...]
        dv_p1 = jnp.where(row == ts - 1, nxt[0:1, :], pltpu.roll(dv, ts - 1, 0))
        dv_p2 = jnp.where(row == ts - 1, nxt[1:2, :],
                          jnp.where(row == ts - 2, nxt[0:1, :], pltpu.roll(dv, ts - 2, 0)))
        dcu = w2 * dv + w1 * dv_p1 + w0 * dv_p2
        carry[...] = dv[0:8, :]
        dg_ref[:, 1024:1536] = (dcu * u).astype(BF16)
        dg_ref[:, 1536:2048] = (dcu * gc).astype(BF16)

    tile = lambda w: pl.BlockSpec((ts, w), lambda i: (nt - 1 - i, 0))
    seg = lambda c: pl.BlockSpec((ts, 1536), lambda i: (nt - 1 - i, c))
    halo = lambda c: pl.BlockSpec((hb, 1536), lambda i: (jnp.maximum((nt - 1 - i) * (ts // hb) - 1, 0), c))
    full = lambda a: pl.BlockSpec(a.shape, lambda i: (0,) * a.ndim)
    acc = lambda r, c: pl.BlockSpec((r, c), lambda i: (0, 0))
    return pl.pallas_call(
        body, name="token_local", grid=(nt,),
        out_shape=[jax.ShapeDtypeStruct((S, D_MODEL), F32), jax.ShapeDtypeStruct((S, GATE_COLS), BF16),
                   jax.ShapeDtypeStruct((S, D_ATT), BF16), jax.ShapeDtypeStruct((D_MODEL, D_MODEL), F32),
                   jax.ShapeDtypeStruct((D_ATT, D_MODEL), F32), jax.ShapeDtypeStruct((D_CONV, D_MODEL), F32),
                   jax.ShapeDtypeStruct((8, D_MODEL), F32), jax.ShapeDtypeStruct((8, D_CONV), F32)],
        in_specs=[tile(D_MODEL), tile(D_MODEL), seg(1), seg(2), seg(3), halo(1), halo(2), tile(D_ATT),
                  full(w_att_out), full(w_conv_out), full(w_out), full(conv_w), full(conv_b), full(final_g)],
        out_specs=[tile(D_MODEL), tile(GATE_COLS), tile(D_ATT), acc(D_MODEL, D_MODEL), acc(D_ATT, D_MODEL),
                   acc(D_CONV, D_MODEL), acc(8, D_MODEL), acc(8, D_CONV)],
        scratch_shapes=[pltpu.VMEM((8, D_CONV), F32)],
        compiler_params=_params(1),
    )(x, tgt, P, P, P, P, P, att, w_att_out, w_conv_out, w_out, conv_w, conv_b, final_g)


def _attn_bwd(P, att, datt, lse, bias_tab):
    S = P.shape[0]
    nb = S // QB

    def body(q_ref, att_ref, datt_ref, lse_ref, p_hbm, bias_ref, out_ref, db_ref,
             kp, vp, dq_ring, dk_ring, dv_ring, sem):
        g = pl.program_id(0)

        @pl.when(g == 0)
        def _():
            _load_keys(p_hbm, kp, vp, sem)
            db_ref[...] = jnp.zeros_like(db_ref)
            dk_ring[...] = jnp.zeros_like(dk_ring)
            dv_ring[...] = jnp.zeros_like(dv_ring)

        s_new = g % 3
        s_mid = (g + 2) % 3
        s_old = (g + 1) % 3

        @pl.when(g < nb)
        def _():
            start = pl.multiple_of(g * QB, QB)
            lane = lax.broadcasted_iota(jnp.int32, (1, 128), 1)
            for p in range(HEADS // 2):
                cols = slice(128 * p, 128 * (p + 1))
                qp = q_ref[:, cols] * SCALE
                op = att_ref[:, cols].astype(F32)
                dop = datt_ref[:, cols]
                kpair = kp[pl.ds(start, KB), cols]
                vpair = vp[pl.ds(start, KB), cols]
                dqs = []
                dk_acc = jnp.zeros((KB, 128), F32)
                dv_acc = jnp.zeros((KB, 128), F32)
                for e in range(2):
                    h = 2 * p + e
                    lm = (lane < 64) if e == 0 else (lane >= 64)
                    qm = jnp.where(lm, qp, jnp.zeros_like(qp))
                    dom = jnp.where(lm, dop, jnp.zeros_like(dop))
                    s = _dot_nt(qm, kpair) * LOG2E + bias_ref[0, h]
                    pr = jnp.exp2(s - lse_ref[:, h:h + 1])
                    dp = _dot_nt(dom, vpair)
                    delta = jnp.sum(dom.astype(F32) * op, axis=-1, keepdims=True)
                    ds = pr * (dp - delta)
                    db_ref[h] += ds
                    dsb = ds.astype(BF16)
                    prb = pr.astype(BF16)
                    dqs.append(_dot(dsb, kpair) * SCALE)
                    dk_acc = dk_acc + _dot_tn(dsb, qm)
                    dv_acc = dv_acc + _dot_tn(prb, dom)
                dq_ring[s_new, :, cols] = jnp.where(lane < 64, dqs[0], dqs[1])
                dk_ring[s_old, :, cols] += dk_acc[0:QB]
                dk_ring[s_mid, :, cols] += dk_acc[QB:2 * QB]
                dk_ring[s_new, :, cols] = dk_acc[2 * QB:3 * QB]
                dv_ring[s_old, :, cols] += dv_acc[0:QB]
                dv_ring[s_mid, :, cols] += dv_acc[QB:2 * QB]
                dv_ring[s_new, :, cols] = dv_acc[2 * QB:3 * QB]

        @pl.when(g >= 2)
        def _():
            out_ref[:, 0:D_ATT] = dq_ring[s_old].astype(BF16)
            out_ref[:, D_ATT:2 * D_ATT] = dk_ring[s_old].astype(BF16)
            out_ref[:, 2 * D_ATT:3 * D_ATT] = dv_ring[s_old].astype(BF16)

    qblk = lambda w: pl.BlockSpec((QB, w), lambda g: (jnp.minimum(g, nb - 1), 0))
    return pl.pallas_call(
        body, name="attn_bwd", grid=(nb + 2,),
        out_shape=[jax.ShapeDtypeStruct((S, 3 * D_ATT), BF16), jax.ShapeDtypeStruct((HEADS, QB, KB), F32)],
        in_specs=[qblk(D_ATT), qblk(D_ATT), qblk(D_ATT), qblk(HEADS), ANY,
                  pl.BlockSpec((1, HEADS, QB, KB), lambda g: (jnp.minimum(g, 2), 0, 0, 0))],
        out_specs=[pl.BlockSpec((QB, 3 * D_ATT), lambda g: (jnp.maximum(g - 2, 0), 0)),
                   pl.BlockSpec((HEADS, QB, KB), lambda g: (0, 0, 0))],
        scratch_shapes=[pltpu.VMEM((S + PADK, D_ATT), BF16), pltpu.VMEM((S + PADK, D_ATT), BF16),
                        pltpu.VMEM((3, QB, D_ATT), F32), pltpu.VMEM((3, QB, D_ATT), F32),
                        pltpu.VMEM((3, QB, D_ATT), F32), pltpu.SemaphoreType.DMA((2,))],
        compiler_params=_params(1),
    )(P, att, datt, lse, P, bias_tab)


def _bias_fold(dscore):
    wide = 1024

    def body(d_ref, o_ref):
        sub = lax.broadcasted_iota(jnp.int32, (8, 1), 0)
        col = lax.broadcasted_iota(jnp.int32, (1, wide), 1)
        pad = jnp.zeros((8, wide - KB), F32)
        for h in range(HEADS):
            acc = jnp.concatenate([d_ref[h, 0:8, :], pad], axis=1)
            for qh in range(1, QB // 8):
                a = jnp.concatenate([d_ref[h, 8 * qh:8 * qh + 8, :], pad], axis=1)
                acc = acc + pltpu.roll(a, wide - 8 * qh, 1)
            for r in range(1, 8):
                acc = jnp.where(sub == r, pltpu.roll(acc, wide - r, 1), acc)
            vec = jnp.sum(acc, axis=0, keepdims=True)
            far = (col <= PADK - MAX_REL) | (col > KB)
            tail = jnp.sum(jnp.where(far, vec, 0.0), axis=-1, keepdims=True)
            o_ref[h:h + 1, :] = jnp.where(col == wide - 1, tail, vec)

    return pl.pallas_call(
        body, name="bias_fold",
        out_shape=jax.ShapeDtypeStruct((HEADS, wide), F32),
        compiler_params=pltpu.CompilerParams(vmem_limit_bytes=VMEM_LIMIT),
    )(dscore)


def _in_proj_bwd(x, norm_g, dx2, dqkv, dgate, w_in_g):
    S = x.shape[0]
    ts = 256

    def body(x_ref, g_ref, dx2_ref, dq_ref, dg_ref, w_ref, gx_ref, dn_ref):
        @pl.when(pl.program_id(0) == 0)
        def _():
            dn_ref[...] = jnp.zeros_like(dn_ref)

        dh = jnp.zeros((ts, D_MODEL), F32)
        for j in range(N_DEV):
            if j < 2:
                d = dq_ref[:, j * W_BLK:(j + 1) * W_BLK]
            else:
                d = dg_ref[:, (j - 2) * W_BLK:(j - 1) * W_BLK]
            dh = dh + _dot_nt(d, w_ref[j])
        xf = x_ref[...]
        r = lax.rsqrt(jnp.mean(xf * xf, axis=-1, keepdims=True) + EPS)
        xn = xf * r
        dn_ref[0:1, :] += jnp.sum(dh * xn, axis=0, keepdims=True)
        dhg = dh * g_ref[...]
        gx_ref[...] = dx2_ref[...] + r * (dhg - xn * jnp.mean(dhg * xn, axis=-1, keepdims=True))

    tile = lambda w: pl.BlockSpec((ts, w), lambda i: (i, 0))
    return pl.pallas_call(
        body, name="in_proj_bwd", grid=(S // ts,),
        out_shape=[jax.ShapeDtypeStruct((S, D_MODEL), F32), jax.ShapeDtypeStruct((8, D_MODEL), F32)],
        in_specs=[tile(D_MODEL), pl.BlockSpec((1, D_MODEL), lambda i: (0, 0)), tile(D_MODEL),
                  tile(3 * D_ATT), tile(GATE_COLS),
                  pl.BlockSpec((N_DEV, D_MODEL, W_BLK), lambda i: (0, 0, 0))],
        out_specs=[tile(D_MODEL), pl.BlockSpec((8, D_MODEL), lambda i: (0, 0))],
        compiler_params=_params(1),
    )(x, norm_g, dx2, dqkv, dgate, w_in_g)


PEER_MASKS = (1, 4, 2, 6, 5, 3, 7, 0)


def _w_in_grad_scatter(ht, dqkv, dgate, d_proj, d_wo, order):
    S = ht.shape[1]
    ts = 1024
    nt = S // ts
    n_steps = len(PEER_MASKS)

    def body(order_ref, ht_ref, dq_ref, dg_ref, proj_hbm, wo_hbm, rwin, rproj, rwo,
             acc, stage, send_sems, recv_sems, small_send, small_recv, local_sems):
        k, i = pl.program_id(0), pl.program_id(1)
        x, y, c = _mesh_pos()
        my = _flat((x, y, c))
        peers = [(x, y, 1 - c), (1 - x, y, c), (x, 1 - y, c), (1 - x, 1 - y, c),
                 (1 - x, y, 1 - c), (x, 1 - y, 1 - c), (1 - x, 1 - y, 1 - c)]
        small = ((proj_hbm, rproj), (wo_hbm, rwo))

        def small_copy(kk, a, receive=False):
            src, dst = small[a]
            slot = _flat(peers[kk]) if receive else my
            return pltpu.make_async_remote_copy(
                src_ref=src.at[_flat(peers[kk])], dst_ref=dst.at[slot],
                send_sem=small_send.at[kk, a], recv_sem=small_recv.at[kk, a],
                device_id=peers[kk], device_id_type=MESH)

        def win_copy(kk, receive=False):
            slot = _flat(peers[kk]) if receive else my
            return pltpu.make_async_remote_copy(
                src_ref=stage.at[kk % 2], dst_ref=rwin.at[slot],
                send_sem=send_sems.at[kk], recv_sem=recv_sems.at[kk],
                device_id=peers[kk], device_id_type=MESH)

        own_small = [pltpu.make_async_copy(src.at[my], dst.at[my], local_sems.at[a])
                     for a, (src, dst) in enumerate(small)]

        @pl.when((k == 0) & (i == 0))
        def _():
            for cp in own_small:
                cp.start()
            for kk in range(n_steps - 1):
                for a in range(2):
                    small_copy(kk, a).start()

        def accumulate(d_ref):
            prod = _dot(ht_ref[...], d_ref[...])

            @pl.when(i == 0)
            def _():
                acc[...] = prod

            @pl.when(i > 0)
            def _():
                acc[...] += prod

        @pl.when(order_ref[k] < 2)
        def _():
            accumulate(dq_ref)

        @pl.when(order_ref[k] >= 2)
        def _():
            accumulate(dg_ref)

        @pl.when(i == nt - 1)
        def _():
            for kk in range(n_steps):
                @pl.when(k == kk)
                def _():
                    if kk >= 2:
                        win_copy(kk - 2).wait_send()
                    stage[kk % 2] = acc[...].astype(BF16)
                    if kk < n_steps - 1:
                        win_copy(kk).start()
                    else:
                        own = pltpu.make_async_copy(stage.at[kk % 2], rwin.at[my], local_sems.at[2])
                        own.start()
                        win_copy(kk - 1).wait_send()
                        for q in range(n_steps - 1):
                            win_copy(q, receive=True).wait_recv()
                            for a in range(2):
                                small_copy(q, a).wait_send()
                                small_copy(q, a, receive=True).wait_recv()
                        for cp in own_small:
                            cp.wait()
                        own.wait()

    grid_spec = pltpu.PrefetchScalarGridSpec(
        num_scalar_prefetch=1, grid=(n_steps, nt),
        in_specs=[pl.BlockSpec((D_MODEL, ts), lambda k, i, o: (0, i)),
                  pl.BlockSpec((ts, W_BLK), lambda k, i, o: (i, jnp.minimum(o[k], 1))),
                  pl.BlockSpec((ts, W_BLK), lambda k, i, o: (i, jnp.maximum(o[k] - 2, 0))),
                  ANY, ANY],
        out_specs=[ANY, ANY, ANY],
        scratch_shapes=[pltpu.VMEM((D_MODEL, W_BLK), F32), pltpu.VMEM((2, D_MODEL, W_BLK), BF16),
                        pltpu.SemaphoreType.DMA((n_steps - 1,)), pltpu.SemaphoreType.DMA((n_steps - 1,)),
                        pltpu.SemaphoreType.DMA((n_steps - 1, 2)), pltpu.SemaphoreType.DMA((n_steps - 1, 2)),
                        pltpu.SemaphoreType.DMA((3,))])
    return pl.pallas_call(
        body, name="w_in_grad_scatter", grid_spec=grid_spec,
        out_shape=[jax.ShapeDtypeStruct((N_DEV, D_MODEL, W_BLK), BF16),
                   jax.ShapeDtypeStruct(d_proj.shape, BF16), jax.ShapeDtypeStruct(d_wo.shape, BF16)],
        compiler_params=_params(2),
    )(order, ht, dqkv, dgate, d_proj, d_wo)


def _adamw(w, g, m, v):
    m = ADAM_B1 * m + (1.0 - ADAM_B1) * g
    v = ADAM_B2 * v + (1.0 - ADAM_B2) * (g * g)
    m_hat = m / (1.0 - ADAM_B1 ** ADAM_STEP)
    v_hat = v / (1.0 - ADAM_B2 ** ADAM_STEP)
    delta = -ADAM_LR * (m_hat / (jnp.sqrt(v_hat) + ADAM_EPS) + ADAM_WD * w)
    return delta, m, v


def _sum_adamw(parts, w, m, v, name):
    R, C = w.shape
    tr = min(R, 256)

    def body(p_ref, w_ref, m_ref, v_ref, g_ref, d_ref, nm_ref, nv_ref):
        g = p_ref[0].astype(F32)
        for s in range(1, N_DEV):
            g = g + p_ref[s].astype(F32)
        g_ref[...] = g
        d_ref[...], nm_ref[...], nv_ref[...] = _adamw(w_ref[...], g, m_ref[...], v_ref[...])

    tile = pl.BlockSpec((tr, C), lambda i: (i, 0))
    return pl.pallas_call(
        body, name=name, grid=(R // tr,),
        out_shape=[jax.ShapeDtypeStruct((R, C), F32)] * 4,
        in_specs=[pl.BlockSpec((N_DEV, tr, C), lambda i: (0, i, 0)), tile, tile, tile],
        out_specs=[tile] * 4,
        compiler_params=_params(1),
    )(parts, w, m, v)


def _sum_parts(parts):
    def body(p_ref, o_ref):
        g = p_ref[0]
        for s in range(1, N_DEV):
            g = g + p_ref[s]
        o_ref[...] = g

    return pl.pallas_call(body, name="sum_small",
                          out_shape=jax.ShapeDtypeStruct(parts.shape[1:], F32))(parts)


def _adamw_small(w, g, m, v):
    def body(w_ref, g_ref, m_ref, v_ref, d_ref, nm_ref, nv_ref):
        d_ref[...], nm_ref[...], nv_ref[...] = _adamw(w_ref[...], g_ref[...], m_ref[...], v_ref[...])

    return pl.pallas_call(body, name="adamw_small",
                          out_shape=[jax.ShapeDtypeStruct(w.shape, F32)] * 3)(w, g, m, v)


def _pad_row(a, width=D_MODEL):
    a = a.reshape(-1, a.shape[-1])
    return jnp.pad(a, ((0, 0), (0, width - a.shape[-1])))


def _owner_cols(a, blk):
    r = a.shape[0]
    return a.reshape(r, N_DEV, blk).transpose(1, 0, 2)


def kernel(x, norm_g, w_in, rel_bias, w_att_out, conv_w, conv_b, w_conv_out, w_out, final_norm_g, loss_target, m_norm_g, m_w_in, m_rel_bias, m_w_att_out, m_conv_w, m_conv_b, m_w_conv_out, m_w_out, m_final_norm_g, v_norm_g, v_w_in, v_rel_bias, v_w_att_out, v_conv_w, v_conv_b, v_w_conv_out, v_w_out, v_final_norm_g):
    S = x.shape[1]
    x2d = x.reshape(S, D_MODEL)
    tgt = loss_target.reshape(S, D_MODEL)
    me = 4 * lax.axis_index("x") + 2 * lax.axis_index("y") + lax.axis_index("c")

    proj_sh = jnp.concatenate([w_att_out[0], w_conv_out[0]], axis=1).astype(BF16)
    cw_sh = jnp.pad(conv_w[0], ((0, 5), (0, 64)))
    P, ht, w_in_g, proj_g, w_out_g, cw_g = _gather_in_proj(
        x2d, norm_g, w_in[0].astype(BF16), [proj_sh, w_out[0].astype(BF16), cw_sh],
        me ^ jnp.array(GATHER_MASKS, jnp.int32))
    wao = proj_g[:, :, 0:128].transpose(1, 0, 2).reshape(D_ATT, D_MODEL)
    wco = proj_g[:, :, 128:256].transpose(1, 0, 2).reshape(D_CONV, D_MODEL)
    wo = w_out_g.reshape(D_MODEL, D_MODEL)
    cw = cw_g[:, 0:3, 0:64].transpose(1, 0, 2).reshape(3, D_CONV)

    bias_tab = _bias_table(rel_bias[0])
    att, lse = _attn_fwd(P, bias_tab)
    dx2, dgate, datt, d_wo, d_wao, d_wco, sm1, sm2 = _token_local(
        x2d, tgt, P, att, wao, wco, wo, cw, conv_b, final_norm_g.reshape(1, D_MODEL))
    dqkv, dscore = _attn_bwd(P, att, datt, lse, bias_tab)
    dbias = _bias_fold(dscore)
    grad_x, dnorm = _in_proj_bwd(x2d, norm_g, dx2, dqkv, dgate, w_in_g)

    d_proj = jnp.concatenate([_owner_cols(d_wao, 128), _owner_cols(d_wco, 128)], axis=2).astype(BF16)
    order = me ^ jnp.array(PEER_MASKS, jnp.int32)
    r_win, r_proj, r_wo = _w_in_grad_scatter(
        ht, dqkv, dgate, d_proj, d_wo.reshape(N_DEV, 128, D_MODEL).astype(BF16), order)
    small = jnp.concatenate([dnorm[0:1], sm1[0:2], _pad_row(sm2[0:4]), jnp.zeros((1, D_MODEL), F32), dbias],
                            axis=0)
    (small_g,) = _all_gather([small], "gather_small")
    tot = _sum_parts(small_g)

    g_win, d_win_, nm_win, nv_win = _sum_adamw(r_win, w_in[0], m_w_in[0], v_w_in[0], "adamw_w_in")
    w_proj = jnp.concatenate([w_att_out[0], w_conv_out[0]], axis=1)
    m_proj = jnp.concatenate([m_w_att_out[0], m_w_conv_out[0]], axis=1)
    v_proj = jnp.concatenate([v_w_att_out[0], v_w_conv_out[0]], axis=1)
    g_proj, d_proj_, nm_proj, nv_proj = _sum_adamw(r_proj, w_proj, m_proj, v_proj, "adamw_proj")
    g_wo, d_wo_, nm_wo, nv_wo = _sum_adamw(r_wo, w_out[0], m_w_out[0], v_w_out[0], "adamw_w_out")

    loss = jnp.sum(tot[2])
    g_norm, g_final = tot[0:1], tot[1]
    g_cw = lax.dynamic_slice(tot[3:6, 0:D_CONV], (0, me * 64), (3, 64))
    g_cb = tot[6:7, 0:D_CONV]
    vec = tot[8:16]
    g_rel = jnp.concatenate([jnp.zeros((HEADS, MAX_REL - CHUNK + 1), F32),
                             vec[:, PADK - MAX_REL + 1:PADK + CHUNK][:, ::-1], vec[:, 1023:1024]], axis=1)

    def pack(norm, final, cwv, cbv, rel):
        return jnp.concatenate([norm, final.reshape(1, D_MODEL), _pad_row(cwv), _pad_row(cbv),
                                jnp.zeros((2, D_MODEL), F32), _pad_row(rel)], axis=0)

    w_s = pack(norm_g, final_norm_g, conv_w[0], conv_b, rel_bias[0])
    g_s = pack(g_norm, g_final, g_cw, g_cb, g_rel)
    m_s = pack(m_norm_g, m_final_norm_g, m_conv_w[0], m_conv_b, m_rel_bias[0])
    v_s = pack(v_norm_g, v_final_norm_g, v_conv_w[0], v_conv_b, v_rel_bias[0])
    d_s, nm_s, nv_s = _adamw_small(w_s, g_s, m_s, v_s)

    def unpack(p):
        return (p[0:1], p[1], p[2:5, 0:64][None], p[5:6, 0:D_CONV], p[8:16, 0:N_REL][None])

    def sharded(a_in, a_proj, a_wo):
        return a_in[None], a_proj[:, 0:128][None], a_proj[:, 128:256][None], a_wo[None]

    outs = []
    for small_pack, big in ((g_s, (g_win, g_proj, g_wo)), (d_s, (d_win_, d_proj_, d_wo_)),
                            (nm_s, (nm_win, nm_proj, nm_wo)), (nv_s, (nv_win, nv_proj, nv_wo))):
        norm, final, cwv, cbv, rel = unpack(small_pack)
        b_in, b_ao, b_co, b_wo = sharded(*big)
        outs += [norm, b_in, rel, b_ao, cwv, cbv, b_co, b_wo, final]
    return (loss, grad_x.reshape(1, S, D_MODEL), *outs)
```

```python
import functools

import numpy as np
import jax
import jax.numpy as jnp
from jax import lax
from jax.experimental import pallas as pl
from jax.experimental.pallas import tpu as pltpu

F32 = jnp.float32
BF16 = jnp.bfloat16

D_MODEL = 1024
CHUNK = 64
N_LEFT = 8
HEADS = 8
D_ATT = 512
D_CONV = 512
MAX_REL = 128
N_REL = 2 * MAX_REL + 1
IN_COLS = 6144
EPS = 1e-6
NEG_BIG = -1e30
N_DEV = 8
W_BLK = IN_COLS // N_DEV
QB = 4 * CHUNK
KB = QB + N_LEFT * CHUNK
PADK = N_LEFT * CHUNK
SCALE = 64 ** -0.5
LOG2E = 1.4426950408889634
GATE_COLS = IN_COLS - 3 * D_ATT

ADAM_LR = 0.001
ADAM_B1 = 0.9
ADAM_B2 = 0.999
ADAM_EPS = 1e-08
ADAM_WD = 0.01
ADAM_STEP = 10

VMEM_LIMIT = 56 * 1024 * 1024

MESH = pl.DeviceIdType.MESH
ANY = pl.BlockSpec(memory_space=pl.ANY)


def _params(n_grid):
    return pltpu.CompilerParams(dimension_semantics=("arbitrary",) * n_grid,
                                vmem_limit_bytes=VMEM_LIMIT)


def _dot(a, b):
    return jnp.dot(a, b, preferred_element_type=F32)


def _dot_nt(a, b):
    return lax.dot_general(a, b, (((1,), (1,)), ((), ())), preferred_element_type=F32)


def _dot_tn(a, b):
    return lax.dot_general(a, b, (((0,), (0,)), ((), ())), preferred_element_type=F32)


def _sigmoid(z):
    return 1.0 / (1.0 + jnp.exp(-z))


def _mesh_pos():
    return lax.axis_index("x"), lax.axis_index("y"), lax.axis_index("c")


def _flat(p):
    return 4 * p[0] + 2 * p[1] + p[2]


def _by_core(masks):
    m0, m1 = (jnp.array(m, jnp.int32) for m in masks)
    return jnp.where(lax.axis_index("c") == 0, m0, m1)


def _all_gather(shards, name):
    n = len(shards)

    def body(*refs):
        ins, outs = refs[:n], refs[n:2 * n]
        send_sems, recv_sems, local_sems = refs[2 * n:]
        x, y, c = _mesh_pos()
        me, sibling = (x, y, c), (x, y, 1 - c)
        chips = [(1 - x, y), (x, 1 - y), (1 - x, 1 - y)]

        def copy(k, a, block, to, from_input=False):
            rows = outs[a].at[_flat(block)]
            return pltpu.make_async_remote_copy(
                src_ref=ins[a] if from_input else rows, dst_ref=rows,
                send_sem=send_sems.at[k, a], recv_sem=recv_sems.at[k, a],
                device_id=to, device_id_type=MESH)

        mine = [pltpu.make_async_copy(ins[a], outs[a].at[_flat(me)], local_sems.at[a]) for a in range(n)]
        for cp in mine:
            cp.start()
        first = [copy(0, a, me, sibling, True) for a in range(n)]
        for j, chip in enumerate(chips):
            first += [copy(1 + j, a, me, (*chip, c), True) for a in range(n)]
        for cp in first:
            cp.start()
        passed = []
        for j, chip in enumerate(chips):
            for a in range(n):
                copy(1 + j, a, (*chip, c), me).wait_recv()
            fwd = [copy(4 + j, a, (*chip, c), sibling) for a in range(n)]
            for cp in fwd:
                cp.start()
            passed += fwd
        for a in range(n):
            copy(0, a, sibling, me).wait_recv()
        for j, chip in enumerate(chips):
            for a in range(n):
                copy(4 + j, a, (*chip, 1 - c), me).wait_recv()
        for cp in first + passed:
            cp.wait_send()
        for cp in mine:
            cp.wait()

    return pl.pallas_call(
        body, name=name,
        out_shape=[jax.ShapeDtypeStruct((N_DEV,) + s.shape, s.dtype) for s in shards],
        in_specs=[ANY] * n, out_specs=[ANY] * n,
        scratch_shapes=[pltpu.SemaphoreType.DMA((7, n)), pltpu.SemaphoreType.DMA((7, n)),
                        pltpu.SemaphoreType.DMA((n,))],
    )(*shards)


GATHER_MASKS = ((0, 1, 4, 3, 2, 5, 6, 7), (0, 1, 2, 5, 4, 3, 6, 7))


def _gather_in_proj(x, norm_g, w_sh, smalls, order):
    S = x.shape[0]
    ts = 512
    nt = S // ts
    n_small = len(smalls)
    n_steps = N_DEV

    def body(order_ref, x_ref, g_ref, w_hbm, *rest):
        small_in = rest[:n_small]
        p_ref, ht_ref, wg_hbm = rest[n_small:n_small + 3]
        small_out = rest[n_small + 3:2 * n_small + 3]
        (wbuf, hbuf, own_sem, send_sems, recv_sems, out_sems,
         small_send, small_recv, small_local) = rest[2 * n_small + 3:]
        k, i = pl.program_id(0), pl.program_id(1)
        x_, y_, c_ = _mesh_pos()
        me, sibling = (x_, y_, c_), (x_, y_, 1 - c_)
        my = _flat(me)
        chips = [(x_ ^ (1 - c_), y_ ^ c_), (x_ ^ c_, y_ ^ (1 - c_)), (1 - x_, 1 - y_)]
        peers = [sibling] + [(*chip, c_) for chip in chips] + [(*chip, 1 - c_) for chip in chips]

        def wcopy(sem, block, to, from_input=False):
            dst = wbuf.at[_flat(block)]
            return pltpu.make_async_remote_copy(
                src_ref=w_hbm if from_input else dst, dst_ref=dst,
                send_sem=send_sems.at[sem], recv_sem=recv_sems.at[sem], device_id=to, device_id_type=MESH)

        def small_copy(q, a, receive=False):
            slot = _flat(peers[q]) if receive else my
            return pltpu.make_async_remote_copy(
                src_ref=small_in[a], dst_ref=small_out[a].at[slot],
                send_sem=small_send.at[q, a], recv_sem=small_recv.at[q, a],
                device_id=peers[q], device_id_type=MESH)

        def keep(step, block):
            return pltpu.make_async_copy(wbuf.at[_flat(block)], wg_hbm.at[_flat(block)], out_sems.at[step])

        own = pltpu.make_async_copy(w_hbm, wbuf.at[my], own_sem)
        small_own = [pltpu.make_async_copy(small_in[a], small_out[a].at[my], small_local.at[a])
                     for a in range(n_small)]
        passed_on = [(*chips[1], 1 - c_), (*chips[0], 1 - c_), (*chips[2], 1 - c_)]
        arrivals = [me, sibling]
        for j in range(3):
            arrivals += [(*chips[j], c_), passed_on[j]]

        @pl.when(i == 0)
        def _():
            for kk in range(n_steps):
                @pl.when(k == kk)
                def _():
                    j = kk // 2 - 1
                    if kk == 0:
                        own.start()
                        wcopy(0, me, sibling, True).start()
                        wcopy(1, me, (*chips[0], c_), True).start()
                        own.wait()
                    elif kk == 1:
                        wcopy(0, sibling, me).wait_recv()
                        wcopy(2, me, (*chips[1], c_), True).start()
                    elif kk % 2 == 0:
                        wcopy(1 + j, (*chips[j], c_), me).wait_recv()
                        wcopy(4 + j, (*chips[j], c_), sibling).start()
                        if kk == 2:
                            wcopy(3, me, (*chips[2], c_), True).start()
                    else:
                        wcopy(4 + j, passed_on[j], me).wait_recv()
                        if kk == 3:
                            for cp in small_own:
                                cp.start()
                            for q in range(len(peers)):
                                for a in range(n_small):
                                    small_copy(q, a).start()
                    keep(kk, arrivals[kk]).start()

        row0 = pl.multiple_of(i * ts, ts)

        @pl.when(k == 0)
        def _():
            xf = x_ref[...]
            r = lax.rsqrt(jnp.mean(xf * xf, axis=-1, keepdims=True) + EPS)
            hf = (xf * r) * g_ref[...]
            hbuf[pl.ds(row0, ts), :] = hf.astype(BF16)
            ht_ref[...] = hf.T.astype(BF16)

        p_ref[...] = _dot(hbuf[pl.ds(row0, ts), :], wbuf[order_ref[k]]).astype(BF16)

        @pl.when((k == n_steps - 1) & (i == nt - 1))
        def _():
            wcopy(0, me, sibling, True).wait_send()
            for j, chip in enumerate(chips):
                wcopy(1 + j, me, (*chip, c_), True).wait_send()
                wcopy(4 + j, (*chip, c_), sibling).wait_send()
            for kk in range(n_steps):
                keep(kk, arrivals[kk]).wait()
            for cp in small_own:
                cp.wait()
            for q in range(len(peers)):
                for a in range(n_small):
                    small_copy(q, a).wait_send()
                    small_copy(q, a, receive=True).wait_recv()

    first_pass = lambda k, i: jnp.where(k == 0, i, nt - 1)
    grid_spec = pltpu.PrefetchScalarGridSpec(
        num_scalar_prefetch=1, grid=(n_steps, nt),
        in_specs=[pl.BlockSpec((ts, D_MODEL), lambda k, i, o: (first_pass(k, i), 0)),
                  pl.BlockSpec((1, D_MODEL), lambda k, i, o: (0, 0)), ANY] + [ANY] * n_small,
        out_specs=[pl.BlockSpec((ts, W_BLK), lambda k, i, o: (i, o[k])),
                   pl.BlockSpec((D_MODEL, ts), lambda k, i, o: (0, first_pass(k, i))), ANY] + [ANY] * n_small,
        scratch_shapes=[pltpu.VMEM((N_DEV, D_MODEL, W_BLK), BF16), pltpu.VMEM((S, D_MODEL), BF16),
                        pltpu.SemaphoreType.DMA, pltpu.SemaphoreType.DMA((7,)), pltpu.SemaphoreType.DMA((7,)),
                        pltpu.SemaphoreType.DMA((n_steps,)),
                        pltpu.SemaphoreType.DMA((7, n_small)), pltpu.SemaphoreType.DMA((7, n_small)),
                        pltpu.SemaphoreType.DMA((n_small,))])
    return pl.pallas_call(
        body, name="gather_in_proj", grid_spec=grid_spec,
        out_shape=[jax.ShapeDtypeStruct((S, IN_COLS), BF16), jax.ShapeDtypeStruct((D_MODEL, S), BF16),
                   jax.ShapeDtypeStruct((N_DEV,) + w_sh.shape, BF16)]
        + [jax.ShapeDtypeStruct((N_DEV,) + s.shape, s.dtype) for s in smalls],
        compiler_params=_params(2),
    )(order, x, norm_g, w_sh, *smalls)


def _bias_table(rel_bias):
    wide = 1024

    def body(r_ref, o_ref):
        h = pl.program_id(0)
        col = lax.broadcasted_iota(jnp.int32, (1, wide), 1)
        k_minus_q = jnp.where(col < KB, col, col - wide)
        idx = jnp.clip(PADK - k_minus_q, -MAX_REL, MAX_REL) + MAX_REL
        f = jnp.zeros((1, wide), F32)
        for r in range(MAX_REL - CHUNK + 1, N_REL):
            f = jnp.where(idx == r, r_ref[h, r], f)
        kcol = lax.broadcasted_iota(jnp.int32, (1, KB), 1)
        kc = kcol >> 6
        sub = lax.broadcasted_iota(jnp.int32, (8, 1), 0)
        f8 = jnp.broadcast_to(f * LOG2E, (8, wide))
        base = f8
        for r in range(1, 8):
            base = jnp.where(sub == r, pltpu.roll(f8, r, 1), base)
        for qh in range(QB // 8):
            rows = (pltpu.roll(base, 8 * qh, 1) if qh else base)[:, 0:KB]
            qc = (8 * qh) // CHUNK
            band = (kc >= qc) & (kc <= qc + N_LEFT)
            for t in range(3):
                o_ref[t, 0, 8 * qh:8 * qh + 8, :] = jnp.where(band & (kcol >= PADK - t * QB), rows, NEG_BIG)

    return pl.pallas_call(
        body, name="bias_table", grid=(HEADS,),
        out_shape=jax.ShapeDtypeStruct((3, HEADS, QB, KB), F32),
        in_specs=[pl.BlockSpec(memory_space=pltpu.SMEM)],
        out_specs=pl.BlockSpec((3, 1, QB, KB), lambda h: (0, h, 0, 0)),
        compiler_params=_params(1),
    )(rel_bias)


def _load_keys(p_hbm, kp, vp, sem):
    kp[0:PADK, :] = jnp.zeros((PADK, D_ATT), BF16)
    vp[0:PADK, :] = jnp.zeros((PADK, D_ATT), BF16)
    S = p_hbm.shape[0]
    ck = pltpu.make_async_copy(p_hbm.at[:, D_ATT:2 * D_ATT], kp.at[PADK:PADK + S, :], sem.at[0])
    cv = pltpu.make_async_copy(p_hbm.at[:, 2 * D_ATT:3 * D_ATT], vp.at[PADK:PADK + S, :], sem.at[1])
    ck.start()
    cv.start()
    ck.wait()
    cv.wait()


def _attn_fwd(P, bias_tab):
    S = P.shape[0]
    nb = S // QB

    def body(q_ref, p_hbm, bias_ref, o_ref, lse_ref, kp, vp, sem):
        g = pl.program_id(0)

        @pl.when(g == 0)
        def _():
            _load_keys(p_hbm, kp, vp, sem)

        start = pl.multiple_of(g * QB, QB)
        lane = lax.broadcasted_iota(jnp.int32, (1, 128), 1)
        for p in range(HEADS // 2):
            cols = slice(128 * p, 128 * (p + 1))
            qp = q_ref[:, cols] * SCALE
            kpair = kp[pl.ds(start, KB), cols]
            vpair = vp[pl.ds(start, KB), cols]
            outs = []
            for e in range(2):
                h = 2 * p + e
                lm = (lane < 64) if e == 0 else (lane >= 64)
                qm = jnp.where(lm, qp, jnp.zeros_like(qp))
                s = _dot_nt(qm, kpair) * LOG2E + bias_ref[0, h]
                mx = jnp.max(s, axis=-1, keepdims=True)
                ex = jnp.exp2(s - mx)
                sm = jnp.sum(ex, axis=-1, keepdims=True)
                outs.append(_dot(ex.astype(BF16), vpair) * (1.0 / sm))
                lse_ref[:, h:h + 1] = mx + jnp.log2(sm)
            o_ref[:, cols] = jnp.where(lane < 64, outs[0], outs[1]).astype(BF16)

    return pl.pallas_call(
        body, name="attn_fwd", grid=(nb,),
        out_shape=[jax.ShapeDtypeStruct((S, D_ATT), BF16), jax.ShapeDtypeStruct((S, HEADS), F32)],
        in_specs=[pl.BlockSpec((QB, D_ATT), lambda g: (g, 0)), ANY,
                  pl.BlockSpec((1, HEADS, QB, KB), lambda g: (jnp.minimum(g, 2), 0, 0, 0))],
        out_specs=[pl.BlockSpec((QB, D_ATT), lambda g: (g, 0)),
                   pl.BlockSpec((QB, HEADS), lambda g: (g, 0))],
        scratch_shapes=[pltpu.VMEM((S + PADK, D_ATT), BF16), pltpu.VMEM((S + PADK, D_ATT), BF16),
                        pltpu.SemaphoreType.DMA((2,))],
        compiler_params=_params(1),
    )(P, P, bias_tab)


def _token_local(x, tgt, P, att, w_att_out, w_conv_out, w_out, conv_w, conv_b, final_g):
    S = x.shape[0]
    ts = 256
    nt = S // ts
    hb = 16

    def body(x_ref, t_ref, s1_ref, s2_ref, s3_ref, h1_ref, h2_ref, att_ref,
             wao_ref, wco_ref, wo_ref, cw_ref, cb_ref, g2_ref,
             dx2_ref, dg_ref, datt_ref, dwo_ref, dwao_ref, dwco_ref, sm1_ref, sm2_ref, carry):
        i = pl.program_id(0)
        t = nt - 1 - i

        @pl.when(i == 0)
        def _():
            dwo_ref[...] = jnp.zeros_like(dwo_ref)
            dwao_ref[...] = jnp.zeros_like(dwao_ref)
            dwco_ref[...] = jnp.zeros_like(dwco_ref)
            sm1_ref[...] = jnp.zeros_like(sm1_ref)
            sm2_ref[...] = jnp.zeros_like(sm2_ref)
            carry[...] = jnp.zeros_like(carry)

        za = s1_ref[:, 0:512].astype(F32)
        gb = s1_ref[:, 512:1024].astype(F32)
        gc = s1_ref[:, 1024:1536].astype(F32)
        u = s2_ref[:, 0:512].astype(F32)
        zc = s2_ref[:, 512:1024].astype(F32)
        ga = jnp.concatenate([s2_ref[:, 1024:1536], s3_ref[:, 0:512]], axis=1).astype(F32)
        gv = s3_ref[:, 512:1536].astype(F32)
        att = att_ref[...].astype(F32)
        row = lax.broadcasted_iota(jnp.int32, (ts, 1), 0)

        sa = _sigmoid(za)
        silu_a = za * sa
        att_g = (att * silu_a).astype(BF16)
        y_att = _dot(att_g, wao_ref[...])

        cu = gc * u
        keep = jnp.where(t > 0, 1.0, 0.0).astype(F32)
        hcu = (h1_ref[:, 1024:1536].astype(F32) * h2_ref[:, 0:512].astype(F32)) * keep
        cu_m1 = jnp.where(row == 0, hcu[hb - 1:hb, :], pltpu.roll(cu, 1, 0))
        cu_m2 = jnp.where(row == 0, hcu[hb - 2:hb - 1, :],
                          jnp.where(row == 1, hcu[hb - 1:hb, :], pltpu.roll(cu, 2, 0)))
        w0, w1, w2 = cw_ref[0:1, :], cw_ref[1:2, :], cw_ref[2:3, :]
        vconv = w0 * cu_m2 + w1 * cu_m1 + w2 * cu + cb_ref[...]
        sc = _sigmoid(zc)
        silu_c = zc * sc
        cg = (gb * vconv * silu_c).astype(BF16)
        y_conv = _dot(cg, wco_ref[...])

        sga = _sigmoid(ga)
        sgv = _sigmoid(gv)
        m = (sga * y_att + sgv * y_conv).astype(BF16)
        x2 = x_ref[...] + _dot(m, wo_ref[...])
        r2 = lax.rsqrt(jnp.mean(x2 * x2, axis=-1, keepdims=True) + EPS)
        xn2 = x2 * r2
        g2 = g2_ref[...]
        err = xn2 * g2 - t_ref[...]
        sm1_ref[1:2, :] += jnp.sum(err * err, axis=0, keepdims=True) * (0.5 / D_MODEL)

        dy = err * (1.0 / D_MODEL)
        sm1_ref[0:1, :] += jnp.sum(dy * xn2, axis=0, keepdims=True)
        dxn = dy * g2
        dx2 = r2 * (dxn - xn2 * jnp.mean(dxn * xn2, axis=-1, keepdims=True))
        dx2_ref[...] = dx2
        dx2b = dx2.astype(BF16)
        dwo_ref[...] += _dot_tn(m, dx2b)
        dm = _dot_nt(dx2b, wo_ref[...])
        dya = (dm * sga).astype(BF16)
        dyc = (dm * sgv).astype(BF16)
        dg_ref[:, 2560:3584] = (dm * y_att * (sga * (1.0 - sga))).astype(BF16)
        dg_ref[:, 3584:4608] = (dm * y_conv * (sgv * (1.0 - sgv))).astype(BF16)
        dwao_ref[...] += _dot_tn(att_g, dya)
        dwco_ref[...] += _dot_tn(cg, dyc)
        datt_g = _dot_nt(dya, wao_ref[...])
        dcg = _dot_nt(dyc, wco_ref[...])
        datt_ref[...] = (datt_g * silu_a).astype(BF16)
        dg_ref[:, 0:512] = (datt_g * att * (sa * (1.0 + za * (1.0 - sa)))).astype(BF16)
        dg_ref[:, 512:1024] = (dcg * vconv * silu_c).astype(BF16)
        dg_ref[:, 2048:2560] = (dcg * gb * vconv * (sc * (1.0 + zc * (1.0 - sc)))).astype(BF16)
        dv = dcg * gb * silu_c
        sm2_ref[3:4, :] += jnp.sum(dv, axis=0, keepdims=True)
        sm2_ref[0:1, :] += jnp.sum(dv * cu_m2, axis=0, keepdims=True)
        sm2_ref[1:2, :] += jnp.sum(dv * cu_m1, axis=0, keepdims=True)
        sm2_ref[2:3, :] += jnp.sum(dv * cu, axis=0, keepdims=True)
        nxt = carry[...]
        dv_p1 = jnp.where(row == ts - 1, nxt[0:1, :], pltpu.roll(dv, ts - 1, 0))
        dv_p2 = jnp.where(row == ts - 1, nxt[1:2, :],
                          jnp.where(row == ts - 2, nxt[0:1, :], pltpu.roll(dv, ts - 2, 0)))
        dcu = w2 * dv + w1 * dv_p1 + w0 * dv_p2
        carry[...] = dv[0:8, :]
        dg_ref[:, 1024:1536] = (dcu * u).astype(BF16)
        dg_ref[:, 1536:2048] = (dcu * gc).astype(BF16)

    tile = lambda w: pl.BlockSpec((ts, w), lambda i: (nt - 1 - i, 0))
    seg = lambda c: pl.BlockSpec((ts, 1536), lambda i: (nt - 1 - i, c))
    halo = lambda c: pl.BlockSpec((hb, 1536), lambda i: (jnp.maximum((nt - 1 - i) * (ts // hb) - 1, 0), c))
    full = lambda a: pl.BlockSpec(a.shape, lambda i: (0,) * a.ndim)
    acc = lambda r, c: pl.BlockSpec((r, c), lambda i: (0, 0))
    return pl.pallas_call(
        body, name="token_local", grid=(nt,),
        out_shape=[jax.ShapeDtypeStruct((S, D_MODEL), F32), jax.ShapeDtypeStruct((S, GATE_COLS), BF16),
                   jax.ShapeDtypeStruct((S, D_ATT), BF16), jax.ShapeDtypeStruct((D_MODEL, D_MODEL), F32),
                   jax.ShapeDtypeStruct((D_ATT, D_MODEL), F32), jax.ShapeDtypeStruct((D_CONV, D_MODEL), F32),
                   jax.ShapeDtypeStruct((8, D_MODEL), F32), jax.ShapeDtypeStruct((8, D_CONV), F32)],
        in_specs=[tile(D_MODEL), tile(D_MODEL), seg(1), seg(2), seg(3), halo(1), halo(2), tile(D_ATT),
                  full(w_att_out), full(w_conv_out), full(w_out), full(conv_w), full(conv_b), full(final_g)],
        out_specs=[tile(D_MODEL), tile(GATE_COLS), tile(D_ATT), acc(D_MODEL, D_MODEL), acc(D_ATT, D_MODEL),
                   acc(D_CONV, D_MODEL), acc(8, D_MODEL), acc(8, D_CONV)],
        scratch_shapes=[pltpu.VMEM((8, D_CONV), F32)],
        compiler_params=_params(1),
    )(x, tgt, P, P, P, P, P, att, w_att_out, w_conv_out, w_out, conv_w, conv_b, final_g)


def _attn_bwd(P, att, datt, lse, bias_tab):
    S = P.shape[0]
    nb = S // QB

    def body(q_ref, att_ref, datt_ref, lse_ref, p_hbm, bias_ref, out_ref, db_ref,
             kp, vp, dq_ring, dk_ring, dv_ring, sem):
        g = pl.program_id(0)

        @pl.when(g == 0)
        def _():
            _load_keys(p_hbm, kp, vp, sem)
            db_ref[...] = jnp.zeros_like(db_ref)
            dk_ring[...] = jnp.zeros_like(dk_ring)
            dv_ring[...] = jnp.zeros_like(dv_ring)

        s_new = g % 3
        s_mid = (g + 2) % 3
        s_old = (g + 1) % 3

        @pl.when(g < nb)
        def _():
            start = pl.multiple_of(g * QB, QB)
            lane = lax.broadcasted_iota(jnp.int32, (1, 128), 1)
            for p in range(HEADS // 2):
                cols = slice(128 * p, 128 * (p + 1))
                qp = q_ref[:, cols] * SCALE
                op = att_ref[:, cols].astype(F32)
                dop = datt_ref[:, cols]
                kpair = kp[pl.ds(start, KB), cols]
                vpair = vp[pl.ds(start, KB), cols]
                dqs = []
                dk_acc = jnp.zeros((KB, 128), F32)
                dv_acc = jnp.zeros((KB, 128), F32)
                for e in range(2):
                    h = 2 * p + e
                    lm = (lane < 64) if e == 0 else (lane >= 64)
                    qm = jnp.where(lm, qp, jnp.zeros_like(qp))
                    dom = jnp.where(lm, dop, jnp.zeros_like(dop))
                    s = _dot_nt(qm, kpair) * LOG2E + bias_ref[0, h]
                    pr = jnp.exp2(s - lse_ref[:, h:h + 1])
                    dp = _dot_nt(dom, vpair)
                    delta = jnp.sum(dom.astype(F32) * op, axis=-1, keepdims=True)
                    ds = pr * (dp - delta)
                    db_ref[h] += ds
                    dsb = ds.astype(BF16)
                    prb = pr.astype(BF16)
                    dqs.append(_dot(dsb, kpair) * SCALE)
                    dk_acc = dk_acc + _dot_tn(dsb, qm)
                    dv_acc = dv_acc + _dot_tn(prb, dom)
                dq_ring[s_new, :, cols] = jnp.where(lane < 64, dqs[0], dqs[1])
                dk_ring[s_old, :, cols] += dk_acc[0:QB]
                dk_ring[s_mid, :, cols] += dk_acc[QB:2 * QB]
                dk_ring[s_new, :, cols] = dk_acc[2 * QB:3 * QB]
                dv_ring[s_old, :, cols] += dv_acc[0:QB]
                dv_ring[s_mid, :, cols] += dv_acc[QB:2 * QB]
                dv_ring[s_new, :, cols] = dv_acc[2 * QB:3 * QB]

        @pl.when(g >= 2)
        def _():
            out_ref[:, 0:D_ATT] = dq_ring[s_old].astype(BF16)
            out_ref[:, D_ATT:2 * D_ATT] = dk_ring[s_old].astype(BF16)
            out_ref[:, 2 * D_ATT:3 * D_ATT] = dv_ring[s_old].astype(BF16)

    qblk = lambda w: pl.BlockSpec((QB, w), lambda g: (jnp.minimum(g, nb - 1), 0))
    return pl.pallas_call(
        body, name="attn_bwd", grid=(nb + 2,),
        out_shape=[jax.ShapeDtypeStruct((S, 3 * D_ATT), BF16), jax.ShapeDtypeStruct((HEADS, QB, KB), F32)],
        in_specs=[qblk(D_ATT), qblk(D_ATT), qblk(D_ATT), qblk(HEADS), ANY,
                  pl.BlockSpec((1, HEADS, QB, KB), lambda g: (jnp.minimum(g, 2), 0, 0, 0))],
        out_specs=[pl.BlockSpec((QB, 3 * D_ATT), lambda g: (jnp.maximum(g - 2, 0), 0)),
                   pl.BlockSpec((HEADS, QB, KB), lambda g: (0, 0, 0))],
        scratch_shapes=[pltpu.VMEM((S + PADK, D_ATT), BF16), pltpu.VMEM((S + PADK, D_ATT), BF16),
                        pltpu.VMEM((3, QB, D_ATT), F32), pltpu.VMEM((3, QB, D_ATT), F32),
                        pltpu.VMEM((3, QB, D_ATT), F32), pltpu.SemaphoreType.DMA((2,))],
        compiler_params=_params(1),
    )(P, att, datt, lse, P, bias_tab)


def _bias_fold(dscore):
    wide = 1024

    def body(d_ref, o_ref):
        sub = lax.broadcasted_iota(jnp.int32, (8, 1), 0)
        col = lax.broadcasted_iota(jnp.int32, (1, wide), 1)
        pad = jnp.zeros((8, wide - KB), F32)
        for h in range(HEADS):
            acc = jnp.concatenate([d_ref[h, 0:8, :], pad], axis=1)
            for qh in range(1, QB // 8):
                a = jnp.concatenate([d_ref[h, 8 * qh:8 * qh + 8, :], pad], axis=1)
                acc = acc + pltpu.roll(a, wide - 8 * qh, 1)
            for r in range(1, 8):
                acc = jnp.where(sub == r, pltpu.roll(acc, wide - r, 1), acc)
            vec = jnp.sum(acc, axis=0, keepdims=True)
            far = (col <= PADK - MAX_REL) | (col > KB)
            tail = jnp.sum(jnp.where(far, vec, 0.0), axis=-1, keepdims=True)
            o_ref[h:h + 1, :] = jnp.where(col == wide - 1, tail, vec)

    return pl.pallas_call(
        body, name="bias_fold",
        out_shape=jax.ShapeDtypeStruct((HEADS, wide), F32),
        compiler_params=pltpu.CompilerParams(vmem_limit_bytes=VMEM_LIMIT),
    )(dscore)


def _in_proj_bwd(x, norm_g, dx2, dqkv, dgate, w_in_g):
    S = x.shape[0]
    ts = 256

    def body(x_ref, g_ref, dx2_ref, dq_ref, dg_ref, w_ref, gx_ref, dn_ref):
        @pl.when(pl.program_id(0) == 0)
        def _():
            dn_ref[...] = jnp.zeros_like(dn_ref)

        dh = jnp.zeros((ts, D_MODEL), F32)
        for j in range(N_DEV):
            if j < 2:
                d = dq_ref[:, j * W_BLK:(j + 1) * W_BLK]
            else:
                d = dg_ref[:, (j - 2) * W_BLK:(j - 1) * W_BLK]
            dh = dh + _dot_nt(d, w_ref[j])
        xf = x_ref[...]
        r = lax.rsqrt(jnp.mean(xf * xf, axis=-1, keepdims=True) + EPS)
        xn = xf * r
        dn_ref[0:1, :] += jnp.sum(dh * xn, axis=0, keepdims=True)
        dhg = dh * g_ref[...]
        gx_ref[...] = dx2_ref[...] + r * (dhg - xn * jnp.mean(dhg * xn, axis=-1, keepdims=True))

    tile = lambda w: pl.BlockSpec((ts, w), lambda i: (i, 0))
    return pl.pallas_call(
        body, name="in_proj_bwd", grid=(S // ts,),
        out_shape=[jax.ShapeDtypeStruct((S, D_MODEL), F32), jax.ShapeDtypeStruct((8, D_MODEL), F32)],
        in_specs=[tile(D_MODEL), pl.BlockSpec((1, D_MODEL), lambda i: (0, 0)), tile(D_MODEL),
                  tile(3 * D_ATT), tile(GATE_COLS),
                  pl.BlockSpec((N_DEV, D_MODEL, W_BLK), lambda i: (0, 0, 0))],
        out_specs=[tile(D_MODEL), pl.BlockSpec((8, D_MODEL), lambda i: (0, 0))],
        compiler_params=_params(1),
    )(x, norm_g, dx2, dqkv, dgate, w_in_g)


SCATTER_MASKS = ((3, 4, 5, 2, 7, 6, 1, 0), (5, 2, 3, 4, 7, 6, 1, 0))


def _w_in_grad_scatter(ht, dqkv, dgate, d_proj, d_wo, order):
    S = ht.shape[1]
    ts = 1024
    nt = S // ts
    n_steps = 8

    def body(order_ref, ht_ref, dq_ref, dg_ref, proj_hbm, wo_hbm, g_ref, rproj, rwo,
             acc, stage, rsib, rici, d2d_send, d2d_recv, ici_send, ici_recv, small_send, small_recv, local_sems):
        k, i = pl.program_id(0), pl.program_id(1)
        x, y, c = _mesh_pos()
        my = _flat((x, y, c))
        sibling = (x, y, 1 - c)
        owners = [(x ^ (1 - c), y ^ c, c), (x ^ c, y ^ (1 - c), c), (1 - x, 1 - y, c)]
        peers = [sibling, (1 - x, y, c), (x, 1 - y, c), (1 - x, 1 - y, c),
                 (1 - x, y, 1 - c), (x, 1 - y, 1 - c), (1 - x, 1 - y, 1 - c)]
        small = ((proj_hbm, rproj), (wo_hbm, rwo))

        def small_copy(kk, a, receive=False):
            src, dst = small[a]
            slot = _flat(peers[kk]) if receive else my
            return pltpu.make_async_remote_copy(
                src_ref=src.at[_flat(peers[kk])], dst_ref=dst.at[slot],
                send_sem=small_send.at[kk, a], recv_sem=small_recv.at[kk, a],
                device_id=peers[kk], device_id_type=MESH)

        def to_sibling(t):
            return pltpu.make_async_remote_copy(
                src_ref=stage.at[0], dst_ref=rsib.at[t], send_sem=d2d_send.at[t], recv_sem=d2d_recv.at[t],
                device_id=sibling, device_id_type=MESH)

        def to_owner(t):
            return pltpu.make_async_remote_copy(
                src_ref=stage.at[1], dst_ref=rici.at[t], send_sem=ici_send.at[t], recv_sem=ici_recv.at[t],
                device_id=owners[t], device_id_type=MESH)

        own_small = [pltpu.make_async_copy(src.at[my], dst.at[my], local_sems.at[a])
                     for a, (src, dst) in enumerate(small)]

        @pl.when((k == 0) & (i == 0))
        def _():
            for cp in own_small:
                cp.start()
            for kk in range(len(peers)):
                for a in range(2):
                    small_copy(kk, a).start()

        def accumulate(d_ref):
            prod = _dot(ht_ref[...], d_ref[...])

            @pl.when(i == 0)
            def _():
                acc[...] = prod

            @pl.when(i > 0)
            def _():
                acc[...] += prod

        @pl.when(order_ref[k] < 2)
        def _():
            accumulate(dq_ref)

        @pl.when(order_ref[k] >= 2)
        def _():
            accumulate(dg_ref)

        @pl.when(i == nt - 1)
        def _():
            for s in range(n_steps):
                @pl.when(k == s)
                def _():
                    t = s // 2
                    if s % 2 == 0:
                        if t >= 1:
                            to_sibling(t - 1).wait_send()
                        stage[0] = acc[...].astype(BF16)
                        to_sibling(t).start()
                    elif t < 3:
                        if t >= 1:
                            to_owner(t - 1).wait_send()
                        to_sibling(t).wait_recv()
                        stage[1] = (acc[...] + rsib[t].astype(F32)).astype(BF16)
                        to_owner(t).start()
                    else:
                        to_sibling(t).wait_recv()
                        total = acc[...] + rsib[t].astype(F32)
                        for j in range(3):
                            to_owner(j).wait_recv()
                            total = total + rici[j].astype(F32)
                        g_ref[...] = total
                        to_owner(2).wait_send()
                        to_sibling(3).wait_send()
                        for q in range(len(peers)):
                            for a in range(2):
                                small_copy(q, a).wait_send()
                                small_copy(q, a, receive=True).wait_recv()
                        for cp in own_small:
                            cp.wait()

    blk = (D_MODEL, W_BLK)
    grid_spec = pltpu.PrefetchScalarGridSpec(
        num_scalar_prefetch=1, grid=(n_steps, nt),
        in_specs=[pl.BlockSpec((D_MODEL, ts), lambda k, i, o: (0, i)),
                  pl.BlockSpec((ts, W_BLK), lambda k, i, o: (i, jnp.minimum(o[k], 1))),
                  pl.BlockSpec((ts, W_BLK), lambda k, i, o: (i, jnp.maximum(o[k] - 2, 0))),
                  ANY, ANY],
        out_specs=[pl.BlockSpec(blk, lambda k, i, o: (0, 0)), ANY, ANY],
        scratch_shapes=[pltpu.VMEM(blk, F32), pltpu.VMEM((2,) + blk, BF16),
                        pltpu.VMEM((4,) + blk, BF16), pltpu.VMEM((3,) + blk, BF16),
                        pltpu.SemaphoreType.DMA((4,)), pltpu.SemaphoreType.DMA((4,)),
                        pltpu.SemaphoreType.DMA((3,)), pltpu.SemaphoreType.DMA((3,)),
                        pltpu.SemaphoreType.DMA((7, 2)), pltpu.SemaphoreType.DMA((7, 2)),
                        pltpu.SemaphoreType.DMA((2,))])
    return pl.pallas_call(
        body, name="w_in_grad_scatter", grid_spec=grid_spec,
        out_shape=[jax.ShapeDtypeStruct(blk, F32),
                   jax.ShapeDtypeStruct(d_proj.shape, BF16), jax.ShapeDtypeStruct(d_wo.shape, BF16)],
        compiler_params=_params(2),
    )(order, ht, dqkv, dgate, d_proj, d_wo)


def _adamw(w, g, m, v):
    m = ADAM_B1 * m + (1.0 - ADAM_B1) * g
    v = ADAM_B2 * v + (1.0 - ADAM_B2) * (g * g)
    m_hat = m / (1.0 - ADAM_B1 ** ADAM_STEP)
    v_hat = v / (1.0 - ADAM_B2 ** ADAM_STEP)
    delta = -ADAM_LR * (m_hat / (jnp.sqrt(v_hat) + ADAM_EPS) + ADAM_WD * w)
    return delta, m, v


def _sum_adamw(parts, w, m, v, name):
    R, C = w.shape
    n = parts.shape[0]
    tr = min(R, 256)

    def body(p_ref, w_ref, m_ref, v_ref, g_ref, d_ref, nm_ref, nv_ref):
        g = p_ref[0].astype(F32)
        for s in range(1, n):
            g = g + p_ref[s].astype(F32)
        g_ref[...] = g
        d_ref[...], nm_ref[...], nv_ref[...] = _adamw(w_ref[...], g, m_ref[...], v_ref[...])

    tile = pl.BlockSpec((tr, C), lambda i: (i, 0))
    return pl.pallas_call(
        body, name=name, grid=(R // tr,),
        out_shape=[jax.ShapeDtypeStruct((R, C), F32)] * 4,
        in_specs=[pl.BlockSpec((n, tr, C), lambda i: (0, i, 0)), tile, tile, tile],
        out_specs=[tile] * 4,
        compiler_params=_params(1),
    )(parts, w, m, v)


def _sum_parts(parts):
    def body(p_ref, o_ref):
        g = p_ref[0]
        for s in range(1, N_DEV):
            g = g + p_ref[s]
        o_ref[...] = g

    return pl.pallas_call(body, name="sum_small",
                          out_shape=jax.ShapeDtypeStruct(parts.shape[1:], F32))(parts)


def _adamw_small(w, g, m, v):
    def body(w_ref, g_ref, m_ref, v_ref, d_ref, nm_ref, nv_ref):
        d_ref[...], nm_ref[...], nv_ref[...] = _adamw(w_ref[...], g_ref[...], m_ref[...], v_ref[...])

    return pl.pallas_call(body, name="adamw_small",
                          out_shape=[jax.ShapeDtypeStruct(w.shape, F32)] * 3)(w, g, m, v)


def _pad_row(a, width=D_MODEL):
    a = a.reshape(-1, a.shape[-1])
    return jnp.pad(a, ((0, 0), (0, width - a.shape[-1])))


def _owner_cols(a, blk):
    r = a.shape[0]
    return a.reshape(r, N_DEV, blk).transpose(1, 0, 2)


def kernel(x, norm_g, w_in, rel_bias, w_att_out, conv_w, conv_b, w_conv_out, w_out, final_norm_g, loss_target, m_norm_g, m_w_in, m_rel_bias, m_w_att_out, m_conv_w, m_conv_b, m_w_conv_out, m_w_out, m_final_norm_g, v_norm_g, v_w_in, v_rel_bias, v_w_att_out, v_conv_w, v_conv_b, v_w_conv_out, v_w_out, v_final_norm_g):
    S = x.shape[1]
    x2d = x.reshape(S, D_MODEL)
    tgt = loss_target.reshape(S, D_MODEL)
    me = 4 * lax.axis_index("x") + 2 * lax.axis_index("y") + lax.axis_index("c")

    proj_sh = jnp.concatenate([w_att_out[0], w_conv_out[0]], axis=1).astype(BF16)
    cw_sh = jnp.pad(conv_w[0], ((0, 5), (0, 64)))
    P, ht, w_in_g, proj_g, w_out_g, cw_g = _gather_in_proj(
        x2d, norm_g, w_in[0].astype(BF16), [proj_sh, w_out[0].astype(BF16), cw_sh],
        me ^ _by_core(GATHER_MASKS))
    wao = proj_g[:, :, 0:128].transpose(1, 0, 2).reshape(D_ATT, D_MODEL)
    wco = proj_g[:, :, 128:256].transpose(1, 0, 2).reshape(D_CONV, D_MODEL)
    wo = w_out_g.reshape(D_MODEL, D_MODEL)
    cw = cw_g[:, 0:3, 0:64].transpose(1, 0, 2).reshape(3, D_CONV)

    bias_tab = _bias_table(rel_bias[0])
    att, lse = _attn_fwd(P, bias_tab)
    dx2, dgate, datt, d_wo, d_wao, d_wco, sm1, sm2 = _token_local(
        x2d, tgt, P, att, wao, wco, wo, cw, conv_b, final_norm_g.reshape(1, D_MODEL))
    dqkv, dscore = _attn_bwd(P, att, datt, lse, bias_tab)
    dbias = _bias_fold(dscore)
    grad_x, dnorm = _in_proj_bwd(x2d, norm_g, dx2, dqkv, dgate, w_in_g)

    d_proj = jnp.concatenate([_owner_cols(d_wao, 128), _owner_cols(d_wco, 128)], axis=2).astype(BF16)
    order = me ^ _by_core(SCATTER_MASKS)
    g_win_sum, r_proj, r_wo = _w_in_grad_scatter(
        ht, dqkv, dgate, d_proj, d_wo.reshape(N_DEV, 128, D_MODEL).astype(BF16), order)
    small = jnp.concatenate([dnorm[0:1], sm1[0:2], _pad_row(sm2[0:4]), jnp.zeros((1, D_MODEL), F32), dbias],
                            axis=0)
    (small_g,) = _all_gather([small], "gather_small")
    tot = _sum_parts(small_g)

    g_win, d_win_, nm_win, nv_win = _sum_adamw(g_win_sum[None], w_in[0], m_w_in[0], v_w_in[0], "adamw_w_in")
    w_proj = jnp.concatenate([w_att_out[0], w_conv_out[0]], axis=1)
    m_proj = jnp.concatenate([m_w_att_out[0], m_w_conv_out[0]], axis=1)
    v_proj = jnp.concatenate([v_w_att_out[0], v_w_conv_out[0]], axis=1)
    g_proj, d_proj_, nm_proj, nv_proj = _sum_adamw(r_proj, w_proj, m_proj, v_proj, "adamw_proj")
    g_wo, d_wo_, nm_wo, nv_wo = _sum_adamw(r_wo, w_out[0], m_w_out[0], v_w_out[0], "adamw_w_out")

    loss = jnp.sum(tot[2])
    g_norm, g_final = tot[0:1], tot[1]
    g_cw = lax.dynamic_slice(tot[3:6, 0:D_CONV], (0, me * 64), (3, 64))
    g_cb = tot[6:7, 0:D_CONV]
    vec = tot[8:16]
    g_rel = jnp.concatenate([jnp.zeros((HEADS, MAX_REL - CHUNK + 1), F32),
                             vec[:, PADK - MAX_REL + 1:PADK + CHUNK][:, ::-1], vec[:, 1023:1024]], axis=1)

    def pack(norm, final, cwv, cbv, rel):
        return jnp.concatenate([norm, final.reshape(1, D_MODEL), _pad_row(cwv), _pad_row(cbv),
                                jnp.zeros((2, D_MODEL), F32), _pad_row(rel)], axis=0)

    w_s = pack(norm_g, final_norm_g, conv_w[0], conv_b, rel_bias[0])
    g_s = pack(g_norm, g_final, g_cw, g_cb, g_rel)
    m_s = pack(m_norm_g, m_final_norm_g, m_conv_w[0], m_conv_b, m_rel_bias[0])
    v_s = pack(v_norm_g, v_final_norm_g, v_conv_w[0], v_conv_b, v_rel_bias[0])
    d_s, nm_s, nv_s = _adamw_small(w_s, g_s, m_s, v_s)

    def unpack(p):
        return (p[0:1], p[1], p[2:5, 0:64][None], p[5:6, 0:D_CONV], p[8:16, 0:N_REL][None])

    def sharded(a_in, a_proj, a_wo):
        return a_in[None], a_proj[:, 0:128][None], a_proj[:, 128:256][None], a_wo[None]

    outs = []
    for small_pack, big in ((g_s, (g_win, g_proj, g_wo)), (d_s, (d_win_, d_proj_, d_wo_)),
                            (nm_s, (nm_win, nm_proj, nm_wo)), (nv_s, (nv_win, nv_proj, nv_wo))):
        norm, final, cwv, cbv, rel = unpack(small_pack)
        b_in, b_ao, b_co, b_wo = sharded(*big)
        outs += [norm, b_in, rel, b_ao, cwv, cbv, b_co, b_wo, final]
    return (loss, grad_x.reshape(1, S, D_MODEL), *outs)
```

```python
import functools

import numpy as np
import jax
import jax.numpy as jnp
from jax import lax
from jax.experimental import pallas as pl
from jax.experimental.pallas import tpu as pltpu

F32 = jnp.float32
BF16 = jnp.bfloat16

D_MODEL = 1024
CHUNK = 64
N_LEFT = 8
HEADS = 8
D_ATT = 512
D_CONV = 512
MAX_REL = 128
N_REL = 2 * MAX_REL + 1
IN_COLS = 6144
EPS = 1e-6
NEG_BIG = -1e30
N_DEV = 8
W_BLK = IN_COLS // N_DEV
QB = 4 * CHUNK
KB = QB + N_LEFT * CHUNK
PADK = N_LEFT * CHUNK
SCALE = 64 ** -0.5
LOG2E = 1.4426950408889634
GATE_COLS = IN_COLS - 3 * D_ATT

ADAM_LR = 0.001
ADAM_B1 = 0.9
ADAM_B2 = 0.999
ADAM_EPS = 1e-08
ADAM_WD = 0.01
ADAM_STEP = 10

VMEM_LIMIT = 56 * 1024 * 1024

MESH = pl.DeviceIdType.MESH
ANY = pl.BlockSpec(memory_space=pl.ANY)


def _params(n_grid):
    return pltpu.CompilerParams(dimension_semantics=("arbitrary",) * n_grid,
                                vmem_limit_bytes=VMEM_LIMIT)


def _dot(a, b):
    return jnp.dot(a, b, preferred_element_type=F32)


def _dot_nt(a, b):
    return lax.dot_general(a, b, (((1,), (1,)), ((), ())), preferred_element_type=F32)


def _dot_tn(a, b):
    return lax.dot_general(a, b, (((0,), (0,)), ((), ())), preferred_element_type=F32)


def _sigmoid(z):
    return 1.0 / (1.0 + jnp.exp(-z))


def _mesh_pos():
    return lax.axis_index("x"), lax.axis_index("y"), lax.axis_index("c")


def _flat(p):
    return 4 * p[0] + 2 * p[1] + p[2]


def _by_core(masks):
    m0, m1 = (jnp.array(m, jnp.int32) for m in masks)
    return jnp.where(lax.axis_index("c") == 0, m0, m1)


GATHER_MASKS = ((0, 1, 4, 3, 2, 5, 6, 7), (0, 1, 2, 5, 4, 3, 6, 7))


def _gather_in_proj(x, norm_g, w_sh, smalls, order):
    S = x.shape[0]
    ts = 1024
    nt = S // ts
    n_small = len(smalls)
    n_steps = N_DEV

    def body(order_ref, x_ref, g_ref, w_hbm, *rest):
        small_in = rest[:n_small]
        p_ref, ht_ref, wg_hbm = rest[n_small:n_small + 3]
        small_out = rest[n_small + 3:2 * n_small + 3]
        (wbuf, hbuf, own_sem, send_sems, recv_sems, out_sems,
         small_send, small_recv, small_local) = rest[2 * n_small + 3:]
        k, i = pl.program_id(0), pl.program_id(1)
        x_, y_, c_ = _mesh_pos()
        me, sibling = (x_, y_, c_), (x_, y_, 1 - c_)
        my = _flat(me)
        chips = [(x_ ^ (1 - c_), y_ ^ c_), (x_ ^ c_, y_ ^ (1 - c_)), (1 - x_, 1 - y_)]
        peers = [sibling] + [(*chip, c_) for chip in chips] + [(*chip, 1 - c_) for chip in chips]

        def wcopy(sem, block, to, from_input=False):
            dst = wbuf.at[_flat(block)]
            return pltpu.make_async_remote_copy(
                src_ref=w_hbm if from_input else dst, dst_ref=dst,
                send_sem=send_sems.at[sem], recv_sem=recv_sems.at[sem], device_id=to, device_id_type=MESH)

        def small_copy(q, a, receive=False):
            slot = _flat(peers[q]) if receive else my
            return pltpu.make_async_remote_copy(
                src_ref=small_in[a], dst_ref=small_out[a].at[slot],
                send_sem=small_send.at[q, a], recv_sem=small_recv.at[q, a],
                device_id=peers[q], device_id_type=MESH)

        def keep(step, block):
            return pltpu.make_async_copy(wbuf.at[_flat(block)], wg_hbm.at[_flat(block)], out_sems.at[step])

        own = pltpu.make_async_copy(w_hbm, wbuf.at[my], own_sem)
        small_own = [pltpu.make_async_copy(small_in[a], small_out[a].at[my], small_local.at[a])
                     for a in range(n_small)]
        passed_on = [(*chips[1], 1 - c_), (*chips[0], 1 - c_), (*chips[2], 1 - c_)]
        arrivals = [me, sibling]
        for j in range(3):
            arrivals += [(*chips[j], c_), passed_on[j]]

        @pl.when(i == 0)
        def _():
            for kk in range(n_steps):
                @pl.when(k == kk)
                def _():
                    j = kk // 2 - 1
                    if kk == 0:
                        own.start()
                        wcopy(0, me, sibling, True).start()
                        wcopy(1, me, (*chips[0], c_), True).start()
                        own.wait()
                    elif kk == 1:
                        wcopy(0, sibling, me).wait_recv()
                        wcopy(2, me, (*chips[1], c_), True).start()
                    elif kk % 2 == 0:
                        wcopy(1 + j, (*chips[j], c_), me).wait_recv()
                        wcopy(4 + j, (*chips[j], c_), sibling).start()
                        if kk == 2:
                            wcopy(3, me, (*chips[2], c_), True).start()
                    else:
                        wcopy(4 + j, passed_on[j], me).wait_recv()
                        if kk == 3:
                            for cp in small_own:
                                cp.start()
                            for q in range(len(peers)):
                                for a in range(n_small):
                                    small_copy(q, a).start()
                    keep(kk, arrivals[kk]).start()

        row0 = pl.multiple_of(i * ts, ts)

        @pl.when(k == 0)
        def _():
            xf = x_ref[...]
            r = lax.rsqrt(jnp.mean(xf * xf, axis=-1, keepdims=True) + EPS)
            hf = (xf * r) * g_ref[...]
            hbuf[pl.ds(row0, ts), :] = hf.astype(BF16)
            ht_ref[...] = hf.T.astype(BF16)

        p_ref[...] = _dot(hbuf[pl.ds(row0, ts), :], wbuf[order_ref[k]]).astype(BF16)

        @pl.when((k == n_steps - 1) & (i == nt - 1))
        def _():
            wcopy(0, me, sibling, True).wait_send()
            for j, chip in enumerate(chips):
                wcopy(1 + j, me, (*chip, c_), True).wait_send()
                wcopy(4 + j, (*chip, c_), sibling).wait_send()
            for kk in range(n_steps):
                keep(kk, arrivals[kk]).wait()
            for cp in small_own:
                cp.wait()
            for q in range(len(peers)):
                for a in range(n_small):
                    small_copy(q, a).wait_send()
                    small_copy(q, a, receive=True).wait_recv()

    first_pass = lambda k, i: jnp.where(k == 0, i, nt - 1)
    grid_spec = pltpu.PrefetchScalarGridSpec(
        num_scalar_prefetch=1, grid=(n_steps, nt),
        in_specs=[pl.BlockSpec((ts, D_MODEL), lambda k, i, o: (first_pass(k, i), 0)),
                  pl.BlockSpec((1, D_MODEL), lambda k, i, o: (0, 0)), ANY] + [ANY] * n_small,
        out_specs=[pl.BlockSpec((ts, W_BLK), lambda k, i, o: (i, o[k])),
                   pl.BlockSpec((D_MODEL, ts), lambda k, i, o: (0, first_pass(k, i))), ANY] + [ANY] * n_small,
        scratch_shapes=[pltpu.VMEM((N_DEV, D_MODEL, W_BLK), BF16), pltpu.VMEM((S, D_MODEL), BF16),
                        pltpu.SemaphoreType.DMA, pltpu.SemaphoreType.DMA((7,)), pltpu.SemaphoreType.DMA((7,)),
                        pltpu.SemaphoreType.DMA((n_steps,)),
                        pltpu.SemaphoreType.DMA((7, n_small)), pltpu.SemaphoreType.DMA((7, n_small)),
                        pltpu.SemaphoreType.DMA((n_small,))])
    return pl.pallas_call(
        body, name="gather_in_proj", grid_spec=grid_spec,
        out_shape=[jax.ShapeDtypeStruct((S, IN_COLS), BF16), jax.ShapeDtypeStruct((D_MODEL, S), BF16),
                   jax.ShapeDtypeStruct((N_DEV,) + w_sh.shape, BF16)]
        + [jax.ShapeDtypeStruct((N_DEV,) + s.shape, s.dtype) for s in smalls],
        compiler_params=_params(2),
    )(order, x, norm_g, w_sh, *smalls)


def _bias_table(rel_bias):
    wide = 1024

    def body(r_ref, o_ref):
        h = pl.program_id(0)
        col = lax.broadcasted_iota(jnp.int32, (1, wide), 1)
        k_minus_q = jnp.where(col < KB, col, col - wide)
        idx = jnp.clip(PADK - k_minus_q, -MAX_REL, MAX_REL) + MAX_REL
        f = jnp.zeros((1, wide), F32)
        for r in range(MAX_REL - CHUNK + 1, N_REL):
            f = jnp.where(idx == r, r_ref[h, r], f)
        kcol = lax.broadcasted_iota(jnp.int32, (1, KB), 1)
        kc = kcol >> 6
        sub = lax.broadcasted_iota(jnp.int32, (8, 1), 0)
        f8 = jnp.broadcast_to(f * LOG2E, (8, wide))
        base = f8
        for r in range(1, 8):
            base = jnp.where(sub == r, pltpu.roll(f8, r, 1), base)
        for qh in range(QB // 8):
            rows = (pltpu.roll(base, 8 * qh, 1) if qh else base)[:, 0:KB]
            qc = (8 * qh) // CHUNK
            band = (kc >= qc) & (kc <= qc + N_LEFT)
            for t in range(3):
                o_ref[t, 0, 8 * qh:8 * qh + 8, :] = jnp.where(band & (kcol >= PADK - t * QB), rows, NEG_BIG)

    return pl.pallas_call(
        body, name="bias_table", grid=(HEADS,),
        out_shape=jax.ShapeDtypeStruct((3, HEADS, QB, KB), F32),
        in_specs=[pl.BlockSpec(memory_space=pltpu.SMEM)],
        out_specs=pl.BlockSpec((3, 1, QB, KB), lambda h: (0, h, 0, 0)),
        compiler_params=_params(1),
    )(rel_bias)


def _load_keys(p_hbm, kp, vp, sem):
    kp[0:PADK, :] = jnp.zeros((PADK, D_ATT), BF16)
    vp[0:PADK, :] = jnp.zeros((PADK, D_ATT), BF16)
    S = p_hbm.shape[0]
    ck = pltpu.make_async_copy(p_hbm.at[:, D_ATT:2 * D_ATT], kp.at[PADK:PADK + S, :], sem.at[0])
    cv = pltpu.make_async_copy(p_hbm.at[:, 2 * D_ATT:3 * D_ATT], vp.at[PADK:PADK + S, :], sem.at[1])
    ck.start()
    cv.start()
    ck.wait()
    cv.wait()


def _attn_fwd(P, bias_tab):
    S = P.shape[0]
    nb = S // QB

    def body(q_ref, p_hbm, bias_ref, o_ref, lse_ref, kp, vp, sem):
        g = pl.program_id(0)

        @pl.when(g == 0)
        def _():
            _load_keys(p_hbm, kp, vp, sem)

        start = pl.multiple_of(g * QB, QB)
        lane = lax.broadcasted_iota(jnp.int32, (1, 128), 1)
        for p in range(HEADS // 2):
            cols = slice(128 * p, 128 * (p + 1))
            qp = q_ref[:, cols] * SCALE
            kpair = kp[pl.ds(start, KB), cols]
            vpair = vp[pl.ds(start, KB), cols]
            outs = []
            for e in range(2):
                h = 2 * p + e
                lm = (lane < 64) if e == 0 else (lane >= 64)
                qm = jnp.where(lm, qp, jnp.zeros_like(qp))
                s = _dot_nt(qm, kpair) * LOG2E + bias_ref[0, h]
                mx = jnp.max(s, axis=-1, keepdims=True)
                ex = jnp.exp2(s - mx)
                sm = jnp.sum(ex, axis=-1, keepdims=True)
                outs.append(_dot(ex.astype(BF16), vpair) * (1.0 / sm))
                lse_ref[:, h:h + 1] = mx + jnp.log2(sm)
            o_ref[:, cols] = jnp.where(lane < 64, outs[0], outs[1]).astype(BF16)

    return pl.pallas_call(
        body, name="attn_fwd", grid=(nb,),
        out_shape=[jax.ShapeDtypeStruct((S, D_ATT), BF16), jax.ShapeDtypeStruct((S, HEADS), F32)],
        in_specs=[pl.BlockSpec((QB, D_ATT), lambda g: (g, 0)), ANY,
                  pl.BlockSpec((1, HEADS, QB, KB), lambda g: (jnp.minimum(g, 2), 0, 0, 0))],
        out_specs=[pl.BlockSpec((QB, D_ATT), lambda g: (g, 0)),
                   pl.BlockSpec((QB, HEADS), lambda g: (g, 0))],
        scratch_shapes=[pltpu.VMEM((S + PADK, D_ATT), BF16), pltpu.VMEM((S + PADK, D_ATT), BF16),
                        pltpu.SemaphoreType.DMA((2,))],
        compiler_params=_params(1),
    )(P, P, bias_tab)


def _token_local(x, tgt, P, att, proj_g, w_out, cw_g, conv_b, final_g):
    S = x.shape[0]
    ts = 256
    nt = S // ts
    hb = 16

    def body(x_ref, t_ref, s1_ref, s2_ref, s3_ref, h1_ref, h2_ref, att_ref,
             pg_ref, wo_ref, cwg_ref, cb_ref, g2_ref,
             dx2_ref, dg_ref, datt_ref, dwo_ref, dproj_ref, sm1_ref, sm2_ref,
             carry, wao_ref, wco_ref, cw_ref, dwo_acc, dwao_acc, dwco_acc):
        i = pl.program_id(0)
        t = nt - 1 - i

        @pl.when(i == 0)
        def _():
            dwo_acc[...] = jnp.zeros_like(dwo_acc)
            dwao_acc[...] = jnp.zeros_like(dwao_acc)
            dwco_acc[...] = jnp.zeros_like(dwco_acc)
            lane = lax.broadcasted_iota(jnp.int32, (1, 128), 1)
            for j in range(N_DEV):
                wao_ref[:, 128 * j:128 * (j + 1)] = pg_ref[j, :, 0:128]
                wco_ref[:, 128 * j:128 * (j + 1)] = pg_ref[j, :, 128:256]
            for p in range(N_DEV // 2):
                cw_ref[:, 128 * p:128 * (p + 1)] = jnp.where(
                    lane < 64, cwg_ref[2 * p], pltpu.roll(cwg_ref[2 * p + 1], 64, 1))
            sm1_ref[...] = jnp.zeros_like(sm1_ref)
            sm2_ref[...] = jnp.zeros_like(sm2_ref)
            carry[...] = jnp.zeros_like(carry)

        za = s1_ref[:, 0:512].astype(F32)
        gb = s1_ref[:, 512:1024].astype(F32)
        gc = s1_ref[:, 1024:1536].astype(F32)
        u = s2_ref[:, 0:512].astype(F32)
        zc = s2_ref[:, 512:1024].astype(F32)
        ga = jnp.concatenate([s2_ref[:, 1024:1536], s3_ref[:, 0:512]], axis=1).astype(F32)
        gv = s3_ref[:, 512:1536].astype(F32)
        att = att_ref[...].astype(F32)
        row = lax.broadcasted_iota(jnp.int32, (ts, 1), 0)

        sa = _sigmoid(za)
        silu_a = za * sa
        att_g = (att * silu_a).astype(BF16)
        y_att = _dot(att_g, wao_ref[...])

        cu = gc * u
        keep = jnp.where(t > 0, 1.0, 0.0).astype(F32)
        hcu = (h1_ref[:, 1024:1536].astype(F32) * h2_ref[:, 0:512].astype(F32)) * keep
        cu_m1 = jnp.where(row == 0, hcu[hb - 1:hb, :], pltpu.roll(cu, 1, 0))
        cu_m2 = jnp.where(row == 0, hcu[hb - 2:hb - 1, :],
                          jnp.where(row == 1, hcu[hb - 1:hb, :], pltpu.roll(cu, 2, 0)))
        w0, w1, w2 = cw_ref[0:1, :], cw_ref[1:2, :], cw_ref[2:3, :]
        vconv = w0 * cu_m2 + w1 * cu_m1 + w2 * cu + cb_ref[...]
        sc = _sigmoid(zc)
        silu_c = zc * sc
        cg = (gb * vconv * silu_c).astype(BF16)
        y_conv = _dot(cg, wco_ref[...])

        sga = _sigmoid(ga)
        sgv = _sigmoid(gv)
        m = (sga * y_att + sgv * y_conv).astype(BF16)
        x2 = x_ref[...] + _dot(m, wo_ref[...])
        r2 = lax.rsqrt(jnp.mean(x2 * x2, axis=-1, keepdims=True) + EPS)
        xn2 = x2 * r2
        g2 = g2_ref[...]
        err = xn2 * g2 - t_ref[...]
        sm1_ref[1:2, :] += jnp.sum(err * err, axis=0, keepdims=True) * (0.5 / D_MODEL)

        dy = err * (1.0 / D_MODEL)
        sm1_ref[0:1, :] += jnp.sum(dy * xn2, axis=0, keepdims=True)
        dxn = dy * g2
        dx2 = r2 * (dxn - xn2 * jnp.mean(dxn * xn2, axis=-1, keepdims=True))
        dx2_ref[...] = dx2
        dx2b = dx2.astype(BF16)
        dwo_acc[...] += _dot_tn(m, dx2b)
        dm = _dot_nt(dx2b, wo_ref[...])
        dya = (dm * sga).astype(BF16)
        dyc = (dm * sgv).astype(BF16)
        dg_ref[:, 2560:3584] = (dm * y_att * (sga * (1.0 - sga))).astype(BF16)
        dg_ref[:, 3584:4608] = (dm * y_conv * (sgv * (1.0 - sgv))).astype(BF16)
        dwao_acc[...] += _dot_tn(att_g, dya)
        dwco_acc[...] += _dot_tn(cg, dyc)
        datt_g = _dot_nt(dya, wao_ref[...])
        dcg = _dot_nt(dyc, wco_ref[...])
        datt_ref[...] = (datt_g * silu_a).astype(BF16)
        dg_ref[:, 0:512] = (datt_g * att * (sa * (1.0 + za * (1.0 - sa)))).astype(BF16)
        dg_ref[:, 512:1024] = (dcg * vconv * silu_c).astype(BF16)
        dg_ref[:, 2048:2560] = (dcg * gb * vconv * (sc * (1.0 + zc * (1.0 - sc)))).astype(BF16)
        dv = dcg * gb * silu_c
        sm2_ref[3:4, :] += jnp.sum(dv, axis=0, keepdims=True)
        sm2_ref[0:1, :] += jnp.sum(dv * cu_m2, axis=0, keepdims=True)
        sm2_ref[1:2, :] += jnp.sum(dv * cu_m1, axis=0, keepdims=True)
        sm2_ref[2:3, :] += jnp.sum(dv * cu, axis=0, keepdims=True)
        nxt = carry[...]
        dv_p1 = jnp.where(row == ts - 1, nxt[0:1, :], pltpu.roll(dv, ts - 1, 0))
        dv_p2 = jnp.where(row == ts - 1, nxt[1:2, :],
                          jnp.where(row == ts - 2, nxt[0:1, :], pltpu.roll(dv, ts - 2, 0)))
        dcu = w2 * dv + w1 * dv_p1 + w0 * dv_p2
        carry[...] = dv[0:8, :]
        dg_ref[:, 1024:1536] = (dcu * u).astype(BF16)
        dg_ref[:, 1536:2048] = (dcu * gc).astype(BF16)

        @pl.when(i == nt - 1)
        def _():
            dwo_ref[...] = dwo_acc[...].astype(BF16)
            for j in range(N_DEV):
                dproj_ref[j, :, 0:128] = dwao_acc[:, 128 * j:128 * (j + 1)].astype(BF16)
                dproj_ref[j, :, 128:256] = dwco_acc[:, 128 * j:128 * (j + 1)].astype(BF16)

    tile = lambda w: pl.BlockSpec((ts, w), lambda i: (nt - 1 - i, 0))
    seg = lambda c: pl.BlockSpec((ts, 1536), lambda i: (nt - 1 - i, c))
    halo = lambda c: pl.BlockSpec((hb, 1536), lambda i: (jnp.maximum((nt - 1 - i) * (ts // hb) - 1, 0), c))
    full = lambda a: pl.BlockSpec(a.shape, lambda i: (0,) * a.ndim)
    acc = lambda r, c: pl.BlockSpec((r, c), lambda i: (0, 0))
    return pl.pallas_call(
        body, name="token_local", grid=(nt,),
        out_shape=[jax.ShapeDtypeStruct((S, D_MODEL), F32), jax.ShapeDtypeStruct((S, GATE_COLS), BF16),
                   jax.ShapeDtypeStruct((S, D_ATT), BF16), jax.ShapeDtypeStruct((D_MODEL, D_MODEL), BF16),
                   jax.ShapeDtypeStruct(proj_g.shape, BF16),
                   jax.ShapeDtypeStruct((8, D_MODEL), F32), jax.ShapeDtypeStruct((8, D_CONV), F32)],
        in_specs=[tile(D_MODEL), tile(D_MODEL), seg(1), seg(2), seg(3), halo(1), halo(2), tile(D_ATT),
                  full(proj_g), full(w_out), full(cw_g), full(conv_b), full(final_g)],
        out_specs=[tile(D_MODEL), tile(GATE_COLS), tile(D_ATT), acc(D_MODEL, D_MODEL), full(proj_g),
                   acc(8, D_MODEL), acc(8, D_CONV)],
        scratch_shapes=[pltpu.VMEM((8, D_CONV), F32),
                        pltpu.VMEM((D_ATT, D_MODEL), BF16), pltpu.VMEM((D_CONV, D_MODEL), BF16),
                        pltpu.VMEM((8, D_CONV), F32), pltpu.VMEM((D_MODEL, D_MODEL), F32),
                        pltpu.VMEM((D_ATT, D_MODEL), F32), pltpu.VMEM((D_CONV, D_MODEL), F32)],
        compiler_params=_params(1),
    )(x, tgt, P, P, P, P, P, att, proj_g, w_out, cw_g, conv_b, final_g)


def _attn_bwd(P, att, datt, lse, bias_tab):
    S = P.shape[0]
    nb = S // QB

    def body(q_ref, att_ref, datt_ref, lse_ref, p_hbm, bias_ref, out_ref, db_ref,
             kp, vp, dq_ring, dk_ring, dv_ring, sem):
        g = pl.program_id(0)

        @pl.when(g == 0)
        def _():
            _load_keys(p_hbm, kp, vp, sem)
            db_ref[...] = jnp.zeros_like(db_ref)
            dk_ring[...] = jnp.zeros_like(dk_ring)
            dv_ring[...] = jnp.zeros_like(dv_ring)

        s_new = g % 3
        s_mid = (g + 2) % 3
        s_old = (g + 1) % 3

        @pl.when(g < nb)
        def _():
            start = pl.multiple_of(g * QB, QB)
            lane = lax.broadcasted_iota(jnp.int32, (1, 128), 1)
            for p in range(HEADS // 2):
                cols = slice(128 * p, 128 * (p + 1))
                qp = q_ref[:, cols] * SCALE
                op = att_ref[:, cols].astype(F32)
                dop = datt_ref[:, cols]
                kpair = kp[pl.ds(start, KB), cols]
                vpair = vp[pl.ds(start, KB), cols]
                dqs = []
                dk_acc = jnp.zeros((KB, 128), F32)
                dv_acc = jnp.zeros((KB, 128), F32)
                for e in range(2):
                    h = 2 * p + e
                    lm = (lane < 64) if e == 0 else (lane >= 64)
                    qm = jnp.where(lm, qp, jnp.zeros_like(qp))
                    dom = jnp.where(lm, dop, jnp.zeros_like(dop))
                    s = _dot_nt(qm, kpair) * LOG2E + bias_ref[0, h]
                    pr = jnp.exp2(s - lse_ref[:, h:h + 1])
                    dp = _dot_nt(dom, vpair)
                    delta = jnp.sum(dom.astype(F32) * op, axis=-1, keepdims=True)
                    ds = pr * (dp - delta)
                    db_ref[h] += ds
                    dsb = ds.astype(BF16)
                    prb = pr.astype(BF16)
                    dqs.append(_dot(dsb, kpair) * SCALE)
                    dk_acc = dk_acc + _dot_tn(dsb, qm)
                    dv_acc = dv_acc + _dot_tn(prb, dom)
                dq_ring[s_new, :, cols] = jnp.where(lane < 64, dqs[0], dqs[1])
                dk_ring[s_old, :, cols] += dk_acc[0:QB]
                dk_ring[s_mid, :, cols] += dk_acc[QB:2 * QB]
                dk_ring[s_new, :, cols] = dk_acc[2 * QB:3 * QB]
                dv_ring[s_old, :, cols] += dv_acc[0:QB]
                dv_ring[s_mid, :, cols] += dv_acc[QB:2 * QB]
                dv_ring[s_new, :, cols] = dv_acc[2 * QB:3 * QB]

        @pl.when(g >= 2)
        def _():
            out_ref[:, 0:D_ATT] = dq_ring[s_old].astype(BF16)
            out_ref[:, D_ATT:2 * D_ATT] = dk_ring[s_old].astype(BF16)
            out_ref[:, 2 * D_ATT:3 * D_ATT] = dv_ring[s_old].astype(BF16)

    qblk = lambda w: pl.BlockSpec((QB, w), lambda g: (jnp.minimum(g, nb - 1), 0))
    return pl.pallas_call(
        body, name="attn_bwd", grid=(nb + 2,),
        out_shape=[jax.ShapeDtypeStruct((S, 3 * D_ATT), BF16), jax.ShapeDtypeStruct((HEADS, QB, KB), F32)],
        in_specs=[qblk(D_ATT), qblk(D_ATT), qblk(D_ATT), qblk(HEADS), ANY,
                  pl.BlockSpec((1, HEADS, QB, KB), lambda g: (jnp.minimum(g, 2), 0, 0, 0))],
        out_specs=[pl.BlockSpec((QB, 3 * D_ATT), lambda g: (jnp.maximum(g - 2, 0), 0)),
                   pl.BlockSpec((HEADS, QB, KB), lambda g: (0, 0, 0))],
        scratch_shapes=[pltpu.VMEM((S + PADK, D_ATT), BF16), pltpu.VMEM((S + PADK, D_ATT), BF16),
                        pltpu.VMEM((3, QB, D_ATT), F32), pltpu.VMEM((3, QB, D_ATT), F32),
                        pltpu.VMEM((3, QB, D_ATT), F32), pltpu.SemaphoreType.DMA((2,))],
        compiler_params=_params(1),
    )(P, att, datt, lse, P, bias_tab)


def _bias_fold(dscore):
    wide = 1024

    def body(d_ref, o_ref):
        sub = lax.broadcasted_iota(jnp.int32, (8, 1), 0)
        col = lax.broadcasted_iota(jnp.int32, (1, wide), 1)
        pad = jnp.zeros((8, wide - KB), F32)
        for h in range(HEADS):
            acc = jnp.concatenate([d_ref[h, 0:8, :], pad], axis=1)
            for qh in range(1, QB // 8):
                a = jnp.concatenate([d_ref[h, 8 * qh:8 * qh + 8, :], pad], axis=1)
                acc = acc + pltpu.roll(a, wide - 8 * qh, 1)
            for r in range(1, 8):
                acc = jnp.where(sub == r, pltpu.roll(acc, wide - r, 1), acc)
            vec = jnp.sum(acc, axis=0, keepdims=True)
            far = (col <= PADK - MAX_REL) | (col > KB)
            tail = jnp.sum(jnp.where(far, vec, 0.0), axis=-1, keepdims=True)
            o_ref[h:h + 1, :] = jnp.where(col == wide - 1, tail, vec)

    return pl.pallas_call(
        body, name="bias_fold",
        out_shape=jax.ShapeDtypeStruct((HEADS, wide), F32),
        compiler_params=pltpu.CompilerParams(vmem_limit_bytes=VMEM_LIMIT),
    )(dscore)


def _in_proj_bwd(x, norm_g, dx2, dqkv, dgate, w_in_g):
    S = x.shape[0]
    ts = 256

    def body(x_ref, g_ref, dx2_ref, dq_ref, dg_ref, w_ref, gx_ref, dn_ref):
        @pl.when(pl.program_id(0) == 0)
        def _():
            dn_ref[...] = jnp.zeros_like(dn_ref)

        dh = jnp.zeros((ts, D_MODEL), F32)
        for j in range(N_DEV):
            if j < 2:
                d = dq_ref[:, j * W_BLK:(j + 1) * W_BLK]
            else:
                d = dg_ref[:, (j - 2) * W_BLK:(j - 1) * W_BLK]
            dh = dh + _dot_nt(d, w_ref[j])
        xf = x_ref[...]
        r = lax.rsqrt(jnp.mean(xf * xf, axis=-1, keepdims=True) + EPS)
        xn = xf * r
        dn_ref[0:1, :] += jnp.sum(dh * xn, axis=0, keepdims=True)
        dhg = dh * g_ref[...]
        gx_ref[...] = dx2_ref[...] + r * (dhg - xn * jnp.mean(dhg * xn, axis=-1, keepdims=True))

    tile = lambda w: pl.BlockSpec((ts, w), lambda i: (i, 0))
    return pl.pallas_call(
        body, name="in_proj_bwd", grid=(S // ts,),
        out_shape=[jax.ShapeDtypeStruct((S, D_MODEL), F32), jax.ShapeDtypeStruct((8, D_MODEL), F32)],
        in_specs=[tile(D_MODEL), pl.BlockSpec((1, D_MODEL), lambda i: (0, 0)), tile(D_MODEL),
                  tile(3 * D_ATT), tile(GATE_COLS),
                  pl.BlockSpec((N_DEV, D_MODEL, W_BLK), lambda i: (0, 0, 0))],
        out_specs=[tile(D_MODEL), pl.BlockSpec((8, D_MODEL), lambda i: (0, 0))],
        compiler_params=_params(1),
    )(x, norm_g, dx2, dqkv, dgate, w_in_g)


SCATTER_MASKS = ((3, 4, 5, 2, 7, 6, 1, 0), (5, 2, 3, 4, 7, 6, 1, 0))


def _w_in_grad_scatter(ht, dqkv, dgate, d_proj, d_wo, pack, order):
    S = ht.shape[1]
    ts = min(S, 2048)
    nt = S // ts
    n_steps = 8

    def body(order_ref, ht_ref, dq_ref, dg_ref, proj_hbm, wo_hbm, pack_hbm, g_ref, rproj, rwo, rpack,
             acc, stage, rsib, rici, d2d_send, d2d_recv, ici_send, ici_recv, small_send, small_recv, local_sems):
        k, i = pl.program_id(0), pl.program_id(1)
        x, y, c = _mesh_pos()
        my = _flat((x, y, c))
        sibling = (x, y, 1 - c)
        owners = [(x ^ (1 - c), y ^ c, c), (x ^ c, y ^ (1 - c), c), (1 - x, 1 - y, c)]
        peers = [sibling, (1 - x, y, c), (x, 1 - y, c), (1 - x, 1 - y, c),
                 (1 - x, y, 1 - c), (x, 1 - y, 1 - c), (1 - x, 1 - y, 1 - c)]
        small = ((proj_hbm, rproj, True), (wo_hbm, rwo, True), (pack_hbm, rpack, False))
        n_small = len(small)

        def small_copy(kk, a, receive=False):
            src, dst, per_peer = small[a]
            slot = _flat(peers[kk]) if receive else my
            return pltpu.make_async_remote_copy(
                src_ref=src.at[_flat(peers[kk])] if per_peer else src, dst_ref=dst.at[slot],
                send_sem=small_send.at[kk, a], recv_sem=small_recv.at[kk, a],
                device_id=peers[kk], device_id_type=MESH)

        def to_sibling(t):
            return pltpu.make_async_remote_copy(
                src_ref=stage.at[0], dst_ref=rsib.at[t], send_sem=d2d_send.at[t], recv_sem=d2d_recv.at[t],
                device_id=sibling, device_id_type=MESH)

        def to_owner(t):
            return pltpu.make_async_remote_copy(
                src_ref=stage.at[1], dst_ref=rici.at[t], send_sem=ici_send.at[t], recv_sem=ici_recv.at[t],
                device_id=owners[t], device_id_type=MESH)

        own_small = [pltpu.make_async_copy(src.at[my] if per_peer else src, dst.at[my], local_sems.at[a])
                     for a, (src, dst, per_peer) in enumerate(small)]

        @pl.when((k == 0) & (i == 0))
        def _():
            for cp in own_small:
                cp.start()
            for kk in range(len(peers)):
                for a in range(n_small):
                    small_copy(kk, a).start()

        def accumulate(d_ref):
            prod = _dot(ht_ref[...], d_ref[...])

            @pl.when(i == 0)
            def _():
                acc[...] = prod

            @pl.when(i > 0)
            def _():
                acc[...] += prod

        @pl.when(order_ref[k] < 2)
        def _():
            accumulate(dq_ref)

        @pl.when(order_ref[k] >= 2)
        def _():
            accumulate(dg_ref)

        @pl.when(i == nt - 1)
        def _():
            for s in range(n_steps):
                @pl.when(k == s)
                def _():
                    t = s // 2
                    if s % 2 == 0:
                        if t >= 1:
                            to_sibling(t - 1).wait_send()
                        stage[0] = acc[...].astype(BF16)
                        to_sibling(t).start()
                    elif t < 3:
                        if t >= 1:
                            to_owner(t - 1).wait_send()
                        to_sibling(t).wait_recv()
                        stage[1] = (acc[...] + rsib[t].astype(F32)).astype(BF16)
                        to_owner(t).start()
                    else:
                        to_sibling(t).wait_recv()
                        total = acc[...] + rsib[t].astype(F32)
                        for j in range(3):
                            to_owner(j).wait_recv()
                            total = total + rici[j].astype(F32)
                        g_ref[...] = total
                        to_owner(2).wait_send()
                        to_sibling(3).wait_send()
                        for q in range(len(peers)):
                            for a in range(n_small):
                                small_copy(q, a).wait_send()
                                small_copy(q, a, receive=True).wait_recv()
                        for cp in own_small:
                            cp.wait()

    blk = (D_MODEL, W_BLK)
    grid_spec = pltpu.PrefetchScalarGridSpec(
        num_scalar_prefetch=1, grid=(n_steps, nt),
        in_specs=[pl.BlockSpec((D_MODEL, ts), lambda k, i, o: (0, i)),
                  pl.BlockSpec((ts, W_BLK), lambda k, i, o: (i, jnp.minimum(o[k], 1))),
                  pl.BlockSpec((ts, W_BLK), lambda k, i, o: (i, jnp.maximum(o[k] - 2, 0))),
                  ANY, ANY, ANY],
        out_specs=[pl.BlockSpec(blk, lambda k, i, o: (0, 0)), ANY, ANY, ANY],
        scratch_shapes=[pltpu.VMEM(blk, F32), pltpu.VMEM((2,) + blk, BF16),
                        pltpu.VMEM((4,) + blk, BF16), pltpu.VMEM((3,) + blk, BF16),
                        pltpu.SemaphoreType.DMA((4,)), pltpu.SemaphoreType.DMA((4,)),
                        pltpu.SemaphoreType.DMA((3,)), pltpu.SemaphoreType.DMA((3,)),
                        pltpu.SemaphoreType.DMA((7, 3)), pltpu.SemaphoreType.DMA((7, 3)),
                        pltpu.SemaphoreType.DMA((3,))])
    return pl.pallas_call(
        body, name="w_in_grad_scatter", grid_spec=grid_spec,
        out_shape=[jax.ShapeDtypeStruct(blk, F32),
                   jax.ShapeDtypeStruct(d_proj.shape, BF16), jax.ShapeDtypeStruct(d_wo.shape, BF16),
                   jax.ShapeDtypeStruct((N_DEV,) + pack.shape, F32)],
        compiler_params=_params(2),
    )(order, ht, dqkv, dgate, d_proj, d_wo, pack)


def _adamw(w, g, m, v):
    m = ADAM_B1 * m + (1.0 - ADAM_B1) * g
    v = ADAM_B2 * v + (1.0 - ADAM_B2) * (g * g)
    m_hat = m / (1.0 - ADAM_B1 ** ADAM_STEP)
    v_hat = v / (1.0 - ADAM_B2 ** ADAM_STEP)
    delta = -ADAM_LR * (m_hat / (jnp.sqrt(v_hat) + ADAM_EPS) + ADAM_WD * w)
    return delta, m, v


def _sum_adamw(parts, w, m, v, name):
    R, C = w.shape
    n = parts.shape[0]
    tr = min(R, 256)

    def body(p_ref, w_ref, m_ref, v_ref, g_ref, d_ref, nm_ref, nv_ref):
        g = p_ref[0].astype(F32)
        for s in range(1, n):
            g = g + p_ref[s].astype(F32)
        g_ref[...] = g
        d_ref[...], nm_ref[...], nv_ref[...] = _adamw(w_ref[...], g, m_ref[...], v_ref[...])

    tile = pl.BlockSpec((tr, C), lambda i: (i, 0))
    return pl.pallas_call(
        body, name=name, grid=(R // tr,),
        out_shape=[jax.ShapeDtypeStruct((R, C), F32)] * 4,
        in_specs=[pl.BlockSpec((n, tr, C), lambda i: (0, i, 0)), tile, tile, tile],
        out_specs=[tile] * 4,
        compiler_params=_params(1),
    )(parts, w, m, v)


def _adamw_mid(r_proj, r_wo, params):
    def body(rp_ref, rw_ref, *refs):
        ins, outs = refs[:9], refs[9:]

        def total(part):
            g = part(0).astype(F32)
            for s in range(1, N_DEV):
                g = g + part(s).astype(F32)
            return g

        grads = (total(lambda s: rp_ref[s, :, 0:128]), total(lambda s: rp_ref[s, :, 128:256]),
                 total(lambda s: rw_ref[s]))
        for n, g in enumerate(grads):
            w, m, v = (r[...] for r in ins[3 * n:3 * n + 3])
            outs[4 * n][...] = g
            outs[4 * n + 1][...], outs[4 * n + 2][...], outs[4 * n + 3][...] = _adamw(w, g, m, v)

    return pl.pallas_call(
        body, name="adamw_mid",
        out_shape=[jax.ShapeDtypeStruct(params[3 * n].shape, F32) for n in range(3) for _ in range(4)],
        compiler_params=pltpu.CompilerParams(vmem_limit_bytes=VMEM_LIMIT),
    )(r_proj, r_wo, *params)


def _adamw_small(r_pack, params):
    wide = 384

    def body(p_ref, *refs):
        ins, loss_ref, outs = refs[:15], refs[15], refs[16:]
        tot = p_ref[0]
        for s in range(1, N_DEV):
            tot = tot + p_ref[s]
        me = _flat(_mesh_pos())
        loss_ref[...] = jnp.sum(tot[2:3, :], axis=-1, keepdims=True)
        mine = pltpu.roll(tot[0:8, 0:D_CONV], (D_CONV - 64 * me) % D_CONV, 1)
        col = lax.broadcasted_iota(jnp.int32, (D_MODEL, wide), 0)
        idx = lax.broadcasted_iota(jnp.int32, (D_MODEL, wide), 1)
        near = (idx > MAX_REL - CHUNK) & (idx < 2 * MAX_REL) & (col == PADK + MAX_REL - idx)
        far = (idx == 2 * MAX_REL) & (col == D_MODEL - 1)
        perm = jnp.where(near | far, 1.0, 0.0).astype(F32)
        g_rel = jnp.dot(tot[8:16], perm, precision=lax.Precision.HIGHEST, preferred_element_type=F32)
        grads = (tot[0:1], tot[1:2], mine[3:6, 0:64], tot[6:7, 0:D_CONV], g_rel[:, 0:N_REL])
        for n, g in enumerate(grads):
            w, m, v = (r[...] for r in ins[3 * n:3 * n + 3])
            outs[4 * n][...] = g
            outs[4 * n + 1][...], outs[4 * n + 2][...], outs[4 * n + 3][...] = _adamw(w, g, m, v)

    return pl.pallas_call(
        body, name="adamw_small",
        out_shape=[jax.ShapeDtypeStruct((1, 1), F32)]
        + [jax.ShapeDtypeStruct(params[3 * n].shape, F32) for n in range(5) for _ in range(4)],
    )(r_pack, *params)


def _pad_row(a, width=D_MODEL):
    a = a.reshape(-1, a.shape[-1])
    return jnp.pad(a, ((0, 0), (0, width - a.shape[-1])))


def kernel(x, norm_g, w_in, rel_bias, w_att_out, conv_w, conv_b, w_conv_out, w_out, final_norm_g, loss_target, m_norm_g, m_w_in, m_rel_bias, m_w_att_out, m_conv_w, m_conv_b, m_w_conv_out, m_w_out, m_final_norm_g, v_norm_g, v_w_in, v_rel_bias, v_w_att_out, v_conv_w, v_conv_b, v_w_conv_out, v_w_out, v_final_norm_g):
    S = x.shape[1]
    x2d = x.reshape(S, D_MODEL)
    tgt = loss_target.reshape(S, D_MODEL)
    me = 4 * lax.axis_index("x") + 2 * lax.axis_index("y") + lax.axis_index("c")
    row = lambda a: a.reshape(1, D_MODEL)

    proj_sh = jnp.concatenate([w_att_out[0], w_conv_out[0]], axis=1).astype(BF16)
    cw_sh = jnp.pad(conv_w[0], ((0, 5), (0, 64)))
    P, ht, w_in_g, proj_g, w_out_g, cw_g = _gather_in_proj(
        x2d, norm_g, w_in[0].astype(BF16), [proj_sh, w_out[0].astype(BF16), cw_sh],
        me ^ _by_core(GATHER_MASKS))

    bias_tab = _bias_table(rel_bias[0])
    att, lse = _attn_fwd(P, bias_tab)
    dx2, dgate, datt, d_wo, d_proj, sm1, sm2 = _token_local(
        x2d, tgt, P, att, proj_g, w_out_g.reshape(D_MODEL, D_MODEL), cw_g, conv_b, row(final_norm_g))
    dqkv, dscore = _attn_bwd(P, att, datt, lse, bias_tab)
    dbias = _bias_fold(dscore)
    grad_x, dnorm = _in_proj_bwd(x2d, norm_g, dx2, dqkv, dgate, w_in_g)

    pack = jnp.concatenate([dnorm[0:1], sm1[0:2], _pad_row(sm2[0:4]), jnp.zeros((1, D_MODEL), F32), dbias],
                           axis=0)
    g_win_sum, r_proj, r_wo, r_pack = _w_in_grad_scatter(
        ht, dqkv, dgate, d_proj, d_wo.reshape(N_DEV, 128, D_MODEL), pack, me ^ _by_core(SCATTER_MASKS))

    res = {"w_in": _sum_adamw(g_win_sum[None], w_in[0], m_w_in[0], v_w_in[0], "adamw_w_in")}
    mid = _adamw_mid(r_proj, r_wo, (w_att_out[0], m_w_att_out[0], v_w_att_out[0],
                                    w_conv_out[0], m_w_conv_out[0], v_w_conv_out[0],
                                    w_out[0], m_w_out[0], v_w_out[0]))
    for n, name in enumerate(("w_att_out", "w_conv_out", "w_out")):
        res[name] = mid[4 * n:4 * n + 4]
    small = _adamw_small(r_pack, (norm_g, m_norm_g, v_norm_g,
                                  row(final_norm_g), row(m_final_norm_g), row(v_final_norm_g),
                                  conv_w[0], m_conv_w[0], v_conv_w[0], conv_b, m_conv_b, v_conv_b,
                                  rel_bias[0], m_rel_bias[0], v_rel_bias[0]))
    loss = small[0].reshape(())
    for n, name in enumerate(("norm_g", "final_norm_g", "conv_w", "conv_b", "rel_bias")):
        res[name] = small[1 + 4 * n:5 + 4 * n]

    leading = {"norm_g": (1, D_MODEL), "final_norm_g": (D_MODEL,), "conv_b": (1, D_CONV)}
    outs = []
    for kind in range(4):
        for name in ("norm_g", "w_in", "rel_bias", "w_att_out", "conv_w", "conv_b", "w_conv_out", "w_out",
                     "final_norm_g"):
            a = res[name][kind]
            outs.append(a.reshape(leading[name]) if name in leading else a[None])
    return (loss, grad_x.reshape(1, S, D_MODEL), *outs)
```

```python
import functools

import numpy as np
import jax
import jax.numpy as jnp
from jax import lax
from jax.experimental import pallas as pl
from jax.experimental.pallas import tpu as pltpu

F32 = jnp.float32
BF16 = jnp.bfloat16

D_MODEL = 1024
CHUNK = 64
N_LEFT = 8
HEADS = 8
D_ATT = 512
D_CONV = 512
MAX_REL = 128
N_REL = 2 * MAX_REL + 1
IN_COLS = 6144
EPS = 1e-6
NEG_BIG = -1e30
N_DEV = 8
W_BLK = IN_COLS // N_DEV
QB = 4 * CHUNK
KB = QB + N_LEFT * CHUNK
PADK = N_LEFT * CHUNK
SCALE = 64 ** -0.5
LOG2E = 1.4426950408889634
GATE_COLS = IN_COLS - 3 * D_ATT

ADAM_LR = 0.001
ADAM_B1 = 0.9
ADAM_B2 = 0.999
ADAM_EPS = 1e-08
ADAM_WD = 0.01
ADAM_STEP = 10

VMEM_LIMIT = 56 * 1024 * 1024

MESH = pl.DeviceIdType.MESH
ANY = pl.BlockSpec(memory_space=pl.ANY)


def _params(n_grid):
    return pltpu.CompilerParams(dimension_semantics=("arbitrary",) * n_grid,
                                vmem_limit_bytes=VMEM_LIMIT)


def _dot(a, b):
    return jnp.dot(a, b, preferred_element_type=F32)


def _dot_nt(a, b):
    return lax.dot_general(a, b, (((1,), (1,)), ((), ())), preferred_element_type=F32)


def _dot_tn(a, b):
    return lax.dot_general(a, b, (((0,), (0,)), ((), ())), preferred_element_type=F32)


def _sigmoid(z):
    return 0.5 * jnp.tanh(0.5 * z) + 0.5


def _mesh_pos():
    return lax.axis_index("x"), lax.axis_index("y"), lax.axis_index("c")


def _flat(p):
    return 4 * p[0] + 2 * p[1] + p[2]


def _by_core(masks):
    m0, m1 = (jnp.array(m, jnp.int32) for m in masks)
    return jnp.where(lax.axis_index("c") == 0, m0, m1)


GATHER_MASKS = ((0, 1, 4, 3, 2, 5, 6, 7), (0, 1, 2, 5, 4, 3, 6, 7))


def _gather_in_proj(x, norm_g, w_sh, smalls, order):
    S = x.shape[0]
    ts = 1024
    nt = S // ts
    n_small = len(smalls)
    n_steps = N_DEV

    def body(order_ref, x_ref, g_ref, w_hbm, *rest):
        small_in = rest[:n_small]
        p_ref, ht_ref, wg_hbm = rest[n_small:n_small + 3]
        small_out = rest[n_small + 3:2 * n_small + 3]
        (wbuf, hbuf, own_sem, send_sems, recv_sems, out_sems,
         small_send, small_recv, small_local) = rest[2 * n_small + 3:]
        k, i = pl.program_id(0), pl.program_id(1)
        x_, y_, c_ = _mesh_pos()
        me, sibling = (x_, y_, c_), (x_, y_, 1 - c_)
        my = _flat(me)
        chips = [(x_ ^ (1 - c_), y_ ^ c_), (x_ ^ c_, y_ ^ (1 - c_)), (1 - x_, 1 - y_)]
        peers = [sibling] + [(*chip, c_) for chip in chips] + [(*chip, 1 - c_) for chip in chips]

        def wcopy(sem, block, to, from_input=False):
            dst = wbuf.at[_flat(block)]
            return pltpu.make_async_remote_copy(
                src_ref=w_hbm if from_input else dst, dst_ref=dst,
                send_sem=send_sems.at[sem], recv_sem=recv_sems.at[sem], device_id=to, device_id_type=MESH)

        def small_copy(q, a, receive=False):
            slot = _flat(peers[q]) if receive else my
            return pltpu.make_async_remote_copy(
                src_ref=small_in[a], dst_ref=small_out[a].at[slot],
                send_sem=small_send.at[q, a], recv_sem=small_recv.at[q, a],
                device_id=peers[q], device_id_type=MESH)

        def keep(step, block):
            return pltpu.make_async_copy(wbuf.at[_flat(block)], wg_hbm.at[_flat(block)], out_sems.at[step])

        own = pltpu.make_async_copy(w_hbm, wbuf.at[my], own_sem)
        small_own = [pltpu.make_async_copy(small_in[a], small_out[a].at[my], small_local.at[a])
                     for a in range(n_small)]
        passed_on = [(*chips[1], 1 - c_), (*chips[0], 1 - c_), (*chips[2], 1 - c_)]
        arrivals = [me, sibling]
        for j in range(3):
            arrivals += [(*chips[j], c_), passed_on[j]]

        @pl.when(i == 0)
        def _():
            for kk in range(n_steps):
                @pl.when(k == kk)
                def _():
                    j = kk // 2 - 1
                    if kk == 0:
                        own.start()
                        wcopy(0, me, sibling, True).start()
                        wcopy(1, me, (*chips[0], c_), True).start()
                        own.wait()
                    elif kk == 1:
                        wcopy(0, sibling, me).wait_recv()
                        wcopy(2, me, (*chips[1], c_), True).start()
                    elif kk % 2 == 0:
                        wcopy(1 + j, (*chips[j], c_), me).wait_recv()
                        wcopy(4 + j, (*chips[j], c_), sibling).start()
                        if kk == 2:
                            wcopy(3, me, (*chips[2], c_), True).start()
                    else:
                        wcopy(4 + j, passed_on[j], me).wait_recv()
                        if kk == 3:
                            for cp in small_own:
                                cp.start()
                            for q in range(len(peers)):
                                for a in range(n_small):
                                    small_copy(q, a).start()
                    keep(kk, arrivals[kk]).start()

        row0 = pl.multiple_of(i * ts, ts)

        @pl.when(k == 0)
        def _():
            xf = x_ref[...]
            r = lax.rsqrt(jnp.mean(xf * xf, axis=-1, keepdims=True) + EPS)
            hf = (xf * r) * g_ref[...]
            hbuf[pl.ds(row0, ts), :] = hf.astype(BF16)
            ht_ref[...] = hf.T.astype(BF16)

        p_ref[...] = _dot(hbuf[pl.ds(row0, ts), :], wbuf[order_ref[k]]).astype(BF16)

        @pl.when((k == n_steps - 1) & (i == nt - 1))
        def _():
            wcopy(0, me, sibling, True).wait_send()
            for j, chip in enumerate(chips):
                wcopy(1 + j, me, (*chip, c_), True).wait_send()
                wcopy(4 + j, (*chip, c_), sibling).wait_send()
            for kk in range(n_steps):
                keep(kk, arrivals[kk]).wait()
            for cp in small_own:
                cp.wait()
            for q in range(len(peers)):
                for a in range(n_small):
                    small_copy(q, a).wait_send()
                    small_copy(q, a, receive=True).wait_recv()

    first_pass = lambda k, i: jnp.where(k == 0, i, nt - 1)
    grid_spec = pltpu.PrefetchScalarGridSpec(
        num_scalar_prefetch=1, grid=(n_steps, nt),
        in_specs=[pl.BlockSpec((ts, D_MODEL), lambda k, i, o: (first_pass(k, i), 0)),
                  pl.BlockSpec((1, D_MODEL), lambda k, i, o: (0, 0)), ANY] + [ANY] * n_small,
        out_specs=[pl.BlockSpec((ts, W_BLK), lambda k, i, o: (i, o[k])),
                   pl.BlockSpec((D_MODEL, ts), lambda k, i, o: (0, first_pass(k, i))), ANY] + [ANY] * n_small,
        scratch_shapes=[pltpu.VMEM((N_DEV, D_MODEL, W_BLK), BF16), pltpu.VMEM((S, D_MODEL), BF16),
                        pltpu.SemaphoreType.DMA, pltpu.SemaphoreType.DMA((7,)), pltpu.SemaphoreType.DMA((7,)),
                        pltpu.SemaphoreType.DMA((n_steps,)),
                        pltpu.SemaphoreType.DMA((7, n_small)), pltpu.SemaphoreType.DMA((7, n_small)),
                        pltpu.SemaphoreType.DMA((n_small,))])
    return pl.pallas_call(
        body, name="gather_in_proj", grid_spec=grid_spec,
        out_shape=[jax.ShapeDtypeStruct((S, IN_COLS), BF16), jax.ShapeDtypeStruct((D_MODEL, S), BF16),
                   jax.ShapeDtypeStruct((N_DEV,) + w_sh.shape, BF16)]
        + [jax.ShapeDtypeStruct((N_DEV,) + s.shape, s.dtype) for s in smalls],
        compiler_params=_params(2),
    )(order, x, norm_g, w_sh, *smalls)


def _bias_table(rel_bias):
    wide = 1024

    def body(r_ref, o_ref):
        h = pl.program_id(0)
        col = lax.broadcasted_iota(jnp.int32, (1, wide), 1)
        k_minus_q = jnp.where(col < KB, col, col - wide)
        idx = jnp.clip(PADK - k_minus_q, -MAX_REL, MAX_REL) + MAX_REL
        f = jnp.zeros((1, wide), F32)
        for r in range(MAX_REL - CHUNK + 1, N_REL):
            f = jnp.where(idx == r, r_ref[h, r], f)
        kcol = lax.broadcasted_iota(jnp.int32, (1, KB), 1)
        kc = kcol >> 6
        sub = lax.broadcasted_iota(jnp.int32, (8, 1), 0)
        f8 = jnp.broadcast_to(f * LOG2E, (8, wide))
        base = f8
        for r in range(1, 8):
            base = jnp.where(sub == r, pltpu.roll(f8, r, 1), base)
        for qh in range(QB // 8):
            rows = (pltpu.roll(base, 8 * qh, 1) if qh else base)[:, 0:KB]
            qc = (8 * qh) // CHUNK
            band = (kc >= qc) & (kc <= qc + N_LEFT)
            for t in range(3):
                o_ref[t, 0, 8 * qh:8 * qh + 8, :] = jnp.where(band & (kcol >= PADK - t * QB), rows, NEG_BIG)

    return pl.pallas_call(
        body, name="bias_table", grid=(HEADS,),
        out_shape=jax.ShapeDtypeStruct((3, HEADS, QB, KB), F32),
        in_specs=[pl.BlockSpec(memory_space=pltpu.SMEM)],
        out_specs=pl.BlockSpec((3, 1, QB, KB), lambda h: (0, h, 0, 0)),
        compiler_params=_params(1),
    )(rel_bias)


def _load_keys(p_hbm, kp, vp, sem):
    kp[0:PADK, :] = jnp.zeros((PADK, D_ATT), BF16)
    vp[0:PADK, :] = jnp.zeros((PADK, D_ATT), BF16)
    S = p_hbm.shape[0]
    ck = pltpu.make_async_copy(p_hbm.at[:, D_ATT:2 * D_ATT], kp.at[PADK:PADK + S, :], sem.at[0])
    cv = pltpu.make_async_copy(p_hbm.at[:, 2 * D_ATT:3 * D_ATT], vp.at[PADK:PADK + S, :], sem.at[1])
    ck.start()
    cv.start()
    ck.wait()
    cv.wait()


def _attn_fwd(P, bias_tab):
    S = P.shape[0]
    nb = S // QB

    def body(q_ref, p_hbm, bias_ref, o_ref, lse_ref, kp, vp, sem):
        g = pl.program_id(0)

        @pl.when(g == 0)
        def _():
            _load_keys(p_hbm, kp, vp, sem)

        start = pl.multiple_of(g * QB, QB)
        lane = lax.broadcasted_iota(jnp.int32, (1, 128), 1)
        for p in range(HEADS // 2):
            cols = slice(128 * p, 128 * (p + 1))
            qp = q_ref[:, cols] * SCALE
            kpair = kp[pl.ds(start, KB), cols]
            vpair = vp[pl.ds(start, KB), cols]
            outs = []
            for e in range(2):
                h = 2 * p + e
                lm = (lane < 64) if e == 0 else (lane >= 64)
                qm = jnp.where(lm, qp, jnp.zeros_like(qp))
                s = _dot_nt(qm, kpair) * LOG2E + bias_ref[0, h]
                mx = jnp.max(s, axis=-1, keepdims=True)
                ex = jnp.exp2(s - mx)
                sm = jnp.sum(ex, axis=-1, keepdims=True)
                outs.append(_dot(ex.astype(BF16), vpair) * (1.0 / sm))
                lse_ref[:, h:h + 1] = mx + jnp.log2(sm)
            o_ref[:, cols] = jnp.where(lane < 64, outs[0], outs[1]).astype(BF16)

    return pl.pallas_call(
        body, name="attn_fwd", grid=(nb,),
        out_shape=[jax.ShapeDtypeStruct((S, D_ATT), BF16), jax.ShapeDtypeStruct((S, HEADS), F32)],
        in_specs=[pl.BlockSpec((QB, D_ATT), lambda g: (g, 0)), ANY,
                  pl.BlockSpec((1, HEADS, QB, KB), lambda g: (jnp.minimum(g, 2), 0, 0, 0))],
        out_specs=[pl.BlockSpec((QB, D_ATT), lambda g: (g, 0)),
                   pl.BlockSpec((QB, HEADS), lambda g: (g, 0))],
        scratch_shapes=[pltpu.VMEM((S + PADK, D_ATT), BF16), pltpu.VMEM((S + PADK, D_ATT), BF16),
                        pltpu.SemaphoreType.DMA((2,))],
        compiler_params=_params(1),
    )(P, P, bias_tab)


def _token_local(x, tgt, P, att, proj_g, w_out, cw_g, conv_b, final_g):
    S = x.shape[0]
    ts = 256
    nt = S // ts
    hb = 16
    SR = 32

    def body(x_ref, t_ref, s1_ref, s2_ref, s3_ref, h1_ref, h2_ref, att_ref,
             pg_ref, wo_ref, cwg_ref, cb_ref, g2_ref,
             dx2_ref, dg_ref, datt_ref, dwo_ref, dproj_ref, sm1_ref, sm2_ref,
             wao_ref, wco_ref, cw_ref, dwo_acc, dwao_acc, dwco_acc,
             cu_s, dv_s, attg_s, cg_s, ya_s, yc_s, m_s, mo_s, dx2b_s, dm_s, dya_s, dyc_s, dag_s, dcg_s):
        i = pl.program_id(0)
        t = nt - 1 - i

        @pl.when(i == 0)
        def _():
            dwo_acc[...] = jnp.zeros_like(dwo_acc)
            dwao_acc[...] = jnp.zeros_like(dwao_acc)
            dwco_acc[...] = jnp.zeros_like(dwco_acc)
            lane = lax.broadcasted_iota(jnp.int32, (1, 128), 1)
            for j in range(N_DEV):
                wao_ref[:, 128 * j:128 * (j + 1)] = pg_ref[j, :, 0:128]
                wco_ref[:, 128 * j:128 * (j + 1)] = pg_ref[j, :, 128:256]
            for p in range(N_DEV // 2):
                cw_ref[:, 128 * p:128 * (p + 1)] = jnp.where(
                    lane < 64, cwg_ref[2 * p], pltpu.roll(cwg_ref[2 * p + 1], 64, 1))
            sm1_ref[...] = jnp.zeros_like(sm1_ref)
            sm2_ref[...] = jnp.zeros_like(sm2_ref)
            dv_s[ts:ts + 8, :] = jnp.zeros((8, D_CONV), F32)

        w0, w1, w2 = cw_ref[0:1, :], cw_ref[1:2, :], cw_ref[2:3, :]
        cb, g2 = cb_ref[...], g2_ref[...]
        f32 = lambda v: v.astype(F32)

        def strips(sr, fn):
            def step(r, c):
                fn(r, pl.ds(pl.multiple_of(r * sr, sr), sr))
                return c
            lax.fori_loop(0, ts // sr, step, 0)

        def conv_taps(r):
            ext = cu_s[pl.ds(pl.multiple_of(r * SR, 8), SR + 8), :]
            return ext[8:], pltpu.roll(ext, 1, 0)[8:], pltpu.roll(ext, 2, 0)[8:]

        def gates(rows):
            ga = f32(jnp.concatenate([s2_ref[rows, 1024:1536], s3_ref[rows, 0:512]], axis=1))
            return _sigmoid(ga), _sigmoid(f32(s3_ref[rows, 512:1536]))

        keep = jnp.where(t > 0, 1.0, 0.0).astype(F32)
        cu_s[0:8, :] = f32(h1_ref[hb - 8:hb, 1024:1536]) * f32(h2_ref[hb - 8:hb, 0:512]) * keep

        def attn_gate_and_conv(r, rows):
            za = f32(s1_ref[rows, 0:512])
            attg_s[rows, :] = (f32(att_ref[rows, :]) * (za * _sigmoid(za))).astype(BF16)
            cu_s[pl.ds(pl.multiple_of(r * SR + 8, 8), SR), :] = f32(s1_ref[rows, 1024:1536]) * f32(s2_ref[rows, 0:512])
            cu, cu_m1, cu_m2 = conv_taps(r)
            vconv = w0 * cu_m2 + w1 * cu_m1 + w2 * cu + cb
            zc = f32(s2_ref[rows, 512:1024])
            cg_s[rows, :] = (f32(s1_ref[rows, 512:1024]) * vconv * (zc * _sigmoid(zc))).astype(BF16)

        strips(SR, attn_gate_and_conv)
        ya_s[...] = _dot(attg_s[...], wao_ref[...])
        yc_s[...] = _dot(cg_s[...], wco_ref[...])

        def merge(r, rows):
            sga, sgv = gates(rows)
            m_s[rows, :] = (sga * ya_s[rows, :] + sgv * yc_s[rows, :]).astype(BF16)

        strips(SR // 2, merge)
        mo_s[...] = _dot(m_s[...], wo_ref[...])

        def head(r, rows):
            x2 = x_ref[rows, :] + mo_s[rows, :]
            r2 = lax.rsqrt(jnp.mean(x2 * x2, axis=-1, keepdims=True) + EPS)
            xn2 = x2 * r2
            err = xn2 * g2 - t_ref[rows, :]
            sm1_ref[1:2, :] += jnp.sum(err * err, axis=0, keepdims=True) * (0.5 / D_MODEL)
            dy = err * (1.0 / D_MODEL)
            sm1_ref[0:1, :] += jnp.sum(dy * xn2, axis=0, keepdims=True)
            dxn = dy * g2
            dx2 = r2 * (dxn - xn2 * jnp.mean(dxn * xn2, axis=-1, keepdims=True))
            dx2_ref[rows, :] = dx2
            dx2b_s[rows, :] = dx2.astype(BF16)

        strips(SR // 2, head)
        dwo_acc[...] += _dot_tn(m_s[...], dx2b_s[...])
        dm_s[...] = _dot_nt(dx2b_s[...], wo_ref[...])

        def merge_bwd(r, rows):
            sga, sgv = gates(rows)
            dm = dm_s[rows, :]
            dya_s[rows, :] = (dm * sga).astype(BF16)
            dyc_s[rows, :] = (dm * sgv).astype(BF16)
            dg_ref[rows, 2560:3584] = (dm * ya_s[rows, :] * (sga * (1.0 - sga))).astype(BF16)
            dg_ref[rows, 3584:4608] = (dm * yc_s[rows, :] * (sgv * (1.0 - sgv))).astype(BF16)

        strips(SR // 2, merge_bwd)
        dwao_acc[...] += _dot_tn(attg_s[...], dya_s[...])
        dwco_acc[...] += _dot_tn(cg_s[...], dyc_s[...])
        dag_s[...] = _dot_nt(dya_s[...], wao_ref[...])
        dcg_s[...] = _dot_nt(dyc_s[...], wco_ref[...])

        def attn_gate_and_conv_bwd(r, rows):
            za = f32(s1_ref[rows, 0:512])
            sa = _sigmoid(za)
            dag = dag_s[rows, :]
            datt_ref[rows, :] = (dag * (za * sa)).astype(BF16)
            dg_ref[rows, 0:512] = (dag * f32(att_ref[rows, :]) * (sa * (1.0 + za * (1.0 - sa)))).astype(BF16)
            cu, cu_m1, cu_m2 = conv_taps(r)
            vconv = w0 * cu_m2 + w1 * cu_m1 + w2 * cu + cb
            zc = f32(s2_ref[rows, 512:1024])
            sc = _sigmoid(zc)
            silu_c = zc * sc
            gb = f32(s1_ref[rows, 512:1024])
            dcg = dcg_s[rows, :]
            dg_ref[rows, 512:1024] = (dcg * vconv * silu_c).astype(BF16)
            dg_ref[rows, 2048:2560] = (dcg * gb * vconv * (sc * (1.0 + zc * (1.0 - sc)))).astype(BF16)
            dv = dcg * gb * silu_c
            dv_s[rows, :] = dv
            sm2_ref[3:4, :] += jnp.sum(dv, axis=0, keepdims=True)
            sm2_ref[0:1, :] += jnp.sum(dv * cu_m2, axis=0, keepdims=True)
            sm2_ref[1:2, :] += jnp.sum(dv * cu_m1, axis=0, keepdims=True)
            sm2_ref[2:3, :] += jnp.sum(dv * cu, axis=0, keepdims=True)

        strips(SR, attn_gate_and_conv_bwd)

        def conv_bwd(r, rows):
            ext = dv_s[pl.ds(pl.multiple_of(r * SR, 8), SR + 8), :]
            dcu = w2 * ext[0:SR] + w1 * pltpu.roll(ext, SR + 7, 0)[0:SR] + w0 * pltpu.roll(ext, SR + 6, 0)[0:SR]
            dg_ref[rows, 1024:1536] = (dcu * f32(s2_ref[rows, 0:512])).astype(BF16)
            dg_ref[rows, 1536:2048] = (dcu * f32(s1_ref[rows, 1024:1536])).astype(BF16)

        strips(SR, conv_bwd)
        dv_s[ts:ts + 8, :] = dv_s[0:8, :]

        @pl.when(i == nt - 1)
        def _():
            dwo_ref[...] = dwo_acc[...].astype(BF16)
            for j in range(N_DEV):
                dproj_ref[j, :, 0:128] = dwao_acc[:, 128 * j:128 * (j + 1)].astype(BF16)
                dproj_ref[j, :, 128:256] = dwco_acc[:, 128 * j:128 * (j + 1)].astype(BF16)

    tile = lambda w: pl.BlockSpec((ts, w), lambda i: (nt - 1 - i, 0))
    seg = lambda c: pl.BlockSpec((ts, 1536), lambda i: (nt - 1 - i, c))
    halo = lambda c: pl.BlockSpec((hb, 1536), lambda i: (jnp.maximum((nt - 1 - i) * (ts // hb) - 1, 0), c))
    full = lambda a: pl.BlockSpec(a.shape, lambda i: (0,) * a.ndim)
    acc = lambda r, c: pl.BlockSpec((r, c), lambda i: (0, 0))
    return pl.pallas_call(
        body, name="token_local", grid=(nt,),
        out_shape=[jax.ShapeDtypeStruct((S, D_MODEL), F32), jax.ShapeDtypeStruct((S, GATE_COLS), BF16),
                   jax.ShapeDtypeStruct((S, D_ATT), BF16), jax.ShapeDtypeStruct((D_MODEL, D_MODEL), BF16),
                   jax.ShapeDtypeStruct(proj_g.shape, BF16),
                   jax.ShapeDtypeStruct((8, D_MODEL), F32), jax.ShapeDtypeStruct((8, D_CONV), F32)],
        in_specs=[tile(D_MODEL), tile(D_MODEL), seg(1), seg(2), seg(3), halo(1), halo(2), tile(D_ATT),
                  full(proj_g), full(w_out), full(cw_g), full(conv_b), full(final_g)],
        out_specs=[tile(D_MODEL), tile(GATE_COLS), tile(D_ATT), acc(D_MODEL, D_MODEL), full(proj_g),
                   acc(8, D_MODEL), acc(8, D_CONV)],
        scratch_shapes=[pltpu.VMEM((D_ATT, D_MODEL), BF16), pltpu.VMEM((D_CONV, D_MODEL), BF16),
                        pltpu.VMEM((8, D_CONV), F32), pltpu.VMEM((D_MODEL, D_MODEL), F32),
                        pltpu.VMEM((D_ATT, D_MODEL), F32), pltpu.VMEM((D_CONV, D_MODEL), F32),
                        pltpu.VMEM((ts + 8, D_CONV), F32), pltpu.VMEM((ts + 8, D_CONV), F32),
                        pltpu.VMEM((ts, D_ATT), BF16), pltpu.VMEM((ts, D_CONV), BF16),
                        pltpu.VMEM((ts, D_MODEL), F32), pltpu.VMEM((ts, D_MODEL), F32),
                        pltpu.VMEM((ts, D_MODEL), BF16), pltpu.VMEM((ts, D_MODEL), F32),
                        pltpu.VMEM((ts, D_MODEL), BF16), pltpu.VMEM((ts, D_MODEL), F32),
                        pltpu.VMEM((ts, D_MODEL), BF16), pltpu.VMEM((ts, D_MODEL), BF16),
                        pltpu.VMEM((ts, D_ATT), F32), pltpu.VMEM((ts, D_CONV), F32)],
        compiler_params=_params(1),
    )(x, tgt, P, P, P, P, P, att, proj_g, w_out, cw_g, conv_b, final_g)


def _attn_bwd(P, att, datt, lse, bias_tab):
    S = P.shape[0]
    nb = S // QB

    def body(q_ref, att_ref, datt_ref, lse_ref, p_hbm, bias_ref, out_ref, db_ref,
             kp, vp, dq_ring, dk_ring, dv_ring, sem):
        g = pl.program_id(0)

        @pl.when(g == 0)
        def _():
            _load_keys(p_hbm, kp, vp, sem)
            db_ref[...] = jnp.zeros_like(db_ref)
            dk_ring[...] = jnp.zeros_like(dk_ring)
            dv_ring[...] = jnp.zeros_like(dv_ring)

        s_new = g % 3
        s_mid = (g + 2) % 3
        s_old = (g + 1) % 3

        @pl.when(g < nb)
        def _():
            start = pl.multiple_of(g * QB, QB)
            lane = lax.broadcasted_iota(jnp.int32, (1, 128), 1)
            for p in range(HEADS // 2):
                cols = slice(128 * p, 128 * (p + 1))
                qp = q_ref[:, cols] * SCALE
                op = att_ref[:, cols].astype(F32)
                dop = datt_ref[:, cols]
                kpair = kp[pl.ds(start, KB), cols]
                vpair = vp[pl.ds(start, KB), cols]
                dqs = []
                dk_acc = jnp.zeros((KB, 128), F32)
                dv_acc = jnp.zeros((KB, 128), F32)
                for e in range(2):
                    h = 2 * p + e
                    lm = (lane < 64) if e == 0 else (lane >= 64)
                    qm = jnp.where(lm, qp, jnp.zeros_like(qp))
                    dom = jnp.where(lm, dop, jnp.zeros_like(dop))
                    s = _dot_nt(qm, kpair) * LOG2E + bias_ref[0, h]
                    pr = jnp.exp2(s - lse_ref[:, h:h + 1])
                    dp = _dot_nt(dom, vpair)
                    delta = jnp.sum(dom.astype(F32) * op, axis=-1, keepdims=True)
                    ds = pr * (dp - delta)
                    db_ref[h] += ds
                    dsb = ds.astype(BF16)
                    prb = pr.astype(BF16)
                    dqs.append(_dot(dsb, kpair) * SCALE)
                    dk_acc = dk_acc + _dot_tn(dsb, qm)
                    dv_acc = dv_acc + _dot_tn(prb, dom)
                dq_ring[s_new, :, cols] = jnp.where(lane < 64, dqs[0], dqs[1])
                dk_ring[s_old, :, cols] += dk_acc[0:QB]
                dk_ring[s_mid, :, cols] += dk_acc[QB:2 * QB]
                dk_ring[s_new, :, cols] = dk_acc[2 * QB:3 * QB]
                dv_ring[s_old, :, cols] += dv_acc[0:QB]
                dv_ring[s_mid, :, cols] += dv_acc[QB:2 * QB]
                dv_ring[s_new, :, cols] = dv_acc[2 * QB:3 * QB]

        @pl.when(g >= 2)
        def _():
            out_ref[:, 0:D_ATT] = dq_ring[s_old].astype(BF16)
            out_ref[:, D_ATT:2 * D_ATT] = dk_ring[s_old].astype(BF16)
            out_ref[:, 2 * D_ATT:3 * D_ATT] = dv_ring[s_old].astype(BF16)

    qblk = lambda w: pl.BlockSpec((QB, w), lambda g: (jnp.minimum(g, nb - 1), 0))
    return pl.pallas_call(
        body, name="attn_bwd", grid=(nb + 2,),
        out_shape=[jax.ShapeDtypeStruct((S, 3 * D_ATT), BF16), jax.ShapeDtypeStruct((HEADS, QB, KB), F32)],
        in_specs=[qblk(D_ATT), qblk(D_ATT), qblk(D_ATT), qblk(HEADS), ANY,
                  pl.BlockSpec((1, HEADS, QB, KB), lambda g: (jnp.minimum(g, 2), 0, 0, 0))],
        out_specs=[pl.BlockSpec((QB, 3 * D_ATT), lambda g: (jnp.maximum(g - 2, 0), 0)),
                   pl.BlockSpec((HEADS, QB, KB), lambda g: (0, 0, 0))],
        scratch_shapes=[pltpu.VMEM((S + PADK, D_ATT), BF16), pltpu.VMEM((S + PADK, D_ATT), BF16),
                        pltpu.VMEM((3, QB, D_ATT), F32), pltpu.VMEM((3, QB, D_ATT), F32),
                        pltpu.VMEM((3, QB, D_ATT), F32), pltpu.SemaphoreType.DMA((2,))],
        compiler_params=_params(1),
    )(P, att, datt, lse, P, bias_tab)


def _bias_fold(dscore):
    wide = 1024

    def body(d_ref, o_ref):
        sub = lax.broadcasted_iota(jnp.int32, (8, 1), 0)
        col = lax.broadcasted_iota(jnp.int32, (1, wide), 1)
        pad = jnp.zeros((8, wide - KB), F32)
        for h in range(HEADS):
            acc = jnp.concatenate([d_ref[h, 0:8, :], pad], axis=1)
            for qh in range(1, QB // 8):
                a = jnp.concatenate([d_ref[h, 8 * qh:8 * qh + 8, :], pad], axis=1)
                acc = acc + pltpu.roll(a, wide - 8 * qh, 1)
            for r in range(1, 8):
                acc = jnp.where(sub == r, pltpu.roll(acc, wide - r, 1), acc)
            vec = jnp.sum(acc, axis=0, keepdims=True)
            far = (col <= PADK - MAX_REL) | (col > KB)
            tail = jnp.sum(jnp.where(far, vec, 0.0), axis=-1, keepdims=True)
            o_ref[h:h + 1, :] = jnp.where(col == wide - 1, tail, vec)

    return pl.pallas_call(
        body, name="bias_fold",
        out_shape=jax.ShapeDtypeStruct((HEADS, wide), F32),
        compiler_params=pltpu.CompilerParams(vmem_limit_bytes=VMEM_LIMIT),
    )(dscore)


def _in_proj_bwd(x, norm_g, dx2, dqkv, dgate, w_in_g):
    S = x.shape[0]
    ts = 256

    def body(x_ref, g_ref, dx2_ref, dq_ref, dg_ref, w_ref, gx_ref, dn_ref):
        @pl.when(pl.program_id(0) == 0)
        def _():
            dn_ref[...] = jnp.zeros_like(dn_ref)

        dh = jnp.zeros((ts, D_MODEL), F32)
        for j in range(N_DEV):
            if j < 2:
                d = dq_ref[:, j * W_BLK:(j + 1) * W_BLK]
            else:
                d = dg_ref[:, (j - 2) * W_BLK:(j - 1) * W_BLK]
            dh = dh + _dot_nt(d, w_ref[j])
        xf = x_ref[...]
        r = lax.rsqrt(jnp.mean(xf * xf, axis=-1, keepdims=True) + EPS)
        xn = xf * r
        dn_ref[0:1, :] += jnp.sum(dh * xn, axis=0, keepdims=True)
        dhg = dh * g_ref[...]
        gx_ref[...] = dx2_ref[...] + r * (dhg - xn * jnp.mean(dhg * xn, axis=-1, keepdims=True))

    tile = lambda w: pl.BlockSpec((ts, w), lambda i: (i, 0))
    return pl.pallas_call(
        body, name="in_proj_bwd", grid=(S // ts,),
        out_shape=[jax.ShapeDtypeStruct((S, D_MODEL), F32), jax.ShapeDtypeStruct((8, D_MODEL), F32)],
        in_specs=[tile(D_MODEL), pl.BlockSpec((1, D_MODEL), lambda i: (0, 0)), tile(D_MODEL),
                  tile(3 * D_ATT), tile(GATE_COLS),
                  pl.BlockSpec((N_DEV, D_MODEL, W_BLK), lambda i: (0, 0, 0))],
        out_specs=[tile(D_MODEL), pl.BlockSpec((8, D_MODEL), lambda i: (0, 0))],
        compiler_params=_params(1),
    )(x, norm_g, dx2, dqkv, dgate, w_in_g)


SCATTER_MASKS = ((3, 4, 5, 2, 7, 6, 1, 0), (5, 2, 3, 4, 7, 6, 1, 0))


def _w_in_grad_scatter(ht, dqkv, dgate, d_proj, d_wo, pack, order):
    S = ht.shape[1]
    ts = min(S, 2048)
    nt = S // ts
    n_steps = 8

    def body(order_ref, ht_ref, dq_ref, dg_ref, proj_hbm, wo_hbm, pack_hbm, g_ref, rproj, rwo, rpack,
             acc, stage, rsib, rici, d2d_send, d2d_recv, ici_send, ici_recv, small_send, small_recv, local_sems):
        k, i = pl.program_id(0), pl.program_id(1)
        x, y, c = _mesh_pos()
        my = _flat((x, y, c))
        sibling = (x, y, 1 - c)
        owners = [(x ^ (1 - c), y ^ c, c), (x ^ c, y ^ (1 - c), c), (1 - x, 1 - y, c)]
        peers = [sibling, (1 - x, y, c), (x, 1 - y, c), (1 - x, 1 - y, c),
                 (1 - x, y, 1 - c), (x, 1 - y, 1 - c), (1 - x, 1 - y, 1 - c)]
        small = ((proj_hbm, rproj, True), (wo_hbm, rwo, True), (pack_hbm, rpack, False))
        n_small = len(small)

        def small_copy(kk, a, receive=False):
            src, dst, per_peer = small[a]
            slot = _flat(peers[kk]) if receive else my
            return pltpu.make_async_remote_copy(
                src_ref=src.at[_flat(peers[kk])] if per_peer else src, dst_ref=dst.at[slot],
                send_sem=small_send.at[kk, a], recv_sem=small_recv.at[kk, a],
                device_id=peers[kk], device_id_type=MESH)

        def to_sibling(t):
            return pltpu.make_async_remote_copy(
                src_ref=stage.at[0], dst_ref=rsib.at[t], send_sem=d2d_send.at[t], recv_sem=d2d_recv.at[t],
                device_id=sibling, device_id_type=MESH)

        def to_owner(t):
            return pltpu.make_async_remote_copy(
                src_ref=stage.at[1], dst_ref=rici.at[t], send_sem=ici_send.at[t], recv_sem=ici_recv.at[t],
                device_id=owners[t], device_id_type=MESH)

        own_small = [pltpu.make_async_copy(src.at[my] if per_peer else src, dst.at[my], local_sems.at[a])
                     for a, (src, dst, per_peer) in enumerate(small)]

        @pl.when((k == 0) & (i == 0))
        def _():
            for cp in own_small:
                cp.start()
            for kk in range(len(peers)):
                for a in range(n_small):
                    small_copy(kk, a).start()

        def accumulate(d_ref):
            prod = _dot(ht_ref[...], d_ref[...])

            @pl.when(i == 0)
            def _():
                acc[...] = prod

            @pl.when(i > 0)
            def _():
                acc[...] += prod

        @pl.when(order_ref[k] < 2)
        def _():
            accumulate(dq_ref)

        @pl.when(order_ref[k] >= 2)
        def _():
            accumulate(dg_ref)

        @pl.when(i == nt - 1)
        def _():
            for s in range(n_steps):
                @pl.when(k == s)
                def _():
                    t = s // 2
                    if s % 2 == 0:
                        if t >= 1:
                            to_sibling(t - 1).wait_send()
                        stage[0] = acc[...].astype(BF16)
                        to_sibling(t).start()
                    elif t < 3:
                        if t >= 1:
                            to_owner(t - 1).wait_send()
                        to_sibling(t).wait_recv()
                        stage[1] = (acc[...] + rsib[t].astype(F32)).astype(BF16)
                        to_owner(t).start()
                    else:
                        to_sibling(t).wait_recv()
                        total = acc[...] + rsib[t].astype(F32)
                        for j in range(3):
                            to_owner(j).wait_recv()
                            total = total + rici[j].astype(F32)
                        g_ref[...] = total
                        to_owner(2).wait_send()
                        to_sibling(3).wait_send()
                        for q in range(len(peers)):
                            for a in range(n_small):
                                small_copy(q, a).wait_send()
                                small_copy(q, a, receive=True).wait_recv()
                        for cp in own_small:
                            cp.wait()

    blk = (D_MODEL, W_BLK)
    grid_spec = pltpu.PrefetchScalarGridSpec(
        num_scalar_prefetch=1, grid=(n_steps, nt),
        in_specs=[pl.BlockSpec((D_MODEL, ts), lambda k, i, o: (0, i)),
                  pl.BlockSpec((ts, W_BLK), lambda k, i, o: (i, jnp.minimum(o[k], 1))),
                  pl.BlockSpec((ts, W_BLK), lambda k, i, o: (i, jnp.maximum(o[k] - 2, 0))),
                  ANY, ANY, ANY],
        out_specs=[pl.BlockSpec(blk, lambda k, i, o: (0, 0)), ANY, ANY, ANY],
        scratch_shapes=[pltpu.VMEM(blk, F32), pltpu.VMEM((2,) + blk, BF16),
                        pltpu.VMEM((4,) + blk, BF16), pltpu.VMEM((3,) + blk, BF16),
                        pltpu.SemaphoreType.DMA((4,)), pltpu.SemaphoreType.DMA((4,)),
                        pltpu.SemaphoreType.DMA((3,)), pltpu.SemaphoreType.DMA((3,)),
                        pltpu.SemaphoreType.DMA((7, 3)), pltpu.SemaphoreType.DMA((7, 3)),
                        pltpu.SemaphoreType.DMA((3,))])
    return pl.pallas_call(
        body, name="w_in_grad_scatter", grid_spec=grid_spec,
        out_shape=[jax.ShapeDtypeStruct(blk, F32),
                   jax.ShapeDtypeStruct(d_proj.shape, BF16), jax.ShapeDtypeStruct(d_wo.shape, BF16),
                   jax.ShapeDtypeStruct((N_DEV,) + pack.shape, F32)],
        compiler_params=_params(2),
    )(order, ht, dqkv, dgate, d_proj, d_wo, pack)


def _adamw(w, g, m, v):
    m = ADAM_B1 * m + (1.0 - ADAM_B1) * g
    v = ADAM_B2 * v + (1.0 - ADAM_B2) * (g * g)
    m_hat = m / (1.0 - ADAM_B1 ** ADAM_STEP)
    v_hat = v / (1.0 - ADAM_B2 ** ADAM_STEP)
    delta = -ADAM_LR * (m_hat / (jnp.sqrt(v_hat) + ADAM_EPS) + ADAM_WD * w)
    return delta, m, v


def _sum_adamw(parts, w, m, v, name):
    R, C = w.shape
    n = parts.shape[0]
    tr = min(R, 256)

    def body(p_ref, w_ref, m_ref, v_ref, g_ref, d_ref, nm_ref, nv_ref):
        g = p_ref[0].astype(F32)
        for s in range(1, n):
            g = g + p_ref[s].astype(F32)
        g_ref[...] = g
        d_ref[...], nm_ref[...], nv_ref[...] = _adamw(w_ref[...], g, m_ref[...], v_ref[...])

    tile = pl.BlockSpec((tr, C), lambda i: (i, 0))
    return pl.pallas_call(
        body, name=name, grid=(R // tr,),
        out_shape=[jax.ShapeDtypeStruct((R, C), F32)] * 4,
        in_specs=[pl.BlockSpec((n, tr, C), lambda i: (0, i, 0)), tile, tile, tile],
        out_specs=[tile] * 4,
        compiler_params=_params(1),
    )(parts, w, m, v)


def _adamw_mid(r_proj, r_wo, params):
    def body(rp_ref, rw_ref, *refs):
        ins, outs = refs[:9], refs[9:]

        def total(part):
            g = part(0).astype(F32)
            for s in range(1, N_DEV):
                g = g + part(s).astype(F32)
            return g

        grads = (total(lambda s: rp_ref[s, :, 0:128]), total(lambda s: rp_ref[s, :, 128:256]),
                 total(lambda s: rw_ref[s]))
        for n, g in enumerate(grads):
            w, m, v = (r[...] for r in ins[3 * n:3 * n + 3])
            outs[4 * n][...] = g
            outs[4 * n + 1][...], outs[4 * n + 2][...], outs[4 * n + 3][...] = _adamw(w, g, m, v)

    return pl.pallas_call(
        body, name="adamw_mid",
        out_shape=[jax.ShapeDtypeStruct(params[3 * n].shape, F32) for n in range(3) for _ in range(4)],
        compiler_params=pltpu.CompilerParams(vmem_limit_bytes=VMEM_LIMIT),
    )(r_proj, r_wo, *params)


def _adamw_small(r_pack, params):
    wide = 384

    def body(p_ref, *refs):
        ins, loss_ref, outs = refs[:15], refs[15], refs[16:]
        tot = p_ref[0]
        for s in range(1, N_DEV):
            tot = tot + p_ref[s]
        me = _flat(_mesh_pos())
        loss_ref[...] = jnp.sum(tot[2:3, :], axis=-1, keepdims=True)
        mine = pltpu.roll(tot[0:8, 0:D_CONV], (D_CONV - 64 * me) % D_CONV, 1)
        col = lax.broadcasted_iota(jnp.int32, (D_MODEL, wide), 0)
        idx = lax.broadcasted_iota(jnp.int32, (D_MODEL, wide), 1)
        near = (idx > MAX_REL - CHUNK) & (idx < 2 * MAX_REL) & (col == PADK + MAX_REL - idx)
        far = (idx == 2 * MAX_REL) & (col == D_MODEL - 1)
        perm = jnp.where(near | far, 1.0, 0.0).astype(F32)
        g_rel = jnp.dot(tot[8:16], perm, precision=lax.Precision.HIGHEST, preferred_element_type=F32)
        grads = (tot[0:1], tot[1:2], mine[3:6, 0:64], tot[6:7, 0:D_CONV], g_rel[:, 0:N_REL])
        for n, g in enumerate(grads):
            w, m, v = (r[...] for r in ins[3 * n:3 * n + 3])
            outs[4 * n][...] = g
            outs[4 * n + 1][...], outs[4 * n + 2][...], outs[4 * n + 3][...] = _adamw(w, g, m, v)

    return pl.pallas_call(
        body, name="adamw_small",
        out_shape=[jax.ShapeDtypeStruct((1, 1), F32)]
        + [jax.ShapeDtypeStruct(params[3 * n].shape, F32) for n in range(5) for _ in range(4)],
    )(r_pack, *params)


def _pad_row(a, width=D_MODEL):
    a = a.reshape(-1, a.shape[-1])
    return jnp.pad(a, ((0, 0), (0, width - a.shape[-1])))


def kernel(x, norm_g, w_in, rel_bias, w_att_out, conv_w, conv_b, w_conv_out, w_out, final_norm_g, loss_target, m_norm_g, m_w_in, m_rel_bias, m_w_att_out, m_conv_w, m_conv_b, m_w_conv_out, m_w_out, m_final_norm_g, v_norm_g, v_w_in, v_rel_bias, v_w_att_out, v_conv_w, v_conv_b, v_w_conv_out, v_w_out, v_final_norm_g):
    S = x.shape[1]
    x2d = x.reshape(S, D_MODEL)
    tgt = loss_target.reshape(S, D_MODEL)
    me = 4 * lax.axis_index("x") + 2 * lax.axis_index("y") + lax.axis_index("c")
    row = lambda a: a.reshape(1, D_MODEL)

    proj_sh = jnp.concatenate([w_att_out[0], w_conv_out[0]], axis=1).astype(BF16)
    cw_sh = jnp.pad(conv_w[0], ((0, 5), (0, 64)))
    P, ht, w_in_g, proj_g, w_out_g, cw_g = _gather_in_proj(
        x2d, norm_g, w_in[0].astype(BF16), [proj_sh, w_out[0].astype(BF16), cw_sh],
        me ^ _by_core(GATHER_MASKS))

    bias_tab = _bias_table(rel_bias[0])
    att, lse = _attn_fwd(P, bias_tab)
    dx2, dgate, datt, d_wo, d_proj, sm1, sm2 = _token_local(
        x2d, tgt, P, att, proj_g, w_out_g.reshape(D_MODEL, D_MODEL), cw_g, conv_b, row(final_norm_g))
    dqkv, dscore = _attn_bwd(P, att, datt, lse, bias_tab)
    dbias = _bias_fold(dscore)
    grad_x, dnorm = _in_proj_bwd(x2d, norm_g, dx2, dqkv, dgate, w_in_g)

    pack = jnp.concatenate([dnorm[0:1], sm1[0:2], _pad_row(sm2[0:4]), jnp.zeros((1, D_MODEL), F32), dbias],
                           axis=0)
    g_win_sum, r_proj, r_wo, r_pack = _w_in_grad_scatter(
        ht, dqkv, dgate, d_proj, d_wo.reshape(N_DEV, 128, D_MODEL), pack, me ^ _by_core(SCATTER_MASKS))

    res = {"w_in": _sum_adamw(g_win_sum[None], w_in[0], m_w_in[0], v_w_in[0], "adamw_w_in")}
    mid = _adamw_mid(r_proj, r_wo, (w_att_out[0], m_w_att_out[0], v_w_att_out[0],
                                    w_conv_out[0], m_w_conv_out[0], v_w_conv_out[0],
                                    w_out[0], m_w_out[0], v_w_out[0]))
    for n, name in enumerate(("w_att_out", "w_conv_out", "w_out")):
        res[name] = mid[4 * n:4 * n + 4]
    small = _adamw_small(r_pack, (norm_g, m_norm_g, v_norm_g,
                                  row(final_norm_g), row(m_final_norm_g), row(v_final_norm_g),
                                  conv_w[0], m_conv_w[0], v_conv_w[0], conv_b, m_conv_b, v_conv_b,
                                  rel_bias[0], m_rel_bias[0], v_rel_bias[0]))
    loss = small[0].reshape(())
    for n, name in enumerate(("norm_g", "final_norm_g", "conv_w", "conv_b", "rel_bias")):
        res[name] = small[1 + 4 * n:5 + 4 * n]

    leading = {"norm_g": (1, D_MODEL), "final_norm_g": (D_MODEL,), "conv_b": (1, D_CONV)}
    outs = []
    for kind in range(4):
        for name in ("norm_g", "w_in", "rel_bias", "w_att_out", "conv_w", "conv_b", "w_conv_out", "w_out",
                     "final_norm_g"):
            a = res[name][kind]
            outs.append(a.reshape(leading[name]) if name in leading else a[None])
    return (loss, grad_x.reshape(1, S, D_MODEL), *outs)
```

```python
import functools

import numpy as np
import jax
import jax.numpy as jnp
from jax import lax
from jax.experimental import pallas as pl
from jax.experimental.pallas import tpu as pltpu

F32 = jnp.float32
BF16 = jnp.bfloat16

D_MODEL = 1024
CHUNK = 64
N_LEFT = 8
HEADS = 8
D_ATT = 512
D_CONV = 512
MAX_REL = 128
N_REL = 2 * MAX_REL + 1
IN_COLS = 6144
EPS = 1e-6
NEG_BIG = -1e30
N_DEV = 8
W_BLK = IN_COLS // N_DEV
QB = 4 * CHUNK
KB = QB + N_LEFT * CHUNK
PADK = N_LEFT * CHUNK
SCALE = 64 ** -0.5
LOG2E = 1.4426950408889634
GATE_COLS = IN_COLS - 3 * D_ATT

ADAM_LR = 0.001
ADAM_B1 = 0.9
ADAM_B2 = 0.999
ADAM_EPS = 1e-08
ADAM_WD = 0.01
ADAM_STEP = 10

VMEM_LIMIT = 56 * 1024 * 1024

MESH = pl.DeviceIdType.MESH
ANY = pl.BlockSpec(memory_space=pl.ANY)


def _params(n_grid):
    return pltpu.CompilerParams(dimension_semantics=("arbitrary",) * n_grid,
                                vmem_limit_bytes=VMEM_LIMIT)


def _dot(a, b):
    return jnp.dot(a, b, preferred_element_type=F32)


def _dot_nt(a, b):
    return lax.dot_general(a, b, (((1,), (1,)), ((), ())), preferred_element_type=F32)


def _dot_tn(a, b):
    return lax.dot_general(a, b, (((0,), (0,)), ((), ())), preferred_element_type=F32)


def _sigmoid(z):
    return 0.5 * jnp.tanh(0.5 * z) + 0.5


def _mesh_pos():
    return lax.axis_index("x"), lax.axis_index("y"), lax.axis_index("c")


def _flat(p):
    return 4 * p[0] + 2 * p[1] + p[2]


def _by_core(masks):
    m0, m1 = (jnp.array(m, jnp.int32) for m in masks)
    return jnp.where(lax.axis_index("c") == 0, m0, m1)


GATHER_MASKS = ((0, 1, 4, 3, 2, 5, 6, 7), (0, 1, 2, 5, 4, 3, 6, 7))


def _gather_in_proj(x, norm_g, w_sh, smalls, order):
    S = x.shape[0]
    ts = 1024
    nt = S // ts
    n_small = len(smalls)
    n_steps = N_DEV

    def body(order_ref, x_ref, g_ref, w_hbm, *rest):
        small_in = rest[:n_small]
        p_ref, ht_ref, wg_hbm = rest[n_small:n_small + 3]
        small_out = rest[n_small + 3:2 * n_small + 3]
        (wbuf, hbuf, own_sem, send_sems, recv_sems, out_sems,
         small_send, small_recv, small_local) = rest[2 * n_small + 3:]
        k, i = pl.program_id(0), pl.program_id(1)
        x_, y_, c_ = _mesh_pos()
        me, sibling = (x_, y_, c_), (x_, y_, 1 - c_)
        my = _flat(me)
        chips = [(x_ ^ (1 - c_), y_ ^ c_), (x_ ^ c_, y_ ^ (1 - c_)), (1 - x_, 1 - y_)]
        peers = [sibling] + [(*chip, c_) for chip in chips] + [(*chip, 1 - c_) for chip in chips]

        def wcopy(sem, block, to, from_input=False):
            dst = wbuf.at[_flat(block)]
            return pltpu.make_async_remote_copy(
                src_ref=w_hbm if from_input else dst, dst_ref=dst,
                send_sem=send_sems.at[sem], recv_sem=recv_sems.at[sem], device_id=to, device_id_type=MESH)

        def small_copy(q, a, receive=False):
            slot = _flat(peers[q]) if receive else my
            return pltpu.make_async_remote_copy(
                src_ref=small_in[a], dst_ref=small_out[a].at[slot],
                send_sem=small_send.at[q, a], recv_sem=small_recv.at[q, a],
                device_id=peers[q], device_id_type=MESH)

        def keep(step, block):
            return pltpu.make_async_copy(wbuf.at[_flat(block)], wg_hbm.at[_flat(block)], out_sems.at[step])

        own = pltpu.make_async_copy(w_hbm, wbuf.at[my], own_sem)
        small_own = [pltpu.make_async_copy(small_in[a], small_out[a].at[my], small_local.at[a])
                     for a in range(n_small)]
        passed_on = [(*chips[1], 1 - c_), (*chips[0], 1 - c_), (*chips[2], 1 - c_)]
        arrivals = [me, sibling]
        for j in range(3):
            arrivals += [(*chips[j], c_), passed_on[j]]

        @pl.when(i == 0)
        def _():
            for kk in range(n_steps):
                @pl.when(k == kk)
                def _():
                    j = kk // 2 - 1
                    if kk == 0:
                        own.start()
                        wcopy(0, me, sibling, True).start()
                        wcopy(1, me, (*chips[0], c_), True).start()
                        own.wait()
                    elif kk == 1:
                        wcopy(0, sibling, me).wait_recv()
                        wcopy(2, me, (*chips[1], c_), True).start()
                    elif kk % 2 == 0:
                        wcopy(1 + j, (*chips[j], c_), me).wait_recv()
                        wcopy(4 + j, (*chips[j], c_), sibling).start()
                        if kk == 2:
                            wcopy(3, me, (*chips[2], c_), True).start()
                    else:
                        wcopy(4 + j, passed_on[j], me).wait_recv()
                        if kk == 3:
                            for cp in small_own:
                                cp.start()
                            for q in range(len(peers)):
                                for a in range(n_small):
                                    small_copy(q, a).start()
                    keep(kk, arrivals[kk]).start()

        row0 = pl.multiple_of(i * ts, ts)

        @pl.when(k == 0)
        def _():
            xf = x_ref[...]
            r = lax.rsqrt(jnp.mean(xf * xf, axis=-1, keepdims=True) + EPS)
            hf = (xf * r) * g_ref[...]
            hbuf[pl.ds(row0, ts), :] = hf.astype(BF16)
            ht_ref[...] = hf.T.astype(BF16)

        p_ref[...] = _dot(hbuf[pl.ds(row0, ts), :], wbuf[order_ref[k]]).astype(BF16)

        @pl.when((k == n_steps - 1) & (i == nt - 1))
        def _():
            wcopy(0, me, sibling, True).wait_send()
            for j, chip in enumerate(chips):
                wcopy(1 + j, me, (*chip, c_), True).wait_send()
                wcopy(4 + j, (*chip, c_), sibling).wait_send()
            for kk in range(n_steps):
                keep(kk, arrivals[kk]).wait()
            for cp in small_own:
                cp.wait()
            for q in range(len(peers)):
                for a in range(n_small):
                    small_copy(q, a).wait_send()
                    small_copy(q, a, receive=True).wait_recv()

    first_pass = lambda k, i: jnp.where(k == 0, i, nt - 1)
    grid_spec = pltpu.PrefetchScalarGridSpec(
        num_scalar_prefetch=1, grid=(n_steps, nt),
        in_specs=[pl.BlockSpec((ts, D_MODEL), lambda k, i, o: (first_pass(k, i), 0)),
                  pl.BlockSpec((1, D_MODEL), lambda k, i, o: (0, 0)), ANY] + [ANY] * n_small,
        out_specs=[pl.BlockSpec((ts, W_BLK), lambda k, i, o: (i, o[k])),
                   pl.BlockSpec((D_MODEL, ts), lambda k, i, o: (0, first_pass(k, i))), ANY] + [ANY] * n_small,
        scratch_shapes=[pltpu.VMEM((N_DEV, D_MODEL, W_BLK), BF16), pltpu.VMEM((S, D_MODEL), BF16),
                        pltpu.SemaphoreType.DMA, pltpu.SemaphoreType.DMA((7,)), pltpu.SemaphoreType.DMA((7,)),
                        pltpu.SemaphoreType.DMA((n_steps,)),
                        pltpu.SemaphoreType.DMA((7, n_small)), pltpu.SemaphoreType.DMA((7, n_small)),
                        pltpu.SemaphoreType.DMA((n_small,))])
    return pl.pallas_call(
        body, name="gather_in_proj", grid_spec=grid_spec,
        out_shape=[jax.ShapeDtypeStruct((S, IN_COLS), BF16), jax.ShapeDtypeStruct((D_MODEL, S), BF16),
                   jax.ShapeDtypeStruct((N_DEV,) + w_sh.shape, BF16)]
        + [jax.ShapeDtypeStruct((N_DEV,) + s.shape, s.dtype) for s in smalls],
        compiler_params=_params(2),
    )(order, x, norm_g, w_sh, *smalls)


def _bias_table(rel_bias):
    wide = 1024

    def body(r_ref, o_ref):
        h = pl.program_id(0)
        col = lax.broadcasted_iota(jnp.int32, (1, wide), 1)
        k_minus_q = jnp.where(col < KB, col, col - wide)
        idx = jnp.clip(PADK - k_minus_q, -MAX_REL, MAX_REL) + MAX_REL
        f = jnp.zeros((1, wide), F32)
        for r in range(MAX_REL - CHUNK + 1, N_REL):
            f = jnp.where(idx == r, r_ref[h, r], f)
        kcol = lax.broadcasted_iota(jnp.int32, (1, KB), 1)
        kc = kcol >> 6
        sub = lax.broadcasted_iota(jnp.int32, (8, 1), 0)
        f8 = jnp.broadcast_to(f * LOG2E, (8, wide))
        base = f8
        for r in range(1, 8):
            base = jnp.where(sub == r, pltpu.roll(f8, r, 1), base)
        for qh in range(QB // 8):
            rows = (pltpu.roll(base, 8 * qh, 1) if qh else base)[:, 0:KB]
            qc = (8 * qh) // CHUNK
            band = (kc >= qc) & (kc <= qc + N_LEFT)
            for t in range(3):
                o_ref[t, 0, 8 * qh:8 * qh + 8, :] = jnp.where(band & (kcol >= PADK - t * QB), rows, NEG_BIG)

    return pl.pallas_call(
        body, name="bias_table", grid=(HEADS,),
        out_shape=jax.ShapeDtypeStruct((3, HEADS, QB, KB), F32),
        in_specs=[pl.BlockSpec(memory_space=pltpu.SMEM)],
        out_specs=pl.BlockSpec((3, 1, QB, KB), lambda h: (0, h, 0, 0)),
        compiler_params=_params(1),
    )(rel_bias)


def _load_keys(p_hbm, kp, vp, sem):
    kp[0:PADK, :] = jnp.zeros((PADK, D_ATT), BF16)
    vp[0:PADK, :] = jnp.zeros((PADK, D_ATT), BF16)
    S = p_hbm.shape[0]
    ck = pltpu.make_async_copy(p_hbm.at[:, D_ATT:2 * D_ATT], kp.at[PADK:PADK + S, :], sem.at[0])
    cv = pltpu.make_async_copy(p_hbm.at[:, 2 * D_ATT:3 * D_ATT], vp.at[PADK:PADK + S, :], sem.at[1])
    ck.start()
    cv.start()
    ck.wait()
    cv.wait()


def _attn_fwd(P, bias_tab):
    S = P.shape[0]
    nb = S // QB

    def body(q_ref, p_hbm, bias_ref, o_ref, lse_ref, kp, vp, sem):
        g = pl.program_id(0)

        @pl.when(g == 0)
        def _():
            _load_keys(p_hbm, kp, vp, sem)

        start = pl.multiple_of(g * QB, QB)
        lane = lax.broadcasted_iota(jnp.int32, (1, 128), 1)
        for p in range(HEADS // 2):
            cols = slice(128 * p, 128 * (p + 1))
            qp = q_ref[:, cols] * SCALE
            kpair = kp[pl.ds(start, KB), cols]
            vpair = vp[pl.ds(start, KB), cols]
            outs = []
            for e in range(2):
                h = 2 * p + e
                lm = (lane < 64) if e == 0 else (lane >= 64)
                qm = jnp.where(lm, qp, jnp.zeros_like(qp))
                s = (_dot_nt(qm, kpair) * LOG2E + bias_ref[0, h]).astype(BF16)
                mx = jnp.max(s, axis=-1, keepdims=True)
                ex = jnp.exp2(s - mx)
                o = _dot(ex, jnp.where(lm, vpair, jnp.ones_like(vpair)))
                sums = pltpu.roll(o, 64, 1)
                outs.append(o / sums)
                lse_ref[:, h:h + 1] = mx.astype(F32) + jnp.log2(sums[:, 0:1] if e == 0 else o[:, 0:1])
            o_ref[:, cols] = jnp.where(lane < 64, outs[0], outs[1]).astype(BF16)

    return pl.pallas_call(
        body, name="attn_fwd", grid=(nb,),
        out_shape=[jax.ShapeDtypeStruct((S, D_ATT), BF16), jax.ShapeDtypeStruct((S, HEADS), F32)],
        in_specs=[pl.BlockSpec((QB, D_ATT), lambda g: (g, 0)), ANY,
                  pl.BlockSpec((1, HEADS, QB, KB), lambda g: (jnp.minimum(g, 2), 0, 0, 0))],
        out_specs=[pl.BlockSpec((QB, D_ATT), lambda g: (g, 0)),
                   pl.BlockSpec((QB, HEADS), lambda g: (g, 0))],
        scratch_shapes=[pltpu.VMEM((S + PADK, D_ATT), BF16), pltpu.VMEM((S + PADK, D_ATT), BF16),
                        pltpu.SemaphoreType.DMA((2,))],
        compiler_params=_params(1),
    )(P, P, bias_tab)


def _token_local(x, tgt, P, att, proj_g, w_out, cw_g, conv_b, final_g):
    S = x.shape[0]
    ts = 256
    nt = S // ts
    hb = 16

    def body(x_ref, t_ref, s1_ref, s2_ref, s3_ref, h1_ref, h2_ref, att_ref,
             pg_ref, wo_ref, cwg_ref, cb_ref, g2_ref,
             dx2_ref, dg_ref, datt_ref, dwo_ref, dproj_ref, sm1_ref, sm2_ref,
             carry, wao_ref, wco_ref, cw_ref, dwo_acc, dwao_acc, dwco_acc):
        i = pl.program_id(0)
        t = nt - 1 - i

        @pl.when(i == 0)
        def _():
            dwo_acc[...] = jnp.zeros_like(dwo_acc)
            dwao_acc[...] = jnp.zeros_like(dwao_acc)
            dwco_acc[...] = jnp.zeros_like(dwco_acc)
            lane = lax.broadcasted_iota(jnp.int32, (1, 128), 1)
            for j in range(N_DEV):
                wao_ref[:, 128 * j:128 * (j + 1)] = pg_ref[j, :, 0:128]
                wco_ref[:, 128 * j:128 * (j + 1)] = pg_ref[j, :, 128:256]
            for p in range(N_DEV // 2):
                cw_ref[:, 128 * p:128 * (p + 1)] = jnp.where(
                    lane < 64, cwg_ref[2 * p], pltpu.roll(cwg_ref[2 * p + 1], 64, 1))
            sm1_ref[...] = jnp.zeros_like(sm1_ref)
            sm2_ref[...] = jnp.zeros_like(sm2_ref)
            carry[...] = jnp.zeros_like(carry)

        za = s1_ref[:, 0:512].astype(F32)
        gb = s1_ref[:, 512:1024].astype(F32)
        gc = s1_ref[:, 1024:1536].astype(F32)
        u = s2_ref[:, 0:512].astype(F32)
        zc = s2_ref[:, 512:1024].astype(F32)
        ga = jnp.concatenate([s2_ref[:, 1024:1536], s3_ref[:, 0:512]], axis=1).astype(F32)
        gv = s3_ref[:, 512:1536].astype(F32)
        att = att_ref[...].astype(F32)
        row = lax.broadcasted_iota(jnp.int32, (ts, 1), 0)

        sa = _sigmoid(za)
        silu_a = za * sa
        att_g = (att * silu_a).astype(BF16)
        y_att = _dot(att_g, wao_ref[...])

        cu = gc * u
        keep = jnp.where(t > 0, 1.0, 0.0).astype(F32)
        hcu = (h1_ref[:, 1024:1536].astype(F32) * h2_ref[:, 0:512].astype(F32)) * keep
        cu_m1 = jnp.where(row == 0, hcu[hb - 1:hb, :], pltpu.roll(cu, 1, 0))
        cu_m2 = jnp.where(row == 0, hcu[hb - 2:hb - 1, :],
                          jnp.where(row == 1, hcu[hb - 1:hb, :], pltpu.roll(cu, 2, 0)))
        w0, w1, w2 = cw_ref[0:1, :], cw_ref[1:2, :], cw_ref[2:3, :]
        vconv = w0 * cu_m2 + w1 * cu_m1 + w2 * cu + cb_ref[...]
        sc = _sigmoid(zc)
        silu_c = zc * sc
        cg = (gb * vconv * silu_c).astype(BF16)
        y_conv = _dot(cg, wco_ref[...])

        sga = _sigmoid(ga)
        sgv = _sigmoid(gv)
        m = (sga * y_att + sgv * y_conv).astype(BF16)
        x2 = x_ref[...] + _dot(m, wo_ref[...])
        r2 = lax.rsqrt(jnp.mean(x2 * x2, axis=-1, keepdims=True) + EPS)
        xn2 = x2 * r2
        g2 = g2_ref[...]
        err = xn2 * g2 - t_ref[...]
        sm1_ref[1:2, :] += jnp.sum(err * err, axis=0, keepdims=True) * (0.5 / D_MODEL)

        dy = err * (1.0 / D_MODEL)
        sm1_ref[0:1, :] += jnp.sum(dy * xn2, axis=0, keepdims=True)
        dxn = dy * g2
        dx2 = r2 * (dxn - xn2 * jnp.mean(dxn * xn2, axis=-1, keepdims=True))
        dx2_ref[...] = dx2
        dx2b = dx2.astype(BF16)
        dwo_acc[...] += _dot_tn(m, dx2b)
        dm = _dot_nt(dx2b, wo_ref[...])
        dya = (dm * sga).astype(BF16)
        dyc = (dm * sgv).astype(BF16)
        dg_ref[:, 2560:3584] = (dm * y_att * (sga * (1.0 - sga))).astype(BF16)
        dg_ref[:, 3584:4608] = (dm * y_conv * (sgv * (1.0 - sgv))).astype(BF16)
        dwao_acc[...] += _dot_tn(att_g, dya)
        dwco_acc[...] += _dot_tn(cg, dyc)
        datt_g = _dot_nt(dya, wao_ref[...])
        dcg = _dot_nt(dyc, wco_ref[...])
        datt_ref[...] = (datt_g * silu_a).astype(BF16)
        dg_ref[:, 0:512] = (datt_g * att * (sa + silu_a * (1.0 - sa))).astype(BF16)
        dg_ref[:, 512:1024] = (dcg * vconv * silu_c).astype(BF16)
        dg_ref[:, 2048:2560] = (dcg * gb * vconv * (sc + silu_c * (1.0 - sc))).astype(BF16)
        dv = dcg * gb * silu_c
        sm2_ref[3:4, :] += jnp.sum(dv, axis=0, keepdims=True)
        sm2_ref[0:1, :] += jnp.sum(dv * cu_m2, axis=0, keepdims=True)
        sm2_ref[1:2, :] += jnp.sum(dv * cu_m1, axis=0, keepdims=True)
        sm2_ref[2:3, :] += jnp.sum(dv * cu, axis=0, keepdims=True)
        nxt = carry[...]
        dv_p1 = jnp.where(row == ts - 1, nxt[0:1, :], pltpu.roll(dv, ts - 1, 0))
        dv_p2 = jnp.where(row == ts - 1, nxt[1:2, :],
                          jnp.where(row == ts - 2, nxt[0:1, :], pltpu.roll(dv, ts - 2, 0)))
        dcu = w2 * dv + w1 * dv_p1 + w0 * dv_p2
        carry[...] = dv[0:8, :]
        dg_ref[:, 1024:1536] = (dcu * u).astype(BF16)
        dg_ref[:, 1536:2048] = (dcu * gc).astype(BF16)

        @pl.when(i == nt - 1)
        def _():
            dwo_ref[...] = dwo_acc[...].astype(BF16)
            for j in range(N_DEV):
                dproj_ref[j, :, 0:128] = dwao_acc[:, 128 * j:128 * (j + 1)].astype(BF16)
                dproj_ref[j, :, 128:256] = dwco_acc[:, 128 * j:128 * (j + 1)].astype(BF16)

    tile = lambda w: pl.BlockSpec((ts, w), lambda i: (nt - 1 - i, 0))
    seg = lambda c: pl.BlockSpec((ts, 1536), lambda i: (nt - 1 - i, c))
    halo = lambda c: pl.BlockSpec((hb, 1536), lambda i: (jnp.maximum((nt - 1 - i) * (ts // hb) - 1, 0), c))
    full = lambda a: pl.BlockSpec(a.shape, lambda i: (0,) * a.ndim)
    acc = lambda r, c: pl.BlockSpec((r, c), lambda i: (0, 0))
    return pl.pallas_call(
        body, name="token_local", grid=(nt,),
        out_shape=[jax.ShapeDtypeStruct((S, D_MODEL), F32), jax.ShapeDtypeStruct((S, GATE_COLS), BF16),
                   jax.ShapeDtypeStruct((S, D_ATT), BF16), jax.ShapeDtypeStruct((D_MODEL, D_MODEL), BF16),
                   jax.ShapeDtypeStruct(proj_g.shape, BF16),
                   jax.ShapeDtypeStruct((8, D_MODEL), F32), jax.ShapeDtypeStruct((8, D_CONV), F32)],
        in_specs=[tile(D_MODEL), tile(D_MODEL), seg(1), seg(2), seg(3), halo(1), halo(2), tile(D_ATT),
                  full(proj_g), full(w_out), full(cw_g), full(conv_b), full(final_g)],
        out_specs=[tile(D_MODEL), tile(GATE_COLS), tile(D_ATT), acc(D_MODEL, D_MODEL), full(proj_g),
                   acc(8, D_MODEL), acc(8, D_CONV)],
        scratch_shapes=[pltpu.VMEM((8, D_CONV), F32),
                        pltpu.VMEM((D_ATT, D_MODEL), BF16), pltpu.VMEM((D_CONV, D_MODEL), BF16),
                        pltpu.VMEM((8, D_CONV), F32), pltpu.VMEM((D_MODEL, D_MODEL), F32),
                        pltpu.VMEM((D_ATT, D_MODEL), F32), pltpu.VMEM((D_CONV, D_MODEL), F32)],
        compiler_params=_params(1),
    )(x, tgt, P, P, P, P, P, att, proj_g, w_out, cw_g, conv_b, final_g)


def _attn_bwd(P, att, datt, lse, bias_tab):
    S = P.shape[0]
    nb = S // QB

    def body(q_ref, att_ref, datt_ref, lse_ref, p_hbm, bias_ref, out_ref, db_ref,
             kp, vp, dq_ring, dk_ring, dv_ring, sem):
        g = pl.program_id(0)

        @pl.when(g == 0)
        def _():
            _load_keys(p_hbm, kp, vp, sem)
            db_ref[...] = jnp.zeros_like(db_ref)
            dk_ring[...] = jnp.zeros_like(dk_ring)
            dv_ring[...] = jnp.zeros_like(dv_ring)

        s_new = g % 3
        s_mid = (g + 2) % 3
        s_old = (g + 1) % 3

        @pl.when(g < nb)
        def _():
            start = pl.multiple_of(g * QB, QB)
            lane = lax.broadcasted_iota(jnp.int32, (1, 128), 1)
            for p in range(HEADS // 2):
                cols = slice(128 * p, 128 * (p + 1))
                qp = q_ref[:, cols] * SCALE
                op = att_ref[:, cols].astype(F32)
                dop = datt_ref[:, cols]
                kpair = kp[pl.ds(start, KB), cols]
                vpair = vp[pl.ds(start, KB), cols]
                dqs = []
                dk_acc = jnp.zeros((KB, 128), F32)
                dv_acc = jnp.zeros((KB, 128), F32)
                for e in range(2):
                    h = 2 * p + e
                    lm = (lane < 64) if e == 0 else (lane >= 64)
                    qm = jnp.where(lm, qp, jnp.zeros_like(qp))
                    dom = jnp.where(lm, dop, jnp.zeros_like(dop))
                    s = _dot_nt(qm, kpair) * LOG2E + bias_ref[0, h]
                    pr = jnp.exp2(s - lse_ref[:, h:h + 1])
                    dp = _dot_nt(dom, vpair)
                    delta = jnp.sum(dom.astype(F32) * op, axis=-1, keepdims=True)
                    ds = pr * (dp - delta)
                    db_ref[h] += ds
                    dsb = ds.astype(BF16)
                    prb = pr.astype(BF16)
                    dqs.append(_dot(dsb, kpair) * SCALE)
                    dk_acc = dk_acc + _dot_tn(dsb, qm)
                    dv_acc = dv_acc + _dot_tn(prb, dom)
                dq_ring[s_new, :, cols] = jnp.where(lane < 64, dqs[0], dqs[1])
                dk_ring[s_old, :, cols] += dk_acc[0:QB]
                dk_ring[s_mid, :, cols] += dk_acc[QB:2 * QB]
                dk_ring[s_new, :, cols] = dk_acc[2 * QB:3 * QB]
                dv_ring[s_old, :, cols] += dv_acc[0:QB]
                dv_ring[s_mid, :, cols] += dv_acc[QB:2 * QB]
                dv_ring[s_new, :, cols] = dv_acc[2 * QB:3 * QB]

        @pl.when(g >= 2)
        def _():
            out_ref[:, 0:D_ATT] = dq_ring[s_old].astype(BF16)
            out_ref[:, D_ATT:2 * D_ATT] = dk_ring[s_old].astype(BF16)
            out_ref[:, 2 * D_ATT:3 * D_ATT] = dv_ring[s_old].astype(BF16)

    qblk = lambda w: pl.BlockSpec((QB, w), lambda g: (jnp.minimum(g, nb - 1), 0))
    return pl.pallas_call(
        body, name="attn_bwd", grid=(nb + 2,),
        out_shape=[jax.ShapeDtypeStruct((S, 3 * D_ATT), BF16), jax.ShapeDtypeStruct((HEADS, QB, KB), F32)],
        in_specs=[qblk(D_ATT), qblk(D_ATT), qblk(D_ATT), qblk(HEADS), ANY,
                  pl.BlockSpec((1, HEADS, QB, KB), lambda g: (jnp.minimum(g, 2), 0, 0, 0))],
        out_specs=[pl.BlockSpec((QB, 3 * D_ATT), lambda g: (jnp.maximum(g - 2, 0), 0)),
                   pl.BlockSpec((HEADS, QB, KB), lambda g: (0, 0, 0))],
        scratch_shapes=[pltpu.VMEM((S + PADK, D_ATT), BF16), pltpu.VMEM((S + PADK, D_ATT), BF16),
                        pltpu.VMEM((3, QB, D_ATT), F32), pltpu.VMEM((3, QB, D_ATT), F32),
                        pltpu.VMEM((3, QB, D_ATT), F32), pltpu.SemaphoreType.DMA((2,))],
        compiler_params=_params(1),
    )(P, att, datt, lse, P, bias_tab)


def _bias_fold(dscore):
    wide = 1024

    def body(d_ref, o_ref):
        sub = lax.broadcasted_iota(jnp.int32, (8, 1), 0)
        col = lax.broadcasted_iota(jnp.int32, (1, wide), 1)
        pad = jnp.zeros((8, wide - KB), F32)
        for h in range(HEADS):
            acc = jnp.concatenate([d_ref[h, 0:8, :], pad], axis=1)
            for qh in range(1, QB // 8):
                a = jnp.concatenate([d_ref[h, 8 * qh:8 * qh + 8, :], pad], axis=1)
                acc = acc + pltpu.roll(a, wide - 8 * qh, 1)
            for r in range(1, 8):
                acc = jnp.where(sub == r, pltpu.roll(acc, wide - r, 1), acc)
            vec = jnp.sum(acc, axis=0, keepdims=True)
            far = (col <= PADK - MAX_REL) | (col > KB)
            tail = jnp.sum(jnp.where(far, vec, 0.0), axis=-1, keepdims=True)
            o_ref[h:h + 1, :] = jnp.where(col == wide - 1, tail, vec)

    return pl.pallas_call(
        body, name="bias_fold",
        out_shape=jax.ShapeDtypeStruct((HEADS, wide), F32),
        compiler_params=pltpu.CompilerParams(vmem_limit_bytes=VMEM_LIMIT),
    )(dscore)


def _in_proj_bwd(x, norm_g, dx2, dqkv, dgate, w_in_g):
    S = x.shape[0]
    ts = 256

    def body(x_ref, g_ref, dx2_ref, dq_ref, dg_ref, w_ref, gx_ref, dn_ref):
        @pl.when(pl.program_id(0) == 0)
        def _():
            dn_ref[...] = jnp.zeros_like(dn_ref)

        dh = jnp.zeros((ts, D_MODEL), F32)
        for j in range(N_DEV):
            if j < 2:
                d = dq_ref[:, j * W_BLK:(j + 1) * W_BLK]
            else:
                d = dg_ref[:, (j - 2) * W_BLK:(j - 1) * W_BLK]
            dh = dh + _dot_nt(d, w_ref[j])
        xf = x_ref[...]
        r = lax.rsqrt(jnp.mean(xf * xf, axis=-1, keepdims=True) + EPS)
        xn = xf * r
        dn_ref[0:1, :] += jnp.sum(dh * xn, axis=0, keepdims=True)
        dhg = dh * g_ref[...]
        gx_ref[...] = dx2_ref[...] + r * (dhg - xn * jnp.mean(dhg * xn, axis=-1, keepdims=True))

    tile = lambda w: pl.BlockSpec((ts, w), lambda i: (i, 0))
    return pl.pallas_call(
        body, name="in_proj_bwd", grid=(S // ts,),
        out_shape=[jax.ShapeDtypeStruct((S, D_MODEL), F32), jax.ShapeDtypeStruct((8, D_MODEL), F32)],
        in_specs=[tile(D_MODEL), pl.BlockSpec((1, D_MODEL), lambda i: (0, 0)), tile(D_MODEL),
                  tile(3 * D_ATT), tile(GATE_COLS),
                  pl.BlockSpec((N_DEV, D_MODEL, W_BLK), lambda i: (0, 0, 0))],
        out_specs=[tile(D_MODEL), pl.BlockSpec((8, D_MODEL), lambda i: (0, 0))],
        compiler_params=_params(1),
    )(x, norm_g, dx2, dqkv, dgate, w_in_g)


SCATTER_MASKS = ((3, 4, 5, 2, 7, 6, 1, 0), (5, 2, 3, 4, 7, 6, 1, 0))


def _w_in_grad_scatter(ht, dqkv, dgate, d_proj, d_wo, pack, order):
    S = ht.shape[1]
    ts = min(S, 2048)
    nt = S // ts
    n_steps = 8

    def body(order_ref, ht_ref, dq_ref, dg_ref, proj_hbm, wo_hbm, pack_hbm, g_ref, rproj, rwo, rpack,
             acc, stage, rsib, rici, d2d_send, d2d_recv, ici_send, ici_recv, small_send, small_recv, local_sems):
        k, i = pl.program_id(0), pl.program_id(1)
        x, y, c = _mesh_pos()
        my = _flat((x, y, c))
        sibling = (x, y, 1 - c)
        owners = [(x ^ (1 - c), y ^ c, c), (x ^ c, y ^ (1 - c), c), (1 - x, 1 - y, c)]
        peers = [sibling, (1 - x, y, c), (x, 1 - y, c), (1 - x, 1 - y, c),
                 (1 - x, y, 1 - c), (x, 1 - y, 1 - c), (1 - x, 1 - y, 1 - c)]
        small = ((proj_hbm, rproj, True), (wo_hbm, rwo, True), (pack_hbm, rpack, False))
        n_small = len(small)

        def small_copy(kk, a, receive=False):
            src, dst, per_peer = small[a]
            slot = _flat(peers[kk]) if receive else my
            return pltpu.make_async_remote_copy(
                src_ref=src.at[_flat(peers[kk])] if per_peer else src, dst_ref=dst.at[slot],
                send_sem=small_send.at[kk, a], recv_sem=small_recv.at[kk, a],
                device_id=peers[kk], device_id_type=MESH)

        def to_sibling(t):
            return pltpu.make_async_remote_copy(
                src_ref=stage.at[0], dst_ref=rsib.at[t], send_sem=d2d_send.at[t], recv_sem=d2d_recv.at[t],
                device_id=sibling, device_id_type=MESH)

        def to_owner(t):
            return pltpu.make_async_remote_copy(
                src_ref=stage.at[1], dst_ref=rici.at[t], send_sem=ici_send.at[t], recv_sem=ici_recv.at[t],
                device_id=owners[t], device_id_type=MESH)

        own_small = [pltpu.make_async_copy(src.at[my] if per_peer else src, dst.at[my], local_sems.at[a])
                     for a, (src, dst, per_peer) in enumerate(small)]

        @pl.when((k == 0) & (i == 0))
        def _():
            for cp in own_small:
                cp.start()
            for kk in range(len(peers)):
                for a in range(n_small):
                    small_copy(kk, a).start()

        def accumulate(d_ref):
            prod = _dot(ht_ref[...], d_ref[...])

            @pl.when(i == 0)
            def _():
                acc[...] = prod

            @pl.when(i > 0)
            def _():
                acc[...] += prod

        @pl.when(order_ref[k] < 2)
        def _():
            accumulate(dq_ref)

        @pl.when(order_ref[k] >= 2)
        def _():
            accumulate(dg_ref)

        @pl.when(i == nt - 1)
        def _():
            for s in range(n_steps):
                @pl.when(k == s)
                def _():
                    t = s // 2
                    if s % 2 == 0:
                        if t >= 1:
                            to_sibling(t - 1).wait_send()
                        stage[0] = acc[...].astype(BF16)
                        to_sibling(t).start()
                    elif t < 3:
                        if t >= 1:
                            to_owner(t - 1).wait_send()
                        to_sibling(t).wait_recv()
                        stage[1] = (acc[...] + rsib[t].astype(F32)).astype(BF16)
                        to_owner(t).start()
                    else:
                        to_sibling(t).wait_recv()
                        total = acc[...] + rsib[t].astype(F32)
                        for j in range(3):
                            to_owner(j).wait_recv()
                            total = total + rici[j].astype(F32)
                        g_ref[...] = total
                        to_owner(2).wait_send()
                        to_sibling(3).wait_send()
                        for q in range(len(peers)):
                            for a in range(n_small):
                                small_copy(q, a).wait_send()
                                small_copy(q, a, receive=True).wait_recv()
                        for cp in own_small:
                            cp.wait()

    blk = (D_MODEL, W_BLK)
    grid_spec = pltpu.PrefetchScalarGridSpec(
        num_scalar_prefetch=1, grid=(n_steps, nt),
        in_specs=[pl.BlockSpec((D_MODEL, ts), lambda k, i, o: (0, i)),
                  pl.BlockSpec((ts, W_BLK), lambda k, i, o: (i, jnp.minimum(o[k], 1))),
                  pl.BlockSpec((ts, W_BLK), lambda k, i, o: (i, jnp.maximum(o[k] - 2, 0))),
                  ANY, ANY, ANY],
        out_specs=[pl.BlockSpec(blk, lambda k, i, o: (0, 0)), ANY, ANY, ANY],
        scratch_shapes=[pltpu.VMEM(blk, F32), pltpu.VMEM((2,) + blk, BF16),
                        pltpu.VMEM((4,) + blk, BF16), pltpu.VMEM((3,) + blk, BF16),
                        pltpu.SemaphoreType.DMA((4,)), pltpu.SemaphoreType.DMA((4,)),
                        pltpu.SemaphoreType.DMA((3,)), pltpu.SemaphoreType.DMA((3,)),
                        pltpu.SemaphoreType.DMA((7, 3)), pltpu.SemaphoreType.DMA((7, 3)),
                        pltpu.SemaphoreType.DMA((3,))])
    return pl.pallas_call(
        body, name="w_in_grad_scatter", grid_spec=grid_spec,
        out_shape=[jax.ShapeDtypeStruct(blk, F32),
                   jax.ShapeDtypeStruct(d_proj.shape, BF16), jax.ShapeDtypeStruct(d_wo.shape, BF16),
                   jax.ShapeDtypeStruct((N_DEV,) + pack.shape, F32)],
        compiler_params=_params(2),
    )(order, ht, dqkv, dgate, d_proj, d_wo, pack)


def _adamw(w, g, m, v):
    m = ADAM_B1 * m + (1.0 - ADAM_B1) * g
    v = ADAM_B2 * v + (1.0 - ADAM_B2) * (g * g)
    m_hat = m / (1.0 - ADAM_B1 ** ADAM_STEP)
    v_hat = v / (1.0 - ADAM_B2 ** ADAM_STEP)
    delta = -ADAM_LR * (m_hat / (jnp.sqrt(v_hat) + ADAM_EPS) + ADAM_WD * w)
    return delta, m, v


def _sum_adamw(parts, w, m, v, name):
    R, C = w.shape
    n = parts.shape[0]
    tr = min(R, 256)

    def body(p_ref, w_ref, m_ref, v_ref, g_ref, d_ref, nm_ref, nv_ref):
        g = p_ref[0].astype(F32)
        for s in range(1, n):
            g = g + p_ref[s].astype(F32)
        g_ref[...] = g
        d_ref[...], nm_ref[...], nv_ref[...] = _adamw(w_ref[...], g, m_ref[...], v_ref[...])

    tile = pl.BlockSpec((tr, C), lambda i: (i, 0))
    return pl.pallas_call(
        body, name=name, grid=(R // tr,),
        out_shape=[jax.ShapeDtypeStruct((R, C), F32)] * 4,
        in_specs=[pl.BlockSpec((n, tr, C), lambda i: (0, i, 0)), tile, tile, tile],
        out_specs=[tile] * 4,
        compiler_params=_params(1),
    )(parts, w, m, v)


def _adamw_mid(r_proj, r_wo, params):
    def body(rp_ref, rw_ref, *refs):
        ins, outs = refs[:9], refs[9:]

        def total(part):
            g = part(0).astype(F32)
            for s in range(1, N_DEV):
                g = g + part(s).astype(F32)
            return g

        grads = (total(lambda s: rp_ref[s, :, 0:128]), total(lambda s: rp_ref[s, :, 128:256]),
                 total(lambda s: rw_ref[s]))
        for n, g in enumerate(grads):
            w, m, v = (r[...] for r in ins[3 * n:3 * n + 3])
            outs[4 * n][...] = g
            outs[4 * n + 1][...], outs[4 * n + 2][...], outs[4 * n + 3][...] = _adamw(w, g, m, v)

    return pl.pallas_call(
        body, name="adamw_mid",
        out_shape=[jax.ShapeDtypeStruct(params[3 * n].shape, F32) for n in range(3) for _ in range(4)],
        compiler_params=pltpu.CompilerParams(vmem_limit_bytes=VMEM_LIMIT),
    )(r_proj, r_wo, *params)


def _adamw_small(r_pack, params):
    wide = 384

    def body(p_ref, *refs):
        ins, loss_ref, outs = refs[:15], refs[15], refs[16:]
        tot = p_ref[0]
        for s in range(1, N_DEV):
            tot = tot + p_ref[s]
        me = _flat(_mesh_pos())
        loss_ref[...] = jnp.sum(tot[2:3, :], axis=-1, keepdims=True)
        mine = pltpu.roll(tot[0:8, 0:D_CONV], (D_CONV - 64 * me) % D_CONV, 1)
        col = lax.broadcasted_iota(jnp.int32, (D_MODEL, wide), 0)
        idx = lax.broadcasted_iota(jnp.int32, (D_MODEL, wide), 1)
        near = (idx > MAX_REL - CHUNK) & (idx < 2 * MAX_REL) & (col == PADK + MAX_REL - idx)
        far = (idx == 2 * MAX_REL) & (col == D_MODEL - 1)
        perm = jnp.where(near | far, 1.0, 0.0).astype(F32)
        g_rel = jnp.dot(tot[8:16], perm, precision=lax.Precision.HIGHEST, preferred_element_type=F32)
        grads = (tot[0:1], tot[1:2], mine[3:6, 0:64], tot[6:7, 0:D_CONV], g_rel[:, 0:N_REL])
        for n, g in enumerate(grads):
            w, m, v = (r[...] for r in ins[3 * n:3 * n + 3])
            outs[4 * n][...] = g
            outs[4 * n + 1][...], outs[4 * n + 2][...], outs[4 * n + 3][...] = _adamw(w, g, m, v)

    return pl.pallas_call(
        body, name="adamw_small",
        out_shape=[jax.ShapeDtypeStruct((1, 1), F32)]
        + [jax.ShapeDtypeStruct(params[3 * n].shape, F32) for n in range(5) for _ in range(4)],
    )(r_pack, *params)


def _pad_row(a, width=D_MODEL):
    a = a.reshape(-1, a.shape[-1])
    return jnp.pad(a, ((0, 0), (0, width - a.shape[-1])))


def kernel(x, norm_g, w_in, rel_bias, w_att_out, conv_w, conv_b, w_conv_out, w_out, final_norm_g, loss_target, m_norm_g, m_w_in, m_rel_bias, m_w_att_out, m_conv_w, m_conv_b, m_w_conv_out, m_w_out, m_final_norm_g, v_norm_g, v_w_in, v_rel_bias, v_w_att_out, v_conv_w, v_conv_b, v_w_conv_out, v_w_out, v_final_norm_g):
    S = x.shape[1]
    x2d = x.reshape(S, D_MODEL)
    tgt = loss_target.reshape(S, D_MODEL)
    me = 4 * lax.axis_index("x") + 2 * lax.axis_index("y") + lax.axis_index("c")
    row = lambda a: a.reshape(1, D_MODEL)

    proj_sh = jnp.concatenate([w_att_out[0], w_conv_out[0]], axis=1).astype(BF16)
    cw_sh = jnp.pad(conv_w[0], ((0, 5), (0, 64)))
    P, ht, w_in_g, proj_g, w_out_g, cw_g = _gather_in_proj(
        x2d, norm_g, w_in[0].astype(BF16), [proj_sh, w_out[0].astype(BF16), cw_sh],
        me ^ _by_core(GATHER_MASKS))

    bias_tab = _bias_table(rel_bias[0])
    att, lse = _attn_fwd(P, bias_tab)
    dx2, dgate, datt, d_wo, d_proj, sm1, sm2 = _token_local(
        x2d, tgt, P, att, proj_g, w_out_g.reshape(D_MODEL, D_MODEL), cw_g, conv_b, row(final_norm_g))
    dqkv, dscore = _attn_bwd(P, att, datt, lse, bias_tab)
    dbias = _bias_fold(dscore)
    grad_x, dnorm = _in_proj_bwd(x2d, norm_g, dx2, dqkv, dgate, w_in_g)

    pack = jnp.concatenate([dnorm[0:1], sm1[0:2], _pad_row(sm2[0:4]), jnp.zeros((1, D_MODEL), F32), dbias],
                           axis=0)
    g_win_sum, r_proj, r_wo, r_pack = _w_in_grad_scatter(
        ht, dqkv, dgate, d_proj, d_wo.reshape(N_DEV, 128, D_MODEL), pack, me ^ _by_core(SCATTER_MASKS))

    res = {"w_in": _sum_adamw(g_win_sum[None], w_in[0], m_w_in[0], v_w_in[0], "adamw_w_in")}
    mid = _adamw_mid(r_proj, r_wo, (w_att_out[0], m_w_att_out[0], v_w_att_out[0],
                                    w_conv_out[0], m_w_conv_out[0], v_w_conv_out[0],
                                    w_out[0], m_w_out[0], v_w_out[0]))
    for n, name in enumerate(("w_att_out", "w_conv_out", "w_out")):
        res[name] = mid[4 * n:4 * n + 4]
    small = _adamw_small(r_pack, (norm_g, m_norm_g, v_norm_g,
                                  row(final_norm_g), row(m_final_norm_g), row(v_final_norm_g),
                                  conv_w[0], m_conv_w[0], v_conv_w[0], conv_b, m_conv_b, v_conv_b,
                                  rel_bias[0], m_rel_bias[0], v_rel_bias[0]))
    loss = small[0].reshape(())
    for n, name in enumerate(("norm_g", "final_norm_g", "conv_w", "conv_b", "rel_bias")):
        res[name] = small[1 + 4 * n:5 + 4 * n]

    leading = {"norm_g": (1, D_MODEL), "final_norm_g": (D_MODEL,), "conv_b": (1, D_CONV)}
    outs = []
    for kind in range(4):
        for name in ("norm_g", "w_in", "rel_bias", "w_att_out", "conv_w", "conv_b", "w_conv_out", "w_out",
                     "final_norm_g"):
            a = res[name][kind]
            outs.append(a.reshape(leading[name]) if name in leading else a[None])
    return (loss, grad_x.reshape(1, S, D_MODEL), *outs)
```

```python
import functools

import numpy as np
import jax
import jax.numpy as jnp
from jax import lax
from jax.experimental import pallas as pl
from jax.experimental.pallas import tpu as pltpu

F32 = jnp.float32
BF16 = jnp.bfloat16

D_MODEL = 1024
CHUNK = 64
N_LEFT = 8
HEADS = 8
D_ATT = 512
D_CONV = 512
MAX_REL = 128
N_REL = 2 * MAX_REL + 1
IN_COLS = 6144
EPS = 1e-6
NEG_BIG = -1e30
N_DEV = 8
W_BLK = IN_COLS // N_DEV
QB = 4 * CHUNK
KB = QB + N_LEFT * CHUNK
PADK = N_LEFT * CHUNK
SCALE = 64 ** -0.5
LOG2E = 1.4426950408889634
GATE_COLS = IN_COLS - 3 * D_ATT

ADAM_LR = 0.001
ADAM_B1 = 0.9
ADAM_B2 = 0.999
ADAM_EPS = 1e-08
ADAM_WD = 0.01
ADAM_STEP = 10

VMEM_LIMIT = 56 * 1024 * 1024

MESH = pl.DeviceIdType.MESH
ANY = pl.BlockSpec(memory_space=pl.ANY)


def _params(n_grid, vmem_limit=VMEM_LIMIT):
    return pltpu.CompilerParams(dimension_semantics=("arbitrary",) * n_grid,
                                vmem_limit_bytes=vmem_limit)


def _dot(a, b):
    return jnp.dot(a, b, preferred_element_type=F32)


def _dot_nt(a, b):
    return lax.dot_general(a, b, (((1,), (1,)), ((), ())), preferred_element_type=F32)


def _dot_tn(a, b):
    return lax.dot_general(a, b, (((0,), (0,)), ((), ())), preferred_element_type=F32)


def _sigmoid(z):
    return 0.5 * jnp.tanh(0.5 * z) + 0.5


def _mesh_pos():
    return lax.axis_index("x"), lax.axis_index("y"), lax.axis_index("c")


def _flat(p):
    return 4 * p[0] + 2 * p[1] + p[2]


def _by_core(masks):
    m0, m1 = (jnp.array(m, jnp.int32) for m in masks)
    return jnp.where(lax.axis_index("c") == 0, m0, m1)


GATHER_MASKS = ((0, 1, 4, 3, 2, 5, 6, 7), (0, 1, 2, 5, 4, 3, 6, 7))


def _gather_in_proj(x, norm_g, w_sh, smalls, order):
    S = x.shape[0]
    ts = 1024
    nt = S // ts
    n_small = len(smalls)
    n_steps = N_DEV

    def body(order_ref, x_ref, g_ref, w_hbm, *rest):
        small_in = rest[:n_small]
        p_ref, ht_ref, wg_hbm = rest[n_small:n_small + 3]
        small_out = rest[n_small + 3:2 * n_small + 3]
        (wbuf, hbuf, own_sem, send_sems, recv_sems, out_sems,
         small_send, small_recv, small_local) = rest[2 * n_small + 3:]
        k, i = pl.program_id(0), pl.program_id(1)
        x_, y_, c_ = _mesh_pos()
        me, sibling = (x_, y_, c_), (x_, y_, 1 - c_)
        my = _flat(me)
        chips = [(x_ ^ (1 - c_), y_ ^ c_), (x_ ^ c_, y_ ^ (1 - c_)), (1 - x_, 1 - y_)]
        peers = [sibling] + [(*chip, c_) for chip in chips] + [(*chip, 1 - c_) for chip in chips]

        def wcopy(sem, block, to, from_input=False):
            dst = wbuf.at[_flat(block)]
            return pltpu.make_async_remote_copy(
                src_ref=w_hbm if from_input else dst, dst_ref=dst,
                send_sem=send_sems.at[sem], recv_sem=recv_sems.at[sem], device_id=to, device_id_type=MESH)

        def small_copy(q, a, receive=False):
            slot = _flat(peers[q]) if receive else my
            return pltpu.make_async_remote_copy(
                src_ref=small_in[a], dst_ref=small_out[a].at[slot],
                send_sem=small_send.at[q, a], recv_sem=small_recv.at[q, a],
                device_id=peers[q], device_id_type=MESH)

        def keep(step, block):
            return pltpu.make_async_copy(wbuf.at[_flat(block)], wg_hbm.at[_flat(block)], out_sems.at[step])

        own = pltpu.make_async_copy(w_hbm, wbuf.at[my], own_sem)
        small_own = [pltpu.make_async_copy(small_in[a], small_out[a].at[my], small_local.at[a])
                     for a in range(n_small)]
        passed_on = [(*chips[1], 1 - c_), (*chips[0], 1 - c_), (*chips[2], 1 - c_)]
        arrivals = [me, sibling]
        for j in range(3):
            arrivals += [(*chips[j], c_), passed_on[j]]

        @pl.when(i == 0)
        def _():
            for kk in range(n_steps):
                @pl.when(k == kk)
                def _():
                    j = kk // 2 - 1
                    if kk == 0:
                        own.start()
                        wcopy(0, me, sibling, True).start()
                        wcopy(1, me, (*chips[0], c_), True).start()
                        own.wait()
                    elif kk == 1:
                        wcopy(0, sibling, me).wait_recv()
                        wcopy(2, me, (*chips[1], c_), True).start()
                    elif kk % 2 == 0:
                        wcopy(1 + j, (*chips[j], c_), me).wait_recv()
                        wcopy(4 + j, (*chips[j], c_), sibling).start()
                        if kk == 2:
                            wcopy(3, me, (*chips[2], c_), True).start()
                    else:
                        wcopy(4 + j, passed_on[j], me).wait_recv()
                        if kk == 3:
                            for cp in small_own:
                                cp.start()
                            for q in range(len(peers)):
                                for a in range(n_small):
                                    small_copy(q, a).start()
                    keep(kk, arrivals[kk]).start()

        row0 = pl.multiple_of(i * ts, ts)

        @pl.when(k == 0)
        def _():
            xf = x_ref[...]
            r = lax.rsqrt(jnp.mean(xf * xf, axis=-1, keepdims=True) + EPS)
            hf = (xf * r) * g_ref[...]
            hbuf[pl.ds(row0, ts), :] = hf.astype(BF16)
            ht_ref[...] = hf.astype(BF16).T

        p_ref[...] = _dot(hbuf[pl.ds(row0, ts), :], wbuf[order_ref[k]]).astype(BF16)

        @pl.when((k == n_steps - 1) & (i == nt - 1))
        def _():
            wcopy(0, me, sibling, True).wait_send()
            for j, chip in enumerate(chips):
                wcopy(1 + j, me, (*chip, c_), True).wait_send()
                wcopy(4 + j, (*chip, c_), sibling).wait_send()
            for kk in range(n_steps):
                keep(kk, arrivals[kk]).wait()
            for cp in small_own:
                cp.wait()
            for q in range(len(peers)):
                for a in range(n_small):
                    small_copy(q, a).wait_send()
                    small_copy(q, a, receive=True).wait_recv()

    first_pass = lambda k, i: jnp.where(k == 0, i, nt - 1)
    grid_spec = pltpu.PrefetchScalarGridSpec(
        num_scalar_prefetch=1, grid=(n_steps, nt),
        in_specs=[pl.BlockSpec((ts, D_MODEL), lambda k, i, o: (first_pass(k, i), 0)),
                  pl.BlockSpec((1, D_MODEL), lambda k, i, o: (0, 0)), ANY] + [ANY] * n_small,
        out_specs=[pl.BlockSpec((ts, W_BLK), lambda k, i, o: (i, o[k])),
                   pl.BlockSpec((D_MODEL, ts), lambda k, i, o: (0, first_pass(k, i))), ANY] + [ANY] * n_small,
        scratch_shapes=[pltpu.VMEM((N_DEV, D_MODEL, W_BLK), BF16), pltpu.VMEM((S, D_MODEL), BF16),
                        pltpu.SemaphoreType.DMA, pltpu.SemaphoreType.DMA((7,)), pltpu.SemaphoreType.DMA((7,)),
                        pltpu.SemaphoreType.DMA((n_steps,)),
                        pltpu.SemaphoreType.DMA((7, n_small)), pltpu.SemaphoreType.DMA((7, n_small)),
                        pltpu.SemaphoreType.DMA((n_small,))])
    return pl.pallas_call(
        body, name="gather_in_proj", grid_spec=grid_spec,
        out_shape=[jax.ShapeDtypeStruct((S, IN_COLS), BF16), jax.ShapeDtypeStruct((D_MODEL, S), BF16),
                   jax.ShapeDtypeStruct((N_DEV,) + w_sh.shape, BF16)]
        + [jax.ShapeDtypeStruct((N_DEV,) + s.shape, s.dtype) for s in smalls],
        compiler_params=_params(2),
    )(order, x, norm_g, w_sh, *smalls)


def _bias_table(rel_bias):
    wide = 1024

    def body(r_ref, o_ref):
        h = pl.program_id(0)
        col = lax.broadcasted_iota(jnp.int32, (1, wide), 1)
        k_minus_q = jnp.where(col < KB, col, col - wide)
        idx = jnp.clip(PADK - k_minus_q, -MAX_REL, MAX_REL) + MAX_REL
        f = jnp.zeros((1, wide), F32)
        for r in range(MAX_REL - CHUNK + 1, N_REL):
            f = jnp.where(idx == r, r_ref[h, r], f)
        kcol = lax.broadcasted_iota(jnp.int32, (1, KB), 1)
        kc = kcol >> 6
        sub = lax.broadcasted_iota(jnp.int32, (8, 1), 0)
        f8 = jnp.broadcast_to(f * LOG2E, (8, wide))
        base = f8
        for r in range(1, 8):
            base = jnp.where(sub == r, pltpu.roll(f8, r, 1), base)
        for qh in range(QB // 8):
            rows = (pltpu.roll(base, 8 * qh, 1) if qh else base)[:, 0:KB]
            qc = (8 * qh) // CHUNK
            band = (kc >= qc) & (kc <= qc + N_LEFT)
            for t in range(3):
                o_ref[t, 0, 8 * qh:8 * qh + 8, :] = jnp.where(band & (kcol >= PADK - t * QB), rows, NEG_BIG)

    return pl.pallas_call(
        body, name="bias_table", grid=(HEADS,),
        out_shape=jax.ShapeDtypeStruct((3, HEADS, QB, KB), F32),
        in_specs=[pl.BlockSpec(memory_space=pltpu.SMEM)],
        out_specs=pl.BlockSpec((3, 1, QB, KB), lambda h: (0, h, 0, 0)),
        compiler_params=_params(1),
    )(rel_bias)


def _load_keys(p_hbm, kp, vp, sem):
    kp[0:PADK, :] = jnp.zeros((PADK, D_ATT), BF16)
    vp[0:PADK, :] = jnp.zeros((PADK, D_ATT), BF16)
    S = p_hbm.shape[0]
    ck = pltpu.make_async_copy(p_hbm.at[:, D_ATT:2 * D_ATT], kp.at[PADK:PADK + S, :], sem.at[0])
    cv = pltpu.make_async_copy(p_hbm.at[:, 2 * D_ATT:3 * D_ATT], vp.at[PADK:PADK + S, :], sem.at[1])
    ck.start()
    cv.start()
    ck.wait()
    cv.wait()


def _attn_fwd(P, bias_tab):
    S = P.shape[0]
    nb = S // QB

    def body(q_ref, p_hbm, bias_ref, o_ref, lse_ref, kp, vp, sem):
        g = pl.program_id(0)

        @pl.when(g == 0)
        def _():
            _load_keys(p_hbm, kp, vp, sem)

        start = pl.multiple_of(g * QB, QB)
        lane = lax.broadcasted_iota(jnp.int32, (1, 128), 1)
        for p in range(HEADS // 2):
            cols = slice(128 * p, 128 * (p + 1))
            qp = q_ref[:, cols] * SCALE
            kpair = kp[pl.ds(start, KB), cols]
            vpair = vp[pl.ds(start, KB), cols]
            outs = []
            for e in range(2):
                h = 2 * p + e
                lm = (lane < 64) if e == 0 else (lane >= 64)
                qm = jnp.where(lm, qp, jnp.zeros_like(qp))
                s = (_dot_nt(qm, kpair) * LOG2E + bias_ref[0, h]).astype(BF16)
                mx = jnp.max(s, axis=-1, keepdims=True)
                ex = jnp.exp2(s - mx)
                o = _dot(ex, jnp.where(lm, vpair, jnp.ones_like(vpair)))
                sums = pltpu.roll(o, 64, 1)
                outs.append(o / sums)
                lse_ref[:, h:h + 1] = mx.astype(F32) + jnp.log2(sums[:, 0:1] if e == 0 else o[:, 0:1])
            o_ref[:, cols] = jnp.where(lane < 64, outs[0], outs[1]).astype(BF16)

    return pl.pallas_call(
        body, name="attn_fwd", grid=(nb,),
        out_shape=[jax.ShapeDtypeStruct((S, D_ATT), BF16), jax.ShapeDtypeStruct((S, HEADS), F32)],
        in_specs=[pl.BlockSpec((QB, D_ATT), lambda g: (g, 0)), ANY,
                  pl.BlockSpec((1, HEADS, QB, KB), lambda g: (jnp.minimum(g, 2), 0, 0, 0))],
        out_specs=[pl.BlockSpec((QB, D_ATT), lambda g: (g, 0)),
                   pl.BlockSpec((QB, HEADS), lambda g: (g, 0))],
        scratch_shapes=[pltpu.VMEM((S + PADK, D_ATT), BF16), pltpu.VMEM((S + PADK, D_ATT), BF16),
                        pltpu.SemaphoreType.DMA((2,))],
        compiler_params=_params(1),
    )(P, P, bias_tab)


def _token_local(x, tgt, P, att, proj_g, w_out, cw_g, conv_b, final_g):
    S = x.shape[0]
    ts = 256
    nt = S // ts
    hb = 16

    def body(x_ref, t_ref, s1_ref, s2_ref, s3_ref, h1_ref, h2_ref, att_ref,
             pg_ref, wo_ref, cwg_ref, cb_ref, g2_ref,
             dx2_ref, dg_ref, datt_ref, dwo_ref, dproj_ref, sm1_ref, sm2_ref,
             carry, wao_ref, wco_ref, cw_ref, dwo_acc, dwao_acc, dwco_acc):
        i = pl.program_id(0)
        t = nt - 1 - i

        @pl.when(i == 0)
        def _():
            dwo_acc[...] = jnp.zeros_like(dwo_acc)
            dwao_acc[...] = jnp.zeros_like(dwao_acc)
            dwco_acc[...] = jnp.zeros_like(dwco_acc)
            lane = lax.broadcasted_iota(jnp.int32, (1, 128), 1)
            for j in range(N_DEV):
                wao_ref[:, 128 * j:128 * (j + 1)] = pg_ref[j, :, 0:128]
                wco_ref[:, 128 * j:128 * (j + 1)] = pg_ref[j, :, 128:256]
            for p in range(N_DEV // 2):
                cw_ref[:, 128 * p:128 * (p + 1)] = jnp.where(
                    lane < 64, cwg_ref[2 * p], pltpu.roll(cwg_ref[2 * p + 1], 64, 1))
            sm1_ref[...] = jnp.zeros_like(sm1_ref)
            sm2_ref[...] = jnp.zeros_like(sm2_ref)
            carry[...] = jnp.zeros_like(carry)

        za = s1_ref[:, 0:512].astype(F32)
        gb = s1_ref[:, 512:1024].astype(F32)
        gc = s1_ref[:, 1024:1536].astype(F32)
        u = s2_ref[:, 0:512].astype(F32)
        zc = s2_ref[:, 512:1024].astype(F32)
        ga = jnp.concatenate([s2_ref[:, 1024:1536], s3_ref[:, 0:512]], axis=1).astype(F32)
        gv = s3_ref[:, 512:1536].astype(F32)
        att = att_ref[...].astype(F32)
        row = lax.broadcasted_iota(jnp.int32, (ts, 1), 0)

        sa = _sigmoid(za)
        silu_a = za * sa
        att_g = (att * silu_a).astype(BF16)
        y_att = _dot(att_g, wao_ref[...])

        cu = gc * u
        keep = jnp.where(t > 0, 1.0, 0.0).astype(F32)
        hcu = (h1_ref[:, 1024:1536].astype(F32) * h2_ref[:, 0:512].astype(F32)) * keep
        cu_m1 = jnp.where(row == 0, hcu[hb - 1:hb, :], pltpu.roll(cu, 1, 0))
        cu_m2 = jnp.where(row == 0, hcu[hb - 2:hb - 1, :],
                          jnp.where(row == 1, hcu[hb - 1:hb, :], pltpu.roll(cu, 2, 0)))
        w0, w1, w2 = cw_ref[0:1, :], cw_ref[1:2, :], cw_ref[2:3, :]
        vconv = w0 * cu_m2 + w1 * cu_m1 + w2 * cu + cb_ref[...]
        sc = _sigmoid(zc)
        silu_c = zc * sc
        cg = (gb * vconv * silu_c).astype(BF16)
        y_conv = _dot(cg, wco_ref[...])

        sga = _sigmoid(ga)
        sgv = _sigmoid(gv)
        m = (sga * y_att + sgv * y_conv).astype(BF16)
        x2 = x_ref[...] + _dot(m, wo_ref[...])
        r2 = lax.rsqrt(jnp.mean(x2 * x2, axis=-1, keepdims=True) + EPS)
        xn2 = x2 * r2
        g2 = g2_ref[...]
        err = xn2 * g2 - t_ref[...]
        sm1_ref[1:2, :] += jnp.sum(err * err, axis=0, keepdims=True) * (0.5 / D_MODEL)

        dy = err * (1.0 / D_MODEL)
        sm1_ref[0:1, :] += jnp.sum(dy * xn2, axis=0, keepdims=True)
        dxn = dy * g2
        dx2 = r2 * (dxn - xn2 * jnp.mean(dxn * xn2, axis=-1, keepdims=True))
        dx2_ref[...] = dx2
        dx2b = dx2.astype(BF16)
        dwo_acc[...] += _dot_tn(m, dx2b)
        dm = _dot_nt(dx2b, wo_ref[...])
        dya = (dm * sga).astype(BF16)
        dyc = (dm * sgv).astype(BF16)
        dg_ref[:, 2560:3584] = (dm * y_att * (sga * (1.0 - sga))).astype(BF16)
        dg_ref[:, 3584:4608] = (dm * y_conv * (sgv * (1.0 - sgv))).astype(BF16)
        dwao_acc[...] += _dot_tn(att_g, dya)
        dwco_acc[...] += _dot_tn(cg, dyc)
        datt_g = _dot_nt(dya, wao_ref[...])
        dcg = _dot_nt(dyc, wco_ref[...])
        datt_ref[...] = (datt_g * silu_a).astype(BF16)
        dg_ref[:, 0:512] = (datt_g * att * (sa + silu_a * (1.0 - sa))).astype(BF16)
        dg_ref[:, 512:1024] = (dcg * vconv * silu_c).astype(BF16)
        dg_ref[:, 2048:2560] = (dcg * gb * vconv * (sc + silu_c * (1.0 - sc))).astype(BF16)
        dv = dcg * gb * silu_c
        sm2_ref[3:4, :] += jnp.sum(dv, axis=0, keepdims=True)
        sm2_ref[0:1, :] += jnp.sum(dv * cu_m2, axis=0, keepdims=True)
        sm2_ref[1:2, :] += jnp.sum(dv * cu_m1, axis=0, keepdims=True)
        sm2_ref[2:3, :] += jnp.sum(dv * cu, axis=0, keepdims=True)
        nxt = carry[...]
        dv_p1 = jnp.where(row == ts - 1, nxt[0:1, :], pltpu.roll(dv, ts - 1, 0))
        dv_p2 = jnp.where(row == ts - 1, nxt[1:2, :],
                          jnp.where(row == ts - 2, nxt[0:1, :], pltpu.roll(dv, ts - 2, 0)))
        dcu = w2 * dv + w1 * dv_p1 + w0 * dv_p2
        carry[...] = dv[0:8, :]
        dg_ref[:, 1024:1536] = (dcu * u).astype(BF16)
        dg_ref[:, 1536:2048] = (dcu * gc).astype(BF16)

        @pl.when(i == nt - 1)
        def _():
            dwo_ref[...] = dwo_acc[...].astype(BF16)
            for j in range(N_DEV):
                dproj_ref[j, :, 0:128] = dwao_acc[:, 128 * j:128 * (j + 1)].astype(BF16)
                dproj_ref[j, :, 128:256] = dwco_acc[:, 128 * j:128 * (j + 1)].astype(BF16)

    tile = lambda w: pl.BlockSpec((ts, w), lambda i: (nt - 1 - i, 0))
    seg = lambda c: pl.BlockSpec((ts, 1536), lambda i: (nt - 1 - i, c))
    halo = lambda c: pl.BlockSpec((hb, 1536), lambda i: (jnp.maximum((nt - 1 - i) * (ts // hb) - 1, 0), c))
    full = lambda a: pl.BlockSpec(a.shape, lambda i: (0,) * a.ndim)
    acc = lambda r, c: pl.BlockSpec((r, c), lambda i: (0, 0))
    return pl.pallas_call(
        body, name="token_local", grid=(nt,),
        out_shape=[jax.ShapeDtypeStruct((S, D_MODEL), F32), jax.ShapeDtypeStruct((S, IN_COLS), BF16),
                   jax.ShapeDtypeStruct((S, D_ATT), BF16), jax.ShapeDtypeStruct((D_MODEL, D_MODEL), BF16),
                   jax.ShapeDtypeStruct(proj_g.shape, BF16),
                   jax.ShapeDtypeStruct((8, D_MODEL), F32), jax.ShapeDtypeStruct((8, D_CONV), F32)],
        in_specs=[tile(D_MODEL), tile(D_MODEL), seg(1), seg(2), seg(3), halo(1), halo(2), tile(D_ATT),
                  full(proj_g), full(w_out), full(cw_g), full(conv_b), full(final_g)],
        out_specs=[tile(D_MODEL), tile(GATE_COLS), tile(D_ATT), acc(D_MODEL, D_MODEL), full(proj_g),
                   acc(8, D_MODEL), acc(8, D_CONV)],
        scratch_shapes=[pltpu.VMEM((8, D_CONV), F32),
                        pltpu.VMEM((D_ATT, D_MODEL), BF16), pltpu.VMEM((D_CONV, D_MODEL), BF16),
                        pltpu.VMEM((8, D_CONV), F32), pltpu.VMEM((D_MODEL, D_MODEL), F32),
                        pltpu.VMEM((D_ATT, D_MODEL), F32), pltpu.VMEM((D_CONV, D_MODEL), F32)],
        compiler_params=_params(1),
    )(x, tgt, P, P, P, P, P, att, proj_g, w_out, cw_g, conv_b, final_g)


def _attn_bwd(P, att, datt, lse, bias_tab, dP):
    S = P.shape[0]
    nb = S // QB

    def body(q_ref, att_ref, datt_ref, lse_ref, p_hbm, bias_ref, dp_hbm, out_ref, db_ref,
             kp, vp, dq_ring, dk_ring, dv_ring, sem):
        g = pl.program_id(0)

        @pl.when(g == 0)
        def _():
            _load_keys(p_hbm, kp, vp, sem)
            db_ref[...] = jnp.zeros_like(db_ref)
            dk_ring[...] = jnp.zeros_like(dk_ring)
            dv_ring[...] = jnp.zeros_like(dv_ring)

        s_new = g % 3
        s_mid = (g + 2) % 3
        s_old = (g + 1) % 3

        @pl.when(g < nb)
        def _():
            start = pl.multiple_of(g * QB, QB)
            lane = lax.broadcasted_iota(jnp.int32, (1, 128), 1)
            for p in range(HEADS // 2):
                cols = slice(128 * p, 128 * (p + 1))
                qp = q_ref[:, cols] * SCALE
                op = att_ref[:, cols].astype(F32)
                dop = datt_ref[:, cols]
                kpair = kp[pl.ds(start, KB), cols]
                vpair = vp[pl.ds(start, KB), cols]
                dqs = []
                dk_acc = jnp.zeros((KB, 128), F32)
                dv_acc = jnp.zeros((KB, 128), F32)
                for e in range(2):
                    h = 2 * p + e
                    lm = (lane < 64) if e == 0 else (lane >= 64)
                    qm = jnp.where(lm, qp, jnp.zeros_like(qp))
                    dom = jnp.where(lm, dop, jnp.zeros_like(dop))
                    s = _dot_nt(qm, kpair) * LOG2E + bias_ref[0, h]
                    pr = jnp.exp2(s - lse_ref[:, h:h + 1])
                    dp = _dot_nt(dom, vpair)
                    delta = jnp.sum(dom.astype(F32) * op, axis=-1, keepdims=True)
                    ds = pr * (dp - delta)
                    db_ref[h] += ds
                    dsb = ds.astype(BF16)
                    prb = pr.astype(BF16)
                    dqs.append(_dot(dsb, kpair) * SCALE)
                    dk_acc = dk_acc + _dot_tn(dsb, qm)
                    dv_acc = dv_acc + _dot_tn(prb, dom)
                dq_ring[s_new, :, cols] = jnp.where(lane < 64, dqs[0], dqs[1])
                dk_ring[s_old, :, cols] += dk_acc[0:QB]
                dk_ring[s_mid, :, cols] += dk_acc[QB:2 * QB]
                dk_ring[s_new, :, cols] = dk_acc[2 * QB:3 * QB]
                dv_ring[s_old, :, cols] += dv_acc[0:QB]
                dv_ring[s_mid, :, cols] += dv_acc[QB:2 * QB]
                dv_ring[s_new, :, cols] = dv_acc[2 * QB:3 * QB]

        @pl.when(g >= 2)
        def _():
            out_ref[:, 0:D_ATT] = dq_ring[s_old].astype(BF16)
            out_ref[:, D_ATT:2 * D_ATT] = dk_ring[s_old].astype(BF16)
            out_ref[:, 2 * D_ATT:3 * D_ATT] = dv_ring[s_old].astype(BF16)

    qblk = lambda w: pl.BlockSpec((QB, w), lambda g: (jnp.minimum(g, nb - 1), 0))
    return pl.pallas_call(
        body, name="attn_bwd", grid=(nb + 2,),
        out_shape=[jax.ShapeDtypeStruct((S, IN_COLS), BF16), jax.ShapeDtypeStruct((HEADS, QB, KB), F32)],
        in_specs=[qblk(D_ATT), qblk(D_ATT), qblk(D_ATT), qblk(HEADS), ANY,
                  pl.BlockSpec((1, HEADS, QB, KB), lambda g: (jnp.minimum(g, 2), 0, 0, 0)), ANY],
        out_specs=[pl.BlockSpec((QB, 3 * D_ATT), lambda g: (jnp.maximum(g - 2, 0), GATE_COLS // (3 * D_ATT))),
                   pl.BlockSpec((HEADS, QB, KB), lambda g: (0, 0, 0))],
        input_output_aliases={6: 0},
        scratch_shapes=[pltpu.VMEM((S + PADK, D_ATT), BF16), pltpu.VMEM((S + PADK, D_ATT), BF16),
                        pltpu.VMEM((3, QB, D_ATT), F32), pltpu.VMEM((3, QB, D_ATT), F32),
                        pltpu.VMEM((3, QB, D_ATT), F32), pltpu.SemaphoreType.DMA((2,))],
        compiler_params=_params(1),
    )(P, att, datt, lse, P, bias_tab, dP)


def _bias_fold(dscore):
    wide = 1024

    def body(d_ref, o_ref):
        sub = lax.broadcasted_iota(jnp.int32, (8, 1), 0)
        col = lax.broadcasted_iota(jnp.int32, (1, wide), 1)
        pad = jnp.zeros((8, wide - KB), F32)
        for h in range(HEADS):
            acc = jnp.concatenate([d_ref[h, 0:8, :], pad], axis=1)
            for qh in range(1, QB // 8):
                a = jnp.concatenate([d_ref[h, 8 * qh:8 * qh + 8, :], pad], axis=1)
                acc = acc + pltpu.roll(a, wide - 8 * qh, 1)
            for r in range(1, 8):
                acc = jnp.where(sub == r, pltpu.roll(acc, wide - r, 1), acc)
            vec = jnp.sum(acc, axis=0, keepdims=True)
            far = (col <= PADK - MAX_REL) | (col > KB)
            tail = jnp.sum(jnp.where(far, vec, 0.0), axis=-1, keepdims=True)
            o_ref[h:h + 1, :] = jnp.where(col == wide - 1, tail, vec)

    return pl.pallas_call(
        body, name="bias_fold",
        out_shape=jax.ShapeDtypeStruct((HEADS, wide), F32),
        compiler_params=pltpu.CompilerParams(vmem_limit_bytes=VMEM_LIMIT),
    )(dscore)


def _dp_block(j):
    return (j + GATE_COLS // W_BLK) % N_DEV


def _in_proj_bwd(x, norm_g, dx2, dP, w_in_g):
    S = x.shape[0]
    ts = 256

    def body(x_ref, g_ref, dx2_ref, dp_ref, w_ref, gx_ref, dn_ref):
        @pl.when(pl.program_id(0) == 0)
        def _():
            dn_ref[...] = jnp.zeros_like(dn_ref)

        dh = jnp.zeros((ts, D_MODEL), F32)
        for j in range(N_DEV):
            b = _dp_block(j)
            dh = dh + _dot_nt(dp_ref[:, b * W_BLK:(b + 1) * W_BLK], w_ref[j])
        xf = x_ref[...]
        r = lax.rsqrt(jnp.mean(xf * xf, axis=-1, keepdims=True) + EPS)
        xn = xf * r
        dn_ref[0:1, :] += jnp.sum(dh * xn, axis=0, keepdims=True)
        dhg = dh * g_ref[...]
        gx_ref[...] = dx2_ref[...] + r * (dhg - xn * jnp.mean(dhg * xn, axis=-1, keepdims=True))

    tile = lambda w: pl.BlockSpec((ts, w), lambda i: (i, 0))
    return pl.pallas_call(
        body, name="in_proj_bwd", grid=(S // ts,),
        out_shape=[jax.ShapeDtypeStruct((S, D_MODEL), F32), jax.ShapeDtypeStruct((8, D_MODEL), F32)],
        in_specs=[tile(D_MODEL), pl.BlockSpec((1, D_MODEL), lambda i: (0, 0)), tile(D_MODEL),
                  tile(IN_COLS),
                  pl.BlockSpec((N_DEV, D_MODEL, W_BLK), lambda i: (0, 0, 0))],
        out_specs=[tile(D_MODEL), pl.BlockSpec((8, D_MODEL), lambda i: (0, 0))],
        compiler_params=_params(1),
    )(x, norm_g, dx2, dP, w_in_g)


SCATTER_MASKS = ((3, 4, 5, 2, 7, 6, 1, 0), (5, 2, 3, 4, 7, 6, 1, 0))


def _w_in_grad_scatter(ht, dP, d_proj, d_wo, pack, order):
    S = ht.shape[1]
    ts = min(S, 2048)
    nt = S // ts
    n_steps = 8

    def body(order_ref, ht_ref, d_ref, proj_hbm, wo_hbm, pack_hbm, g_ref, rproj, rwo, rpack,
             acc, stage, rsib, rici, d2d_send, d2d_recv, ici_send, ici_recv, small_send, small_recv, local_sems):
        k, i = pl.program_id(0), pl.program_id(1)
        x, y, c = _mesh_pos()
        my = _flat((x, y, c))
        sibling = (x, y, 1 - c)
        owners = [(x ^ (1 - c), y ^ c, c), (x ^ c, y ^ (1 - c), c), (1 - x, 1 - y, c)]
        peers = [sibling, (1 - x, y, c), (x, 1 - y, c), (1 - x, 1 - y, c),
                 (1 - x, y, 1 - c), (x, 1 - y, 1 - c), (1 - x, 1 - y, 1 - c)]
        small = ((proj_hbm, rproj, True), (wo_hbm, rwo, True), (pack_hbm, rpack, False))
        n_small = len(small)

        def small_copy(kk, a, receive=False):
            src, dst, per_peer = small[a]
            slot = _flat(peers[kk]) if receive else my
            return pltpu.make_async_remote_copy(
                src_ref=src.at[_flat(peers[kk])] if per_peer else src, dst_ref=dst.at[slot],
                send_sem=small_send.at[kk, a], recv_sem=small_recv.at[kk, a],
                device_id=peers[kk], device_id_type=MESH)

        def to_sibling(t):
            return pltpu.make_async_remote_copy(
                src_ref=stage.at[0], dst_ref=rsib.at[t % 2], send_sem=d2d_send.at[t], recv_sem=d2d_recv.at[t],
                device_id=sibling, device_id_type=MESH)

        def to_owner(t):
            return pltpu.make_async_remote_copy(
                src_ref=stage.at[1], dst_ref=rici.at[t], send_sem=ici_send.at[t], recv_sem=ici_recv.at[t],
                device_id=owners[t], device_id_type=MESH)

        own_small = [pltpu.make_async_copy(src.at[my] if per_peer else src, dst.at[my], local_sems.at[a])
                     for a, (src, dst, per_peer) in enumerate(small)]

        @pl.when((k == 0) & (i == 0))
        def _():
            for cp in own_small:
                cp.start()
            for kk in range(len(peers)):
                for a in range(n_small):
                    small_copy(kk, a).start()

        prod = _dot(ht_ref[...], d_ref[...])

        @pl.when(i == 0)
        def _():
            acc[...] = prod

        @pl.when(i > 0)
        def _():
            acc[...] += prod

        @pl.when(i == nt - 1)
        def _():
            for s in range(n_steps):
                @pl.when(k == s)
                def _():
                    t = s // 2
                    if s % 2 == 0:
                        if t >= 1:
                            to_sibling(t - 1).wait_send()
                        stage[0] = acc[...].astype(BF16)
                        to_sibling(t).start()
                    elif t < 3:
                        if t >= 1:
                            to_owner(t - 1).wait_send()
                        to_sibling(t).wait_recv()
                        stage[1] = (acc[...] + rsib[t % 2].astype(F32)).astype(BF16)
                        to_owner(t).start()
                    else:
                        to_sibling(t).wait_recv()
                        total = acc[...] + rsib[t % 2].astype(F32)
                        for j in range(3):
                            to_owner(j).wait_recv()
                            total = total + rici[j].astype(F32)
                        g_ref[...] = total
                        to_owner(2).wait_send()
                        to_sibling(3).wait_send()
                        for q in range(len(peers)):
                            for a in range(n_small):
                                small_copy(q, a).wait_send()
                                small_copy(q, a, receive=True).wait_recv()
                        for cp in own_small:
                            cp.wait()

    blk = (D_MODEL, W_BLK)
    grid_spec = pltpu.PrefetchScalarGridSpec(
        num_scalar_prefetch=1, grid=(n_steps, nt),
        in_specs=[pl.BlockSpec((D_MODEL, ts), lambda k, i, o: (0, i)),
                  pl.BlockSpec((ts, W_BLK), lambda k, i, o: (i, _dp_block(o[k]))),
                  ANY, ANY, ANY],
        out_specs=[pl.BlockSpec(blk, lambda k, i, o: (0, 0)), ANY, ANY, ANY],
        scratch_shapes=[pltpu.VMEM(blk, F32), pltpu.VMEM((2,) + blk, BF16),
                        pltpu.VMEM((2,) + blk, BF16), pltpu.VMEM((3,) + blk, BF16),
                        pltpu.SemaphoreType.DMA((4,)), pltpu.SemaphoreType.DMA((4,)),
                        pltpu.SemaphoreType.DMA((3,)), pltpu.SemaphoreType.DMA((3,)),
                        pltpu.SemaphoreType.DMA((7, 3)), pltpu.SemaphoreType.DMA((7, 3)),
                        pltpu.SemaphoreType.DMA((3,))])
    return pl.pallas_call(
        body, name="w_in_grad_scatter", grid_spec=grid_spec,
        out_shape=[jax.ShapeDtypeStruct(blk, F32),
                   jax.ShapeDtypeStruct(d_proj.shape, BF16), jax.ShapeDtypeStruct(d_wo.shape, BF16),
                   jax.ShapeDtypeStruct((N_DEV,) + pack.shape, F32)],
        compiler_params=_params(2),
    )(order, ht, dP, d_proj, d_wo, pack)


def _adamw(w, g, m, v):
    m = ADAM_B1 * m + (1.0 - ADAM_B1) * g
    v = ADAM_B2 * v + (1.0 - ADAM_B2) * (g * g)
    m_hat = m / (1.0 - ADAM_B1 ** ADAM_STEP)
    v_hat = v / (1.0 - ADAM_B2 ** ADAM_STEP)
    delta = -ADAM_LR * (m_hat / (jnp.sqrt(v_hat) + ADAM_EPS) + ADAM_WD * w)
    return delta, m, v


def _sum_adamw(parts, w, m, v, name):
    R, C = w.shape
    n = parts.shape[0]
    tr = min(R, 256)

    def body(p_ref, w_ref, m_ref, v_ref, g_ref, d_ref, nm_ref, nv_ref):
        g = p_ref[0].astype(F32)
        for s in range(1, n):
            g = g + p_ref[s].astype(F32)
        g_ref[...] = g
        d_ref[...], nm_ref[...], nv_ref[...] = _adamw(w_ref[...], g, m_ref[...], v_ref[...])

    tile = pl.BlockSpec((tr, C), lambda i: (i, 0))
    return pl.pallas_call(
        body, name=name, grid=(R // tr,),
        out_shape=[jax.ShapeDtypeStruct((R, C), F32)] * 4,
        in_specs=[pl.BlockSpec((n, tr, C), lambda i: (0, i, 0)), tile, tile, tile],
        out_specs=[tile] * 4,
        compiler_params=_params(1),
    )(parts, w, m, v)


def _adamw_mid(r_proj, r_wo, params):
    def body(rp_ref, rw_ref, *refs):
        ins, outs = refs[:9], refs[9:]

        def total(part):
            g = part(0).astype(F32)
            for s in range(1, N_DEV):
                g = g + part(s).astype(F32)
            return g

        grads = (total(lambda s: rp_ref[s, :, 0:128]), total(lambda s: rp_ref[s, :, 128:256]),
                 total(lambda s: rw_ref[s]))
        for n, g in enumerate(grads):
            w, m, v = (r[...] for r in ins[3 * n:3 * n + 3])
            outs[4 * n][...] = g
            outs[4 * n + 1][...], outs[4 * n + 2][...], outs[4 * n + 3][...] = _adamw(w, g, m, v)

    return pl.pallas_call(
        body, name="adamw_mid",
        out_shape=[jax.ShapeDtypeStruct(params[3 * n].shape, F32) for n in range(3) for _ in range(4)],
        compiler_params=pltpu.CompilerParams(vmem_limit_bytes=VMEM_LIMIT),
    )(r_proj, r_wo, *params)


def _adamw_small(r_pack, params):
    wide = 384

    def body(p_ref, *refs):
        ins, loss_ref, outs = refs[:15], refs[15], refs[16:]
        tot = p_ref[0]
        for s in range(1, N_DEV):
            tot = tot + p_ref[s]
        me = _flat(_mesh_pos())
        loss_ref[...] = jnp.sum(tot[2:3, :], axis=-1, keepdims=True)
        mine = pltpu.roll(tot[0:8, 0:D_CONV], (D_CONV - 64 * me) % D_CONV, 1)
        col = lax.broadcasted_iota(jnp.int32, (D_MODEL, wide), 0)
        idx = lax.broadcasted_iota(jnp.int32, (D_MODEL, wide), 1)
        near = (idx > MAX_REL - CHUNK) & (idx < 2 * MAX_REL) & (col == PADK + MAX_REL - idx)
        far = (idx == 2 * MAX_REL) & (col == D_MODEL - 1)
        perm = jnp.where(near | far, 1.0, 0.0).astype(F32)
        g_rel = jnp.dot(tot[8:16], perm, precision=lax.Precision.HIGHEST, preferred_element_type=F32)
        grads = (tot[0:1], tot[1:2], mine[3:6, 0:64], tot[6:7, 0:D_CONV], g_rel[:, 0:N_REL])
        for n, g in enumerate(grads):
            w, m, v = (r[...] for r in ins[3 * n:3 * n + 3])
            outs[4 * n][...] = g
            outs[4 * n + 1][...], outs[4 * n + 2][...], outs[4 * n + 3][...] = _adamw(w, g, m, v)

    return pl.pallas_call(
        body, name="adamw_small",
        out_shape=[jax.ShapeDtypeStruct((1, 1), F32)]
        + [jax.ShapeDtypeStruct(params[3 * n].shape, F32) for n in range(5) for _ in range(4)],
    )(r_pack, *params)


def _pad_row(a, width=D_MODEL):
    a = a.reshape(-1, a.shape[-1])
    return jnp.pad(a, ((0, 0), (0, width - a.shape[-1])))


def kernel(x, norm_g, w_in, rel_bias, w_att_out, conv_w, conv_b, w_conv_out, w_out, final_norm_g, loss_target, m_norm_g, m_w_in, m_rel_bias, m_w_att_out, m_conv_w, m_conv_b, m_w_conv_out, m_w_out, m_final_norm_g, v_norm_g, v_w_in, v_rel_bias, v_w_att_out, v_conv_w, v_conv_b, v_w_conv_out, v_w_out, v_final_norm_g):
    S = x.shape[1]
    x2d = x.reshape(S, D_MODEL)
    tgt = loss_target.reshape(S, D_MODEL)
    me = 4 * lax.axis_index("x") + 2 * lax.axis_index("y") + lax.axis_index("c")
    row = lambda a: a.reshape(1, D_MODEL)

    proj_sh = jnp.concatenate([w_att_out[0], w_conv_out[0]], axis=1).astype(BF16)
    cw_sh = jnp.pad(conv_w[0], ((0, 5), (0, 64)))
    P, ht, w_in_g, proj_g, w_out_g, cw_g = _gather_in_proj(
        x2d, norm_g, w_in[0].astype(BF16), [proj_sh, w_out[0].astype(BF16), cw_sh],
        me ^ _by_core(GATHER_MASKS))

    bias_tab = _bias_table(rel_bias[0])
    att, lse = _attn_fwd(P, bias_tab)
    dx2, dP, datt, d_wo, d_proj, sm1, sm2 = _token_local(
        x2d, tgt, P, att, proj_g, w_out_g.reshape(D_MODEL, D_MODEL), cw_g, conv_b, row(final_norm_g))
    dP, dscore = _attn_bwd(P, att, datt, lse, bias_tab, dP)
    dbias = _bias_fold(dscore)
    grad_x, dnorm = _in_proj_bwd(x2d, norm_g, dx2, dP, w_in_g)

    pack = jnp.concatenate([dnorm[0:1], sm1[0:2], _pad_row(sm2[0:4]), jnp.zeros((1, D_MODEL), F32), dbias],
                           axis=0)
    g_win_sum, r_proj, r_wo, r_pack = _w_in_grad_scatter(
        ht, dP, d_proj, d_wo.reshape(N_DEV, 128, D_MODEL), pack, me ^ _by_core(SCATTER_MASKS))

    res = {"w_in": _sum_adamw(g_win_sum[None], w_in[0], m_w_in[0], v_w_in[0], "adamw_w_in")}
    mid = _adamw_mid(r_proj, r_wo, (w_att_out[0], m_w_att_out[0], v_w_att_out[0],
                                    w_conv_out[0], m_w_conv_out[0], v_w_conv_out[0],
                                    w_out[0], m_w_out[0], v_w_out[0]))
    for n, name in enumerate(("w_att_out", "w_conv_out", "w_out")):
        res[name] = mid[4 * n:4 * n + 4]
    small = _adamw_small(r_pack, (norm_g, m_norm_g, v_norm_g,
                                  row(final_norm_g), row(m_final_norm_g), row(v_final_norm_g),
                                  conv_w[0], m_conv_w[0], v_conv_w[0], conv_b, m_conv_b, v_conv_b,
                                  rel_bias[0], m_rel_bias[0], v_rel_bias[0]))
    loss = small[0].reshape(())
    for n, name in enumerate(("norm_g", "final_norm_g", "conv_w", "conv_b", "rel_bias")):
        res[name] = small[1 + 4 * n:5 + 4 * n]

    leading = {"norm_g": (1, D_MODEL), "final_norm_g": (D_MODEL,), "conv_b": (1, D_CONV)}
    outs = []
    for kind in range(4):
        for name in ("norm_g", "w_in", "rel_bias", "w_att_out", "conv_w", "conv_b", "w_conv_out", "w_out",
                     "final_norm_g"):
            a = res[name][kind]
            outs.append(a.reshape(leading[name]) if name in leading else a[None])
    return (loss, grad_x.reshape(1, S, D_MODEL), *outs)
```

```python
import functools

import numpy as np
import jax
import jax.numpy as jnp
from jax import lax
from jax.experimental import pallas as pl
from jax.experimental.pallas import tpu as pltpu

F32 = jnp.float32
BF16 = jnp.bfloat16

D_MODEL = 1024
CHUNK = 64
N_LEFT = 8
HEADS = 8
D_ATT = 512
D_CONV = 512
MAX_REL = 128
N_REL = 2 * MAX_REL + 1
IN_COLS = 6144
EPS = 1e-6
NEG_BIG = -1e30
N_DEV = 8
W_BLK = IN_COLS // N_DEV
QB = 4 * CHUNK
KB = QB + N_LEFT * CHUNK
PADK = N_LEFT * CHUNK
SCALE = 64 ** -0.5
LOG2E = 1.4426950408889634
GATE_COLS = IN_COLS - 3 * D_ATT

ADAM_LR = 0.001
ADAM_B1 = 0.9
ADAM_B2 = 0.999
ADAM_EPS = 1e-08
ADAM_WD = 0.01
ADAM_STEP = 10

VMEM_LIMIT = 56 * 1024 * 1024

MESH = pl.DeviceIdType.MESH
ANY = pl.BlockSpec(memory_space=pl.ANY)


def _params(n_grid, vmem_limit=VMEM_LIMIT):
    return pltpu.CompilerParams(dimension_semantics=("arbitrary",) * n_grid,
                                vmem_limit_bytes=vmem_limit)


def _dot(a, b):
    return jnp.dot(a, b, preferred_element_type=F32)


def _dot_nt(a, b):
    return lax.dot_general(a, b, (((1,), (1,)), ((), ())), preferred_element_type=F32)


def _dot_tn(a, b):
    return lax.dot_general(a, b, (((0,), (0,)), ((), ())), preferred_element_type=F32)


def _sigmoid(z):
    return 0.5 * jnp.tanh(0.5 * z) + 0.5


def _mesh_pos():
    return lax.axis_index("x"), lax.axis_index("y"), lax.axis_index("c")


def _flat(p):
    return 4 * p[0] + 2 * p[1] + p[2]


def _by_core(masks):
    m0, m1 = (jnp.array(m, jnp.int32) for m in masks)
    return jnp.where(lax.axis_index("c") == 0, m0, m1)


GATHER_MASKS = ((0, 1, 4, 3, 2, 5, 6, 7), (0, 1, 2, 5, 4, 3, 6, 7))


def _gather_in_proj(x, norm_g, w_sh, smalls, order):
    S = x.shape[0]
    ts = 1024
    nt = S // ts
    n_small = len(smalls)
    n_steps = N_DEV

    def body(order_ref, x_ref, g_ref, w_hbm, *rest):
        small_in = rest[:n_small]
        p_ref, ht_ref, wg_hbm = rest[n_small:n_small + 3]
        small_out = rest[n_small + 3:2 * n_small + 3]
        (wbuf, hbuf, own_sem, send_sems, recv_sems, out_sems,
         small_send, small_recv, small_local) = rest[2 * n_small + 3:]
        k, i = pl.program_id(0), pl.program_id(1)
        x_, y_, c_ = _mesh_pos()
        me, sibling = (x_, y_, c_), (x_, y_, 1 - c_)
        my = _flat(me)
        chips = [(x_ ^ (1 - c_), y_ ^ c_), (x_ ^ c_, y_ ^ (1 - c_)), (1 - x_, 1 - y_)]
        peers = [sibling] + [(*chip, c_) for chip in chips] + [(*chip, 1 - c_) for chip in chips]

        def wcopy(sem, block, to, from_input=False):
            dst = wbuf.at[_flat(block)]
            return pltpu.make_async_remote_copy(
                src_ref=w_hbm if from_input else dst, dst_ref=dst,
                send_sem=send_sems.at[sem], recv_sem=recv_sems.at[sem], device_id=to, device_id_type=MESH)

        def small_copy(q, a, receive=False):
            slot = _flat(peers[q]) if receive else my
            return pltpu.make_async_remote_copy(
                src_ref=small_in[a], dst_ref=small_out[a].at[slot],
                send_sem=small_send.at[q, a], recv_sem=small_recv.at[q, a],
                device_id=peers[q], device_id_type=MESH)

        def keep(step, block):
            return pltpu.make_async_copy(wbuf.at[_flat(block)], wg_hbm.at[_flat(block)], out_sems.at[step])

        own = pltpu.make_async_copy(w_hbm, wbuf.at[my], own_sem)
        small_own = [pltpu.make_async_copy(small_in[a], small_out[a].at[my], small_local.at[a])
                     for a in range(n_small)]
        passed_on = [(*chips[1], 1 - c_), (*chips[0], 1 - c_), (*chips[2], 1 - c_)]
        arrivals = [me, sibling]
        for j in range(3):
            arrivals += [(*chips[j], c_), passed_on[j]]

        @pl.when(i == 0)
        def _():
            for kk in range(n_steps):
                @pl.when(k == kk)
                def _():
                    j = kk // 2 - 1
                    if kk == 0:
                        own.start()
                        wcopy(0, me, sibling, True).start()
                        wcopy(1, me, (*chips[0], c_), True).start()
                        own.wait()
                    elif kk == 1:
                        wcopy(0, sibling, me).wait_recv()
                        wcopy(2, me, (*chips[1], c_), True).start()
                    elif kk % 2 == 0:
                        wcopy(1 + j, (*chips[j], c_), me).wait_recv()
                        wcopy(4 + j, (*chips[j], c_), sibling).start()
                        if kk == 2:
                            wcopy(3, me, (*chips[2], c_), True).start()
                    else:
                        wcopy(4 + j, passed_on[j], me).wait_recv()
                        if kk == 3:
                            for cp in small_own:
                                cp.start()
                            for q in range(len(peers)):
                                for a in range(n_small):
                                    small_copy(q, a).start()
                    keep(kk, arrivals[kk]).start()

        row0 = pl.multiple_of(i * ts, ts)

        @pl.when(k == 0)
        def _():
            xf = x_ref[...]
            r = lax.rsqrt(jnp.mean(xf * xf, axis=-1, keepdims=True) + EPS)
            hf = (xf * r) * g_ref[...]
            hbuf[pl.ds(row0, ts), :] = hf.astype(BF16)
            ht_ref[...] = hf.astype(BF16).T

        p_ref[...] = _dot(hbuf[pl.ds(row0, ts), :], wbuf[order_ref[k]]).astype(BF16)

        @pl.when((k == n_steps - 1) & (i == nt - 1))
        def _():
            wcopy(0, me, sibling, True).wait_send()
            for j, chip in enumerate(chips):
                wcopy(1 + j, me, (*chip, c_), True).wait_send()
                wcopy(4 + j, (*chip, c_), sibling).wait_send()
            for kk in range(n_steps):
                keep(kk, arrivals[kk]).wait()
            for cp in small_own:
                cp.wait()
            for q in range(len(peers)):
                for a in range(n_small):
                    small_copy(q, a).wait_send()
                    small_copy(q, a, receive=True).wait_recv()

    first_pass = lambda k, i: jnp.where(k == 0, i, nt - 1)
    grid_spec = pltpu.PrefetchScalarGridSpec(
        num_scalar_prefetch=1, grid=(n_steps, nt),
        in_specs=[pl.BlockSpec((ts, D_MODEL), lambda k, i, o: (first_pass(k, i), 0)),
                  pl.BlockSpec((1, D_MODEL), lambda k, i, o: (0, 0)), ANY] + [ANY] * n_small,
        out_specs=[pl.BlockSpec((ts, W_BLK), lambda k, i, o: (i, o[k])),
                   pl.BlockSpec((D_MODEL, ts), lambda k, i, o: (0, first_pass(k, i))), ANY] + [ANY] * n_small,
        scratch_shapes=[pltpu.VMEM((N_DEV, D_MODEL, W_BLK), BF16), pltpu.VMEM((S, D_MODEL), BF16),
                        pltpu.SemaphoreType.DMA, pltpu.SemaphoreType.DMA((7,)), pltpu.SemaphoreType.DMA((7,)),
                        pltpu.SemaphoreType.DMA((n_steps,)),
                        pltpu.SemaphoreType.DMA((7, n_small)), pltpu.SemaphoreType.DMA((7, n_small)),
                        pltpu.SemaphoreType.DMA((n_small,))])
    return pl.pallas_call(
        body, name="gather_in_proj", grid_spec=grid_spec,
        out_shape=[jax.ShapeDtypeStruct((S, IN_COLS), BF16), jax.ShapeDtypeStruct((D_MODEL, S), BF16),
                   jax.ShapeDtypeStruct((N_DEV,) + w_sh.shape, BF16)]
        + [jax.ShapeDtypeStruct((N_DEV,) + s.shape, s.dtype) for s in smalls],
        compiler_params=_params(2),
    )(order, x, norm_g, w_sh, *smalls)


def _bias_table(rel_bias):
    wide = 1024

    def body(r_ref, o_ref):
        h = pl.program_id(0)
        col = lax.broadcasted_iota(jnp.int32, (1, wide), 1)
        k_minus_q = jnp.where(col < KB, col, col - wide)
        idx = jnp.clip(PADK - k_minus_q, -MAX_REL, MAX_REL) + MAX_REL
        f = jnp.zeros((1, wide), F32)
        for r in range(MAX_REL - CHUNK + 1, N_REL):
            f = jnp.where(idx == r, r_ref[h, r], f)
        kcol = lax.broadcasted_iota(jnp.int32, (1, KB), 1)
        kc = kcol >> 6
        sub = lax.broadcasted_iota(jnp.int32, (8, 1), 0)
        f8 = jnp.broadcast_to(f * LOG2E, (8, wide))
        base = f8
        for r in range(1, 8):
            base = jnp.where(sub == r, pltpu.roll(f8, r, 1), base)
        for qh in range(QB // 8):
            rows = (pltpu.roll(base, 8 * qh, 1) if qh else base)[:, 0:KB]
            qc = (8 * qh) // CHUNK
            band = (kc >= qc) & (kc <= qc + N_LEFT)
            for t in range(3):
                o_ref[t, 0, 8 * qh:8 * qh + 8, :] = jnp.where(band & (kcol >= PADK - t * QB), rows, NEG_BIG)

    return pl.pallas_call(
        body, name="bias_table", grid=(HEADS,),
        out_shape=jax.ShapeDtypeStruct((3, HEADS, QB, KB), F32),
        in_specs=[pl.BlockSpec(memory_space=pltpu.SMEM)],
        out_specs=pl.BlockSpec((3, 1, QB, KB), lambda h: (0, h, 0, 0)),
        compiler_params=_params(1),
    )(rel_bias)


KEY_GROUP = 4


def _load_keys(g, nb, p_hbm, kp, vp, sem):
    rows = KEY_GROUP * QB
    n_groups = p_hbm.shape[0] // rows

    def copies(c):
        src = pl.ds(c * rows, rows)
        dst = pl.ds(PADK + c * rows, rows)
        return (pltpu.make_async_copy(p_hbm.at[src, D_ATT:2 * D_ATT], kp.at[dst, :], sem.at[0, c]),
                pltpu.make_async_copy(p_hbm.at[src, 2 * D_ATT:3 * D_ATT], vp.at[dst, :], sem.at[1, c]))

    @pl.when(g == 0)
    def _():
        kp[0:PADK, :] = jnp.zeros((PADK, D_ATT), BF16)
        vp[0:PADK, :] = jnp.zeros((PADK, D_ATT), BF16)
        for c in range(n_groups):
            for cp in copies(c):
                cp.start()

    @pl.when((g % KEY_GROUP == 0) & (g < nb))
    def _():
        for cp in copies(g // KEY_GROUP):
            cp.wait()


def _attn_fwd(P, bias_tab):
    S = P.shape[0]
    nb = S // QB

    def body(q_ref, p_hbm, bias_ref, o_ref, lse_ref, kp, vp, sem):
        g = pl.program_id(0)
        _load_keys(g, nb, p_hbm, kp, vp, sem)
        start = pl.multiple_of(g * QB, QB)
        lane = lax.broadcasted_iota(jnp.int32, (1, 128), 1)
        for p in range(HEADS // 2):
            cols = slice(128 * p, 128 * (p + 1))
            qp = q_ref[:, cols] * SCALE
            kpair = kp[pl.ds(start, KB), cols]
            vpair = vp[pl.ds(start, KB), cols]
            outs = []
            for e in range(2):
                h = 2 * p + e
                lm = (lane < 64) if e == 0 else (lane >= 64)
                qm = jnp.where(lm, qp, jnp.zeros_like(qp))
                s = (_dot_nt(qm, kpair) * LOG2E + bias_ref[0, h]).astype(BF16)
                mx = jnp.max(s, axis=-1, keepdims=True)
                ex = jnp.exp2(s - mx)
                o = _dot(ex, jnp.where(lm, vpair, jnp.ones_like(vpair)))
                sums = pltpu.roll(o, 64, 1)
                outs.append(o / sums)
                lse_ref[:, h:h + 1] = mx.astype(F32) + jnp.log2(sums[:, 0:1] if e == 0 else o[:, 0:1])
            o_ref[:, cols] = jnp.where(lane < 64, outs[0], outs[1]).astype(BF16)

    return pl.pallas_call(
        body, name="attn_fwd", grid=(nb,),
        out_shape=[jax.ShapeDtypeStruct((S, D_ATT), BF16), jax.ShapeDtypeStruct((S, HEADS), F32)],
        in_specs=[pl.BlockSpec((QB, D_ATT), lambda g: (g, 0)), ANY,
                  pl.BlockSpec((1, HEADS, QB, KB), lambda g: (jnp.minimum(g, 2), 0, 0, 0))],
        out_specs=[pl.BlockSpec((QB, D_ATT), lambda g: (g, 0)),
                   pl.BlockSpec((QB, HEADS), lambda g: (g, 0))],
        scratch_shapes=[pltpu.VMEM((S + PADK, D_ATT), BF16), pltpu.VMEM((S + PADK, D_ATT), BF16),
                        pltpu.SemaphoreType.DMA((2, S // (KEY_GROUP * QB)))],
        compiler_params=_params(1),
    )(P, P, bias_tab)


def _token_local(x, tgt, P, att, proj_g, w_out, cw_g, conv_b, final_g):
    S = x.shape[0]
    ts = 256
    nt = S // ts
    hb = 16

    def body(x_ref, t_ref, s1_ref, s2_ref, s3_ref, h1_ref, h2_ref, att_ref,
             pg_ref, wo_ref, cwg_ref, cb_ref, g2_ref,
             dx2_ref, dg_ref, datt_ref, dwo_ref, dproj_ref, sm1_ref, sm2_ref,
             carry, wao_ref, wco_ref, cw_ref, dwo_acc, dwao_acc, dwco_acc):
        i = pl.program_id(0)
        t = nt - 1 - i

        @pl.when(i == 0)
        def _():
            dwo_acc[...] = jnp.zeros_like(dwo_acc)
            dwao_acc[...] = jnp.zeros_like(dwao_acc)
            dwco_acc[...] = jnp.zeros_like(dwco_acc)
            lane = lax.broadcasted_iota(jnp.int32, (1, 128), 1)
            for j in range(N_DEV):
                wao_ref[:, 128 * j:128 * (j + 1)] = pg_ref[j, :, 0:128]
                wco_ref[:, 128 * j:128 * (j + 1)] = pg_ref[j, :, 128:256]
            for p in range(N_DEV // 2):
                cw_ref[:, 128 * p:128 * (p + 1)] = jnp.where(
                    lane < 64, cwg_ref[2 * p], pltpu.roll(cwg_ref[2 * p + 1], 64, 1))
            sm1_ref[...] = jnp.zeros_like(sm1_ref)
            sm2_ref[...] = jnp.zeros_like(sm2_ref)
            carry[...] = jnp.zeros_like(carry)

        za = s1_ref[:, 0:512].astype(F32)
        gb = s1_ref[:, 512:1024].astype(F32)
        gc = s1_ref[:, 1024:1536].astype(F32)
        u = s2_ref[:, 0:512].astype(F32)
        zc = s2_ref[:, 512:1024].astype(F32)
        ga = jnp.concatenate([s2_ref[:, 1024:1536], s3_ref[:, 0:512]], axis=1).astype(F32)
        gv = s3_ref[:, 512:1536].astype(F32)
        att = att_ref[...].astype(F32)
        row = lax.broadcasted_iota(jnp.int32, (ts, 1), 0)

        sa = _sigmoid(za)
        silu_a = za * sa
        att_g = (att * silu_a).astype(BF16)
        y_att = _dot(att_g, wao_ref[...])

        cu = gc * u
        keep = jnp.where(t > 0, 1.0, 0.0).astype(F32)
        hcu = (h1_ref[:, 1024:1536].astype(F32) * h2_ref[:, 0:512].astype(F32)) * keep
        cu_m1 = jnp.where(row == 0, hcu[hb - 1:hb, :], pltpu.roll(cu, 1, 0))
        cu_m2 = jnp.where(row == 0, hcu[hb - 2:hb - 1, :],
                          jnp.where(row == 1, hcu[hb - 1:hb, :], pltpu.roll(cu, 2, 0)))
        w0, w1, w2 = cw_ref[0:1, :], cw_ref[1:2, :], cw_ref[2:3, :]
        vconv = w0 * cu_m2 + w1 * cu_m1 + w2 * cu + cb_ref[...]
        sc = _sigmoid(zc)
        silu_c = zc * sc
        cg = (gb * vconv * silu_c).astype(BF16)
        y_conv = _dot(cg, wco_ref[...])

        sga = _sigmoid(ga)
        sgv = _sigmoid(gv)
        m = (sga * y_att + sgv * y_conv).astype(BF16)
        x2 = x_ref[...] + _dot(m, wo_ref[...])
        r2 = lax.rsqrt(jnp.mean(x2 * x2, axis=-1, keepdims=True) + EPS)
        xn2 = x2 * r2
        g2 = g2_ref[...]
        err = xn2 * g2 - t_ref[...]
        sm1_ref[1:2, :] += jnp.sum(err * err, axis=0, keepdims=True) * (0.5 / D_MODEL)

        dy = err * (1.0 / D_MODEL)
        sm1_ref[0:1, :] += jnp.sum(dy * xn2, axis=0, keepdims=True)
        dxn = dy * g2
        dx2 = r2 * (dxn - xn2 * jnp.mean(dxn * xn2, axis=-1, keepdims=True))
        dx2_ref[...] = dx2
        dx2b = dx2.astype(BF16)
        dwo_acc[...] += _dot_tn(m, dx2b)
        dm = _dot_nt(dx2b, wo_ref[...])
        dya = (dm * sga).astype(BF16)
        dyc = (dm * sgv).astype(BF16)
        dg_ref[:, 2560:3584] = (dm * y_att * (sga * (1.0 - sga))).astype(BF16)
        dg_ref[:, 3584:4608] = (dm * y_conv * (sgv * (1.0 - sgv))).astype(BF16)
        dwao_acc[...] += _dot_tn(att_g, dya)
        dwco_acc[...] += _dot_tn(cg, dyc)
        datt_g = _dot_nt(dya, wao_ref[...])
        dcg = _dot_nt(dyc, wco_ref[...])
        datt_ref[...] = (datt_g * silu_a).astype(BF16)
        dg_ref[:, 0:512] = (datt_g * att * (sa + silu_a * (1.0 - sa))).astype(BF16)
        dg_ref[:, 512:1024] = (dcg * vconv * silu_c).astype(BF16)
        dg_ref[:, 2048:2560] = (dcg * gb * vconv * (sc + silu_c * (1.0 - sc))).astype(BF16)
        dv = dcg * gb * silu_c
        sm2_ref[3:4, :] += jnp.sum(dv, axis=0, keepdims=True)
        sm2_ref[0:1, :] += jnp.sum(dv * cu_m2, axis=0, keepdims=True)
        sm2_ref[1:2, :] += jnp.sum(dv * cu_m1, axis=0, keepdims=True)
        sm2_ref[2:3, :] += jnp.sum(dv * cu, axis=0, keepdims=True)
        nxt = carry[...]
        dv_p1 = jnp.where(row == ts - 1, nxt[0:1, :], pltpu.roll(dv, ts - 1, 0))
        dv_p2 = jnp.where(row == ts - 1, nxt[1:2, :],
                          jnp.where(row == ts - 2, nxt[0:1, :], pltpu.roll(dv, ts - 2, 0)))
        dcu = w2 * dv + w1 * dv_p1 + w0 * dv_p2
        carry[...] = dv[0:8, :]
        dg_ref[:, 1024:1536] = (dcu * u).astype(BF16)
        dg_ref[:, 1536:2048] = (dcu * gc).astype(BF16)

        @pl.when(i == nt - 1)
        def _():
            dwo_ref[...] = dwo_acc[...].astype(BF16)
            for j in range(N_DEV):
                dproj_ref[j, :, 0:128] = dwao_acc[:, 128 * j:128 * (j + 1)].astype(BF16)
                dproj_ref[j, :, 128:256] = dwco_acc[:, 128 * j:128 * (j + 1)].astype(BF16)

    tile = lambda w: pl.BlockSpec((ts, w), lambda i: (nt - 1 - i, 0))
    seg = lambda c: pl.BlockSpec((ts, 1536), lambda i: (nt - 1 - i, c))
    halo = lambda c: pl.BlockSpec((hb, 1536), lambda i: (jnp.maximum((nt - 1 - i) * (ts // hb) - 1, 0), c))
    full = lambda a: pl.BlockSpec(a.shape, lambda i: (0,) * a.ndim)
    acc = lambda r, c: pl.BlockSpec((r, c), lambda i: (0, 0))
    return pl.pallas_call(
        body, name="token_local", grid=(nt,),
        out_shape=[jax.ShapeDtypeStruct((S, D_MODEL), F32), jax.ShapeDtypeStruct((S, IN_COLS), BF16),
                   jax.ShapeDtypeStruct((S, D_ATT), BF16), jax.ShapeDtypeStruct((D_MODEL, D_MODEL), BF16),
                   jax.ShapeDtypeStruct(proj_g.shape, BF16),
                   jax.ShapeDtypeStruct((8, D_MODEL), F32), jax.ShapeDtypeStruct((8, D_CONV), F32)],
        in_specs=[tile(D_MODEL), tile(D_MODEL), seg(1), seg(2), seg(3), halo(1), halo(2), tile(D_ATT),
                  full(proj_g), full(w_out), full(cw_g), full(conv_b), full(final_g)],
        out_specs=[tile(D_MODEL), tile(GATE_COLS), tile(D_ATT), acc(D_MODEL, D_MODEL), full(proj_g),
                   acc(8, D_MODEL), acc(8, D_CONV)],
        scratch_shapes=[pltpu.VMEM((8, D_CONV), F32),
                        pltpu.VMEM((D_ATT, D_MODEL), BF16), pltpu.VMEM((D_CONV, D_MODEL), BF16),
                        pltpu.VMEM((8, D_CONV), F32), pltpu.VMEM((D_MODEL, D_MODEL), F32),
                        pltpu.VMEM((D_ATT, D_MODEL), F32), pltpu.VMEM((D_CONV, D_MODEL), F32)],
        compiler_params=_params(1),
    )(x, tgt, P, P, P, P, P, att, proj_g, w_out, cw_g, conv_b, final_g)


def _attn_bwd(P, att, datt, lse, bias_tab, dP):
    S = P.shape[0]
    nb = S // QB

    def body(q_ref, att_ref, datt_ref, lse_ref, p_hbm, bias_ref, dp_hbm, out_ref, db_ref,
             kp, vp, dq_ring, dk_ring, dv_ring, sem):
        g = pl.program_id(0)

        _load_keys(g, nb, p_hbm, kp, vp, sem)

        @pl.when(g == 0)
        def _():
            db_ref[...] = jnp.zeros_like(db_ref)
            dk_ring[...] = jnp.zeros_like(dk_ring)
            dv_ring[...] = jnp.zeros_like(dv_ring)

        s_new = g % 3
        s_mid = (g + 2) % 3
        s_old = (g + 1) % 3

        @pl.when(g < nb)
        def _():
            start = pl.multiple_of(g * QB, QB)
            lane = lax.broadcasted_iota(jnp.int32, (1, 128), 1)
            for p in range(HEADS // 2):
                cols = slice(128 * p, 128 * (p + 1))
                qp = q_ref[:, cols] * SCALE
                op = att_ref[:, cols].astype(F32)
                dop = datt_ref[:, cols]
                kpair = kp[pl.ds(start, KB), cols]
                vpair = vp[pl.ds(start, KB), cols]
                dqs = []
                dk_acc = jnp.zeros((KB, 128), F32)
                dv_acc = jnp.zeros((KB, 128), F32)
                for e in range(2):
                    h = 2 * p + e
                    lm = (lane < 64) if e == 0 else (lane >= 64)
                    qm = jnp.where(lm, qp, jnp.zeros_like(qp))
                    dom = jnp.where(lm, dop, jnp.zeros_like(dop))
                    s = _dot_nt(qm, kpair) * LOG2E + bias_ref[0, h]
                    pr = jnp.exp2(s - lse_ref[:, h:h + 1])
                    dp = _dot_nt(dom, vpair)
                    delta = jnp.sum(dom.astype(F32) * op, axis=-1, keepdims=True)
                    ds = pr * (dp - delta)
                    db_ref[h] += ds
                    dsb = ds.astype(BF16)
                    prb = pr.astype(BF16)
                    dqs.append(_dot(dsb, kpair) * SCALE)
                    dk_acc = dk_acc + _dot_tn(dsb, qm)
                    dv_acc = dv_acc + _dot_tn(prb, dom)
                dq_ring[s_new, :, cols] = jnp.where(lane < 64, dqs[0], dqs[1])
                dk_ring[s_old, :, cols] += dk_acc[0:QB]
                dk_ring[s_mid, :, cols] += dk_acc[QB:2 * QB]
                dk_ring[s_new, :, cols] = dk_acc[2 * QB:3 * QB]
                dv_ring[s_old, :, cols] += dv_acc[0:QB]
                dv_ring[s_mid, :, cols] += dv_acc[QB:2 * QB]
                dv_ring[s_new, :, cols] = dv_acc[2 * QB:3 * QB]

        @pl.when(g >= 2)
        def _():
            out_ref[:, 0:D_ATT] = dq_ring[s_old].astype(BF16)
            out_ref[:, D_ATT:2 * D_ATT] = dk_ring[s_old].astype(BF16)
            out_ref[:, 2 * D_ATT:3 * D_ATT] = dv_ring[s_old].astype(BF16)

    qblk = lambda w: pl.BlockSpec((QB, w), lambda g: (jnp.minimum(g, nb - 1), 0))
    return pl.pallas_call(
        body, name="attn_bwd", grid=(nb + 2,),
        out_shape=[jax.ShapeDtypeStruct((S, IN_COLS), BF16), jax.ShapeDtypeStruct((HEADS, QB, KB), F32)],
        in_specs=[qblk(D_ATT), qblk(D_ATT), qblk(D_ATT), qblk(HEADS), ANY,
                  pl.BlockSpec((1, HEADS, QB, KB), lambda g: (jnp.minimum(g, 2), 0, 0, 0)), ANY],
        out_specs=[pl.BlockSpec((QB, 3 * D_ATT), lambda g: (jnp.maximum(g - 2, 0), GATE_COLS // (3 * D_ATT))),
                   pl.BlockSpec((HEADS, QB, KB), lambda g: (0, 0, 0))],
        input_output_aliases={6: 0},
        scratch_shapes=[pltpu.VMEM((S + PADK, D_ATT), BF16), pltpu.VMEM((S + PADK, D_ATT), BF16),
                        pltpu.VMEM((3, QB, D_ATT), F32), pltpu.VMEM((3, QB, D_ATT), F32),
                        pltpu.VMEM((3, QB, D_ATT), F32), pltpu.SemaphoreType.DMA((2, S // (KEY_GROUP * QB)))],
        compiler_params=_params(1),
    )(P, att, datt, lse, P, bias_tab, dP)


def _bias_fold(dscore):
    wide = 1024

    def body(d_ref, o_ref):
        sub = lax.broadcasted_iota(jnp.int32, (8, 1), 0)
        col = lax.broadcasted_iota(jnp.int32, (1, wide), 1)
        pad = jnp.zeros((8, wide - KB), F32)
        for h in range(HEADS):
            acc = jnp.concatenate([d_ref[h, 0:8, :], pad], axis=1)
            for qh in range(1, QB // 8):
                a = jnp.concatenate([d_ref[h, 8 * qh:8 * qh + 8, :], pad], axis=1)
                acc = acc + pltpu.roll(a, wide - 8 * qh, 1)
            for r in range(1, 8):
                acc = jnp.where(sub == r, pltpu.roll(acc, wide - r, 1), acc)
            vec = jnp.sum(acc, axis=0, keepdims=True)
            far = (col <= PADK - MAX_REL) | (col > KB)
            tail = jnp.sum(jnp.where(far, vec, 0.0), axis=-1, keepdims=True)
            o_ref[h:h + 1, :] = jnp.where(col == wide - 1, tail, vec)

    return pl.pallas_call(
        body, name="bias_fold",
        out_shape=jax.ShapeDtypeStruct((HEADS, wide), F32),
        compiler_params=pltpu.CompilerParams(vmem_limit_bytes=VMEM_LIMIT),
    )(dscore)


def _dp_block(j):
    return (j + GATE_COLS // W_BLK) % N_DEV


def _in_proj_bwd(x, norm_g, dx2, dP, w_in_g):
    S = x.shape[0]
    ts = 256

    def body(x_ref, g_ref, dx2_ref, dp_ref, w_ref, gx_ref, dn_ref):
        @pl.when(pl.program_id(0) == 0)
        def _():
            dn_ref[...] = jnp.zeros_like(dn_ref)

        dh = jnp.zeros((ts, D_MODEL), F32)
        for j in range(N_DEV):
            b = _dp_block(j)
            dh = dh + _dot_nt(dp_ref[:, b * W_BLK:(b + 1) * W_BLK], w_ref[j])
        xf = x_ref[...]
        r = lax.rsqrt(jnp.mean(xf * xf, axis=-1, keepdims=True) + EPS)
        xn = xf * r
        dn_ref[0:1, :] += jnp.sum(dh * xn, axis=0, keepdims=True)
        dhg = dh * g_ref[...]
        gx_ref[...] = dx2_ref[...] + r * (dhg - xn * jnp.mean(dhg * xn, axis=-1, keepdims=True))

    tile = lambda w: pl.BlockSpec((ts, w), lambda i: (i, 0))
    return pl.pallas_call(
        body, name="in_proj_bwd", grid=(S // ts,),
        out_shape=[jax.ShapeDtypeStruct((S, D_MODEL), F32), jax.ShapeDtypeStruct((8, D_MODEL), F32)],
        in_specs=[tile(D_MODEL), pl.BlockSpec((1, D_MODEL), lambda i: (0, 0)), tile(D_MODEL),
                  tile(IN_COLS),
                  pl.BlockSpec((N_DEV, D_MODEL, W_BLK), lambda i: (0, 0, 0))],
        out_specs=[tile(D_MODEL), pl.BlockSpec((8, D_MODEL), lambda i: (0, 0))],
        compiler_params=_params(1),
    )(x, norm_g, dx2, dP, w_in_g)


SCATTER_MASKS = ((3, 4, 5, 2, 7, 6, 1, 0), (5, 2, 3, 4, 7, 6, 1, 0))


def _w_in_grad_scatter(ht, dP, d_proj, d_wo, pack, order):
    S = ht.shape[1]
    ts = min(S, 2048)
    nt = S // ts
    n_steps = 8

    def body(order_ref, ht_ref, d_ref, proj_hbm, wo_hbm, pack_hbm, g_ref, rproj, rwo, rpack,
             acc, stage, rsib, rici, d2d_send, d2d_recv, ici_send, ici_recv, small_send, small_recv, local_sems):
        k, i = pl.program_id(0), pl.program_id(1)
        x, y, c = _mesh_pos()
        my = _flat((x, y, c))
        sibling = (x, y, 1 - c)
        owners = [(x ^ (1 - c), y ^ c, c), (x ^ c, y ^ (1 - c), c), (1 - x, 1 - y, c)]
        peers = [sibling, (1 - x, y, c), (x, 1 - y, c), (1 - x, 1 - y, c),
                 (1 - x, y, 1 - c), (x, 1 - y, 1 - c), (1 - x, 1 - y, 1 - c)]
        small = ((proj_hbm, rproj, True), (wo_hbm, rwo, True), (pack_hbm, rpack, False))
        n_small = len(small)

        def small_copy(kk, a, receive=False):
            src, dst, per_peer = small[a]
            slot = _flat(peers[kk]) if receive else my
            return pltpu.make_async_remote_copy(
                src_ref=src.at[_flat(peers[kk])] if per_peer else src, dst_ref=dst.at[slot],
                send_sem=small_send.at[kk, a], recv_sem=small_recv.at[kk, a],
                device_id=peers[kk], device_id_type=MESH)

        def to_sibling(t):
            return pltpu.make_async_remote_copy(
                src_ref=stage.at[0], dst_ref=rsib.at[t % 2], send_sem=d2d_send.at[t], recv_sem=d2d_recv.at[t],
                device_id=sibling, device_id_type=MESH)

        def to_owner(t):
            return pltpu.make_async_remote_copy(
                src_ref=stage.at[1], dst_ref=rici.at[t], send_sem=ici_send.at[t], recv_sem=ici_recv.at[t],
                device_id=owners[t], device_id_type=MESH)

        own_small = [pltpu.make_async_copy(src.at[my] if per_peer else src, dst.at[my], local_sems.at[a])
                     for a, (src, dst, per_peer) in enumerate(small)]

        @pl.when((k == 0) & (i == 0))
        def _():
            for cp in own_small:
                cp.start()
            for kk in range(len(peers)):
                for a in range(n_small):
                    small_copy(kk, a).start()

        prod = _dot(ht_ref[...], d_ref[...])

        @pl.when(i == 0)
        def _():
            acc[...] = prod

        @pl.when(i > 0)
        def _():
            acc[...] += prod

        @pl.when(i == nt - 1)
        def _():
            for s in range(n_steps):
                @pl.when(k == s)
                def _():
                    t = s // 2
                    if s % 2 == 0:
                        if t >= 1:
                            to_sibling(t - 1).wait_send()
                        stage[0] = acc[...].astype(BF16)
                        to_sibling(t).start()
                    elif t < 3:
                        if t >= 1:
                            to_owner(t - 1).wait_send()
                        to_sibling(t).wait_recv()
                        stage[1] = (acc[...] + rsib[t % 2].astype(F32)).astype(BF16)
                        to_owner(t).start()
                    else:
                        to_sibling(t).wait_recv()
                        total = acc[...] + rsib[t % 2].astype(F32)
                        for j in range(3):
                            to_owner(j).wait_recv()
                            total = total + rici[j].astype(F32)
                        g_ref[...] = total
                        to_owner(2).wait_send()
                        to_sibling(3).wait_send()
                        for q in range(len(peers)):
                            for a in range(n_small):
                                small_copy(q, a).wait_send()
                                small_copy(q, a, receive=True).wait_recv()
                        for cp in own_small:
                            cp.wait()

    blk = (D_MODEL, W_BLK)
    grid_spec = pltpu.PrefetchScalarGridSpec(
        num_scalar_prefetch=1, grid=(n_steps, nt),
        in_specs=[pl.BlockSpec((D_MODEL, ts), lambda k, i, o: (0, i)),
                  pl.BlockSpec((ts, W_BLK), lambda k, i, o: (i, _dp_block(o[k]))),
                  ANY, ANY, ANY],
        out_specs=[pl.BlockSpec(blk, lambda k, i, o: (0, 0)), ANY, ANY, ANY],
        scratch_shapes=[pltpu.VMEM(blk, F32), pltpu.VMEM((2,) + blk, BF16),
                        pltpu.VMEM((2,) + blk, BF16), pltpu.VMEM((3,) + blk, BF16),
                        pltpu.SemaphoreType.DMA((4,)), pltpu.SemaphoreType.DMA((4,)),
                        pltpu.SemaphoreType.DMA((3,)), pltpu.SemaphoreType.DMA((3,)),
                        pltpu.SemaphoreType.DMA((7, 3)), pltpu.SemaphoreType.DMA((7, 3)),
                        pltpu.SemaphoreType.DMA((3,))])
    return pl.pallas_call(
        body, name="w_in_grad_scatter", grid_spec=grid_spec,
        out_shape=[jax.ShapeDtypeStruct(blk, F32),
                   jax.ShapeDtypeStruct(d_proj.shape, BF16), jax.ShapeDtypeStruct(d_wo.shape, BF16),
                   jax.ShapeDtypeStruct((N_DEV,) + pack.shape, F32)],
        compiler_params=_params(2),
    )(order, ht, dP, d_proj, d_wo, pack)


def _adamw(w, g, m, v):
    m = ADAM_B1 * m + (1.0 - ADAM_B1) * g
    v = ADAM_B2 * v + (1.0 - ADAM_B2) * (g * g)
    m_hat = m / (1.0 - ADAM_B1 ** ADAM_STEP)
    v_hat = v / (1.0 - ADAM_B2 ** ADAM_STEP)
    delta = -ADAM_LR * (m_hat / (jnp.sqrt(v_hat) + ADAM_EPS) + ADAM_WD * w)
    return delta, m, v


def _sum_adamw(parts, w, m, v, name):
    R, C = w.shape
    n = parts.shape[0]
    tr = min(R, 256)

    def body(p_ref, w_ref, m_ref, v_ref, g_ref, d_ref, nm_ref, nv_ref):
        g = p_ref[0].astype(F32)
        for s in range(1, n):
            g = g + p_ref[s].astype(F32)
        g_ref[...] = g
        d_ref[...], nm_ref[...], nv_ref[...] = _adamw(w_ref[...], g, m_ref[...], v_ref[...])

    tile = pl.BlockSpec((tr, C), lambda i: (i, 0))
    return pl.pallas_call(
        body, name=name, grid=(R // tr,),
        out_shape=[jax.ShapeDtypeStruct((R, C), F32)] * 4,
        in_specs=[pl.BlockSpec((n, tr, C), lambda i: (0, i, 0)), tile, tile, tile],
        out_specs=[tile] * 4,
        compiler_params=_params(1),
    )(parts, w, m, v)


def _adamw_mid(r_proj, r_wo, params):
    def body(rp_ref, rw_ref, *refs):
        ins, outs = refs[:9], refs[9:]

        def total(part):
            g = part(0).astype(F32)
            for s in range(1, N_DEV):
                g = g + part(s).astype(F32)
            return g

        grads = (total(lambda s: rp_ref[s, :, 0:128]), total(lambda s: rp_ref[s, :, 128:256]),
                 total(lambda s: rw_ref[s]))
        for n, g in enumerate(grads):
            w, m, v = (r[...] for r in ins[3 * n:3 * n + 3])
            outs[4 * n][...] = g
            outs[4 * n + 1][...], outs[4 * n + 2][...], outs[4 * n + 3][...] = _adamw(w, g, m, v)

    return pl.pallas_call(
        body, name="adamw_mid",
        out_shape=[jax.ShapeDtypeStruct(params[3 * n].shape, F32) for n in range(3) for _ in range(4)],
        compiler_params=pltpu.CompilerParams(vmem_limit_bytes=VMEM_LIMIT),
    )(r_proj, r_wo, *params)


def _adamw_small(r_pack, params):
    wide = 384

    def body(p_ref, *refs):
        ins, loss_ref, outs = refs[:15], refs[15], refs[16:]
        tot = p_ref[0]
        for s in range(1, N_DEV):
            tot = tot + p_ref[s]
        me = _flat(_mesh_pos())
        loss_ref[...] = jnp.sum(tot[2:3, :], axis=-1, keepdims=True)
        mine = pltpu.roll(tot[0:8, 0:D_CONV], (D_CONV - 64 * me) % D_CONV, 1)
        col = lax.broadcasted_iota(jnp.int32, (D_MODEL, wide), 0)
        idx = lax.broadcasted_iota(jnp.int32, (D_MODEL, wide), 1)
        near = (idx > MAX_REL - CHUNK) & (idx < 2 * MAX_REL) & (col == PADK + MAX_REL - idx)
        far = (idx == 2 * MAX_REL) & (col == D_MODEL - 1)
        perm = jnp.where(near | far, 1.0, 0.0).astype(F32)
        g_rel = jnp.dot(tot[8:16], perm, precision=lax.Precision.HIGHEST, preferred_element_type=F32)
        grads = (tot[0:1], tot[1:2], mine[3:6, 0:64], tot[6:7, 0:D_CONV], g_rel[:, 0:N_REL])
        for n, g in enumerate(grads):
            w, m, v = (r[...] for r in ins[3 * n:3 * n + 3])
            outs[4 * n][...] = g
            outs[4 * n + 1][...], outs[4 * n + 2][...], outs[4 * n + 3][...] = _adamw(w, g, m, v)

    return pl.pallas_call(
        body, name="adamw_small",
        out_shape=[jax.ShapeDtypeStruct((1, 1), F32)]
        + [jax.ShapeDtypeStruct(params[3 * n].shape, F32) for n in range(5) for _ in range(4)],
    )(r_pack, *params)


def _pad_row(a, width=D_MODEL):
    a = a.reshape(-1, a.shape[-1])
    return jnp.pad(a, ((0, 0), (0, width - a.shape[-1])))


def kernel(x, norm_g, w_in, rel_bias, w_att_out, conv_w, conv_b, w_conv_out, w_out, final_norm_g, loss_target, m_norm_g, m_w_in, m_rel_bias, m_w_att_out, m_conv_w, m_conv_b, m_w_conv_out, m_w_out, m_final_norm_g, v_norm_g, v_w_in, v_rel_bias, v_w_att_out, v_conv_w, v_conv_b, v_w_conv_out, v_w_out, v_final_norm_g):
    S = x.shape[1]
    x2d = x.reshape(S, D_MODEL)
    tgt = loss_target.reshape(S, D_MODEL)
    me = 4 * lax.axis_index("x") + 2 * lax.axis_index("y") + lax.axis_index("c")
    row = lambda a: a.reshape(1, D_MODEL)

    proj_sh = jnp.concatenate([w_att_out[0], w_conv_out[0]], axis=1).astype(BF16)
    cw_sh = jnp.pad(conv_w[0], ((0, 5), (0, 64)))
    P, ht, w_in_g, proj_g, w_out_g, cw_g = _gather_in_proj(
        x2d, norm_g, w_in[0].astype(BF16), [proj_sh, w_out[0].astype(BF16), cw_sh],
        me ^ _by_core(GATHER_MASKS))

    bias_tab = _bias_table(rel_bias[0])
    att, lse = _attn_fwd(P, bias_tab)
    dx2, dP, datt, d_wo, d_proj, sm1, sm2 = _token_local(
        x2d, tgt, P, att, proj_g, w_out_g.reshape(D_MODEL, D_MODEL), cw_g, conv_b, row(final_norm_g))
    dP, dscore = _attn_bwd(P, att, datt, lse, bias_tab, dP)
    dbias = _bias_fold(dscore)
    grad_x, dnorm = _in_proj_bwd(x2d, norm_g, dx2, dP, w_in_g)

    pack = jnp.concatenate([dnorm[0:1], sm1[0:2], _pad_row(sm2[0:4]), jnp.zeros((1, D_MODEL), F32), dbias],
                           axis=0)
    g_win_sum, r_proj, r_wo, r_pack = _w_in_grad_scatter(
        ht, dP, d_proj, d_wo.reshape(N_DEV, 128, D_MODEL), pack, me ^ _by_core(SCATTER_MASKS))

    res = {"w_in": _sum_adamw(g_win_sum[None], w_in[0], m_w_in[0], v_w_in[0], "adamw_w_in")}
    mid = _adamw_mid(r_proj, r_wo, (w_att_out[0], m_w_att_out[0], v_w_att_out[0],
                                    w_conv_out[0], m_w_conv_out[0], v_w_conv_out[0],
                                    w_out[0], m_w_out[0], v_w_out[0]))
    for n, name in enumerate(("w_att_out", "w_conv_out", "w_out")):
        res[name] = mid[4 * n:4 * n + 4]
    small = _adamw_small(r_pack, (norm_g, m_norm_g, v_norm_g,
                                  row(final_norm_g), row(m_final_norm_g), row(v_final_norm_g),
                                  conv_w[0], m_conv_w[0], v_conv_w[0], conv_b, m_conv_b, v_conv_b,
                                  rel_bias[0], m_rel_bias[0], v_rel_bias[0]))
    loss = small[0].reshape(())
    for n, name in enumerate(("norm_g", "final_norm_g", "conv_w", "conv_b", "rel_bias")):
        res[name] = small[1 + 4 * n:5 + 4 * n]

    leading = {"norm_g": (1, D_MODEL), "final_norm_g": (D_MODEL,), "conv_b": (1, D_CONV)}
    outs = []
    for kind in range(4):
        for name in ("norm_g", "w_in", "rel_bias", "w_att_out", "conv_w", "conv_b", "w_conv_out", "w_out",
                     "final_norm_g"):
            a = res[name][kind]
            outs.append(a.reshape(leading[name]) if name in leading else a[None])
    return (loss, grad_x.reshape(1, S, D_MODEL), *outs)
```

```python
import functools

import numpy as np
import jax
import jax.numpy as jnp
from jax import lax
from jax.experimental import pallas as pl
from jax.experimental.pallas import tpu as pltpu

F32 = jnp.float32
BF16 = jnp.bfloat16

D_MODEL = 1024
CHUNK = 64
N_LEFT = 8
HEADS = 8
D_ATT = 512
D_CONV = 512
MAX_REL = 128
N_REL = 2 * MAX_REL + 1
IN_COLS = 6144
EPS = 1e-6
NEG_BIG = -1e30
N_DEV = 8
W_BLK = IN_COLS // N_DEV
QB = 4 * CHUNK
KB = QB + N_LEFT * CHUNK
PADK = N_LEFT * CHUNK
SCALE = 64 ** -0.5
LOG2E = 1.4426950408889634
GATE_COLS = IN_COLS - 3 * D_ATT

ADAM_LR = 0.001
ADAM_B1 = 0.9
ADAM_B2 = 0.999
ADAM_EPS = 1e-08
ADAM_WD = 0.01
ADAM_STEP = 10

VMEM_LIMIT = 56 * 1024 * 1024

MESH = pl.DeviceIdType.MESH
ANY = pl.BlockSpec(memory_space=pl.ANY)


def _params(n_grid, vmem_limit=VMEM_LIMIT):
    return pltpu.CompilerParams(dimension_semantics=("arbitrary",) * n_grid,
                                vmem_limit_bytes=vmem_limit)


def _dot(a, b):
    return jnp.dot(a, b, preferred_element_type=F32)


def _dot_nt(a, b):
    return lax.dot_general(a, b, (((1,), (1,)), ((), ())), preferred_element_type=F32)


def _dot_tn(a, b):
    return lax.dot_general(a, b, (((0,), (0,)), ((), ())), preferred_element_type=F32)


def _sigmoid(z):
    return 0.5 * jnp.tanh(0.5 * z) + 0.5


def _mesh_pos():
    return lax.axis_index("x"), lax.axis_index("y"), lax.axis_index("c")


def _flat(p):
    return 4 * p[0] + 2 * p[1] + p[2]


def _by_core(masks):
    m0, m1 = (jnp.array(m, jnp.int32) for m in masks)
    return jnp.where(lax.axis_index("c") == 0, m0, m1)


GATHER_MASKS = ((0, 1, 4, 3, 2, 5, 6, 7), (0, 1, 2, 5, 4, 3, 6, 7))


def _gather_in_proj(x, norm_g, w_sh, smalls, order):
    S = x.shape[0]
    ts = 1024
    nt = S // ts
    n_small = len(smalls)
    n_steps = N_DEV

    def body(order_ref, x_ref, g_ref, w_hbm, *rest):
        small_in = rest[:n_small]
        p_ref, ht_ref, wg_hbm = rest[n_small:n_small + 3]
        small_out = rest[n_small + 3:2 * n_small + 3]
        (wbuf, hbuf, own_sem, send_sems, recv_sems, out_sems,
         small_send, small_recv, small_local) = rest[2 * n_small + 3:]
        k, i = pl.program_id(0), pl.program_id(1)
        x_, y_, c_ = _mesh_pos()
        me, sibling = (x_, y_, c_), (x_, y_, 1 - c_)
        my = _flat(me)
        chips = [(x_ ^ (1 - c_), y_ ^ c_), (x_ ^ c_, y_ ^ (1 - c_)), (1 - x_, 1 - y_)]
        peers = [sibling] + [(*chip, c_) for chip in chips] + [(*chip, 1 - c_) for chip in chips]

        def wcopy(sem, block, to, from_input=False):
            dst = wbuf.at[_flat(block)]
            return pltpu.make_async_remote_copy(
                src_ref=w_hbm if from_input else dst, dst_ref=dst,
                send_sem=send_sems.at[sem], recv_sem=recv_sems.at[sem], device_id=to, device_id_type=MESH)

        def small_copy(q, a, receive=False):
            slot = _flat(peers[q]) if receive else my
            return pltpu.make_async_remote_copy(
                src_ref=small_in[a], dst_ref=small_out[a].at[slot],
                send_sem=small_send.at[q, a], recv_sem=small_recv.at[q, a],
                device_id=peers[q], device_id_type=MESH)

        def keep(step, block):
            col = pl.multiple_of(_dp_block(_flat(block)) * W_BLK, 128)
            return pltpu.make_async_copy(wbuf.at[_flat(block)], wg_hbm.at[:, pl.ds(col, W_BLK)], out_sems.at[step])

        own = pltpu.make_async_copy(w_hbm, wbuf.at[my], own_sem)
        small_own = [pltpu.make_async_copy(small_in[a], small_out[a].at[my], small_local.at[a])
                     for a in range(n_small)]
        passed_on = [(*chips[1], 1 - c_), (*chips[0], 1 - c_), (*chips[2], 1 - c_)]
        arrivals = [me, sibling]
        for j in range(3):
            arrivals += [(*chips[j], c_), passed_on[j]]

        @pl.when(i == 0)
        def _():
            for kk in range(n_steps):
                @pl.when(k == kk)
                def _():
                    j = kk // 2 - 1
                    if kk == 0:
                        own.start()
                        wcopy(0, me, sibling, True).start()
                        wcopy(1, me, (*chips[0], c_), True).start()
                        own.wait()
                    elif kk == 1:
                        wcopy(0, sibling, me).wait_recv()
                        wcopy(2, me, (*chips[1], c_), True).start()
                    elif kk % 2 == 0:
                        wcopy(1 + j, (*chips[j], c_), me).wait_recv()
                        wcopy(4 + j, (*chips[j], c_), sibling).start()
                        if kk == 2:
                            wcopy(3, me, (*chips[2], c_), True).start()
                    else:
                        wcopy(4 + j, passed_on[j], me).wait_recv()
                        if kk == 3:
                            for cp in small_own:
                                cp.start()
                            for q in range(len(peers)):
                                for a in range(n_small):
                                    small_copy(q, a).start()
                    keep(kk, arrivals[kk]).start()

        row0 = pl.multiple_of(i * ts, ts)

        @pl.when(k == 0)
        def _():
            xf = x_ref[...]
            r = lax.rsqrt(jnp.mean(xf * xf, axis=-1, keepdims=True) + EPS)
            hf = (xf * r) * g_ref[...]
            hbuf[pl.ds(row0, ts), :] = hf.astype(BF16)
            ht_ref[...] = hf.astype(BF16).T

        p_ref[...] = _dot(hbuf[pl.ds(row0, ts), :], wbuf[order_ref[k]]).astype(BF16)

        @pl.when((k == n_steps - 1) & (i == nt - 1))
        def _():
            wcopy(0, me, sibling, True).wait_send()
            for j, chip in enumerate(chips):
                wcopy(1 + j, me, (*chip, c_), True).wait_send()
                wcopy(4 + j, (*chip, c_), sibling).wait_send()
            for kk in range(n_steps):
                keep(kk, arrivals[kk]).wait()
            for cp in small_own:
                cp.wait()
            for q in range(len(peers)):
                for a in range(n_small):
                    small_copy(q, a).wait_send()
                    small_copy(q, a, receive=True).wait_recv()

    first_pass = lambda k, i: jnp.where(k == 0, i, nt - 1)
    grid_spec = pltpu.PrefetchScalarGridSpec(
        num_scalar_prefetch=1, grid=(n_steps, nt),
        in_specs=[pl.BlockSpec((ts, D_MODEL), lambda k, i, o: (first_pass(k, i), 0)),
                  pl.BlockSpec((1, D_MODEL), lambda k, i, o: (0, 0)), ANY] + [ANY] * n_small,
        out_specs=[pl.BlockSpec((ts, W_BLK), lambda k, i, o: (i, o[k])),
                   pl.BlockSpec((D_MODEL, ts), lambda k, i, o: (0, first_pass(k, i))), ANY] + [ANY] * n_small,
        scratch_shapes=[pltpu.VMEM((N_DEV, D_MODEL, W_BLK), BF16), pltpu.VMEM((S, D_MODEL), BF16),
                        pltpu.SemaphoreType.DMA, pltpu.SemaphoreType.DMA((7,)), pltpu.SemaphoreType.DMA((7,)),
                        pltpu.SemaphoreType.DMA((n_steps,)),
                        pltpu.SemaphoreType.DMA((7, n_small)), pltpu.SemaphoreType.DMA((7, n_small)),
                        pltpu.SemaphoreType.DMA((n_small,))])
    return pl.pallas_call(
        body, name="gather_in_proj", grid_spec=grid_spec,
        out_shape=[jax.ShapeDtypeStruct((S, IN_COLS), BF16), jax.ShapeDtypeStruct((D_MODEL, S), BF16),
                   jax.ShapeDtypeStruct((D_MODEL, IN_COLS), BF16)]
        + [jax.ShapeDtypeStruct((N_DEV,) + s.shape, s.dtype) for s in smalls],
        compiler_params=_params(2),
    )(order, x, norm_g, w_sh, *smalls)


def _bias_table(rel_bias):
    wide = 1024

    def body(r_ref, o_ref):
        h = pl.program_id(0)
        col = lax.broadcasted_iota(jnp.int32, (1, wide), 1)
        k_minus_q = jnp.where(col < KB, col, col - wide)
        idx = jnp.clip(PADK - k_minus_q, -MAX_REL, MAX_REL) + MAX_REL
        f = jnp.zeros((1, wide), F32)
        for r in range(MAX_REL - CHUNK + 1, N_REL):
            f = jnp.where(idx == r, r_ref[h, r], f)
        kcol = lax.broadcasted_iota(jnp.int32, (1, KB), 1)
        kc = kcol >> 6
        sub = lax.broadcasted_iota(jnp.int32, (8, 1), 0)
        f8 = jnp.broadcast_to(f * LOG2E, (8, wide))
        base = f8
        for r in range(1, 8):
            base = jnp.where(sub == r, pltpu.roll(f8, r, 1), base)
        for qh in range(QB // 8):
            rows = (pltpu.roll(base, 8 * qh, 1) if qh else base)[:, 0:KB]
            qc = (8 * qh) // CHUNK
            band = (kc >= qc) & (kc <= qc + N_LEFT)
            for t in range(3):
                o_ref[t, 0, 8 * qh:8 * qh + 8, :] = jnp.where(band & (kcol >= PADK - t * QB), rows, NEG_BIG)

    return pl.pallas_call(
        body, name="bias_table", grid=(HEADS,),
        out_shape=jax.ShapeDtypeStruct((3, HEADS, QB, KB), F32),
        in_specs=[pl.BlockSpec(memory_space=pltpu.SMEM)],
        out_specs=pl.BlockSpec((3, 1, QB, KB), lambda h: (0, h, 0, 0)),
        compiler_params=_params(1),
    )(rel_bias)


KEY_GROUP = 4


def _load_keys(g, nb, p_hbm, kp, vp, sem):
    rows = KEY_GROUP * QB
    n_groups = p_hbm.shape[0] // rows

    def copies(c):
        src = pl.ds(c * rows, rows)
        dst = pl.ds(PADK + c * rows, rows)
        return (pltpu.make_async_copy(p_hbm.at[src, D_ATT:2 * D_ATT], kp.at[dst, :], sem.at[0, c]),
                pltpu.make_async_copy(p_hbm.at[src, 2 * D_ATT:3 * D_ATT], vp.at[dst, :], sem.at[1, c]))

    @pl.when(g == 0)
    def _():
        kp[0:PADK, :] = jnp.zeros((PADK, D_ATT), BF16)
        vp[0:PADK, :] = jnp.zeros((PADK, D_ATT), BF16)
        for c in range(n_groups):
            for cp in copies(c):
                cp.start()

    @pl.when((g % KEY_GROUP == 0) & (g < nb))
    def _():
        for cp in copies(g // KEY_GROUP):
            cp.wait()


def _attn_fwd(P, bias_tab):
    S = P.shape[0]
    nb = S // QB

    def body(q_ref, p_hbm, bias_ref, o_ref, lse_ref, kp, vp, sem):
        g = pl.program_id(0)
        _load_keys(g, nb, p_hbm, kp, vp, sem)
        start = pl.multiple_of(g * QB, QB)
        lane = lax.broadcasted_iota(jnp.int32, (1, 128), 1)
        for p in range(HEADS // 2):
            cols = slice(128 * p, 128 * (p + 1))
            qp = q_ref[:, cols] * SCALE
            kpair = kp[pl.ds(start, KB), cols]
            vpair = vp[pl.ds(start, KB), cols]
            outs = []
            for e in range(2):
                h = 2 * p + e
                lm = (lane < 64) if e == 0 else (lane >= 64)
                qm = jnp.where(lm, qp, jnp.zeros_like(qp))
                s = (_dot_nt(qm, kpair) * LOG2E + bias_ref[0, h]).astype(BF16)
                mx = jnp.max(s, axis=-1, keepdims=True)
                ex = jnp.exp2(s - mx)
                o = _dot(ex, jnp.where(lm, vpair, jnp.ones_like(vpair)))
                sums = pltpu.roll(o, 64, 1)
                outs.append(o / sums)
                lse_ref[:, h:h + 1] = mx.astype(F32) + jnp.log2(sums[:, 0:1] if e == 0 else o[:, 0:1])
            o_ref[:, cols] = jnp.where(lane < 64, outs[0], outs[1]).astype(BF16)

    return pl.pallas_call(
        body, name="attn_fwd", grid=(nb,),
        out_shape=[jax.ShapeDtypeStruct((S, D_ATT), BF16), jax.ShapeDtypeStruct((S, HEADS), F32)],
        in_specs=[pl.BlockSpec((QB, D_ATT), lambda g: (g, 0)), ANY,
                  pl.BlockSpec((1, HEADS, QB, KB), lambda g: (jnp.minimum(g, 2), 0, 0, 0))],
        out_specs=[pl.BlockSpec((QB, D_ATT), lambda g: (g, 0)),
                   pl.BlockSpec((QB, HEADS), lambda g: (g, 0))],
        scratch_shapes=[pltpu.VMEM((S + PADK, D_ATT), BF16), pltpu.VMEM((S + PADK, D_ATT), BF16),
                        pltpu.SemaphoreType.DMA((2, S // (KEY_GROUP * QB)))],
        compiler_params=_params(1),
    )(P, P, bias_tab)


def _token_local(x, tgt, P, att, proj_g, w_out, cw_g, conv_b, final_g):
    S = x.shape[0]
    ts = 256
    nt = S // ts
    hb = 16

    def body(x_ref, t_ref, s1_ref, s2_ref, s3_ref, h1_ref, h2_ref, att_ref,
             pg_ref, wo_ref, cwg_ref, cb_ref, g2_ref,
             dx2_ref, dg_ref, datt_ref, dwo_ref, dproj_ref, sm1_ref, sm2_ref,
             carry, wao_ref, wco_ref, cw_ref, dwo_acc, dwao_acc, dwco_acc):
        i = pl.program_id(0)
        t = nt - 1 - i

        @pl.when(i == 0)
        def _():
            dwo_acc[...] = jnp.zeros_like(dwo_acc)
            dwao_acc[...] = jnp.zeros_like(dwao_acc)
            dwco_acc[...] = jnp.zeros_like(dwco_acc)
            lane = lax.broadcasted_iota(jnp.int32, (1, 128), 1)
            for j in range(N_DEV):
                wao_ref[:, 128 * j:128 * (j + 1)] = pg_ref[j, :, 0:128]
                wco_ref[:, 128 * j:128 * (j + 1)] = pg_ref[j, :, 128:256]
            for p in range(N_DEV // 2):
                cw_ref[:, 128 * p:128 * (p + 1)] = jnp.where(
                    lane < 64, cwg_ref[2 * p], pltpu.roll(cwg_ref[2 * p + 1], 64, 1))
            sm1_ref[...] = jnp.zeros_like(sm1_ref)
            sm2_ref[...] = jnp.zeros_like(sm2_ref)
            carry[...] = jnp.zeros_like(carry)

        za = s1_ref[:, 0:512].astype(F32)
        gb = s1_ref[:, 512:1024].astype(F32)
        gc = s1_ref[:, 1024:1536].astype(F32)
        u = s2_ref[:, 0:512].astype(F32)
        zc = s2_ref[:, 512:1024].astype(F32)
        ga = jnp.concatenate([s2_ref[:, 1024:1536], s3_ref[:, 0:512]], axis=1).astype(F32)
        gv = s3_ref[:, 512:1536].astype(F32)
        att = att_ref[...].astype(F32)
        row = lax.broadcasted_iota(jnp.int32, (ts, 1), 0)

        sa = _sigmoid(za)
        silu_a = za * sa
        att_g = (att * silu_a).astype(BF16)
        y_att = _dot(att_g, wao_ref[...])

        cu = gc * u
        keep = jnp.where(t > 0, 1.0, 0.0).astype(F32)
        hcu = (h1_ref[:, 1024:1536].astype(F32) * h2_ref[:, 0:512].astype(F32)) * keep
        cu_m1 = jnp.where(row == 0, hcu[hb - 1:hb, :], pltpu.roll(cu, 1, 0))
        cu_m2 = jnp.where(row == 0, hcu[hb - 2:hb - 1, :],
                          jnp.where(row == 1, hcu[hb - 1:hb, :], pltpu.roll(cu, 2, 0)))
        w0, w1, w2 = cw_ref[0:1, :], cw_ref[1:2, :], cw_ref[2:3, :]
        vconv = w0 * cu_m2 + w1 * cu_m1 + w2 * cu + cb_ref[...]
        sc = _sigmoid(zc)
        silu_c = zc * sc
        cg = (gb * vconv * silu_c).astype(BF16)
        y_conv = _dot(cg, wco_ref[...])

        sga = _sigmoid(ga)
        sgv = _sigmoid(gv)
        m = (sga * y_att + sgv * y_conv).astype(BF16)
        x2 = x_ref[...] + _dot(m, wo_ref[...])
        r2 = lax.rsqrt(jnp.mean(x2 * x2, axis=-1, keepdims=True) + EPS)
        xn2 = x2 * r2
        g2 = g2_ref[...]
        err = xn2 * g2 - t_ref[...]
        sm1_ref[1:2, :] += jnp.sum(err * err, axis=0, keepdims=True) * (0.5 / D_MODEL)

        dy = err * (1.0 / D_MODEL)
        sm1_ref[0:1, :] += jnp.sum(dy * xn2, axis=0, keepdims=True)
        dxn = dy * g2
        dx2 = r2 * (dxn - xn2 * jnp.mean(dxn * xn2, axis=-1, keepdims=True))
        dx2_ref[...] = dx2
        dx2b = dx2.astype(BF16)
        dwo_acc[...] += _dot_tn(m, dx2b)
        dm = _dot_nt(dx2b, wo_ref[...])
        dya = (dm * sga).astype(BF16)
        dyc = (dm * sgv).astype(BF16)
        dg_ref[:, 2560:3584] = (dm * y_att * (sga * (1.0 - sga))).astype(BF16)
        dg_ref[:, 3584:4608] = (dm * y_conv * (sgv * (1.0 - sgv))).astype(BF16)
        dwao_acc[...] += _dot_tn(att_g, dya)
        dwco_acc[...] += _dot_tn(cg, dyc)
        datt_g = _dot_nt(dya, wao_ref[...])
        dcg = _dot_nt(dyc, wco_ref[...])
        datt_ref[...] = (datt_g * silu_a).astype(BF16)
        dg_ref[:, 0:512] = (datt_g * att * (sa + silu_a * (1.0 - sa))).astype(BF16)
        dg_ref[:, 512:1024] = (dcg * vconv * silu_c).astype(BF16)
        dg_ref[:, 2048:2560] = (dcg * gb * vconv * (sc + silu_c * (1.0 - sc))).astype(BF16)
        dv = dcg * gb * silu_c
        sm2_ref[3:4, :] += jnp.sum(dv, axis=0, keepdims=True)
        sm2_ref[0:1, :] += jnp.sum(dv * cu_m2, axis=0, keepdims=True)
        sm2_ref[1:2, :] += jnp.sum(dv * cu_m1, axis=0, keepdims=True)
        sm2_ref[2:3, :] += jnp.sum(dv * cu, axis=0, keepdims=True)
        nxt = carry[...]
        dv_p1 = jnp.where(row == ts - 1, nxt[0:1, :], pltpu.roll(dv, ts - 1, 0))
        dv_p2 = jnp.where(row == ts - 1, nxt[1:2, :],
                          jnp.where(row == ts - 2, nxt[0:1, :], pltpu.roll(dv, ts - 2, 0)))
        dcu = w2 * dv + w1 * dv_p1 + w0 * dv_p2
        carry[...] = dv[0:8, :]
        dg_ref[:, 1024:1536] = (dcu * u).astype(BF16)
        dg_ref[:, 1536:2048] = (dcu * gc).astype(BF16)

        @pl.when(i == nt - 1)
        def _():
            dwo_ref[...] = dwo_acc[...].astype(BF16)
            for j in range(N_DEV):
                dproj_ref[j, :, 0:128] = dwao_acc[:, 128 * j:128 * (j + 1)].astype(BF16)
                dproj_ref[j, :, 128:256] = dwco_acc[:, 128 * j:128 * (j + 1)].astype(BF16)

    tile = lambda w: pl.BlockSpec((ts, w), lambda i: (nt - 1 - i, 0))
    seg = lambda c: pl.BlockSpec((ts, 1536), lambda i: (nt - 1 - i, c))
    halo = lambda c: pl.BlockSpec((hb, 1536), lambda i: (jnp.maximum((nt - 1 - i) * (ts // hb) - 1, 0), c))
    full = lambda a: pl.BlockSpec(a.shape, lambda i: (0,) * a.ndim)
    acc = lambda r, c: pl.BlockSpec((r, c), lambda i: (0, 0))
    return pl.pallas_call(
        body, name="token_local", grid=(nt,),
        out_shape=[jax.ShapeDtypeStruct((S, D_MODEL), F32), jax.ShapeDtypeStruct((S, IN_COLS), BF16),
                   jax.ShapeDtypeStruct((S, D_ATT), BF16), jax.ShapeDtypeStruct((D_MODEL, D_MODEL), BF16),
                   jax.ShapeDtypeStruct(proj_g.shape, BF16),
                   jax.ShapeDtypeStruct((8, D_MODEL), F32), jax.ShapeDtypeStruct((8, D_CONV), F32)],
        in_specs=[tile(D_MODEL), tile(D_MODEL), seg(1), seg(2), seg(3), halo(1), halo(2), tile(D_ATT),
                  full(proj_g), full(w_out), full(cw_g), full(conv_b), full(final_g)],
        out_specs=[tile(D_MODEL), tile(GATE_COLS), tile(D_ATT), acc(D_MODEL, D_MODEL), full(proj_g),
                   acc(8, D_MODEL), acc(8, D_CONV)],
        scratch_shapes=[pltpu.VMEM((8, D_CONV), F32),
                        pltpu.VMEM((D_ATT, D_MODEL), BF16), pltpu.VMEM((D_CONV, D_MODEL), BF16),
                        pltpu.VMEM((8, D_CONV), F32), pltpu.VMEM((D_MODEL, D_MODEL), F32),
                        pltpu.VMEM((D_ATT, D_MODEL), F32), pltpu.VMEM((D_CONV, D_MODEL), F32)],
        compiler_params=_params(1),
    )(x, tgt, P, P, P, P, P, att, proj_g, w_out, cw_g, conv_b, final_g)


def _attn_bwd(P, att, datt, lse, bias_tab, dP):
    S = P.shape[0]
    nb = S // QB

    def body(q_ref, att_ref, datt_ref, lse_ref, p_hbm, bias_ref, dp_hbm, out_ref, db_ref,
             kp, vp, dq_ring, dk_ring, dv_ring, sem):
        g = pl.program_id(0)

        _load_keys(g, nb, p_hbm, kp, vp, sem)

        @pl.when(g == 0)
        def _():
            db_ref[...] = jnp.zeros_like(db_ref)
            dk_ring[...] = jnp.zeros_like(dk_ring)
            dv_ring[...] = jnp.zeros_like(dv_ring)

        s_new = g % 3
        s_mid = (g + 2) % 3
        s_old = (g + 1) % 3

        @pl.when(g < nb)
        def _():
            start = pl.multiple_of(g * QB, QB)
            lane = lax.broadcasted_iota(jnp.int32, (1, 128), 1)
            for p in range(HEADS // 2):
                cols = slice(128 * p, 128 * (p + 1))
                qp = q_ref[:, cols] * SCALE
                op = att_ref[:, cols].astype(F32)
                dop = datt_ref[:, cols]
                kpair = kp[pl.ds(start, KB), cols]
                vpair = vp[pl.ds(start, KB), cols]
                dqs = []
                dk_acc = jnp.zeros((KB, 128), F32)
                dv_acc = jnp.zeros((KB, 128), F32)
                for e in range(2):
                    h = 2 * p + e
                    lm = (lane < 64) if e == 0 else (lane >= 64)
                    qm = jnp.where(lm, qp, jnp.zeros_like(qp))
                    dom = jnp.where(lm, dop, jnp.zeros_like(dop))
                    s = _dot_nt(qm, kpair) * LOG2E + bias_ref[0, h]
                    pr = jnp.exp2(s - lse_ref[:, h:h + 1])
                    dp = _dot_nt(dom, vpair)
                    delta = jnp.sum(dom.astype(F32) * op, axis=-1, keepdims=True)
                    ds = pr * (dp - delta)
                    db_ref[h] += ds
                    dsb = ds.astype(BF16)
                    prb = pr.astype(BF16)
                    dqs.append(_dot(dsb, kpair) * SCALE)
                    dk_acc = dk_acc + _dot_tn(dsb, qm)
                    dv_acc = dv_acc + _dot_tn(prb, dom)
                dq_ring[s_new, :, cols] = jnp.where(lane < 64, dqs[0], dqs[1])
                dk_ring[s_old, :, cols] += dk_acc[0:QB]
                dk_ring[s_mid, :, cols] += dk_acc[QB:2 * QB]
                dk_ring[s_new, :, cols] = dk_acc[2 * QB:3 * QB]
                dv_ring[s_old, :, cols] += dv_acc[0:QB]
                dv_ring[s_mid, :, cols] += dv_acc[QB:2 * QB]
                dv_ring[s_new, :, cols] = dv_acc[2 * QB:3 * QB]

        @pl.when(g >= 2)
        def _():
            out_ref[:, 0:D_ATT] = dq_ring[s_old].astype(BF16)
            out_ref[:, D_ATT:2 * D_ATT] = dk_ring[s_old].astype(BF16)
            out_ref[:, 2 * D_ATT:3 * D_ATT] = dv_ring[s_old].astype(BF16)

    qblk = lambda w: pl.BlockSpec((QB, w), lambda g: (jnp.minimum(g, nb - 1), 0))
    return pl.pallas_call(
        body, name="attn_bwd", grid=(nb + 2,),
        out_shape=[jax.ShapeDtypeStruct((S, IN_COLS), BF16), jax.ShapeDtypeStruct((HEADS, QB, KB), F32)],
        in_specs=[qblk(D_ATT), qblk(D_ATT), qblk(D_ATT), qblk(HEADS), ANY,
                  pl.BlockSpec((1, HEADS, QB, KB), lambda g: (jnp.minimum(g, 2), 0, 0, 0)), ANY],
        out_specs=[pl.BlockSpec((QB, 3 * D_ATT), lambda g: (jnp.maximum(g - 2, 0), GATE_COLS // (3 * D_ATT))),
                   pl.BlockSpec((HEADS, QB, KB), lambda g: (0, 0, 0))],
        input_output_aliases={6: 0},
        scratch_shapes=[pltpu.VMEM((S + PADK, D_ATT), BF16), pltpu.VMEM((S + PADK, D_ATT), BF16),
                        pltpu.VMEM((3, QB, D_ATT), F32), pltpu.VMEM((3, QB, D_ATT), F32),
                        pltpu.VMEM((3, QB, D_ATT), F32), pltpu.SemaphoreType.DMA((2, S // (KEY_GROUP * QB)))],
        compiler_params=_params(1),
    )(P, att, datt, lse, P, bias_tab, dP)


def _bias_fold(dscore):
    wide = 1024

    def body(d_ref, o_ref):
        sub = lax.broadcasted_iota(jnp.int32, (8, 1), 0)
        col = lax.broadcasted_iota(jnp.int32, (1, wide), 1)
        pad = jnp.zeros((8, wide - KB), F32)
        for h in range(HEADS):
            acc = jnp.concatenate([d_ref[h, 0:8, :], pad], axis=1)
            for qh in range(1, QB // 8):
                a = jnp.concatenate([d_ref[h, 8 * qh:8 * qh + 8, :], pad], axis=1)
                acc = acc + pltpu.roll(a, wide - 8 * qh, 1)
            for r in range(1, 8):
                acc = jnp.where(sub == r, pltpu.roll(acc, wide - r, 1), acc)
            vec = jnp.sum(acc, axis=0, keepdims=True)
            far = (col <= PADK - MAX_REL) | (col > KB)
            tail = jnp.sum(jnp.where(far, vec, 0.0), axis=-1, keepdims=True)
            o_ref[h:h + 1, :] = jnp.where(col == wide - 1, tail, vec)

    return pl.pallas_call(
        body, name="bias_fold",
        out_shape=jax.ShapeDtypeStruct((HEADS, wide), F32),
        compiler_params=pltpu.CompilerParams(vmem_limit_bytes=VMEM_LIMIT),
    )(dscore)


def _dp_block(j):
    return (j + GATE_COLS // W_BLK) % N_DEV


def _in_proj_bwd(x, norm_g, dx2, dP, w_in_g):
    S = x.shape[0]
    ts = 512

    def body(x_ref, g_ref, dx2_ref, dp_ref, w_ref, gx_ref, dn_ref):
        @pl.when(pl.program_id(0) == 0)
        def _():
            dn_ref[...] = jnp.zeros_like(dn_ref)

        dh = _dot_nt(dp_ref[...], w_ref[...])
        xf = x_ref[...]
        r = lax.rsqrt(jnp.mean(xf * xf, axis=-1, keepdims=True) + EPS)
        xn = xf * r
        dn_ref[0:1, :] += jnp.sum(dh * xn, axis=0, keepdims=True)
        dhg = dh * g_ref[...]
        gx_ref[...] = dx2_ref[...] + r * (dhg - xn * jnp.mean(dhg * xn, axis=-1, keepdims=True))

    tile = lambda w: pl.BlockSpec((ts, w), lambda i: (i, 0))
    return pl.pallas_call(
        body, name="in_proj_bwd", grid=(S // ts,),
        out_shape=[jax.ShapeDtypeStruct((S, D_MODEL), F32), jax.ShapeDtypeStruct((8, D_MODEL), F32)],
        in_specs=[tile(D_MODEL), pl.BlockSpec((1, D_MODEL), lambda i: (0, 0)), tile(D_MODEL),
                  tile(IN_COLS),
                  pl.BlockSpec((D_MODEL, IN_COLS), lambda i: (0, 0))],
        out_specs=[tile(D_MODEL), pl.BlockSpec((8, D_MODEL), lambda i: (0, 0))],
        compiler_params=_params(1),
    )(x, norm_g, dx2, dP, w_in_g)


SCATTER_MASKS = ((3, 4, 5, 2, 7, 6, 1, 0), (5, 2, 3, 4, 7, 6, 1, 0))


def _w_in_grad_scatter(ht, dP, d_proj, d_wo, pack, order):
    S = ht.shape[1]
    ts = min(S, 2048)
    nt = S // ts
    n_steps = 8

    def body(order_ref, ht_ref, d_ref, proj_hbm, wo_hbm, pack_hbm, g_ref, rproj, rwo, rpack,
             acc, stage, rsib, rici, d2d_send, d2d_recv, ici_send, ici_recv, small_send, small_recv, local_sems):
        k, i = pl.program_id(0), pl.program_id(1)
        x, y, c = _mesh_pos()
        my = _flat((x, y, c))
        sibling = (x, y, 1 - c)
        owners = [(x ^ (1 - c), y ^ c, c), (x ^ c, y ^ (1 - c), c), (1 - x, 1 - y, c)]
        peers = [sibling, (1 - x, y, c), (x, 1 - y, c), (1 - x, 1 - y, c),
                 (1 - x, y, 1 - c), (x, 1 - y, 1 - c), (1 - x, 1 - y, 1 - c)]
        small = ((proj_hbm, rproj, True), (wo_hbm, rwo, True), (pack_hbm, rpack, False))
        n_small = len(small)

        def small_copy(kk, a, receive=False):
            src, dst, per_peer = small[a]
            slot = _flat(peers[kk]) if receive else my
            return pltpu.make_async_remote_copy(
                src_ref=src.at[_flat(peers[kk])] if per_peer else src, dst_ref=dst.at[slot],
                send_sem=small_send.at[kk, a], recv_sem=small_recv.at[kk, a],
                device_id=peers[kk], device_id_type=MESH)

        def to_sibling(t):
            return pltpu.make_async_remote_copy(
                src_ref=stage.at[0], dst_ref=rsib.at[t % 2], send_sem=d2d_send.at[t], recv_sem=d2d_recv.at[t],
                device_id=sibling, device_id_type=MESH)

        def to_owner(t):
            return pltpu.make_async_remote_copy(
                src_ref=stage.at[1], dst_ref=rici.at[t], send_sem=ici_send.at[t], recv_sem=ici_recv.at[t],
                device_id=owners[t], device_id_type=MESH)

        own_small = [pltpu.make_async_copy(src.at[my] if per_peer else src, dst.at[my], local_sems.at[a])
                     for a, (src, dst, per_peer) in enumerate(small)]

        @pl.when((k == 0) & (i == 0))
        def _():
            for cp in own_small:
                cp.start()
            for kk in range(len(peers)):
                for a in range(n_small):
                    small_copy(kk, a).start()

        prod = _dot(ht_ref[...], d_ref[...])

        @pl.when(i == 0)
        def _():
            acc[...] = prod

        @pl.when(i > 0)
        def _():
            acc[...] += prod

        @pl.when(i == nt - 1)
        def _():
            for s in range(n_steps):
                @pl.when(k == s)
                def _():
                    t = s // 2
                    if s % 2 == 0:
                        if t >= 1:
                            to_sibling(t - 1).wait_send()
                        stage[0] = acc[...].astype(BF16)
                        to_sibling(t).start()
                    elif t < 3:
                        if t >= 1:
                            to_owner(t - 1).wait_send()
                        to_sibling(t).wait_recv()
                        stage[1] = (acc[...] + rsib[t % 2].astype(F32)).astype(BF16)
                        to_owner(t).start()
                    else:
                        to_sibling(t).wait_recv()
                        total = acc[...] + rsib[t % 2].astype(F32)
                        for j in range(3):
                            to_owner(j).wait_recv()
                            total = total + rici[j].astype(F32)
                        g_ref[...] = total
                        to_owner(2).wait_send()
                        to_sibling(3).wait_send()
                        for q in range(len(peers)):
                            for a in range(n_small):
                                small_copy(q, a).wait_send()
                                small_copy(q, a, receive=True).wait_recv()
                        for cp in own_small:
                            cp.wait()

    blk = (D_MODEL, W_BLK)
    grid_spec = pltpu.PrefetchScalarGridSpec(
        num_scalar_prefetch=1, grid=(n_steps, nt),
        in_specs=[pl.BlockSpec((D_MODEL, ts), lambda k, i, o: (0, i)),
                  pl.BlockSpec((ts, W_BLK), lambda k, i, o: (i, _dp_block(o[k]))),
                  ANY, ANY, ANY],
        out_specs=[pl.BlockSpec(blk, lambda k, i, o: (0, 0)), ANY, ANY, ANY],
        scratch_shapes=[pltpu.VMEM(blk, F32), pltpu.VMEM((2,) + blk, BF16),
                        pltpu.VMEM((2,) + blk, BF16), pltpu.VMEM((3,) + blk, BF16),
                        pltpu.SemaphoreType.DMA((4,)), pltpu.SemaphoreType.DMA((4,)),
                        pltpu.SemaphoreType.DMA((3,)), pltpu.SemaphoreType.DMA((3,)),
                        pltpu.SemaphoreType.DMA((7, 3)), pltpu.SemaphoreType.DMA((7, 3)),
                        pltpu.SemaphoreType.DMA((3,))])
    return pl.pallas_call(
        body, name="w_in_grad_scatter", grid_spec=grid_spec,
        out_shape=[jax.ShapeDtypeStruct(blk, F32),
                   jax.ShapeDtypeStruct(d_proj.shape, BF16), jax.ShapeDtypeStruct(d_wo.shape, BF16),
                   jax.ShapeDtypeStruct((N_DEV,) + pack.shape, F32)],
        compiler_params=_params(2),
    )(order, ht, dP, d_proj, d_wo, pack)


def _adamw(w, g, m, v):
    m = ADAM_B1 * m + (1.0 - ADAM_B1) * g
    v = ADAM_B2 * v + (1.0 - ADAM_B2) * (g * g)
    m_hat = m / (1.0 - ADAM_B1 ** ADAM_STEP)
    v_hat = v / (1.0 - ADAM_B2 ** ADAM_STEP)
    delta = -ADAM_LR * (m_hat / (jnp.sqrt(v_hat) + ADAM_EPS) + ADAM_WD * w)
    return delta, m, v


def _sum_adamw(parts, w, m, v, name):
    R, C = w.shape
    n = parts.shape[0]
    tr = min(R, 256)

    def body(p_ref, w_ref, m_ref, v_ref, g_ref, d_ref, nm_ref, nv_ref):
        g = p_ref[0].astype(F32)
        for s in range(1, n):
            g = g + p_ref[s].astype(F32)
        g_ref[...] = g
        d_ref[...], nm_ref[...], nv_ref[...] = _adamw(w_ref[...], g, m_ref[...], v_ref[...])

    tile = pl.BlockSpec((tr, C), lambda i: (i, 0))
    return pl.pallas_call(
        body, name=name, grid=(R // tr,),
        out_shape=[jax.ShapeDtypeStruct((R, C), F32)] * 4,
        in_specs=[pl.BlockSpec((n, tr, C), lambda i: (0, i, 0)), tile, tile, tile],
        out_specs=[tile] * 4,
        compiler_params=_params(1),
    )(parts, w, m, v)


def _adamw_mid(r_proj, r_wo, params):
    def body(rp_ref, rw_ref, *refs):
        ins, outs = refs[:9], refs[9:]

        def total(part):
            g = part(0).astype(F32)
            for s in range(1, N_DEV):
                g = g + part(s).astype(F32)
            return g

        grads = (total(lambda s: rp_ref[s, :, 0:128]), total(lambda s: rp_ref[s, :, 128:256]),
                 total(lambda s: rw_ref[s]))
        for n, g in enumerate(grads):
            w, m, v = (r[...] for r in ins[3 * n:3 * n + 3])
            outs[4 * n][...] = g
            outs[4 * n + 1][...], outs[4 * n + 2][...], outs[4 * n + 3][...] = _adamw(w, g, m, v)

    return pl.pallas_call(
        body, name="adamw_mid",
        out_shape=[jax.ShapeDtypeStruct(params[3 * n].shape, F32) for n in range(3) for _ in range(4)],
        compiler_params=pltpu.CompilerParams(vmem_limit_bytes=VMEM_LIMIT),
    )(r_proj, r_wo, *params)


def _adamw_small(r_pack, params):
    wide = 384

    def body(p_ref, *refs):
        ins, loss_ref, outs = refs[:15], refs[15], refs[16:]
        tot = p_ref[0]
        for s in range(1, N_DEV):
            tot = tot + p_ref[s]
        me = _flat(_mesh_pos())
        loss_ref[...] = jnp.sum(tot[2:3, :], axis=-1, keepdims=True)
        mine = pltpu.roll(tot[0:8, 0:D_CONV], (D_CONV - 64 * me) % D_CONV, 1)
        col = lax.broadcasted_iota(jnp.int32, (D_MODEL, wide), 0)
        idx = lax.broadcasted_iota(jnp.int32, (D_MODEL, wide), 1)
        near = (idx > MAX_REL - CHUNK) & (idx < 2 * MAX_REL) & (col == PADK + MAX_REL - idx)
        far = (idx == 2 * MAX_REL) & (col == D_MODEL - 1)
        perm = jnp.where(near | far, 1.0, 0.0).astype(F32)
        g_rel = jnp.dot(tot[8:16], perm, precision=lax.Precision.HIGHEST, preferred_element_type=F32)
        grads = (tot[0:1], tot[1:2], mine[3:6, 0:64], tot[6:7, 0:D_CONV], g_rel[:, 0:N_REL])
        for n, g in enumerate(grads):
            w, m, v = (r[...] for r in ins[3 * n:3 * n + 3])
            outs[4 * n][...] = g
            outs[4 * n + 1][...], outs[4 * n + 2][...], outs[4 * n + 3][...] = _adamw(w, g, m, v)

    return pl.pallas_call(
        body, name="adamw_small",
        out_shape=[jax.ShapeDtypeStruct((1, 1), F32)]
        + [jax.ShapeDtypeStruct(params[3 * n].shape, F32) for n in range(5) for _ in range(4)],
    )(r_pack, *params)


def _pad_row(a, width=D_MODEL):
    a = a.reshape(-1, a.shape[-1])
    return jnp.pad(a, ((0, 0), (0, width - a.shape[-1])))


def kernel(x, norm_g, w_in, rel_bias, w_att_out, conv_w, conv_b, w_conv_out, w_out, final_norm_g, loss_target, m_norm_g, m_w_in, m_rel_bias, m_w_att_out, m_conv_w, m_conv_b, m_w_conv_out, m_w_out, m_final_norm_g, v_norm_g, v_w_in, v_rel_bias, v_w_att_out, v_conv_w, v_conv_b, v_w_conv_out, v_w_out, v_final_norm_g):
    S = x.shape[1]
    x2d = x.reshape(S, D_MODEL)
    tgt = loss_target.reshape(S, D_MODEL)
    me = 4 * lax.axis_index("x") + 2 * lax.axis_index("y") + lax.axis_index("c")
    row = lambda a: a.reshape(1, D_MODEL)

    proj_sh = jnp.concatenate([w_att_out[0], w_conv_out[0]], axis=1).astype(BF16)
    cw_sh = jnp.pad(conv_w[0], ((0, 5), (0, 64)))
    P, ht, w_in_g, proj_g, w_out_g, cw_g = _gather_in_proj(
        x2d, norm_g, w_in[0].astype(BF16), [proj_sh, w_out[0].astype(BF16), cw_sh],
        me ^ _by_core(GATHER_MASKS))

    bias_tab = _bias_table(rel_bias[0])
    att, lse = _attn_fwd(P, bias_tab)
    dx2, dP, datt, d_wo, d_proj, sm1, sm2 = _token_local(
        x2d, tgt, P, att, proj_g, w_out_g.reshape(D_MODEL, D_MODEL), cw_g, conv_b, row(final_norm_g))
    dP, dscore = _attn_bwd(P, att, datt, lse, bias_tab, dP)
    dbias = _bias_fold(dscore)
    grad_x, dnorm = _in_proj_bwd(x2d, norm_g, dx2, dP, w_in_g)

    pack = jnp.concatenate([dnorm[0:1], sm1[0:2], _pad_row(sm2[0:4]), jnp.zeros((1, D_MODEL), F32), dbias],
                           axis=0)
    g_win_sum, r_proj, r_wo, r_pack = _w_in_grad_scatter(
        ht, dP, d_proj, d_wo.reshape(N_DEV, 128, D_MODEL), pack, me ^ _by_core(SCATTER_MASKS))

    res = {"w_in": _sum_adamw(g_win_sum[None], w_in[0], m_w_in[0], v_w_in[0], "adamw_w_in")}
    mid = _adamw_mid(r_proj, r_wo, (w_att_out[0], m_w_att_out[0], v_w_att_out[0],
                                    w_conv_out[0], m_w_conv_out[0], v_w_conv_out[0],
                                    w_out[0], m_w_out[0], v_w_out[0]))
    for n, name in enumerate(("w_att_out", "w_conv_out", "w_out")):
        res[name] = mid[4 * n:4 * n + 4]
    small = _adamw_small(r_pack, (norm_g, m_norm_g, v_norm_g,
                                  row(final_norm_g), row(m_final_norm_g), row(v_final_norm_g),
                                  conv_w[0], m_conv_w[0], v_conv_w[0], conv_b, m_conv_b, v_conv_b,
                                  rel_bias[0], m_rel_bias[0], v_rel_bias[0]))
    loss = small[0].reshape(())
    for n, name in enumerate(("norm_g", "final_norm_g", "conv_w", "conv_b", "rel_bias")):
        res[name] = small[1 + 4 * n:5 + 4 * n]

    leading = {"norm_g": (1, D_MODEL), "final_norm_g": (D_MODEL,), "conv_b": (1, D_CONV)}
    outs = []
    for kind in range(4):
        for name in ("norm_g", "w_in", "rel_bias", "w_att_out", "conv_w", "conv_b", "w_conv_out", "w_out",
                     "final_norm_g"):
            a = res[name][kind]
            outs.append(a.reshape(leading[name]) if name in leading else a[None])
    return (loss, grad_x.reshape(1, S, D_MODEL), *outs)
```

```python
import functools

import numpy as np
import jax
import jax.numpy as jnp
from jax import lax
from jax.experimental import pallas as pl
from jax.experimental.pallas import tpu as pltpu

F32 = jnp.float32
BF16 = jnp.bfloat16

D_MODEL = 1024
CHUNK = 64
N_LEFT = 8
HEADS = 8
D_ATT = 512
D_CONV = 512
MAX_REL = 128
N_REL = 2 * MAX_REL + 1
IN_COLS = 6144
EPS = 1e-6
NEG_BIG = -1e30
N_DEV = 8
W_BLK = IN_COLS // N_DEV
QB = 4 * CHUNK
KB = QB + N_LEFT * CHUNK
PADK = N_LEFT * CHUNK
SCALE = 64 ** -0.5
LOG2E = 1.4426950408889634
GATE_COLS = IN_COLS - 3 * D_ATT

ADAM_LR = 0.001
ADAM_B1 = 0.9
ADAM_B2 = 0.999
ADAM_EPS = 1e-08
ADAM_WD = 0.01
ADAM_STEP = 10

VMEM_LIMIT = 56 * 1024 * 1024

MESH = pl.DeviceIdType.MESH
ANY = pl.BlockSpec(memory_space=pl.ANY)


def _params(n_grid, vmem_limit=VMEM_LIMIT):
    return pltpu.CompilerParams(dimension_semantics=("arbitrary",) * n_grid,
                                vmem_limit_bytes=vmem_limit)


def _dot(a, b):
    return jnp.dot(a, b, preferred_element_type=F32)


def _dot_nt(a, b):
    return lax.dot_general(a, b, (((1,), (1,)), ((), ())), preferred_element_type=F32)


def _dot_tn(a, b):
    return lax.dot_general(a, b, (((0,), (0,)), ((), ())), preferred_element_type=F32)


def _sigmoid(z):
    return 0.5 * jnp.tanh(0.5 * z) + 0.5


def _mesh_pos():
    return lax.axis_index("x"), lax.axis_index("y"), lax.axis_index("c")


def _flat(p):
    return 4 * p[0] + 2 * p[1] + p[2]


def _by_core(masks):
    m0, m1 = (jnp.array(m, jnp.int32) for m in masks)
    return jnp.where(lax.axis_index("c") == 0, m0, m1)


GATHER_MASKS = ((0, 1, 4, 3, 2, 5, 6, 7), (0, 1, 2, 5, 4, 3, 6, 7))


def _gather_in_proj(x, norm_g, w_sh, smalls, order):
    S = x.shape[0]
    ts = 1024
    nt = S // ts
    n_small = len(smalls)
    n_steps = N_DEV

    def body(order_ref, x_ref, g_ref, w_hbm, *rest):
        small_in = rest[:n_small]
        p_ref, ht_ref, wg_hbm = rest[n_small:n_small + 3]
        small_out = rest[n_small + 3:2 * n_small + 3]
        (wbuf, hbuf, own_sem, send_sems, recv_sems, out_sems,
         small_send, small_recv, small_local) = rest[2 * n_small + 3:]
        k, i = pl.program_id(0), pl.program_id(1)
        x_, y_, c_ = _mesh_pos()
        me, sibling = (x_, y_, c_), (x_, y_, 1 - c_)
        my = _flat(me)
        chips = [(x_ ^ (1 - c_), y_ ^ c_), (x_ ^ c_, y_ ^ (1 - c_)), (1 - x_, 1 - y_)]
        peers = [sibling] + [(*chip, c_) for chip in chips] + [(*chip, 1 - c_) for chip in chips]

        def wcopy(sem, block, to, from_input=False):
            dst = wbuf.at[_flat(block)]
            return pltpu.make_async_remote_copy(
                src_ref=w_hbm if from_input else dst, dst_ref=dst,
                send_sem=send_sems.at[sem], recv_sem=recv_sems.at[sem], device_id=to, device_id_type=MESH)

        def small_copy(q, a, receive=False):
            slot = _flat(peers[q]) if receive else my
            return pltpu.make_async_remote_copy(
                src_ref=small_in[a], dst_ref=small_out[a].at[slot],
                send_sem=small_send.at[q, a], recv_sem=small_recv.at[q, a],
                device_id=peers[q], device_id_type=MESH)

        def keep(step, block):
            col = pl.multiple_of(_dp_block(_flat(block)) * W_BLK, 128)
            return pltpu.make_async_copy(wbuf.at[_flat(block)], wg_hbm.at[:, pl.ds(col, W_BLK)], out_sems.at[step])

        own = pltpu.make_async_copy(w_hbm, wbuf.at[my], own_sem)
        small_own = [pltpu.make_async_copy(small_in[a], small_out[a].at[my], small_local.at[a])
                     for a in range(n_small)]
        passed_on = [(*chips[1], 1 - c_), (*chips[0], 1 - c_), (*chips[2], 1 - c_)]
        arrivals = [me, sibling]
        for j in range(3):
            arrivals += [(*chips[j], c_), passed_on[j]]

        @pl.when(i == 0)
        def _():
            for kk in range(n_steps):
                @pl.when(k == kk)
                def _():
                    j = kk // 2 - 1
                    if kk == 0:
                        own.start()
                        wcopy(0, me, sibling, True).start()
                        wcopy(1, me, (*chips[0], c_), True).start()
                        own.wait()
                    elif kk == 1:
                        wcopy(0, sibling, me).wait_recv()
                        wcopy(2, me, (*chips[1], c_), True).start()
                    elif kk % 2 == 0:
                        wcopy(1 + j, (*chips[j], c_), me).wait_recv()
                        wcopy(4 + j, (*chips[j], c_), sibling).start()
                        if kk == 2:
                            wcopy(3, me, (*chips[2], c_), True).start()
                    else:
                        wcopy(4 + j, passed_on[j], me).wait_recv()
                        if kk == 3:
                            for cp in small_own:
                                cp.start()
                            for q in range(len(peers)):
                                for a in range(n_small):
                                    small_copy(q, a).start()
                    keep(kk, arrivals[kk]).start()

        row0 = pl.multiple_of(i * ts, ts)

        @pl.when(k == 0)
        def _():
            xf = x_ref[...]
            r = lax.rsqrt(jnp.mean(xf * xf, axis=-1, keepdims=True) + EPS)
            hf = (xf * r) * g_ref[...]
            hbuf[pl.ds(row0, ts), :] = hf.astype(BF16)
            ht_ref[...] = hf.astype(BF16).T

        p_ref[...] = _dot(hbuf[pl.ds(row0, ts), :], wbuf[order_ref[k]]).astype(BF16)

        @pl.when((k == n_steps - 1) & (i == nt - 1))
        def _():
            wcopy(0, me, sibling, True).wait_send()
            for j, chip in enumerate(chips):
                wcopy(1 + j, me, (*chip, c_), True).wait_send()
                wcopy(4 + j, (*chip, c_), sibling).wait_send()
            for kk in range(n_steps):
                keep(kk, arrivals[kk]).wait()
            for cp in small_own:
                cp.wait()
            for q in range(len(peers)):
                for a in range(n_small):
                    small_copy(q, a).wait_send()
                    small_copy(q, a, receive=True).wait_recv()

    first_pass = lambda k, i: jnp.where(k == 0, i, nt - 1)
    grid_spec = pltpu.PrefetchScalarGridSpec(
        num_scalar_prefetch=1, grid=(n_steps, nt),
        in_specs=[pl.BlockSpec((ts, D_MODEL), lambda k, i, o: (first_pass(k, i), 0)),
                  pl.BlockSpec((1, D_MODEL), lambda k, i, o: (0, 0)), ANY] + [ANY] * n_small,
        out_specs=[pl.BlockSpec((ts, W_BLK), lambda k, i, o: (i, o[k])),
                   pl.BlockSpec((D_MODEL, ts), lambda k, i, o: (0, first_pass(k, i))), ANY] + [ANY] * n_small,
        scratch_shapes=[pltpu.VMEM((N_DEV, D_MODEL, W_BLK), BF16), pltpu.VMEM((S, D_MODEL), BF16),
                        pltpu.SemaphoreType.DMA, pltpu.SemaphoreType.DMA((7,)), pltpu.SemaphoreType.DMA((7,)),
                        pltpu.SemaphoreType.DMA((n_steps,)),
                        pltpu.SemaphoreType.DMA((7, n_small)), pltpu.SemaphoreType.DMA((7, n_small)),
                        pltpu.SemaphoreType.DMA((n_small,))])
    return pl.pallas_call(
        body, name="gather_in_proj", grid_spec=grid_spec,
        out_shape=[jax.ShapeDtypeStruct((S, IN_COLS), BF16), jax.ShapeDtypeStruct((D_MODEL, S), BF16),
                   jax.ShapeDtypeStruct((D_MODEL, IN_COLS), BF16)]
        + [jax.ShapeDtypeStruct((N_DEV,) + s.shape, s.dtype) for s in smalls],
        compiler_params=_params(2),
    )(order, x, norm_g, w_sh, *smalls)


def _bias_table(rel_bias):
    wide = 1024

    def body(r_ref, o_ref):
        h = pl.program_id(0)
        col = lax.broadcasted_iota(jnp.int32, (1, wide), 1)
        k_minus_q = jnp.where(col < KB, col, col - wide)
        idx = jnp.clip(PADK - k_minus_q, -MAX_REL, MAX_REL) + MAX_REL
        f = jnp.zeros((1, wide), F32)
        for r in range(MAX_REL - CHUNK + 1, N_REL):
            f = jnp.where(idx == r, r_ref[h, r], f)
        kcol = lax.broadcasted_iota(jnp.int32, (1, KB), 1)
        kc = kcol >> 6
        sub = lax.broadcasted_iota(jnp.int32, (8, 1), 0)
        f8 = jnp.broadcast_to(f * LOG2E, (8, wide))
        base = f8
        for r in range(1, 8):
            base = jnp.where(sub == r, pltpu.roll(f8, r, 1), base)
        for qh in range(QB // 8):
            rows = (pltpu.roll(base, 8 * qh, 1) if qh else base)[:, 0:KB]
            qc = (8 * qh) // CHUNK
            band = (kc >= qc) & (kc <= qc + N_LEFT)
            for t in range(3):
                o_ref[t, 0, 8 * qh:8 * qh + 8, :] = jnp.where(band & (kcol >= PADK - t * QB), rows, NEG_BIG)

    return pl.pallas_call(
        body, name="bias_table", grid=(HEADS,),
        out_shape=jax.ShapeDtypeStruct((3, HEADS, QB, KB), F32),
        in_specs=[pl.BlockSpec(memory_space=pltpu.SMEM)],
        out_specs=pl.BlockSpec((3, 1, QB, KB), lambda h: (0, h, 0, 0)),
        compiler_params=_params(1),
    )(rel_bias)


KEY_GROUP = 4


def _load_keys(g, nb, p_hbm, kp, vp, sem):
    rows = KEY_GROUP * QB
    n_groups = p_hbm.shape[0] // rows

    def copies(c):
        src = pl.ds(c * rows, rows)
        dst = pl.ds(PADK + c * rows, rows)
        return (pltpu.make_async_copy(p_hbm.at[src, D_ATT:2 * D_ATT], kp.at[dst, :], sem.at[0, c]),
                pltpu.make_async_copy(p_hbm.at[src, 2 * D_ATT:3 * D_ATT], vp.at[dst, :], sem.at[1, c]))

    @pl.when(g == 0)
    def _():
        kp[0:PADK, :] = jnp.zeros((PADK, D_ATT), BF16)
        vp[0:PADK, :] = jnp.zeros((PADK, D_ATT), BF16)
        for c in range(n_groups):
            for cp in copies(c):
                cp.start()

    @pl.when((g % KEY_GROUP == 0) & (g < nb))
    def _():
        for cp in copies(g // KEY_GROUP):
            cp.wait()


def _attn_fwd(P, bias_tab):
    S = P.shape[0]
    nb = S // QB

    def body(q_ref, p_hbm, bias_ref, o_ref, lse_ref, kp, vp, sem):
        g = pl.program_id(0)
        _load_keys(g, nb, p_hbm, kp, vp, sem)
        start = pl.multiple_of(g * QB, QB)
        lane = lax.broadcasted_iota(jnp.int32, (1, 128), 1)
        for p in range(HEADS // 2):
            cols = slice(128 * p, 128 * (p + 1))
            qp = q_ref[:, cols] * SCALE
            kpair = kp[pl.ds(start, KB), cols]
            vpair = vp[pl.ds(start, KB), cols]
            outs = []
            for e in range(2):
                h = 2 * p + e
                lm = (lane < 64) if e == 0 else (lane >= 64)
                qm = jnp.where(lm, qp, jnp.zeros_like(qp))
                s = (_dot_nt(qm, kpair) * LOG2E + bias_ref[0, h]).astype(BF16)
                mx = jnp.max(s, axis=-1, keepdims=True)
                ex = jnp.exp2(s - mx)
                o = _dot(ex, jnp.where(lm, vpair, jnp.ones_like(vpair)))
                sums = pltpu.roll(o, 64, 1)
                outs.append(o / sums)
                lse_ref[:, h:h + 1] = mx.astype(F32) + jnp.log2(sums[:, 0:1] if e == 0 else o[:, 0:1])
            o_ref[:, cols] = jnp.where(lane < 64, outs[0], outs[1]).astype(BF16)

    return pl.pallas_call(
        body, name="attn_fwd", grid=(nb,),
        out_shape=[jax.ShapeDtypeStruct((S, D_ATT), BF16), jax.ShapeDtypeStruct((S, HEADS), F32)],
        in_specs=[pl.BlockSpec((QB, D_ATT), lambda g: (g, 0)), ANY,
                  pl.BlockSpec((1, HEADS, QB, KB), lambda g: (jnp.minimum(g, 2), 0, 0, 0))],
        out_specs=[pl.BlockSpec((QB, D_ATT), lambda g: (g, 0)),
                   pl.BlockSpec((QB, HEADS), lambda g: (g, 0))],
        scratch_shapes=[pltpu.VMEM((S + PADK, D_ATT), BF16), pltpu.VMEM((S + PADK, D_ATT), BF16),
                        pltpu.SemaphoreType.DMA((2, S // (KEY_GROUP * QB)))],
        compiler_params=_params(1),
    )(P, P, bias_tab)


def _token_local(x, tgt, P, att, proj_g, w_out, cw_g, conv_b, final_g):
    S = x.shape[0]
    ts = 256
    nt = S // ts
    hb = 16

    def body(x_ref, t_ref, s1_ref, s2_ref, s3_ref, h1_ref, h2_ref, att_ref,
             pg_ref, wo_ref, cwg_ref, cb_ref, g2_ref,
             dx2_ref, dg_ref, datt_ref, dwo_ref, dproj_ref, sm1_ref, sm2_ref,
             carry, wao_ref, wco_ref, cw_ref, dwo_acc, dwao_acc, dwco_acc):
        i = pl.program_id(0)
        t = nt - 1 - i

        @pl.when(i == 0)
        def _():
            dwo_acc[...] = jnp.zeros_like(dwo_acc)
            dwao_acc[...] = jnp.zeros_like(dwao_acc)
            dwco_acc[...] = jnp.zeros_like(dwco_acc)
            lane = lax.broadcasted_iota(jnp.int32, (1, 128), 1)
            for j in range(N_DEV):
                wao_ref[:, 128 * j:128 * (j + 1)] = pg_ref[j, :, 0:128]
                wco_ref[:, 128 * j:128 * (j + 1)] = pg_ref[j, :, 128:256]
            for p in range(N_DEV // 2):
                cw_ref[:, 128 * p:128 * (p + 1)] = jnp.where(
                    lane < 64, cwg_ref[2 * p], pltpu.roll(cwg_ref[2 * p + 1], 64, 1))
            sm1_ref[...] = jnp.zeros_like(sm1_ref)
            sm2_ref[...] = jnp.zeros_like(sm2_ref)
            carry[...] = jnp.zeros_like(carry)

        za = s1_ref[:, 0:512].astype(F32)
        gb = s1_ref[:, 512:1024].astype(F32)
        gc = s1_ref[:, 1024:1536].astype(F32)
        u = s2_ref[:, 0:512].astype(F32)
        zc = s2_ref[:, 512:1024].astype(F32)
        ga = jnp.concatenate([s2_ref[:, 1024:1536], s3_ref[:, 0:512]], axis=1)
        gv = s3_ref[:, 512:1536]
        att = att_ref[...].astype(F32)
        row = lax.broadcasted_iota(jnp.int32, (ts, 1), 0)

        sa = _sigmoid(za)
        silu_a = za * sa
        att_g = (att * silu_a).astype(BF16)
        y_att = _dot(att_g, wao_ref[...])

        cu = gc * u
        keep = jnp.where(t > 0, 1.0, 0.0).astype(F32)
        hcu = (h1_ref[:, 1024:1536].astype(F32) * h2_ref[:, 0:512].astype(F32)) * keep
        cu_m1 = jnp.where(row == 0, hcu[hb - 1:hb, :], pltpu.roll(cu, 1, 0))
        cu_m2 = jnp.where(row == 0, hcu[hb - 2:hb - 1, :],
                          jnp.where(row == 1, hcu[hb - 1:hb, :], pltpu.roll(cu, 2, 0)))
        w0, w1, w2 = cw_ref[0:1, :], cw_ref[1:2, :], cw_ref[2:3, :]
        vconv = w0 * cu_m2 + w1 * cu_m1 + w2 * cu + cb_ref[...]
        sc = _sigmoid(zc)
        silu_c = zc * sc
        cg = (gb * vconv * silu_c).astype(BF16)
        y_conv = _dot(cg, wco_ref[...])

        sga = _sigmoid(ga)
        sgv = _sigmoid(gv)
        yab, ycb = y_att.astype(BF16), y_conv.astype(BF16)
        m = sga * yab + sgv * ycb
        x2 = x_ref[...] + _dot(m, wo_ref[...])
        r2 = lax.rsqrt(jnp.mean(x2 * x2, axis=-1, keepdims=True) + EPS)
        xn2 = x2 * r2
        g2 = g2_ref[...]
        err = xn2 * g2 - t_ref[...]
        sm1_ref[1:2, :] += jnp.sum(err * err, axis=0, keepdims=True) * (0.5 / D_MODEL)

        dy = err * (1.0 / D_MODEL)
        sm1_ref[0:1, :] += jnp.sum(dy * xn2, axis=0, keepdims=True)
        dxn = dy * g2
        dx2 = r2 * (dxn - xn2 * jnp.mean(dxn * xn2, axis=-1, keepdims=True))
        dx2_ref[...] = dx2
        dx2b = dx2.astype(BF16)
        dwo_acc[...] += _dot_tn(m, dx2b)
        dm = _dot_nt(dx2b, wo_ref[...])
        dmb = dm.astype(BF16)
        dya = dmb * sga
        dyc = dmb * sgv
        dg_ref[:, 2560:3584] = dmb * yab * (sga * (1.0 - sga))
        dg_ref[:, 3584:4608] = dmb * ycb * (sgv * (1.0 - sgv))
        dwao_acc[...] += _dot_tn(att_g, dya)
        dwco_acc[...] += _dot_tn(cg, dyc)
        datt_g = _dot_nt(dya, wao_ref[...])
        dcg = _dot_nt(dyc, wco_ref[...])
        datt_ref[...] = (datt_g * silu_a).astype(BF16)
        dg_ref[:, 0:512] = (datt_g * att * (sa + silu_a * (1.0 - sa))).astype(BF16)
        dg_ref[:, 512:1024] = (dcg * vconv * silu_c).astype(BF16)
        dg_ref[:, 2048:2560] = (dcg * gb * vconv * (sc + silu_c * (1.0 - sc))).astype(BF16)
        dv = dcg * gb * silu_c
        sm2_ref[3:4, :] += jnp.sum(dv, axis=0, keepdims=True)
        sm2_ref[0:1, :] += jnp.sum(dv * cu_m2, axis=0, keepdims=True)
        sm2_ref[1:2, :] += jnp.sum(dv * cu_m1, axis=0, keepdims=True)
        sm2_ref[2:3, :] += jnp.sum(dv * cu, axis=0, keepdims=True)
        nxt = carry[...]
        dv_p1 = jnp.where(row == ts - 1, nxt[0:1, :], pltpu.roll(dv, ts - 1, 0))
        dv_p2 = jnp.where(row == ts - 1, nxt[1:2, :],
                          jnp.where(row == ts - 2, nxt[0:1, :], pltpu.roll(dv, ts - 2, 0)))
        dcu = w2 * dv + w1 * dv_p1 + w0 * dv_p2
        carry[...] = dv[0:8, :]
        dg_ref[:, 1024:1536] = (dcu * u).astype(BF16)
        dg_ref[:, 1536:2048] = (dcu * gc).astype(BF16)

        @pl.when(i == nt - 1)
        def _():
            dwo_ref[...] = dwo_acc[...].astype(BF16)
            for j in range(N_DEV):
                dproj_ref[j, :, 0:128] = dwao_acc[:, 128 * j:128 * (j + 1)].astype(BF16)
                dproj_ref[j, :, 128:256] = dwco_acc[:, 128 * j:128 * (j + 1)].astype(BF16)

    tile = lambda w: pl.BlockSpec((ts, w), lambda i: (nt - 1 - i, 0))
    seg = lambda c: pl.BlockSpec((ts, 1536), lambda i: (nt - 1 - i, c))
    halo = lambda c: pl.BlockSpec((hb, 1536), lambda i: (jnp.maximum((nt - 1 - i) * (ts // hb) - 1, 0), c))
    full = lambda a: pl.BlockSpec(a.shape, lambda i: (0,) * a.ndim)
    acc = lambda r, c: pl.BlockSpec((r, c), lambda i: (0, 0))
    return pl.pallas_call(
        body, name="token_local", grid=(nt,),
        out_shape=[jax.ShapeDtypeStruct((S, D_MODEL), F32), jax.ShapeDtypeStruct((S, IN_COLS), BF16),
                   jax.ShapeDtypeStruct((S, D_ATT), BF16), jax.ShapeDtypeStruct((D_MODEL, D_MODEL), BF16),
                   jax.ShapeDtypeStruct(proj_g.shape, BF16),
                   jax.ShapeDtypeStruct((8, D_MODEL), F32), jax.ShapeDtypeStruct((8, D_CONV), F32)],
        in_specs=[tile(D_MODEL), tile(D_MODEL), seg(1), seg(2), seg(3), halo(1), halo(2), tile(D_ATT),
                  full(proj_g), full(w_out), full(cw_g), full(conv_b), full(final_g)],
        out_specs=[tile(D_MODEL), tile(GATE_COLS), tile(D_ATT), acc(D_MODEL, D_MODEL), full(proj_g),
                   acc(8, D_MODEL), acc(8, D_CONV)],
        scratch_shapes=[pltpu.VMEM((8, D_CONV), F32),
                        pltpu.VMEM((D_ATT, D_MODEL), BF16), pltpu.VMEM((D_CONV, D_MODEL), BF16),
                        pltpu.VMEM((8, D_CONV), F32), pltpu.VMEM((D_MODEL, D_MODEL), F32),
                        pltpu.VMEM((D_ATT, D_MODEL), F32), pltpu.VMEM((D_CONV, D_MODEL), F32)],
        compiler_params=_params(1),
    )(x, tgt, P, P, P, P, P, att, proj_g, w_out, cw_g, conv_b, final_g)


def _fold_diagonals(d_ref, o_ref):
    wide = D_MODEL
    sub = lax.broadcasted_iota(jnp.int32, (8, 1), 0)
    col = lax.broadcasted_iota(jnp.int32, (1, wide), 1)
    pad = jnp.zeros((8, wide - KB), F32)
    for h in range(HEADS):
        acc = jnp.concatenate([d_ref[h, 0:8, :], pad], axis=1)
        for qh in range(1, QB // 8):
            a = jnp.concatenate([d_ref[h, 8 * qh:8 * qh + 8, :], pad], axis=1)
            acc = acc + pltpu.roll(a, wide - 8 * qh, 1)
        for r in range(1, 8):
            acc = jnp.where(sub == r, pltpu.roll(acc, wide - r, 1), acc)
        vec = jnp.sum(acc, axis=0, keepdims=True)
        far = (col <= PADK - MAX_REL) | (col > KB)
        tail = jnp.sum(jnp.where(far, vec, 0.0), axis=-1, keepdims=True)
        o_ref[h:h + 1, :] = jnp.where(col == wide - 1, tail, vec)


def _attn_bwd(P, att, datt, lse, bias_tab, dP):
    S = P.shape[0]
    nb = S // QB

    def body(q_ref, att_ref, datt_ref, lse_ref, p_hbm, bias_ref, dp_hbm, out_ref, dbias_ref,
             kp, vp, dq_ring, dk_ring, dv_ring, db_ref, sem):
        g = pl.program_id(0)

        _load_keys(g, nb, p_hbm, kp, vp, sem)

        @pl.when(g == 0)
        def _():
            db_ref[...] = jnp.zeros_like(db_ref)
            dk_ring[...] = jnp.zeros_like(dk_ring)
            dv_ring[...] = jnp.zeros_like(dv_ring)

        s_new = g % 3
        s_mid = (g + 2) % 3
        s_old = (g + 1) % 3

        @pl.when(g < nb)
        def _():
            start = pl.multiple_of(g * QB, QB)
            lane = lax.broadcasted_iota(jnp.int32, (1, 128), 1)
            for p in range(HEADS // 2):
                cols = slice(128 * p, 128 * (p + 1))
                qp = q_ref[:, cols] * SCALE
                op = att_ref[:, cols].astype(F32)
                dop = datt_ref[:, cols]
                kpair = kp[pl.ds(start, KB), cols]
                vpair = vp[pl.ds(start, KB), cols]
                dqs = []
                dk_acc = jnp.zeros((KB, 128), F32)
                dv_acc = jnp.zeros((KB, 128), F32)
                for e in range(2):
                    h = 2 * p + e
                    lm = (lane < 64) if e == 0 else (lane >= 64)
                    qm = jnp.where(lm, qp, jnp.zeros_like(qp))
                    dom = jnp.where(lm, dop, jnp.zeros_like(dop))
                    s = _dot_nt(qm, kpair) * LOG2E + bias_ref[0, h]
                    pr = jnp.exp2(s - lse_ref[:, h:h + 1])
                    dp = _dot_nt(dom, vpair)
                    delta = jnp.sum(dom.astype(F32) * op, axis=-1, keepdims=True)
                    ds = pr * (dp - delta)
                    db_ref[h] += ds
                    dsb = ds.astype(BF16)
                    prb = pr.astype(BF16)
                    dqs.append(_dot(dsb, kpair) * SCALE)
                    dk_acc = dk_acc + _dot_tn(dsb, qm)
                    dv_acc = dv_acc + _dot_tn(prb, dom)
                dq_ring[s_new, :, cols] = jnp.where(lane < 64, dqs[0], dqs[1])
                dk_ring[s_old, :, cols] += dk_acc[0:QB]
                dk_ring[s_mid, :, cols] += dk_acc[QB:2 * QB]
                dk_ring[s_new, :, cols] = dk_acc[2 * QB:3 * QB]
                dv_ring[s_old, :, cols] += dv_acc[0:QB]
                dv_ring[s_mid, :, cols] += dv_acc[QB:2 * QB]
                dv_ring[s_new, :, cols] = dv_acc[2 * QB:3 * QB]

        @pl.when(g >= 2)
        def _():
            out_ref[:, 0:D_ATT] = dq_ring[s_old].astype(BF16)
            out_ref[:, D_ATT:2 * D_ATT] = dk_ring[s_old].astype(BF16)
            out_ref[:, 2 * D_ATT:3 * D_ATT] = dv_ring[s_old].astype(BF16)

        @pl.when(g == nb + 1)
        def _():
            _fold_diagonals(db_ref, dbias_ref)

    qblk = lambda w: pl.BlockSpec((QB, w), lambda g: (jnp.minimum(g, nb - 1), 0))
    return pl.pallas_call(
        body, name="attn_bwd", grid=(nb + 2,),
        out_shape=[jax.ShapeDtypeStruct((S, IN_COLS), BF16), jax.ShapeDtypeStruct((HEADS, D_MODEL), F32)],
        in_specs=[qblk(D_ATT), qblk(D_ATT), qblk(D_ATT), qblk(HEADS), ANY,
                  pl.BlockSpec((1, HEADS, QB, KB), lambda g: (jnp.minimum(g, 2), 0, 0, 0)), ANY],
        out_specs=[pl.BlockSpec((QB, 3 * D_ATT), lambda g: (jnp.maximum(g - 2, 0), GATE_COLS // (3 * D_ATT))),
                   pl.BlockSpec((HEADS, D_MODEL), lambda g: (0, 0))],
        input_output_aliases={6: 0},
        scratch_shapes=[pltpu.VMEM((S + PADK, D_ATT), BF16), pltpu.VMEM((S + PADK, D_ATT), BF16),
                        pltpu.VMEM((3, QB, D_ATT), F32), pltpu.VMEM((3, QB, D_ATT), F32),
                        pltpu.VMEM((3, QB, D_ATT), F32), pltpu.VMEM((HEADS, QB, KB), F32),
                        pltpu.SemaphoreType.DMA((2, S // (KEY_GROUP * QB)))],
        compiler_params=_params(1),
    )(P, att, datt, lse, P, bias_tab, dP)


def _dp_block(j):
    return (j + GATE_COLS // W_BLK) % N_DEV


def _in_proj_bwd(x, norm_g, dx2, dP, w_in_g):
    S = x.shape[0]
    ts = 512

    def body(x_ref, g_ref, dx2_ref, dp_ref, w_ref, gx_ref, dn_ref):
        @pl.when(pl.program_id(0) == 0)
        def _():
            dn_ref[...] = jnp.zeros_like(dn_ref)

        dh = _dot_nt(dp_ref[...], w_ref[...])
        xf = x_ref[...]
        r = lax.rsqrt(jnp.mean(xf * xf, axis=-1, keepdims=True) + EPS)
        xn = xf * r
        dn_ref[0:1, :] += jnp.sum(dh * xn, axis=0, keepdims=True)
        dhg = dh * g_ref[...]
        gx_ref[...] = dx2_ref[...] + r * (dhg - xn * jnp.mean(dhg * xn, axis=-1, keepdims=True))

    tile = lambda w: pl.BlockSpec((ts, w), lambda i: (i, 0))
    return pl.pallas_call(
        body, name="in_proj_bwd", grid=(S // ts,),
        out_shape=[jax.ShapeDtypeStruct((S, D_MODEL), F32), jax.ShapeDtypeStruct((8, D_MODEL), F32)],
        in_specs=[tile(D_MODEL), pl.BlockSpec((1, D_MODEL), lambda i: (0, 0)), tile(D_MODEL),
                  tile(IN_COLS),
                  pl.BlockSpec((D_MODEL, IN_COLS), lambda i: (0, 0))],
        out_specs=[tile(D_MODEL), pl.BlockSpec((8, D_MODEL), lambda i: (0, 0))],
        compiler_params=_params(1),
    )(x, norm_g, dx2, dP, w_in_g)


SCATTER_MASKS = ((3, 4, 5, 2, 7, 6, 1, 0), (5, 2, 3, 4, 7, 6, 1, 0))


def _w_in_grad_scatter(ht, dP, d_proj, d_wo, pack, order):
    S = ht.shape[1]
    ts = min(S, 2048)
    nt = S // ts
    n_steps = 8

    def body(order_ref, ht_ref, d_ref, proj_hbm, wo_hbm, pack_hbm, g_ref, rproj, rwo, rpack,
             acc, stage, rsib, rici, d2d_send, d2d_recv, ici_send, ici_recv, small_send, small_recv, local_sems):
        k, i = pl.program_id(0), pl.program_id(1)
        x, y, c = _mesh_pos()
        my = _flat((x, y, c))
        sibling = (x, y, 1 - c)
        owners = [(x ^ (1 - c), y ^ c, c), (x ^ c, y ^ (1 - c), c), (1 - x, 1 - y, c)]
        peers = [sibling, (1 - x, y, c), (x, 1 - y, c), (1 - x, 1 - y, c),
                 (1 - x, y, 1 - c), (x, 1 - y, 1 - c), (1 - x, 1 - y, 1 - c)]
        small = ((proj_hbm, rproj, True), (wo_hbm, rwo, True), (pack_hbm, rpack, False))
        n_small = len(small)

        def small_copy(kk, a, receive=False):
            src, dst, per_peer = small[a]
            slot = _flat(peers[kk]) if receive else my
            return pltpu.make_async_remote_copy(
                src_ref=src.at[_flat(peers[kk])] if per_peer else src, dst_ref=dst.at[slot],
                send_sem=small_send.at[kk, a], recv_sem=small_recv.at[kk, a],
                device_id=peers[kk], device_id_type=MESH)

        def to_sibling(t):
            return pltpu.make_async_remote_copy(
                src_ref=stage.at[0], dst_ref=rsib.at[t % 2], send_sem=d2d_send.at[t], recv_sem=d2d_recv.at[t],
                device_id=sibling, device_id_type=MESH)

        def to_owner(t):
            return pltpu.make_async_remote_copy(
                src_ref=stage.at[1], dst_ref=rici.at[t], send_sem=ici_send.at[t], recv_sem=ici_recv.at[t],
                device_id=owners[t], device_id_type=MESH)

        own_small = [pltpu.make_async_copy(src.at[my] if per_peer else src, dst.at[my], local_sems.at[a])
                     for a, (src, dst, per_peer) in enumerate(small)]

        @pl.when((k == 0) & (i == 0))
        def _():
            for cp in own_small:
                cp.start()
            for kk in range(len(peers)):
                for a in range(n_small):
                    small_copy(kk, a).start()

        prod = _dot(ht_ref[...], d_ref[...])

        @pl.when(i == 0)
        def _():
            acc[...] = prod

        @pl.when(i > 0)
        def _():
            acc[...] += prod

        @pl.when(i == nt - 1)
        def _():
            for s in range(n_steps):
                @pl.when(k == s)
                def _():
                    t = s // 2
                    if s % 2 == 0:
                        if t >= 1:
                            to_sibling(t - 1).wait_send()
                        stage[0] = acc[...].astype(BF16)
                        to_sibling(t).start()
                    elif t < 3:
                        if t >= 1:
                            to_owner(t - 1).wait_send()
                        to_sibling(t).wait_recv()
                        stage[1] = (acc[...] + rsib[t % 2].astype(F32)).astype(BF16)
                        to_owner(t).start()
                    else:
                        to_sibling(t).wait_recv()
                        total = acc[...] + rsib[t % 2].astype(F32)
                        for j in range(3):
                            to_owner(j).wait_recv()
                            total = total + rici[j].astype(F32)
                        g_ref[...] = total
                        to_owner(2).wait_send()
                        to_sibling(3).wait_send()
                        for q in range(len(peers)):
                            for a in range(n_small):
                                small_copy(q, a).wait_send()
                                small_copy(q, a, receive=True).wait_recv()
                        for cp in own_small:
                            cp.wait()

    blk = (D_MODEL, W_BLK)
    grid_spec = pltpu.PrefetchScalarGridSpec(
        num_scalar_prefetch=1, grid=(n_steps, nt),
        in_specs=[pl.BlockSpec((D_MODEL, ts), lambda k, i, o: (0, i)),
                  pl.BlockSpec((ts, W_BLK), lambda k, i, o: (i, _dp_block(o[k]))),
                  ANY, ANY, ANY],
        out_specs=[pl.BlockSpec(blk, lambda k, i, o: (0, 0)), ANY, ANY, ANY],
        scratch_shapes=[pltpu.VMEM(blk, F32), pltpu.VMEM((2,) + blk, BF16),
                        pltpu.VMEM((2,) + blk, BF16), pltpu.VMEM((3,) + blk, BF16),
                        pltpu.SemaphoreType.DMA((4,)), pltpu.SemaphoreType.DMA((4,)),
                        pltpu.SemaphoreType.DMA((3,)), pltpu.SemaphoreType.DMA((3,)),
                        pltpu.SemaphoreType.DMA((7, 3)), pltpu.SemaphoreType.DMA((7, 3)),
                        pltpu.SemaphoreType.DMA((3,))])
    return pl.pallas_call(
        body, name="w_in_grad_scatter", grid_spec=grid_spec,
        out_shape=[jax.ShapeDtypeStruct(blk, F32),
                   jax.ShapeDtypeStruct(d_proj.shape, BF16), jax.ShapeDtypeStruct(d_wo.shape, BF16),
                   jax.ShapeDtypeStruct((N_DEV,) + pack.shape, F32)],
        compiler_params=_params(2),
    )(order, ht, dP, d_proj, d_wo, pack)


def _adamw(w, g, m, v):
    m = ADAM_B1 * m + (1.0 - ADAM_B1) * g
    v = ADAM_B2 * v + (1.0 - ADAM_B2) * (g * g)
    m_hat = m / (1.0 - ADAM_B1 ** ADAM_STEP)
    v_hat = v / (1.0 - ADAM_B2 ** ADAM_STEP)
    delta = -ADAM_LR * (m_hat / (jnp.sqrt(v_hat) + ADAM_EPS) + ADAM_WD * w)
    return delta, m, v


def _sum_adamw(parts, w, m, v, name):
    R, C = w.shape
    n = parts.shape[0]
    tr = min(R, 256)

    def body(p_ref, w_ref, m_ref, v_ref, g_ref, d_ref, nm_ref, nv_ref):
        g = p_ref[0].astype(F32)
        for s in range(1, n):
            g = g + p_ref[s].astype(F32)
        g_ref[...] = g
        d_ref[...], nm_ref[...], nv_ref[...] = _adamw(w_ref[...], g, m_ref[...], v_ref[...])

    tile = pl.BlockSpec((tr, C), lambda i: (i, 0))
    return pl.pallas_call(
        body, name=name, grid=(R // tr,),
        out_shape=[jax.ShapeDtypeStruct((R, C), F32)] * 4,
        in_specs=[pl.BlockSpec((n, tr, C), lambda i: (0, i, 0)), tile, tile, tile],
        out_specs=[tile] * 4,
        compiler_params=_params(1),
    )(parts, w, m, v)


def _adamw_mid(r_proj, r_wo, params):
    def body(rp_ref, rw_ref, *refs):
        ins, outs = refs[:9], refs[9:]

        def total(part):
            g = part(0).astype(F32)
            for s in range(1, N_DEV):
                g = g + part(s).astype(F32)
            return g

        grads = (total(lambda s: rp_ref[s, :, 0:128]), total(lambda s: rp_ref[s, :, 128:256]),
                 total(lambda s: rw_ref[s]))
        for n, g in enumerate(grads):
            w, m, v = (r[...] for r in ins[3 * n:3 * n + 3])
            outs[4 * n][...] = g
            outs[4 * n + 1][...], outs[4 * n + 2][...], outs[4 * n + 3][...] = _adamw(w, g, m, v)

    return pl.pallas_call(
        body, name="adamw_mid",
        out_shape=[jax.ShapeDtypeStruct(params[3 * n].shape, F32) for n in range(3) for _ in range(4)],
        compiler_params=pltpu.CompilerParams(vmem_limit_bytes=VMEM_LIMIT),
    )(r_proj, r_wo, *params)


def _adamw_small(r_pack, params):
    wide = 384

    def body(p_ref, *refs):
        ins, loss_ref, outs = refs[:15], refs[15], refs[16:]
        tot = p_ref[0]
        for s in range(1, N_DEV):
            tot = tot + p_ref[s]
        me = _flat(_mesh_pos())
        loss_ref[...] = jnp.sum(tot[2:3, :], axis=-1, keepdims=True)
        mine = pltpu.roll(tot[0:8, 0:D_CONV], (D_CONV - 64 * me) % D_CONV, 1)
        col = lax.broadcasted_iota(jnp.int32, (D_MODEL, wide), 0)
        idx = lax.broadcasted_iota(jnp.int32, (D_MODEL, wide), 1)
        near = (idx > MAX_REL - CHUNK) & (idx < 2 * MAX_REL) & (col == PADK + MAX_REL - idx)
        far = (idx == 2 * MAX_REL) & (col == D_MODEL - 1)
        perm = jnp.where(near | far, 1.0, 0.0).astype(F32)
        g_rel = jnp.dot(tot[8:16], perm, precision=lax.Precision.HIGHEST, preferred_element_type=F32)
        grads = (tot[0:1], tot[1:2], mine[3:6, 0:64], tot[6:7, 0:D_CONV], g_rel[:, 0:N_REL])
        for n, g in enumerate(grads):
            w, m, v = (r[...] for r in ins[3 * n:3 * n + 3])
            outs[4 * n][...] = g
            outs[4 * n + 1][...], outs[4 * n + 2][...], outs[4 * n + 3][...] = _adamw(w, g, m, v)

    return pl.pallas_call(
        body, name="adamw_small",
        out_shape=[jax.ShapeDtypeStruct((1, 1), F32)]
        + [jax.ShapeDtypeStruct(params[3 * n].shape, F32) for n in range(5) for _ in range(4)],
    )(r_pack, *params)


def _pad_row(a, width=D_MODEL):
    a = a.reshape(-1, a.shape[-1])
    return jnp.pad(a, ((0, 0), (0, width - a.shape[-1])))


def kernel(x, norm_g, w_in, rel_bias, w_att_out, conv_w, conv_b, w_conv_out, w_out, final_norm_g, loss_target, m_norm_g, m_w_in, m_rel_bias, m_w_att_out, m_conv_w, m_conv_b, m_w_conv_out, m_w_out, m_final_norm_g, v_norm_g, v_w_in, v_rel_bias, v_w_att_out, v_conv_w, v_conv_b, v_w_conv_out, v_w_out, v_final_norm_g):
    S = x.shape[1]
    x2d = x.reshape(S, D_MODEL)
    tgt = loss_target.reshape(S, D_MODEL)
    me = 4 * lax.axis_index("x") + 2 * lax.axis_index("y") + lax.axis_index("c")
    row = lambda a: a.reshape(1, D_MODEL)

    proj_sh = jnp.concatenate([w_att_out[0], w_conv_out[0]], axis=1).astype(BF16)
    cw_sh = jnp.pad(conv_w[0], ((0, 5), (0, 64)))
    P, ht, w_in_g, proj_g, w_out_g, cw_g = _gather_in_proj(
        x2d, norm_g, w_in[0].astype(BF16), [proj_sh, w_out[0].astype(BF16), cw_sh],
        me ^ _by_core(GATHER_MASKS))

    bias_tab = _bias_table(rel_bias[0])
    att, lse = _attn_fwd(P, bias_tab)
    dx2, dP, datt, d_wo, d_proj, sm1, sm2 = _token_local(
        x2d, tgt, P, att, proj_g, w_out_g.reshape(D_MODEL, D_MODEL), cw_g, conv_b, row(final_norm_g))
    dP, dbias = _attn_bwd(P, att, datt, lse, bias_tab, dP)
    grad_x, dnorm = _in_proj_bwd(x2d, norm_g, dx2, dP, w_in_g)

    pack = jnp.concatenate([dnorm[0:1], sm1[0:2], _pad_row(sm2[0:4]), jnp.zeros((1, D_MODEL), F32), dbias],
                           axis=0)
    g_win_sum, r_proj, r_wo, r_pack = _w_in_grad_scatter(
        ht, dP, d_proj, d_wo.reshape(N_DEV, 128, D_MODEL), pack, me ^ _by_core(SCATTER_MASKS))

    res = {"w_in": _sum_adamw(g_win_sum[None], w_in[0], m_w_in[0], v_w_in[0], "adamw_w_in")}
    mid = _adamw_mid(r_proj, r_wo, (w_att_out[0], m_w_att_out[0], v_w_att_out[0],
                                    w_conv_out[0], m_w_conv_out[0], v_w_conv_out[0],
                                    w_out[0], m_w_out[0], v_w_out[0]))
    for n, name in enumerate(("w_att_out", "w_conv_out", "w_out")):
        res[name] = mid[4 * n:4 * n + 4]
    small = _adamw_small(r_pack, (norm_g, m_norm_g, v_norm_g,
                                  row(final_norm_g), row(m_final_norm_g), row(v_final_norm_g),
                                  conv_w[0], m_conv_w[0], v_conv_w[0], conv_b, m_conv_b, v_conv_b,
                                  rel_bias[0], m_rel_bias[0], v_rel_bias[0]))
    loss = small[0].reshape(())
    for n, name in enumerate(("norm_g", "final_norm_g", "conv_w", "conv_b", "rel_bias")):
        res[name] = small[1 + 4 * n:5 + 4 * n]

    leading = {"norm_g": (1, D_MODEL), "final_norm_g": (D_MODEL,), "conv_b": (1, D_CONV)}
    outs = []
    for kind in range(4):
        for name in ("norm_g", "w_in", "rel_bias", "w_att_out", "conv_w", "conv_b", "w_conv_out", "w_out",
                     "final_norm_g"):
            a = res[name][kind]
            outs.append(a.reshape(leading[name]) if name in leading else a[None])
    return (loss, grad_x.reshape(1, S, D_MODEL), *outs)
```

```python
import functools

import numpy as np
import jax
import jax.numpy as jnp
from jax import lax
from jax.experimental import pallas as pl
from jax.experimental.pallas import tpu as pltpu

F32 = jnp.float32
BF16 = jnp.bfloat16

D_MODEL = 1024
CHUNK = 64
N_LEFT = 8
HEADS = 8
D_ATT = 512
D_CONV = 512
MAX_REL = 128
N_REL = 2 * MAX_REL + 1
IN_COLS = 6144
EPS = 1e-6
NEG_BIG = -1e30
N_DEV = 8
W_BLK = IN_COLS // N_DEV
QB = 4 * CHUNK
KB = QB + N_LEFT * CHUNK
PADK = N_LEFT * CHUNK
SCALE = 64 ** -0.5
LOG2E = 1.4426950408889634
GATE_COLS = IN_COLS - 3 * D_ATT

ADAM_LR = 0.001
ADAM_B1 = 0.9
ADAM_B2 = 0.999
ADAM_EPS = 1e-08
ADAM_WD = 0.01
ADAM_STEP = 10

VMEM_LIMIT = 56 * 1024 * 1024

MESH = pl.DeviceIdType.MESH
ANY = pl.BlockSpec(memory_space=pl.ANY)


def _params(n_grid, vmem_limit=VMEM_LIMIT):
    return pltpu.CompilerParams(dimension_semantics=("arbitrary",) * n_grid,
                                vmem_limit_bytes=vmem_limit)


def _dot(a, b):
    return jnp.dot(a, b, preferred_element_type=F32)


def _dot_nt(a, b):
    return lax.dot_general(a, b, (((1,), (1,)), ((), ())), preferred_element_type=F32)


def _dot_tn(a, b):
    return lax.dot_general(a, b, (((0,), (0,)), ((), ())), preferred_element_type=F32)


def _sigmoid(z):
    return 0.5 * jnp.tanh(0.5 * z) + 0.5


def _mesh_pos():
    return lax.axis_index("x"), lax.axis_index("y"), lax.axis_index("c")


def _flat(p):
    return 4 * p[0] + 2 * p[1] + p[2]


def _by_core(masks):
    m0, m1 = (jnp.array(m, jnp.int32) for m in masks)
    return jnp.where(lax.axis_index("c") == 0, m0, m1)


GATHER_MASKS = ((0, 1, 4, 3, 2, 5, 6, 7), (0, 1, 2, 5, 4, 3, 6, 7))


def _gather_in_proj(x, norm_g, w_sh, smalls, order):
    S = x.shape[0]
    ts = 1024
    nt = S // ts
    n_small = len(smalls)
    n_steps = N_DEV

    def body(order_ref, x_ref, g_ref, w_hbm, *rest):
        small_in = rest[:n_small]
        p_ref, ht_ref, wg_hbm = rest[n_small:n_small + 3]
        small_out = rest[n_small + 3:2 * n_small + 3]
        (wbuf, hbuf, own_sem, send_sems, recv_sems, out_sems,
         small_send, small_recv, small_local) = rest[2 * n_small + 3:]
        k, i = pl.program_id(0), pl.program_id(1)
        x_, y_, c_ = _mesh_pos()
        me, sibling = (x_, y_, c_), (x_, y_, 1 - c_)
        my = _flat(me)
        chips = [(x_ ^ (1 - c_), y_ ^ c_), (x_ ^ c_, y_ ^ (1 - c_)), (1 - x_, 1 - y_)]
        peers = [sibling] + [(*chip, c_) for chip in chips] + [(*chip, 1 - c_) for chip in chips]

        def wcopy(sem, block, to, from_input=False):
            dst = wbuf.at[_flat(block)]
            return pltpu.make_async_remote_copy(
                src_ref=w_hbm if from_input else dst, dst_ref=dst,
                send_sem=send_sems.at[sem], recv_sem=recv_sems.at[sem], device_id=to, device_id_type=MESH)

        def small_copy(q, a, receive=False):
            slot = _flat(peers[q]) if receive else my
            return pltpu.make_async_remote_copy(
                src_ref=small_in[a], dst_ref=small_out[a].at[slot],
                send_sem=small_send.at[q, a], recv_sem=small_recv.at[q, a],
                device_id=peers[q], device_id_type=MESH)

        def keep(step, block):
            col = pl.multiple_of(_dp_block(_flat(block)) * W_BLK, 128)
            return pltpu.make_async_copy(wbuf.at[_flat(block)], wg_hbm.at[:, pl.ds(col, W_BLK)], out_sems.at[step])

        own = pltpu.make_async_copy(w_hbm, wbuf.at[my], own_sem)
        small_own = [pltpu.make_async_copy(small_in[a], small_out[a].at[my], small_local.at[a])
                     for a in range(n_small)]
        passed_on = [(*chips[1], 1 - c_), (*chips[0], 1 - c_), (*chips[2], 1 - c_)]
        arrivals = [me, sibling]
        for j in range(3):
            arrivals += [(*chips[j], c_), passed_on[j]]

        @pl.when(i == 0)
        def _():
            for kk in range(n_steps):
                @pl.when(k == kk)
                def _():
                    j = kk // 2 - 1
                    if kk == 0:
                        own.start()
                        wcopy(0, me, sibling, True).start()
                        wcopy(1, me, (*chips[0], c_), True).start()
                        own.wait()
                    elif kk == 1:
                        wcopy(0, sibling, me).wait_recv()
                        wcopy(2, me, (*chips[1], c_), True).start()
                    elif kk % 2 == 0:
                        wcopy(1 + j, (*chips[j], c_), me).wait_recv()
                        wcopy(4 + j, (*chips[j], c_), sibling).start()
                        if kk == 2:
                            wcopy(3, me, (*chips[2], c_), True).start()
                    else:
                        wcopy(4 + j, passed_on[j], me).wait_recv()
                        if kk == 3:
                            for cp in small_own:
                                cp.start()
                            for q in range(len(peers)):
                                for a in range(n_small):
                                    small_copy(q, a).start()
                    keep(kk, arrivals[kk]).start()

        row0 = pl.multiple_of(i * ts, ts)

        @pl.when(k == 0)
        def _():
            xf = x_ref[...]
            r = lax.rsqrt(jnp.mean(xf * xf, axis=-1, keepdims=True) + EPS)
            hf = (xf * r) * g_ref[...]
            hbuf[pl.ds(row0, ts), :] = hf.astype(BF16)
            ht_ref[...] = hf.astype(BF16).T

        p_ref[...] = _dot(hbuf[pl.ds(row0, ts), :], wbuf[order_ref[k]]).astype(BF16)

        @pl.when((k == n_steps - 1) & (i == nt - 1))
        def _():
            wcopy(0, me, sibling, True).wait_send()
            for j, chip in enumerate(chips):
                wcopy(1 + j, me, (*chip, c_), True).wait_send()
                wcopy(4 + j, (*chip, c_), sibling).wait_send()
            for kk in range(n_steps):
                keep(kk, arrivals[kk]).wait()
            for cp in small_own:
                cp.wait()
            for q in range(len(peers)):
                for a in range(n_small):
                    small_copy(q, a).wait_send()
                    small_copy(q, a, receive=True).wait_recv()

    first_pass = lambda k, i: jnp.where(k == 0, i, nt - 1)
    grid_spec = pltpu.PrefetchScalarGridSpec(
        num_scalar_prefetch=1, grid=(n_steps, nt),
        in_specs=[pl.BlockSpec((ts, D_MODEL), lambda k, i, o: (first_pass(k, i), 0)),
                  pl.BlockSpec((1, D_MODEL), lambda k, i, o: (0, 0)), ANY] + [ANY] * n_small,
        out_specs=[pl.BlockSpec((ts, W_BLK), lambda k, i, o: (i, o[k])),
                   pl.BlockSpec((D_MODEL, ts), lambda k, i, o: (0, first_pass(k, i))), ANY] + [ANY] * n_small,
        scratch_shapes=[pltpu.VMEM((N_DEV, D_MODEL, W_BLK), BF16), pltpu.VMEM((S, D_MODEL), BF16),
                        pltpu.SemaphoreType.DMA, pltpu.SemaphoreType.DMA((7,)), pltpu.SemaphoreType.DMA((7,)),
                        pltpu.SemaphoreType.DMA((n_steps,)),
                        pltpu.SemaphoreType.DMA((7, n_small)), pltpu.SemaphoreType.DMA((7, n_small)),
                        pltpu.SemaphoreType.DMA((n_small,))])
    return pl.pallas_call(
        body, name="gather_in_proj", grid_spec=grid_spec,
        out_shape=[jax.ShapeDtypeStruct((S, IN_COLS), BF16), jax.ShapeDtypeStruct((D_MODEL, S), BF16),
                   jax.ShapeDtypeStruct((D_MODEL, IN_COLS), BF16)]
        + [jax.ShapeDtypeStruct((N_DEV,) + s.shape, s.dtype) for s in smalls],
        compiler_params=_params(2),
    )(order, x, norm_g, w_sh, *smalls)


def _bias_table(rel_bias):
    wide = 1024

    def body(r_ref, o_ref):
        h = pl.program_id(0)
        col = lax.broadcasted_iota(jnp.int32, (1, wide), 1)
        k_minus_q = jnp.where(col < KB, col, col - wide)
        idx = jnp.clip(PADK - k_minus_q, -MAX_REL, MAX_REL) + MAX_REL
        f = jnp.zeros((1, wide), F32)
        for r in range(MAX_REL - CHUNK + 1, N_REL):
            f = jnp.where(idx == r, r_ref[h, r], f)
        kcol = lax.broadcasted_iota(jnp.int32, (1, KB), 1)
        kc = kcol >> 6
        sub = lax.broadcasted_iota(jnp.int32, (8, 1), 0)
        f8 = jnp.broadcast_to(f * LOG2E, (8, wide))
        base = f8
        for r in range(1, 8):
            base = jnp.where(sub == r, pltpu.roll(f8, r, 1), base)
        for qh in range(QB // 8):
            rows = (pltpu.roll(base, 8 * qh, 1) if qh else base)[:, 0:KB]
            qc = (8 * qh) // CHUNK
            band = (kc >= qc) & (kc <= qc + N_LEFT)
            for t in range(3):
                o_ref[t, 0, 8 * qh:8 * qh + 8, :] = jnp.where(band & (kcol >= PADK - t * QB), rows, NEG_BIG)

    return pl.pallas_call(
        body, name="bias_table", grid=(HEADS,),
        out_shape=jax.ShapeDtypeStruct((3, HEADS, QB, KB), F32),
        in_specs=[pl.BlockSpec(memory_space=pltpu.SMEM)],
        out_specs=pl.BlockSpec((3, 1, QB, KB), lambda h: (0, h, 0, 0)),
        compiler_params=_params(1),
    )(rel_bias)


KEY_GROUP = 4


def _load_keys(g, nb, p_hbm, kp, vp, sem):
    rows = KEY_GROUP * QB
    n_groups = p_hbm.shape[0] // rows

    def copies(c):
        src = pl.ds(c * rows, rows)
        dst = pl.ds(PADK + c * rows, rows)
        return (pltpu.make_async_copy(p_hbm.at[src, D_ATT:2 * D_ATT], kp.at[dst, :], sem.at[0, c]),
                pltpu.make_async_copy(p_hbm.at[src, 2 * D_ATT:3 * D_ATT], vp.at[dst, :], sem.at[1, c]))

    @pl.when(g == 0)
    def _():
        kp[0:PADK, :] = jnp.zeros((PADK, D_ATT), BF16)
        vp[0:PADK, :] = jnp.zeros((PADK, D_ATT), BF16)
        for c in range(n_groups):
            for cp in copies(c):
                cp.start()

    @pl.when((g % KEY_GROUP == 0) & (g < nb))
    def _():
        for cp in copies(g // KEY_GROUP):
            cp.wait()


def _attn_fwd(P, bias_tab):
    S = P.shape[0]
    nb = S // QB

    def body(q_ref, p_hbm, bias_ref, o_ref, lse_ref, kp, vp, sem):
        g = pl.program_id(0)
        _load_keys(g, nb, p_hbm, kp, vp, sem)
        start = pl.multiple_of(g * QB, QB)
        lane = lax.broadcasted_iota(jnp.int32, (1, 128), 1)
        for p in range(HEADS // 2):
            cols = slice(128 * p, 128 * (p + 1))
            qp = q_ref[:, cols] * SCALE
            kpair = kp[pl.ds(start, KB), cols]
            vpair = vp[pl.ds(start, KB), cols]
            outs = []
            for e in range(2):
                h = 2 * p + e
                lm = (lane < 64) if e == 0 else (lane >= 64)
                qm = jnp.where(lm, qp, jnp.zeros_like(qp))
                s = (_dot_nt(qm, kpair) * LOG2E + bias_ref[0, h]).astype(BF16)
                mx = jnp.max(s, axis=-1, keepdims=True)
                ex = jnp.exp2(s - mx)
                o = _dot(ex, jnp.where(lm, vpair, jnp.ones_like(vpair)))
                sums = pltpu.roll(o, 64, 1)
                outs.append(o / sums)
                lse_ref[:, h:h + 1] = mx.astype(F32) + jnp.log2(sums[:, 0:1] if e == 0 else o[:, 0:1])
            o_ref[:, cols] = jnp.where(lane < 64, outs[0], outs[1]).astype(BF16)

    return pl.pallas_call(
        body, name="attn_fwd", grid=(nb,),
        out_shape=[jax.ShapeDtypeStruct((S, D_ATT), BF16), jax.ShapeDtypeStruct((S, HEADS), F32)],
        in_specs=[pl.BlockSpec((QB, D_ATT), lambda g: (g, 0)), ANY,
                  pl.BlockSpec((1, HEADS, QB, KB), lambda g: (jnp.minimum(g, 2), 0, 0, 0))],
        out_specs=[pl.BlockSpec((QB, D_ATT), lambda g: (g, 0)),
                   pl.BlockSpec((QB, HEADS), lambda g: (g, 0))],
        scratch_shapes=[pltpu.VMEM((S + PADK, D_ATT), BF16), pltpu.VMEM((S + PADK, D_ATT), BF16),
                        pltpu.SemaphoreType.DMA((2, S // (KEY_GROUP * QB)))],
        compiler_params=_params(1),
    )(P, P, bias_tab)


def _token_local(x, tgt, P, att, proj_g, w_out, cw_g, conv_b, final_g):
    S = x.shape[0]
    ts = 256
    nt = S // ts
    hb = 16

    def body(x_ref, t_ref, s1_ref, s2_ref, s3_ref, h1_ref, h2_ref, att_ref,
             pg_ref, wo_ref, cwg_ref, cb_ref, g2_ref,
             dx2_ref, dg_ref, datt_ref, dwo_ref, dproj_ref, sm1_ref, sm2_ref,
             carry, wao_ref, wco_ref, cw_ref, dwo_acc, dwao_acc, dwco_acc):
        i = pl.program_id(0)
        t = nt - 1 - i

        @pl.when(i == 0)
        def _():
            dwo_acc[...] = jnp.zeros_like(dwo_acc)
            dwao_acc[...] = jnp.zeros_like(dwao_acc)
            dwco_acc[...] = jnp.zeros_like(dwco_acc)
            lane = lax.broadcasted_iota(jnp.int32, (1, 128), 1)
            for j in range(N_DEV):
                wao_ref[:, 128 * j:128 * (j + 1)] = pg_ref[j, :, 0:128]
                wco_ref[:, 128 * j:128 * (j + 1)] = pg_ref[j, :, 128:256]
            for p in range(N_DEV // 2):
                cw_ref[:, 128 * p:128 * (p + 1)] = jnp.where(
                    lane < 64, cwg_ref[2 * p], pltpu.roll(cwg_ref[2 * p + 1], 64, 1))
            sm1_ref[...] = jnp.zeros_like(sm1_ref)
            sm2_ref[...] = jnp.zeros_like(sm2_ref)
            carry[...] = jnp.zeros_like(carry)

        za = s1_ref[:, 0:512]
        gb = s1_ref[:, 512:1024]
        gc = s1_ref[:, 1024:1536].astype(F32)
        u = s2_ref[:, 0:512].astype(F32)
        zc = s2_ref[:, 512:1024]
        ga = jnp.concatenate([s2_ref[:, 1024:1536], s3_ref[:, 0:512]], axis=1)
        gv = s3_ref[:, 512:1536]
        att = att_ref[...]
        row = lax.broadcasted_iota(jnp.int32, (ts, 1), 0)

        sa = _sigmoid(za)
        silu_a = za * sa
        att_g = att * silu_a
        y_att = _dot(att_g, wao_ref[...])

        cu = gc * u
        keep = jnp.where(t > 0, 1.0, 0.0).astype(F32)
        hcu = (h1_ref[:, 1024:1536].astype(F32) * h2_ref[:, 0:512].astype(F32)) * keep
        cu_m1 = jnp.where(row == 0, hcu[hb - 1:hb, :], pltpu.roll(cu, 1, 0))
        cu_m2 = jnp.where(row == 0, hcu[hb - 2:hb - 1, :],
                          jnp.where(row == 1, hcu[hb - 1:hb, :], pltpu.roll(cu, 2, 0)))
        w0, w1, w2 = cw_ref[0:1, :], cw_ref[1:2, :], cw_ref[2:3, :]
        vconv = w0 * cu_m2 + w1 * cu_m1 + w2 * cu + cb_ref[...]
        vcb = vconv.astype(BF16)
        sc = _sigmoid(zc)
        silu_c = zc * sc
        cg = gb * vcb * silu_c
        y_conv = _dot(cg, wco_ref[...])

        sga = _sigmoid(ga)
        sgv = _sigmoid(gv)
        yab, ycb = y_att.astype(BF16), y_conv.astype(BF16)
        m = sga * yab + sgv * ycb
        x2 = x_ref[...] + _dot(m, wo_ref[...])
        r2 = lax.rsqrt(jnp.mean(x2 * x2, axis=-1, keepdims=True) + EPS)
        xn2 = x2 * r2
        g2 = g2_ref[...]
        err = xn2 * g2 - t_ref[...]
        sm1_ref[1:2, :] += jnp.sum(err * err, axis=0, keepdims=True) * (0.5 / D_MODEL)

        dy = err * (1.0 / D_MODEL)
        sm1_ref[0:1, :] += jnp.sum(dy * xn2, axis=0, keepdims=True)
        dxn = dy * g2
        dx2 = r2 * (dxn - xn2 * jnp.mean(dxn * xn2, axis=-1, keepdims=True))
        dx2_ref[...] = dx2
        dx2b = dx2.astype(BF16)
        dwo_acc[...] += _dot_tn(m, dx2b)
        dm = _dot_nt(dx2b, wo_ref[...])
        dmb = dm.astype(BF16)
        dya = dmb * sga
        dyc = dmb * sgv
        dg_ref[:, 2560:3584] = dmb * yab * (sga * (1.0 - sga))
        dg_ref[:, 3584:4608] = dmb * ycb * (sgv * (1.0 - sgv))
        dwao_acc[...] += _dot_tn(att_g, dya)
        dwco_acc[...] += _dot_tn(cg, dyc)
        datt_g = _dot_nt(dya, wao_ref[...])
        dcg = _dot_nt(dyc, wco_ref[...])
        dagb, dcgb = datt_g.astype(BF16), dcg.astype(BF16)
        datt_ref[...] = dagb * silu_a
        dg_ref[:, 0:512] = dagb * att * (sa + silu_a * (1.0 - sa))
        dg_ref[:, 512:1024] = dcgb * vcb * silu_c
        dg_ref[:, 2048:2560] = dcgb * gb * vcb * (sc + silu_c * (1.0 - sc))
        dv = dcg * (gb * silu_c).astype(F32)
        sm2_ref[3:4, :] += jnp.sum(dv, axis=0, keepdims=True)
        sm2_ref[0:1, :] += jnp.sum(dv * cu_m2, axis=0, keepdims=True)
        sm2_ref[1:2, :] += jnp.sum(dv * cu_m1, axis=0, keepdims=True)
        sm2_ref[2:3, :] += jnp.sum(dv * cu, axis=0, keepdims=True)
        nxt = carry[...]
        dv_p1 = jnp.where(row == ts - 1, nxt[0:1, :], pltpu.roll(dv, ts - 1, 0))
        dv_p2 = jnp.where(row == ts - 1, nxt[1:2, :],
                          jnp.where(row == ts - 2, nxt[0:1, :], pltpu.roll(dv, ts - 2, 0)))
        dcu = w2 * dv + w1 * dv_p1 + w0 * dv_p2
        carry[...] = dv[0:8, :]
        dg_ref[:, 1024:1536] = (dcu * u).astype(BF16)
        dg_ref[:, 1536:2048] = (dcu * gc).astype(BF16)

        @pl.when(i == nt - 1)
        def _():
            dwo_ref[...] = dwo_acc[...].astype(BF16)
            for j in range(N_DEV):
                dproj_ref[j, :, 0:128] = dwao_acc[:, 128 * j:128 * (j + 1)].astype(BF16)
                dproj_ref[j, :, 128:256] = dwco_acc[:, 128 * j:128 * (j + 1)].astype(BF16)

    tile = lambda w: pl.BlockSpec((ts, w), lambda i: (nt - 1 - i, 0))
    seg = lambda c: pl.BlockSpec((ts, 1536), lambda i: (nt - 1 - i, c))
    halo = lambda c: pl.BlockSpec((hb, 1536), lambda i: (jnp.maximum((nt - 1 - i) * (ts // hb) - 1, 0), c))
    full = lambda a: pl.BlockSpec(a.shape, lambda i: (0,) * a.ndim)
    acc = lambda r, c: pl.BlockSpec((r, c), lambda i: (0, 0))
    return pl.pallas_call(
        body, name="token_local", grid=(nt,),
        out_shape=[jax.ShapeDtypeStruct((S, D_MODEL), F32), jax.ShapeDtypeStruct((S, IN_COLS), BF16),
                   jax.ShapeDtypeStruct((S, D_ATT), BF16), jax.ShapeDtypeStruct((D_MODEL, D_MODEL), BF16),
                   jax.ShapeDtypeStruct(proj_g.shape, BF16),
                   jax.ShapeDtypeStruct((8, D_MODEL), F32), jax.ShapeDtypeStruct((8, D_CONV), F32)],
        in_specs=[tile(D_MODEL), tile(D_MODEL), seg(1), seg(2), seg(3), halo(1), halo(2), tile(D_ATT),
                  full(proj_g), full(w_out), full(cw_g), full(conv_b), full(final_g)],
        out_specs=[tile(D_MODEL), tile(GATE_COLS), tile(D_ATT), acc(D_MODEL, D_MODEL), full(proj_g),
                   acc(8, D_MODEL), acc(8, D_CONV)],
        scratch_shapes=[pltpu.VMEM((8, D_CONV), F32),
                        pltpu.VMEM((D_ATT, D_MODEL), BF16), pltpu.VMEM((D_CONV, D_MODEL), BF16),
                        pltpu.VMEM((8, D_CONV), F32), pltpu.VMEM((D_MODEL, D_MODEL), F32),
                        pltpu.VMEM((D_ATT, D_MODEL), F32), pltpu.VMEM((D_CONV, D_MODEL), F32)],
        compiler_params=_params(1),
    )(x, tgt, P, P, P, P, P, att, proj_g, w_out, cw_g, conv_b, final_g)


def _fold_diagonals(d_ref, o_ref):
    wide = D_MODEL
    sub = lax.broadcasted_iota(jnp.int32, (8, 1), 0)
    col = lax.broadcasted_iota(jnp.int32, (1, wide), 1)
    pad = jnp.zeros((8, wide - KB), F32)
    for h in range(HEADS):
        acc = jnp.concatenate([d_ref[h, 0:8, :], pad], axis=1)
        for qh in range(1, QB // 8):
            a = jnp.concatenate([d_ref[h, 8 * qh:8 * qh + 8, :], pad], axis=1)
            acc = acc + pltpu.roll(a, wide - 8 * qh, 1)
        for r in range(1, 8):
            acc = jnp.where(sub == r, pltpu.roll(acc, wide - r, 1), acc)
        vec = jnp.sum(acc, axis=0, keepdims=True)
        far = (col <= PADK - MAX_REL) | (col > KB)
        tail = jnp.sum(jnp.where(far, vec, 0.0), axis=-1, keepdims=True)
        o_ref[h:h + 1, :] = jnp.where(col == wide - 1, tail, vec)


def _attn_bwd(P, att, datt, lse, bias_tab, dP):
    S = P.shape[0]
    nb = S // QB

    def body(q_ref, att_ref, datt_ref, lse_ref, p_hbm, bias_ref, dp_hbm, out_ref, dbias_ref,
             kp, vp, dq_ring, dk_ring, dv_ring, db_ref, sem):
        g = pl.program_id(0)

        _load_keys(g, nb, p_hbm, kp, vp, sem)

        @pl.when(g == 0)
        def _():
            db_ref[...] = jnp.zeros_like(db_ref)
            dk_ring[...] = jnp.zeros_like(dk_ring)
            dv_ring[...] = jnp.zeros_like(dv_ring)

        s_new = g % 3
        s_mid = (g + 2) % 3
        s_old = (g + 1) % 3

        @pl.when(g < nb)
        def _():
            start = pl.multiple_of(g * QB, QB)
            lane = lax.broadcasted_iota(jnp.int32, (1, 128), 1)
            for p in range(HEADS // 2):
                cols = slice(128 * p, 128 * (p + 1))
                qp = q_ref[:, cols] * SCALE
                op = att_ref[:, cols].astype(F32)
                dop = datt_ref[:, cols]
                kpair = kp[pl.ds(start, KB), cols]
                vpair = vp[pl.ds(start, KB), cols]
                dqs = []
                dk_acc = jnp.zeros((KB, 128), F32)
                dv_acc = jnp.zeros((KB, 128), F32)
                for e in range(2):
                    h = 2 * p + e
                    lm = (lane < 64) if e == 0 else (lane >= 64)
                    qm = jnp.where(lm, qp, jnp.zeros_like(qp))
                    dom = jnp.where(lm, dop, jnp.zeros_like(dop))
                    s = _dot_nt(qm, kpair) * LOG2E + bias_ref[0, h]
                    pr = jnp.exp2(s - lse_ref[:, h:h + 1])
                    dp = _dot_nt(dom, vpair)
                    delta = jnp.sum(dom.astype(F32) * op, axis=-1, keepdims=True)
                    ds = pr * (dp - delta)
                    db_ref[h] += ds
                    dsb = ds.astype(BF16)
                    prb = pr.astype(BF16)
                    dqs.append(_dot(dsb, kpair) * SCALE)
                    dk_acc = dk_acc + _dot_tn(dsb, qm)
                    dv_acc = dv_acc + _dot_tn(prb, dom)
                dq_ring[s_new, :, cols] = jnp.where(lane < 64, dqs[0], dqs[1])
                dk_ring[s_old, :, cols] += dk_acc[0:QB]
                dk_ring[s_mid, :, cols] += dk_acc[QB:2 * QB]
                dk_ring[s_new, :, cols] = dk_acc[2 * QB:3 * QB]
                dv_ring[s_old, :, cols] += dv_acc[0:QB]
                dv_ring[s_mid, :, cols] += dv_acc[QB:2 * QB]
                dv_ring[s_new, :, cols] = dv_acc[2 * QB:3 * QB]

        @pl.when(g >= 2)
        def _():
            out_ref[:, 0:D_ATT] = dq_ring[s_old].astype(BF16)
            out_ref[:, D_ATT:2 * D_ATT] = dk_ring[s_old].astype(BF16)
            out_ref[:, 2 * D_ATT:3 * D_ATT] = dv_ring[s_old].astype(BF16)

        @pl.when(g == nb + 1)
        def _():
            _fold_diagonals(db_ref, dbias_ref)

    qblk = lambda w: pl.BlockSpec((QB, w), lambda g: (jnp.minimum(g, nb - 1), 0))
    return pl.pallas_call(
        body, name="attn_bwd", grid=(nb + 2,),
        out_shape=[jax.ShapeDtypeStruct((S, IN_COLS), BF16), jax.ShapeDtypeStruct((HEADS, D_MODEL), F32)],
        in_specs=[qblk(D_ATT), qblk(D_ATT), qblk(D_ATT), qblk(HEADS), ANY,
                  pl.BlockSpec((1, HEADS, QB, KB), lambda g: (jnp.minimum(g, 2), 0, 0, 0)), ANY],
        out_specs=[pl.BlockSpec((QB, 3 * D_ATT), lambda g: (jnp.maximum(g - 2, 0), GATE_COLS // (3 * D_ATT))),
                   pl.BlockSpec((HEADS, D_MODEL), lambda g: (0, 0))],
        input_output_aliases={6: 0},
        scratch_shapes=[pltpu.VMEM((S + PADK, D_ATT), BF16), pltpu.VMEM((S + PADK, D_ATT), BF16),
                        pltpu.VMEM((3, QB, D_ATT), F32), pltpu.VMEM((3, QB, D_ATT), F32),
                        pltpu.VMEM((3, QB, D_ATT), F32), pltpu.VMEM((HEADS, QB, KB), F32),
                        pltpu.SemaphoreType.DMA((2, S // (KEY_GROUP * QB)))],
        compiler_params=_params(1),
    )(P, att, datt, lse, P, bias_tab, dP)


def _dp_block(j):
    return (j + GATE_COLS // W_BLK) % N_DEV


def _in_proj_bwd(x, norm_g, dx2, dP, w_in_g):
    S = x.shape[0]
    ts = 512

    def body(x_ref, g_ref, dx2_ref, dp_ref, w_ref, gx_ref, dn_ref):
        @pl.when(pl.program_id(0) == 0)
        def _():
            dn_ref[...] = jnp.zeros_like(dn_ref)

        dh = _dot_nt(dp_ref[...], w_ref[...])
        xf = x_ref[...]
        r = lax.rsqrt(jnp.mean(xf * xf, axis=-1, keepdims=True) + EPS)
        xn = xf * r
        dn_ref[0:1, :] += jnp.sum(dh * xn, axis=0, keepdims=True)
        dhg = dh * g_ref[...]
        gx_ref[...] = dx2_ref[...] + r * (dhg - xn * jnp.mean(dhg * xn, axis=-1, keepdims=True))

    tile = lambda w: pl.BlockSpec((ts, w), lambda i: (i, 0))
    return pl.pallas_call(
        body, name="in_proj_bwd", grid=(S // ts,),
        out_shape=[jax.ShapeDtypeStruct((S, D_MODEL), F32), jax.ShapeDtypeStruct((8, D_MODEL), F32)],
        in_specs=[tile(D_MODEL), pl.BlockSpec((1, D_MODEL), lambda i: (0, 0)), tile(D_MODEL),
                  tile(IN_COLS),
                  pl.BlockSpec((D_MODEL, IN_COLS), lambda i: (0, 0))],
        out_specs=[tile(D_MODEL), pl.BlockSpec((8, D_MODEL), lambda i: (0, 0))],
        compiler_params=_params(1),
    )(x, norm_g, dx2, dP, w_in_g)


SCATTER_MASKS = ((3, 4, 5, 2, 7, 6, 1, 0), (5, 2, 3, 4, 7, 6, 1, 0))


def _w_in_grad_scatter(ht, dP, d_proj, d_wo, pack, order):
    S = ht.shape[1]
    ts = min(S, 2048)
    nt = S // ts
    n_steps = 8

    def body(order_ref, ht_ref, d_ref, proj_hbm, wo_hbm, pack_hbm, g_ref, rproj, rwo, rpack,
             acc, stage, rsib, rici, d2d_send, d2d_recv, ici_send, ici_recv, small_send, small_recv, local_sems):
        k, i = pl.program_id(0), pl.program_id(1)
        x, y, c = _mesh_pos()
        my = _flat((x, y, c))
        sibling = (x, y, 1 - c)
        owners = [(x ^ (1 - c), y ^ c, c), (x ^ c, y ^ (1 - c), c), (1 - x, 1 - y, c)]
        peers = [sibling, (1 - x, y, c), (x, 1 - y, c), (1 - x, 1 - y, c),
                 (1 - x, y, 1 - c), (x, 1 - y, 1 - c), (1 - x, 1 - y, 1 - c)]
        small = ((proj_hbm, rproj, True), (wo_hbm, rwo, True), (pack_hbm, rpack, False))
        n_small = len(small)

        def small_copy(kk, a, receive=False):
            src, dst, per_peer = small[a]
            slot = _flat(peers[kk]) if receive else my
            return pltpu.make_async_remote_copy(
                src_ref=src.at[_flat(peers[kk])] if per_peer else src, dst_ref=dst.at[slot],
                send_sem=small_send.at[kk, a], recv_sem=small_recv.at[kk, a],
                device_id=peers[kk], device_id_type=MESH)

        def to_sibling(t):
            return pltpu.make_async_remote_copy(
                src_ref=stage.at[0], dst_ref=rsib.at[t % 2], send_sem=d2d_send.at[t], recv_sem=d2d_recv.at[t],
                device_id=sibling, device_id_type=MESH)

        def to_owner(t):
            return pltpu.make_async_remote_copy(
                src_ref=stage.at[1], dst_ref=rici.at[t], send_sem=ici_send.at[t], recv_sem=ici_recv.at[t],
                device_id=owners[t], device_id_type=MESH)

        own_small = [pltpu.make_async_copy(src.at[my] if per_peer else src, dst.at[my], local_sems.at[a])
                     for a, (src, dst, per_peer) in enumerate(small)]

        @pl.when((k == 0) & (i == 0))
        def _():
            for cp in own_small:
                cp.start()
            for kk in range(len(peers)):
                for a in range(n_small):
                    small_copy(kk, a).start()

        prod = _dot(ht_ref[...], d_ref[...])

        @pl.when(i == 0)
        def _():
            acc[...] = prod

        @pl.when(i > 0)
        def _():
            acc[...] += prod

        @pl.when(i == nt - 1)
        def _():
            for s in range(n_steps):
                @pl.when(k == s)
                def _():
                    t = s // 2
                    if s % 2 == 0:
                        if t >= 1:
                            to_sibling(t - 1).wait_send()
                        stage[0] = acc[...].astype(BF16)
                        to_sibling(t).start()
                    elif t < 3:
                        if t >= 1:
                            to_owner(t - 1).wait_send()
                        to_sibling(t).wait_recv()
                        stage[1] = (acc[...] + rsib[t % 2].astype(F32)).astype(BF16)
                        to_owner(t).start()
                    else:
                        to_sibling(t).wait_recv()
                        total = acc[...] + rsib[t % 2].astype(F32)
                        for j in range(3):
                            to_owner(j).wait_recv()
                            total = total + rici[j].astype(F32)
                        g_ref[...] = total
                        to_owner(2).wait_send()
                        to_sibling(3).wait_send()
                        for q in range(len(peers)):
                            for a in range(n_small):
                                small_copy(q, a).wait_send()
                                small_copy(q, a, receive=True).wait_recv()
                        for cp in own_small:
                            cp.wait()

    blk = (D_MODEL, W_BLK)
    grid_spec = pltpu.PrefetchScalarGridSpec(
        num_scalar_prefetch=1, grid=(n_steps, nt),
        in_specs=[pl.BlockSpec((D_MODEL, ts), lambda k, i, o: (0, i)),
                  pl.BlockSpec((ts, W_BLK), lambda k, i, o: (i, _dp_block(o[k]))),
                  ANY, ANY, ANY],
        out_specs=[pl.BlockSpec(blk, lambda k, i, o: (0, 0)), ANY, ANY, ANY],
        scratch_shapes=[pltpu.VMEM(blk, F32), pltpu.VMEM((2,) + blk, BF16),
                        pltpu.VMEM((2,) + blk, BF16), pltpu.VMEM((3,) + blk, BF16),
                        pltpu.SemaphoreType.DMA((4,)), pltpu.SemaphoreType.DMA((4,)),
                        pltpu.SemaphoreType.DMA((3,)), pltpu.SemaphoreType.DMA((3,)),
                        pltpu.SemaphoreType.DMA((7, 3)), pltpu.SemaphoreType.DMA((7, 3)),
                        pltpu.SemaphoreType.DMA((3,))])
    return pl.pallas_call(
        body, name="w_in_grad_scatter", grid_spec=grid_spec,
        out_shape=[jax.ShapeDtypeStruct(blk, F32),
                   jax.ShapeDtypeStruct(d_proj.shape, BF16), jax.ShapeDtypeStruct(d_wo.shape, BF16),
                   jax.ShapeDtypeStruct((N_DEV,) + pack.shape, F32)],
        compiler_params=_params(2),
    )(order, ht, dP, d_proj, d_wo, pack)


def _adamw(w, g, m, v):
    m = ADAM_B1 * m + (1.0 - ADAM_B1) * g
    v = ADAM_B2 * v + (1.0 - ADAM_B2) * (g * g)
    m_hat = m / (1.0 - ADAM_B1 ** ADAM_STEP)
    v_hat = v / (1.0 - ADAM_B2 ** ADAM_STEP)
    delta = -ADAM_LR * (m_hat / (jnp.sqrt(v_hat) + ADAM_EPS) + ADAM_WD * w)
    return delta, m, v


def _sum_adamw(parts, w, m, v, name):
    R, C = w.shape
    n = parts.shape[0]
    tr = min(R, 256)

    def body(p_ref, w_ref, m_ref, v_ref, g_ref, d_ref, nm_ref, nv_ref):
        g = p_ref[0].astype(F32)
        for s in range(1, n):
            g = g + p_ref[s].astype(F32)
        g_ref[...] = g
        d_ref[...], nm_ref[...], nv_ref[...] = _adamw(w_ref[...], g, m_ref[...], v_ref[...])

    tile = pl.BlockSpec((tr, C), lambda i: (i, 0))
    return pl.pallas_call(
        body, name=name, grid=(R // tr,),
        out_shape=[jax.ShapeDtypeStruct((R, C), F32)] * 4,
        in_specs=[pl.BlockSpec((n, tr, C), lambda i: (0, i, 0)), tile, tile, tile],
        out_specs=[tile] * 4,
        compiler_params=_params(1),
    )(parts, w, m, v)


def _adamw_mid(r_proj, r_wo, params):
    def body(rp_ref, rw_ref, *refs):
        ins, outs = refs[:9], refs[9:]

        def total(part):
            g = part(0).astype(F32)
            for s in range(1, N_DEV):
                g = g + part(s).astype(F32)
            return g

        grads = (total(lambda s: rp_ref[s, :, 0:128]), total(lambda s: rp_ref[s, :, 128:256]),
                 total(lambda s: rw_ref[s]))
        for n, g in enumerate(grads):
            w, m, v = (r[...] for r in ins[3 * n:3 * n + 3])
            outs[4 * n][...] = g
            outs[4 * n + 1][...], outs[4 * n + 2][...], outs[4 * n + 3][...] = _adamw(w, g, m, v)

    return pl.pallas_call(
        body, name="adamw_mid",
        out_shape=[jax.ShapeDtypeStruct(params[3 * n].shape, F32) for n in range(3) for _ in range(4)],
        compiler_params=pltpu.CompilerParams(vmem_limit_bytes=VMEM_LIMIT),
    )(r_proj, r_wo, *params)


def _adamw_small(r_pack, params):
    wide = 384

    def body(p_ref, *refs):
        ins, loss_ref, outs = refs[:15], refs[15], refs[16:]
        tot = p_ref[0]
        for s in range(1, N_DEV):
            tot = tot + p_ref[s]
        me = _flat(_mesh_pos())
        loss_ref[...] = jnp.sum(tot[2:3, :], axis=-1, keepdims=True)
        mine = pltpu.roll(tot[0:8, 0:D_CONV], (D_CONV - 64 * me) % D_CONV, 1)
        col = lax.broadcasted_iota(jnp.int32, (D_MODEL, wide), 0)
        idx = lax.broadcasted_iota(jnp.int32, (D_MODEL, wide), 1)
        near = (idx > MAX_REL - CHUNK) & (idx < 2 * MAX_REL) & (col == PADK + MAX_REL - idx)
        far = (idx == 2 * MAX_REL) & (col == D_MODEL - 1)
        perm = jnp.where(near | far, 1.0, 0.0).astype(F32)
        g_rel = jnp.dot(tot[8:16], perm, precision=lax.Precision.HIGHEST, preferred_element_type=F32)
        grads = (tot[0:1], tot[1:2], mine[3:6, 0:64], tot[6:7, 0:D_CONV], g_rel[:, 0:N_REL])
        for n, g in enumerate(grads):
            w, m, v = (r[...] for r in ins[3 * n:3 * n + 3])
            outs[4 * n][...] = g
            outs[4 * n + 1][...], outs[4 * n + 2][...], outs[4 * n + 3][...] = _adamw(w, g, m, v)

    return pl.pallas_call(
        body, name="adamw_small",
        out_shape=[jax.ShapeDtypeStruct((1, 1), F32)]
        + [jax.ShapeDtypeStruct(params[3 * n].shape, F32) for n in range(5) for _ in range(4)],
    )(r_pack, *params)


def _pad_row(a, width=D_MODEL):
    a = a.reshape(-1, a.shape[-1])
    return jnp.pad(a, ((0, 0), (0, width - a.shape[-1])))


def kernel(x, norm_g, w_in, rel_bias, w_att_out, conv_w, conv_b, w_conv_out, w_out, final_norm_g, loss_target, m_norm_g, m_w_in, m_rel_bias, m_w_att_out, m_conv_w, m_conv_b, m_w_conv_out, m_w_out, m_final_norm_g, v_norm_g, v_w_in, v_rel_bias, v_w_att_out, v_conv_w, v_conv_b, v_w_conv_out, v_w_out, v_final_norm_g):
    S = x.shape[1]
    x2d = x.reshape(S, D_MODEL)
    tgt = loss_target.reshape(S, D_MODEL)
    me = 4 * lax.axis_index("x") + 2 * lax.axis_index("y") + lax.axis_index("c")
    row = lambda a: a.reshape(1, D_MODEL)

    proj_sh = jnp.concatenate([w_att_out[0], w_conv_out[0]], axis=1).astype(BF16)
    cw_sh = jnp.pad(conv_w[0], ((0, 5), (0, 64)))
    P, ht, w_in_g, proj_g, w_out_g, cw_g = _gather_in_proj(
        x2d, norm_g, w_in[0].astype(BF16), [proj_sh, w_out[0].astype(BF16), cw_sh],
        me ^ _by_core(GATHER_MASKS))

    bias_tab = _bias_table(rel_bias[0])
    att, lse = _attn_fwd(P, bias_tab)
    dx2, dP, datt, d_wo, d_proj, sm1, sm2 = _token_local(
        x2d, tgt, P, att, proj_g, w_out_g.reshape(D_MODEL, D_MODEL), cw_g, conv_b, row(final_norm_g))
    dP, dbias = _attn_bwd(P, att, datt, lse, bias_tab, dP)
    grad_x, dnorm = _in_proj_bwd(x2d, norm_g, dx2, dP, w_in_g)

    pack = jnp.concatenate([dnorm[0:1], sm1[0:2], _pad_row(sm2[0:4]), jnp.zeros((1, D_MODEL), F32), dbias],
                           axis=0)
    g_win_sum, r_proj, r_wo, r_pack = _w_in_grad_scatter(
        ht, dP, d_proj, d_wo.reshape(N_DEV, 128, D_MODEL), pack, me ^ _by_core(SCATTER_MASKS))

    res = {"w_in": _sum_adamw(g_win_sum[None], w_in[0], m_w_in[0], v_w_in[0], "adamw_w_in")}
    mid = _adamw_mid(r_proj, r_wo, (w_att_out[0], m_w_att_out[0], v_w_att_out[0],
                                    w_conv_out[0], m_w_conv_out[0], v_w_conv_out[0],
                                    w_out[0], m_w_out[0], v_w_out[0]))
    for n, name in enumerate(("w_att_out", "w_conv_out", "w_out")):
        res[name] = mid[4 * n:4 * n + 4]
    small = _adamw_small(r_pack, (norm_g, m_norm_g, v_norm_g,
                                  row(final_norm_g), row(m_final_norm_g), row(v_final_norm_g),
                                  conv_w[0], m_conv_w[0], v_conv_w[0], conv_b, m_conv_b, v_conv_b,
                                  rel_bias[0], m_rel_bias[0], v_rel_bias[0]))
    loss = small[0].reshape(())
    for n, name in enumerate(("norm_g", "final_norm_g", "conv_w", "conv_b", "rel_bias")):
        res[name] = small[1 + 4 * n:5 + 4 * n]

    leading = {"norm_g": (1, D_MODEL), "final_norm_g": (D_MODEL,), "conv_b": (1, D_CONV)}
    outs = []
    for kind in range(4):
        for name in ("norm_g", "w_in", "rel_bias", "w_att_out", "conv_w", "conv_b", "w_conv_out", "w_out",
                     "final_norm_g"):
            a = res[name][kind]
            outs.append(a.reshape(leading[name]) if name in leading else a[None])
    return (loss, grad_x.reshape(1, S, D_MODEL), *outs)
```

```python
import functools

import numpy as np
import jax
import jax.numpy as jnp
from jax import lax
from jax.experimental import pallas as pl
from jax.experimental.pallas import tpu as pltpu

F32 = jnp.float32
BF16 = jnp.bfloat16

D_MODEL = 1024
CHUNK = 64
N_LEFT = 8
HEADS = 8
D_ATT = 512
D_CONV = 512
MAX_REL = 128
N_REL = 2 * MAX_REL + 1
IN_COLS = 6144
EPS = 1e-6
NEG_BIG = -1e30
N_DEV = 8
W_BLK = IN_COLS // N_DEV
QB = 4 * CHUNK
KB = QB + N_LEFT * CHUNK
PADK = N_LEFT * CHUNK
SCALE = 64 ** -0.5
LOG2E = 1.4426950408889634
GATE_COLS = IN_COLS - 3 * D_ATT

ADAM_LR = 0.001
ADAM_B1 = 0.9
ADAM_B2 = 0.999
ADAM_EPS = 1e-08
ADAM_WD = 0.01
ADAM_STEP = 10

VMEM_LIMIT = 56 * 1024 * 1024

MESH = pl.DeviceIdType.MESH
ANY = pl.BlockSpec(memory_space=pl.ANY)


def _params(n_grid, vmem_limit=VMEM_LIMIT):
    return pltpu.CompilerParams(dimension_semantics=("arbitrary",) * n_grid,
                                vmem_limit_bytes=vmem_limit)


def _dot(a, b):
    return jnp.dot(a, b, preferred_element_type=F32)


def _dot_nt(a, b):
    return lax.dot_general(a, b, (((1,), (1,)), ((), ())), preferred_element_type=F32)


def _dot_tn(a, b):
    return lax.dot_general(a, b, (((0,), (0,)), ((), ())), preferred_element_type=F32)


def _sigmoid(z):
    return 0.5 * jnp.tanh(0.5 * z) + 0.5


def _mesh_pos():
    return lax.axis_index("x"), lax.axis_index("y"), lax.axis_index("c")


def _flat(p):
    return 4 * p[0] + 2 * p[1] + p[2]


def _by_core(masks):
    m0, m1 = (jnp.array(m, jnp.int32) for m in masks)
    return jnp.where(lax.axis_index("c") == 0, m0, m1)


GATHER_MASKS = ((0, 1, 4, 3, 2, 5, 6, 7), (0, 1, 2, 5, 4, 3, 6, 7))


def _gather_in_proj(x, norm_g, w_sh, smalls, order):
    S = x.shape[0]
    ts = 1024
    nt = S // ts
    n_small = len(smalls)
    n_steps = N_DEV

    def body(order_ref, x_ref, g_ref, w_hbm, *rest):
        small_in = rest[:n_small]
        p_ref, ht_ref, wg_hbm = rest[n_small:n_small + 3]
        small_out = rest[n_small + 3:2 * n_small + 3]
        (wbuf, hbuf, own_sem, send_sems, recv_sems, out_sems,
         small_send, small_recv, small_local) = rest[2 * n_small + 3:]
        k, i = pl.program_id(0), pl.program_id(1)
        x_, y_, c_ = _mesh_pos()
        me, sibling = (x_, y_, c_), (x_, y_, 1 - c_)
        my = _flat(me)
        chips = [(x_ ^ (1 - c_), y_ ^ c_), (x_ ^ c_, y_ ^ (1 - c_)), (1 - x_, 1 - y_)]
        peers = [sibling] + [(*chip, c_) for chip in chips] + [(*chip, 1 - c_) for chip in chips]

        def wcopy(sem, block, to, from_input=False):
            dst = wbuf.at[_flat(block)]
            return pltpu.make_async_remote_copy(
                src_ref=w_hbm if from_input else dst, dst_ref=dst,
                send_sem=send_sems.at[sem], recv_sem=recv_sems.at[sem], device_id=to, device_id_type=MESH)

        def small_copy(q, a, receive=False):
            slot = _flat(peers[q]) if receive else my
            return pltpu.make_async_remote_copy(
                src_ref=small_in[a], dst_ref=small_out[a].at[slot],
                send_sem=small_send.at[q, a], recv_sem=small_recv.at[q, a],
                device_id=peers[q], device_id_type=MESH)

        def keep(step, block):
            col = pl.multiple_of(_dp_block(_flat(block)) * W_BLK, 128)
            return pltpu.make_async_copy(wbuf.at[_flat(block)], wg_hbm.at[:, pl.ds(col, W_BLK)], out_sems.at[step])

        own = pltpu.make_async_copy(w_hbm, wbuf.at[my], own_sem)
        small_own = [pltpu.make_async_copy(small_in[a], small_out[a].at[my], small_local.at[a])
                     for a in range(n_small)]
        passed_on = [(*chips[1], 1 - c_), (*chips[0], 1 - c_), (*chips[2], 1 - c_)]
        arrivals = [me, sibling]
        for j in range(3):
            arrivals += [(*chips[j], c_), passed_on[j]]

        @pl.when(i == 0)
        def _():
            for kk in range(n_steps):
                @pl.when(k == kk)
                def _():
                    j = kk // 2 - 1
                    if kk == 0:
                        own.start()
                        wcopy(0, me, sibling, True).start()
                        wcopy(1, me, (*chips[0], c_), True).start()
                        own.wait()
                    elif kk == 1:
                        wcopy(0, sibling, me).wait_recv()
                        wcopy(2, me, (*chips[1], c_), True).start()
                    elif kk % 2 == 0:
                        wcopy(1 + j, (*chips[j], c_), me).wait_recv()
                        wcopy(4 + j, (*chips[j], c_), sibling).start()
                        if kk == 2:
                            wcopy(3, me, (*chips[2], c_), True).start()
                    else:
                        wcopy(4 + j, passed_on[j], me).wait_recv()
                        if kk == 3:
                            for cp in small_own:
                                cp.start()
                            for q in range(len(peers)):
                                for a in range(n_small):
                                    small_copy(q, a).start()
                    keep(kk, arrivals[kk]).start()

        row0 = pl.multiple_of(i * ts, ts)

        @pl.when(k == 0)
        def _():
            xf = x_ref[...]
            r = lax.rsqrt(jnp.mean(xf * xf, axis=-1, keepdims=True) + EPS)
            hf = (xf * r) * g_ref[...]
            hbuf[pl.ds(row0, ts), :] = hf.astype(BF16)
            ht_ref[...] = hf.astype(BF16).T

        p_ref[...] = _dot(hbuf[pl.ds(row0, ts), :], wbuf[order_ref[k]]).astype(BF16)

        @pl.when((k == n_steps - 1) & (i == nt - 1))
        def _():
            wcopy(0, me, sibling, True).wait_send()
            for j, chip in enumerate(chips):
                wcopy(1 + j, me, (*chip, c_), True).wait_send()
                wcopy(4 + j, (*chip, c_), sibling).wait_send()
            for kk in range(n_steps):
                keep(kk, arrivals[kk]).wait()
            for cp in small_own:
                cp.wait()
            for q in range(len(peers)):
                for a in range(n_small):
                    small_copy(q, a).wait_send()
                    small_copy(q, a, receive=True).wait_recv()

    first_pass = lambda k, i: jnp.where(k == 0, i, nt - 1)
    grid_spec = pltpu.PrefetchScalarGridSpec(
        num_scalar_prefetch=1, grid=(n_steps, nt),
        in_specs=[pl.BlockSpec((ts, D_MODEL), lambda k, i, o: (first_pass(k, i), 0)),
                  pl.BlockSpec((1, D_MODEL), lambda k, i, o: (0, 0)), ANY] + [ANY] * n_small,
        out_specs=[pl.BlockSpec((ts, W_BLK), lambda k, i, o: (i, o[k])),
                   pl.BlockSpec((D_MODEL, ts), lambda k, i, o: (0, first_pass(k, i))), ANY] + [ANY] * n_small,
        scratch_shapes=[pltpu.VMEM((N_DEV, D_MODEL, W_BLK), BF16), pltpu.VMEM((S, D_MODEL), BF16),
                        pltpu.SemaphoreType.DMA, pltpu.SemaphoreType.DMA((7,)), pltpu.SemaphoreType.DMA((7,)),
                        pltpu.SemaphoreType.DMA((n_steps,)),
                        pltpu.SemaphoreType.DMA((7, n_small)), pltpu.SemaphoreType.DMA((7, n_small)),
                        pltpu.SemaphoreType.DMA((n_small,))])
    return pl.pallas_call(
        body, name="gather_in_proj", grid_spec=grid_spec,
        out_shape=[jax.ShapeDtypeStruct((S, IN_COLS), BF16), jax.ShapeDtypeStruct((D_MODEL, S), BF16),
                   jax.ShapeDtypeStruct((D_MODEL, IN_COLS), BF16)]
        + [jax.ShapeDtypeStruct((N_DEV,) + s.shape, s.dtype) for s in smalls],
        compiler_params=_params(2),
    )(order, x, norm_g, w_sh, *smalls)


def _bias_table(rel_bias):
    wide = 1024

    def body(r_ref, o_ref):
        h = pl.program_id(0)
        col = lax.broadcasted_iota(jnp.int32, (1, wide), 1)
        k_minus_q = jnp.where(col < KB, col, col - wide)
        idx = jnp.clip(PADK - k_minus_q, -MAX_REL, MAX_REL) + MAX_REL
        f = jnp.zeros((1, wide), F32)
        for r in range(MAX_REL - CHUNK + 1, N_REL):
            f = jnp.where(idx == r, r_ref[h, r], f)
        kcol = lax.broadcasted_iota(jnp.int32, (1, KB), 1)
        kc = kcol >> 6
        sub = lax.broadcasted_iota(jnp.int32, (8, 1), 0)
        f8 = jnp.broadcast_to(f * LOG2E, (8, wide))
        base = f8
        for r in range(1, 8):
            base = jnp.where(sub == r, pltpu.roll(f8, r, 1), base)
        for qh in range(QB // 8):
            rows = (pltpu.roll(base, 8 * qh, 1) if qh else base)[:, 0:KB]
            qc = (8 * qh) // CHUNK
            band = (kc >= qc) & (kc <= qc + N_LEFT)
            for t in range(3):
                o_ref[t, 0, 8 * qh:8 * qh + 8, :] = jnp.where(band & (kcol >= PADK - t * QB), rows, NEG_BIG)

    return pl.pallas_call(
        body, name="bias_table", grid=(HEADS,),
        out_shape=jax.ShapeDtypeStruct((3, HEADS, QB, KB), F32),
        in_specs=[pl.BlockSpec(memory_space=pltpu.SMEM)],
        out_specs=pl.BlockSpec((3, 1, QB, KB), lambda h: (0, h, 0, 0)),
        compiler_params=_params(1),
    )(rel_bias)


KEY_GROUP = 4


def _load_keys(g, nb, p_hbm, kp, vp, sem):
    rows = KEY_GROUP * QB
    n_groups = p_hbm.shape[0] // rows

    def copies(c):
        src = pl.ds(c * rows, rows)
        dst = pl.ds(PADK + c * rows, rows)
        return (pltpu.make_async_copy(p_hbm.at[src, D_ATT:2 * D_ATT], kp.at[dst, :], sem.at[0, c]),
                pltpu.make_async_copy(p_hbm.at[src, 2 * D_ATT:3 * D_ATT], vp.at[dst, :], sem.at[1, c]))

    @pl.when(g == 0)
    def _():
        kp[0:PADK, :] = jnp.zeros((PADK, D_ATT), BF16)
        vp[0:PADK, :] = jnp.zeros((PADK, D_ATT), BF16)
        for c in range(n_groups):
            for cp in copies(c):
                cp.start()

    @pl.when((g % KEY_GROUP == 0) & (g < nb))
    def _():
        for cp in copies(g // KEY_GROUP):
            cp.wait()


def _attn_fwd(P, bias_tab):
    S = P.shape[0]
    nb = S // QB

    def body(q_ref, p_hbm, bias_ref, o_ref, ex_ref, rinv_ref, kp, vp, sem):
        g = pl.program_id(0)
        _load_keys(g, nb, p_hbm, kp, vp, sem)
        start = pl.multiple_of(g * QB, QB)
        lane = lax.broadcasted_iota(jnp.int32, (1, 128), 1)
        for p in range(HEADS // 2):
            cols = slice(128 * p, 128 * (p + 1))
            qp = q_ref[:, cols] * SCALE
            kpair = kp[pl.ds(start, KB), cols]
            vpair = vp[pl.ds(start, KB), cols]
            outs = []
            for e in range(2):
                h = 2 * p + e
                lm = (lane < 64) if e == 0 else (lane >= 64)
                qm = jnp.where(lm, qp, jnp.zeros_like(qp))
                s = (_dot_nt(qm, kpair) * LOG2E + bias_ref[0, h]).astype(BF16)
                mx = jnp.max(s, axis=-1, keepdims=True)
                ex = jnp.exp2(s - mx)
                ex_ref[:, KB * h:KB * (h + 1)] = ex
                o = _dot(ex, jnp.where(lm, vpair, jnp.ones_like(vpair)))
                rinv = 1.0 / pltpu.roll(o, 64, 1)
                outs.append(o * rinv)
                rinv_ref[:, h:h + 1] = rinv[:, 0:1] if e == 0 else 1.0 / o[:, 0:1]
            o_ref[:, cols] = jnp.where(lane < 64, outs[0], outs[1]).astype(BF16)

    return pl.pallas_call(
        body, name="attn_fwd", grid=(nb,),
        out_shape=[jax.ShapeDtypeStruct((S, D_ATT), BF16), jax.ShapeDtypeStruct((S, HEADS * KB), BF16),
                   jax.ShapeDtypeStruct((S, HEADS), F32)],
        in_specs=[pl.BlockSpec((QB, D_ATT), lambda g: (g, 0)), ANY,
                  pl.BlockSpec((1, HEADS, QB, KB), lambda g: (jnp.minimum(g, 2), 0, 0, 0))],
        out_specs=[pl.BlockSpec((QB, D_ATT), lambda g: (g, 0)),
                   pl.BlockSpec((QB, HEADS * KB), lambda g: (g, 0)),
                   pl.BlockSpec((QB, HEADS), lambda g: (g, 0))],
        scratch_shapes=[pltpu.VMEM((S + PADK, D_ATT), BF16), pltpu.VMEM((S + PADK, D_ATT), BF16),
                        pltpu.SemaphoreType.DMA((2, S // (KEY_GROUP * QB)))],
        compiler_params=_params(1),
    )(P, P, bias_tab)


def _token_local(x, tgt, P, att, proj_g, w_out, cw_g, conv_b, final_g):
    S = x.shape[0]
    ts = 256
    nt = S // ts
    hb = 16

    def body(x_ref, t_ref, s1_ref, s2_ref, s3_ref, h1_ref, h2_ref, att_ref,
             pg_ref, wo_ref, cwg_ref, cb_ref, g2_ref,
             dx2_ref, dg_ref, datt_ref, dwo_ref, dproj_ref, sm1_ref, sm2_ref,
             carry, wao_ref, wco_ref, cw_ref, dwo_acc, dwao_acc, dwco_acc):
        i = pl.program_id(0)
        t = nt - 1 - i

        @pl.when(i == 0)
        def _():
            dwo_acc[...] = jnp.zeros_like(dwo_acc)
            dwao_acc[...] = jnp.zeros_like(dwao_acc)
            dwco_acc[...] = jnp.zeros_like(dwco_acc)
            lane = lax.broadcasted_iota(jnp.int32, (1, 128), 1)
            for j in range(N_DEV):
                wao_ref[:, 128 * j:128 * (j + 1)] = pg_ref[j, :, 0:128]
                wco_ref[:, 128 * j:128 * (j + 1)] = pg_ref[j, :, 128:256]
            for p in range(N_DEV // 2):
                cw_ref[:, 128 * p:128 * (p + 1)] = jnp.where(
                    lane < 64, cwg_ref[2 * p], pltpu.roll(cwg_ref[2 * p + 1], 64, 1))
            sm1_ref[...] = jnp.zeros_like(sm1_ref)
            sm2_ref[...] = jnp.zeros_like(sm2_ref)
            carry[...] = jnp.zeros_like(carry)

        za = s1_ref[:, 0:512]
        gb = s1_ref[:, 512:1024]
        gc = s1_ref[:, 1024:1536].astype(F32)
        u = s2_ref[:, 0:512].astype(F32)
        zc = s2_ref[:, 512:1024]
        ga = jnp.concatenate([s2_ref[:, 1024:1536], s3_ref[:, 0:512]], axis=1)
        gv = s3_ref[:, 512:1536]
        att = att_ref[...]
        row = lax.broadcasted_iota(jnp.int32, (ts, 1), 0)

        sa = _sigmoid(za)
        silu_a = za * sa
        att_g = att * silu_a
        y_att = _dot(att_g, wao_ref[...])

        cu = gc * u
        keep = jnp.where(t > 0, 1.0, 0.0).astype(F32)
        hcu = (h1_ref[:, 1024:1536].astype(F32) * h2_ref[:, 0:512].astype(F32)) * keep
        cu_m1 = jnp.where(row == 0, hcu[hb - 1:hb, :], pltpu.roll(cu, 1, 0))
        cu_m2 = jnp.where(row == 0, hcu[hb - 2:hb - 1, :],
                          jnp.where(row == 1, hcu[hb - 1:hb, :], pltpu.roll(cu, 2, 0)))
        w0, w1, w2 = cw_ref[0:1, :], cw_ref[1:2, :], cw_ref[2:3, :]
        vconv = w0 * cu_m2 + w1 * cu_m1 + w2 * cu + cb_ref[...]
        vcb = vconv.astype(BF16)
        sc = _sigmoid(zc)
        silu_c = zc * sc
        cg = gb * vcb * silu_c
        y_conv = _dot(cg, wco_ref[...])

        sga = _sigmoid(ga)
        sgv = _sigmoid(gv)
        yab, ycb = y_att.astype(BF16), y_conv.astype(BF16)
        m = sga * yab + sgv * ycb
        x2 = x_ref[...] + _dot(m, wo_ref[...])
        r2 = lax.rsqrt(jnp.mean(x2 * x2, axis=-1, keepdims=True) + EPS)
        xn2 = x2 * r2
        g2 = g2_ref[...]
        err = xn2 * g2 - t_ref[...]
        sm1_ref[1:2, :] += jnp.sum(err * err, axis=0, keepdims=True) * (0.5 / D_MODEL)

        dy = err * (1.0 / D_MODEL)
        sm1_ref[0:1, :] += jnp.sum(dy * xn2, axis=0, keepdims=True)
        dxn = dy * g2
        dx2 = r2 * (dxn - xn2 * jnp.mean(dxn * xn2, axis=-1, keepdims=True))
        dx2_ref[...] = dx2
        dx2b = dx2.astype(BF16)
        dwo_acc[...] += _dot_tn(m, dx2b)
        dm = _dot_nt(dx2b, wo_ref[...])
        dmb = dm.astype(BF16)
        dya = dmb * sga
        dyc = dmb * sgv
        dg_ref[:, 2560:3584] = dmb * yab * (sga * (1.0 - sga))
        dg_ref[:, 3584:4608] = dmb * ycb * (sgv * (1.0 - sgv))
        dwao_acc[...] += _dot_tn(att_g, dya)
        dwco_acc[...] += _dot_tn(cg, dyc)
        datt_g = _dot_nt(dya, wao_ref[...])
        dcg = _dot_nt(dyc, wco_ref[...])
        dagb, dcgb = datt_g.astype(BF16), dcg.astype(BF16)
        datt_ref[...] = dagb * silu_a
        dg_ref[:, 0:512] = dagb * att * (sa + silu_a * (1.0 - sa))
        dg_ref[:, 512:1024] = dcgb * vcb * silu_c
        dg_ref[:, 2048:2560] = dcgb * gb * vcb * (sc + silu_c * (1.0 - sc))
        dv = dcg * (gb * silu_c).astype(F32)
        sm2_ref[3:4, :] += jnp.sum(dv, axis=0, keepdims=True)
        sm2_ref[0:1, :] += jnp.sum(dv * cu_m2, axis=0, keepdims=True)
        sm2_ref[1:2, :] += jnp.sum(dv * cu_m1, axis=0, keepdims=True)
        sm2_ref[2:3, :] += jnp.sum(dv * cu, axis=0, keepdims=True)
        nxt = carry[...]
        dv_p1 = jnp.where(row == ts - 1, nxt[0:1, :], pltpu.roll(dv, ts - 1, 0))
        dv_p2 = jnp.where(row == ts - 1, nxt[1:2, :],
                          jnp.where(row == ts - 2, nxt[0:1, :], pltpu.roll(dv, ts - 2, 0)))
        dcu = w2 * dv + w1 * dv_p1 + w0 * dv_p2
        carry[...] = dv[0:8, :]
        dg_ref[:, 1024:1536] = (dcu * u).astype(BF16)
        dg_ref[:, 1536:2048] = (dcu * gc).astype(BF16)

        @pl.when(i == nt - 1)
        def _():
            dwo_ref[...] = dwo_acc[...].astype(BF16)
            for j in range(N_DEV):
                dproj_ref[j, :, 0:128] = dwao_acc[:, 128 * j:128 * (j + 1)].astype(BF16)
                dproj_ref[j, :, 128:256] = dwco_acc[:, 128 * j:128 * (j + 1)].astype(BF16)

    tile = lambda w: pl.BlockSpec((ts, w), lambda i: (nt - 1 - i, 0))
    seg = lambda c: pl.BlockSpec((ts, 1536), lambda i: (nt - 1 - i, c))
    halo = lambda c: pl.BlockSpec((hb, 1536), lambda i: (jnp.maximum((nt - 1 - i) * (ts // hb) - 1, 0), c))
    full = lambda a: pl.BlockSpec(a.shape, lambda i: (0,) * a.ndim)
    acc = lambda r, c: pl.BlockSpec((r, c), lambda i: (0, 0))
    return pl.pallas_call(
        body, name="token_local", grid=(nt,),
        out_shape=[jax.ShapeDtypeStruct((S, D_MODEL), F32), jax.ShapeDtypeStruct((S, IN_COLS), BF16),
                   jax.ShapeDtypeStruct((S, D_ATT), BF16), jax.ShapeDtypeStruct((D_MODEL, D_MODEL), BF16),
                   jax.ShapeDtypeStruct(proj_g.shape, BF16),
                   jax.ShapeDtypeStruct((8, D_MODEL), F32), jax.ShapeDtypeStruct((8, D_CONV), F32)],
        in_specs=[tile(D_MODEL), tile(D_MODEL), seg(1), seg(2), seg(3), halo(1), halo(2), tile(D_ATT),
                  full(proj_g), full(w_out), full(cw_g), full(conv_b), full(final_g)],
        out_specs=[tile(D_MODEL), tile(GATE_COLS), tile(D_ATT), acc(D_MODEL, D_MODEL), full(proj_g),
                   acc(8, D_MODEL), acc(8, D_CONV)],
        scratch_shapes=[pltpu.VMEM((8, D_CONV), F32),
                        pltpu.VMEM((D_ATT, D_MODEL), BF16), pltpu.VMEM((D_CONV, D_MODEL), BF16),
                        pltpu.VMEM((8, D_CONV), F32), pltpu.VMEM((D_MODEL, D_MODEL), F32),
                        pltpu.VMEM((D_ATT, D_MODEL), F32), pltpu.VMEM((D_CONV, D_MODEL), F32)],
        compiler_params=_params(1),
    )(x, tgt, P, P, P, P, P, att, proj_g, w_out, cw_g, conv_b, final_g)


def _fold_diagonals(d_ref, o_ref):
    wide = D_MODEL
    sub = lax.broadcasted_iota(jnp.int32, (8, 1), 0)
    col = lax.broadcasted_iota(jnp.int32, (1, wide), 1)
    pad = jnp.zeros((8, wide - KB), F32)
    for h in range(HEADS):
        acc = jnp.concatenate([d_ref[h, 0:8, :], pad], axis=1)
        for qh in range(1, QB // 8):
            a = jnp.concatenate([d_ref[h, 8 * qh:8 * qh + 8, :], pad], axis=1)
            acc = acc + pltpu.roll(a, wide - 8 * qh, 1)
        for r in range(1, 8):
            acc = jnp.where(sub == r, pltpu.roll(acc, wide - r, 1), acc)
        vec = jnp.sum(acc, axis=0, keepdims=True)
        far = (col <= PADK - MAX_REL) | (col > KB)
        tail = jnp.sum(jnp.where(far, vec, 0.0), axis=-1, keepdims=True)
        o_ref[h:h + 1, :] = jnp.where(col == wide - 1, tail, vec)


def _attn_bwd(P, att, datt, ex, rinv, dP):
    S = P.shape[0]
    nb = S // QB

    def body(q_ref, att_ref, datt_ref, ex_ref, rinv_ref, p_hbm, dp_hbm, out_ref, dbias_ref,
             kp, vp, dq_ring, dk_ring, dv_ring, db_ref, sem):
        g = pl.program_id(0)

        _load_keys(g, nb, p_hbm, kp, vp, sem)

        @pl.when(g == 0)
        def _():
            db_ref[...] = jnp.zeros_like(db_ref)
            dk_ring[...] = jnp.zeros_like(dk_ring)
            dv_ring[...] = jnp.zeros_like(dv_ring)

        s_new = g % 3
        s_mid = (g + 2) % 3
        s_old = (g + 1) % 3

        @pl.when(g < nb)
        def _():
            start = pl.multiple_of(g * QB, QB)
            lane = lax.broadcasted_iota(jnp.int32, (1, 128), 1)
            for p in range(HEADS // 2):
                cols = slice(128 * p, 128 * (p + 1))
                qp = q_ref[:, cols] * SCALE
                op = att_ref[:, cols].astype(F32)
                dop = datt_ref[:, cols]
                kpair = kp[pl.ds(start, KB), cols]
                vpair = vp[pl.ds(start, KB), cols]
                dqs = []
                dk_acc = jnp.zeros((KB, 128), F32)
                dv_acc = jnp.zeros((KB, 128), F32)
                for e in range(2):
                    h = 2 * p + e
                    lm = (lane < 64) if e == 0 else (lane >= 64)
                    qm = jnp.where(lm, qp, jnp.zeros_like(qp))
                    dom = jnp.where(lm, dop, jnp.zeros_like(dop))
                    exh = ex_ref[:, KB * h:KB * (h + 1)]
                    rinv = rinv_ref[:, h:h + 1]
                    domf = dom.astype(F32)
                    dp = _dot_nt(dom, vpair)
                    delta = jnp.sum(domf * op, axis=-1, keepdims=True)
                    dsb = exh * ((dp - delta) * rinv).astype(BF16)
                    db_ref[h] += dsb.astype(F32)
                    dqs.append(_dot(dsb, kpair) * SCALE)
                    dk_acc = dk_acc + _dot_tn(dsb, qm)
                    dv_acc = dv_acc + _dot_tn(exh, (domf * rinv).astype(BF16))
                dq_ring[s_new, :, cols] = jnp.where(lane < 64, dqs[0], dqs[1])
                dk_ring[s_old, :, cols] += dk_acc[0:QB]
                dk_ring[s_mid, :, cols] += dk_acc[QB:2 * QB]
                dk_ring[s_new, :, cols] = dk_acc[2 * QB:3 * QB]
                dv_ring[s_old, :, cols] += dv_acc[0:QB]
                dv_ring[s_mid, :, cols] += dv_acc[QB:2 * QB]
                dv_ring[s_new, :, cols] = dv_acc[2 * QB:3 * QB]

        @pl.when(g >= 2)
        def _():
            out_ref[:, 0:D_ATT] = dq_ring[s_old].astype(BF16)
            out_ref[:, D_ATT:2 * D_ATT] = dk_ring[s_old].astype(BF16)
            out_ref[:, 2 * D_ATT:3 * D_ATT] = dv_ring[s_old].astype(BF16)

        @pl.when(g == nb + 1)
        def _():
            _fold_diagonals(db_ref, dbias_ref)

    qblk = lambda w: pl.BlockSpec((QB, w), lambda g: (jnp.minimum(g, nb - 1), 0))
    return pl.pallas_call(
        body, name="attn_bwd", grid=(nb + 2,),
        out_shape=[jax.ShapeDtypeStruct((S, IN_COLS), BF16), jax.ShapeDtypeStruct((HEADS, D_MODEL), F32)],
        in_specs=[qblk(D_ATT), qblk(D_ATT), qblk(D_ATT), qblk(HEADS * KB), qblk(HEADS), ANY, ANY],
        out_specs=[pl.BlockSpec((QB, 3 * D_ATT), lambda g: (jnp.maximum(g - 2, 0), GATE_COLS // (3 * D_ATT))),
                   pl.BlockSpec((HEADS, D_MODEL), lambda g: (0, 0))],
        input_output_aliases={6: 0},
        scratch_shapes=[pltpu.VMEM((S + PADK, D_ATT), BF16), pltpu.VMEM((S + PADK, D_ATT), BF16),
                        pltpu.VMEM((3, QB, D_ATT), F32), pltpu.VMEM((3, QB, D_ATT), F32),
                        pltpu.VMEM((3, QB, D_ATT), F32), pltpu.VMEM((HEADS, QB, KB), F32),
                        pltpu.SemaphoreType.DMA((2, S // (KEY_GROUP * QB)))],
        compiler_params=_params(1),
    )(P, att, datt, ex, rinv, P, dP)


def _dp_block(j):
    return (j + GATE_COLS // W_BLK) % N_DEV


def _in_proj_bwd(x, norm_g, dx2, dP, w_in_g):
    S = x.shape[0]
    ts = 512

    def body(x_ref, g_ref, dx2_ref, dp_ref, w_ref, gx_ref, dn_ref):
        @pl.when(pl.program_id(0) == 0)
        def _():
            dn_ref[...] = jnp.zeros_like(dn_ref)

        dh = _dot_nt(dp_ref[...], w_ref[...])
        xf = x_ref[...]
        r = lax.rsqrt(jnp.mean(xf * xf, axis=-1, keepdims=True) + EPS)
        xn = xf * r
        dn_ref[0:1, :] += jnp.sum(dh * xn, axis=0, keepdims=True)
        dhg = dh * g_ref[...]
        gx_ref[...] = dx2_ref[...] + r * (dhg - xn * jnp.mean(dhg * xn, axis=-1, keepdims=True))

    tile = lambda w: pl.BlockSpec((ts, w), lambda i: (i, 0))
    return pl.pallas_call(
        body, name="in_proj_bwd", grid=(S // ts,),
        out_shape=[jax.ShapeDtypeStruct((S, D_MODEL), F32), jax.ShapeDtypeStruct((8, D_MODEL), F32)],
        in_specs=[tile(D_MODEL), pl.BlockSpec((1, D_MODEL), lambda i: (0, 0)), tile(D_MODEL),
                  tile(IN_COLS),
                  pl.BlockSpec((D_MODEL, IN_COLS), lambda i: (0, 0))],
        out_specs=[tile(D_MODEL), pl.BlockSpec((8, D_MODEL), lambda i: (0, 0))],
        compiler_params=_params(1),
    )(x, norm_g, dx2, dP, w_in_g)


SCATTER_MASKS = ((3, 4, 5, 2, 7, 6, 1, 0), (5, 2, 3, 4, 7, 6, 1, 0))


def _w_in_grad_scatter(ht, dP, d_proj, d_wo, pack, order):
    S = ht.shape[1]
    ts = min(S, 2048)
    nt = S // ts
    n_steps = 8

    def body(order_ref, ht_ref, d_ref, proj_hbm, wo_hbm, pack_hbm, g_ref, rproj, rwo, rpack,
             acc, stage, rsib, rici, d2d_send, d2d_recv, ici_send, ici_recv, small_send, small_recv, local_sems):
        k, i = pl.program_id(0), pl.program_id(1)
        x, y, c = _mesh_pos()
        my = _flat((x, y, c))
        sibling = (x, y, 1 - c)
        owners = [(x ^ (1 - c), y ^ c, c), (x ^ c, y ^ (1 - c), c), (1 - x, 1 - y, c)]
        peers = [sibling, (1 - x, y, c), (x, 1 - y, c), (1 - x, 1 - y, c),
                 (1 - x, y, 1 - c), (x, 1 - y, 1 - c), (1 - x, 1 - y, 1 - c)]
        small = ((proj_hbm, rproj, True), (wo_hbm, rwo, True), (pack_hbm, rpack, False))
        n_small = len(small)

        def small_copy(kk, a, receive=False):
            src, dst, per_peer = small[a]
            slot = _flat(peers[kk]) if receive else my
            return pltpu.make_async_remote_copy(
                src_ref=src.at[_flat(peers[kk])] if per_peer else src, dst_ref=dst.at[slot],
                send_sem=small_send.at[kk, a], recv_sem=small_recv.at[kk, a],
                device_id=peers[kk], device_id_type=MESH)

        def to_sibling(t):
            return pltpu.make_async_remote_copy(
                src_ref=stage.at[0], dst_ref=rsib.at[t % 2], send_sem=d2d_send.at[t], recv_sem=d2d_recv.at[t],
                device_id=sibling, device_id_type=MESH)

        def to_owner(t):
            return pltpu.make_async_remote_copy(
                src_ref=stage.at[1], dst_ref=rici.at[t], send_sem=ici_send.at[t], recv_sem=ici_recv.at[t],
                device_id=owners[t], device_id_type=MESH)

        own_small = [pltpu.make_async_copy(src.at[my] if per_peer else src, dst.at[my], local_sems.at[a])
                     for a, (src, dst, per_peer) in enumerate(small)]

        @pl.when((k == 0) & (i == 0))
        def _():
            for cp in own_small:
                cp.start()
            for kk in range(len(peers)):
                for a in range(n_small):
                    small_copy(kk, a).start()

        prod = _dot(ht_ref[...], d_ref[...])

        @pl.when(i == 0)
        def _():
            acc[...] = prod

        @pl.when(i > 0)
        def _():
            acc[...] += prod

        @pl.when(i == nt - 1)
        def _():
            for s in range(n_steps):
                @pl.when(k == s)
                def _():
                    t = s // 2
                    if s % 2 == 0:
                        if t >= 1:
                            to_sibling(t - 1).wait_send()
                        stage[0] = acc[...].astype(BF16)
                        to_sibling(t).start()
                    elif t < 3:
                        if t >= 1:
                            to_owner(t - 1).wait_send()
                        to_sibling(t).wait_recv()
                        stage[1] = (acc[...] + rsib[t % 2].astype(F32)).astype(BF16)
                        to_owner(t).start()
                    else:
                        to_sibling(t).wait_recv()
                        total = acc[...] + rsib[t % 2].astype(F32)
                        for j in range(3):
                            to_owner(j).wait_recv()
                            total = total + rici[j].astype(F32)
                        g_ref[...] = total
                        to_owner(2).wait_send()
                        to_sibling(3).wait_send()
                        for q in range(len(peers)):
                            for a in range(n_small):
                                small_copy(q, a).wait_send()
                                small_copy(q, a, receive=True).wait_recv()
                        for cp in own_small:
                            cp.wait()

    blk = (D_MODEL, W_BLK)
    grid_spec = pltpu.PrefetchScalarGridSpec(
        num_scalar_prefetch=1, grid=(n_steps, nt),
        in_specs=[pl.BlockSpec((D_MODEL, ts), lambda k, i, o: (0, i)),
                  pl.BlockSpec((ts, W_BLK), lambda k, i, o: (i, _dp_block(o[k]))),
                  ANY, ANY, ANY],
        out_specs=[pl.BlockSpec(blk, lambda k, i, o: (0, 0)), ANY, ANY, ANY],
        scratch_shapes=[pltpu.VMEM(blk, F32), pltpu.VMEM((2,) + blk, BF16),
                        pltpu.VMEM((2,) + blk, BF16), pltpu.VMEM((3,) + blk, BF16),
                        pltpu.SemaphoreType.DMA((4,)), pltpu.SemaphoreType.DMA((4,)),
                        pltpu.SemaphoreType.DMA((3,)), pltpu.SemaphoreType.DMA((3,)),
                        pltpu.SemaphoreType.DMA((7, 3)), pltpu.SemaphoreType.DMA((7, 3)),
                        pltpu.SemaphoreType.DMA((3,))])
    return pl.pallas_call(
        body, name="w_in_grad_scatter", grid_spec=grid_spec,
        out_shape=[jax.ShapeDtypeStruct(blk, F32),
                   jax.ShapeDtypeStruct(d_proj.shape, BF16), jax.ShapeDtypeStruct(d_wo.shape, BF16),
                   jax.ShapeDtypeStruct((N_DEV,) + pack.shape, F32)],
        compiler_params=_params(2),
    )(order, ht, dP, d_proj, d_wo, pack)


def _adamw(w, g, m, v):
    m = ADAM_B1 * m + (1.0 - ADAM_B1) * g
    v = ADAM_B2 * v + (1.0 - ADAM_B2) * (g * g)
    m_hat = m / (1.0 - ADAM_B1 ** ADAM_STEP)
    v_hat = v / (1.0 - ADAM_B2 ** ADAM_STEP)
    delta = -ADAM_LR * (m_hat / (jnp.sqrt(v_hat) + ADAM_EPS) + ADAM_WD * w)
    return delta, m, v


def _sum_adamw(parts, w, m, v, name):
    R, C = w.shape
    n = parts.shape[0]
    tr = min(R, 256)

    def body(p_ref, w_ref, m_ref, v_ref, g_ref, d_ref, nm_ref, nv_ref):
        g = p_ref[0].astype(F32)
        for s in range(1, n):
            g = g + p_ref[s].astype(F32)
        g_ref[...] = g
        d_ref[...], nm_ref[...], nv_ref[...] = _adamw(w_ref[...], g, m_ref[...], v_ref[...])

    tile = pl.BlockSpec((tr, C), lambda i: (i, 0))
    return pl.pallas_call(
        body, name=name, grid=(R // tr,),
        out_shape=[jax.ShapeDtypeStruct((R, C), F32)] * 4,
        in_specs=[pl.BlockSpec((n, tr, C), lambda i: (0, i, 0)), tile, tile, tile],
        out_specs=[tile] * 4,
        compiler_params=_params(1),
    )(parts, w, m, v)


def _adamw_mid(r_proj, r_wo, params):
    def body(rp_ref, rw_ref, *refs):
        ins, outs = refs[:9], refs[9:]

        def total(part):
            g = part(0).astype(F32)
            for s in range(1, N_DEV):
                g = g + part(s).astype(F32)
            return g

        grads = (total(lambda s: rp_ref[s, :, 0:128]), total(lambda s: rp_ref[s, :, 128:256]),
                 total(lambda s: rw_ref[s]))
        for n, g in enumerate(grads):
            w, m, v = (r[...] for r in ins[3 * n:3 * n + 3])
            outs[4 * n][...] = g
            outs[4 * n + 1][...], outs[4 * n + 2][...], outs[4 * n + 3][...] = _adamw(w, g, m, v)

    return pl.pallas_call(
        body, name="adamw_mid",
        out_shape=[jax.ShapeDtypeStruct(params[3 * n].shape, F32) for n in range(3) for _ in range(4)],
        compiler_params=pltpu.CompilerParams(vmem_limit_bytes=VMEM_LIMIT),
    )(r_proj, r_wo, *params)


def _adamw_small(r_pack, params):
    wide = 384

    def body(p_ref, *refs):
        ins, loss_ref, outs = refs[:15], refs[15], refs[16:]
        tot = p_ref[0]
        for s in range(1, N_DEV):
            tot = tot + p_ref[s]
        me = _flat(_mesh_pos())
        loss_ref[...] = jnp.sum(tot[2:3, :], axis=-1, keepdims=True)
        mine = pltpu.roll(tot[0:8, 0:D_CONV], (D_CONV - 64 * me) % D_CONV, 1)
        col = lax.broadcasted_iota(jnp.int32, (D_MODEL, wide), 0)
        idx = lax.broadcasted_iota(jnp.int32, (D_MODEL, wide), 1)
        near = (idx > MAX_REL - CHUNK) & (idx < 2 * MAX_REL) & (col == PADK + MAX_REL - idx)
        far = (idx == 2 * MAX_REL) & (col == D_MODEL - 1)
        perm = jnp.where(near | far, 1.0, 0.0).astype(F32)
        g_rel = jnp.dot(tot[8:16], perm, precision=lax.Precision.HIGHEST, preferred_element_type=F32)
        grads = (tot[0:1], tot[1:2], mine[3:6, 0:64], tot[6:7, 0:D_CONV], g_rel[:, 0:N_REL])
        for n, g in enumerate(grads):
            w, m, v = (r[...] for r in ins[3 * n:3 * n + 3])
            outs[4 * n][...] = g
            outs[4 * n + 1][...], outs[4 * n + 2][...], outs[4 * n + 3][...] = _adamw(w, g, m, v)

    return pl.pallas_call(
        body, name="adamw_small",
        out_shape=[jax.ShapeDtypeStruct((1, 1), F32)]
        + [jax.ShapeDtypeStruct(params[3 * n].shape, F32) for n in range(5) for _ in range(4)],
    )(r_pack, *params)


def _pad_row(a, width=D_MODEL):
    a = a.reshape(-1, a.shape[-1])
    return jnp.pad(a, ((0, 0), (0, width - a.shape[-1])))


def kernel(x, norm_g, w_in, rel_bias, w_att_out, conv_w, conv_b, w_conv_out, w_out, final_norm_g, loss_target, m_norm_g, m_w_in, m_rel_bias, m_w_att_out, m_conv_w, m_conv_b, m_w_conv_out, m_w_out, m_final_norm_g, v_norm_g, v_w_in, v_rel_bias, v_w_att_out, v_conv_w, v_conv_b, v_w_conv_out, v_w_out, v_final_norm_g):
    S = x.shape[1]
    x2d = x.reshape(S, D_MODEL)
    tgt = loss_target.reshape(S, D_MODEL)
    me = 4 * lax.axis_index("x") + 2 * lax.axis_index("y") + lax.axis_index("c")
    row = lambda a: a.reshape(1, D_MODEL)

    proj_sh = jnp.concatenate([w_att_out[0], w_conv_out[0]], axis=1).astype(BF16)
    cw_sh = jnp.pad(conv_w[0], ((0, 5), (0, 64)))
    P, ht, w_in_g, proj_g, w_out_g, cw_g = _gather_in_proj(
        x2d, norm_g, w_in[0].astype(BF16), [proj_sh, w_out[0].astype(BF16), cw_sh],
        me ^ _by_core(GATHER_MASKS))

    bias_tab = _bias_table(rel_bias[0])
    att, ex, rinv = _attn_fwd(P, bias_tab)
    dx2, dP, datt, d_wo, d_proj, sm1, sm2 = _token_local(
        x2d, tgt, P, att, proj_g, w_out_g.reshape(D_MODEL, D_MODEL), cw_g, conv_b, row(final_norm_g))
    dP, dbias = _attn_bwd(P, att, datt, ex, rinv, dP)
    grad_x, dnorm = _in_proj_bwd(x2d, norm_g, dx2, dP, w_in_g)

    pack = jnp.concatenate([dnorm[0:1], sm1[0:2], _pad_row(sm2[0:4]), jnp.zeros((1, D_MODEL), F32), dbias],
                           axis=0)
    g_win_sum, r_proj, r_wo, r_pack = _w_in_grad_scatter(
        ht, dP, d_proj, d_wo.reshape(N_DEV, 128, D_MODEL), pack, me ^ _by_core(SCATTER_MASKS))

    res = {"w_in": _sum_adamw(g_win_sum[None], w_in[0], m_w_in[0], v_w_in[0], "adamw_w_in")}
    mid = _adamw_mid(r_proj, r_wo, (w_att_out[0], m_w_att_out[0], v_w_att_out[0],
                                    w_conv_out[0], m_w_conv_out[0], v_w_conv_out[0],
                                    w_out[0], m_w_out[0], v_w_out[0]))
    for n, name in enumerate(("w_att_out", "w_conv_out", "w_out")):
        res[name] = mid[4 * n:4 * n + 4]
    small = _adamw_small(r_pack, (norm_g, m_norm_g, v_norm_g,
                                  row(final_norm_g), row(m_final_norm_g), row(v_final_norm_g),
                                  conv_w[0], m_conv_w[0], v_conv_w[0], conv_b, m_conv_b, v_conv_b,
                                  rel_bias[0], m_rel_bias[0], v_rel_bias[0]))
    loss = small[0].reshape(())
    for n, name in enumerate(("norm_g", "final_norm_g", "conv_w", "conv_b", "rel_bias")):
        res[name] = small[1 + 4 * n:5 + 4 * n]

    leading = {"norm_g": (1, D_MODEL), "final_norm_g": (D_MODEL,), "conv_b": (1, D_CONV)}
    outs = []
    for kind in range(4):
        for name in ("norm_g", "w_in", "rel_bias", "w_att_out", "conv_w", "conv_b", "w_conv_out", "w_out",
                     "final_norm_g"):
            a = res[name][kind]
            outs.append(a.reshape(leading[name]) if name in leading else a[None])
    return (loss, grad_x.reshape(1, S, D_MODEL), *outs)
```

```python
import functools

import numpy as np
import jax
import jax.numpy as jnp
from jax import lax
from jax.experimental import pallas as pl
from jax.experimental.pallas import tpu as pltpu

F32 = jnp.float32
BF16 = jnp.bfloat16

D_MODEL = 1024
CHUNK = 64
N_LEFT = 8
HEADS = 8
D_ATT = 512
D_CONV = 512
MAX_REL = 128
N_REL = 2 * MAX_REL + 1
IN_COLS = 6144
EPS = 1e-6
NEG_BIG = -1e30
N_DEV = 8
W_BLK = IN_COLS // N_DEV
QB = 4 * CHUNK
KB = QB + N_LEFT * CHUNK
PADK = N_LEFT * CHUNK
SCALE = 64 ** -0.5
LOG2E = 1.4426950408889634
GATE_COLS = IN_COLS - 3 * D_ATT

ADAM_LR = 0.001
ADAM_B1 = 0.9
ADAM_B2 = 0.999
ADAM_EPS = 1e-08
ADAM_WD = 0.01
ADAM_STEP = 10

VMEM_LIMIT = 56 * 1024 * 1024

MESH = pl.DeviceIdType.MESH
ANY = pl.BlockSpec(memory_space=pl.ANY)


def _params(n_grid, vmem_limit=VMEM_LIMIT):
    return pltpu.CompilerParams(dimension_semantics=("arbitrary",) * n_grid,
                                vmem_limit_bytes=vmem_limit)


def _dot(a, b):
    return jnp.dot(a, b, preferred_element_type=F32)


def _dot_nt(a, b):
    return lax.dot_general(a, b, (((1,), (1,)), ((), ())), preferred_element_type=F32)


def _dot_tn(a, b):
    return lax.dot_general(a, b, (((0,), (0,)), ((), ())), preferred_element_type=F32)


def _sigmoid(z):
    return 0.5 * jnp.tanh(0.5 * z) + 0.5


def _mesh_pos():
    return lax.axis_index("x"), lax.axis_index("y"), lax.axis_index("c")


def _flat(p):
    return 4 * p[0] + 2 * p[1] + p[2]


def _by_core(masks):
    m0, m1 = (jnp.array(m, jnp.int32) for m in masks)
    return jnp.where(lax.axis_index("c") == 0, m0, m1)


GATHER_MASKS = ((0, 1, 4, 3, 2, 5, 6, 7), (0, 1, 2, 5, 4, 3, 6, 7))


def _gather_in_proj(x, norm_g, w_sh, smalls, order):
    S = x.shape[0]
    ts = 1024
    nt = S // ts
    n_small = len(smalls)
    n_steps = N_DEV

    def body(order_ref, x_ref, g_ref, w_hbm, *rest):
        small_in = rest[:n_small]
        p_ref, ht_ref, wg_hbm = rest[n_small:n_small + 3]
        small_out = rest[n_small + 3:2 * n_small + 3]
        (wbuf, hbuf, own_sem, send_sems, recv_sems, out_sems,
         small_send, small_recv, small_local) = rest[2 * n_small + 3:]
        k, i = pl.program_id(0), pl.program_id(1)
        x_, y_, c_ = _mesh_pos()
        me, sibling = (x_, y_, c_), (x_, y_, 1 - c_)
        my = _flat(me)
        chips = [(x_ ^ (1 - c_), y_ ^ c_), (x_ ^ c_, y_ ^ (1 - c_)), (1 - x_, 1 - y_)]
        peers = [sibling] + [(*chip, c_) for chip in chips] + [(*chip, 1 - c_) for chip in chips]

        def wcopy(sem, block, to, from_input=False):
            dst = wbuf.at[_flat(block)]
            return pltpu.make_async_remote_copy(
                src_ref=w_hbm if from_input else dst, dst_ref=dst,
                send_sem=send_sems.at[sem], recv_sem=recv_sems.at[sem], device_id=to, device_id_type=MESH)

        def small_copy(q, a, receive=False):
            slot = _flat(peers[q]) if receive else my
            return pltpu.make_async_remote_copy(
                src_ref=small_in[a], dst_ref=small_out[a].at[slot],
                send_sem=small_send.at[q, a], recv_sem=small_recv.at[q, a],
                device_id=peers[q], device_id_type=MESH)

        def keep(step, block):
            col = pl.multiple_of(_dp_block(_flat(block)) * W_BLK, 128)
            return pltpu.make_async_copy(wbuf.at[_flat(block)], wg_hbm.at[:, pl.ds(col, W_BLK)], out_sems.at[step])

        own = pltpu.make_async_copy(w_hbm, wbuf.at[my], own_sem)
        small_own = [pltpu.make_async_copy(small_in[a], small_out[a].at[my], small_local.at[a])
                     for a in range(n_small)]
        passed_on = [(*chips[1], 1 - c_), (*chips[0], 1 - c_), (*chips[2], 1 - c_)]
        arrivals = [me, sibling]
        for j in range(3):
            arrivals += [(*chips[j], c_), passed_on[j]]

        @pl.when(i == 0)
        def _():
            for kk in range(n_steps):
                @pl.when(k == kk)
                def _():
                    j = kk // 2 - 1
                    if kk == 0:
                        own.start()
                        wcopy(0, me, sibling, True).start()
                        wcopy(1, me, (*chips[0], c_), True).start()
                        own.wait()
                    elif kk == 1:
                        wcopy(0, sibling, me).wait_recv()
                        wcopy(2, me, (*chips[1], c_), True).start()
                    elif kk % 2 == 0:
                        wcopy(1 + j, (*chips[j], c_), me).wait_recv()
                        wcopy(4 + j, (*chips[j], c_), sibling).start()
                        if kk == 2:
                            wcopy(3, me, (*chips[2], c_), True).start()
                    else:
                        wcopy(4 + j, passed_on[j], me).wait_recv()
                        if kk == 3:
                            for cp in small_own:
                                cp.start()
                            for q in range(len(peers)):
                                for a in range(n_small):
                                    small_copy(q, a).start()
                    keep(kk, arrivals[kk]).start()

        row0 = pl.multiple_of(i * ts, ts)

        @pl.when(k == 0)
        def _():
            xf = x_ref[...]
            r = lax.rsqrt(jnp.mean(xf * xf, axis=-1, keepdims=True) + EPS)
            hf = (xf * r) * g_ref[...]
            hbuf[pl.ds(row0, ts), :] = hf.astype(BF16)
            ht_ref[...] = hf.astype(BF16).T

        p_ref[...] = _dot(hbuf[pl.ds(row0, ts), :], wbuf[order_ref[k]]).astype(BF16)

        @pl.when((k == n_steps - 1) & (i == nt - 1))
        def _():
            wcopy(0, me, sibling, True).wait_send()
            for j, chip in enumerate(chips):
                wcopy(1 + j, me, (*chip, c_), True).wait_send()
                wcopy(4 + j, (*chip, c_), sibling).wait_send()
            for kk in range(n_steps):
                keep(kk, arrivals[kk]).wait()
            for cp in small_own:
                cp.wait()
            for q in range(len(peers)):
                for a in range(n_small):
                    small_copy(q, a).wait_send()
                    small_copy(q, a, receive=True).wait_recv()

    first_pass = lambda k, i: jnp.where(k == 0, i, nt - 1)
    grid_spec = pltpu.PrefetchScalarGridSpec(
        num_scalar_prefetch=1, grid=(n_steps, nt),
        in_specs=[pl.BlockSpec((ts, D_MODEL), lambda k, i, o: (first_pass(k, i), 0)),
                  pl.BlockSpec((1, D_MODEL), lambda k, i, o: (0, 0)), ANY] + [ANY] * n_small,
        out_specs=[pl.BlockSpec((ts, W_BLK), lambda k, i, o: (i, o[k])),
                   pl.BlockSpec((D_MODEL, ts), lambda k, i, o: (0, first_pass(k, i))), ANY] + [ANY] * n_small,
        scratch_shapes=[pltpu.VMEM((N_DEV, D_MODEL, W_BLK), BF16), pltpu.VMEM((S, D_MODEL), BF16),
                        pltpu.SemaphoreType.DMA, pltpu.SemaphoreType.DMA((7,)), pltpu.SemaphoreType.DMA((7,)),
                        pltpu.SemaphoreType.DMA((n_steps,)),
                        pltpu.SemaphoreType.DMA((7, n_small)), pltpu.SemaphoreType.DMA((7, n_small)),
                        pltpu.SemaphoreType.DMA((n_small,))])
    return pl.pallas_call(
        body, name="gather_in_proj", grid_spec=grid_spec,
        out_shape=[jax.ShapeDtypeStruct((S, IN_COLS), BF16), jax.ShapeDtypeStruct((D_MODEL, S), BF16),
                   jax.ShapeDtypeStruct((D_MODEL, IN_COLS), BF16)]
        + [jax.ShapeDtypeStruct((N_DEV,) + s.shape, s.dtype) for s in smalls],
        compiler_params=_params(2),
    )(order, x, norm_g, w_sh, *smalls)


def _bias_table(rel_bias):
    wide = 1024

    def body(r_ref, o_ref):
        h = pl.program_id(0)
        col = lax.broadcasted_iota(jnp.int32, (1, wide), 1)
        k_minus_q = jnp.where(col < KB, col, col - wide)
        idx = jnp.clip(PADK - k_minus_q, -MAX_REL, MAX_REL) + MAX_REL
        f = jnp.zeros((1, wide), F32)
        for r in range(MAX_REL - CHUNK + 1, N_REL):
            f = jnp.where(idx == r, r_ref[h, r], f)
        kcol = lax.broadcasted_iota(jnp.int32, (1, KB), 1)
        kc = kcol >> 6
        sub = lax.broadcasted_iota(jnp.int32, (8, 1), 0)
        f8 = jnp.broadcast_to(f * LOG2E, (8, wide))
        base = f8
        for r in range(1, 8):
            base = jnp.where(sub == r, pltpu.roll(f8, r, 1), base)
        for qh in range(QB // 8):
            rows = (pltpu.roll(base, 8 * qh, 1) if qh else base)[:, 0:KB]
            qc = (8 * qh) // CHUNK
            band = (kc >= qc) & (kc <= qc + N_LEFT)
            for t in range(3):
                o_ref[t, 0, 8 * qh:8 * qh + 8, :] = jnp.where(band & (kcol >= PADK - t * QB), rows, NEG_BIG)

    return pl.pallas_call(
        body, name="bias_table", grid=(HEADS,),
        out_shape=jax.ShapeDtypeStruct((3, HEADS, QB, KB), F32),
        in_specs=[pl.BlockSpec(memory_space=pltpu.SMEM)],
        out_specs=pl.BlockSpec((3, 1, QB, KB), lambda h: (0, h, 0, 0)),
        compiler_params=_params(1),
    )(rel_bias)


KEY_GROUP = 4


def _load_keys(g, nb, p_hbm, kp, vp, sem):
    rows = KEY_GROUP * QB
    n_groups = p_hbm.shape[0] // rows

    def copies(c):
        src = pl.ds(c * rows, rows)
        dst = pl.ds(PADK + c * rows, rows)
        return (pltpu.make_async_copy(p_hbm.at[src, D_ATT:2 * D_ATT], kp.at[dst, :], sem.at[0, c]),
                pltpu.make_async_copy(p_hbm.at[src, 2 * D_ATT:3 * D_ATT], vp.at[dst, :], sem.at[1, c]))

    @pl.when(g == 0)
    def _():
        kp[0:PADK, :] = jnp.zeros((PADK, D_ATT), BF16)
        vp[0:PADK, :] = jnp.zeros((PADK, D_ATT), BF16)
        for c in range(n_groups):
            for cp in copies(c):
                cp.start()

    @pl.when((g % KEY_GROUP == 0) & (g < nb))
    def _():
        for cp in copies(g // KEY_GROUP):
            cp.wait()


def _attn_fwd(P, bias_tab):
    S = P.shape[0]
    nb = S // QB

    def body(q_ref, p_hbm, bias_ref, o_ref, ex_ref, rinv_ref, kp, vp, sem):
        g = pl.program_id(0)
        _load_keys(g, nb, p_hbm, kp, vp, sem)
        start = pl.multiple_of(g * QB, QB)
        lane = lax.broadcasted_iota(jnp.int32, (1, 128), 1)
        half = lambda h: (lane < 64) if h % 2 == 0 else (lane >= 64)
        pair = lambda h: slice(128 * (h // 2), 128 * (h // 2 + 1))

        def scores(h):
            qp = q_ref[:, pair(h)] * SCALE
            qm = jnp.where(half(h), qp, jnp.zeros_like(qp))
            return (_dot_nt(qm, kp[pl.ds(start, KB), pair(h)]) * LOG2E + bias_ref[0, h]).astype(BF16)

        outs = []
        s_next = scores(0)
        for h in range(HEADS):
            s = s_next
            if h + 1 < HEADS:
                s_next = scores(h + 1)
            mx = jnp.max(s, axis=-1, keepdims=True)
            ex = jnp.exp2(s - mx)
            ex_ref[:, KB * h:KB * (h + 1)] = ex
            vpair = vp[pl.ds(start, KB), pair(h)]
            o = _dot(ex, jnp.where(half(h), vpair, jnp.ones_like(vpair)))
            rinv = 1.0 / pltpu.roll(o, 64, 1)
            outs.append(o * rinv)
            rinv_ref[:, h:h + 1] = rinv[:, 0:1] if h % 2 == 0 else 1.0 / o[:, 0:1]
            if h % 2 == 1:
                o_ref[:, pair(h)] = jnp.where(lane < 64, outs[h - 1], outs[h]).astype(BF16)

    return pl.pallas_call(
        body, name="attn_fwd", grid=(nb,),
        out_shape=[jax.ShapeDtypeStruct((S, D_ATT), BF16), jax.ShapeDtypeStruct((S, HEADS * KB), BF16),
                   jax.ShapeDtypeStruct((S, HEADS), F32)],
        in_specs=[pl.BlockSpec((QB, D_ATT), lambda g: (g, 0)), ANY,
                  pl.BlockSpec((1, HEADS, QB, KB), lambda g: (jnp.minimum(g, 2), 0, 0, 0))],
        out_specs=[pl.BlockSpec((QB, D_ATT), lambda g: (g, 0)),
                   pl.BlockSpec((QB, HEADS * KB), lambda g: (g, 0)),
                   pl.BlockSpec((QB, HEADS), lambda g: (g, 0))],
        scratch_shapes=[pltpu.VMEM((S + PADK, D_ATT), BF16), pltpu.VMEM((S + PADK, D_ATT), BF16),
                        pltpu.SemaphoreType.DMA((2, S // (KEY_GROUP * QB)))],
        compiler_params=_params(1),
    )(P, P, bias_tab)


def _token_local(x, tgt, P, att, proj_g, w_out, cw_g, conv_b, final_g):
    S = x.shape[0]
    ts = 256
    nt = S // ts
    hb = 16

    def body(x_ref, t_ref, s1_ref, s2_ref, s3_ref, h1_ref, h2_ref, att_ref,
             pg_ref, wo_ref, cwg_ref, cb_ref, g2_ref,
             dx2_ref, dg_ref, datt_ref, dwo_ref, dproj_ref, sm1_ref, sm2_ref,
             carry, wao_ref, wco_ref, cw_ref, dwo_acc, dwao_acc, dwco_acc):
        i = pl.program_id(0)
        t = nt - 1 - i

        @pl.when(i == 0)
        def _():
            dwo_acc[...] = jnp.zeros_like(dwo_acc)
            dwao_acc[...] = jnp.zeros_like(dwao_acc)
            dwco_acc[...] = jnp.zeros_like(dwco_acc)
            lane = lax.broadcasted_iota(jnp.int32, (1, 128), 1)
            for j in range(N_DEV):
                wao_ref[:, 128 * j:128 * (j + 1)] = pg_ref[j, :, 0:128]
                wco_ref[:, 128 * j:128 * (j + 1)] = pg_ref[j, :, 128:256]
            for p in range(N_DEV // 2):
                cw_ref[:, 128 * p:128 * (p + 1)] = jnp.where(
                    lane < 64, cwg_ref[2 * p], pltpu.roll(cwg_ref[2 * p + 1], 64, 1))
            sm1_ref[...] = jnp.zeros_like(sm1_ref)
            sm2_ref[...] = jnp.zeros_like(sm2_ref)
            carry[...] = jnp.zeros_like(carry)

        za = s1_ref[:, 0:512]
        gb = s1_ref[:, 512:1024]
        gc = s1_ref[:, 1024:1536].astype(F32)
        u = s2_ref[:, 0:512].astype(F32)
        zc = s2_ref[:, 512:1024]
        ga = jnp.concatenate([s2_ref[:, 1024:1536], s3_ref[:, 0:512]], axis=1)
        gv = s3_ref[:, 512:1536]
        att = att_ref[...]
        row = lax.broadcasted_iota(jnp.int32, (ts, 1), 0)

        sa = _sigmoid(za)
        silu_a = za * sa
        att_g = att * silu_a
        y_att = _dot(att_g, wao_ref[...])

        cu = gc * u
        keep = jnp.where(t > 0, 1.0, 0.0).astype(F32)
        hcu = (h1_ref[:, 1024:1536].astype(F32) * h2_ref[:, 0:512].astype(F32)) * keep
        cu_m1 = jnp.where(row == 0, hcu[hb - 1:hb, :], pltpu.roll(cu, 1, 0))
        cu_m2 = jnp.where(row == 0, hcu[hb - 2:hb - 1, :],
                          jnp.where(row == 1, hcu[hb - 1:hb, :], pltpu.roll(cu, 2, 0)))
        w0, w1, w2 = cw_ref[0:1, :], cw_ref[1:2, :], cw_ref[2:3, :]
        vconv = w0 * cu_m2 + w1 * cu_m1 + w2 * cu + cb_ref[...]
        vcb = vconv.astype(BF16)
        sc = _sigmoid(zc)
        silu_c = zc * sc
        cg = gb * vcb * silu_c
        y_conv = _dot(cg, wco_ref[...])

        sga = _sigmoid(ga)
        sgv = _sigmoid(gv)
        yab, ycb = y_att.astype(BF16), y_conv.astype(BF16)
        m = sga * yab + sgv * ycb
        x2 = x_ref[...] + _dot(m, wo_ref[...])
        r2 = lax.rsqrt(jnp.mean(x2 * x2, axis=-1, keepdims=True) + EPS)
        xn2 = x2 * r2
        g2 = g2_ref[...]
        err = xn2 * g2 - t_ref[...]
        sm1_ref[1:2, :] += jnp.sum(err * err, axis=0, keepdims=True) * (0.5 / D_MODEL)

        dy = err * (1.0 / D_MODEL)
        sm1_ref[0:1, :] += jnp.sum(dy * xn2, axis=0, keepdims=True)
        dxn = dy * g2
        dx2 = r2 * (dxn - xn2 * jnp.mean(dxn * xn2, axis=-1, keepdims=True))
        dx2_ref[...] = dx2
        dx2b = dx2.astype(BF16)
        dm = _dot_nt(dx2b, wo_ref[...])
        dmb = dm.astype(BF16)
        dya = dmb * sga
        dyc = dmb * sgv
        dg_ref[:, 2560:3584] = dmb * yab * (sga * (1.0 - sga))
        dg_ref[:, 3584:4608] = dmb * ycb * (sgv * (1.0 - sgv))
        datt_g = _dot_nt(dya, wao_ref[...])
        dcg = _dot_nt(dyc, wco_ref[...])
        dagb, dcgb = datt_g.astype(BF16), dcg.astype(BF16)
        datt_ref[...] = dagb * silu_a
        dg_ref[:, 0:512] = dagb * att * (sa + silu_a * (1.0 - sa))
        dg_ref[:, 512:1024] = dcgb * vcb * silu_c
        dg_ref[:, 2048:2560] = dcgb * gb * vcb * (sc + silu_c * (1.0 - sc))
        dv = dcg * (gb * silu_c).astype(F32)
        sm2_ref[3:4, :] += jnp.sum(dv, axis=0, keepdims=True)
        sm2_ref[0:1, :] += jnp.sum(dv * cu_m2, axis=0, keepdims=True)
        sm2_ref[1:2, :] += jnp.sum(dv * cu_m1, axis=0, keepdims=True)
        sm2_ref[2:3, :] += jnp.sum(dv * cu, axis=0, keepdims=True)
        nxt = carry[...]
        dv_p1 = jnp.where(row == ts - 1, nxt[0:1, :], pltpu.roll(dv, ts - 1, 0))
        dv_p2 = jnp.where(row == ts - 1, nxt[1:2, :],
                          jnp.where(row == ts - 2, nxt[0:1, :], pltpu.roll(dv, ts - 2, 0)))
        dcu = w2 * dv + w1 * dv_p1 + w0 * dv_p2
        carry[...] = dv[0:8, :]
        dg_ref[:, 1024:1536] = (dcu * u).astype(BF16)
        dg_ref[:, 1536:2048] = (dcu * gc).astype(BF16)
        dwo_acc[...] += _dot_tn(m, dx2b)
        dwao_acc[...] += _dot_tn(att_g, dya)
        dwco_acc[...] += _dot_tn(cg, dyc)

        @pl.when(i == nt - 1)
        def _():
            dwo_ref[...] = dwo_acc[...].astype(BF16)
            for j in range(N_DEV):
                dproj_ref[j, :, 0:128] = dwao_acc[:, 128 * j:128 * (j + 1)].astype(BF16)
                dproj_ref[j, :, 128:256] = dwco_acc[:, 128 * j:128 * (j + 1)].astype(BF16)

    tile = lambda w: pl.BlockSpec((ts, w), lambda i: (nt - 1 - i, 0))
    seg = lambda c: pl.BlockSpec((ts, 1536), lambda i: (nt - 1 - i, c))
    halo = lambda c: pl.BlockSpec((hb, 1536), lambda i: (jnp.maximum((nt - 1 - i) * (ts // hb) - 1, 0), c))
    full = lambda a: pl.BlockSpec(a.shape, lambda i: (0,) * a.ndim)
    acc = lambda r, c: pl.BlockSpec((r, c), lambda i: (0, 0))
    return pl.pallas_call(
        body, name="token_local", grid=(nt,),
        out_shape=[jax.ShapeDtypeStruct((S, D_MODEL), F32), jax.ShapeDtypeStruct((S, IN_COLS), BF16),
                   jax.ShapeDtypeStruct((S, D_ATT), BF16), jax.ShapeDtypeStruct((D_MODEL, D_MODEL), BF16),
                   jax.ShapeDtypeStruct(proj_g.shape, BF16),
                   jax.ShapeDtypeStruct((8, D_MODEL), F32), jax.ShapeDtypeStruct((8, D_CONV), F32)],
        in_specs=[tile(D_MODEL), tile(D_MODEL), seg(1), seg(2), seg(3), halo(1), halo(2), tile(D_ATT),
                  full(proj_g), full(w_out), full(cw_g), full(conv_b), full(final_g)],
        out_specs=[tile(D_MODEL), tile(GATE_COLS), tile(D_ATT), acc(D_MODEL, D_MODEL), full(proj_g),
                   acc(8, D_MODEL), acc(8, D_CONV)],
        scratch_shapes=[pltpu.VMEM((8, D_CONV), F32),
                        pltpu.VMEM((D_ATT, D_MODEL), BF16), pltpu.VMEM((D_CONV, D_MODEL), BF16),
                        pltpu.VMEM((8, D_CONV), F32), pltpu.VMEM((D_MODEL, D_MODEL), F32),
                        pltpu.VMEM((D_ATT, D_MODEL), F32), pltpu.VMEM((D_CONV, D_MODEL), F32)],
        compiler_params=_params(1),
    )(x, tgt, P, P, P, P, P, att, proj_g, w_out, cw_g, conv_b, final_g)


def _fold_diagonals(d_ref, o_ref):
    wide = D_MODEL
    sub = lax.broadcasted_iota(jnp.int32, (8, 1), 0)
    col = lax.broadcasted_iota(jnp.int32, (1, wide), 1)
    pad = jnp.zeros((8, wide - KB), F32)
    for h in range(HEADS):
        acc = jnp.concatenate([d_ref[h, 0:8, :], pad], axis=1)
        for qh in range(1, QB // 8):
            a = jnp.concatenate([d_ref[h, 8 * qh:8 * qh + 8, :], pad], axis=1)
            acc = acc + pltpu.roll(a, wide - 8 * qh, 1)
        for r in range(1, 8):
            acc = jnp.where(sub == r, pltpu.roll(acc, wide - r, 1), acc)
        vec = jnp.sum(acc, axis=0, keepdims=True)
        far = (col <= PADK - MAX_REL) | (col > KB)
        tail = jnp.sum(jnp.where(far, vec, 0.0), axis=-1, keepdims=True)
        o_ref[h:h + 1, :] = jnp.where(col == wide - 1, tail, vec)


def _attn_bwd(P, att, datt, ex, rinv, dP):
    S = P.shape[0]
    nb = S // QB

    def body(q_ref, att_ref, datt_ref, ex_ref, rinv_ref, p_hbm, dp_hbm, out_ref, dbias_ref,
             kp, vp, dq_ring, dk_ring, dv_ring, db_ref, sem):
        g = pl.program_id(0)

        _load_keys(g, nb, p_hbm, kp, vp, sem)

        @pl.when(g == 0)
        def _():
            db_ref[...] = jnp.zeros_like(db_ref)
            dk_ring[...] = jnp.zeros_like(dk_ring)
            dv_ring[...] = jnp.zeros_like(dv_ring)

        s_new = g % 3
        s_mid = (g + 2) % 3
        s_old = (g + 1) % 3

        @pl.when(g < nb)
        def _():
            start = pl.multiple_of(g * QB, QB)
            lane = lax.broadcasted_iota(jnp.int32, (1, 128), 1)
            half = lambda h: (lane < 64) if h % 2 == 0 else (lane >= 64)
            pair = lambda h: slice(128 * (h // 2), 128 * (h // 2 + 1))

            def d_probs(h):
                dop = datt_ref[:, pair(h)]
                dom = jnp.where(half(h), dop, jnp.zeros_like(dop))
                return dom, _dot_nt(dom, vp[pl.ds(start, KB), pair(h)])

            nxt = d_probs(0)
            for h in range(HEADS):
                cols = pair(h)
                dom, dp = nxt
                if h + 1 < HEADS:
                    nxt = d_probs(h + 1)
                qp = q_ref[:, cols] * SCALE
                qm = jnp.where(half(h), qp, jnp.zeros_like(qp))
                exh = ex_ref[:, KB * h:KB * (h + 1)]
                rinv = rinv_ref[:, h:h + 1]
                domf = dom.astype(F32)
                delta = jnp.sum(domf * att_ref[:, cols].astype(F32), axis=-1, keepdims=True)
                dsb = exh * ((dp - delta) * rinv).astype(BF16)
                db_ref[h] += dsb.astype(F32)
                dq = _dot(dsb, kp[pl.ds(start, KB), cols]) * SCALE
                dk = _dot_tn(dsb, qm)
                dv = _dot_tn(exh, (domf * rinv).astype(BF16))
                if h % 2 == 0:
                    dq_even, dk_even, dv_even = dq, dk, dv
                else:
                    dk, dv = dk_even + dk, dv_even + dv
                    dq_ring[s_new, :, cols] = jnp.where(lane < 64, dq_even, dq)
                    dk_ring[s_old, :, cols] += dk[0:QB]
                    dk_ring[s_mid, :, cols] += dk[QB:2 * QB]
                    dk_ring[s_new, :, cols] = dk[2 * QB:3 * QB]
                    dv_ring[s_old, :, cols] += dv[0:QB]
                    dv_ring[s_mid, :, cols] += dv[QB:2 * QB]
                    dv_ring[s_new, :, cols] = dv[2 * QB:3 * QB]

        @pl.when(g >= 2)
        def _():
            out_ref[:, 0:D_ATT] = dq_ring[s_old].astype(BF16)
            out_ref[:, D_ATT:2 * D_ATT] = dk_ring[s_old].astype(BF16)
            out_ref[:, 2 * D_ATT:3 * D_ATT] = dv_ring[s_old].astype(BF16)

        @pl.when(g == nb + 1)
        def _():
            _fold_diagonals(db_ref, dbias_ref)

    qblk = lambda w: pl.BlockSpec((QB, w), lambda g: (jnp.minimum(g, nb - 1), 0))
    return pl.pallas_call(
        body, name="attn_bwd", grid=(nb + 2,),
        out_shape=[jax.ShapeDtypeStruct((S, IN_COLS), BF16), jax.ShapeDtypeStruct((HEADS, D_MODEL), F32)],
        in_specs=[qblk(D_ATT), qblk(D_ATT), qblk(D_ATT), qblk(HEADS * KB), qblk(HEADS), ANY, ANY],
        out_specs=[pl.BlockSpec((QB, 3 * D_ATT), lambda g: (jnp.maximum(g - 2, 0), GATE_COLS // (3 * D_ATT))),
                   pl.BlockSpec((HEADS, D_MODEL), lambda g: (0, 0))],
        input_output_aliases={6: 0},
        scratch_shapes=[pltpu.VMEM((S + PADK, D_ATT), BF16), pltpu.VMEM((S + PADK, D_ATT), BF16),
                        pltpu.VMEM((3, QB, D_ATT), F32), pltpu.VMEM((3, QB, D_ATT), F32),
                        pltpu.VMEM((3, QB, D_ATT), F32), pltpu.VMEM((HEADS, QB, KB), F32),
                        pltpu.SemaphoreType.DMA((2, S // (KEY_GROUP * QB)))],
        compiler_params=_params(1),
    )(P, att, datt, ex, rinv, P, dP)


def _dp_block(j):
    return (j + GATE_COLS // W_BLK) % N_DEV


def _in_proj_bwd(x, norm_g, dx2, dP, w_in_g):
    S = x.shape[0]
    ts = 512

    def body(x_ref, g_ref, dx2_ref, dp_ref, w_ref, gx_ref, dn_ref):
        @pl.when(pl.program_id(0) == 0)
        def _():
            dn_ref[...] = jnp.zeros_like(dn_ref)

        dh = _dot_nt(dp_ref[...], w_ref[...])
        xf = x_ref[...]
        r = lax.rsqrt(jnp.mean(xf * xf, axis=-1, keepdims=True) + EPS)
        xn = xf * r
        dn_ref[0:1, :] += jnp.sum(dh * xn, axis=0, keepdims=True)
        dhg = dh * g_ref[...]
        gx_ref[...] = dx2_ref[...] + r * (dhg - xn * jnp.mean(dhg * xn, axis=-1, keepdims=True))

    tile = lambda w: pl.BlockSpec((ts, w), lambda i: (i, 0))
    return pl.pallas_call(
        body, name="in_proj_bwd", grid=(S // ts,),
        out_shape=[jax.ShapeDtypeStruct((S, D_MODEL), F32), jax.ShapeDtypeStruct((8, D_MODEL), F32)],
        in_specs=[tile(D_MODEL), pl.BlockSpec((1, D_MODEL), lambda i: (0, 0)), tile(D_MODEL),
                  tile(IN_COLS),
                  pl.BlockSpec((D_MODEL, IN_COLS), lambda i: (0, 0))],
        out_specs=[tile(D_MODEL), pl.BlockSpec((8, D_MODEL), lambda i: (0, 0))],
        compiler_params=_params(1),
    )(x, norm_g, dx2, dP, w_in_g)


SCATTER_MASKS = ((3, 4, 5, 2, 7, 6, 1, 0), (5, 2, 3, 4, 7, 6, 1, 0))


def _w_in_grad_scatter(ht, dP, d_proj, d_wo, pack, order):
    S = ht.shape[1]
    ts = min(S, 2048)
    nt = S // ts
    n_steps = 8

    def body(order_ref, ht_ref, d_ref, proj_hbm, wo_hbm, pack_hbm, g_ref, rproj, rwo, rpack,
             acc, stage, rsib, rici, d2d_send, d2d_recv, ici_send, ici_recv, small_send, small_recv, local_sems):
        k, i = pl.program_id(0), pl.program_id(1)
        x, y, c = _mesh_pos()
        my = _flat((x, y, c))
        sibling = (x, y, 1 - c)
        owners = [(x ^ (1 - c), y ^ c, c), (x ^ c, y ^ (1 - c), c), (1 - x, 1 - y, c)]
        peers = [sibling, (1 - x, y, c), (x, 1 - y, c), (1 - x, 1 - y, c),
                 (1 - x, y, 1 - c), (x, 1 - y, 1 - c), (1 - x, 1 - y, 1 - c)]
        small = ((proj_hbm, rproj, True), (wo_hbm, rwo, True), (pack_hbm, rpack, False))
        n_small = len(small)

        def small_copy(kk, a, receive=False):
            src, dst, per_peer = small[a]
            slot = _flat(peers[kk]) if receive else my
            return pltpu.make_async_remote_copy(
                src_ref=src.at[_flat(peers[kk])] if per_peer else src, dst_ref=dst.at[slot],
                send_sem=small_send.at[kk, a], recv_sem=small_recv.at[kk, a],
                device_id=peers[kk], device_id_type=MESH)

        def to_sibling(t):
            return pltpu.make_async_remote_copy(
                src_ref=stage.at[0], dst_ref=rsib.at[t % 2], send_sem=d2d_send.at[t], recv_sem=d2d_recv.at[t],
                device_id=sibling, device_id_type=MESH)

        def to_owner(t):
            return pltpu.make_async_remote_copy(
                src_ref=stage.at[1], dst_ref=rici.at[t], send_sem=ici_send.at[t], recv_sem=ici_recv.at[t],
                device_id=owners[t], device_id_type=MESH)

        own_small = [pltpu.make_async_copy(src.at[my] if per_peer else src, dst.at[my], local_sems.at[a])
                     for a, (src, dst, per_peer) in enumerate(small)]

        @pl.when((k == 0) & (i == 0))
        def _():
            for cp in own_small:
                cp.start()
            for kk in range(len(peers)):
                for a in range(n_small):
                    small_copy(kk, a).start()

        prod = _dot(ht_ref[...], d_ref[...])

        @pl.when(i == 0)
        def _():
            acc[...] = prod

        @pl.when(i > 0)
        def _():
            acc[...] += prod

        @pl.when(i == nt - 1)
        def _():
            for s in range(n_steps):
                @pl.when(k == s)
                def _():
                    t = s // 2
                    if s % 2 == 0:
                        if t >= 1:
                            to_sibling(t - 1).wait_send()
                        stage[0] = acc[...].astype(BF16)
                        to_sibling(t).start()
                    elif t < 3:
                        if t >= 1:
                            to_owner(t - 1).wait_send()
                        to_sibling(t).wait_recv()
                        stage[1] = (acc[...] + rsib[t % 2].astype(F32)).astype(BF16)
                        to_owner(t).start()
                    else:
                        to_sibling(t).wait_recv()
                        total = acc[...] + rsib[t % 2].astype(F32)
                        for j in range(3):
                            to_owner(j).wait_recv()
                            total = total + rici[j].astype(F32)
                        g_ref[...] = total
                        to_owner(2).wait_send()
                        to_sibling(3).wait_send()
                        for q in range(len(peers)):
                            for a in range(n_small):
                                small_copy(q, a).wait_send()
                                small_copy(q, a, receive=True).wait_recv()
                        for cp in own_small:
                            cp.wait()

    blk = (D_MODEL, W_BLK)
    grid_spec = pltpu.PrefetchScalarGridSpec(
        num_scalar_prefetch=1, grid=(n_steps, nt),
        in_specs=[pl.BlockSpec((D_MODEL, ts), lambda k, i, o: (0, i)),
                  pl.BlockSpec((ts, W_BLK), lambda k, i, o: (i, _dp_block(o[k]))),
                  ANY, ANY, ANY],
        out_specs=[pl.BlockSpec(blk, lambda k, i, o: (0, 0)), ANY, ANY, ANY],
        scratch_shapes=[pltpu.VMEM(blk, F32), pltpu.VMEM((2,) + blk, BF16),
                        pltpu.VMEM((2,) + blk, BF16), pltpu.VMEM((3,) + blk, BF16),
                        pltpu.SemaphoreType.DMA((4,)), pltpu.SemaphoreType.DMA((4,)),
                        pltpu.SemaphoreType.DMA((3,)), pltpu.SemaphoreType.DMA((3,)),
                        pltpu.SemaphoreType.DMA((7, 3)), pltpu.SemaphoreType.DMA((7, 3)),
                        pltpu.SemaphoreType.DMA((3,))])
    return pl.pallas_call(
        body, name="w_in_grad_scatter", grid_spec=grid_spec,
        out_shape=[jax.ShapeDtypeStruct(blk, F32),
                   jax.ShapeDtypeStruct(d_proj.shape, BF16), jax.ShapeDtypeStruct(d_wo.shape, BF16),
                   jax.ShapeDtypeStruct((N_DEV,) + pack.shape, F32)],
        compiler_params=_params(2),
    )(order, ht, dP, d_proj, d_wo, pack)


def _adamw(w, g, m, v):
    m = ADAM_B1 * m + (1.0 - ADAM_B1) * g
    v = ADAM_B2 * v + (1.0 - ADAM_B2) * (g * g)
    m_hat = m / (1.0 - ADAM_B1 ** ADAM_STEP)
    v_hat = v / (1.0 - ADAM_B2 ** ADAM_STEP)
    delta = -ADAM_LR * (m_hat / (jnp.sqrt(v_hat) + ADAM_EPS) + ADAM_WD * w)
    return delta, m, v


def _sum_adamw(parts, w, m, v, name):
    R, C = w.shape
    n = parts.shape[0]
    tr = min(R, 256)

    def body(p_ref, w_ref, m_ref, v_ref, g_ref, d_ref, nm_ref, nv_ref):
        g = p_ref[0].astype(F32)
        for s in range(1, n):
            g = g + p_ref[s].astype(F32)
        g_ref[...] = g
        d_ref[...], nm_ref[...], nv_ref[...] = _adamw(w_ref[...], g, m_ref[...], v_ref[...])

    tile = pl.BlockSpec((tr, C), lambda i: (i, 0))
    return pl.pallas_call(
        body, name=name, grid=(R // tr,),
        out_shape=[jax.ShapeDtypeStruct((R, C), F32)] * 4,
        in_specs=[pl.BlockSpec((n, tr, C), lambda i: (0, i, 0)), tile, tile, tile],
        out_specs=[tile] * 4,
        compiler_params=_params(1),
    )(parts, w, m, v)


def _adamw_mid(r_proj, r_wo, params):
    def body(rp_ref, rw_ref, *refs):
        ins, outs = refs[:9], refs[9:]

        def total(part):
            g = part(0).astype(F32)
            for s in range(1, N_DEV):
                g = g + part(s).astype(F32)
            return g

        grads = (total(lambda s: rp_ref[s, :, 0:128]), total(lambda s: rp_ref[s, :, 128:256]),
                 total(lambda s: rw_ref[s]))
        for n, g in enumerate(grads):
            w, m, v = (r[...] for r in ins[3 * n:3 * n + 3])
            outs[4 * n][...] = g
            outs[4 * n + 1][...], outs[4 * n + 2][...], outs[4 * n + 3][...] = _adamw(w, g, m, v)

    return pl.pallas_call(
        body, name="adamw_mid",
        out_shape=[jax.ShapeDtypeStruct(params[3 * n].shape, F32) for n in range(3) for _ in range(4)],
        compiler_params=pltpu.CompilerParams(vmem_limit_bytes=VMEM_LIMIT),
    )(r_proj, r_wo, *params)


def _adamw_small(r_pack, params):
    wide = 384

    def body(p_ref, *refs):
        ins, loss_ref, outs = refs[:15], refs[15], refs[16:]
        tot = p_ref[0]
        for s in range(1, N_DEV):
            tot = tot + p_ref[s]
        me = _flat(_mesh_pos())
        loss_ref[...] = jnp.sum(tot[2:3, :], axis=-1, keepdims=True)
        mine = pltpu.roll(tot[0:8, 0:D_CONV], (D_CONV - 64 * me) % D_CONV, 1)
        col = lax.broadcasted_iota(jnp.int32, (D_MODEL, wide), 0)
        idx = lax.broadcasted_iota(jnp.int32, (D_MODEL, wide), 1)
        near = (idx > MAX_REL - CHUNK) & (idx < 2 * MAX_REL) & (col == PADK + MAX_REL - idx)
        far = (idx == 2 * MAX_REL) & (col == D_MODEL - 1)
        perm = jnp.where(near | far, 1.0, 0.0).astype(F32)
        g_rel = jnp.dot(tot[8:16], perm, precision=lax.Precision.HIGHEST, preferred_element_type=F32)
        grads = (tot[0:1], tot[1:2], mine[3:6, 0:64], tot[6:7, 0:D_CONV], g_rel[:, 0:N_REL])
        for n, g in enumerate(grads):
            w, m, v = (r[...] for r in ins[3 * n:3 * n + 3])
            outs[4 * n][...] = g
            outs[4 * n + 1][...], outs[4 * n + 2][...], outs[4 * n + 3][...] = _adamw(w, g, m, v)

    return pl.pallas_call(
        body, name="adamw_small",
        out_shape=[jax.ShapeDtypeStruct((1, 1), F32)]
        + [jax.ShapeDtypeStruct(params[3 * n].shape, F32) for n in range(5) for _ in range(4)],
    )(r_pack, *params)


def _pad_row(a, width=D_MODEL):
    a = a.reshape(-1, a.shape[-1])
    return jnp.pad(a, ((0, 0), (0, width - a.shape[-1])))


def kernel(x, norm_g, w_in, rel_bias, w_att_out, conv_w, conv_b, w_conv_out, w_out, final_norm_g, loss_target, m_norm_g, m_w_in, m_rel_bias, m_w_att_out, m_conv_w, m_conv_b, m_w_conv_out, m_w_out, m_final_norm_g, v_norm_g, v_w_in, v_rel_bias, v_w_att_out, v_conv_w, v_conv_b, v_w_conv_out, v_w_out, v_final_norm_g):
    S = x.shape[1]
    x2d = x.reshape(S, D_MODEL)
    tgt = loss_target.reshape(S, D_MODEL)
    me = 4 * lax.axis_index("x") + 2 * lax.axis_index("y") + lax.axis_index("c")
    row = lambda a: a.reshape(1, D_MODEL)

    proj_sh = jnp.concatenate([w_att_out[0], w_conv_out[0]], axis=1).astype(BF16)
    cw_sh = jnp.pad(conv_w[0], ((0, 5), (0, 64)))
    P, ht, w_in_g, proj_g, w_out_g, cw_g = _gather_in_proj(
        x2d, norm_g, w_in[0].astype(BF16), [proj_sh, w_out[0].astype(BF16), cw_sh],
        me ^ _by_core(GATHER_MASKS))

    bias_tab = _bias_table(rel_bias[0])
    att, ex, rinv = _attn_fwd(P, bias_tab)
    dx2, dP, datt, d_wo, d_proj, sm1, sm2 = _token_local(
        x2d, tgt, P, att, proj_g, w_out_g.reshape(D_MODEL, D_MODEL), cw_g, conv_b, row(final_norm_g))
    dP, dbias = _attn_bwd(P, att, datt, ex, rinv, dP)
    grad_x, dnorm = _in_proj_bwd(x2d, norm_g, dx2, dP, w_in_g)

    pack = jnp.concatenate([dnorm[0:1], sm1[0:2], _pad_row(sm2[0:4]), jnp.zeros((1, D_MODEL), F32), dbias],
                           axis=0)
    g_win_sum, r_proj, r_wo, r_pack = _w_in_grad_scatter(
        ht, dP, d_proj, d_wo.reshape(N_DEV, 128, D_MODEL), pack, me ^ _by_core(SCATTER_MASKS))

    res = {"w_in": _sum_adamw(g_win_sum[None], w_in[0], m_w_in[0], v_w_in[0], "adamw_w_in")}
    mid = _adamw_mid(r_proj, r_wo, (w_att_out[0], m_w_att_out[0], v_w_att_out[0],
                                    w_conv_out[0], m_w_conv_out[0], v_w_conv_out[0],
                                    w_out[0], m_w_out[0], v_w_out[0]))
    for n, name in enumerate(("w_att_out", "w_conv_out", "w_out")):
        res[name] = mid[4 * n:4 * n + 4]
    small = _adamw_small(r_pack, (norm_g, m_norm_g, v_norm_g,
                                  row(final_norm_g), row(m_final_norm_g), row(v_final_norm_g),
                                  conv_w[0], m_conv_w[0], v_conv_w[0], conv_b, m_conv_b, v_conv_b,
                                  rel_bias[0], m_rel_bias[0], v_rel_bias[0]))
    loss = small[0].reshape(())
    for n, name in enumerate(("norm_g", "final_norm_g", "conv_w", "conv_b", "rel_bias")):
        res[name] = small[1 + 4 * n:5 + 4 * n]

    leading = {"norm_g": (1, D_MODEL), "final_norm_g": (D_MODEL,), "conv_b": (1, D_CONV)}
    outs = []
    for kind in range(4):
        for name in ("norm_g", "w_in", "rel_bias", "w_att_out", "conv_w", "conv_b", "w_conv_out", "w_out",
                     "final_norm_g"):
            a = res[name][kind]
            outs.append(a.reshape(leading[name]) if name in leading else a[None])
    return (loss, grad_x.reshape(1, S, D_MODEL), *outs)
```

```python
import functools

import numpy as np
import jax
import jax.numpy as jnp
from jax import lax
from jax.experimental import pallas as pl
from jax.experimental.pallas import tpu as pltpu

F32 = jnp.float32
BF16 = jnp.bfloat16

D_MODEL = 1024
CHUNK = 64
N_LEFT = 8
HEADS = 8
D_ATT = 512
D_CONV = 512
MAX_REL = 128
N_REL = 2 * MAX_REL + 1
IN_COLS = 6144
EPS = 1e-6
NEG_BIG = -1e30
N_DEV = 8
W_BLK = IN_COLS // N_DEV
QB = 4 * CHUNK
KB = QB + N_LEFT * CHUNK
PADK = N_LEFT * CHUNK
SCALE = 64 ** -0.5
LOG2E = 1.4426950408889634
GATE_COLS = IN_COLS - 3 * D_ATT

ADAM_LR = 0.001
ADAM_B1 = 0.9
ADAM_B2 = 0.999
ADAM_EPS = 1e-08
ADAM_WD = 0.01
ADAM_STEP = 10

VMEM_LIMIT = 56 * 1024 * 1024

MESH = pl.DeviceIdType.MESH
ANY = pl.BlockSpec(memory_space=pl.ANY)


def _params(n_grid, vmem_limit=VMEM_LIMIT):
    return pltpu.CompilerParams(dimension_semantics=("arbitrary",) * n_grid,
                                vmem_limit_bytes=vmem_limit)


def _dot(a, b):
    return jnp.dot(a, b, preferred_element_type=F32)


def _dot_nt(a, b):
    return lax.dot_general(a, b, (((1,), (1,)), ((), ())), preferred_element_type=F32)


def _dot_tn(a, b):
    return lax.dot_general(a, b, (((0,), (0,)), ((), ())), preferred_element_type=F32)


def _sigmoid(z):
    return 0.5 * jnp.tanh(0.5 * z) + 0.5


def _mesh_pos():
    return lax.axis_index("x"), lax.axis_index("y"), lax.axis_index("c")


def _flat(p):
    return 4 * p[0] + 2 * p[1] + p[2]


def _by_core(masks):
    m0, m1 = (jnp.array(m, jnp.int32) for m in masks)
    return jnp.where(lax.axis_index("c") == 0, m0, m1)


GATHER_MASKS = ((0, 1, 4, 3, 2, 5, 6, 7), (0, 1, 2, 5, 4, 3, 6, 7))


def _gather_in_proj(x, norm_g, w_sh, smalls, order):
    S = x.shape[0]
    ts = 1024
    nt = S // ts
    n_small = len(smalls)
    n_steps = N_DEV

    def body(order_ref, x_ref, g_ref, w_hbm, *rest):
        small_in = rest[:n_small]
        p_ref, ht_ref, wg_hbm = rest[n_small:n_small + 3]
        small_out = rest[n_small + 3:2 * n_small + 3]
        (wbuf, hbuf, own_sem, send_sems, recv_sems, out_sems,
         small_send, small_recv, small_local) = rest[2 * n_small + 3:]
        k, i = pl.program_id(0), pl.program_id(1)
        x_, y_, c_ = _mesh_pos()
        me, sibling = (x_, y_, c_), (x_, y_, 1 - c_)
        my = _flat(me)
        chips = [(x_ ^ (1 - c_), y_ ^ c_), (x_ ^ c_, y_ ^ (1 - c_)), (1 - x_, 1 - y_)]
        peers = [sibling] + [(*chip, c_) for chip in chips] + [(*chip, 1 - c_) for chip in chips]

        def wcopy(sem, block, to, from_input=False):
            dst = wbuf.at[_flat(block)]
            return pltpu.make_async_remote_copy(
                src_ref=w_hbm if from_input else dst, dst_ref=dst,
                send_sem=send_sems.at[sem], recv_sem=recv_sems.at[sem], device_id=to, device_id_type=MESH)

        def small_copy(q, a, receive=False):
            slot = _flat(peers[q]) if receive else my
            return pltpu.make_async_remote_copy(
                src_ref=small_in[a], dst_ref=small_out[a].at[slot],
                send_sem=small_send.at[q, a], recv_sem=small_recv.at[q, a],
                device_id=peers[q], device_id_type=MESH)

        def keep(step, block):
            col = pl.multiple_of(_dp_block(_flat(block)) * W_BLK, 128)
            return pltpu.make_async_copy(wbuf.at[_flat(block)], wg_hbm.at[:, pl.ds(col, W_BLK)], out_sems.at[step])

        own = pltpu.make_async_copy(w_hbm, wbuf.at[my], own_sem)
        small_own = [pltpu.make_async_copy(small_in[a], small_out[a].at[my], small_local.at[a])
                     for a in range(n_small)]
        passed_on = [(*chips[1], 1 - c_), (*chips[0], 1 - c_), (*chips[2], 1 - c_)]
        arrivals = [me, sibling]
        for j in range(3):
            arrivals += [(*chips[j], c_), passed_on[j]]

        @pl.when(i == 0)
        def _():
            for kk in range(n_steps):
                @pl.when(k == kk)
                def _():
                    j = kk // 2 - 1
                    if kk == 0:
                        own.start()
                        wcopy(0, me, sibling, True).start()
                        wcopy(1, me, (*chips[0], c_), True).start()
                        own.wait()
                    elif kk == 1:
                        wcopy(0, sibling, me).wait_recv()
                        wcopy(2, me, (*chips[1], c_), True).start()
                    elif kk % 2 == 0:
                        wcopy(1 + j, (*chips[j], c_), me).wait_recv()
                        wcopy(4 + j, (*chips[j], c_), sibling).start()
                        if kk == 2:
                            wcopy(3, me, (*chips[2], c_), True).start()
                    else:
                        wcopy(4 + j, passed_on[j], me).wait_recv()
                        if kk == 3:
                            for cp in small_own:
                                cp.start()
                            for q in range(len(peers)):
                                for a in range(n_small):
                                    small_copy(q, a).start()
                    keep(kk, arrivals[kk]).start()

        row0 = pl.multiple_of(i * ts, ts)

        @pl.when(k == 0)
        def _():
            xf = x_ref[...]
            r = lax.rsqrt(jnp.mean(xf * xf, axis=-1, keepdims=True) + EPS)
            hf = (xf * r) * g_ref[...]
            hbuf[pl.ds(row0, ts), :] = hf.astype(BF16)
            ht_ref[...] = hf.astype(BF16).T

        p_ref[...] = _dot(hbuf[pl.ds(row0, ts), :], wbuf[order_ref[k]]).astype(BF16)

        @pl.when((k == n_steps - 1) & (i == nt - 1))
        def _():
            wcopy(0, me, sibling, True).wait_send()
            for j, chip in enumerate(chips):
                wcopy(1 + j, me, (*chip, c_), True).wait_send()
                wcopy(4 + j, (*chip, c_), sibling).wait_send()
            for kk in range(n_steps):
                keep(kk, arrivals[kk]).wait()
            for cp in small_own:
                cp.wait()
            for q in range(len(peers)):
                for a in range(n_small):
                    small_copy(q, a).wait_send()
                    small_copy(q, a, receive=True).wait_recv()

    first_pass = lambda k, i: jnp.where(k == 0, i, nt - 1)
    grid_spec = pltpu.PrefetchScalarGridSpec(
        num_scalar_prefetch=1, grid=(n_steps, nt),
        in_specs=[pl.BlockSpec((ts, D_MODEL), lambda k, i, o: (first_pass(k, i), 0)),
                  pl.BlockSpec((1, D_MODEL), lambda k, i, o: (0, 0)), ANY] + [ANY] * n_small,
        out_specs=[pl.BlockSpec((ts, W_BLK), lambda k, i, o: (i, o[k])),
                   pl.BlockSpec((D_MODEL, ts), lambda k, i, o: (0, first_pass(k, i))), ANY] + [ANY] * n_small,
        scratch_shapes=[pltpu.VMEM((N_DEV, D_MODEL, W_BLK), BF16), pltpu.VMEM((S, D_MODEL), BF16),
                        pltpu.SemaphoreType.DMA, pltpu.SemaphoreType.DMA((7,)), pltpu.SemaphoreType.DMA((7,)),
                        pltpu.SemaphoreType.DMA((n_steps,)),
                        pltpu.SemaphoreType.DMA((7, n_small)), pltpu.SemaphoreType.DMA((7, n_small)),
                        pltpu.SemaphoreType.DMA((n_small,))])
    return pl.pallas_call(
        body, name="gather_in_proj", grid_spec=grid_spec,
        out_shape=[jax.ShapeDtypeStruct((S, IN_COLS), BF16), jax.ShapeDtypeStruct((D_MODEL, S), BF16),
                   jax.ShapeDtypeStruct((D_MODEL, IN_COLS), BF16)]
        + [jax.ShapeDtypeStruct((N_DEV,) + s.shape, s.dtype) for s in smalls],
        compiler_params=_params(2),
    )(order, x, norm_g, w_sh, *smalls)


def _bias_table(rel_bias):
    wide = 1024

    def body(r_ref, o_ref):
        h = pl.program_id(0)
        col = lax.broadcasted_iota(jnp.int32, (1, wide), 1)
        k_minus_q = jnp.where(col < KB, col, col - wide)
        idx = jnp.clip(PADK - k_minus_q, -MAX_REL, MAX_REL) + MAX_REL
        f = jnp.zeros((1, wide), F32)
        for r in range(MAX_REL - CHUNK + 1, N_REL):
            f = jnp.where(idx == r, r_ref[h, r], f)
        kcol = lax.broadcasted_iota(jnp.int32, (1, KB), 1)
        kc = kcol >> 6
        sub = lax.broadcasted_iota(jnp.int32, (8, 1), 0)
        f8 = jnp.broadcast_to(f * LOG2E, (8, wide))
        base = f8
        for r in range(1, 8):
            base = jnp.where(sub == r, pltpu.roll(f8, r, 1), base)
        for qh in range(QB // 8):
            rows = (pltpu.roll(base, 8 * qh, 1) if qh else base)[:, 0:KB]
            qc = (8 * qh) // CHUNK
            band = (kc >= qc) & (kc <= qc + N_LEFT)
            for t in range(3):
                o_ref[t, 0, 8 * qh:8 * qh + 8, :] = jnp.where(band & (kcol >= PADK - t * QB), rows, NEG_BIG)

    return pl.pallas_call(
        body, name="bias_table", grid=(HEADS,),
        out_shape=jax.ShapeDtypeStruct((3, HEADS, QB, KB), F32),
        in_specs=[pl.BlockSpec(memory_space=pltpu.SMEM)],
        out_specs=pl.BlockSpec((3, 1, QB, KB), lambda h: (0, h, 0, 0)),
        compiler_params=_params(1),
    )(rel_bias)


KEY_GROUP = 4


def _load_keys(g, nb, p_hbm, kp, vp, sem):
    rows = KEY_GROUP * QB
    n_groups = p_hbm.shape[0] // rows

    def copies(c):
        src = pl.ds(c * rows, rows)
        dst = pl.ds(PADK + c * rows, rows)
        return (pltpu.make_async_copy(p_hbm.at[src, D_ATT:2 * D_ATT], kp.at[dst, :], sem.at[0, c]),
                pltpu.make_async_copy(p_hbm.at[src, 2 * D_ATT:3 * D_ATT], vp.at[dst, :], sem.at[1, c]))

    @pl.when(g == 0)
    def _():
        kp[0:PADK, :] = jnp.zeros((PADK, D_ATT), BF16)
        vp[0:PADK, :] = jnp.zeros((PADK, D_ATT), BF16)
        for c in range(n_groups):
            for cp in copies(c):
                cp.start()

    @pl.when((g % KEY_GROUP == 0) & (g < nb))
    def _():
        for cp in copies(g // KEY_GROUP):
            cp.wait()


def _attn_fwd(P, bias_tab):
    S = P.shape[0]
    nb = S // QB

    def body(q_ref, p_hbm, bias_ref, o_ref, ex_ref, rinv_ref, kp, vp, sem):
        g = pl.program_id(0)
        _load_keys(g, nb, p_hbm, kp, vp, sem)
        start = pl.multiple_of(g * QB, QB)
        lane = lax.broadcasted_iota(jnp.int32, (1, 128), 1)
        half = lambda h: (lane < 64) if h % 2 == 0 else (lane >= 64)
        pair = lambda h: slice(128 * (h // 2), 128 * (h // 2 + 1))

        def scores(h):
            qp = q_ref[:, pair(h)] * SCALE
            qm = jnp.where(half(h), qp, jnp.zeros_like(qp))
            return (_dot_nt(qm, kp[pl.ds(start, KB), pair(h)]) * LOG2E + bias_ref[0, h]).astype(BF16)

        outs = []
        s_next = scores(0)
        for h in range(HEADS):
            s = s_next
            if h + 1 < HEADS:
                s_next = scores(h + 1)
            mx = jnp.max(s, axis=-1, keepdims=True)
            ex = jnp.exp2(s - mx)
            ex_ref[:, KB * h:KB * (h + 1)] = ex
            vpair = vp[pl.ds(start, KB), pair(h)]
            o = _dot(ex, jnp.where(half(h), vpair, jnp.ones_like(vpair)))
            rinv = 1.0 / pltpu.roll(o, 64, 1)
            outs.append(o * rinv)
            rinv_ref[:, h:h + 1] = rinv[:, 0:1] if h % 2 == 0 else 1.0 / o[:, 0:1]
            if h % 2 == 1:
                o_ref[:, pair(h)] = jnp.where(lane < 64, outs[h - 1], outs[h]).astype(BF16)

    return pl.pallas_call(
        body, name="attn_fwd", grid=(nb,),
        out_shape=[jax.ShapeDtypeStruct((S, D_ATT), BF16), jax.ShapeDtypeStruct((S, HEADS * KB), BF16),
                   jax.ShapeDtypeStruct((S, HEADS), F32)],
        in_specs=[pl.BlockSpec((QB, D_ATT), lambda g: (g, 0)), ANY,
                  pl.BlockSpec((1, HEADS, QB, KB), lambda g: (jnp.minimum(g, 2), 0, 0, 0))],
        out_specs=[pl.BlockSpec((QB, D_ATT), lambda g: (g, 0)),
                   pl.BlockSpec((QB, HEADS * KB), lambda g: (g, 0)),
                   pl.BlockSpec((QB, HEADS), lambda g: (g, 0))],
        scratch_shapes=[pltpu.VMEM((S + PADK, D_ATT), BF16), pltpu.VMEM((S + PADK, D_ATT), BF16),
                        pltpu.SemaphoreType.DMA((2, S // (KEY_GROUP * QB)))],
        compiler_params=_params(1),
    )(P, P, bias_tab)


def _token_local(x, tgt, P, att, proj_g, w_out, cw_g, conv_b, final_g):
    S = x.shape[0]
    ts = 256
    nt = S // ts
    hb = 16

    def body(x_ref, t_ref, s1_ref, s2_ref, s3_ref, h1_ref, h2_ref, att_ref,
             pg_ref, wo_ref, cwg_ref, cb_ref, g2_ref,
             dx2_ref, dg_ref, datt_ref, dwo_ref, dproj_ref, sm1_ref, sm2_ref,
             carry, wao_ref, wco_ref, cw_ref, dwo_acc, dwao_acc, dwco_acc):
        i = pl.program_id(0)
        t = nt - 1 - i

        @pl.when(i == 0)
        def _():
            dwo_acc[...] = jnp.zeros_like(dwo_acc)
            dwao_acc[...] = jnp.zeros_like(dwao_acc)
            dwco_acc[...] = jnp.zeros_like(dwco_acc)
            lane = lax.broadcasted_iota(jnp.int32, (1, 128), 1)
            for j in range(N_DEV):
                wao_ref[:, 128 * j:128 * (j + 1)] = pg_ref[j, :, 0:128]
                wco_ref[:, 128 * j:128 * (j + 1)] = pg_ref[j, :, 128:256]
            for p in range(N_DEV // 2):
                cw_ref[:, 128 * p:128 * (p + 1)] = jnp.where(
                    lane < 64, cwg_ref[2 * p], pltpu.roll(cwg_ref[2 * p + 1], 64, 1))
            sm1_ref[...] = jnp.zeros_like(sm1_ref)
            sm2_ref[...] = jnp.zeros_like(sm2_ref)
            carry[...] = jnp.zeros_like(carry)

        za = s1_ref[:, 0:512]
        gb = s1_ref[:, 512:1024]
        gc = s1_ref[:, 1024:1536].astype(F32)
        u = s2_ref[:, 0:512].astype(F32)
        zc = s2_ref[:, 512:1024]
        ga = jnp.concatenate([s2_ref[:, 1024:1536], s3_ref[:, 0:512]], axis=1)
        gv = s3_ref[:, 512:1536]
        att = att_ref[...]
        row = lax.broadcasted_iota(jnp.int32, (ts, 1), 0)

        sa = _sigmoid(za)
        silu_a = za * sa
        att_g = att * silu_a
        y_att = _dot(att_g, wao_ref[...])

        cu = gc * u
        keep = jnp.where(t > 0, 1.0, 0.0).astype(F32)
        hcu = (h1_ref[:, 1024:1536].astype(F32) * h2_ref[:, 0:512].astype(F32)) * keep
        cu_m1 = jnp.where(row == 0, hcu[hb - 1:hb, :], pltpu.roll(cu, 1, 0))
        cu_m2 = jnp.where(row == 0, hcu[hb - 2:hb - 1, :],
                          jnp.where(row == 1, hcu[hb - 1:hb, :], pltpu.roll(cu, 2, 0)))
        w0, w1, w2 = cw_ref[0:1, :], cw_ref[1:2, :], cw_ref[2:3, :]
        vconv = w0 * cu_m2 + w1 * cu_m1 + w2 * cu + cb_ref[...]
        vcb = vconv.astype(BF16)
        sc = _sigmoid(zc)
        silu_c = zc * sc
        cg = gb * vcb * silu_c
        y_conv = _dot(cg, wco_ref[...])

        sga = _sigmoid(ga)
        sgv = _sigmoid(gv)
        yab, ycb = y_att.astype(BF16), y_conv.astype(BF16)
        m = sga * yab + sgv * ycb
        x2 = x_ref[...] + _dot(m, wo_ref[...])
        r2 = lax.rsqrt(jnp.mean(x2 * x2, axis=-1, keepdims=True) + EPS)
        xn2 = x2 * r2
        g2 = g2_ref[...]
        err = xn2 * g2 - t_ref[...]
        sm1_ref[1:2, :] += jnp.sum(err * err, axis=0, keepdims=True) * (0.5 / D_MODEL)

        dy = err * (1.0 / D_MODEL)
        sm1_ref[0:1, :] += jnp.sum(dy * xn2, axis=0, keepdims=True)
        dxn = dy * g2
        dx2 = r2 * (dxn - xn2 * jnp.mean(dxn * xn2, axis=-1, keepdims=True))
        dx2_ref[...] = dx2
        dx2b = dx2.astype(BF16)
        dwo_acc[...] += _dot_tn(m, dx2b)
        dm = _dot_nt(dx2b, wo_ref[...])
        dmb = dm.astype(BF16)
        dya = dmb * sga
        dyc = dmb * sgv
        dg_ref[:, 2560:3584] = dmb * yab * (sga * (1.0 - sga))
        dg_ref[:, 3584:4608] = dmb * ycb * (sgv * (1.0 - sgv))
        dwao_acc[...] += _dot_tn(att_g, dya)
        dwco_acc[...] += _dot_tn(cg, dyc)
        datt_g = _dot_nt(dya, wao_ref[...])
        dcg = _dot_nt(dyc, wco_ref[...])
        dagb, dcgb = datt_g.astype(BF16), dcg.astype(BF16)
        datt_ref[...] = dagb * silu_a
        dg_ref[:, 0:512] = dagb * att * (sa + silu_a * (1.0 - sa))
        dg_ref[:, 512:1024] = dcgb * vcb * silu_c
        dg_ref[:, 2048:2560] = dcgb * gb * vcb * (sc + silu_c * (1.0 - sc))
        dv = dcg * (gb * silu_c).astype(F32)
        sm2_ref[3:4, :] += jnp.sum(dv, axis=0, keepdims=True)
        sm2_ref[0:1, :] += jnp.sum(dv * cu_m2, axis=0, keepdims=True)
        sm2_ref[1:2, :] += jnp.sum(dv * cu_m1, axis=0, keepdims=True)
        sm2_ref[2:3, :] += jnp.sum(dv * cu, axis=0, keepdims=True)
        nxt = carry[...]
        dv_p1 = jnp.where(row == ts - 1, nxt[0:1, :], pltpu.roll(dv, ts - 1, 0))
        dv_p2 = jnp.where(row == ts - 1, nxt[1:2, :],
                          jnp.where(row == ts - 2, nxt[0:1, :], pltpu.roll(dv, ts - 2, 0)))
        dcu = w2 * dv + w1 * dv_p1 + w0 * dv_p2
        carry[...] = dv[0:8, :]
        dg_ref[:, 1024:1536] = (dcu * u).astype(BF16)
        dg_ref[:, 1536:2048] = (dcu * gc).astype(BF16)

        @pl.when(i == nt - 1)
        def _():
            dwo_ref[...] = dwo_acc[...].astype(BF16)
            for j in range(N_DEV):
                dproj_ref[j, :, 0:128] = dwao_acc[:, 128 * j:128 * (j + 1)].astype(BF16)
                dproj_ref[j, :, 128:256] = dwco_acc[:, 128 * j:128 * (j + 1)].astype(BF16)

    tile = lambda w: pl.BlockSpec((ts, w), lambda i: (nt - 1 - i, 0))
    seg = lambda c: pl.BlockSpec((ts, 1536), lambda i: (nt - 1 - i, c))
    halo = lambda c: pl.BlockSpec((hb, 1536), lambda i: (jnp.maximum((nt - 1 - i) * (ts // hb) - 1, 0), c))
    full = lambda a: pl.BlockSpec(a.shape, lambda i: (0,) * a.ndim)
    acc = lambda r, c: pl.BlockSpec((r, c), lambda i: (0, 0))
    return pl.pallas_call(
        body, name="token_local", grid=(nt,),
        out_shape=[jax.ShapeDtypeStruct((S, D_MODEL), F32), jax.ShapeDtypeStruct((S, IN_COLS), BF16),
                   jax.ShapeDtypeStruct((S, D_ATT), BF16), jax.ShapeDtypeStruct((D_MODEL, D_MODEL), BF16),
                   jax.ShapeDtypeStruct(proj_g.shape, BF16),
                   jax.ShapeDtypeStruct((8, D_MODEL), F32), jax.ShapeDtypeStruct((8, D_CONV), F32)],
        in_specs=[tile(D_MODEL), tile(D_MODEL), seg(1), seg(2), seg(3), halo(1), halo(2), tile(D_ATT),
                  full(proj_g), full(w_out), full(cw_g), full(conv_b), full(final_g)],
        out_specs=[tile(D_MODEL), tile(GATE_COLS), tile(D_ATT), acc(D_MODEL, D_MODEL), full(proj_g),
                   acc(8, D_MODEL), acc(8, D_CONV)],
        scratch_shapes=[pltpu.VMEM((8, D_CONV), F32),
                        pltpu.VMEM((D_ATT, D_MODEL), BF16), pltpu.VMEM((D_CONV, D_MODEL), BF16),
                        pltpu.VMEM((8, D_CONV), F32), pltpu.VMEM((D_MODEL, D_MODEL), F32),
                        pltpu.VMEM((D_ATT, D_MODEL), F32), pltpu.VMEM((D_CONV, D_MODEL), F32)],
        compiler_params=_params(1),
    )(x, tgt, P, P, P, P, P, att, proj_g, w_out, cw_g, conv_b, final_g)


def _fold_diagonals(d_ref, o_ref):
    wide = D_MODEL
    sub = lax.broadcasted_iota(jnp.int32, (8, 1), 0)
    col = lax.broadcasted_iota(jnp.int32, (1, wide), 1)
    pad = jnp.zeros((8, wide - KB), F32)
    for h in range(HEADS):
        acc = jnp.concatenate([d_ref[h, 0:8, :], pad], axis=1)
        for qh in range(1, QB // 8):
            a = jnp.concatenate([d_ref[h, 8 * qh:8 * qh + 8, :], pad], axis=1)
            acc = acc + pltpu.roll(a, wide - 8 * qh, 1)
        for r in range(1, 8):
            acc = jnp.where(sub == r, pltpu.roll(acc, wide - r, 1), acc)
        vec = jnp.sum(acc, axis=0, keepdims=True)
        far = (col <= PADK - MAX_REL) | (col > KB)
        tail = jnp.sum(jnp.where(far, vec, 0.0), axis=-1, keepdims=True)
        o_ref[h:h + 1, :] = jnp.where(col == wide - 1, tail, vec)


def _attn_bwd(P, att, datt, ex, rinv, dP):
    S = P.shape[0]
    nb = S // QB

    def body(q_ref, att_ref, datt_ref, ex_ref, rinv_ref, p_hbm, dp_hbm, out_ref, dbias_ref,
             kp, vp, dq_ring, dk_ring, dv_ring, db_ref, sem):
        g = pl.program_id(0)

        _load_keys(g, nb, p_hbm, kp, vp, sem)

        @pl.when(g == 0)
        def _():
            db_ref[...] = jnp.zeros_like(db_ref)
            dk_ring[...] = jnp.zeros_like(dk_ring)
            dv_ring[...] = jnp.zeros_like(dv_ring)

        s_new = g % 3
        s_mid = (g + 2) % 3
        s_old = (g + 1) % 3

        @pl.when(g < nb)
        def _():
            start = pl.multiple_of(g * QB, QB)
            lane = lax.broadcasted_iota(jnp.int32, (1, 128), 1)
            for p in range(HEADS // 2):
                cols = slice(128 * p, 128 * (p + 1))
                qp = q_ref[:, cols] * SCALE
                op = att_ref[:, cols].astype(F32)
                dop = datt_ref[:, cols]
                kpair = kp[pl.ds(start, KB), cols]
                vpair = vp[pl.ds(start, KB), cols]
                dqs = []
                dk_acc = jnp.zeros((KB, 128), F32)
                dv_acc = jnp.zeros((KB, 128), F32)
                for e in range(2):
                    h = 2 * p + e
                    lm = (lane < 64) if e == 0 else (lane >= 64)
                    qm = jnp.where(lm, qp, jnp.zeros_like(qp))
                    dom = jnp.where(lm, dop, jnp.zeros_like(dop))
                    exh = ex_ref[:, KB * h:KB * (h + 1)]
                    rinv = rinv_ref[:, h:h + 1]
                    domf = dom.astype(F32)
                    dp = _dot_nt(dom, vpair)
                    delta = jnp.sum(domf * op, axis=-1, keepdims=True)
                    dsb = exh * ((dp - delta) * rinv).astype(BF16)
                    db_ref[h] += dsb.astype(F32)
                    dqs.append(_dot(dsb, kpair) * SCALE)
                    dk_acc = dk_acc + _dot_tn(dsb, qm)
                    dv_acc = dv_acc + _dot_tn(exh, (domf * rinv).astype(BF16))
                dq_ring[s_new, :, cols] = jnp.where(lane < 64, dqs[0], dqs[1])
                dk_ring[s_old, :, cols] += dk_acc[0:QB]
                dk_ring[s_mid, :, cols] += dk_acc[QB:2 * QB]
                dk_ring[s_new, :, cols] = dk_acc[2 * QB:3 * QB]
                dv_ring[s_old, :, cols] += dv_acc[0:QB]
                dv_ring[s_mid, :, cols] += dv_acc[QB:2 * QB]
                dv_ring[s_new, :, cols] = dv_acc[2 * QB:3 * QB]

        @pl.when(g >= 2)
        def _():
            out_ref[:, 0:D_ATT] = dq_ring[s_old].astype(BF16)
            out_ref[:, D_ATT:2 * D_ATT] = dk_ring[s_old].astype(BF16)
            out_ref[:, 2 * D_ATT:3 * D_ATT] = dv_ring[s_old].astype(BF16)

        @pl.when(g == nb + 1)
        def _():
            _fold_diagonals(db_ref, dbias_ref)

    qblk = lambda w: pl.BlockSpec((QB, w), lambda g: (jnp.minimum(g, nb - 1), 0))
    return pl.pallas_call(
        body, name="attn_bwd", grid=(nb + 2,),
        out_shape=[jax.ShapeDtypeStruct((S, IN_COLS), BF16), jax.ShapeDtypeStruct((HEADS, D_MODEL), F32)],
        in_specs=[qblk(D_ATT), qblk(D_ATT), qblk(D_ATT), qblk(HEADS * KB), qblk(HEADS), ANY, ANY],
        out_specs=[pl.BlockSpec((QB, 3 * D_ATT), lambda g: (jnp.maximum(g - 2, 0), GATE_COLS // (3 * D_ATT))),
                   pl.BlockSpec((HEADS, D_MODEL), lambda g: (0, 0))],
        input_output_aliases={6: 0},
        scratch_shapes=[pltpu.VMEM((S + PADK, D_ATT), BF16), pltpu.VMEM((S + PADK, D_ATT), BF16),
                        pltpu.VMEM((3, QB, D_ATT), F32), pltpu.VMEM((3, QB, D_ATT), F32),
                        pltpu.VMEM((3, QB, D_ATT), F32), pltpu.VMEM((HEADS, QB, KB), F32),
                        pltpu.SemaphoreType.DMA((2, S // (KEY_GROUP * QB)))],
        compiler_params=_params(1),
    )(P, att, datt, ex, rinv, P, dP)


def _dp_block(j):
    return (j + GATE_COLS // W_BLK) % N_DEV


def _in_proj_bwd(x, norm_g, dx2, dP, w_in_g):
    S = x.shape[0]
    ts = 512

    def body(x_ref, g_ref, dx2_ref, dp_ref, w_ref, gx_ref, dn_ref):
        @pl.when(pl.program_id(0) == 0)
        def _():
            dn_ref[...] = jnp.zeros_like(dn_ref)

        dh = _dot_nt(dp_ref[...], w_ref[...])
        xf = x_ref[...]
        r = lax.rsqrt(jnp.mean(xf * xf, axis=-1, keepdims=True) + EPS)
        xn = xf * r
        dn_ref[0:1, :] += jnp.sum(dh * xn, axis=0, keepdims=True)
        dhg = dh * g_ref[...]
        gx_ref[...] = dx2_ref[...] + r * (dhg - xn * jnp.mean(dhg * xn, axis=-1, keepdims=True))

    tile = lambda w: pl.BlockSpec((ts, w), lambda i: (i, 0))
    return pl.pallas_call(
        body, name="in_proj_bwd", grid=(S // ts,),
        out_shape=[jax.ShapeDtypeStruct((S, D_MODEL), F32), jax.ShapeDtypeStruct((8, D_MODEL), F32)],
        in_specs=[tile(D_MODEL), pl.BlockSpec((1, D_MODEL), lambda i: (0, 0)), tile(D_MODEL),
                  tile(IN_COLS),
                  pl.BlockSpec((D_MODEL, IN_COLS), lambda i: (0, 0))],
        out_specs=[tile(D_MODEL), pl.BlockSpec((8, D_MODEL), lambda i: (0, 0))],
        compiler_params=_params(1),
    )(x, norm_g, dx2, dP, w_in_g)


SCATTER_MASKS = ((3, 4, 5, 2, 7, 6, 1, 0), (5, 2, 3, 4, 7, 6, 1, 0))


def _w_in_grad_scatter(ht, dP, d_proj, d_wo, pack, order):
    S = ht.shape[1]
    ts = min(S, 2048)
    nt = S // ts
    n_steps = 8

    def body(order_ref, ht_ref, d_ref, proj_hbm, wo_hbm, pack_hbm, g_ref, rproj, rwo, rpack,
             acc, stage, rsib, rici, d2d_send, d2d_recv, ici_send, ici_recv, small_send, small_recv, local_sems):
        k, i = pl.program_id(0), pl.program_id(1)
        x, y, c = _mesh_pos()
        my = _flat((x, y, c))
        sibling = (x, y, 1 - c)
        owners = [(x ^ (1 - c), y ^ c, c), (x ^ c, y ^ (1 - c), c), (1 - x, 1 - y, c)]
        peers = [sibling, (1 - x, y, c), (x, 1 - y, c), (1 - x, 1 - y, c),
                 (1 - x, y, 1 - c), (x, 1 - y, 1 - c), (1 - x, 1 - y, 1 - c)]
        small = ((proj_hbm, rproj, True), (wo_hbm, rwo, True), (pack_hbm, rpack, False))
        n_small = len(small)

        def small_copy(kk, a, receive=False):
            src, dst, per_peer = small[a]
            slot = _flat(peers[kk]) if receive else my
            return pltpu.make_async_remote_copy(
                src_ref=src.at[_flat(peers[kk])] if per_peer else src, dst_ref=dst.at[slot],
                send_sem=small_send.at[kk, a], recv_sem=small_recv.at[kk, a],
                device_id=peers[kk], device_id_type=MESH)

        def to_sibling(t):
            return pltpu.make_async_remote_copy(
                src_ref=stage.at[0], dst_ref=rsib.at[t % 2], send_sem=d2d_send.at[t], recv_sem=d2d_recv.at[t],
                device_id=sibling, device_id_type=MESH)

        def to_owner(t):
            return pltpu.make_async_remote_copy(
                src_ref=stage.at[1], dst_ref=rici.at[t], send_sem=ici_send.at[t], recv_sem=ici_recv.at[t],
                device_id=owners[t], device_id_type=MESH)

        own_small = [pltpu.make_async_copy(src.at[my] if per_peer else src, dst.at[my], local_sems.at[a])
                     for a, (src, dst, per_peer) in enumerate(small)]

        @pl.when((k == 0) & (i == 0))
        def _():
            for cp in own_small:
                cp.start()
            for kk in range(len(peers)):
                for a in range(n_small):
                    small_copy(kk, a).start()

        prod = _dot(ht_ref[...], d_ref[...])

        @pl.when(i == 0)
        def _():
            acc[...] = prod

        @pl.when(i > 0)
        def _():
            acc[...] += prod

        @pl.when(i == nt - 1)
        def _():
            for s in range(n_steps):
                @pl.when(k == s)
                def _():
                    t = s // 2
                    if s % 2 == 0:
                        if t >= 1:
                            to_sibling(t - 1).wait_send()
                        stage[0] = acc[...].astype(BF16)
                        to_sibling(t).start()
                    elif t < 3:
                        if t >= 1:
                            to_owner(t - 1).wait_send()
                        to_sibling(t).wait_recv()
                        stage[1] = (acc[...] + rsib[t % 2].astype(F32)).astype(BF16)
                        to_owner(t).start()
                    else:
                        to_sibling(t).wait_recv()
                        total = acc[...] + rsib[t % 2].astype(F32)
                        for j in range(3):
                            to_owner(j).wait_recv()
                            total = total + rici[j].astype(F32)
                        g_ref[...] = total
                        to_owner(2).wait_send()
                        to_sibling(3).wait_send()
                        for q in range(len(peers)):
                            for a in range(n_small):
                                small_copy(q, a).wait_send()
                                small_copy(q, a, receive=True).wait_recv()
                        for cp in own_small:
                            cp.wait()

    blk = (D_MODEL, W_BLK)
    grid_spec = pltpu.PrefetchScalarGridSpec(
        num_scalar_prefetch=1, grid=(n_steps, nt),
        in_specs=[pl.BlockSpec((D_MODEL, ts), lambda k, i, o: (0, i)),
                  pl.BlockSpec((ts, W_BLK), lambda k, i, o: (i, _dp_block(o[k]))),
                  ANY, ANY, ANY],
        out_specs=[pl.BlockSpec(blk, lambda k, i, o: (0, 0)), ANY, ANY, ANY],
        scratch_shapes=[pltpu.VMEM(blk, F32), pltpu.VMEM((2,) + blk, BF16),
                        pltpu.VMEM((2,) + blk, BF16), pltpu.VMEM((3,) + blk, BF16),
                        pltpu.SemaphoreType.DMA((4,)), pltpu.SemaphoreType.DMA((4,)),
                        pltpu.SemaphoreType.DMA((3,)), pltpu.SemaphoreType.DMA((3,)),
                        pltpu.SemaphoreType.DMA((7, 3)), pltpu.SemaphoreType.DMA((7, 3)),
                        pltpu.SemaphoreType.DMA((3,))])
    return pl.pallas_call(
        body, name="w_in_grad_scatter", grid_spec=grid_spec,
        out_shape=[jax.ShapeDtypeStruct(blk, F32),
                   jax.ShapeDtypeStruct(d_proj.shape, BF16), jax.ShapeDtypeStruct(d_wo.shape, BF16),
                   jax.ShapeDtypeStruct((N_DEV,) + pack.shape, F32)],
        compiler_params=_params(2),
    )(order, ht, dP, d_proj, d_wo, pack)


def _adamw(w, g, m, v):
    m = ADAM_B1 * m + (1.0 - ADAM_B1) * g
    v = ADAM_B2 * v + (1.0 - ADAM_B2) * (g * g)
    m_hat = m / (1.0 - ADAM_B1 ** ADAM_STEP)
    v_hat = v / (1.0 - ADAM_B2 ** ADAM_STEP)
    delta = -ADAM_LR * (m_hat / (jnp.sqrt(v_hat) + ADAM_EPS) + ADAM_WD * w)
    return delta, m, v


def _sum_adamw(parts, w, m, v, name):
    R, C = w.shape
    n = parts.shape[0]
    tr = min(R, 256)

    def body(p_ref, w_ref, m_ref, v_ref, g_ref, d_ref, nm_ref, nv_ref):
        g = p_ref[0].astype(F32)
        for s in range(1, n):
            g = g + p_ref[s].astype(F32)
        g_ref[...] = g
        d_ref[...], nm_ref[...], nv_ref[...] = _adamw(w_ref[...], g, m_ref[...], v_ref[...])

    tile = pl.BlockSpec((tr, C), lambda i: (i, 0))
    return pl.pallas_call(
        body, name=name, grid=(R // tr,),
        out_shape=[jax.ShapeDtypeStruct((R, C), F32)] * 4,
        in_specs=[pl.BlockSpec((n, tr, C), lambda i: (0, i, 0)), tile, tile, tile],
        out_specs=[tile] * 4,
        compiler_params=_params(1),
    )(parts, w, m, v)


def _adamw_mid(r_proj, r_wo, params):
    def body(rp_ref, rw_ref, *refs):
        ins, outs = refs[:9], refs[9:]

        def total(part):
            g = part(0).astype(F32)
            for s in range(1, N_DEV):
                g = g + part(s).astype(F32)
            return g

        grads = (total(lambda s: rp_ref[s, :, 0:128]), total(lambda s: rp_ref[s, :, 128:256]),
                 total(lambda s: rw_ref[s]))
        for n, g in enumerate(grads):
            w, m, v = (r[...] for r in ins[3 * n:3 * n + 3])
            outs[4 * n][...] = g
            outs[4 * n + 1][...], outs[4 * n + 2][...], outs[4 * n + 3][...] = _adamw(w, g, m, v)

    return pl.pallas_call(
        body, name="adamw_mid",
        out_shape=[jax.ShapeDtypeStruct(params[3 * n].shape, F32) for n in range(3) for _ in range(4)],
        compiler_params=pltpu.CompilerParams(vmem_limit_bytes=VMEM_LIMIT),
    )(r_proj, r_wo, *params)


def _adamw_small(r_pack, params):
    wide = 384

    def body(p_ref, *refs):
        ins, loss_ref, outs = refs[:15], refs[15], refs[16:]
        tot = p_ref[0]
        for s in range(1, N_DEV):
            tot = tot + p_ref[s]
        me = _flat(_mesh_pos())
        loss_ref[...] = jnp.sum(tot[2:3, :], axis=-1, keepdims=True)
        mine = pltpu.roll(tot[0:8, 0:D_CONV], (D_CONV - 64 * me) % D_CONV, 1)
        col = lax.broadcasted_iota(jnp.int32, (D_MODEL, wide), 0)
        idx = lax.broadcasted_iota(jnp.int32, (D_MODEL, wide), 1)
        near = (idx > MAX_REL - CHUNK) & (idx < 2 * MAX_REL) & (col == PADK + MAX_REL - idx)
        far = (idx == 2 * MAX_REL) & (col == D_MODEL - 1)
        perm = jnp.where(near | far, 1.0, 0.0).astype(F32)
        g_rel = jnp.dot(tot[8:16], perm, precision=lax.Precision.HIGHEST, preferred_element_type=F32)
        grads = (tot[0:1], tot[1:2], mine[3:6, 0:64], tot[6:7, 0:D_CONV], g_rel[:, 0:N_REL])
        for n, g in enumerate(grads):
            w, m, v = (r[...] for r in ins[3 * n:3 * n + 3])
            outs[4 * n][...] = g
            outs[4 * n + 1][...], outs[4 * n + 2][...], outs[4 * n + 3][...] = _adamw(w, g, m, v)

    return pl.pallas_call(
        body, name="adamw_small",
        out_shape=[jax.ShapeDtypeStruct((1, 1), F32)]
        + [jax.ShapeDtypeStruct(params[3 * n].shape, F32) for n in range(5) for _ in range(4)],
    )(r_pack, *params)


def _pad_row(a, width=D_MODEL):
    a = a.reshape(-1, a.shape[-1])
    return jnp.pad(a, ((0, 0), (0, width - a.shape[-1])))


def kernel(x, norm_g, w_in, rel_bias, w_att_out, conv_w, conv_b, w_conv_out, w_out, final_norm_g, loss_target, m_norm_g, m_w_in, m_rel_bias, m_w_att_out, m_conv_w, m_conv_b, m_w_conv_out, m_w_out, m_final_norm_g, v_norm_g, v_w_in, v_rel_bias, v_w_att_out, v_conv_w, v_conv_b, v_w_conv_out, v_w_out, v_final_norm_g):
    S = x.shape[1]
    x2d = x.reshape(S, D_MODEL)
    tgt = loss_target.reshape(S, D_MODEL)
    me = 4 * lax.axis_index("x") + 2 * lax.axis_index("y") + lax.axis_index("c")
    row = lambda a: a.reshape(1, D_MODEL)

    proj_sh = jnp.concatenate([w_att_out[0], w_conv_out[0]], axis=1).astype(BF16)
    cw_sh = jnp.pad(conv_w[0], ((0, 5), (0, 64)))
    P, ht, w_in_g, proj_g, w_out_g, cw_g = _gather_in_proj(
        x2d, norm_g, w_in[0].astype(BF16), [proj_sh, w_out[0].astype(BF16), cw_sh],
        me ^ _by_core(GATHER_MASKS))

    bias_tab = _bias_table(rel_bias[0])
    att, ex, rinv = _attn_fwd(P, bias_tab)
    dx2, dP, datt, d_wo, d_proj, sm1, sm2 = _token_local(
        x2d, tgt, P, att, proj_g, w_out_g.reshape(D_MODEL, D_MODEL), cw_g, conv_b, row(final_norm_g))
    dP, dbias = _attn_bwd(P, att, datt, ex, rinv, dP)
    grad_x, dnorm = _in_proj_bwd(x2d, norm_g, dx2, dP, w_in_g)

    pack = jnp.concatenate([dnorm[0:1], sm1[0:2], _pad_row(sm2[0:4]), jnp.zeros((1, D_MODEL), F32), dbias],
                           axis=0)
    g_win_sum, r_proj, r_wo, r_pack = _w_in_grad_scatter(
        ht, dP, d_proj, d_wo.reshape(N_DEV, 128, D_MODEL), pack, me ^ _by_core(SCATTER_MASKS))

    res = {"w_in": _sum_adamw(g_win_sum[None], w_in[0], m_w_in[0], v_w_in[0], "adamw_w_in")}
    mid = _adamw_mid(r_proj, r_wo, (w_att_out[0], m_w_att_out[0], v_w_att_out[0],
                                    w_conv_out[0], m_w_conv_out[0], v_w_conv_out[0],
                                    w_out[0], m_w_out[0], v_w_out[0]))
    for n, name in enumerate(("w_att_out", "w_conv_out", "w_out")):
        res[name] = mid[4 * n:4 * n + 4]
    small = _adamw_small(r_pack, (norm_g, m_norm_g, v_norm_g,
                                  row(final_norm_g), row(m_final_norm_g), row(v_final_norm_g),
                                  conv_w[0], m_conv_w[0], v_conv_w[0], conv_b, m_conv_b, v_conv_b,
                                  rel_bias[0], m_rel_bias[0], v_rel_bias[0]))
    loss = small[0].reshape(())
    for n, name in enumerate(("norm_g", "final_norm_g", "conv_w", "conv_b", "rel_bias")):
        res[name] = small[1 + 4 * n:5 + 4 * n]

    leading = {"norm_g": (1, D_MODEL), "final_norm_g": (D_MODEL,), "conv_b": (1, D_CONV)}
    outs = []
    for kind in range(4):
        for name in ("norm_g", "w_in", "rel_bias", "w_att_out", "conv_w", "conv_b", "w_conv_out", "w_out",
                     "final_norm_g"):
            a = res[name][kind]
            outs.append(a.reshape(leading[name]) if name in leading else a[None])
    return (loss, grad_x.reshape(1, S, D_MODEL), *outs)
```

```python
import functools

import numpy as np
import jax
import jax.numpy as jnp
from jax import lax
from jax.experimental import pallas as pl
from jax.experimental.pallas import tpu as pltpu

F32 = jnp.float32
BF16 = jnp.bfloat16

D_MODEL = 1024
CHUNK = 64
N_LEFT = 8
HEADS = 8
D_ATT = 512
D_CONV = 512
MAX_REL = 128
N_REL = 2 * MAX_REL + 1
IN_COLS = 6144
EPS = 1e-6
NEG_BIG = -1e30
N_DEV = 8
W_BLK = IN_COLS // N_DEV
QB = 4 * CHUNK
KB = QB + N_LEFT * CHUNK
PADK = N_LEFT * CHUNK
SCALE = 64 ** -0.5
LOG2E = 1.4426950408889634
GATE_COLS = IN_COLS - 3 * D_ATT

ADAM_LR = 0.001
ADAM_B1 = 0.9
ADAM_B2 = 0.999
ADAM_EPS = 1e-08
ADAM_WD = 0.01
ADAM_STEP = 10

VMEM_LIMIT = 56 * 1024 * 1024

MESH = pl.DeviceIdType.MESH
ANY = pl.BlockSpec(memory_space=pl.ANY)


def _params(n_grid, vmem_limit=VMEM_LIMIT):
    return pltpu.CompilerParams(dimension_semantics=("arbitrary",) * n_grid,
                                vmem_limit_bytes=vmem_limit)


def _dot(a, b):
    return jnp.dot(a, b, preferred_element_type=F32)


def _dot_nt(a, b):
    return lax.dot_general(a, b, (((1,), (1,)), ((), ())), preferred_element_type=F32)


def _dot_tn(a, b):
    return lax.dot_general(a, b, (((0,), (0,)), ((), ())), preferred_element_type=F32)


def _sigmoid(z):
    return 0.5 * jnp.tanh(0.5 * z) + 0.5


def _mesh_pos():
    return lax.axis_index("x"), lax.axis_index("y"), lax.axis_index("c")


def _flat(p):
    return 4 * p[0] + 2 * p[1] + p[2]


def _by_core(masks):
    m0, m1 = (jnp.array(m, jnp.int32) for m in masks)
    return jnp.where(lax.axis_index("c") == 0, m0, m1)


GATHER_MASKS = ((0, 1, 4, 3, 2, 5, 6, 7), (0, 1, 2, 5, 4, 3, 6, 7))


def _gather_in_proj(x, norm_g, w_sh, smalls, order):
    S = x.shape[0]
    ts = 1024
    nt = S // ts
    n_small = len(smalls)
    n_steps = N_DEV

    def body(order_ref, x_ref, g_ref, w_hbm, *rest):
        small_in = rest[:n_small]
        p_ref, ht_ref, wg_hbm = rest[n_small:n_small + 3]
        small_out = rest[n_small + 3:2 * n_small + 3]
        (wbuf, hbuf, own_sem, send_sems, recv_sems, out_sems,
         small_send, small_recv, small_local) = rest[2 * n_small + 3:]
        k, i = pl.program_id(0), pl.program_id(1)
        x_, y_, c_ = _mesh_pos()
        me, sibling = (x_, y_, c_), (x_, y_, 1 - c_)
        my = _flat(me)
        chips = [(x_ ^ (1 - c_), y_ ^ c_), (x_ ^ c_, y_ ^ (1 - c_)), (1 - x_, 1 - y_)]
        peers = [sibling] + [(*chip, c_) for chip in chips] + [(*chip, 1 - c_) for chip in chips]

        def wcopy(sem, block, to, from_input=False):
            dst = wbuf.at[_flat(block)]
            return pltpu.make_async_remote_copy(
                src_ref=w_hbm if from_input else dst, dst_ref=dst,
                send_sem=send_sems.at[sem], recv_sem=recv_sems.at[sem], device_id=to, device_id_type=MESH)

        def small_copy(q, a, receive=False):
            slot = _flat(peers[q]) if receive else my
            return pltpu.make_async_remote_copy(
                src_ref=small_in[a], dst_ref=small_out[a].at[slot],
                send_sem=small_send.at[q, a], recv_sem=small_recv.at[q, a],
                device_id=peers[q], device_id_type=MESH)

        def keep(step, block):
            col = pl.multiple_of(_dp_block(_flat(block)) * W_BLK, 128)
            return pltpu.make_async_copy(wbuf.at[_flat(block)], wg_hbm.at[:, pl.ds(col, W_BLK)], out_sems.at[step])

        own = pltpu.make_async_copy(w_hbm, wbuf.at[my], own_sem)
        small_own = [pltpu.make_async_copy(small_in[a], small_out[a].at[my], small_local.at[a])
                     for a in range(n_small)]
        passed_on = [(*chips[1], 1 - c_), (*chips[0], 1 - c_), (*chips[2], 1 - c_)]
        arrivals = [me, sibling]
        for j in range(3):
            arrivals += [(*chips[j], c_), passed_on[j]]

        @pl.when(i == 0)
        def _():
            for kk in range(n_steps):
                @pl.when(k == kk)
                def _():
                    j = kk // 2 - 1
                    if kk == 0:
                        own.start()
                        wcopy(0, me, sibling, True).start()
                        wcopy(1, me, (*chips[0], c_), True).start()
                        own.wait()
                    elif kk == 1:
                        wcopy(0, sibling, me).wait_recv()
                        wcopy(2, me, (*chips[1], c_), True).start()
                    elif kk % 2 == 0:
                        wcopy(1 + j, (*chips[j], c_), me).wait_recv()
                        wcopy(4 + j, (*chips[j], c_), sibling).start()
                        if kk == 2:
                            wcopy(3, me, (*chips[2], c_), True).start()
                    else:
                        wcopy(4 + j, passed_on[j], me).wait_recv()
                        if kk == 3:
                            for cp in small_own:
                                cp.start()
                            for q in range(len(peers)):
                                for a in range(n_small):
                                    small_copy(q, a).start()
                    keep(kk, arrivals[kk]).start()

        row0 = pl.multiple_of(i * ts, ts)

        @pl.when(k == 0)
        def _():
            xf = x_ref[...]
            r = lax.rsqrt(jnp.mean(xf * xf, axis=-1, keepdims=True) + EPS)
            hf = (xf * r) * g_ref[...]
            hbuf[pl.ds(row0, ts), :] = hf.astype(BF16)
            ht_ref[...] = hf.astype(BF16).T

        p_ref[...] = _dot(hbuf[pl.ds(row0, ts), :], wbuf[order_ref[k]]).astype(BF16)

        @pl.when((k == n_steps - 1) & (i == nt - 1))
        def _():
            wcopy(0, me, sibling, True).wait_send()
            for j, chip in enumerate(chips):
                wcopy(1 + j, me, (*chip, c_), True).wait_send()
                wcopy(4 + j, (*chip, c_), sibling).wait_send()
            for kk in range(n_steps):
                keep(kk, arrivals[kk]).wait()
            for cp in small_own:
                cp.wait()
            for q in range(len(peers)):
                for a in range(n_small):
                    small_copy(q, a).wait_send()
                    small_copy(q, a, receive=True).wait_recv()

    first_pass = lambda k, i: jnp.where(k == 0, i, nt - 1)
    grid_spec = pltpu.PrefetchScalarGridSpec(
        num_scalar_prefetch=1, grid=(n_steps, nt),
        in_specs=[pl.BlockSpec((ts, D_MODEL), lambda k, i, o: (first_pass(k, i), 0)),
                  pl.BlockSpec((1, D_MODEL), lambda k, i, o: (0, 0)), ANY] + [ANY] * n_small,
        out_specs=[pl.BlockSpec((ts, W_BLK), lambda k, i, o: (i, o[k])),
                   pl.BlockSpec((D_MODEL, ts), lambda k, i, o: (0, first_pass(k, i))), ANY] + [ANY] * n_small,
        scratch_shapes=[pltpu.VMEM((N_DEV, D_MODEL, W_BLK), BF16), pltpu.VMEM((S, D_MODEL), BF16),
                        pltpu.SemaphoreType.DMA, pltpu.SemaphoreType.DMA((7,)), pltpu.SemaphoreType.DMA((7,)),
                        pltpu.SemaphoreType.DMA((n_steps,)),
                        pltpu.SemaphoreType.DMA((7, n_small)), pltpu.SemaphoreType.DMA((7, n_small)),
                        pltpu.SemaphoreType.DMA((n_small,))])
    return pl.pallas_call(
        body, name="gather_in_proj", grid_spec=grid_spec,
        out_shape=[jax.ShapeDtypeStruct((S, IN_COLS), BF16), jax.ShapeDtypeStruct((D_MODEL, S), BF16),
                   jax.ShapeDtypeStruct((D_MODEL, IN_COLS), BF16)]
        + [jax.ShapeDtypeStruct((N_DEV,) + s.shape, s.dtype) for s in smalls],
        compiler_params=_params(2),
    )(order, x, norm_g, w_sh, *smalls)


def _bias_table(rel_bias):
    wide = 1024

    def body(r_ref, o_ref):
        h = pl.program_id(0)
        col = lax.broadcasted_iota(jnp.int32, (1, wide), 1)
        k_minus_q = jnp.where(col < KB, col, col - wide)
        idx = jnp.clip(PADK - k_minus_q, -MAX_REL, MAX_REL) + MAX_REL
        f = jnp.zeros((1, wide), F32)
        for r in range(MAX_REL - CHUNK + 1, N_REL):
            f = jnp.where(idx == r, r_ref[h, r], f)
        kcol = lax.broadcasted_iota(jnp.int32, (1, KB), 1)
        kc = kcol >> 6
        sub = lax.broadcasted_iota(jnp.int32, (8, 1), 0)
        f8 = jnp.broadcast_to(f * LOG2E, (8, wide))
        base = f8
        for r in range(1, 8):
            base = jnp.where(sub == r, pltpu.roll(f8, r, 1), base)
        for qh in range(QB // 8):
            rows = (pltpu.roll(base, 8 * qh, 1) if qh else base)[:, 0:KB]
            qc = (8 * qh) // CHUNK
            band = (kc >= qc) & (kc <= qc + N_LEFT)
            for t in range(3):
                o_ref[t, 0, 8 * qh:8 * qh + 8, :] = jnp.where(band & (kcol >= PADK - t * QB), rows, NEG_BIG)

    return pl.pallas_call(
        body, name="bias_table", grid=(HEADS,),
        out_shape=jax.ShapeDtypeStruct((3, HEADS, QB, KB), F32),
        in_specs=[pl.BlockSpec(memory_space=pltpu.SMEM)],
        out_specs=pl.BlockSpec((3, 1, QB, KB), lambda h: (0, h, 0, 0)),
        compiler_params=_params(1),
    )(rel_bias)


KEY_GROUP = 4


def _load_keys(g, nb, p_hbm, kp, vp, sem):
    rows = KEY_GROUP * QB
    n_groups = p_hbm.shape[0] // rows

    def copies(c):
        src = pl.ds(c * rows, rows)
        dst = pl.ds(PADK + c * rows, rows)
        return (pltpu.make_async_copy(p_hbm.at[src, D_ATT:2 * D_ATT], kp.at[dst, :], sem.at[0, c]),
                pltpu.make_async_copy(p_hbm.at[src, 2 * D_ATT:3 * D_ATT], vp.at[dst, :], sem.at[1, c]))

    @pl.when(g == 0)
    def _():
        kp[0:PADK, :] = jnp.zeros((PADK, D_ATT), BF16)
        vp[0:PADK, :] = jnp.zeros((PADK, D_ATT), BF16)
        for c in range(n_groups):
            for cp in copies(c):
                cp.start()

    @pl.when((g % KEY_GROUP == 0) & (g < nb))
    def _():
        for cp in copies(g // KEY_GROUP):
            cp.wait()


def _attn_fwd(P, bias_tab):
    S = P.shape[0]
    nb = S // QB

    def body(q_ref, p_hbm, bias_ref, o_ref, ex_ref, rinv_ref, kp, vp, sem):
        g = pl.program_id(0)
        _load_keys(g, nb, p_hbm, kp, vp, sem)
        start = pl.multiple_of(g * QB, QB)
        lane = lax.broadcasted_iota(jnp.int32, (1, 128), 1)
        half = lambda h: (lane < 64) if h % 2 == 0 else (lane >= 64)
        pair = lambda h: slice(128 * (h // 2), 128 * (h // 2 + 1))

        def scores(h):
            qp = q_ref[:, pair(h)] * SCALE
            qm = jnp.where(half(h), qp, jnp.zeros_like(qp))
            return (_dot_nt(qm, kp[pl.ds(start, KB), pair(h)]) * LOG2E + bias_ref[0, h]).astype(BF16)

        outs = []
        s_next = scores(0)
        for h in range(HEADS):
            s = s_next
            if h + 1 < HEADS:
                s_next = scores(h + 1)
            mx = jnp.max(s, axis=-1, keepdims=True)
            ex = jnp.exp2(s - mx)
            ex_ref[:, KB * h:KB * (h + 1)] = ex
            vpair = vp[pl.ds(start, KB), pair(h)]
            o = _dot(ex, jnp.where(half(h), vpair, jnp.ones_like(vpair)))
            rinv = 1.0 / pltpu.roll(o, 64, 1)
            outs.append(o * rinv)
            rinv_ref[:, h:h + 1] = rinv[:, 0:1] if h % 2 == 0 else 1.0 / o[:, 0:1]
            if h % 2 == 1:
                o_ref[:, pair(h)] = jnp.where(lane < 64, outs[h - 1], outs[h]).astype(BF16)

    return pl.pallas_call(
        body, name="attn_fwd", grid=(nb,),
        out_shape=[jax.ShapeDtypeStruct((S, D_ATT), BF16), jax.ShapeDtypeStruct((S, HEADS * KB), BF16),
                   jax.ShapeDtypeStruct((S, HEADS), F32)],
        in_specs=[pl.BlockSpec((QB, D_ATT), lambda g: (g, 0)), ANY,
                  pl.BlockSpec((1, HEADS, QB, KB), lambda g: (jnp.minimum(g, 2), 0, 0, 0))],
        out_specs=[pl.BlockSpec((QB, D_ATT), lambda g: (g, 0)),
                   pl.BlockSpec((QB, HEADS * KB), lambda g: (g, 0)),
                   pl.BlockSpec((QB, HEADS), lambda g: (g, 0))],
        scratch_shapes=[pltpu.VMEM((S + PADK, D_ATT), BF16), pltpu.VMEM((S + PADK, D_ATT), BF16),
                        pltpu.SemaphoreType.DMA((2, S // (KEY_GROUP * QB)))],
        compiler_params=_params(1),
    )(P, P, bias_tab)


def _token_local(x, tgt, P, att, proj_g, w_out, cw_g, conv_b, final_g):
    S = x.shape[0]
    ts = 256
    nt = S // ts
    hb = 16

    def body(x_ref, t_ref, s1_ref, s2_ref, s3_ref, h1_ref, h2_ref, att_ref,
             pg_ref, wo_ref, cwg_ref, cb_ref, g2_ref,
             dx2_ref, dg_ref, datt_ref, dwo_ref, dproj_ref, sm1_ref, sm2_ref,
             carry, wao_ref, wco_ref, cw_ref, dwo_acc, dwao_acc, dwco_acc):
        i = pl.program_id(0)
        t = nt - 1 - i

        @pl.when(i == 0)
        def _():
            dwo_acc[...] = jnp.zeros_like(dwo_acc)
            dwao_acc[...] = jnp.zeros_like(dwao_acc)
            dwco_acc[...] = jnp.zeros_like(dwco_acc)
            lane = lax.broadcasted_iota(jnp.int32, (1, 128), 1)
            for j in range(N_DEV):
                wao_ref[:, 128 * j:128 * (j + 1)] = pg_ref[j, :, 0:128]
                wco_ref[:, 128 * j:128 * (j + 1)] = pg_ref[j, :, 128:256]
            for p in range(N_DEV // 2):
                cw_ref[:, 128 * p:128 * (p + 1)] = jnp.where(
                    lane < 64, cwg_ref[2 * p], pltpu.roll(cwg_ref[2 * p + 1], 64, 1))
            sm1_ref[...] = jnp.zeros_like(sm1_ref)
            sm2_ref[...] = jnp.zeros_like(sm2_ref)
            carry[...] = jnp.zeros_like(carry)

        za = s1_ref[:, 0:512]
        gb = s1_ref[:, 512:1024]
        gc = s1_ref[:, 1024:1536].astype(F32)
        u = s2_ref[:, 0:512].astype(F32)
        zc = s2_ref[:, 512:1024]
        ga = jnp.concatenate([s2_ref[:, 1024:1536], s3_ref[:, 0:512]], axis=1)
        gv = s3_ref[:, 512:1536]
        att = att_ref[...]
        row = lax.broadcasted_iota(jnp.int32, (ts, 1), 0)

        sa = _sigmoid(za)
        silu_a = za * sa
        att_g = att * silu_a
        y_att = _dot(att_g, wao_ref[...])

        cu = gc * u
        keep = jnp.where(t > 0, 1.0, 0.0).astype(F32)
        hcu = (h1_ref[:, 1024:1536].astype(F32) * h2_ref[:, 0:512].astype(F32)) * keep
        cu_m1 = jnp.where(row == 0, hcu[hb - 1:hb, :], pltpu.roll(cu, 1, 0))
        cu_m2 = jnp.where(row == 0, hcu[hb - 2:hb - 1, :],
                          jnp.where(row == 1, hcu[hb - 1:hb, :], pltpu.roll(cu, 2, 0)))
        w0, w1, w2 = cw_ref[0:1, :], cw_ref[1:2, :], cw_ref[2:3, :]
        vconv = w0 * cu_m2 + w1 * cu_m1 + w2 * cu + cb_ref[...]
        vcb = vconv.astype(BF16)
        sc = _sigmoid(zc)
        silu_c = zc * sc
        cg = gb * vcb * silu_c
        sga = _sigmoid(ga)
        sgv = _sigmoid(gv)
        y_conv = _dot(cg, wco_ref[...])

        yab, ycb = y_att.astype(BF16), y_conv.astype(BF16)
        m = sga * yab + sgv * ycb
        x2 = x_ref[...] + _dot(m, wo_ref[...])
        r2 = lax.rsqrt(jnp.mean(x2 * x2, axis=-1, keepdims=True) + EPS)
        xn2 = x2 * r2
        g2 = g2_ref[...]
        err = xn2 * g2 - t_ref[...]
        sm1_ref[1:2, :] += jnp.sum(err * err, axis=0, keepdims=True) * (0.5 / D_MODEL)

        dy = err * (1.0 / D_MODEL)
        sm1_ref[0:1, :] += jnp.sum(dy * xn2, axis=0, keepdims=True)
        dxn = dy * g2
        dx2 = r2 * (dxn - xn2 * jnp.mean(dxn * xn2, axis=-1, keepdims=True))
        dx2_ref[...] = dx2
        dx2b = dx2.astype(BF16)
        dwo_acc[...] += _dot_tn(m, dx2b)
        dm = _dot_nt(dx2b, wo_ref[...])
        dmb = dm.astype(BF16)
        dya = dmb * sga
        dyc = dmb * sgv
        dg_ref[:, 2560:3584] = dmb * yab * (sga * (1.0 - sga))
        dg_ref[:, 3584:4608] = dmb * ycb * (sgv * (1.0 - sgv))
        dwao_acc[...] += _dot_tn(att_g, dya)
        dwco_acc[...] += _dot_tn(cg, dyc)
        datt_g = _dot_nt(dya, wao_ref[...])
        dcg = _dot_nt(dyc, wco_ref[...])
        dagb, dcgb = datt_g.astype(BF16), dcg.astype(BF16)
        datt_ref[...] = dagb * silu_a
        dg_ref[:, 0:512] = dagb * att * (sa + silu_a * (1.0 - sa))
        dg_ref[:, 512:1024] = dcgb * vcb * silu_c
        dg_ref[:, 2048:2560] = dcgb * gb * vcb * (sc + silu_c * (1.0 - sc))
        dv = dcg * (gb * silu_c).astype(F32)
        sm2_ref[3:4, :] += jnp.sum(dv, axis=0, keepdims=True)
        sm2_ref[0:1, :] += jnp.sum(dv * cu_m2, axis=0, keepdims=True)
        sm2_ref[1:2, :] += jnp.sum(dv * cu_m1, axis=0, keepdims=True)
        sm2_ref[2:3, :] += jnp.sum(dv * cu, axis=0, keepdims=True)
        nxt = carry[...]
        dv_p1 = jnp.where(row == ts - 1, nxt[0:1, :], pltpu.roll(dv, ts - 1, 0))
        dv_p2 = jnp.where(row == ts - 1, nxt[1:2, :],
                          jnp.where(row == ts - 2, nxt[0:1, :], pltpu.roll(dv, ts - 2, 0)))
        dcu = w2 * dv + w1 * dv_p1 + w0 * dv_p2
        carry[...] = dv[0:8, :]
        dg_ref[:, 1024:1536] = (dcu * u).astype(BF16)
        dg_ref[:, 1536:2048] = (dcu * gc).astype(BF16)

        @pl.when(i == nt - 1)
        def _():
            dwo_ref[...] = dwo_acc[...].astype(BF16)
            for j in range(N_DEV):
                dproj_ref[j, :, 0:128] = dwao_acc[:, 128 * j:128 * (j + 1)].astype(BF16)
                dproj_ref[j, :, 128:256] = dwco_acc[:, 128 * j:128 * (j + 1)].astype(BF16)

    tile = lambda w: pl.BlockSpec((ts, w), lambda i: (nt - 1 - i, 0))
    seg = lambda c: pl.BlockSpec((ts, 1536), lambda i: (nt - 1 - i, c))
    halo = lambda c: pl.BlockSpec((hb, 1536), lambda i: (jnp.maximum((nt - 1 - i) * (ts // hb) - 1, 0), c))
    full = lambda a: pl.BlockSpec(a.shape, lambda i: (0,) * a.ndim)
    acc = lambda r, c: pl.BlockSpec((r, c), lambda i: (0, 0))
    return pl.pallas_call(
        body, name="token_local", grid=(nt,),
        out_shape=[jax.ShapeDtypeStruct((S, D_MODEL), F32), jax.ShapeDtypeStruct((S, IN_COLS), BF16),
                   jax.ShapeDtypeStruct((S, D_ATT), BF16), jax.ShapeDtypeStruct((D_MODEL, D_MODEL), BF16),
                   jax.ShapeDtypeStruct(proj_g.shape, BF16),
                   jax.ShapeDtypeStruct((8, D_MODEL), F32), jax.ShapeDtypeStruct((8, D_CONV), F32)],
        in_specs=[tile(D_MODEL), tile(D_MODEL), seg(1), seg(2), seg(3), halo(1), halo(2), tile(D_ATT),
                  full(proj_g), full(w_out), full(cw_g), full(conv_b), full(final_g)],
        out_specs=[tile(D_MODEL), tile(GATE_COLS), tile(D_ATT), acc(D_MODEL, D_MODEL), full(proj_g),
                   acc(8, D_MODEL), acc(8, D_CONV)],
        scratch_shapes=[pltpu.VMEM((8, D_CONV), F32),
                        pltpu.VMEM((D_ATT, D_MODEL), BF16), pltpu.VMEM((D_CONV, D_MODEL), BF16),
                        pltpu.VMEM((8, D_CONV), F32), pltpu.VMEM((D_MODEL, D_MODEL), F32),
                        pltpu.VMEM((D_ATT, D_MODEL), F32), pltpu.VMEM((D_CONV, D_MODEL), F32)],
        compiler_params=_params(1),
    )(x, tgt, P, P, P, P, P, att, proj_g, w_out, cw_g, conv_b, final_g)


def _fold_diagonals(d_ref, o_ref):
    wide = D_MODEL
    sub = lax.broadcasted_iota(jnp.int32, (8, 1), 0)
    col = lax.broadcasted_iota(jnp.int32, (1, wide), 1)
    pad = jnp.zeros((8, wide - KB), F32)
    for h in range(HEADS):
        acc = jnp.concatenate([d_ref[h, 0:8, :], pad], axis=1)
        for qh in range(1, QB // 8):
            a = jnp.concatenate([d_ref[h, 8 * qh:8 * qh + 8, :], pad], axis=1)
            acc = acc + pltpu.roll(a, wide - 8 * qh, 1)
        for r in range(1, 8):
            acc = jnp.where(sub == r, pltpu.roll(acc, wide - r, 1), acc)
        vec = jnp.sum(acc, axis=0, keepdims=True)
        far = (col <= PADK - MAX_REL) | (col > KB)
        tail = jnp.sum(jnp.where(far, vec, 0.0), axis=-1, keepdims=True)
        o_ref[h:h + 1, :] = jnp.where(col == wide - 1, tail, vec)


def _attn_bwd(P, att, datt, ex, rinv, dP):
    S = P.shape[0]
    nb = S // QB

    def body(q_ref, att_ref, datt_ref, ex_ref, rinv_ref, p_hbm, dp_hbm, out_ref, dbias_ref,
             kp, vp, dq_ring, dk_ring, dv_ring, db_ref, sem):
        g = pl.program_id(0)

        _load_keys(g, nb, p_hbm, kp, vp, sem)

        @pl.when(g == 0)
        def _():
            db_ref[...] = jnp.zeros_like(db_ref)
            dk_ring[...] = jnp.zeros_like(dk_ring)
            dv_ring[...] = jnp.zeros_like(dv_ring)

        s_new = g % 3
        s_mid = (g + 2) % 3
        s_old = (g + 1) % 3

        @pl.when(g < nb)
        def _():
            start = pl.multiple_of(g * QB, QB)
            lane = lax.broadcasted_iota(jnp.int32, (1, 128), 1)
            for p in range(HEADS // 2):
                cols = slice(128 * p, 128 * (p + 1))
                qp = q_ref[:, cols] * SCALE
                op = att_ref[:, cols].astype(F32)
                dop = datt_ref[:, cols]
                kpair = kp[pl.ds(start, KB), cols]
                vpair = vp[pl.ds(start, KB), cols]
                dqs = []
                dk_acc = jnp.zeros((KB, 128), F32)
                dv_acc = jnp.zeros((KB, 128), F32)
                for e in range(2):
                    h = 2 * p + e
                    lm = (lane < 64) if e == 0 else (lane >= 64)
                    qm = jnp.where(lm, qp, jnp.zeros_like(qp))
                    dom = jnp.where(lm, dop, jnp.zeros_like(dop))
                    exh = ex_ref[:, KB * h:KB * (h + 1)]
                    rinv = rinv_ref[:, h:h + 1]
                    domf = dom.astype(F32)
                    dp = _dot_nt(dom, vpair)
                    delta = jnp.sum(domf * op, axis=-1, keepdims=True)
                    dsb = exh * ((dp - delta) * rinv).astype(BF16)
                    db_ref[h] += dsb.astype(F32)
                    dqs.append(_dot(dsb, kpair) * SCALE)
                    dk_acc = dk_acc + _dot_tn(dsb, qm)
                    dv_acc = dv_acc + _dot_tn(exh, (domf * rinv).astype(BF16))
                dq_ring[s_new, :, cols] = jnp.where(lane < 64, dqs[0], dqs[1])
                dk_ring[s_old, :, cols] += dk_acc[0:QB]
                dk_ring[s_mid, :, cols] += dk_acc[QB:2 * QB]
                dk_ring[s_new, :, cols] = dk_acc[2 * QB:3 * QB]
                dv_ring[s_old, :, cols] += dv_acc[0:QB]
                dv_ring[s_mid, :, cols] += dv_acc[QB:2 * QB]
                dv_ring[s_new, :, cols] = dv_acc[2 * QB:3 * QB]

        @pl.when(g >= 2)
        def _():
            out_ref[:, 0:D_ATT] = dq_ring[s_old].astype(BF16)
            out_ref[:, D_ATT:2 * D_ATT] = dk_ring[s_old].astype(BF16)
            out_ref[:, 2 * D_ATT:3 * D_ATT] = dv_ring[s_old].astype(BF16)

        @pl.when(g == nb + 1)
        def _():
            _fold_diagonals(db_ref, dbias_ref)

    qblk = lambda w: pl.BlockSpec((QB, w), lambda g: (jnp.minimum(g, nb - 1), 0))
    return pl.pallas_call(
        body, name="attn_bwd", grid=(nb + 2,),
        out_shape=[jax.ShapeDtypeStruct((S, IN_COLS), BF16), jax.ShapeDtypeStruct((HEADS, D_MODEL), F32)],
        in_specs=[qblk(D_ATT), qblk(D_ATT), qblk(D_ATT), qblk(HEADS * KB), qblk(HEADS), ANY, ANY],
        out_specs=[pl.BlockSpec((QB, 3 * D_ATT), lambda g: (jnp.maximum(g - 2, 0), GATE_COLS // (3 * D_ATT))),
                   pl.BlockSpec((HEADS, D_MODEL), lambda g: (0, 0))],
        input_output_aliases={6: 0},
        scratch_shapes=[pltpu.VMEM((S + PADK, D_ATT), BF16), pltpu.VMEM((S + PADK, D_ATT), BF16),
                        pltpu.VMEM((3, QB, D_ATT), F32), pltpu.VMEM((3, QB, D_ATT), F32),
                        pltpu.VMEM((3, QB, D_ATT), F32), pltpu.VMEM((HEADS, QB, KB), F32),
                        pltpu.SemaphoreType.DMA((2, S // (KEY_GROUP * QB)))],
        compiler_params=_params(1),
    )(P, att, datt, ex, rinv, P, dP)


def _dp_block(j):
    return (j + GATE_COLS // W_BLK) % N_DEV


def _in_proj_bwd(x, norm_g, dx2, dP, w_in_g):
    S = x.shape[0]
    ts = 512

    def body(x_ref, g_ref, dx2_ref, dp_ref, w_ref, gx_ref, dn_ref):
        @pl.when(pl.program_id(0) == 0)
        def _():
            dn_ref[...] = jnp.zeros_like(dn_ref)

        dh = _dot_nt(dp_ref[...], w_ref[...])
        xf = x_ref[...]
        r = lax.rsqrt(jnp.mean(xf * xf, axis=-1, keepdims=True) + EPS)
        xn = xf * r
        dn_ref[0:1, :] += jnp.sum(dh * xn, axis=0, keepdims=True)
        dhg = dh * g_ref[...]
        gx_ref[...] = dx2_ref[...] + r * (dhg - xn * jnp.mean(dhg * xn, axis=-1, keepdims=True))

    tile = lambda w: pl.BlockSpec((ts, w), lambda i: (i, 0))
    return pl.pallas_call(
        body, name="in_proj_bwd", grid=(S // ts,),
        out_shape=[jax.ShapeDtypeStruct((S, D_MODEL), F32), jax.ShapeDtypeStruct((8, D_MODEL), F32)],
        in_specs=[tile(D_MODEL), pl.BlockSpec((1, D_MODEL), lambda i: (0, 0)), tile(D_MODEL),
                  tile(IN_COLS),
                  pl.BlockSpec((D_MODEL, IN_COLS), lambda i: (0, 0))],
        out_specs=[tile(D_MODEL), pl.BlockSpec((8, D_MODEL), lambda i: (0, 0))],
        compiler_params=_params(1),
    )(x, norm_g, dx2, dP, w_in_g)


SCATTER_MASKS = ((3, 4, 5, 2, 7, 6, 1, 0), (5, 2, 3, 4, 7, 6, 1, 0))


def _w_in_grad_scatter(ht, dP, d_proj, d_wo, pack, order):
    S = ht.shape[1]
    ts = min(S, 2048)
    nt = S // ts
    n_steps = 8

    def body(order_ref, ht_ref, d_ref, proj_hbm, wo_hbm, pack_hbm, g_ref, rproj, rwo, rpack,
             acc, stage, rsib, rici, d2d_send, d2d_recv, ici_send, ici_recv, small_send, small_recv, local_sems):
        k, i = pl.program_id(0), pl.program_id(1)
        x, y, c = _mesh_pos()
        my = _flat((x, y, c))
        sibling = (x, y, 1 - c)
        owners = [(x ^ (1 - c), y ^ c, c), (x ^ c, y ^ (1 - c), c), (1 - x, 1 - y, c)]
        peers = [sibling, (1 - x, y, c), (x, 1 - y, c), (1 - x, 1 - y, c),
                 (1 - x, y, 1 - c), (x, 1 - y, 1 - c), (1 - x, 1 - y, 1 - c)]
        small = ((proj_hbm, rproj, True), (wo_hbm, rwo, True), (pack_hbm, rpack, False))
        n_small = len(small)

        def small_copy(kk, a, receive=False):
            src, dst, per_peer = small[a]
            slot = _flat(peers[kk]) if receive else my
            return pltpu.make_async_remote_copy(
                src_ref=src.at[_flat(peers[kk])] if per_peer else src, dst_ref=dst.at[slot],
                send_sem=small_send.at[kk, a], recv_sem=small_recv.at[kk, a],
                device_id=peers[kk], device_id_type=MESH)

        def to_sibling(t):
            return pltpu.make_async_remote_copy(
                src_ref=stage.at[0], dst_ref=rsib.at[t % 2], send_sem=d2d_send.at[t], recv_sem=d2d_recv.at[t],
                device_id=sibling, device_id_type=MESH)

        def to_owner(t):
            return pltpu.make_async_remote_copy(
                src_ref=stage.at[1], dst_ref=rici.at[t], send_sem=ici_send.at[t], recv_sem=ici_recv.at[t],
                device_id=owners[t], device_id_type=MESH)

        own_small = [pltpu.make_async_copy(src.at[my] if per_peer else src, dst.at[my], local_sems.at[a])
                     for a, (src, dst, per_peer) in enumerate(small)]

        @pl.when((k == 0) & (i == 0))
        def _():
            for cp in own_small:
                cp.start()
            for kk in range(len(peers)):
                for a in range(n_small):
                    small_copy(kk, a).start()

        @pl.when(i == 0)
        def _():
            acc[...] = jnp.zeros_like(acc)

        acc[...] += _dot(ht_ref[...], d_ref[...])

        @pl.when(i == nt - 1)
        def _():
            for s in range(n_steps):
                @pl.when(k == s)
                def _():
                    t = s // 2
                    if s % 2 == 0:
                        if t >= 1:
                            to_sibling(t - 1).wait_send()
                        stage[0] = acc[...].astype(BF16)
                        to_sibling(t).start()
                    elif t < 3:
                        if t >= 1:
                            to_owner(t - 1).wait_send()
                        to_sibling(t).wait_recv()
                        stage[1] = (acc[...] + rsib[t % 2].astype(F32)).astype(BF16)
                        to_owner(t).start()
                    else:
                        to_sibling(t).wait_recv()
                        total = acc[...] + rsib[t % 2].astype(F32)
                        for j in range(3):
                            to_owner(j).wait_recv()
                            total = total + rici[j].astype(F32)
                        g_ref[...] = total
                        to_owner(2).wait_send()
                        to_sibling(3).wait_send()
                        for q in range(len(peers)):
                            for a in range(n_small):
                                small_copy(q, a).wait_send()
                                small_copy(q, a, receive=True).wait_recv()
                        for cp in own_small:
                            cp.wait()

    blk = (D_MODEL, W_BLK)
    grid_spec = pltpu.PrefetchScalarGridSpec(
        num_scalar_prefetch=1, grid=(n_steps, nt),
        in_specs=[pl.BlockSpec((D_MODEL, ts), lambda k, i, o: (0, i)),
                  pl.BlockSpec((ts, W_BLK), lambda k, i, o: (i, _dp_block(o[k]))),
                  ANY, ANY, ANY],
        out_specs=[pl.BlockSpec(blk, lambda k, i, o: (0, 0)), ANY, ANY, ANY],
        scratch_shapes=[pltpu.VMEM(blk, F32), pltpu.VMEM((2,) + blk, BF16),
                        pltpu.VMEM((2,) + blk, BF16), pltpu.VMEM((3,) + blk, BF16),
                        pltpu.SemaphoreType.DMA((4,)), pltpu.SemaphoreType.DMA((4,)),
                        pltpu.SemaphoreType.DMA((3,)), pltpu.SemaphoreType.DMA((3,)),
                        pltpu.SemaphoreType.DMA((7, 3)), pltpu.SemaphoreType.DMA((7, 3)),
                        pltpu.SemaphoreType.DMA((3,))])
    return pl.pallas_call(
        body, name="w_in_grad_scatter", grid_spec=grid_spec,
        out_shape=[jax.ShapeDtypeStruct(blk, F32),
                   jax.ShapeDtypeStruct(d_proj.shape, BF16), jax.ShapeDtypeStruct(d_wo.shape, BF16),
                   jax.ShapeDtypeStruct((N_DEV,) + pack.shape, F32)],
        compiler_params=_params(2),
    )(order, ht, dP, d_proj, d_wo, pack)


def _adamw(w, g, m, v):
    m = ADAM_B1 * m + (1.0 - ADAM_B1) * g
    v = ADAM_B2 * v + (1.0 - ADAM_B2) * (g * g)
    m_hat = m / (1.0 - ADAM_B1 ** ADAM_STEP)
    v_hat = v / (1.0 - ADAM_B2 ** ADAM_STEP)
    delta = -ADAM_LR * (m_hat / (jnp.sqrt(v_hat) + ADAM_EPS) + ADAM_WD * w)
    return delta, m, v


def _sum_adamw(parts, w, m, v, name):
    R, C = w.shape
    n = parts.shape[0]
    tr = min(R, 256)

    def body(p_ref, w_ref, m_ref, v_ref, g_ref, d_ref, nm_ref, nv_ref):
        g = p_ref[0].astype(F32)
        for s in range(1, n):
            g = g + p_ref[s].astype(F32)
        g_ref[...] = g
        d_ref[...], nm_ref[...], nv_ref[...] = _adamw(w_ref[...], g, m_ref[...], v_ref[...])

    tile = pl.BlockSpec((tr, C), lambda i: (i, 0))
    return pl.pallas_call(
        body, name=name, grid=(R // tr,),
        out_shape=[jax.ShapeDtypeStruct((R, C), F32)] * 4,
        in_specs=[pl.BlockSpec((n, tr, C), lambda i: (0, i, 0)), tile, tile, tile],
        out_specs=[tile] * 4,
        compiler_params=_params(1),
    )(parts, w, m, v)


def _adamw_mid(r_proj, r_wo, params):
    def body(rp_ref, rw_ref, *refs):
        ins, outs = refs[:9], refs[9:]

        def total(part):
            g = part(0).astype(F32)
            for s in range(1, N_DEV):
                g = g + part(s).astype(F32)
            return g

        grads = (total(lambda s: rp_ref[s, :, 0:128]), total(lambda s: rp_ref[s, :, 128:256]),
                 total(lambda s: rw_ref[s]))
        for n, g in enumerate(grads):
            w, m, v = (r[...] for r in ins[3 * n:3 * n + 3])
            outs[4 * n][...] = g
            outs[4 * n + 1][...], outs[4 * n + 2][...], outs[4 * n + 3][...] = _adamw(w, g, m, v)

    return pl.pallas_call(
        body, name="adamw_mid",
        out_shape=[jax.ShapeDtypeStruct(params[3 * n].shape, F32) for n in range(3) for _ in range(4)],
        compiler_params=pltpu.CompilerParams(vmem_limit_bytes=VMEM_LIMIT),
    )(r_proj, r_wo, *params)


def _adamw_small(r_pack, params):
    wide = 384

    def body(p_ref, *refs):
        ins, loss_ref, outs = refs[:15], refs[15], refs[16:]
        tot = p_ref[0]
        for s in range(1, N_DEV):
            tot = tot + p_ref[s]
        me = _flat(_mesh_pos())
        loss_ref[...] = jnp.sum(tot[2:3, :], axis=-1, keepdims=True)
        mine = pltpu.roll(tot[0:8, 0:D_CONV], (D_CONV - 64 * me) % D_CONV, 1)
        col = lax.broadcasted_iota(jnp.int32, (D_MODEL, wide), 0)
        idx = lax.broadcasted_iota(jnp.int32, (D_MODEL, wide), 1)
        near = (idx > MAX_REL - CHUNK) & (idx < 2 * MAX_REL) & (col == PADK + MAX_REL - idx)
        far = (idx == 2 * MAX_REL) & (col == D_MODEL - 1)
        perm = jnp.where(near | far, 1.0, 0.0).astype(F32)
        g_rel = jnp.dot(tot[8:16], perm, precision=lax.Precision.HIGHEST, preferred_element_type=F32)
        grads = (tot[0:1], tot[1:2], mine[3:6, 0:64], tot[6:7, 0:D_CONV], g_rel[:, 0:N_REL])
        for n, g in enumerate(grads):
            w, m, v = (r[...] for r in ins[3 * n:3 * n + 3])
            outs[4 * n][...] = g
            outs[4 * n + 1][...], outs[4 * n + 2][...], outs[4 * n + 3][...] = _adamw(w, g, m, v)

    return pl.pallas_call(
        body, name="adamw_small",
        out_shape=[jax.ShapeDtypeStruct((1, 1), F32)]
        + [jax.ShapeDtypeStruct(params[3 * n].shape, F32) for n in range(5) for _ in range(4)],
    )(r_pack, *params)


def _pad_row(a, width=D_MODEL):
    a = a.reshape(-1, a.shape[-1])
    return jnp.pad(a, ((0, 0), (0, width - a.shape[-1])))


def kernel(x, norm_g, w_in, rel_bias, w_att_out, conv_w, conv_b, w_conv_out, w_out, final_norm_g, loss_target, m_norm_g, m_w_in, m_rel_bias, m_w_att_out, m_conv_w, m_conv_b, m_w_conv_out, m_w_out, m_final_norm_g, v_norm_g, v_w_in, v_rel_bias, v_w_att_out, v_conv_w, v_conv_b, v_w_conv_out, v_w_out, v_final_norm_g):
    S = x.shape[1]
    x2d = x.reshape(S, D_MODEL)
    tgt = loss_target.reshape(S, D_MODEL)
    me = 4 * lax.axis_index("x") + 2 * lax.axis_index("y") + lax.axis_index("c")
    row = lambda a: a.reshape(1, D_MODEL)

    proj_sh = jnp.concatenate([w_att_out[0], w_conv_out[0]], axis=1).astype(BF16)
    cw_sh = jnp.pad(conv_w[0], ((0, 5), (0, 64)))
    P, ht, w_in_g, proj_g, w_out_g, cw_g = _gather_in_proj(
        x2d, norm_g, w_in[0].astype(BF16), [proj_sh, w_out[0].astype(BF16), cw_sh],
        me ^ _by_core(GATHER_MASKS))

    bias_tab = _bias_table(rel_bias[0])
    att, ex, rinv = _attn_fwd(P, bias_tab)
    dx2, dP, datt, d_wo, d_proj, sm1, sm2 = _token_local(
        x2d, tgt, P, att, proj_g, w_out_g.reshape(D_MODEL, D_MODEL), cw_g, conv_b, row(final_norm_g))
    dP, dbias = _attn_bwd(P, att, datt, ex, rinv, dP)
    grad_x, dnorm = _in_proj_bwd(x2d, norm_g, dx2, dP, w_in_g)

    pack = jnp.concatenate([dnorm[0:1], sm1[0:2], _pad_row(sm2[0:4]), jnp.zeros((1, D_MODEL), F32), dbias],
                           axis=0)
    g_win_sum, r_proj, r_wo, r_pack = _w_in_grad_scatter(
        ht, dP, d_proj, d_wo.reshape(N_DEV, 128, D_MODEL), pack, me ^ _by_core(SCATTER_MASKS))

    res = {"w_in": _sum_adamw(g_win_sum[None], w_in[0], m_w_in[0], v_w_in[0], "adamw_w_in")}
    mid = _adamw_mid(r_proj, r_wo, (w_att_out[0], m_w_att_out[0], v_w_att_out[0],
                                    w_conv_out[0], m_w_conv_out[0], v_w_conv_out[0],
                                    w_out[0], m_w_out[0], v_w_out[0]))
    for n, name in enumerate(("w_att_out", "w_conv_out", "w_out")):
        res[name] = mid[4 * n:4 * n + 4]
    small = _adamw_small(r_pack, (norm_g, m_norm_g, v_norm_g,
                                  row(final_norm_g), row(m_final_norm_g), row(v_final_norm_g),
                                  conv_w[0], m_conv_w[0], v_conv_w[0], conv_b, m_conv_b, v_conv_b,
                                  rel_bias[0], m_rel_bias[0], v_rel_bias[0]))
    loss = small[0].reshape(())
    for n, name in enumerate(("norm_g", "final_norm_g", "conv_w", "conv_b", "rel_bias")):
        res[name] = small[1 + 4 * n:5 + 4 * n]

    leading = {"norm_g": (1, D_MODEL), "final_norm_g": (D_MODEL,), "conv_b": (1, D_CONV)}
    outs = []
    for kind in range(4):
        for name in ("norm_g", "w_in", "rel_bias", "w_att_out", "conv_w", "conv_b", "w_conv_out", "w_out",
                     "final_norm_g"):
            a = res[name][kind]
            outs.append(a.reshape(leading[name]) if name in leading else a[None])
    return (loss, grad_x.reshape(1, S, D_MODEL), *outs)
```

```python
import functools

import numpy as np
import jax
import jax.numpy as jnp
from jax import lax
from jax.experimental import pallas as pl
from jax.experimental.pallas import tpu as pltpu

F32 = jnp.float32
BF16 = jnp.bfloat16

D_MODEL = 1024
CHUNK = 64
N_LEFT = 8
HEADS = 8
D_ATT = 512
D_CONV = 512
MAX_REL = 128
N_REL = 2 * MAX_REL + 1
IN_COLS = 6144
EPS = 1e-6
NEG_BIG = -1e30
N_DEV = 8
W_BLK = IN_COLS // N_DEV
QB = 4 * CHUNK
KB = QB + N_LEFT * CHUNK
PADK = N_LEFT * CHUNK
SCALE = 64 ** -0.5
LOG2E = 1.4426950408889634
GATE_COLS = IN_COLS - 3 * D_ATT

ADAM_LR = 0.001
ADAM_B1 = 0.9
ADAM_B2 = 0.999
ADAM_EPS = 1e-08
ADAM_WD = 0.01
ADAM_STEP = 10

VMEM_LIMIT = 56 * 1024 * 1024

MESH = pl.DeviceIdType.MESH
ANY = pl.BlockSpec(memory_space=pl.ANY)


def _params(n_grid, vmem_limit=VMEM_LIMIT):
    return pltpu.CompilerParams(dimension_semantics=("arbitrary",) * n_grid,
                                vmem_limit_bytes=vmem_limit)


def _dot(a, b):
    return jnp.dot(a, b, preferred_element_type=F32)


def _dot_nt(a, b):
    return lax.dot_general(a, b, (((1,), (1,)), ((), ())), preferred_element_type=F32)


def _dot_tn(a, b):
    return lax.dot_general(a, b, (((0,), (0,)), ((), ())), preferred_element_type=F32)


def _sigmoid(z):
    return 0.5 * jnp.tanh(0.5 * z) + 0.5


def _mesh_pos():
    return lax.axis_index("x"), lax.axis_index("y"), lax.axis_index("c")


def _flat(p):
    return 4 * p[0] + 2 * p[1] + p[2]


def _by_core(masks):
    m0, m1 = (jnp.array(m, jnp.int32) for m in masks)
    return jnp.where(lax.axis_index("c") == 0, m0, m1)


GATHER_MASKS = ((0, 1, 4, 3, 2, 5, 6, 7), (0, 1, 2, 5, 4, 3, 6, 7))


def _gather_in_proj(x, norm_g, w_sh, smalls, order):
    S = x.shape[0]
    ts = 1024
    nt = S // ts
    n_small = len(smalls)
    n_steps = N_DEV

    def body(order_ref, x_ref, g_ref, w_hbm, *rest):
        small_in = rest[:n_small]
        p_ref, ht_ref, wg_hbm = rest[n_small:n_small + 3]
        small_out = rest[n_small + 3:2 * n_small + 3]
        (wbuf, hbuf, own_sem, send_sems, recv_sems, out_sems,
         small_send, small_recv, small_local) = rest[2 * n_small + 3:]
        k, i = pl.program_id(0), pl.program_id(1)
        x_, y_, c_ = _mesh_pos()
        me, sibling = (x_, y_, c_), (x_, y_, 1 - c_)
        my = _flat(me)
        chips = [(x_ ^ (1 - c_), y_ ^ c_), (x_ ^ c_, y_ ^ (1 - c_)), (1 - x_, 1 - y_)]
        peers = [sibling] + [(*chip, c_) for chip in chips] + [(*chip, 1 - c_) for chip in chips]

        def wcopy(sem, block, to, from_input=False):
            dst = wbuf.at[_flat(block)]
            return pltpu.make_async_remote_copy(
                src_ref=w_hbm if from_input else dst, dst_ref=dst,
                send_sem=send_sems.at[sem], recv_sem=recv_sems.at[sem], device_id=to, device_id_type=MESH)

        def small_copy(q, a, receive=False):
            slot = _flat(peers[q]) if receive else my
            return pltpu.make_async_remote_copy(
                src_ref=small_in[a], dst_ref=small_out[a].at[slot],
                send_sem=small_send.at[q, a], recv_sem=small_recv.at[q, a],
                device_id=peers[q], device_id_type=MESH)

        def keep(step, block):
            col = pl.multiple_of(_dp_block(_flat(block)) * W_BLK, 128)
            return pltpu.make_async_copy(wbuf.at[_flat(block)], wg_hbm.at[:, pl.ds(col, W_BLK)], out_sems.at[step])

        own = pltpu.make_async_copy(w_hbm, wbuf.at[my], own_sem)
        small_own = [pltpu.make_async_copy(small_in[a], small_out[a].at[my], small_local.at[a])
                     for a in range(n_small)]
        passed_on = [(*chips[1], 1 - c_), (*chips[0], 1 - c_), (*chips[2], 1 - c_)]
        arrivals = [me, sibling]
        for j in range(3):
            arrivals += [(*chips[j], c_), passed_on[j]]

        @pl.when(i == 0)
        def _():
            for kk in range(n_steps):
                @pl.when(k == kk)
                def _():
                    j = kk // 2 - 1
                    if kk == 0:
                        own.start()
                        wcopy(0, me, sibling, True).start()
                        wcopy(1, me, (*chips[0], c_), True).start()
                        own.wait()
                    elif kk == 1:
                        wcopy(0, sibling, me).wait_recv()
                        wcopy(2, me, (*chips[1], c_), True).start()
                    elif kk % 2 == 0:
                        wcopy(1 + j, (*chips[j], c_), me).wait_recv()
                        wcopy(4 + j, (*chips[j], c_), sibling).start()
                        if kk == 2:
                            wcopy(3, me, (*chips[2], c_), True).start()
                    else:
                        wcopy(4 + j, passed_on[j], me).wait_recv()
                        if kk == 3:
                            for cp in small_own:
                                cp.start()
                            for q in range(len(peers)):
                                for a in range(n_small):
                                    small_copy(q, a).start()
                    keep(kk, arrivals[kk]).start()

        row0 = pl.multiple_of(i * ts, ts)

        @pl.when(k == 0)
        def _():
            xf = x_ref[...]
            r = lax.rsqrt(jnp.mean(xf * xf, axis=-1, keepdims=True) + EPS)
            hf = (xf * r) * g_ref[...]
            hbuf[pl.ds(row0, ts), :] = hf.astype(BF16)
            ht_ref[...] = hf.astype(BF16).T

        p_ref[...] = _dot(hbuf[pl.ds(row0, ts), :], wbuf[order_ref[k]]).astype(BF16)

        @pl.when((k == n_steps - 1) & (i == nt - 1))
        def _():
            wcopy(0, me, sibling, True).wait_send()
            for j, chip in enumerate(chips):
                wcopy(1 + j, me, (*chip, c_), True).wait_send()
                wcopy(4 + j, (*chip, c_), sibling).wait_send()
            for kk in range(n_steps):
                keep(kk, arrivals[kk]).wait()
            for cp in small_own:
                cp.wait()
            for q in range(len(peers)):
                for a in range(n_small):
                    small_copy(q, a).wait_send()
                    small_copy(q, a, receive=True).wait_recv()

    first_pass = lambda k, i: jnp.where(k == 0, i, nt - 1)
    grid_spec = pltpu.PrefetchScalarGridSpec(
        num_scalar_prefetch=1, grid=(n_steps, nt),
        in_specs=[pl.BlockSpec((ts, D_MODEL), lambda k, i, o: (first_pass(k, i), 0)),
                  pl.BlockSpec((1, D_MODEL), lambda k, i, o: (0, 0)), ANY] + [ANY] * n_small,
        out_specs=[pl.BlockSpec((ts, W_BLK), lambda k, i, o: (i, o[k])),
                   pl.BlockSpec((D_MODEL, ts), lambda k, i, o: (0, first_pass(k, i))), ANY] + [ANY] * n_small,
        scratch_shapes=[pltpu.VMEM((N_DEV, D_MODEL, W_BLK), BF16), pltpu.VMEM((S, D_MODEL), BF16),
                        pltpu.SemaphoreType.DMA, pltpu.SemaphoreType.DMA((7,)), pltpu.SemaphoreType.DMA((7,)),
                        pltpu.SemaphoreType.DMA((n_steps,)),
                        pltpu.SemaphoreType.DMA((7, n_small)), pltpu.SemaphoreType.DMA((7, n_small)),
                        pltpu.SemaphoreType.DMA((n_small,))])
    return pl.pallas_call(
        body, name="gather_in_proj", grid_spec=grid_spec,
        out_shape=[jax.ShapeDtypeStruct((S, IN_COLS), BF16), jax.ShapeDtypeStruct((D_MODEL, S), BF16),
                   jax.ShapeDtypeStruct((D_MODEL, IN_COLS), BF16)]
        + [jax.ShapeDtypeStruct((N_DEV,) + s.shape, s.dtype) for s in smalls],
        compiler_params=_params(2),
    )(order, x, norm_g, w_sh, *smalls)


def _bias_table(rel_bias):
    wide = 1024

    def body(r_ref, o_ref):
        h = pl.program_id(0)
        col = lax.broadcasted_iota(jnp.int32, (1, wide), 1)
        k_minus_q = jnp.where(col < KB, col, col - wide)
        idx = jnp.clip(PADK - k_minus_q, -MAX_REL, MAX_REL) + MAX_REL
        f = jnp.zeros((1, wide), F32)
        for r in range(MAX_REL - CHUNK + 1, N_REL):
            f = jnp.where(idx == r, r_ref[h, r], f)
        kcol = lax.broadcasted_iota(jnp.int32, (1, KB), 1)
        kc = kcol >> 6
        sub = lax.broadcasted_iota(jnp.int32, (8, 1), 0)
        f8 = jnp.broadcast_to(f * LOG2E, (8, wide))
        base = f8
        for r in range(1, 8):
            base = jnp.where(sub == r, pltpu.roll(f8, r, 1), base)
        for qh in range(QB // 8):
            rows = (pltpu.roll(base, 8 * qh, 1) if qh else base)[:, 0:KB]
            qc = (8 * qh) // CHUNK
            band = (kc >= qc) & (kc <= qc + N_LEFT)
            for t in range(3):
                o_ref[t, 0, 8 * qh:8 * qh + 8, :] = jnp.where(band & (kcol >= PADK - t * QB), rows, NEG_BIG)

    return pl.pallas_call(
        body, name="bias_table", grid=(HEADS,),
        out_shape=jax.ShapeDtypeStruct((3, HEADS, QB, KB), F32),
        in_specs=[pl.BlockSpec(memory_space=pltpu.SMEM)],
        out_specs=pl.BlockSpec((3, 1, QB, KB), lambda h: (0, h, 0, 0)),
        compiler_params=_params(1),
    )(rel_bias)


KEY_GROUP = 4


def _load_keys(g, nb, p_hbm, kp, vp, sem):
    rows = KEY_GROUP * QB
    n_groups = p_hbm.shape[0] // rows

    def copies(c):
        src = pl.ds(c * rows, rows)
        dst = pl.ds(PADK + c * rows, rows)
        return (pltpu.make_async_copy(p_hbm.at[src, D_ATT:2 * D_ATT], kp.at[dst, :], sem.at[0, c]),
                pltpu.make_async_copy(p_hbm.at[src, 2 * D_ATT:3 * D_ATT], vp.at[dst, :], sem.at[1, c]))

    @pl.when(g == 0)
    def _():
        kp[0:PADK, :] = jnp.zeros((PADK, D_ATT), BF16)
        vp[0:PADK, :] = jnp.zeros((PADK, D_ATT), BF16)
        for c in range(n_groups):
            for cp in copies(c):
                cp.start()

    @pl.when((g % KEY_GROUP == 0) & (g < nb))
    def _():
        for cp in copies(g // KEY_GROUP):
            cp.wait()


def _attn_fwd(P, bias_tab):
    S = P.shape[0]
    nb = S // QB

    def body(q_ref, p_hbm, bias_ref, o_ref, ex_ref, rinv_ref, kp, vp, sem):
        g = pl.program_id(0)
        _load_keys(g, nb, p_hbm, kp, vp, sem)
        start = pl.multiple_of(g * QB, QB)
        lane = lax.broadcasted_iota(jnp.int32, (1, 128), 1)
        half = lambda h: (lane < 64) if h % 2 == 0 else (lane >= 64)
        pair = lambda h: slice(128 * (h // 2), 128 * (h // 2 + 1))

        def scores(h):
            qp = q_ref[:, pair(h)] * SCALE
            qm = jnp.where(half(h), qp, jnp.zeros_like(qp))
            return (_dot_nt(qm, kp[pl.ds(start, KB), pair(h)]) * LOG2E + bias_ref[0, h]).astype(BF16)

        def numerators(h, s):
            ex = jnp.exp2(s - jnp.max(s, axis=-1, keepdims=True))
            ex_ref[:, KB * h:KB * (h + 1)] = ex
            return ex

        def weighted_values(h, ex):
            vpair = vp[pl.ds(start, KB), pair(h)]
            o = _dot(ex, jnp.where(half(h), vpair, jnp.ones_like(vpair)))
            rinv = 1.0 / pltpu.roll(o, 64, 1)
            rinv_ref[:, h:h + 1] = rinv[:, 0:1] if h % 2 == 0 else 1.0 / o[:, 0:1]
            return o * rinv

        outs = []
        s_ahead = {0: scores(0), 1: scores(1)}
        ex_ahead = {0: numerators(0, s_ahead.pop(0))}
        for h in range(HEADS):
            if h + 2 < HEADS:
                s_ahead[h + 2] = scores(h + 2)
            if h + 1 < HEADS:
                ex_ahead[h + 1] = numerators(h + 1, s_ahead.pop(h + 1))
            outs.append(weighted_values(h, ex_ahead.pop(h)))
            if h % 2 == 1:
                o_ref[:, pair(h)] = jnp.where(lane < 64, outs[h - 1], outs[h]).astype(BF16)

    return pl.pallas_call(
        body, name="attn_fwd", grid=(nb,),
        out_shape=[jax.ShapeDtypeStruct((S, D_ATT), BF16), jax.ShapeDtypeStruct((S, HEADS * KB), BF16),
                   jax.ShapeDtypeStruct((S, HEADS), F32)],
        in_specs=[pl.BlockSpec((QB, D_ATT), lambda g: (g, 0)), ANY,
                  pl.BlockSpec((1, HEADS, QB, KB), lambda g: (jnp.minimum(g, 2), 0, 0, 0))],
        out_specs=[pl.BlockSpec((QB, D_ATT), lambda g: (g, 0)),
                   pl.BlockSpec((QB, HEADS * KB), lambda g: (g, 0)),
                   pl.BlockSpec((QB, HEADS), lambda g: (g, 0))],
        scratch_shapes=[pltpu.VMEM((S + PADK, D_ATT), BF16), pltpu.VMEM((S + PADK, D_ATT), BF16),
                        pltpu.SemaphoreType.DMA((2, S // (KEY_GROUP * QB)))],
        compiler_params=_params(1),
    )(P, P, bias_tab)


def _token_local(x, tgt, P, att, proj_g, w_out, cw_g, conv_b, final_g):
    S = x.shape[0]
    ts = 256
    nt = S // ts
    hb = 16

    def body(x_ref, t_ref, s1_ref, s2_ref, s3_ref, h1_ref, h2_ref, att_ref,
             pg_ref, wo_ref, cwg_ref, cb_ref, g2_ref,
             dx2_ref, dg_ref, datt_ref, dwo_ref, dproj_ref, sm1_ref, sm2_ref,
             carry, wao_ref, wco_ref, cw_ref, dwo_acc, dwao_acc, dwco_acc):
        i = pl.program_id(0)
        t = nt - 1 - i

        @pl.when(i == 0)
        def _():
            dwo_acc[...] = jnp.zeros_like(dwo_acc)
            dwao_acc[...] = jnp.zeros_like(dwao_acc)
            dwco_acc[...] = jnp.zeros_like(dwco_acc)
            lane = lax.broadcasted_iota(jnp.int32, (1, 128), 1)
            for j in range(N_DEV):
                wao_ref[:, 128 * j:128 * (j + 1)] = pg_ref[j, :, 0:128]
                wco_ref[:, 128 * j:128 * (j + 1)] = pg_ref[j, :, 128:256]
            for p in range(N_DEV // 2):
                cw_ref[:, 128 * p:128 * (p + 1)] = jnp.where(
                    lane < 64, cwg_ref[2 * p], pltpu.roll(cwg_ref[2 * p + 1], 64, 1))
            sm1_ref[...] = jnp.zeros_like(sm1_ref)
            sm2_ref[...] = jnp.zeros_like(sm2_ref)
            carry[...] = jnp.zeros_like(carry)

        za = s1_ref[:, 0:512]
        gb = s1_ref[:, 512:1024]
        gc = s1_ref[:, 1024:1536].astype(F32)
        u = s2_ref[:, 0:512].astype(F32)
        zc = s2_ref[:, 512:1024]
        ga = jnp.concatenate([s2_ref[:, 1024:1536], s3_ref[:, 0:512]], axis=1)
        gv = s3_ref[:, 512:1536]
        att = att_ref[...]
        row = lax.broadcasted_iota(jnp.int32, (ts, 1), 0)

        sa = _sigmoid(za)
        silu_a = za * sa
        att_g = att * silu_a
        y_att = _dot(att_g, wao_ref[...])

        cu = gc * u
        keep = jnp.where(t > 0, 1.0, 0.0).astype(F32)
        hcu = (h1_ref[:, 1024:1536].astype(F32) * h2_ref[:, 0:512].astype(F32)) * keep
        cu_m1 = jnp.where(row == 0, hcu[hb - 1:hb, :], pltpu.roll(cu, 1, 0))
        cu_m2 = jnp.where(row == 0, hcu[hb - 2:hb - 1, :],
                          jnp.where(row == 1, hcu[hb - 1:hb, :], pltpu.roll(cu, 2, 0)))
        w0, w1, w2 = cw_ref[0:1, :], cw_ref[1:2, :], cw_ref[2:3, :]
        vconv = w0 * cu_m2 + w1 * cu_m1 + w2 * cu + cb_ref[...]
        vcb = vconv.astype(BF16)
        sc = _sigmoid(zc)
        silu_c = zc * sc
        cg = gb * vcb * silu_c
        sga = _sigmoid(ga)
        sgv = _sigmoid(gv)
        y_conv = _dot(cg, wco_ref[...])

        yab, ycb = y_att.astype(BF16), y_conv.astype(BF16)
        m = sga * yab + sgv * ycb
        x2 = x_ref[...] + _dot(m, wo_ref[...])
        r2 = lax.rsqrt(jnp.mean(x2 * x2, axis=-1, keepdims=True) + EPS)
        xn2 = x2 * r2
        g2 = g2_ref[...]
        err = xn2 * g2 - t_ref[...]
        sm1_ref[1:2, :] += jnp.sum(err * err, axis=0, keepdims=True) * (0.5 / D_MODEL)

        dy = err * (1.0 / D_MODEL)
        sm1_ref[0:1, :] += jnp.sum(dy * xn2, axis=0, keepdims=True)
        dxn = dy * g2
        dx2 = r2 * (dxn - xn2 * jnp.mean(dxn * xn2, axis=-1, keepdims=True))
        dx2_ref[...] = dx2
        dx2b = dx2.astype(BF16)
        dwo_acc[...] += _dot_tn(m, dx2b)
        dm = _dot_nt(dx2b, wo_ref[...])
        dmb = dm.astype(BF16)
        dya = dmb * sga
        dyc = dmb * sgv
        dg_ref[:, 2560:3584] = dmb * yab * (sga * (1.0 - sga))
        dg_ref[:, 3584:4608] = dmb * ycb * (sgv * (1.0 - sgv))
        dwao_acc[...] += _dot_tn(att_g, dya)
        dwco_acc[...] += _dot_tn(cg, dyc)
        datt_g = _dot_nt(dya, wao_ref[...])
        dcg = _dot_nt(dyc, wco_ref[...])
        dagb, dcgb = datt_g.astype(BF16), dcg.astype(BF16)
        datt_ref[...] = dagb * silu_a
        dg_ref[:, 0:512] = dagb * att * (sa + silu_a * (1.0 - sa))
        dg_ref[:, 512:1024] = dcgb * vcb * silu_c
        dg_ref[:, 2048:2560] = dcgb * gb * vcb * (sc + silu_c * (1.0 - sc))
        dv = dcg * (gb * silu_c).astype(F32)
        sm2_ref[3:4, :] += jnp.sum(dv, axis=0, keepdims=True)
        sm2_ref[0:1, :] += jnp.sum(dv * cu_m2, axis=0, keepdims=True)
        sm2_ref[1:2, :] += jnp.sum(dv * cu_m1, axis=0, keepdims=True)
        sm2_ref[2:3, :] += jnp.sum(dv * cu, axis=0, keepdims=True)
        nxt = carry[...]
        dv_p1 = jnp.where(row == ts - 1, nxt[0:1, :], pltpu.roll(dv, ts - 1, 0))
        dv_p2 = jnp.where(row == ts - 1, nxt[1:2, :],
                          jnp.where(row == ts - 2, nxt[0:1, :], pltpu.roll(dv, ts - 2, 0)))
        dcu = w2 * dv + w1 * dv_p1 + w0 * dv_p2
        carry[...] = dv[0:8, :]
        dg_ref[:, 1024:1536] = (dcu * u).astype(BF16)
        dg_ref[:, 1536:2048] = (dcu * gc).astype(BF16)

        @pl.when(i == nt - 1)
        def _():
            dwo_ref[...] = dwo_acc[...].astype(BF16)
            for j in range(N_DEV):
                dproj_ref[j, :, 0:128] = dwao_acc[:, 128 * j:128 * (j + 1)].astype(BF16)
                dproj_ref[j, :, 128:256] = dwco_acc[:, 128 * j:128 * (j + 1)].astype(BF16)

    tile = lambda w: pl.BlockSpec((ts, w), lambda i: (nt - 1 - i, 0))
    seg = lambda c: pl.BlockSpec((ts, 1536), lambda i: (nt - 1 - i, c))
    halo = lambda c: pl.BlockSpec((hb, 1536), lambda i: (jnp.maximum((nt - 1 - i) * (ts // hb) - 1, 0), c))
    full = lambda a: pl.BlockSpec(a.shape, lambda i: (0,) * a.ndim)
    acc = lambda r, c: pl.BlockSpec((r, c), lambda i: (0, 0))
    return pl.pallas_call(
        body, name="token_local", grid=(nt,),
        out_shape=[jax.ShapeDtypeStruct((S, D_MODEL), F32), jax.ShapeDtypeStruct((S, IN_COLS), BF16),
                   jax.ShapeDtypeStruct((S, D_ATT), BF16), jax.ShapeDtypeStruct((D_MODEL, D_MODEL), BF16),
                   jax.ShapeDtypeStruct(proj_g.shape, BF16),
                   jax.ShapeDtypeStruct((8, D_MODEL), F32), jax.ShapeDtypeStruct((8, D_CONV), F32)],
        in_specs=[tile(D_MODEL), tile(D_MODEL), seg(1), seg(2), seg(3), halo(1), halo(2), tile(D_ATT),
                  full(proj_g), full(w_out), full(cw_g), full(conv_b), full(final_g)],
        out_specs=[tile(D_MODEL), tile(GATE_COLS), tile(D_ATT), acc(D_MODEL, D_MODEL), full(proj_g),
                   acc(8, D_MODEL), acc(8, D_CONV)],
        scratch_shapes=[pltpu.VMEM((8, D_CONV), F32),
                        pltpu.VMEM((D_ATT, D_MODEL), BF16), pltpu.VMEM((D_CONV, D_MODEL), BF16),
                        pltpu.VMEM((8, D_CONV), F32), pltpu.VMEM((D_MODEL, D_MODEL), F32),
                        pltpu.VMEM((D_ATT, D_MODEL), F32), pltpu.VMEM((D_CONV, D_MODEL), F32)],
        compiler_params=_params(1),
    )(x, tgt, P, P, P, P, P, att, proj_g, w_out, cw_g, conv_b, final_g)


def _fold_diagonals(d_ref, o_ref):
    wide = D_MODEL
    sub = lax.broadcasted_iota(jnp.int32, (8, 1), 0)
    col = lax.broadcasted_iota(jnp.int32, (1, wide), 1)
    pad = jnp.zeros((8, wide - KB), F32)
    for h in range(HEADS):
        acc = jnp.concatenate([d_ref[h, 0:8, :], pad], axis=1)
        for qh in range(1, QB // 8):
            a = jnp.concatenate([d_ref[h, 8 * qh:8 * qh + 8, :], pad], axis=1)
            acc = acc + pltpu.roll(a, wide - 8 * qh, 1)
        for r in range(1, 8):
            acc = jnp.where(sub == r, pltpu.roll(acc, wide - r, 1), acc)
        vec = jnp.sum(acc, axis=0, keepdims=True)
        far = (col <= PADK - MAX_REL) | (col > KB)
        tail = jnp.sum(jnp.where(far, vec, 0.0), axis=-1, keepdims=True)
        o_ref[h:h + 1, :] = jnp.where(col == wide - 1, tail, vec)


def _attn_bwd(P, att, datt, ex, rinv, dP):
    S = P.shape[0]
    nb = S // QB

    def body(q_ref, att_ref, datt_ref, ex_ref, rinv_ref, p_hbm, dp_hbm, out_ref, dbias_ref,
             kp, vp, dq_ring, dk_ring, dv_ring, db_ref, sem):
        g = pl.program_id(0)

        _load_keys(g, nb, p_hbm, kp, vp, sem)

        @pl.when(g == 0)
        def _():
            db_ref[...] = jnp.zeros_like(db_ref)
            dk_ring[...] = jnp.zeros_like(dk_ring)
            dv_ring[...] = jnp.zeros_like(dv_ring)

        s_new = g % 3
        s_mid = (g + 2) % 3
        s_old = (g + 1) % 3

        @pl.when(g < nb)
        def _():
            start = pl.multiple_of(g * QB, QB)
            lane = lax.broadcasted_iota(jnp.int32, (1, 128), 1)
            for p in range(HEADS // 2):
                cols = slice(128 * p, 128 * (p + 1))
                qp = q_ref[:, cols] * SCALE
                op = att_ref[:, cols].astype(F32)
                dop = datt_ref[:, cols]
                kpair = kp[pl.ds(start, KB), cols]
                vpair = vp[pl.ds(start, KB), cols]
                dqs = []
                dk_acc = jnp.zeros((KB, 128), F32)
                dv_acc = jnp.zeros((KB, 128), F32)
                for e in range(2):
                    h = 2 * p + e
                    lm = (lane < 64) if e == 0 else (lane >= 64)
                    qm = jnp.where(lm, qp, jnp.zeros_like(qp))
                    dom = jnp.where(lm, dop, jnp.zeros_like(dop))
                    exh = ex_ref[:, KB * h:KB * (h + 1)]
                    rinv = rinv_ref[:, h:h + 1]
                    domf = dom.astype(F32)
                    dp = _dot_nt(dom, vpair)
                    delta = jnp.sum(domf * op, axis=-1, keepdims=True)
                    dsb = exh * ((dp - delta) * rinv).astype(BF16)
                    db_ref[h] += dsb.astype(F32)
                    dqs.append(_dot(dsb, kpair) * SCALE)
                    dk_acc = dk_acc + _dot_tn(dsb, qm)
                    dv_acc = dv_acc + _dot_tn(exh, (domf * rinv).astype(BF16))
                dq_ring[s_new, :, cols] = jnp.where(lane < 64, dqs[0], dqs[1])
                dk_ring[s_old, :, cols] += dk_acc[0:QB]
                dk_ring[s_mid, :, cols] += dk_acc[QB:2 * QB]
                dk_ring[s_new, :, cols] = dk_acc[2 * QB:3 * QB]
                dv_ring[s_old, :, cols] += dv_acc[0:QB]
                dv_ring[s_mid, :, cols] += dv_acc[QB:2 * QB]
                dv_ring[s_new, :, cols] = dv_acc[2 * QB:3 * QB]

        @pl.when(g >= 2)
        def _():
            out_ref[:, 0:D_ATT] = dq_ring[s_old].astype(BF16)
            out_ref[:, D_ATT:2 * D_ATT] = dk_ring[s_old].astype(BF16)
            out_ref[:, 2 * D_ATT:3 * D_ATT] = dv_ring[s_old].astype(BF16)

        @pl.when(g == nb + 1)
        def _():
            _fold_diagonals(db_ref, dbias_ref)

    qblk = lambda w: pl.BlockSpec((QB, w), lambda g: (jnp.minimum(g, nb - 1), 0))
    return pl.pallas_call(
        body, name="attn_bwd", grid=(nb + 2,),
        out_shape=[jax.ShapeDtypeStruct((S, IN_COLS), BF16), jax.ShapeDtypeStruct((HEADS, D_MODEL), F32)],
        in_specs=[qblk(D_ATT), qblk(D_ATT), qblk(D_ATT), qblk(HEADS * KB), qblk(HEADS), ANY, ANY],
        out_specs=[pl.BlockSpec((QB, 3 * D_ATT), lambda g: (jnp.maximum(g - 2, 0), GATE_COLS // (3 * D_ATT))),
                   pl.BlockSpec((HEADS, D_MODEL), lambda g: (0, 0))],
        input_output_aliases={6: 0},
        scratch_shapes=[pltpu.VMEM((S + PADK, D_ATT), BF16), pltpu.VMEM((S + PADK, D_ATT), BF16),
                        pltpu.VMEM((3, QB, D_ATT), F32), pltpu.VMEM((3, QB, D_ATT), F32),
                        pltpu.VMEM((3, QB, D_ATT), F32), pltpu.VMEM((HEADS, QB, KB), F32),
                        pltpu.SemaphoreType.DMA((2, S // (KEY_GROUP * QB)))],
        compiler_params=_params(1),
    )(P, att, datt, ex, rinv, P, dP)


def _dp_block(j):
    return (j + GATE_COLS // W_BLK) % N_DEV


def _in_proj_bwd(x, norm_g, dx2, dP, w_in_g):
    S = x.shape[0]
    ts = 512

    def body(x_ref, g_ref, dx2_ref, dp_ref, w_ref, gx_ref, dn_ref):
        @pl.when(pl.program_id(0) == 0)
        def _():
            dn_ref[...] = jnp.zeros_like(dn_ref)

        dh = _dot_nt(dp_ref[...], w_ref[...])
        xf = x_ref[...]
        r = lax.rsqrt(jnp.mean(xf * xf, axis=-1, keepdims=True) + EPS)
        xn = xf * r
        dn_ref[0:1, :] += jnp.sum(dh * xn, axis=0, keepdims=True)
        dhg = dh * g_ref[...]
        gx_ref[...] = dx2_ref[...] + r * (dhg - xn * jnp.mean(dhg * xn, axis=-1, keepdims=True))

    tile = lambda w: pl.BlockSpec((ts, w), lambda i: (i, 0))
    return pl.pallas_call(
        body, name="in_proj_bwd", grid=(S // ts,),
        out_shape=[jax.ShapeDtypeStruct((S, D_MODEL), F32), jax.ShapeDtypeStruct((8, D_MODEL), F32)],
        in_specs=[tile(D_MODEL), pl.BlockSpec((1, D_MODEL), lambda i: (0, 0)), tile(D_MODEL),
                  tile(IN_COLS),
                  pl.BlockSpec((D_MODEL, IN_COLS), lambda i: (0, 0))],
        out_specs=[tile(D_MODEL), pl.BlockSpec((8, D_MODEL), lambda i: (0, 0))],
        compiler_params=_params(1),
    )(x, norm_g, dx2, dP, w_in_g)


SCATTER_MASKS = ((3, 4, 5, 2, 7, 6, 1, 0), (5, 2, 3, 4, 7, 6, 1, 0))


def _w_in_grad_scatter(ht, dP, d_proj, d_wo, pack, order):
    S = ht.shape[1]
    ts = min(S, 2048)
    nt = S // ts
    n_steps = 8

    def body(order_ref, ht_ref, d_ref, proj_hbm, wo_hbm, pack_hbm, g_ref, rproj, rwo, rpack,
             acc, stage, rsib, rici, d2d_send, d2d_recv, ici_send, ici_recv, small_send, small_recv, local_sems):
        k, i = pl.program_id(0), pl.program_id(1)
        x, y, c = _mesh_pos()
        my = _flat((x, y, c))
        sibling = (x, y, 1 - c)
        owners = [(x ^ (1 - c), y ^ c, c), (x ^ c, y ^ (1 - c), c), (1 - x, 1 - y, c)]
        peers = [sibling, (1 - x, y, c), (x, 1 - y, c), (1 - x, 1 - y, c),
                 (1 - x, y, 1 - c), (x, 1 - y, 1 - c), (1 - x, 1 - y, 1 - c)]
        small = ((proj_hbm, rproj, True), (wo_hbm, rwo, True), (pack_hbm, rpack, False))
        n_small = len(small)

        def small_copy(kk, a, receive=False):
            src, dst, per_peer = small[a]
            slot = _flat(peers[kk]) if receive else my
            return pltpu.make_async_remote_copy(
                src_ref=src.at[_flat(peers[kk])] if per_peer else src, dst_ref=dst.at[slot],
                send_sem=small_send.at[kk, a], recv_sem=small_recv.at[kk, a],
                device_id=peers[kk], device_id_type=MESH)

        def to_sibling(t):
            return pltpu.make_async_remote_copy(
                src_ref=stage.at[0], dst_ref=rsib.at[t % 2], send_sem=d2d_send.at[t], recv_sem=d2d_recv.at[t],
                device_id=sibling, device_id_type=MESH)

        def to_owner(t):
            return pltpu.make_async_remote_copy(
                src_ref=stage.at[1], dst_ref=rici.at[t], send_sem=ici_send.at[t], recv_sem=ici_recv.at[t],
                device_id=owners[t], device_id_type=MESH)

        own_small = [pltpu.make_async_copy(src.at[my] if per_peer else src, dst.at[my], local_sems.at[a])
                     for a, (src, dst, per_peer) in enumerate(small)]

        @pl.when((k == 0) & (i == 0))
        def _():
            for cp in own_small:
                cp.start()
            for kk in range(len(peers)):
                for a in range(n_small):
                    small_copy(kk, a).start()

        @pl.when(i == 0)
        def _():
            acc[...] = jnp.zeros_like(acc)

        acc[...] += _dot(ht_ref[...], d_ref[...])

        @pl.when(i == nt - 1)
        def _():
            for s in range(n_steps):
                @pl.when(k == s)
                def _():
                    t = s // 2
                    if s % 2 == 0:
                        if t >= 1:
                            to_sibling(t - 1).wait_send()
                        stage[0] = acc[...].astype(BF16)
                        to_sibling(t).start()
                    elif t < 3:
                        if t >= 1:
                            to_owner(t - 1).wait_send()
                        to_sibling(t).wait_recv()
                        stage[1] = (acc[...] + rsib[t % 2].astype(F32)).astype(BF16)
                        to_owner(t).start()
                    else:
                        to_sibling(t).wait_recv()
                        total = acc[...] + rsib[t % 2].astype(F32)
                        for j in range(3):
                            to_owner(j).wait_recv()
                            total = total + rici[j].astype(F32)
                        g_ref[...] = total
                        to_owner(2).wait_send()
                        to_sibling(3).wait_send()
                        for q in range(len(peers)):
                            for a in range(n_small):
                                small_copy(q, a).wait_send()
                                small_copy(q, a, receive=True).wait_recv()
                        for cp in own_small:
                            cp.wait()

    blk = (D_MODEL, W_BLK)
    grid_spec = pltpu.PrefetchScalarGridSpec(
        num_scalar_prefetch=1, grid=(n_steps, nt),
        in_specs=[pl.BlockSpec((D_MODEL, ts), lambda k, i, o: (0, i)),
                  pl.BlockSpec((ts, W_BLK), lambda k, i, o: (i, _dp_block(o[k]))),
                  ANY, ANY, ANY],
        out_specs=[pl.BlockSpec(blk, lambda k, i, o: (0, 0)), ANY, ANY, ANY],
        scratch_shapes=[pltpu.VMEM(blk, F32), pltpu.VMEM((2,) + blk, BF16),
                        pltpu.VMEM((2,) + blk, BF16), pltpu.VMEM((3,) + blk, BF16),
                        pltpu.SemaphoreType.DMA((4,)), pltpu.SemaphoreType.DMA((4,)),
                        pltpu.SemaphoreType.DMA((3,)), pltpu.SemaphoreType.DMA((3,)),
                        pltpu.SemaphoreType.DMA((7, 3)), pltpu.SemaphoreType.DMA((7, 3)),
                        pltpu.SemaphoreType.DMA((3,))])
    return pl.pallas_call(
        body, name="w_in_grad_scatter", grid_spec=grid_spec,
        out_shape=[jax.ShapeDtypeStruct(blk, F32),
                   jax.ShapeDtypeStruct(d_proj.shape, BF16), jax.ShapeDtypeStruct(d_wo.shape, BF16),
                   jax.ShapeDtypeStruct((N_DEV,) + pack.shape, F32)],
        compiler_params=_params(2),
    )(order, ht, dP, d_proj, d_wo, pack)


def _adamw(w, g, m, v):
    m = ADAM_B1 * m + (1.0 - ADAM_B1) * g
    v = ADAM_B2 * v + (1.0 - ADAM_B2) * (g * g)
    m_hat = m / (1.0 - ADAM_B1 ** ADAM_STEP)
    v_hat = v / (1.0 - ADAM_B2 ** ADAM_STEP)
    delta = -ADAM_LR * (m_hat / (jnp.sqrt(v_hat) + ADAM_EPS) + ADAM_WD * w)
    return delta, m, v


def _sum_adamw(parts, w, m, v, name):
    R, C = w.shape
    n = parts.shape[0]
    tr = min(R, 256)

    def body(p_ref, w_ref, m_ref, v_ref, g_ref, d_ref, nm_ref, nv_ref):
        g = p_ref[0].astype(F32)
        for s in range(1, n):
            g = g + p_ref[s].astype(F32)
        g_ref[...] = g
        d_ref[...], nm_ref[...], nv_ref[...] = _adamw(w_ref[...], g, m_ref[...], v_ref[...])

    tile = pl.BlockSpec((tr, C), lambda i: (i, 0))
    return pl.pallas_call(
        body, name=name, grid=(R // tr,),
        out_shape=[jax.ShapeDtypeStruct((R, C), F32)] * 4,
        in_specs=[pl.BlockSpec((n, tr, C), lambda i: (0, i, 0)), tile, tile, tile],
        out_specs=[tile] * 4,
        compiler_params=_params(1),
    )(parts, w, m, v)


def _adamw_mid(r_proj, r_wo, params):
    def body(rp_ref, rw_ref, *refs):
        ins, outs = refs[:9], refs[9:]

        def total(part):
            g = part(0).astype(F32)
            for s in range(1, N_DEV):
                g = g + part(s).astype(F32)
            return g

        grads = (total(lambda s: rp_ref[s, :, 0:128]), total(lambda s: rp_ref[s, :, 128:256]),
                 total(lambda s: rw_ref[s]))
        for n, g in enumerate(grads):
            w, m, v = (r[...] for r in ins[3 * n:3 * n + 3])
            outs[4 * n][...] = g
            outs[4 * n + 1][...], outs[4 * n + 2][...], outs[4 * n + 3][...] = _adamw(w, g, m, v)

    return pl.pallas_call(
        body, name="adamw_mid",
        out_shape=[jax.ShapeDtypeStruct(params[3 * n].shape, F32) for n in range(3) for _ in range(4)],
        compiler_params=pltpu.CompilerParams(vmem_limit_bytes=VMEM_LIMIT),
    )(r_proj, r_wo, *params)


def _adamw_small(r_pack, params):
    wide = 384

    def body(p_ref, *refs):
        ins, loss_ref, outs = refs[:15], refs[15], refs[16:]
        tot = p_ref[0]
        for s in range(1, N_DEV):
            tot = tot + p_ref[s]
        me = _flat(_mesh_pos())
        loss_ref[...] = jnp.sum(tot[2:3, :], axis=-1, keepdims=True)
        mine = pltpu.roll(tot[0:8, 0:D_CONV], (D_CONV - 64 * me) % D_CONV, 1)
        col = lax.broadcasted_iota(jnp.int32, (D_MODEL, wide), 0)
        idx = lax.broadcasted_iota(jnp.int32, (D_MODEL, wide), 1)
        near = (idx > MAX_REL - CHUNK) & (idx < 2 * MAX_REL) & (col == PADK + MAX_REL - idx)
        far = (idx == 2 * MAX_REL) & (col == D_MODEL - 1)
        perm = jnp.where(near | far, 1.0, 0.0).astype(F32)
        g_rel = jnp.dot(tot[8:16], perm, precision=lax.Precision.HIGHEST, preferred_element_type=F32)
        grads = (tot[0:1], tot[1:2], mine[3:6, 0:64], tot[6:7, 0:D_CONV], g_rel[:, 0:N_REL])
        for n, g in enumerate(grads):
            w, m, v = (r[...] for r in ins[3 * n:3 * n + 3])
            outs[4 * n][...] = g
            outs[4 * n + 1][...], outs[4 * n + 2][...], outs[4 * n + 3][...] = _adamw(w, g, m, v)

    return pl.pallas_call(
        body, name="adamw_small",
        out_shape=[jax.ShapeDtypeStruct((1, 1), F32)]
        + [jax.ShapeDtypeStruct(params[3 * n].shape, F32) for n in range(5) for _ in range(4)],
    )(r_pack, *params)


def _pad_row(a, width=D_MODEL):
    a = a.reshape(-1, a.shape[-1])
    return jnp.pad(a, ((0, 0), (0, width - a.shape[-1])))


def kernel(x, norm_g, w_in, rel_bias, w_att_out, conv_w, conv_b, w_conv_out, w_out, final_norm_g, loss_target, m_norm_g, m_w_in, m_rel_bias, m_w_att_out, m_conv_w, m_conv_b, m_w_conv_out, m_w_out, m_final_norm_g, v_norm_g, v_w_in, v_rel_bias, v_w_att_out, v_conv_w, v_conv_b, v_w_conv_out, v_w_out, v_final_norm_g):
    S = x.shape[1]
    x2d = x.reshape(S, D_MODEL)
    tgt = loss_target.reshape(S, D_MODEL)
    me = 4 * lax.axis_index("x") + 2 * lax.axis_index("y") + lax.axis_index("c")
    row = lambda a: a.reshape(1, D_MODEL)

    proj_sh = jnp.concatenate([w_att_out[0], w_conv_out[0]], axis=1).astype(BF16)
    cw_sh = jnp.pad(conv_w[0], ((0, 5), (0, 64)))
    P, ht, w_in_g, proj_g, w_out_g, cw_g = _gather_in_proj(
        x2d, norm_g, w_in[0].astype(BF16), [proj_sh, w_out[0].astype(BF16), cw_sh],
        me ^ _by_core(GATHER_MASKS))

    bias_tab = _bias_table(rel_bias[0])
    att, ex, rinv = _attn_fwd(P, bias_tab)
    dx2, dP, datt, d_wo, d_proj, sm1, sm2 = _token_local(
        x2d, tgt, P, att, proj_g, w_out_g.reshape(D_MODEL, D_MODEL), cw_g, conv_b, row(final_norm_g))
    dP, dbias = _attn_bwd(P, att, datt, ex, rinv, dP)
    grad_x, dnorm = _in_proj_bwd(x2d, norm_g, dx2, dP, w_in_g)

    pack = jnp.concatenate([dnorm[0:1], sm1[0:2], _pad_row(sm2[0:4]), jnp.zeros((1, D_MODEL), F32), dbias],
                           axis=0)
    g_win_sum, r_proj, r_wo, r_pack = _w_in_grad_scatter(
        ht, dP, d_proj, d_wo.reshape(N_DEV, 128, D_MODEL), pack, me ^ _by_core(SCATTER_MASKS))

    res = {"w_in": _sum_adamw(g_win_sum[None], w_in[0], m_w_in[0], v_w_in[0], "adamw_w_in")}
    mid = _adamw_mid(r_proj, r_wo, (w_att_out[0], m_w_att_out[0], v_w_att_out[0],
                                    w_conv_out[0], m_w_conv_out[0], v_w_conv_out[0],
                                    w_out[0], m_w_out[0], v_w_out[0]))
    for n, name in enumerate(("w_att_out", "w_conv_out", "w_out")):
        res[name] = mid[4 * n:4 * n + 4]
    small = _adamw_small(r_pack, (norm_g, m_norm_g, v_norm_g,
                                  row(final_norm_g), row(m_final_norm_g), row(v_final_norm_g),
                                  conv_w[0], m_conv_w[0], v_conv_w[0], conv_b, m_conv_b, v_conv_b,
                                  rel_bias[0], m_rel_bias[0], v_rel_bias[0]))
    loss = small[0].reshape(())
    for n, name in enumerate(("norm_g", "final_norm_g", "conv_w", "conv_b", "rel_bias")):
        res[name] = small[1 + 4 * n:5 + 4 * n]

    leading = {"norm_g": (1, D_MODEL), "final_norm_g": (D_MODEL,), "conv_b": (1, D_CONV)}
    outs = []
    for kind in range(4):
        for name in ("norm_g", "w_in", "rel_bias", "w_att_out", "conv_w", "conv_b", "w_conv_out", "w_out",
                     "final_norm_g"):
            a = res[name][kind]
            outs.append(a.reshape(leading[name]) if name in leading else a[None])
    return (loss, grad_x.reshape(1, S, D_MODEL), *outs)
```

```python
import functools

import numpy as np
import jax
import jax.numpy as jnp
from jax import lax
from jax.experimental import pallas as pl
from jax.experimental.pallas import tpu as pltpu

F32 = jnp.float32
BF16 = jnp.bfloat16

D_MODEL = 1024
CHUNK = 64
N_LEFT = 8
HEADS = 8
D_ATT = 512
D_CONV = 512
MAX_REL = 128
N_REL = 2 * MAX_REL + 1
IN_COLS = 6144
EPS = 1e-6
NEG_BIG = -1e30
N_DEV = 8
W_BLK = IN_COLS // N_DEV
QB = 4 * CHUNK
KB = QB + N_LEFT * CHUNK
PADK = N_LEFT * CHUNK
SCALE = 64 ** -0.5
LOG2E = 1.4426950408889634
GATE_COLS = IN_COLS - 3 * D_ATT

ADAM_LR = 0.001
ADAM_B1 = 0.9
ADAM_B2 = 0.999
ADAM_EPS = 1e-08
ADAM_WD = 0.01
ADAM_STEP = 10

VMEM_LIMIT = 56 * 1024 * 1024

MESH = pl.DeviceIdType.MESH
ANY = pl.BlockSpec(memory_space=pl.ANY)


def _params(n_grid, vmem_limit=VMEM_LIMIT):
    return pltpu.CompilerParams(dimension_semantics=("arbitrary",) * n_grid,
                                vmem_limit_bytes=vmem_limit)


def _dot(a, b):
    return jnp.dot(a, b, preferred_element_type=F32)


def _dot_nt(a, b):
    return lax.dot_general(a, b, (((1,), (1,)), ((), ())), preferred_element_type=F32)


def _dot_tn(a, b):
    return lax.dot_general(a, b, (((0,), (0,)), ((), ())), preferred_element_type=F32)


def _sigmoid(z):
    return 0.5 * jnp.tanh(0.5 * z) + 0.5


def _mesh_pos():
    return lax.axis_index("x"), lax.axis_index("y"), lax.axis_index("c")


def _flat(p):
    return 4 * p[0] + 2 * p[1] + p[2]


def _by_core(masks):
    m0, m1 = (jnp.array(m, jnp.int32) for m in masks)
    return jnp.where(lax.axis_index("c") == 0, m0, m1)


GATHER_MASKS = ((0, 1, 4, 3, 2, 5, 6, 7), (0, 1, 2, 5, 4, 3, 6, 7))


def _gather_in_proj(x, norm_g, w_sh, smalls, order):
    S = x.shape[0]
    ts = 1024
    nt = S // ts
    n_small = len(smalls)
    n_steps = N_DEV

    def body(order_ref, x_ref, g_ref, w_hbm, *rest):
        small_in = rest[:n_small]
        p_ref, ht_ref, wg_hbm = rest[n_small:n_small + 3]
        small_out = rest[n_small + 3:2 * n_small + 3]
        (wbuf, hbuf, own_sem, send_sems, recv_sems, out_sems,
         small_send, small_recv, small_local) = rest[2 * n_small + 3:]
        k, i = pl.program_id(0), pl.program_id(1)
        x_, y_, c_ = _mesh_pos()
        me, sibling = (x_, y_, c_), (x_, y_, 1 - c_)
        my = _flat(me)
        chips = [(x_ ^ (1 - c_), y_ ^ c_), (x_ ^ c_, y_ ^ (1 - c_)), (1 - x_, 1 - y_)]
        peers = [sibling] + [(*chip, c_) for chip in chips] + [(*chip, 1 - c_) for chip in chips]

        def wcopy(sem, block, to, from_input=False):
            dst = wbuf.at[_flat(block)]
            return pltpu.make_async_remote_copy(
                src_ref=w_hbm if from_input else dst, dst_ref=dst,
                send_sem=send_sems.at[sem], recv_sem=recv_sems.at[sem], device_id=to, device_id_type=MESH)

        def small_copy(q, a, receive=False):
            slot = _flat(peers[q]) if receive else my
            return pltpu.make_async_remote_copy(
                src_ref=small_in[a], dst_ref=small_out[a].at[slot],
                send_sem=small_send.at[q, a], recv_sem=small_recv.at[q, a],
                device_id=peers[q], device_id_type=MESH)

        def keep(step, block):
            col = pl.multiple_of(_dp_block(_flat(block)) * W_BLK, 128)
            return pltpu.make_async_copy(wbuf.at[_flat(block)], wg_hbm.at[:, pl.ds(col, W_BLK)], out_sems.at[step])

        own = pltpu.make_async_copy(w_hbm, wbuf.at[my], own_sem)
        small_own = [pltpu.make_async_copy(small_in[a], small_out[a].at[my], small_local.at[a])
                     for a in range(n_small)]
        passed_on = [(*chips[1], 1 - c_), (*chips[0], 1 - c_), (*chips[2], 1 - c_)]
        arrivals = [me, sibling]
        for j in range(3):
            arrivals += [(*chips[j], c_), passed_on[j]]

        @pl.when(i == 0)
        def _():
            for kk in range(n_steps):
                @pl.when(k == kk)
                def _():
                    j = kk // 2 - 1
                    if kk == 0:
                        own.start()
                        wcopy(0, me, sibling, True).start()
                        wcopy(1, me, (*chips[0], c_), True).start()
                        own.wait()
                    elif kk == 1:
                        wcopy(0, sibling, me).wait_recv()
                        wcopy(2, me, (*chips[1], c_), True).start()
                    elif kk % 2 == 0:
                        wcopy(1 + j, (*chips[j], c_), me).wait_recv()
                        wcopy(4 + j, (*chips[j], c_), sibling).start()
                        if kk == 2:
                            wcopy(3, me, (*chips[2], c_), True).start()
                    else:
                        wcopy(4 + j, passed_on[j], me).wait_recv()
                        if kk == 3:
                            for cp in small_own:
                                cp.start()
                            for q in range(len(peers)):
                                for a in range(n_small):
                                    small_copy(q, a).start()
                    keep(kk, arrivals[kk]).start()

        row0 = pl.multiple_of(i * ts, ts)

        @pl.when(k == 0)
        def _():
            xf = x_ref[...]
            r = lax.rsqrt(jnp.mean(xf * xf, axis=-1, keepdims=True) + EPS)
            hf = (xf * r) * g_ref[...]
            hbuf[pl.ds(row0, ts), :] = hf.astype(BF16)
            ht_ref[...] = hf.astype(BF16).T

        p_ref[...] = _dot(hbuf[pl.ds(row0, ts), :], wbuf[order_ref[k]]).astype(BF16)

        @pl.when((k == n_steps - 1) & (i == nt - 1))
        def _():
            wcopy(0, me, sibling, True).wait_send()
            for j, chip in enumerate(chips):
                wcopy(1 + j, me, (*chip, c_), True).wait_send()
                wcopy(4 + j, (*chip, c_), sibling).wait_send()
            for kk in range(n_steps):
                keep(kk, arrivals[kk]).wait()
            for cp in small_own:
                cp.wait()
            for q in range(len(peers)):
                for a in range(n_small):
                    small_copy(q, a).wait_send()
                    small_copy(q, a, receive=True).wait_recv()

    first_pass = lambda k, i: jnp.where(k == 0, i, nt - 1)
    grid_spec = pltpu.PrefetchScalarGridSpec(
        num_scalar_prefetch=1, grid=(n_steps, nt),
        in_specs=[pl.BlockSpec((ts, D_MODEL), lambda k, i, o: (first_pass(k, i), 0)),
                  pl.BlockSpec((1, D_MODEL), lambda k, i, o: (0, 0)), ANY] + [ANY] * n_small,
        out_specs=[pl.BlockSpec((ts, W_BLK), lambda k, i, o: (i, o[k])),
                   pl.BlockSpec((D_MODEL, ts), lambda k, i, o: (0, first_pass(k, i))), ANY] + [ANY] * n_small,
        scratch_shapes=[pltpu.VMEM((N_DEV, D_MODEL, W_BLK), BF16), pltpu.VMEM((S, D_MODEL), BF16),
                        pltpu.SemaphoreType.DMA, pltpu.SemaphoreType.DMA((7,)), pltpu.SemaphoreType.DMA((7,)),
                        pltpu.SemaphoreType.DMA((n_steps,)),
                        pltpu.SemaphoreType.DMA((7, n_small)), pltpu.SemaphoreType.DMA((7, n_small)),
                        pltpu.SemaphoreType.DMA((n_small,))])
    return pl.pallas_call(
        body, name="gather_in_proj", grid_spec=grid_spec,
        out_shape=[jax.ShapeDtypeStruct((S, IN_COLS), BF16), jax.ShapeDtypeStruct((D_MODEL, S), BF16),
                   jax.ShapeDtypeStruct((D_MODEL, IN_COLS), BF16)]
        + [jax.ShapeDtypeStruct((N_DEV,) + s.shape, s.dtype) for s in smalls],
        compiler_params=_params(2),
    )(order, x, norm_g, w_sh, *smalls)


def _bias_table(rel_bias):
    wide = 1024

    def body(r_ref, o_ref):
        h = pl.program_id(0)
        col = lax.broadcasted_iota(jnp.int32, (1, wide), 1)
        k_minus_q = jnp.where(col < KB, col, col - wide)
        idx = jnp.clip(PADK - k_minus_q, -MAX_REL, MAX_REL) + MAX_REL
        f = jnp.zeros((1, wide), F32)
        for r in range(MAX_REL - CHUNK + 1, N_REL):
            f = jnp.where(idx == r, r_ref[h, r], f)
        kcol = lax.broadcasted_iota(jnp.int32, (1, KB), 1)
        kc = kcol >> 6
        sub = lax.broadcasted_iota(jnp.int32, (8, 1), 0)
        f8 = jnp.broadcast_to(f * LOG2E, (8, wide))
        base = f8
        for r in range(1, 8):
            base = jnp.where(sub == r, pltpu.roll(f8, r, 1), base)
        for qh in range(0, QB // 8, 2):
            rows = jnp.concatenate([(pltpu.roll(base, 8 * q, 1) if q else base)[:, 0:KB] for q in (qh, qh + 1)],
                                   axis=0)
            qc = (8 * qh) // CHUNK
            band = (kc >= qc) & (kc <= qc + N_LEFT)
            for t in range(3):
                o_ref[t, 0, 8 * qh:8 * qh + 16, :] = jnp.where(
                    band & (kcol >= PADK - t * QB), rows, NEG_BIG).astype(BF16)

    return pl.pallas_call(
        body, name="bias_table", grid=(HEADS,),
        out_shape=jax.ShapeDtypeStruct((3, HEADS, QB, KB), BF16),
        in_specs=[pl.BlockSpec(memory_space=pltpu.SMEM)],
        out_specs=pl.BlockSpec((3, 1, QB, KB), lambda h: (0, h, 0, 0)),
        compiler_params=_params(1),
    )(rel_bias)


KEY_GROUP = 4


def _load_keys(g, nb, p_hbm, kp, vp, sem):
    rows = KEY_GROUP * QB
    n_groups = p_hbm.shape[0] // rows

    def copies(c):
        src = pl.ds(c * rows, rows)
        dst = pl.ds(PADK + c * rows, rows)
        return (pltpu.make_async_copy(p_hbm.at[src, D_ATT:2 * D_ATT], kp.at[dst, :], sem.at[0, c]),
                pltpu.make_async_copy(p_hbm.at[src, 2 * D_ATT:3 * D_ATT], vp.at[dst, :], sem.at[1, c]))

    @pl.when(g == 0)
    def _():
        kp[0:PADK, :] = jnp.zeros((PADK, D_ATT), BF16)
        vp[0:PADK, :] = jnp.zeros((PADK, D_ATT), BF16)
        for c in range(n_groups):
            for cp in copies(c):
                cp.start()

    @pl.when((g % KEY_GROUP == 0) & (g < nb))
    def _():
        for cp in copies(g // KEY_GROUP):
            cp.wait()


def _attn_fwd(P, bias_tab):
    S = P.shape[0]
    nb = S // QB

    def body(q_ref, p_hbm, bias_ref, o_ref, ex_ref, rinv_ref, kp, vp, sem):
        g = pl.program_id(0)
        _load_keys(g, nb, p_hbm, kp, vp, sem)
        start = pl.multiple_of(g * QB, QB)
        lane = lax.broadcasted_iota(jnp.int32, (1, 128), 1)
        half = lambda h: (lane < 64) if h % 2 == 0 else (lane >= 64)
        pair = lambda h: slice(128 * (h // 2), 128 * (h // 2 + 1))

        def scores(h):
            qp = q_ref[:, pair(h)] * SCALE
            qm = jnp.where(half(h), qp, jnp.zeros_like(qp))
            return (_dot_nt(qm, kp[pl.ds(start, KB), pair(h)]) * LOG2E).astype(BF16) + bias_ref[0, h]

        def numerators(h, s):
            ex = jnp.exp2(s - jnp.max(s, axis=-1, keepdims=True))
            ex_ref[:, KB * h:KB * (h + 1)] = ex
            return ex

        def weighted_values(h, ex):
            vpair = vp[pl.ds(start, KB), pair(h)]
            o = _dot(ex, jnp.where(half(h), vpair, jnp.ones_like(vpair)))
            rinv = 1.0 / pltpu.roll(o, 64, 1)
            rinv_ref[:, h:h + 1] = rinv[:, 0:1] if h % 2 == 0 else 1.0 / o[:, 0:1]
            return o * rinv

        outs = []
        s_ahead = {0: scores(0), 1: scores(1)}
        ex_ahead = {0: numerators(0, s_ahead.pop(0))}
        for h in range(HEADS):
            if h + 2 < HEADS:
                s_ahead[h + 2] = scores(h + 2)
            if h + 1 < HEADS:
                ex_ahead[h + 1] = numerators(h + 1, s_ahead.pop(h + 1))
            outs.append(weighted_values(h, ex_ahead.pop(h)))
            if h % 2 == 1:
                o_ref[:, pair(h)] = jnp.where(lane < 64, outs[h - 1], outs[h]).astype(BF16)

    return pl.pallas_call(
        body, name="attn_fwd", grid=(nb,),
        out_shape=[jax.ShapeDtypeStruct((S, D_ATT), BF16), jax.ShapeDtypeStruct((S, HEADS * KB), BF16),
                   jax.ShapeDtypeStruct((S, HEADS), F32)],
        in_specs=[pl.BlockSpec((QB, D_ATT), lambda g: (g, 0)), ANY,
                  pl.BlockSpec((1, HEADS, QB, KB), lambda g: (jnp.minimum(g, 2), 0, 0, 0))],
        out_specs=[pl.BlockSpec((QB, D_ATT), lambda g: (g, 0)),
                   pl.BlockSpec((QB, HEADS * KB), lambda g: (g, 0)),
                   pl.BlockSpec((QB, HEADS), lambda g: (g, 0))],
        scratch_shapes=[pltpu.VMEM((S + PADK, D_ATT), BF16), pltpu.VMEM((S + PADK, D_ATT), BF16),
                        pltpu.SemaphoreType.DMA((2, S // (KEY_GROUP * QB)))],
        compiler_params=_params(1),
    )(P, P, bias_tab)


def _token_local(x, tgt, P, att, proj_g, w_out, cw_g, conv_b, final_g):
    S = x.shape[0]
    ts = 256
    nt = S // ts
    hb = 16

    def body(x_ref, t_ref, s1_ref, s2_ref, s3_ref, h1_ref, h2_ref, att_ref,
             pg_ref, wo_ref, cwg_ref, cb_ref, g2_ref,
             dx2_ref, dg_ref, datt_ref, dwo_ref, dproj_ref, sm1_ref, sm2_ref,
             carry, wao_ref, wco_ref, cw_ref, dwo_acc, dwao_acc, dwco_acc):
        i = pl.program_id(0)
        t = nt - 1 - i

        @pl.when(i == 0)
        def _():
            dwo_acc[...] = jnp.zeros_like(dwo_acc)
            dwao_acc[...] = jnp.zeros_like(dwao_acc)
            dwco_acc[...] = jnp.zeros_like(dwco_acc)
            lane = lax.broadcasted_iota(jnp.int32, (1, 128), 1)
            for j in range(N_DEV):
                wao_ref[:, 128 * j:128 * (j + 1)] = pg_ref[j, :, 0:128]
                wco_ref[:, 128 * j:128 * (j + 1)] = pg_ref[j, :, 128:256]
            for p in range(N_DEV // 2):
                cw_ref[:, 128 * p:128 * (p + 1)] = jnp.where(
                    lane < 64, cwg_ref[2 * p], pltpu.roll(cwg_ref[2 * p + 1], 64, 1))
            sm1_ref[...] = jnp.zeros_like(sm1_ref)
            sm2_ref[...] = jnp.zeros_like(sm2_ref)
            carry[...] = jnp.zeros_like(carry)

        za = s1_ref[:, 0:512]
        gb = s1_ref[:, 512:1024]
        gc = s1_ref[:, 1024:1536].astype(F32)
        u = s2_ref[:, 0:512].astype(F32)
        zc = s2_ref[:, 512:1024]
        ga = jnp.concatenate([s2_ref[:, 1024:1536], s3_ref[:, 0:512]], axis=1)
        gv = s3_ref[:, 512:1536]
        att = att_ref[...]

        sa = _sigmoid(za)
        silu_a = za * sa
        att_g = att * silu_a
        y_att = _dot(att_g, wao_ref[...])

        cu = gc * u
        keep = jnp.where(t > 0, 1.0, 0.0).astype(F32)
        hcu = (h1_ref[hb - 8:hb, 1024:1536].astype(F32) * h2_ref[hb - 8:hb, 0:512].astype(F32)) * keep
        cu_ext = jnp.concatenate([hcu, cu], axis=0)
        cu_m1 = pltpu.roll(cu_ext, 1, 0)[8:]
        cu_m2 = pltpu.roll(cu_ext, 2, 0)[8:]
        w0, w1, w2 = cw_ref[0:1, :], cw_ref[1:2, :], cw_ref[2:3, :]
        vconv = w0 * cu_m2 + w1 * cu_m1 + w2 * cu + cb_ref[...]
        vcb = vconv.astype(BF16)
        sc = _sigmoid(zc)
        silu_c = zc * sc
        cg = gb * vcb * silu_c
        sga = _sigmoid(ga)
        sgv = _sigmoid(gv)
        y_conv = _dot(cg, wco_ref[...])

        yab, ycb = y_att.astype(BF16), y_conv.astype(BF16)
        m = sga * yab + sgv * ycb
        x2 = x_ref[...] + _dot(m, wo_ref[...])
        r2 = lax.rsqrt(jnp.mean(x2 * x2, axis=-1, keepdims=True) + EPS)
        xn2 = x2 * r2
        g2 = g2_ref[...]
        err = xn2 * g2 - t_ref[...]
        sm1_ref[1:2, :] += jnp.sum(err * err, axis=0, keepdims=True) * (0.5 / D_MODEL)

        dy = err * (1.0 / D_MODEL)
        sm1_ref[0:1, :] += jnp.sum(dy * xn2, axis=0, keepdims=True)
        dxn = dy * g2
        dx2 = r2 * (dxn - xn2 * jnp.mean(dxn * xn2, axis=-1, keepdims=True))
        dx2_ref[...] = dx2
        dx2b = dx2.astype(BF16)
        dwo_acc[...] += _dot_tn(m, dx2b)
        dm = _dot_nt(dx2b, wo_ref[...])
        dmb = dm.astype(BF16)
        dya = dmb * sga
        dyc = dmb * sgv
        dg_ref[:, 2560:3584] = dmb * yab * (sga * (1.0 - sga))
        dg_ref[:, 3584:4608] = dmb * ycb * (sgv * (1.0 - sgv))
        dwao_acc[...] += _dot_tn(att_g, dya)
        dwco_acc[...] += _dot_tn(cg, dyc)
        datt_g = _dot_nt(dya, wao_ref[...])
        dcg = _dot_nt(dyc, wco_ref[...])
        dagb, dcgb = datt_g.astype(BF16), dcg.astype(BF16)
        datt_ref[...] = dagb * silu_a
        dg_ref[:, 0:512] = dagb * att * (sa + silu_a * (1.0 - sa))
        dg_ref[:, 512:1024] = dcgb * vcb * silu_c
        dg_ref[:, 2048:2560] = dcgb * gb * vcb * (sc + silu_c * (1.0 - sc))
        dv = dcg * (gb * silu_c).astype(F32)
        sm2_ref[3:4, :] += jnp.sum(dv, axis=0, keepdims=True)
        sm2_ref[0:1, :] += jnp.sum(dv * cu_m2, axis=0, keepdims=True)
        sm2_ref[1:2, :] += jnp.sum(dv * cu_m1, axis=0, keepdims=True)
        sm2_ref[2:3, :] += jnp.sum(dv * cu, axis=0, keepdims=True)
        dv_ext = jnp.concatenate([dv, carry[...]], axis=0)
        dv_p1 = pltpu.roll(dv_ext, ts + 7, 0)[0:ts]
        dv_p2 = pltpu.roll(dv_ext, ts + 6, 0)[0:ts]
        dcu = w2 * dv + w1 * dv_p1 + w0 * dv_p2
        carry[...] = dv[0:8, :]
        dg_ref[:, 1024:1536] = (dcu * u).astype(BF16)
        dg_ref[:, 1536:2048] = (dcu * gc).astype(BF16)

        @pl.when(i == nt - 1)
        def _():
            dwo_ref[...] = dwo_acc[...].astype(BF16)
            for j in range(N_DEV):
                dproj_ref[j, :, 0:128] = dwao_acc[:, 128 * j:128 * (j + 1)].astype(BF16)
                dproj_ref[j, :, 128:256] = dwco_acc[:, 128 * j:128 * (j + 1)].astype(BF16)

    tile = lambda w: pl.BlockSpec((ts, w), lambda i: (nt - 1 - i, 0))
    seg = lambda c: pl.BlockSpec((ts, 1536), lambda i: (nt - 1 - i, c))
    halo = lambda c: pl.BlockSpec((hb, 1536), lambda i: (jnp.maximum((nt - 1 - i) * (ts // hb) - 1, 0), c))
    full = lambda a: pl.BlockSpec(a.shape, lambda i: (0,) * a.ndim)
    acc = lambda r, c: pl.BlockSpec((r, c), lambda i: (0, 0))
    return pl.pallas_call(
        body, name="token_local", grid=(nt,),
        out_shape=[jax.ShapeDtypeStruct((S, D_MODEL), F32), jax.ShapeDtypeStruct((S, IN_COLS), BF16),
                   jax.ShapeDtypeStruct((S, D_ATT), BF16), jax.ShapeDtypeStruct((D_MODEL, D_MODEL), BF16),
                   jax.ShapeDtypeStruct(proj_g.shape, BF16),
                   jax.ShapeDtypeStruct((8, D_MODEL), F32), jax.ShapeDtypeStruct((8, D_CONV), F32)],
        in_specs=[tile(D_MODEL), tile(D_MODEL), seg(1), seg(2), seg(3), halo(1), halo(2), tile(D_ATT),
                  full(proj_g), full(w_out), full(cw_g), full(conv_b), full(final_g)],
        out_specs=[tile(D_MODEL), tile(GATE_COLS), tile(D_ATT), acc(D_MODEL, D_MODEL), full(proj_g),
                   acc(8, D_MODEL), acc(8, D_CONV)],
        scratch_shapes=[pltpu.VMEM((8, D_CONV), F32),
                        pltpu.VMEM((D_ATT, D_MODEL), BF16), pltpu.VMEM((D_CONV, D_MODEL), BF16),
                        pltpu.VMEM((8, D_CONV), F32), pltpu.VMEM((D_MODEL, D_MODEL), F32),
                        pltpu.VMEM((D_ATT, D_MODEL), F32), pltpu.VMEM((D_CONV, D_MODEL), F32)],
        compiler_params=_params(1),
    )(x, tgt, P, P, P, P, P, att, proj_g, w_out, cw_g, conv_b, final_g)


def _fold_diagonals(d_ref, o_ref):
    wide = D_MODEL
    sub = lax.broadcasted_iota(jnp.int32, (8, 1), 0)
    col = lax.broadcasted_iota(jnp.int32, (1, wide), 1)
    pad = jnp.zeros((8, wide - KB), F32)
    for h in range(HEADS):
        acc = jnp.concatenate([d_ref[h, 0:8, :], pad], axis=1)
        for qh in range(1, QB // 8):
            a = jnp.concatenate([d_ref[h, 8 * qh:8 * qh + 8, :], pad], axis=1)
            acc = acc + pltpu.roll(a, wide - 8 * qh, 1)
        for r in range(1, 8):
            acc = jnp.where(sub == r, pltpu.roll(acc, wide - r, 1), acc)
        vec = jnp.sum(acc, axis=0, keepdims=True)
        far = (col <= PADK - MAX_REL) | (col > KB)
        tail = jnp.sum(jnp.where(far, vec, 0.0), axis=-1, keepdims=True)
        o_ref[h:h + 1, :] = jnp.where(col == wide - 1, tail, vec)


def _attn_bwd(P, att, datt, ex, rinv, dP):
    S = P.shape[0]
    nb = S // QB

    def body(q_ref, att_ref, datt_ref, ex_ref, rinv_ref, p_hbm, dp_hbm, out_ref, dbias_ref,
             kp, vp, dq_ring, dk_ring, dv_ring, db_ref, sem):
        g = pl.program_id(0)

        _load_keys(g, nb, p_hbm, kp, vp, sem)

        @pl.when(g == 0)
        def _():
            db_ref[...] = jnp.zeros_like(db_ref)
            dk_ring[...] = jnp.zeros_like(dk_ring)
            dv_ring[...] = jnp.zeros_like(dv_ring)

        s_new = g % 3
        s_mid = (g + 2) % 3
        s_old = (g + 1) % 3

        @pl.when(g < nb)
        def _():
            start = pl.multiple_of(g * QB, QB)
            lane = lax.broadcasted_iota(jnp.int32, (1, 128), 1)
            for p in range(HEADS // 2):
                cols = slice(128 * p, 128 * (p + 1))
                qp = q_ref[:, cols] * SCALE
                op = att_ref[:, cols].astype(F32)
                dop = datt_ref[:, cols]
                kpair = kp[pl.ds(start, KB), cols]
                vpair = vp[pl.ds(start, KB), cols]
                dqs = []
                dk_acc = jnp.zeros((KB, 128), F32)
                dv_acc = jnp.zeros((KB, 128), F32)
                for e in range(2):
                    h = 2 * p + e
                    lm = (lane < 64) if e == 0 else (lane >= 64)
                    qm = jnp.where(lm, qp, jnp.zeros_like(qp))
                    dom = jnp.where(lm, dop, jnp.zeros_like(dop))
                    exh = ex_ref[:, KB * h:KB * (h + 1)]
                    rinv = rinv_ref[:, h:h + 1]
                    domf = dom.astype(F32)
                    dp = _dot_nt(dom, vpair)
                    delta = jnp.sum(domf * op, axis=-1, keepdims=True)
                    dsb = exh * ((dp - delta) * rinv).astype(BF16)
                    db_ref[h] += dsb.astype(F32)
                    dqs.append(_dot(dsb, kpair) * SCALE)
                    dk_acc = dk_acc + _dot_tn(dsb, qm)
                    dv_acc = dv_acc + _dot_tn(exh, (domf * rinv).astype(BF16))
                dq_ring[s_new, :, cols] = jnp.where(lane < 64, dqs[0], dqs[1])
                dk_ring[s_old, :, cols] += dk_acc[0:QB]
                dk_ring[s_mid, :, cols] += dk_acc[QB:2 * QB]
                dk_ring[s_new, :, cols] = dk_acc[2 * QB:3 * QB]
                dv_ring[s_old, :, cols] += dv_acc[0:QB]
                dv_ring[s_mid, :, cols] += dv_acc[QB:2 * QB]
                dv_ring[s_new, :, cols] = dv_acc[2 * QB:3 * QB]

        @pl.when(g >= 2)
        def _():
            out_ref[:, 0:D_ATT] = dq_ring[s_old].astype(BF16)
            out_ref[:, D_ATT:2 * D_ATT] = dk_ring[s_old].astype(BF16)
            out_ref[:, 2 * D_ATT:3 * D_ATT] = dv_ring[s_old].astype(BF16)

        @pl.when(g == nb + 1)
        def _():
            _fold_diagonals(db_ref, dbias_ref)

    qblk = lambda w: pl.BlockSpec((QB, w), lambda g: (jnp.minimum(g, nb - 1), 0))
    return pl.pallas_call(
        body, name="attn_bwd", grid=(nb + 2,),
        out_shape=[jax.ShapeDtypeStruct((S, IN_COLS), BF16), jax.ShapeDtypeStruct((HEADS, D_MODEL), F32)],
        in_specs=[qblk(D_ATT), qblk(D_ATT), qblk(D_ATT), qblk(HEADS * KB), qblk(HEADS), ANY, ANY],
        out_specs=[pl.BlockSpec((QB, 3 * D_ATT), lambda g: (jnp.maximum(g - 2, 0), GATE_COLS // (3 * D_ATT))),
                   pl.BlockSpec((HEADS, D_MODEL), lambda g: (0, 0))],
        input_output_aliases={6: 0},
        scratch_shapes=[pltpu.VMEM((S + PADK, D_ATT), BF16), pltpu.VMEM((S + PADK, D_ATT), BF16),
                        pltpu.VMEM((3, QB, D_ATT), F32), pltpu.VMEM((3, QB, D_ATT), F32),
                        pltpu.VMEM((3, QB, D_ATT), F32), pltpu.VMEM((HEADS, QB, KB), F32),
                        pltpu.SemaphoreType.DMA((2, S // (KEY_GROUP * QB)))],
        compiler_params=_params(1),
    )(P, att, datt, ex, rinv, P, dP)


def _dp_block(j):
    return (j + GATE_COLS // W_BLK) % N_DEV


def _in_proj_bwd(x, norm_g, dx2, dP, w_in_g):
    S = x.shape[0]
    ts = 512

    def body(x_ref, g_ref, dx2_ref, dp_ref, w_ref, gx_ref, dn_ref):
        @pl.when(pl.program_id(0) == 0)
        def _():
            dn_ref[...] = jnp.zeros_like(dn_ref)

        dh = _dot_nt(dp_ref[...], w_ref[...])
        xf = x_ref[...]
        r = lax.rsqrt(jnp.mean(xf * xf, axis=-1, keepdims=True) + EPS)
        xn = xf * r
        dn_ref[0:1, :] += jnp.sum(dh * xn, axis=0, keepdims=True)
        dhg = dh * g_ref[...]
        gx_ref[...] = dx2_ref[...] + r * (dhg - xn * jnp.mean(dhg * xn, axis=-1, keepdims=True))

    tile = lambda w: pl.BlockSpec((ts, w), lambda i: (i, 0))
    return pl.pallas_call(
        body, name="in_proj_bwd", grid=(S // ts,),
        out_shape=[jax.ShapeDtypeStruct((S, D_MODEL), F32), jax.ShapeDtypeStruct((8, D_MODEL), F32)],
        in_specs=[tile(D_MODEL), pl.BlockSpec((1, D_MODEL), lambda i: (0, 0)), tile(D_MODEL),
                  tile(IN_COLS),
                  pl.BlockSpec((D_MODEL, IN_COLS), lambda i: (0, 0))],
        out_specs=[tile(D_MODEL), pl.BlockSpec((8, D_MODEL), lambda i: (0, 0))],
        compiler_params=_params(1),
    )(x, norm_g, dx2, dP, w_in_g)


SCATTER_MASKS = ((3, 4, 5, 2, 7, 6, 1, 0), (5, 2, 3, 4, 7, 6, 1, 0))


def _w_in_grad_scatter(ht, dP, d_proj, d_wo, pack, order):
    S = ht.shape[1]
    ts = min(S, 2048)
    nt = S // ts
    n_steps = 8

    def body(order_ref, ht_ref, d_ref, proj_hbm, wo_hbm, pack_hbm, g_ref, rproj, rwo, rpack,
             acc, stage, rsib, rici, d2d_send, d2d_recv, ici_send, ici_recv, small_send, small_recv, local_sems):
        k, i = pl.program_id(0), pl.program_id(1)
        x, y, c = _mesh_pos()
        my = _flat((x, y, c))
        sibling = (x, y, 1 - c)
        owners = [(x ^ (1 - c), y ^ c, c), (x ^ c, y ^ (1 - c), c), (1 - x, 1 - y, c)]
        peers = [sibling, (1 - x, y, c), (x, 1 - y, c), (1 - x, 1 - y, c),
                 (1 - x, y, 1 - c), (x, 1 - y, 1 - c), (1 - x, 1 - y, 1 - c)]
        small = ((proj_hbm, rproj, True), (wo_hbm, rwo, True), (pack_hbm, rpack, False))
        n_small = len(small)

        def small_copy(kk, a, receive=False):
            src, dst, per_peer = small[a]
            slot = _flat(peers[kk]) if receive else my
            return pltpu.make_async_remote_copy(
                src_ref=src.at[_flat(peers[kk])] if per_peer else src, dst_ref=dst.at[slot],
                send_sem=small_send.at[kk, a], recv_sem=small_recv.at[kk, a],
                device_id=peers[kk], device_id_type=MESH)

        def to_sibling(t):
            return pltpu.make_async_remote_copy(
                src_ref=stage.at[0], dst_ref=rsib.at[t % 2], send_sem=d2d_send.at[t], recv_sem=d2d_recv.at[t],
                device_id=sibling, device_id_type=MESH)

        def to_owner(t):
            return pltpu.make_async_remote_copy(
                src_ref=stage.at[1], dst_ref=rici.at[t], send_sem=ici_send.at[t], recv_sem=ici_recv.at[t],
                device_id=owners[t], device_id_type=MESH)

        own_small = [pltpu.make_async_copy(src.at[my] if per_peer else src, dst.at[my], local_sems.at[a])
                     for a, (src, dst, per_peer) in enumerate(small)]

        @pl.when((k == 0) & (i == 0))
        def _():
            for cp in own_small:
                cp.start()
            for kk in range(len(peers)):
                for a in range(n_small):
                    small_copy(kk, a).start()

        @pl.when(i == 0)
        def _():
            acc[...] = jnp.zeros_like(acc)

        acc[...] += _dot(ht_ref[...], d_ref[...])

        @pl.when(i == nt - 1)
        def _():
            for s in range(n_steps):
                @pl.when(k == s)
                def _():
                    t = s // 2
                    if s % 2 == 0:
                        if t >= 1:
                            to_sibling(t - 1).wait_send()
                        stage[0] = acc[...].astype(BF16)
                        to_sibling(t).start()
                    elif t < 3:
                        if t >= 1:
                            to_owner(t - 1).wait_send()
                        to_sibling(t).wait_recv()
                        stage[1] = (acc[...] + rsib[t % 2].astype(F32)).astype(BF16)
                        to_owner(t).start()
                    else:
                        to_sibling(t).wait_recv()
                        total = acc[...] + rsib[t % 2].astype(F32)
                        for j in range(3):
                            to_owner(j).wait_recv()
                            total = total + rici[j].astype(F32)
                        g_ref[...] = total
                        to_owner(2).wait_send()
                        to_sibling(3).wait_send()
                        for q in range(len(peers)):
                            for a in range(n_small):
                                small_copy(q, a).wait_send()
                                small_copy(q, a, receive=True).wait_recv()
                        for cp in own_small:
                            cp.wait()

    blk = (D_MODEL, W_BLK)
    grid_spec = pltpu.PrefetchScalarGridSpec(
        num_scalar_prefetch=1, grid=(n_steps, nt),
        in_specs=[pl.BlockSpec((D_MODEL, ts), lambda k, i, o: (0, i)),
                  pl.BlockSpec((ts, W_BLK), lambda k, i, o: (i, _dp_block(o[k]))),
                  ANY, ANY, ANY],
        out_specs=[pl.BlockSpec(blk, lambda k, i, o: (0, 0)), ANY, ANY, ANY],
        scratch_shapes=[pltpu.VMEM(blk, F32), pltpu.VMEM((2,) + blk, BF16),
                        pltpu.VMEM((2,) + blk, BF16), pltpu.VMEM((3,) + blk, BF16),
                        pltpu.SemaphoreType.DMA((4,)), pltpu.SemaphoreType.DMA((4,)),
                        pltpu.SemaphoreType.DMA((3,)), pltpu.SemaphoreType.DMA((3,)),
                        pltpu.SemaphoreType.DMA((7, 3)), pltpu.SemaphoreType.DMA((7, 3)),
                        pltpu.SemaphoreType.DMA((3,))])
    return pl.pallas_call(
        body, name="w_in_grad_scatter", grid_spec=grid_spec,
        out_shape=[jax.ShapeDtypeStruct(blk, F32),
                   jax.ShapeDtypeStruct(d_proj.shape, BF16), jax.ShapeDtypeStruct(d_wo.shape, BF16),
                   jax.ShapeDtypeStruct((N_DEV,) + pack.shape, F32)],
        compiler_params=_params(2),
    )(order, ht, dP, d_proj, d_wo, pack)


def _adamw(w, g, m, v):
    m = ADAM_B1 * m + (1.0 - ADAM_B1) * g
    v = ADAM_B2 * v + (1.0 - ADAM_B2) * (g * g)
    m_hat = m / (1.0 - ADAM_B1 ** ADAM_STEP)
    v_hat = v / (1.0 - ADAM_B2 ** ADAM_STEP)
    delta = -ADAM_LR * (m_hat / (jnp.sqrt(v_hat) + ADAM_EPS) + ADAM_WD * w)
    return delta, m, v


def _sum_adamw(parts, w, m, v, name):
    R, C = w.shape
    n = parts.shape[0]
    tr = min(R, 256)

    def body(p_ref, w_ref, m_ref, v_ref, g_ref, d_ref, nm_ref, nv_ref):
        g = p_ref[0].astype(F32)
        for s in range(1, n):
            g = g + p_ref[s].astype(F32)
        g_ref[...] = g
        d_ref[...], nm_ref[...], nv_ref[...] = _adamw(w_ref[...], g, m_ref[...], v_ref[...])

    tile = pl.BlockSpec((tr, C), lambda i: (i, 0))
    return pl.pallas_call(
        body, name=name, grid=(R // tr,),
        out_shape=[jax.ShapeDtypeStruct((R, C), F32)] * 4,
        in_specs=[pl.BlockSpec((n, tr, C), lambda i: (0, i, 0)), tile, tile, tile],
        out_specs=[tile] * 4,
        compiler_params=_params(1),
    )(parts, w, m, v)


def _adamw_mid(r_proj, r_wo, params):
    def body(rp_ref, rw_ref, *refs):
        ins, outs = refs[:9], refs[9:]

        def total(part):
            g = part(0).astype(F32)
            for s in range(1, N_DEV):
                g = g + part(s).astype(F32)
            return g

        grads = (total(lambda s: rp_ref[s, :, 0:128]), total(lambda s: rp_ref[s, :, 128:256]),
                 total(lambda s: rw_ref[s]))
        for n, g in enumerate(grads):
            w, m, v = (r[...] for r in ins[3 * n:3 * n + 3])
            outs[4 * n][...] = g
            outs[4 * n + 1][...], outs[4 * n + 2][...], outs[4 * n + 3][...] = _adamw(w, g, m, v)

    return pl.pallas_call(
        body, name="adamw_mid",
        out_shape=[jax.ShapeDtypeStruct(params[3 * n].shape, F32) for n in range(3) for _ in range(4)],
        compiler_params=pltpu.CompilerParams(vmem_limit_bytes=VMEM_LIMIT),
    )(r_proj, r_wo, *params)


def _adamw_small(r_pack, params):
    wide = 384

    def body(p_ref, *refs):
        ins, loss_ref, outs = refs[:15], refs[15], refs[16:]
        tot = p_ref[0]
        for s in range(1, N_DEV):
            tot = tot + p_ref[s]
        me = _flat(_mesh_pos())
        loss_ref[...] = jnp.sum(tot[2:3, :], axis=-1, keepdims=True)
        mine = pltpu.roll(tot[0:8, 0:D_CONV], (D_CONV - 64 * me) % D_CONV, 1)
        col = lax.broadcasted_iota(jnp.int32, (D_MODEL, wide), 0)
        idx = lax.broadcasted_iota(jnp.int32, (D_MODEL, wide), 1)
        near = (idx > MAX_REL - CHUNK) & (idx < 2 * MAX_REL) & (col == PADK + MAX_REL - idx)
        far = (idx == 2 * MAX_REL) & (col == D_MODEL - 1)
        perm = jnp.where(near | far, 1.0, 0.0).astype(F32)
        g_rel = jnp.dot(tot[8:16], perm, precision=lax.Precision.HIGHEST, preferred_element_type=F32)
        grads = (tot[0:1], tot[1:2], mine[3:6, 0:64], tot[6:7, 0:D_CONV], g_rel[:, 0:N_REL])
        for n, g in enumerate(grads):
            w, m, v = (r[...] for r in ins[3 * n:3 * n + 3])
            outs[4 * n][...] = g
            outs[4 * n + 1][...], outs[4 * n + 2][...], outs[4 * n + 3][...] = _adamw(w, g, m, v)

    return pl.pallas_call(
        body, name="adamw_small",
        out_shape=[jax.ShapeDtypeStruct((1, 1), F32)]
        + [jax.ShapeDtypeStruct(params[3 * n].shape, F32) for n in range(5) for _ in range(4)],
    )(r_pack, *params)


def _pad_row(a, width=D_MODEL):
    a = a.reshape(-1, a.shape[-1])
    return jnp.pad(a, ((0, 0), (0, width - a.shape[-1])))


def kernel(x, norm_g, w_in, rel_bias, w_att_out, conv_w, conv_b, w_conv_out, w_out, final_norm_g, loss_target, m_norm_g, m_w_in, m_rel_bias, m_w_att_out, m_conv_w, m_conv_b, m_w_conv_out, m_w_out, m_final_norm_g, v_norm_g, v_w_in, v_rel_bias, v_w_att_out, v_conv_w, v_conv_b, v_w_conv_out, v_w_out, v_final_norm_g):
    S = x.shape[1]
    x2d = x.reshape(S, D_MODEL)
    tgt = loss_target.reshape(S, D_MODEL)
    me = 4 * lax.axis_index("x") + 2 * lax.axis_index("y") + lax.axis_index("c")
    row = lambda a: a.reshape(1, D_MODEL)

    proj_sh = jnp.concatenate([w_att_out[0], w_conv_out[0]], axis=1).astype(BF16)
    cw_sh = jnp.pad(conv_w[0], ((0, 5), (0, 64)))
    P, ht, w_in_g, proj_g, w_out_g, cw_g = _gather_in_proj(
        x2d, norm_g, w_in[0].astype(BF16), [proj_sh, w_out[0].astype(BF16), cw_sh],
        me ^ _by_core(GATHER_MASKS))

    bias_tab = _bias_table(rel_bias[0])
    att, ex, rinv = _attn_fwd(P, bias_tab)
    dx2, dP, datt, d_wo, d_proj, sm1, sm2 = _token_local(
        x2d, tgt, P, att, proj_g, w_out_g.reshape(D_MODEL, D_MODEL), cw_g, conv_b, row(final_norm_g))
    dP, dbias = _attn_bwd(P, att, datt, ex, rinv, dP)
    grad_x, dnorm = _in_proj_bwd(x2d, norm_g, dx2, dP, w_in_g)

    pack = jnp.concatenate([dnorm[0:1], sm1[0:2], _pad_row(sm2[0:4]), jnp.zeros((1, D_MODEL), F32), dbias],
                           axis=0)
    g_win_sum, r_proj, r_wo, r_pack = _w_in_grad_scatter(
        ht, dP, d_proj, d_wo.reshape(N_DEV, 128, D_MODEL), pack, me ^ _by_core(SCATTER_MASKS))

    res = {"w_in": _sum_adamw(g_win_sum[None], w_in[0], m_w_in[0], v_w_in[0], "adamw_w_in")}
    mid = _adamw_mid(r_proj, r_wo, (w_att_out[0], m_w_att_out[0], v_w_att_out[0],
                                    w_conv_out[0], m_w_conv_out[0], v_w_conv_out[0],
                                    w_out[0], m_w_out[0], v_w_out[0]))
    for n, name in enumerate(("w_att_out", "w_conv_out", "w_out")):
        res[name] = mid[4 * n:4 * n + 4]
    small = _adamw_small(r_pack, (norm_g, m_norm_g, v_norm_g,
                                  row(final_norm_g), row(m_final_norm_g), row(v_final_norm_g),
                                  conv_w[0], m_conv_w[0], v_conv_w[0], conv_b, m_conv_b, v_conv_b,
                                  rel_bias[0], m_rel_bias[0], v_rel_bias[0]))
    loss = small[0].reshape(())
    for n, name in enumerate(("norm_g", "final_norm_g", "conv_w", "conv_b", "rel_bias")):
        res[name] = small[1 + 4 * n:5 + 4 * n]

    leading = {"norm_g": (1, D_MODEL), "final_norm_g": (D_MODEL,), "conv_b": (1, D_CONV)}
    outs = []
    for kind in range(4):
        for name in ("norm_g", "w_in", "rel_bias", "w_att_out", "conv_w", "conv_b", "w_conv_out", "w_out",
                     "final_norm_g"):
            a = res[name][kind]
            outs.append(a.reshape(leading[name]) if name in leading else a[None])
    return (loss, grad_x.reshape(1, S, D_MODEL), *outs)
```

```python
import functools

import numpy as np
import jax
import jax.numpy as jnp
from jax import lax
from jax.experimental import pallas as pl
from jax.experimental.pallas import tpu as pltpu

F32 = jnp.float32
BF16 = jnp.bfloat16

D_MODEL = 1024
CHUNK = 64
N_LEFT = 8
HEADS = 8
D_ATT = 512
D_CONV = 512
MAX_REL = 128
N_REL = 2 * MAX_REL + 1
IN_COLS = 6144
EPS = 1e-6
NEG_BIG = -1e30
N_DEV = 8
W_BLK = IN_COLS // N_DEV
QB = 4 * CHUNK
KB = QB + N_LEFT * CHUNK
PADK = N_LEFT * CHUNK
SCALE = 64 ** -0.5
LOG2E = 1.4426950408889634
GATE_COLS = IN_COLS - 3 * D_ATT

ADAM_LR = 0.001
ADAM_B1 = 0.9
ADAM_B2 = 0.999
ADAM_EPS = 1e-08
ADAM_WD = 0.01
ADAM_STEP = 10

VMEM_LIMIT = 56 * 1024 * 1024

MESH = pl.DeviceIdType.MESH
ANY = pl.BlockSpec(memory_space=pl.ANY)


def _params(n_grid, vmem_limit=VMEM_LIMIT):
    return pltpu.CompilerParams(dimension_semantics=("arbitrary",) * n_grid,
                                vmem_limit_bytes=vmem_limit)


def _dot(a, b):
    return jnp.dot(a, b, preferred_element_type=F32)


def _dot_nt(a, b):
    return lax.dot_general(a, b, (((1,), (1,)), ((), ())), preferred_element_type=F32)


def _dot_tn(a, b):
    return lax.dot_general(a, b, (((0,), (0,)), ((), ())), preferred_element_type=F32)


def _sigmoid(z):
    return 0.5 * jnp.tanh(0.5 * z) + 0.5


def _mesh_pos():
    return lax.axis_index("x"), lax.axis_index("y"), lax.axis_index("c")


def _flat(p):
    return 4 * p[0] + 2 * p[1] + p[2]


def _by_core(masks):
    m0, m1 = (jnp.array(m, jnp.int32) for m in masks)
    return jnp.where(lax.axis_index("c") == 0, m0, m1)


GATHER_MASKS = ((0, 1, 4, 3, 2, 5, 6, 7), (0, 1, 2, 5, 4, 3, 6, 7))


def _gather_in_proj(x, norm_g, w_sh, smalls, order):
    S = x.shape[0]
    ts = 1024
    nt = S // ts
    n_small = len(smalls)
    n_steps = N_DEV

    def body(order_ref, x_ref, g_ref, w_hbm, *rest):
        small_in = rest[:n_small]
        p_ref, ht_ref, wg_hbm = rest[n_small:n_small + 3]
        small_out = rest[n_small + 3:2 * n_small + 3]
        (wbuf, hbuf, own_sem, send_sems, recv_sems, out_sems,
         small_send, small_recv, small_local) = rest[2 * n_small + 3:]
        k, i = pl.program_id(0), pl.program_id(1)
        x_, y_, c_ = _mesh_pos()
        me, sibling = (x_, y_, c_), (x_, y_, 1 - c_)
        my = _flat(me)
        chips = [(x_ ^ (1 - c_), y_ ^ c_), (x_ ^ c_, y_ ^ (1 - c_)), (1 - x_, 1 - y_)]
        peers = [sibling] + [(*chip, c_) for chip in chips] + [(*chip, 1 - c_) for chip in chips]

        def wcopy(sem, block, to, from_input=False):
            dst = wbuf.at[_flat(block)]
            return pltpu.make_async_remote_copy(
                src_ref=w_hbm if from_input else dst, dst_ref=dst,
                send_sem=send_sems.at[sem], recv_sem=recv_sems.at[sem], device_id=to, device_id_type=MESH)

        def small_copy(q, a, receive=False):
            slot = _flat(peers[q]) if receive else my
            return pltpu.make_async_remote_copy(
                src_ref=small_in[a], dst_ref=small_out[a].at[slot],
                send_sem=small_send.at[q, a], recv_sem=small_recv.at[q, a],
                device_id=peers[q], device_id_type=MESH)

        def keep(step, block):
            col = pl.multiple_of(_dp_block(_flat(block)) * W_BLK, 128)
            return pltpu.make_async_copy(wbuf.at[_flat(block)], wg_hbm.at[:, pl.ds(col, W_BLK)], out_sems.at[step])

        own = pltpu.make_async_copy(w_hbm, wbuf.at[my], own_sem)
        small_own = [pltpu.make_async_copy(small_in[a], small_out[a].at[my], small_local.at[a])
                     for a in range(n_small)]
        passed_on = [(*chips[1], 1 - c_), (*chips[0], 1 - c_), (*chips[2], 1 - c_)]
        arrivals = [me, sibling]
        for j in range(3):
            arrivals += [(*chips[j], c_), passed_on[j]]

        @pl.when(i == 0)
        def _():
            for kk in range(n_steps):
                @pl.when(k == kk)
                def _():
                    j = kk // 2 - 1
                    if kk == 0:
                        own.start()
                        wcopy(0, me, sibling, True).start()
                        wcopy(1, me, (*chips[0], c_), True).start()
                        own.wait()
                    elif kk == 1:
                        wcopy(0, sibling, me).wait_recv()
                        wcopy(2, me, (*chips[1], c_), True).start()
                    elif kk % 2 == 0:
                        wcopy(1 + j, (*chips[j], c_), me).wait_recv()
                        wcopy(4 + j, (*chips[j], c_), sibling).start()
                        if kk == 2:
                            wcopy(3, me, (*chips[2], c_), True).start()
                    else:
                        wcopy(4 + j, passed_on[j], me).wait_recv()
                        if kk == 3:
                            for cp in small_own:
                                cp.start()
                            for q in range(len(peers)):
                                for a in range(n_small):
                                    small_copy(q, a).start()
                    keep(kk, arrivals[kk]).start()

        row0 = pl.multiple_of(i * ts, ts)

        @pl.when(k == 0)
        def _():
            xf = x_ref[...]
            r = lax.rsqrt(jnp.mean(xf * xf, axis=-1, keepdims=True) + EPS)
            hf = (xf * r) * g_ref[...]
            hbuf[pl.ds(row0, ts), :] = hf.astype(BF16)
            ht_ref[...] = hf.astype(BF16).T

        p_ref[...] = _dot(hbuf[pl.ds(row0, ts), :], wbuf[order_ref[k]]).astype(BF16)

        @pl.when((k == n_steps - 1) & (i == nt - 1))
        def _():
            wcopy(0, me, sibling, True).wait_send()
            for j, chip in enumerate(chips):
                wcopy(1 + j, me, (*chip, c_), True).wait_send()
                wcopy(4 + j, (*chip, c_), sibling).wait_send()
            for kk in range(n_steps):
                keep(kk, arrivals[kk]).wait()
            for cp in small_own:
                cp.wait()
            for q in range(len(peers)):
                for a in range(n_small):
                    small_copy(q, a).wait_send()
                    small_copy(q, a, receive=True).wait_recv()

    first_pass = lambda k, i: jnp.where(k == 0, i, nt - 1)
    grid_spec = pltpu.PrefetchScalarGridSpec(
        num_scalar_prefetch=1, grid=(n_steps, nt),
        in_specs=[pl.BlockSpec((ts, D_MODEL), lambda k, i, o: (first_pass(k, i), 0)),
                  pl.BlockSpec((1, D_MODEL), lambda k, i, o: (0, 0)), ANY] + [ANY] * n_small,
        out_specs=[pl.BlockSpec((ts, W_BLK), lambda k, i, o: (i, o[k])),
                   pl.BlockSpec((D_MODEL, ts), lambda k, i, o: (0, first_pass(k, i))), ANY] + [ANY] * n_small,
        scratch_shapes=[pltpu.VMEM((N_DEV, D_MODEL, W_BLK), BF16), pltpu.VMEM((S, D_MODEL), BF16),
                        pltpu.SemaphoreType.DMA, pltpu.SemaphoreType.DMA((7,)), pltpu.SemaphoreType.DMA((7,)),
                        pltpu.SemaphoreType.DMA((n_steps,)),
                        pltpu.SemaphoreType.DMA((7, n_small)), pltpu.SemaphoreType.DMA((7, n_small)),
                        pltpu.SemaphoreType.DMA((n_small,))])
    return pl.pallas_call(
        body, name="gather_in_proj", grid_spec=grid_spec,
        out_shape=[jax.ShapeDtypeStruct((S, IN_COLS), BF16), jax.ShapeDtypeStruct((D_MODEL, S), BF16),
                   jax.ShapeDtypeStruct((D_MODEL, IN_COLS), BF16)]
        + [jax.ShapeDtypeStruct((N_DEV,) + s.shape, s.dtype) for s in smalls],
        compiler_params=_params(2),
    )(order, x, norm_g, w_sh, *smalls)


def _bias_table(rel_bias):
    wide = 1024

    def body(r_ref, o_ref):
        h = pl.program_id(0)
        col = lax.broadcasted_iota(jnp.int32, (1, wide), 1)
        k_minus_q = jnp.where(col < KB, col, col - wide)
        idx = jnp.clip(PADK - k_minus_q, -MAX_REL, MAX_REL) + MAX_REL
        f = jnp.zeros((1, wide), F32)
        for r in range(MAX_REL - CHUNK + 1, N_REL):
            f = jnp.where(idx == r, r_ref[h, r], f)
        kcol = lax.broadcasted_iota(jnp.int32, (1, KB), 1)
        kc = kcol >> 6
        sub = lax.broadcasted_iota(jnp.int32, (8, 1), 0)
        f8 = jnp.broadcast_to(f * LOG2E, (8, wide))
        base = f8
        for r in range(1, 8):
            base = jnp.where(sub == r, pltpu.roll(f8, r, 1), base)
        for qh in range(0, QB // 8, 2):
            rows = jnp.concatenate([(pltpu.roll(base, 8 * q, 1) if q else base)[:, 0:KB] for q in (qh, qh + 1)],
                                   axis=0)
            qc = (8 * qh) // CHUNK
            band = (kc >= qc) & (kc <= qc + N_LEFT)
            for t in range(3):
                o_ref[t, 0, 8 * qh:8 * qh + 16, :] = jnp.where(
                    band & (kcol >= PADK - t * QB), rows, NEG_BIG).astype(BF16)

    return pl.pallas_call(
        body, name="bias_table", grid=(HEADS,),
        out_shape=jax.ShapeDtypeStruct((3, HEADS, QB, KB), BF16),
        in_specs=[pl.BlockSpec(memory_space=pltpu.SMEM)],
        out_specs=pl.BlockSpec((3, 1, QB, KB), lambda h: (0, h, 0, 0)),
        compiler_params=_params(1),
    )(rel_bias)


KEY_GROUP = 4


def _load_keys(g, nb, p_hbm, kp, vp, sem):
    rows = KEY_GROUP * QB
    n_groups = p_hbm.shape[0] // rows

    def copies(c):
        src = pl.ds(c * rows, rows)
        dst = pl.ds(PADK + c * rows, rows)
        return (pltpu.make_async_copy(p_hbm.at[src, D_ATT:2 * D_ATT], kp.at[dst, :], sem.at[0, c]),
                pltpu.make_async_copy(p_hbm.at[src, 2 * D_ATT:3 * D_ATT], vp.at[dst, :], sem.at[1, c]))

    @pl.when(g == 0)
    def _():
        kp[0:PADK, :] = jnp.zeros((PADK, D_ATT), BF16)
        vp[0:PADK, :] = jnp.zeros((PADK, D_ATT), BF16)
        for c in range(n_groups):
            for cp in copies(c):
                cp.start()

    @pl.when((g % KEY_GROUP == 0) & (g < nb))
    def _():
        for cp in copies(g // KEY_GROUP):
            cp.wait()


def _attn_fwd(P, bias_tab):
    S = P.shape[0]
    nb = S // QB

    def body(q_ref, p_hbm, bias_ref, o_ref, ex_ref, rinv_ref, kp, vp, sem):
        g = pl.program_id(0)
        _load_keys(g, nb, p_hbm, kp, vp, sem)
        start = pl.multiple_of(g * QB, QB)
        lane = lax.broadcasted_iota(jnp.int32, (1, 128), 1)
        half = lambda h: (lane < 64) if h % 2 == 0 else (lane >= 64)
        pair = lambda h: slice(128 * (h // 2), 128 * (h // 2 + 1))

        def scores(h):
            qp = q_ref[:, pair(h)] * SCALE
            qm = jnp.where(half(h), qp, jnp.zeros_like(qp))
            return (_dot_nt(qm, kp[pl.ds(start, KB), pair(h)]) * LOG2E).astype(BF16) + bias_ref[0, h]

        def numerators(h, s):
            ex = jnp.exp2(s - jnp.max(s, axis=-1, keepdims=True))
            ex_ref[:, KB * h:KB * (h + 1)] = ex
            return ex

        def weighted_values(h, ex):
            vpair = vp[pl.ds(start, KB), pair(h)]
            o = _dot(ex, jnp.where(half(h), vpair, jnp.ones_like(vpair)))
            rinv = 1.0 / pltpu.roll(o, 64, 1)
            rinv_ref[:, h:h + 1] = rinv[:, 0:1] if h % 2 == 0 else 1.0 / o[:, 0:1]
            return o * rinv

        outs = []
        s_ahead = {0: scores(0), 1: scores(1)}
        ex_ahead = {0: numerators(0, s_ahead.pop(0))}
        for h in range(HEADS):
            if h + 2 < HEADS:
                s_ahead[h + 2] = scores(h + 2)
            if h + 1 < HEADS:
                ex_ahead[h + 1] = numerators(h + 1, s_ahead.pop(h + 1))
            outs.append(weighted_values(h, ex_ahead.pop(h)))
            if h % 2 == 1:
                o_ref[:, pair(h)] = jnp.where(lane < 64, outs[h - 1], outs[h]).astype(BF16)

    return pl.pallas_call(
        body, name="attn_fwd", grid=(nb,),
        out_shape=[jax.ShapeDtypeStruct((S, D_ATT), BF16), jax.ShapeDtypeStruct((S, HEADS * KB), BF16),
                   jax.ShapeDtypeStruct((S, HEADS), F32)],
        in_specs=[pl.BlockSpec((QB, D_ATT), lambda g: (g, 0)), ANY,
                  pl.BlockSpec((1, HEADS, QB, KB), lambda g: (jnp.minimum(g, 2), 0, 0, 0))],
        out_specs=[pl.BlockSpec((QB, D_ATT), lambda g: (g, 0)),
                   pl.BlockSpec((QB, HEADS * KB), lambda g: (g, 0)),
                   pl.BlockSpec((QB, HEADS), lambda g: (g, 0))],
        scratch_shapes=[pltpu.VMEM((S + PADK, D_ATT), BF16), pltpu.VMEM((S + PADK, D_ATT), BF16),
                        pltpu.SemaphoreType.DMA((2, S // (KEY_GROUP * QB)))],
        compiler_params=_params(1),
    )(P, P, bias_tab)


def _token_local(x, tgt, P, att, proj_g, w_out, cw_g, conv_b, final_g):
    S = x.shape[0]
    ts = 256
    nt = S // ts
    hb = 16

    def body(x_ref, t_ref, s1_ref, s2_ref, s3_ref, h1_ref, h2_ref, att_ref,
             pg_ref, wo_ref, cwg_ref, cb_ref, g2_ref,
             dx2_ref, dg_ref, datt_ref, dwo_ref, dproj_ref, sm1_ref, sm2_ref,
             carry, wao_ref, wco_ref, cw_ref, dwo_acc, dwao_acc, dwco_acc):
        i = pl.program_id(0)
        t = nt - 1 - i

        @pl.when(i == 0)
        def _():
            dwo_acc[...] = jnp.zeros_like(dwo_acc)
            dwao_acc[...] = jnp.zeros_like(dwao_acc)
            dwco_acc[...] = jnp.zeros_like(dwco_acc)
            lane = lax.broadcasted_iota(jnp.int32, (1, 128), 1)
            for j in range(N_DEV):
                wao_ref[:, 128 * j:128 * (j + 1)] = pg_ref[j, :, 0:128]
                wco_ref[:, 128 * j:128 * (j + 1)] = pg_ref[j, :, 128:256]
            for p in range(N_DEV // 2):
                cw_ref[:, 128 * p:128 * (p + 1)] = jnp.where(
                    lane < 64, cwg_ref[2 * p], pltpu.roll(cwg_ref[2 * p + 1], 64, 1))
            sm1_ref[...] = jnp.zeros_like(sm1_ref)
            sm2_ref[...] = jnp.zeros_like(sm2_ref)
            carry[...] = jnp.zeros_like(carry)

        za = s1_ref[:, 0:512]
        gb = s1_ref[:, 512:1024]
        gc = s1_ref[:, 1024:1536].astype(F32)
        u = s2_ref[:, 0:512].astype(F32)
        zc = s2_ref[:, 512:1024]
        ga = jnp.concatenate([s2_ref[:, 1024:1536], s3_ref[:, 0:512]], axis=1)
        gv = s3_ref[:, 512:1536]
        att = att_ref[...]

        sa = _sigmoid(za)
        silu_a = za * sa
        att_g = att * silu_a
        y_att = _dot(att_g, wao_ref[...])

        cu = gc * u
        keep = jnp.where(t > 0, 1.0, 0.0).astype(F32)
        hcu = (h1_ref[hb - 8:hb, 1024:1536].astype(F32) * h2_ref[hb - 8:hb, 0:512].astype(F32)) * keep
        cu_ext = jnp.concatenate([hcu, cu], axis=0)
        cu_m1 = pltpu.roll(cu_ext, 1, 0)[8:]
        cu_m2 = pltpu.roll(cu_ext, 2, 0)[8:]
        w0, w1, w2 = cw_ref[0:1, :], cw_ref[1:2, :], cw_ref[2:3, :]
        vconv = w0 * cu_m2 + w1 * cu_m1 + w2 * cu + cb_ref[...]
        vcb = vconv.astype(BF16)
        sc = _sigmoid(zc)
        silu_c = zc * sc
        cg = gb * vcb * silu_c
        sga = _sigmoid(ga)
        sgv = _sigmoid(gv)
        y_conv = _dot(cg, wco_ref[...])

        yab, ycb = y_att.astype(BF16), y_conv.astype(BF16)
        m = sga * yab + sgv * ycb
        x2 = x_ref[...] + _dot(m, wo_ref[...])
        r2 = lax.rsqrt(jnp.mean(x2 * x2, axis=-1, keepdims=True) + EPS)
        xn2 = x2 * r2
        g2 = g2_ref[...]
        err = xn2 * g2 - t_ref[...]
        sm1_ref[1:2, :] += jnp.sum(err * err, axis=0, keepdims=True) * (0.5 / D_MODEL)

        dy = err * (1.0 / D_MODEL)
        sm1_ref[0:1, :] += jnp.sum(dy * xn2, axis=0, keepdims=True)
        dxn = dy * g2
        dx2 = r2 * (dxn - xn2 * jnp.mean(dxn * xn2, axis=-1, keepdims=True))
        dx2_ref[...] = dx2
        dx2b = dx2.astype(BF16)
        dwo_acc[...] += _dot_tn(m, dx2b)
        dm = _dot_nt(dx2b, wo_ref[...])
        dmb = dm.astype(BF16)
        dya = dmb * sga
        dyc = dmb * sgv
        dg_ref[:, 2560:3584] = dmb * yab * (sga * (1.0 - sga))
        dg_ref[:, 3584:4608] = dmb * ycb * (sgv * (1.0 - sgv))
        dwao_acc[...] += _dot_tn(att_g, dya)
        dwco_acc[...] += _dot_tn(cg, dyc)
        datt_g = _dot_nt(dya, wao_ref[...])
        dcg = _dot_nt(dyc, wco_ref[...])
        dagb, dcgb = datt_g.astype(BF16), dcg.astype(BF16)
        datt_ref[...] = dagb * silu_a
        dg_ref[:, 0:512] = dagb * att * (sa + silu_a * (1.0 - sa))
        dg_ref[:, 512:1024] = dcgb * vcb * silu_c
        dg_ref[:, 2048:2560] = dcgb * gb * vcb * (sc + silu_c * (1.0 - sc))
        dv = dcg * (gb * silu_c).astype(F32)
        sm2_ref[3:4, :] += jnp.sum(dv, axis=0, keepdims=True)
        sm2_ref[0:1, :] += jnp.sum(dv * cu_m2, axis=0, keepdims=True)
        sm2_ref[1:2, :] += jnp.sum(dv * cu_m1, axis=0, keepdims=True)
        sm2_ref[2:3, :] += jnp.sum(dv * cu, axis=0, keepdims=True)
        dv_ext = jnp.concatenate([dv, carry[...]], axis=0)
        dv_p1 = pltpu.roll(dv_ext, ts + 7, 0)[0:ts]
        dv_p2 = pltpu.roll(dv_ext, ts + 6, 0)[0:ts]
        dcu = w2 * dv + w1 * dv_p1 + w0 * dv_p2
        carry[...] = dv[0:8, :]
        dg_ref[:, 1024:1536] = (dcu * u).astype(BF16)
        dg_ref[:, 1536:2048] = (dcu * gc).astype(BF16)

        @pl.when(i == nt - 1)
        def _():
            dwo_ref[...] = dwo_acc[...].astype(BF16)
            for j in range(N_DEV):
                dproj_ref[j, :, 0:128] = dwao_acc[:, 128 * j:128 * (j + 1)].astype(BF16)
                dproj_ref[j, :, 128:256] = dwco_acc[:, 128 * j:128 * (j + 1)].astype(BF16)

    tile = lambda w: pl.BlockSpec((ts, w), lambda i: (nt - 1 - i, 0))
    seg = lambda c: pl.BlockSpec((ts, 1536), lambda i: (nt - 1 - i, c))
    halo = lambda c: pl.BlockSpec((hb, 1536), lambda i: (jnp.maximum((nt - 1 - i) * (ts // hb) - 1, 0), c))
    full = lambda a: pl.BlockSpec(a.shape, lambda i: (0,) * a.ndim)
    acc = lambda r, c: pl.BlockSpec((r, c), lambda i: (0, 0))
    return pl.pallas_call(
        body, name="token_local", grid=(nt,),
        out_shape=[jax.ShapeDtypeStruct((S, D_MODEL), F32), jax.ShapeDtypeStruct((S, IN_COLS), BF16),
                   jax.ShapeDtypeStruct((S, D_ATT), BF16), jax.ShapeDtypeStruct((D_MODEL, D_MODEL), BF16),
                   jax.ShapeDtypeStruct(proj_g.shape, BF16),
                   jax.ShapeDtypeStruct((8, D_MODEL), F32), jax.ShapeDtypeStruct((8, D_CONV), F32)],
        in_specs=[tile(D_MODEL), tile(D_MODEL), seg(1), seg(2), seg(3), halo(1), halo(2), tile(D_ATT),
                  full(proj_g), full(w_out), full(cw_g), full(conv_b), full(final_g)],
        out_specs=[tile(D_MODEL), tile(GATE_COLS), tile(D_ATT), acc(D_MODEL, D_MODEL), full(proj_g),
                   acc(8, D_MODEL), acc(8, D_CONV)],
        scratch_shapes=[pltpu.VMEM((8, D_CONV), F32),
                        pltpu.VMEM((D_ATT, D_MODEL), BF16), pltpu.VMEM((D_CONV, D_MODEL), BF16),
                        pltpu.VMEM((8, D_CONV), F32), pltpu.VMEM((D_MODEL, D_MODEL), F32),
                        pltpu.VMEM((D_ATT, D_MODEL), F32), pltpu.VMEM((D_CONV, D_MODEL), F32)],
        compiler_params=_params(1),
    )(x, tgt, P, P, P, P, P, att, proj_g, w_out, cw_g, conv_b, final_g)


def _fold_diagonals(d_ref, o_ref):
    wide = D_MODEL
    sub = lax.broadcasted_iota(jnp.int32, (8, 1), 0)
    col = lax.broadcasted_iota(jnp.int32, (1, wide), 1)
    pad = jnp.zeros((8, wide - KB), F32)
    for h in range(HEADS):
        acc = jnp.concatenate([d_ref[h, 0:8, :], pad], axis=1)
        for qh in range(1, QB // 8):
            a = jnp.concatenate([d_ref[h, 8 * qh:8 * qh + 8, :], pad], axis=1)
            acc = acc + pltpu.roll(a, wide - 8 * qh, 1)
        for r in range(1, 8):
            acc = jnp.where(sub == r, pltpu.roll(acc, wide - r, 1), acc)
        vec = jnp.sum(acc, axis=0, keepdims=True)
        far = (col <= PADK - MAX_REL) | (col > KB)
        tail = jnp.sum(jnp.where(far, vec, 0.0), axis=-1, keepdims=True)
        o_ref[h:h + 1, :] = jnp.where(col == wide - 1, tail, vec)


def _attn_bwd(P, att, datt, ex, rinv, dP):
    S = P.shape[0]
    nb = S // QB

    def body(q_ref, att_ref, datt_ref, ex_ref, rinv_ref, p_hbm, dp_hbm, out_ref, dbias_ref,
             kp, vp, dq_ring, dk_ring, dv_ring, db_ref, sem):
        g = pl.program_id(0)

        _load_keys(g, nb, p_hbm, kp, vp, sem)

        @pl.when(g == 0)
        def _():
            db_ref[...] = jnp.zeros_like(db_ref)
            dk_ring[...] = jnp.zeros_like(dk_ring)
            dv_ring[...] = jnp.zeros_like(dv_ring)

        s_new = g % 3
        s_mid = (g + 2) % 3
        s_old = (g + 1) % 3

        @pl.when(g < nb)
        def _():
            start = pl.multiple_of(g * QB, QB)
            lane = lax.broadcasted_iota(jnp.int32, (1, 128), 1)
            for p in range(HEADS // 2):
                cols = slice(128 * p, 128 * (p + 1))
                qp = q_ref[:, cols] * SCALE
                op = att_ref[:, cols].astype(F32)
                dop = datt_ref[:, cols]
                kpair = kp[pl.ds(start, KB), cols]
                vpair = vp[pl.ds(start, KB), cols]
                dqs = []
                dk_acc = jnp.zeros((KB, 128), F32)
                dv_acc = jnp.zeros((KB, 128), F32)
                for e in range(2):
                    h = 2 * p + e
                    lm = (lane < 64) if e == 0 else (lane >= 64)
                    qm = jnp.where(lm, qp, jnp.zeros_like(qp))
                    dom = jnp.where(lm, dop, jnp.zeros_like(dop))
                    exh = ex_ref[:, KB * h:KB * (h + 1)]
                    rinv = rinv_ref[:, h:h + 1]
                    domf = dom.astype(F32)
                    dp = _dot_nt(dom, vpair)
                    delta = jnp.sum(domf * op, axis=-1, keepdims=True)
                    dsb = exh * ((dp - delta) * rinv).astype(BF16)
                    db_ref[h] += dsb.astype(F32)
                    dqs.append(_dot(dsb, kpair) * SCALE)
                    dk_acc = dk_acc + _dot_tn(dsb, qm)
                    dv_acc = dv_acc + _dot_tn(exh, (domf * rinv).astype(BF16))
                dq_ring[s_new, :, cols] = jnp.where(lane < 64, dqs[0], dqs[1])
                dk_ring[s_old, :, cols] += dk_acc[0:QB]
                dk_ring[s_mid, :, cols] += dk_acc[QB:2 * QB]
                dk_ring[s_new, :, cols] = dk_acc[2 * QB:3 * QB]
                dv_ring[s_old, :, cols] += dv_acc[0:QB]
                dv_ring[s_mid, :, cols] += dv_acc[QB:2 * QB]
                dv_ring[s_new, :, cols] = dv_acc[2 * QB:3 * QB]

        @pl.when(g >= 2)
        def _():
            out_ref[:, 0:D_ATT] = dq_ring[s_old].astype(BF16)
            out_ref[:, D_ATT:2 * D_ATT] = dk_ring[s_old].astype(BF16)
            out_ref[:, 2 * D_ATT:3 * D_ATT] = dv_ring[s_old].astype(BF16)

        @pl.when(g == nb + 1)
        def _():
            _fold_diagonals(db_ref, dbias_ref)

    qblk = lambda w: pl.BlockSpec((QB, w), lambda g: (jnp.minimum(g, nb - 1), 0))
    return pl.pallas_call(
        body, name="attn_bwd", grid=(nb + 2,),
        out_shape=[jax.ShapeDtypeStruct((S, IN_COLS), BF16), jax.ShapeDtypeStruct((HEADS, D_MODEL), F32)],
        in_specs=[qblk(D_ATT), qblk(D_ATT), qblk(D_ATT), qblk(HEADS * KB), qblk(HEADS), ANY, ANY],
        out_specs=[pl.BlockSpec((QB, 3 * D_ATT), lambda g: (jnp.maximum(g - 2, 0), GATE_COLS // (3 * D_ATT))),
                   pl.BlockSpec((HEADS, D_MODEL), lambda g: (0, 0))],
        input_output_aliases={6: 0},
        scratch_shapes=[pltpu.VMEM((S + PADK, D_ATT), BF16), pltpu.VMEM((S + PADK, D_ATT), BF16),
                        pltpu.VMEM((3, QB, D_ATT), F32), pltpu.VMEM((3, QB, D_ATT), F32),
                        pltpu.VMEM((3, QB, D_ATT), F32), pltpu.VMEM((HEADS, QB, KB), F32),
                        pltpu.SemaphoreType.DMA((2, S // (KEY_GROUP * QB)))],
        compiler_params=_params(1),
    )(P, att, datt, ex, rinv, P, dP)


def _dp_block(j):
    return (j + GATE_COLS // W_BLK) % N_DEV


def _in_proj_bwd(x, norm_g, dx2, dP, w_in_g):
    S = x.shape[0]
    ts = 512

    def body(x_ref, g_ref, dx2_ref, dp_ref, w_ref, gx_ref, dn_ref):
        @pl.when(pl.program_id(0) == 0)
        def _():
            dn_ref[...] = jnp.zeros_like(dn_ref)

        dh = _dot_nt(dp_ref[...], w_ref[...])
        xf = x_ref[...]
        r = lax.rsqrt(jnp.mean(xf * xf, axis=-1, keepdims=True) + EPS)
        xn = xf * r
        dn_ref[0:1, :] += jnp.sum(dh * xn, axis=0, keepdims=True)
        dhg = dh * g_ref[...]
        gx_ref[...] = dx2_ref[...] + r * (dhg - xn * jnp.mean(dhg * xn, axis=-1, keepdims=True))

    tile = lambda w: pl.BlockSpec((ts, w), lambda i: (i, 0))
    return pl.pallas_call(
        body, name="in_proj_bwd", grid=(S // ts,),
        out_shape=[jax.ShapeDtypeStruct((S, D_MODEL), F32), jax.ShapeDtypeStruct((8, D_MODEL), F32)],
        in_specs=[tile(D_MODEL), pl.BlockSpec((1, D_MODEL), lambda i: (0, 0)), tile(D_MODEL),
                  tile(IN_COLS),
                  pl.BlockSpec((D_MODEL, IN_COLS), lambda i: (0, 0))],
        out_specs=[tile(D_MODEL), pl.BlockSpec((8, D_MODEL), lambda i: (0, 0))],
        compiler_params=_params(1),
    )(x, norm_g, dx2, dP, w_in_g)


SCATTER_MASKS = ((3, 4, 5, 2, 7, 6, 1, 0), (5, 2, 3, 4, 7, 6, 1, 0))


def _w_in_grad_scatter(ht, dP, d_proj, d_wo, pack, order):
    S = ht.shape[1]
    ts = min(S, 2048)
    nt = S // ts
    n_steps = 8

    def body(order_ref, ht_ref, d_ref, proj_hbm, wo_hbm, pack_hbm, g_ref, rproj, rwo, rpack,
             acc, stage, rsib, rici, d2d_send, d2d_recv, ici_send, ici_recv, small_send, small_recv, local_sems):
        k, i = pl.program_id(0), pl.program_id(1)
        x, y, c = _mesh_pos()
        my = _flat((x, y, c))
        sibling = (x, y, 1 - c)
        owners = [(x ^ (1 - c), y ^ c, c), (x ^ c, y ^ (1 - c), c), (1 - x, 1 - y, c)]
        peers = [sibling, (1 - x, y, c), (x, 1 - y, c), (1 - x, 1 - y, c),
                 (1 - x, y, 1 - c), (x, 1 - y, 1 - c), (1 - x, 1 - y, 1 - c)]
        small = ((proj_hbm, rproj, True), (wo_hbm, rwo, True), (pack_hbm, rpack, False))
        n_small = len(small)

        def small_copy(kk, a, receive=False):
            src, dst, per_peer = small[a]
            slot = _flat(peers[kk]) if receive else my
            return pltpu.make_async_remote_copy(
                src_ref=src.at[_flat(peers[kk])] if per_peer else src, dst_ref=dst.at[slot],
                send_sem=small_send.at[kk, a], recv_sem=small_recv.at[kk, a],
                device_id=peers[kk], device_id_type=MESH)

        def to_sibling(t):
            return pltpu.make_async_remote_copy(
                src_ref=stage.at[0], dst_ref=rsib.at[t % 2], send_sem=d2d_send.at[t], recv_sem=d2d_recv.at[t],
                device_id=sibling, device_id_type=MESH)

        def to_owner(t):
            return pltpu.make_async_remote_copy(
                src_ref=stage.at[1], dst_ref=rici.at[t], send_sem=ici_send.at[t], recv_sem=ici_recv.at[t],
                device_id=owners[t], device_id_type=MESH)

        own_small = [pltpu.make_async_copy(src.at[my] if per_peer else src, dst.at[my], local_sems.at[a])
                     for a, (src, dst, per_peer) in enumerate(small)]

        @pl.when((k == 0) & (i == 0))
        def _():
            for cp in own_small:
                cp.start()
            for kk in range(len(peers)):
                for a in range(n_small):
                    small_copy(kk, a).start()

        @pl.when(i == 0)
        def _():
            acc[...] = jnp.zeros_like(acc)

        acc[...] += _dot(ht_ref[...], d_ref[...])

        @pl.when(i == nt - 1)
        def _():
            for s in range(n_steps):
                @pl.when(k == s)
                def _():
                    t = s // 2
                    if s % 2 == 0:
                        if t >= 1:
                            to_sibling(t - 1).wait_send()
                        stage[0] = acc[...].astype(BF16)
                        to_sibling(t).start()
                    elif t < 3:
                        if t >= 1:
                            to_owner(t - 1).wait_send()
                        to_sibling(t).wait_recv()
                        stage[1] = (acc[...] + rsib[t % 2].astype(F32)).astype(BF16)
                        to_owner(t).start()
                    else:
                        to_sibling(t).wait_recv()
                        total = acc[...] + rsib[t % 2].astype(F32)
                        for j in range(3):
                            to_owner(j).wait_recv()
                            total = total + rici[j].astype(F32)
                        g_ref[...] = total
                        to_owner(2).wait_send()
                        to_sibling(3).wait_send()
                        for q in range(len(peers)):
                            for a in range(n_small):
                                small_copy(q, a).wait_send()
                                small_copy(q, a, receive=True).wait_recv()
                        for cp in own_small:
                            cp.wait()

    blk = (D_MODEL, W_BLK)
    grid_spec = pltpu.PrefetchScalarGridSpec(
        num_scalar_prefetch=1, grid=(n_steps, nt),
        in_specs=[pl.BlockSpec((D_MODEL, ts), lambda k, i, o: (0, i)),
                  pl.BlockSpec((ts, W_BLK), lambda k, i, o: (i, _dp_block(o[k]))),
                  ANY, ANY, ANY],
        out_specs=[pl.BlockSpec(blk, lambda k, i, o: (0, 0)), ANY, ANY, ANY],
        scratch_shapes=[pltpu.VMEM(blk, F32), pltpu.VMEM((2,) + blk, BF16),
                        pltpu.VMEM((2,) + blk, BF16), pltpu.VMEM((3,) + blk, BF16),
                        pltpu.SemaphoreType.DMA((4,)), pltpu.SemaphoreType.DMA((4,)),
                        pltpu.SemaphoreType.DMA((3,)), pltpu.SemaphoreType.DMA((3,)),
                        pltpu.SemaphoreType.DMA((7, 3)), pltpu.SemaphoreType.DMA((7, 3)),
                        pltpu.SemaphoreType.DMA((3,))])
    return pl.pallas_call(
        body, name="w_in_grad_scatter", grid_spec=grid_spec,
        out_shape=[jax.ShapeDtypeStruct(blk, F32),
                   jax.ShapeDtypeStruct(d_proj.shape, BF16), jax.ShapeDtypeStruct(d_wo.shape, BF16),
                   jax.ShapeDtypeStruct((N_DEV,) + pack.shape, F32)],
        compiler_params=_params(2),
    )(order, ht, dP, d_proj, d_wo, pack)


def _adamw(w, g, m, v):
    m = ADAM_B1 * m + (1.0 - ADAM_B1) * g
    v = ADAM_B2 * v + (1.0 - ADAM_B2) * (g * g)
    m_hat = m / (1.0 - ADAM_B1 ** ADAM_STEP)
    v_hat = v / (1.0 - ADAM_B2 ** ADAM_STEP)
    delta = -ADAM_LR * (m_hat / (jnp.sqrt(v_hat) + ADAM_EPS) + ADAM_WD * w)
    return delta, m, v


def _adamw_all(g_w_in, w_in_params, r_proj, r_wo, mid_params, r_pack, small_params):
    R, C = g_w_in.shape
    tr = min(R, 256)
    wide = 384

    def update(params, n, g, outs):
        w, m, v = (r[...] for r in params[3 * n:3 * n + 3])
        outs[4 * n][...] = g
        outs[4 * n + 1][...], outs[4 * n + 2][...], outs[4 * n + 3][...] = _adamw(w, g, m, v)

    def body(*refs):
        g_ref, w_ref, m_ref, v_ref, rp_ref, rw_ref = refs[:6]
        mid_in, pack_ref, small_in, outs = refs[6:15], refs[15], refs[16:31], refs[31:]
        win_out, mid_out, loss_ref, small_out = outs[0:3], outs[3:15], outs[15], outs[16:]
        win_out[0][...], win_out[1][...], win_out[2][...] = _adamw(w_ref[...], g_ref[...], m_ref[...], v_ref[...])

        @pl.when(pl.program_id(0) == 0)
        def _():
            def total(part):
                g = part(0).astype(F32)
                for s in range(1, N_DEV):
                    g = g + part(s).astype(F32)
                return g

            for n, g in enumerate((total(lambda s: rp_ref[s, :, 0:128]), total(lambda s: rp_ref[s, :, 128:256]),
                                   total(lambda s: rw_ref[s]))):
                update(mid_in, n, g, mid_out)

            tot = total(lambda s: pack_ref[s])
            me = _flat(_mesh_pos())
            loss_ref[...] = jnp.sum(tot[2:3, :], axis=-1, keepdims=True)
            mine = pltpu.roll(tot[0:8, 0:D_CONV], (D_CONV - 64 * me) % D_CONV, 1)
            col = lax.broadcasted_iota(jnp.int32, (D_MODEL, wide), 0)
            idx = lax.broadcasted_iota(jnp.int32, (D_MODEL, wide), 1)
            near = (idx > MAX_REL - CHUNK) & (idx < 2 * MAX_REL) & (col == PADK + MAX_REL - idx)
            far = (idx == 2 * MAX_REL) & (col == D_MODEL - 1)
            perm = jnp.where(near | far, 1.0, 0.0).astype(F32)
            g_rel = jnp.dot(tot[8:16], perm, precision=lax.Precision.HIGHEST, preferred_element_type=F32)
            for n, g in enumerate((tot[0:1], tot[1:2], mine[3:6, 0:64], tot[6:7, 0:D_CONV], g_rel[:, 0:N_REL])):
                update(small_in, n, g, small_out)

    tile = pl.BlockSpec((tr, C), lambda i: (i, 0))
    full = lambda a: pl.BlockSpec(a.shape, lambda i: (0,) * len(a.shape))
    whole = [r_proj, r_wo, *mid_params, r_pack, *small_params]
    out_shape = ([jax.ShapeDtypeStruct((R, C), F32)] * 3
                 + [jax.ShapeDtypeStruct(mid_params[3 * n].shape, F32) for n in range(3) for _ in range(4)]
                 + [jax.ShapeDtypeStruct((1, 1), F32)]
                 + [jax.ShapeDtypeStruct(small_params[3 * n].shape, F32) for n in range(5) for _ in range(4)])
    return pl.pallas_call(
        body, name="adamw_all", grid=(R // tr,),
        out_shape=out_shape,
        in_specs=[tile] * 4 + [full(a) for a in whole],
        out_specs=[tile] * 3 + [full(o) for o in out_shape[3:]],
        compiler_params=_params(1),
    )(g_w_in, *w_in_params, *whole)


def _pad_row(a, width=D_MODEL):
    a = a.reshape(-1, a.shape[-1])
    return jnp.pad(a, ((0, 0), (0, width - a.shape[-1])))


def kernel(x, norm_g, w_in, rel_bias, w_att_out, conv_w, conv_b, w_conv_out, w_out, final_norm_g, loss_target, m_norm_g, m_w_in, m_rel_bias, m_w_att_out, m_conv_w, m_conv_b, m_w_conv_out, m_w_out, m_final_norm_g, v_norm_g, v_w_in, v_rel_bias, v_w_att_out, v_conv_w, v_conv_b, v_w_conv_out, v_w_out, v_final_norm_g):
    S = x.shape[1]
    x2d = x.reshape(S, D_MODEL)
    tgt = loss_target.reshape(S, D_MODEL)
    me = 4 * lax.axis_index("x") + 2 * lax.axis_index("y") + lax.axis_index("c")
    row = lambda a: a.reshape(1, D_MODEL)

    proj_sh = jnp.concatenate([w_att_out[0], w_conv_out[0]], axis=1).astype(BF16)
    cw_sh = jnp.pad(conv_w[0], ((0, 5), (0, 64)))
    P, ht, w_in_g, proj_g, w_out_g, cw_g = _gather_in_proj(
        x2d, norm_g, w_in[0].astype(BF16), [proj_sh, w_out[0].astype(BF16), cw_sh],
        me ^ _by_core(GATHER_MASKS))

    bias_tab = _bias_table(rel_bias[0])
    att, ex, rinv = _attn_fwd(P, bias_tab)
    dx2, dP, datt, d_wo, d_proj, sm1, sm2 = _token_local(
        x2d, tgt, P, att, proj_g, w_out_g.reshape(D_MODEL, D_MODEL), cw_g, conv_b, row(final_norm_g))
    dP, dbias = _attn_bwd(P, att, datt, ex, rinv, dP)
    grad_x, dnorm = _in_proj_bwd(x2d, norm_g, dx2, dP, w_in_g)

    pack = jnp.concatenate([dnorm[0:1], sm1[0:2], _pad_row(sm2[0:4]), jnp.zeros((1, D_MODEL), F32), dbias],
                           axis=0)
    g_win_sum, r_proj, r_wo, r_pack = _w_in_grad_scatter(
        ht, dP, d_proj, d_wo.reshape(N_DEV, 128, D_MODEL), pack, me ^ _by_core(SCATTER_MASKS))

    upd = _adamw_all(g_win_sum, (w_in[0], m_w_in[0], v_w_in[0]), r_proj, r_wo,
                     (w_att_out[0], m_w_att_out[0], v_w_att_out[0], w_conv_out[0], m_w_conv_out[0], v_w_conv_out[0],
                      w_out[0], m_w_out[0], v_w_out[0]),
                     r_pack,
                     (norm_g, m_norm_g, v_norm_g, row(final_norm_g), row(m_final_norm_g), row(v_final_norm_g),
                      conv_w[0], m_conv_w[0], v_conv_w[0], conv_b, m_conv_b, v_conv_b,
                      rel_bias[0], m_rel_bias[0], v_rel_bias[0]))
    res = {"w_in": (g_win_sum, *upd[0:3])}
    for n, name in enumerate(("w_att_out", "w_conv_out", "w_out")):
        res[name] = upd[3 + 4 * n:7 + 4 * n]
    loss = upd[15].reshape(())
    for n, name in enumerate(("norm_g", "final_norm_g", "conv_w", "conv_b", "rel_bias")):
        res[name] = upd[16 + 4 * n:20 + 4 * n]

    leading = {"norm_g": (1, D_MODEL), "final_norm_g": (D_MODEL,), "conv_b": (1, D_CONV)}
    outs = []
    for kind in range(4):
        for name in ("norm_g", "w_in", "rel_bias", "w_att_out", "conv_w", "conv_b", "w_conv_out", "w_out",
                     "final_norm_g"):
            a = res[name][kind]
            outs.append(a.reshape(leading[name]) if name in leading else a[None])
    return (loss, grad_x.reshape(1, S, D_MODEL), *outs)
```

```python
import functools

import numpy as np
import jax
import jax.numpy as jnp
from jax import lax
from jax.experimental import pallas as pl
from jax.experimental.pallas import tpu as pltpu

F32 = jnp.float32
BF16 = jnp.bfloat16

D_MODEL = 1024
CHUNK = 64
N_LEFT = 8
HEADS = 8
D_ATT = 512
D_CONV = 512
MAX_REL = 128
N_REL = 2 * MAX_REL + 1
IN_COLS = 6144
EPS = 1e-6
NEG_BIG = -1e30
N_DEV = 8
W_BLK = IN_COLS // N_DEV
QB = 4 * CHUNK
KB = QB + N_LEFT * CHUNK
PADK = N_LEFT * CHUNK
SCALE = 64 ** -0.5
LOG2E = 1.4426950408889634
GATE_COLS = IN_COLS - 3 * D_ATT

ADAM_LR = 0.001
ADAM_B1 = 0.9
ADAM_B2 = 0.999
ADAM_EPS = 1e-08
ADAM_WD = 0.01
ADAM_STEP = 10

VMEM_LIMIT = 56 * 1024 * 1024

MESH = pl.DeviceIdType.MESH
ANY = pl.BlockSpec(memory_space=pl.ANY)


def _params(n_grid, vmem_limit=VMEM_LIMIT):
    return pltpu.CompilerParams(dimension_semantics=("arbitrary",) * n_grid,
                                vmem_limit_bytes=vmem_limit)


def _dot(a, b):
    return jnp.dot(a, b, preferred_element_type=F32)


def _dot_nt(a, b):
    return lax.dot_general(a, b, (((1,), (1,)), ((), ())), preferred_element_type=F32)


def _dot_tn(a, b):
    return lax.dot_general(a, b, (((0,), (0,)), ((), ())), preferred_element_type=F32)


def _sigmoid(z):
    return 0.5 * jnp.tanh(0.5 * z) + 0.5


def _mesh_pos():
    return lax.axis_index("x"), lax.axis_index("y"), lax.axis_index("c")


def _flat(p):
    return 4 * p[0] + 2 * p[1] + p[2]


def _by_core(masks):
    m0, m1 = (jnp.array(m, jnp.int32) for m in masks)
    return jnp.where(lax.axis_index("c") == 0, m0, m1)


GATHER_MASKS = ((0, 1, 4, 3, 2, 5, 6, 7), (0, 1, 2, 5, 4, 3, 6, 7))


def _gather_in_proj(x, norm_g, w_sh, smalls, order):
    S = x.shape[0]
    ts = 1024
    nt = S // ts
    n_small = len(smalls)
    n_steps = N_DEV

    def body(order_ref, x_ref, g_ref, w_hbm, *rest):
        small_in = rest[:n_small]
        p_ref, ht_ref, wg_hbm = rest[n_small:n_small + 3]
        small_out = rest[n_small + 3:2 * n_small + 3]
        (wbuf, hbuf, own_sem, send_sems, recv_sems, out_sems,
         small_send, small_recv, small_local) = rest[2 * n_small + 3:]
        k, i = pl.program_id(0), pl.program_id(1)
        x_, y_, c_ = _mesh_pos()
        me, sibling = (x_, y_, c_), (x_, y_, 1 - c_)
        my = _flat(me)
        chips = [(x_ ^ (1 - c_), y_ ^ c_), (x_ ^ c_, y_ ^ (1 - c_)), (1 - x_, 1 - y_)]
        peers = [sibling] + [(*chip, c_) for chip in chips] + [(*chip, 1 - c_) for chip in chips]

        def wcopy(sem, block, to, from_input=False):
            dst = wbuf.at[_flat(block)]
            return pltpu.make_async_remote_copy(
                src_ref=w_hbm if from_input else dst, dst_ref=dst,
                send_sem=send_sems.at[sem], recv_sem=recv_sems.at[sem], device_id=to, device_id_type=MESH)

        def small_copy(q, a, receive=False):
            slot = _flat(peers[q]) if receive else my
            return pltpu.make_async_remote_copy(
                src_ref=small_in[a], dst_ref=small_out[a].at[slot],
                send_sem=small_send.at[q, a], recv_sem=small_recv.at[q, a],
                device_id=peers[q], device_id_type=MESH)

        def keep(step, block):
            col = pl.multiple_of(_dp_block(_flat(block)) * W_BLK, 128)
            return pltpu.make_async_copy(wbuf.at[_flat(block)], wg_hbm.at[:, pl.ds(col, W_BLK)], out_sems.at[step])

        own = pltpu.make_async_copy(w_hbm, wbuf.at[my], own_sem)
        small_own = [pltpu.make_async_copy(small_in[a], small_out[a].at[my], small_local.at[a])
                     for a in range(n_small)]
        passed_on = [(*chips[1], 1 - c_), (*chips[0], 1 - c_), (*chips[2], 1 - c_)]
        arrivals = [me, sibling]
        for j in range(3):
            arrivals += [(*chips[j], c_), passed_on[j]]

        @pl.when(i == 0)
        def _():
            for kk in range(n_steps):
                @pl.when(k == kk)
                def _():
                    j = kk // 2 - 1
                    if kk == 0:
                        own.start()
                        wcopy(0, me, sibling, True).start()
                        wcopy(1, me, (*chips[0], c_), True).start()
                        own.wait()
                    elif kk == 1:
                        wcopy(0, sibling, me).wait_recv()
                        wcopy(2, me, (*chips[1], c_), True).start()
                    elif kk % 2 == 0:
                        wcopy(1 + j, (*chips[j], c_), me).wait_recv()
                        wcopy(4 + j, (*chips[j], c_), sibling).start()
                        if kk == 2:
                            wcopy(3, me, (*chips[2], c_), True).start()
                    else:
                        wcopy(4 + j, passed_on[j], me).wait_recv()
                        if kk == 3:
                            for cp in small_own:
                                cp.start()
                            for q in range(len(peers)):
                                for a in range(n_small):
                                    small_copy(q, a).start()
                    keep(kk, arrivals[kk]).start()

        row0 = pl.multiple_of(i * ts, ts)

        @pl.when(k == 0)
        def _():
            xf = x_ref[...]
            r = lax.rsqrt(jnp.mean(xf * xf, axis=-1, keepdims=True) + EPS)
            hf = (xf * r) * g_ref[...]
            hbuf[pl.ds(row0, ts), :] = hf.astype(BF16)
            ht_ref[...] = hf.astype(BF16).T

        p_ref[...] = _dot(hbuf[pl.ds(row0, ts), :], wbuf[order_ref[k]]).astype(BF16)

        @pl.when((k == n_steps - 1) & (i == nt - 1))
        def _():
            wcopy(0, me, sibling, True).wait_send()
            for j, chip in enumerate(chips):
                wcopy(1 + j, me, (*chip, c_), True).wait_send()
                wcopy(4 + j, (*chip, c_), sibling).wait_send()
            for kk in range(n_steps):
                keep(kk, arrivals[kk]).wait()
            for cp in small_own:
                cp.wait()
            for q in range(len(peers)):
                for a in range(n_small):
                    small_copy(q, a).wait_send()
                    small_copy(q, a, receive=True).wait_recv()

    first_pass = lambda k, i: jnp.where(k == 0, i, nt - 1)
    grid_spec = pltpu.PrefetchScalarGridSpec(
        num_scalar_prefetch=1, grid=(n_steps, nt),
        in_specs=[pl.BlockSpec((ts, D_MODEL), lambda k, i, o: (first_pass(k, i), 0)),
                  pl.BlockSpec((1, D_MODEL), lambda k, i, o: (0, 0)), ANY] + [ANY] * n_small,
        out_specs=[pl.BlockSpec((ts, W_BLK), lambda k, i, o: (i, o[k])),
                   pl.BlockSpec((D_MODEL, ts), lambda k, i, o: (0, first_pass(k, i))), ANY] + [ANY] * n_small,
        scratch_shapes=[pltpu.VMEM((N_DEV, D_MODEL, W_BLK), BF16), pltpu.VMEM((S, D_MODEL), BF16),
                        pltpu.SemaphoreType.DMA, pltpu.SemaphoreType.DMA((7,)), pltpu.SemaphoreType.DMA((7,)),
                        pltpu.SemaphoreType.DMA((n_steps,)),
                        pltpu.SemaphoreType.DMA((7, n_small)), pltpu.SemaphoreType.DMA((7, n_small)),
                        pltpu.SemaphoreType.DMA((n_small,))])
    return pl.pallas_call(
        body, name="gather_in_proj", grid_spec=grid_spec,
        out_shape=[jax.ShapeDtypeStruct((S, IN_COLS), BF16), jax.ShapeDtypeStruct((D_MODEL, S), BF16),
                   jax.ShapeDtypeStruct((D_MODEL, IN_COLS), BF16)]
        + [jax.ShapeDtypeStruct((N_DEV,) + s.shape, s.dtype) for s in smalls],
        compiler_params=_params(2),
    )(order, x, norm_g, w_sh, *smalls)


def _bias_table(rel_bias):
    wide = 1024
    n_rel = 384

    def body(r_ref, o_ref):
        col = lax.broadcasted_iota(jnp.int32, (1, wide), 1)
        k_minus_q = jnp.where(col < KB, col, col - wide)
        idx = jnp.clip(PADK - k_minus_q, -MAX_REL, MAX_REL) + MAX_REL
        pick = jnp.where(lax.broadcasted_iota(jnp.int32, (n_rel, wide), 0) == idx, 1.0, 0.0).astype(F32)
        f = jnp.dot(r_ref[...], pick, precision=lax.Precision.HIGHEST, preferred_element_type=F32) * LOG2E
        kcol = lax.broadcasted_iota(jnp.int32, (1, KB), 1)
        kc = kcol >> 6
        sub = lax.broadcasted_iota(jnp.int32, (8, 1), 0)
        for h in range(HEADS):
            f8 = jnp.broadcast_to(f[h:h + 1, :], (8, wide))
            base = f8
            for r in range(1, 8):
                base = jnp.where(sub == r, pltpu.roll(f8, r, 1), base)
            for qh in range(0, QB // 8, 2):
                rows = jnp.concatenate(
                    [(pltpu.roll(base, 8 * q, 1) if q else base)[:, 0:KB] for q in (qh, qh + 1)], axis=0)
                qc = (8 * qh) // CHUNK
                band = (kc >= qc) & (kc <= qc + N_LEFT)
                o_ref[h, 8 * qh:8 * qh + 16, :] = jnp.where(band, rows, NEG_BIG).astype(BF16)

    return pl.pallas_call(
        body, name="bias_table",
        out_shape=jax.ShapeDtypeStruct((HEADS, QB, KB), BF16),
        compiler_params=pltpu.CompilerParams(vmem_limit_bytes=VMEM_LIMIT),
    )(jnp.pad(rel_bias, ((0, 0), (0, n_rel - N_REL))))


KEY_GROUP = 4


def _load_keys(g, nb, p_hbm, kp, vp, sem):
    rows = KEY_GROUP * QB
    n_groups = p_hbm.shape[0] // rows

    def copies(c):
        src = pl.ds(c * rows, rows)
        dst = pl.ds(PADK + c * rows, rows)
        return (pltpu.make_async_copy(p_hbm.at[src, D_ATT:2 * D_ATT], kp.at[dst, :], sem.at[0, c]),
                pltpu.make_async_copy(p_hbm.at[src, 2 * D_ATT:3 * D_ATT], vp.at[dst, :], sem.at[1, c]))

    @pl.when(g == 0)
    def _():
        kp[0:PADK, :] = jnp.zeros((PADK, D_ATT), BF16)
        vp[0:PADK, :] = jnp.zeros((PADK, D_ATT), BF16)
        for c in range(n_groups):
            for cp in copies(c):
                cp.start()

    @pl.when((g % KEY_GROUP == 0) & (g < nb))
    def _():
        for cp in copies(g // KEY_GROUP):
            cp.wait()


def _attn_fwd(P, bias_tab):
    S = P.shape[0]
    nb = S // QB

    def body(q_ref, p_hbm, bias_ref, o_ref, ex_ref, rinv_ref, kp, vp, tab, sem):
        g = pl.program_id(0)
        _load_keys(g, nb, p_hbm, kp, vp, sem)

        @pl.when(g * QB <= PADK)
        def _():
            kcol = lax.broadcasted_iota(jnp.int32, (1, KB), 1)
            for h in range(HEADS):
                tab[h] = jnp.where(kcol + g * QB >= PADK, bias_ref[h], NEG_BIG).astype(BF16)

        start = pl.multiple_of(g * QB, QB)
        lane = lax.broadcasted_iota(jnp.int32, (1, 128), 1)
        half = lambda h: (lane < 64) if h % 2 == 0 else (lane >= 64)
        pair = lambda h: slice(128 * (h // 2), 128 * (h // 2 + 1))

        def scores(h):
            qp = q_ref[:, pair(h)] * SCALE
            qm = jnp.where(half(h), qp, jnp.zeros_like(qp))
            return (_dot_nt(qm, kp[pl.ds(start, KB), pair(h)]) * LOG2E).astype(BF16) + tab[h]

        def numerators(h, s):
            ex = jnp.exp2(s - jnp.max(s, axis=-1, keepdims=True))
            ex_ref[:, KB * h:KB * (h + 1)] = ex
            return ex

        def weighted_values(h, ex):
            vpair = vp[pl.ds(start, KB), pair(h)]
            o = _dot(ex, jnp.where(half(h), vpair, jnp.ones_like(vpair)))
            rinv = 1.0 / pltpu.roll(o, 64, 1)
            rinv_ref[:, h:h + 1] = rinv[:, 0:1] if h % 2 == 0 else 1.0 / o[:, 0:1]
            return o * rinv

        outs = []
        s_ahead = {0: scores(0), 1: scores(1)}
        ex_ahead = {0: numerators(0, s_ahead.pop(0))}
        for h in range(HEADS):
            if h + 2 < HEADS:
                s_ahead[h + 2] = scores(h + 2)
            if h + 1 < HEADS:
                ex_ahead[h + 1] = numerators(h + 1, s_ahead.pop(h + 1))
            outs.append(weighted_values(h, ex_ahead.pop(h)))
            if h % 2 == 1:
                o_ref[:, pair(h)] = jnp.where(lane < 64, outs[h - 1], outs[h]).astype(BF16)

    return pl.pallas_call(
        body, name="attn_fwd", grid=(nb,),
        out_shape=[jax.ShapeDtypeStruct((S, D_ATT), BF16), jax.ShapeDtypeStruct((S, HEADS * KB), BF16),
                   jax.ShapeDtypeStruct((S, HEADS), F32)],
        in_specs=[pl.BlockSpec((QB, D_ATT), lambda g: (g, 0)), ANY,
                  pl.BlockSpec((HEADS, QB, KB), lambda g: (0, 0, 0))],
        out_specs=[pl.BlockSpec((QB, D_ATT), lambda g: (g, 0)),
                   pl.BlockSpec((QB, HEADS * KB), lambda g: (g, 0)),
                   pl.BlockSpec((QB, HEADS), lambda g: (g, 0))],
        scratch_shapes=[pltpu.VMEM((S + PADK, D_ATT), BF16), pltpu.VMEM((S + PADK, D_ATT), BF16),
                        pltpu.VMEM((HEADS, QB, KB), BF16),
                        pltpu.SemaphoreType.DMA((2, S // (KEY_GROUP * QB)))],
        compiler_params=_params(1),
    )(P, P, bias_tab)


def _token_local(x, tgt, P, att, proj_g, w_out, cw_g, conv_b, final_g):
    S = x.shape[0]
    ts = 256
    nt = S // ts
    hb = 16

    def body(x_ref, t_ref, s1_ref, s2_ref, s3_ref, h1_ref, h2_ref, att_ref,
             pg_ref, wo_ref, cwg_ref, cb_ref, g2_ref,
             dx2_ref, dg_ref, datt_ref, dwo_ref, dproj_ref, sm1_ref, sm2_ref,
             carry, wao_ref, wco_ref, cw_ref, dwo_acc, dwao_acc, dwco_acc):
        i = pl.program_id(0)
        t = nt - 1 - i

        @pl.when(i == 0)
        def _():
            dwo_acc[...] = jnp.zeros_like(dwo_acc)
            dwao_acc[...] = jnp.zeros_like(dwao_acc)
            dwco_acc[...] = jnp.zeros_like(dwco_acc)
            lane = lax.broadcasted_iota(jnp.int32, (1, 128), 1)
            for j in range(N_DEV):
                wao_ref[:, 128 * j:128 * (j + 1)] = pg_ref[j, :, 0:128]
                wco_ref[:, 128 * j:128 * (j + 1)] = pg_ref[j, :, 128:256]
            for p in range(N_DEV // 2):
                cw_ref[:, 128 * p:128 * (p + 1)] = jnp.where(
                    lane < 64, cwg_ref[2 * p], pltpu.roll(cwg_ref[2 * p + 1], 64, 1))
            sm1_ref[...] = jnp.zeros_like(sm1_ref)
            sm2_ref[...] = jnp.zeros_like(sm2_ref)
            carry[...] = jnp.zeros_like(carry)

        za = s1_ref[:, 0:512]
        gb = s1_ref[:, 512:1024]
        gc = s1_ref[:, 1024:1536].astype(F32)
        u = s2_ref[:, 0:512].astype(F32)
        zc = s2_ref[:, 512:1024]
        ga = jnp.concatenate([s2_ref[:, 1024:1536], s3_ref[:, 0:512]], axis=1)
        gv = s3_ref[:, 512:1536]
        att = att_ref[...]

        sa = _sigmoid(za)
        silu_a = za * sa
        att_g = att * silu_a
        y_att = _dot(att_g, wao_ref[...])

        cu = gc * u
        keep = jnp.where(t > 0, 1.0, 0.0).astype(F32)
        hcu = (h1_ref[hb - 8:hb, 1024:1536].astype(F32) * h2_ref[hb - 8:hb, 0:512].astype(F32)) * keep
        cu_ext = jnp.concatenate([hcu, cu], axis=0)
        cu_m1 = pltpu.roll(cu_ext, 1, 0)[8:]
        cu_m2 = pltpu.roll(cu_ext, 2, 0)[8:]
        w0, w1, w2 = cw_ref[0:1, :], cw_ref[1:2, :], cw_ref[2:3, :]
        vconv = w0 * cu_m2 + w1 * cu_m1 + w2 * cu + cb_ref[...]
        vcb = vconv.astype(BF16)
        sc = _sigmoid(zc)
        silu_c = zc * sc
        cg = gb * vcb * silu_c
        sga = _sigmoid(ga)
        sgv = _sigmoid(gv)
        y_conv = _dot(cg, wco_ref[...])

        yab, ycb = y_att.astype(BF16), y_conv.astype(BF16)
        m = sga * yab + sgv * ycb
        x2 = x_ref[...] + _dot(m, wo_ref[...])
        r2 = lax.rsqrt(jnp.mean(x2 * x2, axis=-1, keepdims=True) + EPS)
        xn2 = x2 * r2
        g2 = g2_ref[...]
        err = xn2 * g2 - t_ref[...]
        sm1_ref[1:2, :] += jnp.sum(err * err, axis=0, keepdims=True) * (0.5 / D_MODEL)

        dy = err * (1.0 / D_MODEL)
        sm1_ref[0:1, :] += jnp.sum(dy * xn2, axis=0, keepdims=True)
        dxn = dy * g2
        dx2 = r2 * (dxn - xn2 * jnp.mean(dxn * xn2, axis=-1, keepdims=True))
        dx2_ref[...] = dx2
        dx2b = dx2.astype(BF16)
        dwo_acc[...] += _dot_tn(m, dx2b)
        dm = _dot_nt(dx2b, wo_ref[...])
        dmb = dm.astype(BF16)
        dya = dmb * sga
        dyc = dmb * sgv
        dg_ref[:, 2560:3584] = dmb * yab * (sga * (1.0 - sga))
        dg_ref[:, 3584:4608] = dmb * ycb * (sgv * (1.0 - sgv))
        dwao_acc[...] += _dot_tn(att_g, dya)
        dwco_acc[...] += _dot_tn(cg, dyc)
        datt_g = _dot_nt(dya, wao_ref[...])
        dcg = _dot_nt(dyc, wco_ref[...])
        dagb, dcgb = datt_g.astype(BF16), dcg.astype(BF16)
        datt_ref[...] = dagb * silu_a
        dg_ref[:, 0:512] = dagb * att * (sa + silu_a * (1.0 - sa))
        dg_ref[:, 512:1024] = dcgb * vcb * silu_c
        dg_ref[:, 2048:2560] = dcgb * gb * vcb * (sc + silu_c * (1.0 - sc))
        dv = dcg * (gb * silu_c).astype(F32)
        sm2_ref[3:4, :] += jnp.sum(dv, axis=0, keepdims=True)
        sm2_ref[0:1, :] += jnp.sum(dv * cu_m2, axis=0, keepdims=True)
        sm2_ref[1:2, :] += jnp.sum(dv * cu_m1, axis=0, keepdims=True)
        sm2_ref[2:3, :] += jnp.sum(dv * cu, axis=0, keepdims=True)
        dv_ext = jnp.concatenate([dv, carry[...]], axis=0)
        dv_p1 = pltpu.roll(dv_ext, ts + 7, 0)[0:ts]
        dv_p2 = pltpu.roll(dv_ext, ts + 6, 0)[0:ts]
        dcu = w2 * dv + w1 * dv_p1 + w0 * dv_p2
        carry[...] = dv[0:8, :]
        dg_ref[:, 1024:1536] = (dcu * u).astype(BF16)
        dg_ref[:, 1536:2048] = (dcu * gc).astype(BF16)

        @pl.when(i == nt - 1)
        def _():
            dwo_ref[...] = dwo_acc[...].astype(BF16)
            for j in range(N_DEV):
                dproj_ref[j, :, 0:128] = dwao_acc[:, 128 * j:128 * (j + 1)].astype(BF16)
                dproj_ref[j, :, 128:256] = dwco_acc[:, 128 * j:128 * (j + 1)].astype(BF16)

    tile = lambda w: pl.BlockSpec((ts, w), lambda i: (nt - 1 - i, 0))
    seg = lambda c: pl.BlockSpec((ts, 1536), lambda i: (nt - 1 - i, c))
    halo = lambda c: pl.BlockSpec((hb, 1536), lambda i: (jnp.maximum((nt - 1 - i) * (ts // hb) - 1, 0), c))
    full = lambda a: pl.BlockSpec(a.shape, lambda i: (0,) * a.ndim)
    acc = lambda r, c: pl.BlockSpec((r, c), lambda i: (0, 0))
    return pl.pallas_call(
        body, name="token_local", grid=(nt,),
        out_shape=[jax.ShapeDtypeStruct((S, D_MODEL), F32), jax.ShapeDtypeStruct((S, IN_COLS), BF16),
                   jax.ShapeDtypeStruct((S, D_ATT), BF16), jax.ShapeDtypeStruct((D_MODEL, D_MODEL), BF16),
                   jax.ShapeDtypeStruct(proj_g.shape, BF16),
                   jax.ShapeDtypeStruct((8, D_MODEL), F32), jax.ShapeDtypeStruct((8, D_CONV), F32)],
        in_specs=[tile(D_MODEL), tile(D_MODEL), seg(1), seg(2), seg(3), halo(1), halo(2), tile(D_ATT),
                  full(proj_g), full(w_out), full(cw_g), full(conv_b), full(final_g)],
        out_specs=[tile(D_MODEL), tile(GATE_COLS), tile(D_ATT), acc(D_MODEL, D_MODEL), full(proj_g),
                   acc(8, D_MODEL), acc(8, D_CONV)],
        scratch_shapes=[pltpu.VMEM((8, D_CONV), F32),
                        pltpu.VMEM((D_ATT, D_MODEL), BF16), pltpu.VMEM((D_CONV, D_MODEL), BF16),
                        pltpu.VMEM((8, D_CONV), F32), pltpu.VMEM((D_MODEL, D_MODEL), F32),
                        pltpu.VMEM((D_ATT, D_MODEL), F32), pltpu.VMEM((D_CONV, D_MODEL), F32)],
        compiler_params=_params(1),
    )(x, tgt, P, P, P, P, P, att, proj_g, w_out, cw_g, conv_b, final_g)


def _fold_diagonals(d_ref, o_ref):
    wide = D_MODEL
    sub = lax.broadcasted_iota(jnp.int32, (8, 1), 0)
    col = lax.broadcasted_iota(jnp.int32, (1, wide), 1)
    pad = jnp.zeros((8, wide - KB), F32)
    for h in range(HEADS):
        acc = jnp.concatenate([d_ref[h, 0:8, :], pad], axis=1)
        for qh in range(1, QB // 8):
            a = jnp.concatenate([d_ref[h, 8 * qh:8 * qh + 8, :], pad], axis=1)
            acc = acc + pltpu.roll(a, wide - 8 * qh, 1)
        for r in range(1, 8):
            acc = jnp.where(sub == r, pltpu.roll(acc, wide - r, 1), acc)
        vec = jnp.sum(acc, axis=0, keepdims=True)
        far = (col <= PADK - MAX_REL) | (col > KB)
        tail = jnp.sum(jnp.where(far, vec, 0.0), axis=-1, keepdims=True)
        o_ref[h:h + 1, :] = jnp.where(col == wide - 1, tail, vec)


def _attn_bwd(P, att, datt, ex, rinv, dP):
    S = P.shape[0]
    nb = S // QB

    def body(q_ref, att_ref, datt_ref, ex_ref, rinv_ref, p_hbm, dp_hbm, out_ref, dbias_ref,
             kp, vp, dq_ring, dk_ring, dv_ring, db_ref, sem):
        g = pl.program_id(0)

        _load_keys(g, nb, p_hbm, kp, vp, sem)

        @pl.when(g == 0)
        def _():
            db_ref[...] = jnp.zeros_like(db_ref)
            dk_ring[...] = jnp.zeros_like(dk_ring)
            dv_ring[...] = jnp.zeros_like(dv_ring)

        s_new = g % 3
        s_mid = (g + 2) % 3
        s_old = (g + 1) % 3

        @pl.when(g < nb)
        def _():
            start = pl.multiple_of(g * QB, QB)
            lane = lax.broadcasted_iota(jnp.int32, (1, 128), 1)
            for p in range(HEADS // 2):
                cols = slice(128 * p, 128 * (p + 1))
                qp = q_ref[:, cols] * SCALE
                op = att_ref[:, cols].astype(F32)
                dop = datt_ref[:, cols]
                kpair = kp[pl.ds(start, KB), cols]
                vpair = vp[pl.ds(start, KB), cols]
                dqs = []
                dk_acc = jnp.zeros((KB, 128), F32)
                dv_acc = jnp.zeros((KB, 128), F32)
                for e in range(2):
                    h = 2 * p + e
                    lm = (lane < 64) if e == 0 else (lane >= 64)
                    qm = jnp.where(lm, qp, jnp.zeros_like(qp))
                    dom = jnp.where(lm, dop, jnp.zeros_like(dop))
                    exh = ex_ref[:, KB * h:KB * (h + 1)]
                    rinv = rinv_ref[:, h:h + 1]
                    domf = dom.astype(F32)
                    dp = _dot_nt(dom, vpair)
                    delta = jnp.sum(domf * op, axis=-1, keepdims=True)
                    dsb = exh * ((dp - delta) * rinv).astype(BF16)
                    db_ref[h] += dsb.astype(F32)
                    dqs.append(_dot(dsb, kpair) * SCALE)
                    dk_acc = dk_acc + _dot_tn(dsb, qm)
                    dv_acc = dv_acc + _dot_tn(exh, (domf * rinv).astype(BF16))
                dq_ring[s_new, :, cols] = jnp.where(lane < 64, dqs[0], dqs[1])
                dk_ring[s_old, :, cols] += dk_acc[0:QB]
                dk_ring[s_mid, :, cols] += dk_acc[QB:2 * QB]
                dk_ring[s_new, :, cols] = dk_acc[2 * QB:3 * QB]
                dv_ring[s_old, :, cols] += dv_acc[0:QB]
                dv_ring[s_mid, :, cols] += dv_acc[QB:2 * QB]
                dv_ring[s_new, :, cols] = dv_acc[2 * QB:3 * QB]

        @pl.when(g >= 2)
        def _():
            out_ref[:, 0:D_ATT] = dq_ring[s_old].astype(BF16)
            out_ref[:, D_ATT:2 * D_ATT] = dk_ring[s_old].astype(BF16)
            out_ref[:, 2 * D_ATT:3 * D_ATT] = dv_ring[s_old].astype(BF16)

        @pl.when(g == nb + 1)
        def _():
            _fold_diagonals(db_ref, dbias_ref)

    qblk = lambda w: pl.BlockSpec((QB, w), lambda g: (jnp.minimum(g, nb - 1), 0))
    return pl.pallas_call(
        body, name="attn_bwd", grid=(nb + 2,),
        out_shape=[jax.ShapeDtypeStruct((S, IN_COLS), BF16), jax.ShapeDtypeStruct((HEADS, D_MODEL), F32)],
        in_specs=[qblk(D_ATT), qblk(D_ATT), qblk(D_ATT), qblk(HEADS * KB), qblk(HEADS), ANY, ANY],
        out_specs=[pl.BlockSpec((QB, 3 * D_ATT), lambda g: (jnp.maximum(g - 2, 0), GATE_COLS // (3 * D_ATT))),
                   pl.BlockSpec((HEADS, D_MODEL), lambda g: (0, 0))],
        input_output_aliases={6: 0},
        scratch_shapes=[pltpu.VMEM((S + PADK, D_ATT), BF16), pltpu.VMEM((S + PADK, D_ATT), BF16),
                        pltpu.VMEM((3, QB, D_ATT), F32), pltpu.VMEM((3, QB, D_ATT), F32),
                        pltpu.VMEM((3, QB, D_ATT), F32), pltpu.VMEM((HEADS, QB, KB), F32),
                        pltpu.SemaphoreType.DMA((2, S // (KEY_GROUP * QB)))],
        compiler_params=_params(1),
    )(P, att, datt, ex, rinv, P, dP)


def _dp_block(j):
    return (j + GATE_COLS // W_BLK) % N_DEV


def _in_proj_bwd(x, norm_g, dx2, dP, w_in_g):
    S = x.shape[0]
    ts = 512

    def body(x_ref, g_ref, dx2_ref, dp_ref, w_ref, gx_ref, dn_ref):
        @pl.when(pl.program_id(0) == 0)
        def _():
            dn_ref[...] = jnp.zeros_like(dn_ref)

        dh = _dot_nt(dp_ref[...], w_ref[...])
        xf = x_ref[...]
        r = lax.rsqrt(jnp.mean(xf * xf, axis=-1, keepdims=True) + EPS)
        xn = xf * r
        dn_ref[0:1, :] += jnp.sum(dh * xn, axis=0, keepdims=True)
        dhg = dh * g_ref[...]
        gx_ref[...] = dx2_ref[...] + r * (dhg - xn * jnp.mean(dhg * xn, axis=-1, keepdims=True))

    tile = lambda w: pl.BlockSpec((ts, w), lambda i: (i, 0))
    return pl.pallas_call(
        body, name="in_proj_bwd", grid=(S // ts,),
        out_shape=[jax.ShapeDtypeStruct((S, D_MODEL), F32), jax.ShapeDtypeStruct((8, D_MODEL), F32)],
        in_specs=[tile(D_MODEL), pl.BlockSpec((1, D_MODEL), lambda i: (0, 0)), tile(D_MODEL),
                  tile(IN_COLS),
                  pl.BlockSpec((D_MODEL, IN_COLS), lambda i: (0, 0))],
        out_specs=[tile(D_MODEL), pl.BlockSpec((8, D_MODEL), lambda i: (0, 0))],
        compiler_params=_params(1),
    )(x, norm_g, dx2, dP, w_in_g)


SCATTER_MASKS = ((3, 4, 5, 2, 7, 6, 1, 0), (5, 2, 3, 4, 7, 6, 1, 0))


def _w_in_grad_scatter(ht, dP, d_proj, d_wo, pack, order):
    S = ht.shape[1]
    ts = min(S, 2048)
    nt = S // ts
    n_steps = 8

    def body(order_ref, ht_ref, d_ref, proj_hbm, wo_hbm, pack_hbm, g_ref, rproj, rwo, rpack,
             acc, stage, rsib, rici, d2d_send, d2d_recv, ici_send, ici_recv, small_send, small_recv, local_sems):
        k, i = pl.program_id(0), pl.program_id(1)
        x, y, c = _mesh_pos()
        my = _flat((x, y, c))
        sibling = (x, y, 1 - c)
        owners = [(x ^ (1 - c), y ^ c, c), (x ^ c, y ^ (1 - c), c), (1 - x, 1 - y, c)]
        peers = [sibling, (1 - x, y, c), (x, 1 - y, c), (1 - x, 1 - y, c),
                 (1 - x, y, 1 - c), (x, 1 - y, 1 - c), (1 - x, 1 - y, 1 - c)]
        small = ((proj_hbm, rproj, True), (wo_hbm, rwo, True), (pack_hbm, rpack, False))
        n_small = len(small)

        def small_copy(kk, a, receive=False):
            src, dst, per_peer = small[a]
            slot = _flat(peers[kk]) if receive else my
            return pltpu.make_async_remote_copy(
                src_ref=src.at[_flat(peers[kk])] if per_peer else src, dst_ref=dst.at[slot],
                send_sem=small_send.at[kk, a], recv_sem=small_recv.at[kk, a],
                device_id=peers[kk], device_id_type=MESH)

        def to_sibling(t):
            return pltpu.make_async_remote_copy(
                src_ref=stage.at[0], dst_ref=rsib.at[t % 2], send_sem=d2d_send.at[t], recv_sem=d2d_recv.at[t],
                device_id=sibling, device_id_type=MESH)

        def to_owner(t):
            return pltpu.make_async_remote_copy(
                src_ref=stage.at[1], dst_ref=rici.at[t], send_sem=ici_send.at[t], recv_sem=ici_recv.at[t],
                device_id=owners[t], device_id_type=MESH)

        own_small = [pltpu.make_async_copy(src.at[my] if per_peer else src, dst.at[my], local_sems.at[a])
                     for a, (src, dst, per_peer) in enumerate(small)]

        @pl.when((k == 0) & (i == 0))
        def _():
            for cp in own_small:
                cp.start()
            for kk in range(len(peers)):
                for a in range(n_small):
                    small_copy(kk, a).start()

        @pl.when(i == 0)
        def _():
            acc[...] = jnp.zeros_like(acc)

        acc[...] += _dot(ht_ref[...], d_ref[...])

        @pl.when(i == nt - 1)
        def _():
            for s in range(n_steps):
                @pl.when(k == s)
                def _():
                    t = s // 2
                    if s % 2 == 0:
                        if t >= 1:
                            to_sibling(t - 1).wait_send()
                        stage[0] = acc[...].astype(BF16)
                        to_sibling(t).start()
                    elif t < 3:
                        if t >= 1:
                            to_owner(t - 1).wait_send()
                        to_sibling(t).wait_recv()
                        stage[1] = (acc[...] + rsib[t % 2].astype(F32)).astype(BF16)
                        to_owner(t).start()
                    else:
                        to_sibling(t).wait_recv()
                        total = acc[...] + rsib[t % 2].astype(F32)
                        for j in range(3):
                            to_owner(j).wait_recv()
                            total = total + rici[j].astype(F32)
                        g_ref[...] = total
                        to_owner(2).wait_send()
                        to_sibling(3).wait_send()
                        for q in range(len(peers)):
                            for a in range(n_small):
                                small_copy(q, a).wait_send()
                                small_copy(q, a, receive=True).wait_recv()
                        for cp in own_small:
                            cp.wait()

    blk = (D_MODEL, W_BLK)
    grid_spec = pltpu.PrefetchScalarGridSpec(
        num_scalar_prefetch=1, grid=(n_steps, nt),
        in_specs=[pl.BlockSpec((D_MODEL, ts), lambda k, i, o: (0, i)),
                  pl.BlockSpec((ts, W_BLK), lambda k, i, o: (i, _dp_block(o[k]))),
                  ANY, ANY, ANY],
        out_specs=[pl.BlockSpec(blk, lambda k, i, o: (0, 0)), ANY, ANY, ANY],
        scratch_shapes=[pltpu.VMEM(blk, F32), pltpu.VMEM((2,) + blk, BF16),
                        pltpu.VMEM((2,) + blk, BF16), pltpu.VMEM((3,) + blk, BF16),
                        pltpu.SemaphoreType.DMA((4,)), pltpu.SemaphoreType.DMA((4,)),
                        pltpu.SemaphoreType.DMA((3,)), pltpu.SemaphoreType.DMA((3,)),
                        pltpu.SemaphoreType.DMA((7, 3)), pltpu.SemaphoreType.DMA((7, 3)),
                        pltpu.SemaphoreType.DMA((3,))])
    return pl.pallas_call(
        body, name="w_in_grad_scatter", grid_spec=grid_spec,
        out_shape=[jax.ShapeDtypeStruct(blk, F32),
                   jax.ShapeDtypeStruct(d_proj.shape, BF16), jax.ShapeDtypeStruct(d_wo.shape, BF16),
                   jax.ShapeDtypeStruct((N_DEV,) + pack.shape, F32)],
        compiler_params=_params(2),
    )(order, ht, dP, d_proj, d_wo, pack)


def _adamw(w, g, m, v):
    m = ADAM_B1 * m + (1.0 - ADAM_B1) * g
    v = ADAM_B2 * v + (1.0 - ADAM_B2) * (g * g)
    m_hat = m / (1.0 - ADAM_B1 ** ADAM_STEP)
    v_hat = v / (1.0 - ADAM_B2 ** ADAM_STEP)
    delta = -ADAM_LR * (m_hat / (jnp.sqrt(v_hat) + ADAM_EPS) + ADAM_WD * w)
    return delta, m, v


def _sum_adamw(parts, w, m, v, name):
    R, C = w.shape
    n = parts.shape[0]
    tr = min(R, 256)

    def body(p_ref, w_ref, m_ref, v_ref, g_ref, d_ref, nm_ref, nv_ref):
        g = p_ref[0].astype(F32)
        for s in range(1, n):
            g = g + p_ref[s].astype(F32)
        g_ref[...] = g
        d_ref[...], nm_ref[...], nv_ref[...] = _adamw(w_ref[...], g, m_ref[...], v_ref[...])

    tile = pl.BlockSpec((tr, C), lambda i: (i, 0))
    return pl.pallas_call(
        body, name=name, grid=(R // tr,),
        out_shape=[jax.ShapeDtypeStruct((R, C), F32)] * 4,
        in_specs=[pl.BlockSpec((n, tr, C), lambda i: (0, i, 0)), tile, tile, tile],
        out_specs=[tile] * 4,
        compiler_params=_params(1),
    )(parts, w, m, v)


def _adamw_mid(r_proj, r_wo, params):
    def body(rp_ref, rw_ref, *refs):
        ins, outs = refs[:9], refs[9:]

        def total(part):
            g = part(0).astype(F32)
            for s in range(1, N_DEV):
                g = g + part(s).astype(F32)
            return g

        grads = (total(lambda s: rp_ref[s, :, 0:128]), total(lambda s: rp_ref[s, :, 128:256]),
                 total(lambda s: rw_ref[s]))
        for n, g in enumerate(grads):
            w, m, v = (r[...] for r in ins[3 * n:3 * n + 3])
            outs[4 * n][...] = g
            outs[4 * n + 1][...], outs[4 * n + 2][...], outs[4 * n + 3][...] = _adamw(w, g, m, v)

    return pl.pallas_call(
        body, name="adamw_mid",
        out_shape=[jax.ShapeDtypeStruct(params[3 * n].shape, F32) for n in range(3) for _ in range(4)],
        compiler_params=pltpu.CompilerParams(vmem_limit_bytes=VMEM_LIMIT),
    )(r_proj, r_wo, *params)


def _adamw_small(r_pack, params):
    wide = 384

    def body(p_ref, *refs):
        ins, loss_ref, outs = refs[:15], refs[15], refs[16:]
        tot = p_ref[0]
        for s in range(1, N_DEV):
            tot = tot + p_ref[s]
        me = _flat(_mesh_pos())
        loss_ref[...] = jnp.sum(tot[2:3, :], axis=-1, keepdims=True)
        mine = pltpu.roll(tot[0:8, 0:D_CONV], (D_CONV - 64 * me) % D_CONV, 1)
        col = lax.broadcasted_iota(jnp.int32, (D_MODEL, wide), 0)
        idx = lax.broadcasted_iota(jnp.int32, (D_MODEL, wide), 1)
        near = (idx > MAX_REL - CHUNK) & (idx < 2 * MAX_REL) & (col == PADK + MAX_REL - idx)
        far = (idx == 2 * MAX_REL) & (col == D_MODEL - 1)
        perm = jnp.where(near | far, 1.0, 0.0).astype(F32)
        g_rel = jnp.dot(tot[8:16], perm, precision=lax.Precision.HIGHEST, preferred_element_type=F32)
        grads = (tot[0:1], tot[1:2], mine[3:6, 0:64], tot[6:7, 0:D_CONV], g_rel[:, 0:N_REL])
        for n, g in enumerate(grads):
            w, m, v = (r[...] for r in ins[3 * n:3 * n + 3])
            outs[4 * n][...] = g
            outs[4 * n + 1][...], outs[4 * n + 2][...], outs[4 * n + 3][...] = _adamw(w, g, m, v)

    return pl.pallas_call(
        body, name="adamw_small",
        out_shape=[jax.ShapeDtypeStruct((1, 1), F32)]
        + [jax.ShapeDtypeStruct(params[3 * n].shape, F32) for n in range(5) for _ in range(4)],
    )(r_pack, *params)


def _pad_row(a, width=D_MODEL):
    a = a.reshape(-1, a.shape[-1])
    return jnp.pad(a, ((0, 0), (0, width - a.shape[-1])))


def kernel(x, norm_g, w_in, rel_bias, w_att_out, conv_w, conv_b, w_conv_out, w_out, final_norm_g, loss_target, m_norm_g, m_w_in, m_rel_bias, m_w_att_out, m_conv_w, m_conv_b, m_w_conv_out, m_w_out, m_final_norm_g, v_norm_g, v_w_in, v_rel_bias, v_w_att_out, v_conv_w, v_conv_b, v_w_conv_out, v_w_out, v_final_norm_g):
    S = x.shape[1]
    x2d = x.reshape(S, D_MODEL)
    tgt = loss_target.reshape(S, D_MODEL)
    me = 4 * lax.axis_index("x") + 2 * lax.axis_index("y") + lax.axis_index("c")
    row = lambda a: a.reshape(1, D_MODEL)

    proj_sh = jnp.concatenate([w_att_out[0], w_conv_out[0]], axis=1).astype(BF16)
    cw_sh = jnp.pad(conv_w[0], ((0, 5), (0, 64)))
    P, ht, w_in_g, proj_g, w_out_g, cw_g = _gather_in_proj(
        x2d, norm_g, w_in[0].astype(BF16), [proj_sh, w_out[0].astype(BF16), cw_sh],
        me ^ _by_core(GATHER_MASKS))

    bias_tab = _bias_table(rel_bias[0])
    att, ex, rinv = _attn_fwd(P, bias_tab)
    dx2, dP, datt, d_wo, d_proj, sm1, sm2 = _token_local(
        x2d, tgt, P, att, proj_g, w_out_g.reshape(D_MODEL, D_MODEL), cw_g, conv_b, row(final_norm_g))
    dP, dbias = _attn_bwd(P, att, datt, ex, rinv, dP)
    grad_x, dnorm = _in_proj_bwd(x2d, norm_g, dx2, dP, w_in_g)

    pack = jnp.concatenate([dnorm[0:1], sm1[0:2], _pad_row(sm2[0:4]), jnp.zeros((1, D_MODEL), F32), dbias],
                           axis=0)
    g_win_sum, r_proj, r_wo, r_pack = _w_in_grad_scatter(
        ht, dP, d_proj, d_wo.reshape(N_DEV, 128, D_MODEL), pack, me ^ _by_core(SCATTER_MASKS))

    res = {"w_in": _sum_adamw(g_win_sum[None], w_in[0], m_w_in[0], v_w_in[0], "adamw_w_in")}
    mid = _adamw_mid(r_proj, r_wo, (w_att_out[0], m_w_att_out[0], v_w_att_out[0],
                                    w_conv_out[0], m_w_conv_out[0], v_w_conv_out[0],
                                    w_out[0], m_w_out[0], v_w_out[0]))
    for n, name in enumerate(("w_att_out", "w_conv_out", "w_out")):
        res[name] = mid[4 * n:4 * n + 4]
    small = _adamw_small(r_pack, (norm_g, m_norm_g, v_norm_g,
                                  row(final_norm_g), row(m_final_norm_g), row(v_final_norm_g),
                                  conv_w[0], m_conv_w[0], v_conv_w[0], conv_b, m_conv_b, v_conv_b,
                                  rel_bias[0], m_rel_bias[0], v_rel_bias[0]))
    loss = small[0].reshape(())
    for n, name in enumerate(("norm_g", "final_norm_g", "conv_w", "conv_b", "rel_bias")):
        res[name] = small[1 + 4 * n:5 + 4 * n]

    leading = {"norm_g": (1, D_MODEL), "final_norm_g": (D_MODEL,), "conv_b": (1, D_CONV)}
    outs = []
    for kind in range(4):
        for name in ("norm_g", "w_in", "rel_bias", "w_att_out", "conv_w", "conv_b", "w_conv_out", "w_out",
                     "final_norm_g"):
            a = res[name][kind]
            outs.append(a.reshape(leading[name]) if name in leading else a[None])
    return (loss, grad_x.reshape(1, S, D_MODEL), *outs)
```

```python
import jax
import jax.numpy as jnp
from jax import lax
from jax.experimental import pallas as pl
from jax.experimental.pallas import tpu as pltpu

F32 = jnp.float32
BF16 = jnp.bfloat16

D_MODEL = 1024
CHUNK = 64
N_LEFT = 8
HEADS = 8
D_ATT = 512
D_CONV = 512
MAX_REL = 128
N_REL = 2 * MAX_REL + 1
IN_COLS = 6144
EPS = 1e-6
NEG_BIG = -1e30
N_DEV = 8
W_BLK = IN_COLS // N_DEV
QB = 4 * CHUNK
KB = QB + N_LEFT * CHUNK
PADK = N_LEFT * CHUNK
SCALE = 64 ** -0.5
LOG2E = 1.4426950408889634
GATE_COLS = IN_COLS - 3 * D_ATT

ADAM_LR = 0.001
ADAM_B1 = 0.9
ADAM_B2 = 0.999
ADAM_EPS = 1e-08
ADAM_WD = 0.01
ADAM_STEP = 10

VMEM_LIMIT = 56 * 1024 * 1024

MESH = pl.DeviceIdType.MESH
ANY = pl.BlockSpec(memory_space=pl.ANY)


def _params(n_grid, vmem_limit=VMEM_LIMIT):
    return pltpu.CompilerParams(dimension_semantics=("arbitrary",) * n_grid,
                                vmem_limit_bytes=vmem_limit)


def _dot(a, b):
    return jnp.dot(a, b, preferred_element_type=F32)


def _dot_nt(a, b):
    return lax.dot_general(a, b, (((1,), (1,)), ((), ())), preferred_element_type=F32)


def _dot_tn(a, b):
    return lax.dot_general(a, b, (((0,), (0,)), ((), ())), preferred_element_type=F32)


def _sigmoid(z):
    return 0.5 * jnp.tanh(0.5 * z) + 0.5


def _mesh_pos():
    return lax.axis_index("x"), lax.axis_index("y"), lax.axis_index("c")


def _flat(p):
    return 4 * p[0] + 2 * p[1] + p[2]


def _by_core(masks):
    m0, m1 = (jnp.array(m, jnp.int32) for m in masks)
    return jnp.where(lax.axis_index("c") == 0, m0, m1)


GATHER_MASKS = ((0, 1, 4, 3, 2, 5, 6, 7), (0, 1, 2, 5, 4, 3, 6, 7))


def _gather_in_proj(x, norm_g, w_sh, smalls, order):
    S = x.shape[0]
    ts = 1024
    nt = S // ts
    n_small = len(smalls)
    n_steps = N_DEV

    def body(order_ref, x_ref, g_ref, w_hbm, *rest):
        small_in = rest[:n_small]
        p_ref, ht_ref, wg_hbm = rest[n_small:n_small + 3]
        small_out = rest[n_small + 3:2 * n_small + 3]
        (wbuf, hbuf, own_sem, send_sems, recv_sems, out_sems,
         small_send, small_recv, small_local) = rest[2 * n_small + 3:]
        k, i = pl.program_id(0), pl.program_id(1)
        x_, y_, c_ = _mesh_pos()
        me, sibling = (x_, y_, c_), (x_, y_, 1 - c_)
        my = _flat(me)
        chips = [(x_ ^ (1 - c_), y_ ^ c_), (x_ ^ c_, y_ ^ (1 - c_)), (1 - x_, 1 - y_)]
        peers = [sibling] + [(*chip, c_) for chip in chips] + [(*chip, 1 - c_) for chip in chips]

        def wcopy(sem, block, to, from_input=False):
            dst = wbuf.at[_flat(block)]
            return pltpu.make_async_remote_copy(
                src_ref=w_hbm if from_input else dst, dst_ref=dst,
                send_sem=send_sems.at[sem], recv_sem=recv_sems.at[sem], device_id=to, device_id_type=MESH)

        def small_copy(q, a, receive=False):
            slot = _flat(peers[q]) if receive else my
            return pltpu.make_async_remote_copy(
                src_ref=small_in[a], dst_ref=small_out[a].at[slot],
                send_sem=small_send.at[q, a], recv_sem=small_recv.at[q, a],
                device_id=peers[q], device_id_type=MESH)

        def keep(step, block):
            col = pl.multiple_of(_dp_block(_flat(block)) * W_BLK, 128)
            return pltpu.make_async_copy(wbuf.at[_flat(block)], wg_hbm.at[:, pl.ds(col, W_BLK)], out_sems.at[step])

        own = pltpu.make_async_copy(w_hbm, wbuf.at[my], own_sem)
        small_own = [pltpu.make_async_copy(small_in[a], small_out[a].at[my], small_local.at[a])
                     for a in range(n_small)]
        passed_on = [(*chips[1], 1 - c_), (*chips[0], 1 - c_), (*chips[2], 1 - c_)]
        arrivals = [me, sibling]
        for j in range(3):
            arrivals += [(*chips[j], c_), passed_on[j]]

        @pl.when(i == 0)
        def _():
            for kk in range(n_steps):
                @pl.when(k == kk)
                def _():
                    j = kk // 2 - 1
                    if kk == 0:
                        own.start()
                        wcopy(0, me, sibling, True).start()
                        wcopy(1, me, (*chips[0], c_), True).start()
                        own.wait()
                    elif kk == 1:
                        wcopy(0, sibling, me).wait_recv()
                        wcopy(2, me, (*chips[1], c_), True).start()
                    elif kk % 2 == 0:
                        wcopy(1 + j, (*chips[j], c_), me).wait_recv()
                        wcopy(4 + j, (*chips[j], c_), sibling).start()
                        if kk == 2:
                            wcopy(3, me, (*chips[2], c_), True).start()
                    else:
                        wcopy(4 + j, passed_on[j], me).wait_recv()
                        if kk == 3:
                            for cp in small_own:
                                cp.start()
                            for q in range(len(peers)):
                                for a in range(n_small):
                                    small_copy(q, a).start()
                    keep(kk, arrivals[kk]).start()

        row0 = pl.multiple_of(i * ts, ts)

        @pl.when(k == 0)
        def _():
            xf = x_ref[...]
            r = lax.rsqrt(jnp.mean(xf * xf, axis=-1, keepdims=True) + EPS)
            hf = (xf * r) * g_ref[...]
            hbuf[pl.ds(row0, ts), :] = hf.astype(BF16)
            ht_ref[...] = hf.astype(BF16).T

        p_ref[...] = _dot(hbuf[pl.ds(row0, ts), :], wbuf[order_ref[k]]).astype(BF16)

        @pl.when((k == n_steps - 1) & (i == nt - 1))
        def _():
            wcopy(0, me, sibling, True).wait_send()
            for j, chip in enumerate(chips):
                wcopy(1 + j, me, (*chip, c_), True).wait_send()
                wcopy(4 + j, (*chip, c_), sibling).wait_send()
            for kk in range(n_steps):
                keep(kk, arrivals[kk]).wait()
            for cp in small_own:
                cp.wait()
            for q in range(len(peers)):
                for a in range(n_small):
                    small_copy(q, a).wait_send()
                    small_copy(q, a, receive=True).wait_recv()

    first_pass = lambda k, i: jnp.where(k == 0, i, nt - 1)
    grid_spec = pltpu.PrefetchScalarGridSpec(
        num_scalar_prefetch=1, grid=(n_steps, nt),
        in_specs=[pl.BlockSpec((ts, D_MODEL), lambda k, i, o: (first_pass(k, i), 0)),
                  pl.BlockSpec((1, D_MODEL), lambda k, i, o: (0, 0)), ANY] + [ANY] * n_small,
        out_specs=[pl.BlockSpec((ts, W_BLK), lambda k, i, o: (i, o[k])),
                   pl.BlockSpec((D_MODEL, ts), lambda k, i, o: (0, first_pass(k, i))), ANY] + [ANY] * n_small,
        scratch_shapes=[pltpu.VMEM((N_DEV, D_MODEL, W_BLK), BF16), pltpu.VMEM((S, D_MODEL), BF16),
                        pltpu.SemaphoreType.DMA, pltpu.SemaphoreType.DMA((7,)), pltpu.SemaphoreType.DMA((7,)),
                        pltpu.SemaphoreType.DMA((n_steps,)),
                        pltpu.SemaphoreType.DMA((7, n_small)), pltpu.SemaphoreType.DMA((7, n_small)),
                        pltpu.SemaphoreType.DMA((n_small,))])
    return pl.pallas_call(
        body, name="gather_in_proj", grid_spec=grid_spec,
        out_shape=[jax.ShapeDtypeStruct((S, IN_COLS), BF16), jax.ShapeDtypeStruct((D_MODEL, S), BF16),
                   jax.ShapeDtypeStruct((D_MODEL, IN_COLS), BF16)]
        + [jax.ShapeDtypeStruct((N_DEV,) + s.shape, s.dtype) for s in smalls],
        compiler_params=_params(2),
    )(order, x, norm_g, w_sh, *smalls)


def _bias_table(rel_bias):
    wide = 1024
    n_rel = 384

    def body(r_ref, o_ref):
        col = lax.broadcasted_iota(jnp.int32, (1, wide), 1)
        k_minus_q = jnp.where(col < KB, col, col - wide)
        idx = jnp.clip(PADK - k_minus_q, -MAX_REL, MAX_REL) + MAX_REL
        pick = jnp.where(lax.broadcasted_iota(jnp.int32, (n_rel, wide), 0) == idx, 1.0, 0.0).astype(F32)
        f = jnp.dot(r_ref[...], pick, precision=lax.Precision.HIGHEST, preferred_element_type=F32) * LOG2E
        kcol = lax.broadcasted_iota(jnp.int32, (1, KB), 1)
        kc = kcol >> 6
        sub = lax.broadcasted_iota(jnp.int32, (8, 1), 0)
        for h in range(HEADS):
            f8 = jnp.broadcast_to(f[h:h + 1, :], (8, wide))
            base = f8
            for r in range(1, 8):
                base = jnp.where(sub == r, pltpu.roll(f8, r, 1), base)
            for qh in range(0, QB // 8, 2):
                rows = jnp.concatenate(
                    [(pltpu.roll(base, 8 * q, 1) if q else base)[:, 0:KB] for q in (qh, qh + 1)], axis=0)
                qc = (8 * qh) // CHUNK
                band = (kc >= qc) & (kc <= qc + N_LEFT)
                o_ref[h, 8 * qh:8 * qh + 16, :] = jnp.where(band, rows, NEG_BIG).astype(BF16)

    return pl.pallas_call(
        body, name="bias_table",
        out_shape=jax.ShapeDtypeStruct((HEADS, QB, KB), BF16),
        compiler_params=pltpu.CompilerParams(vmem_limit_bytes=VMEM_LIMIT),
    )(jnp.pad(rel_bias, ((0, 0), (0, n_rel - N_REL))))


KEY_GROUP = 4


def _load_keys(g, nb, p_hbm, kp, vp, sem):
    rows = KEY_GROUP * QB
    n_groups = p_hbm.shape[0] // rows

    def copies(c):
        src = pl.ds(c * rows, rows)
        dst = pl.ds(PADK + c * rows, rows)
        return (pltpu.make_async_copy(p_hbm.at[src, D_ATT:2 * D_ATT], kp.at[dst, :], sem.at[0, c]),
                pltpu.make_async_copy(p_hbm.at[src, 2 * D_ATT:3 * D_ATT], vp.at[dst, :], sem.at[1, c]))

    @pl.when(g == 0)
    def _():
        kp[0:PADK, :] = jnp.zeros((PADK, D_ATT), BF16)
        vp[0:PADK, :] = jnp.zeros((PADK, D_ATT), BF16)
        for c in range(n_groups):
            for cp in copies(c):
                cp.start()

    @pl.when((g % KEY_GROUP == 0) & (g < nb))
    def _():
        for cp in copies(g // KEY_GROUP):
            cp.wait()


def _attn_fwd(P, bias_tab):
    S = P.shape[0]
    nb = S // QB

    def body(q_ref, p_hbm, bias_ref, o_ref, ex_ref, rinv_ref, kp, vp, tab, sem):
        g = pl.program_id(0)
        _load_keys(g, nb, p_hbm, kp, vp, sem)

        @pl.when(g * QB <= PADK)
        def _():
            kcol = lax.broadcasted_iota(jnp.int32, (1, KB), 1)
            for h in range(HEADS):
                tab[h] = jnp.where(kcol + g * QB >= PADK, bias_ref[h], NEG_BIG).astype(BF16)

        start = pl.multiple_of(g * QB, QB)
        lane = lax.broadcasted_iota(jnp.int32, (1, 128), 1)
        half = lambda h: (lane < 64) if h % 2 == 0 else (lane >= 64)
        pair = lambda h: slice(128 * (h // 2), 128 * (h // 2 + 1))

        def scores(h):
            qp = q_ref[:, pair(h)] * SCALE
            qm = jnp.where(half(h), qp, jnp.zeros_like(qp))
            return (_dot_nt(qm, kp[pl.ds(start, KB), pair(h)]) * LOG2E).astype(BF16) + tab[h]

        def numerators(h, s):
            ex = jnp.exp2(s - jnp.max(s, axis=-1, keepdims=True))
            ex_ref[:, KB * h:KB * (h + 1)] = ex
            return ex

        def weighted_values(h, ex):
            vpair = vp[pl.ds(start, KB), pair(h)]
            o = _dot(ex, jnp.where(half(h), vpair, jnp.ones_like(vpair)))
            rinv = 1.0 / pltpu.roll(o, 64, 1)
            rinv_ref[:, h:h + 1] = rinv[:, 0:1] if h % 2 == 0 else 1.0 / o[:, 0:1]
            return o * rinv

        outs = []
        s_ahead = {0: scores(0), 1: scores(1)}
        ex_ahead = {0: numerators(0, s_ahead.pop(0))}
        for h in range(HEADS):
            if h + 2 < HEADS:
                s_ahead[h + 2] = scores(h + 2)
            if h + 1 < HEADS:
                ex_ahead[h + 1] = numerators(h + 1, s_ahead.pop(h + 1))
            outs.append(weighted_values(h, ex_ahead.pop(h)))
            if h % 2 == 1:
                o_ref[:, pair(h)] = jnp.where(lane < 64, outs[h - 1], outs[h]).astype(BF16)

    return pl.pallas_call(
        body, name="attn_fwd", grid=(nb,),
        out_shape=[jax.ShapeDtypeStruct((S, D_ATT), BF16), jax.ShapeDtypeStruct((S, HEADS * KB), BF16),
                   jax.ShapeDtypeStruct((S, HEADS), F32)],
        in_specs=[pl.BlockSpec((QB, D_ATT), lambda g: (g, 0)), ANY,
                  pl.BlockSpec((HEADS, QB, KB), lambda g: (0, 0, 0))],
        out_specs=[pl.BlockSpec((QB, D_ATT), lambda g: (g, 0)),
                   pl.BlockSpec((QB, HEADS * KB), lambda g: (g, 0)),
                   pl.BlockSpec((QB, HEADS), lambda g: (g, 0))],
        scratch_shapes=[pltpu.VMEM((S + PADK, D_ATT), BF16), pltpu.VMEM((S + PADK, D_ATT), BF16),
                        pltpu.VMEM((HEADS, QB, KB), BF16),
                        pltpu.SemaphoreType.DMA((2, S // (KEY_GROUP * QB)))],
        compiler_params=_params(1),
    )(P, P, bias_tab)


def _token_local(x, tgt, P, att, proj_g, w_out, cw_g, conv_b, final_g):
    S = x.shape[0]
    ts = 256
    nt = S // ts
    hb = 16

    def body(x_ref, t_ref, s1_ref, s2_ref, s3_ref, h1_ref, h2_ref, att_ref,
             pg_ref, wo_ref, cwg_ref, cb_ref, g2_ref,
             dx2_ref, dg_ref, datt_ref, dwo_ref, dproj_ref, sm1_ref, sm2_ref,
             carry, wao_ref, wco_ref, cw_ref, dwo_acc, dwao_acc, dwco_acc):
        i = pl.program_id(0)
        t = nt - 1 - i

        @pl.when(i == 0)
        def _():
            dwo_acc[...] = jnp.zeros_like(dwo_acc)
            dwao_acc[...] = jnp.zeros_like(dwao_acc)
            dwco_acc[...] = jnp.zeros_like(dwco_acc)
            lane = lax.broadcasted_iota(jnp.int32, (1, 128), 1)
            for j in range(N_DEV):
                wao_ref[:, 128 * j:128 * (j + 1)] = pg_ref[j, :, 0:128]
                wco_ref[:, 128 * j:128 * (j + 1)] = pg_ref[j, :, 128:256]
            for p in range(N_DEV // 2):
                cw_ref[:, 128 * p:128 * (p + 1)] = jnp.where(
                    lane < 64, cwg_ref[2 * p], pltpu.roll(cwg_ref[2 * p + 1], 64, 1))
            sm1_ref[...] = jnp.zeros_like(sm1_ref)
            sm2_ref[...] = jnp.zeros_like(sm2_ref)
            carry[...] = jnp.zeros_like(carry)

        za = s1_ref[:, 0:512]
        gb = s1_ref[:, 512:1024]
        gc = s1_ref[:, 1024:1536].astype(F32)
        u = s2_ref[:, 0:512].astype(F32)
        zc = s2_ref[:, 512:1024]
        ga = jnp.concatenate([s2_ref[:, 1024:1536], s3_ref[:, 0:512]], axis=1)
        gv = s3_ref[:, 512:1536]
        att = att_ref[...]

        sa = _sigmoid(za)
        silu_a = za * sa
        att_g = att * silu_a
        y_att = _dot(att_g, wao_ref[...])

        cu = gc * u
        keep = jnp.where(t > 0, 1.0, 0.0).astype(F32)
        hcu = (h1_ref[hb - 8:hb, 1024:1536].astype(F32) * h2_ref[hb - 8:hb, 0:512].astype(F32)) * keep
        cu_ext = jnp.concatenate([hcu, cu], axis=0)
        cu_m1 = pltpu.roll(cu_ext, 1, 0)[8:]
        cu_m2 = pltpu.roll(cu_ext, 2, 0)[8:]
        w0, w1, w2 = cw_ref[0:1, :], cw_ref[1:2, :], cw_ref[2:3, :]
        vconv = w0 * cu_m2 + w1 * cu_m1 + w2 * cu + cb_ref[...]
        vcb = vconv.astype(BF16)
        sc = _sigmoid(zc)
        silu_c = zc * sc
        cg = gb * vcb * silu_c
        sga = _sigmoid(ga)
        sgv = _sigmoid(gv)
        y_conv = _dot(cg, wco_ref[...])

        yab, ycb = y_att.astype(BF16), y_conv.astype(BF16)
        m = sga * yab + sgv * ycb
        x2 = x_ref[...] + _dot(m, wo_ref[...])
        r2 = lax.rsqrt(jnp.mean(x2 * x2, axis=-1, keepdims=True) + EPS)
        xn2 = x2 * r2
        g2 = g2_ref[...]
        err = xn2 * g2 - t_ref[...]
        sm1_ref[1:2, :] += jnp.sum(err * err, axis=0, keepdims=True) * (0.5 / D_MODEL)

        dy = err * (1.0 / D_MODEL)
        sm1_ref[0:1, :] += jnp.sum(dy * xn2, axis=0, keepdims=True)
        dxn = dy * g2
        dx2 = r2 * (dxn - xn2 * jnp.mean(dxn * xn2, axis=-1, keepdims=True))
        dx2_ref[...] = dx2
        dx2b = dx2.astype(BF16)
        dwo_acc[...] += _dot_tn(m, dx2b)
        dm = _dot_nt(dx2b, wo_ref[...])
        dmb = dm.astype(BF16)
        dya = dmb * sga
        dyc = dmb * sgv
        dg_ref[:, 2560:3584] = dmb * yab * (sga * (1.0 - sga))
        dg_ref[:, 3584:4608] = dmb * ycb * (sgv * (1.0 - sgv))
        dwao_acc[...] += _dot_tn(att_g, dya)
        dwco_acc[...] += _dot_tn(cg, dyc)
        datt_g = _dot_nt(dya, wao_ref[...])
        dcg = _dot_nt(dyc, wco_ref[...])
        dagb, dcgb = datt_g.astype(BF16), dcg.astype(BF16)
        datt_ref[...] = dagb * silu_a
        dg_ref[:, 0:512] = dagb * att * (sa + silu_a * (1.0 - sa))
        dg_ref[:, 512:1024] = dcgb * vcb * silu_c
        dg_ref[:, 2048:2560] = dcgb * gb * vcb * (sc + silu_c * (1.0 - sc))
        dv = dcg * (gb * silu_c).astype(F32)
        sm2_ref[3:4, :] += jnp.sum(dv, axis=0, keepdims=True)
        sm2_ref[0:1, :] += jnp.sum(dv * cu_m2, axis=0, keepdims=True)
        sm2_ref[1:2, :] += jnp.sum(dv * cu_m1, axis=0, keepdims=True)
        sm2_ref[2:3, :] += jnp.sum(dv * cu, axis=0, keepdims=True)
        dv_ext = jnp.concatenate([dv, carry[...]], axis=0)
        dv_p1 = pltpu.roll(dv_ext, ts + 7, 0)[0:ts]
        dv_p2 = pltpu.roll(dv_ext, ts + 6, 0)[0:ts]
        dcu = w2 * dv + w1 * dv_p1 + w0 * dv_p2
        carry[...] = dv[0:8, :]
        dg_ref[:, 1024:1536] = (dcu * u).astype(BF16)
        dg_ref[:, 1536:2048] = (dcu * gc).astype(BF16)

        @pl.when(i == nt - 1)
        def _():
            dwo_ref[...] = dwo_acc[...].astype(BF16)
            for j in range(N_DEV):
                dproj_ref[j, :, 0:128] = dwao_acc[:, 128 * j:128 * (j + 1)].astype(BF16)
                dproj_ref[j, :, 128:256] = dwco_acc[:, 128 * j:128 * (j + 1)].astype(BF16)

    tile = lambda w: pl.BlockSpec((ts, w), lambda i: (nt - 1 - i, 0))
    seg = lambda c: pl.BlockSpec((ts, 1536), lambda i: (nt - 1 - i, c))
    halo = lambda c: pl.BlockSpec((hb, 1536), lambda i: (jnp.maximum((nt - 1 - i) * (ts // hb) - 1, 0), c))
    full = lambda a: pl.BlockSpec(a.shape, lambda i: (0,) * a.ndim)
    acc = lambda r, c: pl.BlockSpec((r, c), lambda i: (0, 0))
    return pl.pallas_call(
        body, name="token_local", grid=(nt,),
        out_shape=[jax.ShapeDtypeStruct((S, D_MODEL), F32), jax.ShapeDtypeStruct((S, IN_COLS), BF16),
                   jax.ShapeDtypeStruct((S, D_ATT), BF16), jax.ShapeDtypeStruct((D_MODEL, D_MODEL), BF16),
                   jax.ShapeDtypeStruct(proj_g.shape, BF16),
                   jax.ShapeDtypeStruct((8, D_MODEL), F32), jax.ShapeDtypeStruct((8, D_CONV), F32)],
        in_specs=[tile(D_MODEL), tile(D_MODEL), seg(1), seg(2), seg(3), halo(1), halo(2), tile(D_ATT),
                  full(proj_g), full(w_out), full(cw_g), full(conv_b), full(final_g)],
        out_specs=[tile(D_MODEL), tile(GATE_COLS), tile(D_ATT), acc(D_MODEL, D_MODEL), full(proj_g),
                   acc(8, D_MODEL), acc(8, D_CONV)],
        scratch_shapes=[pltpu.VMEM((8, D_CONV), F32),
                        pltpu.VMEM((D_ATT, D_MODEL), BF16), pltpu.VMEM((D_CONV, D_MODEL), BF16),
                        pltpu.VMEM((8, D_CONV), F32), pltpu.VMEM((D_MODEL, D_MODEL), F32),
                        pltpu.VMEM((D_ATT, D_MODEL), F32), pltpu.VMEM((D_CONV, D_MODEL), F32)],
        compiler_params=_params(1),
    )(x, tgt, P, P, P, P, P, att, proj_g, w_out, cw_g, conv_b, final_g)


def _fold_diagonals(d_ref, o_ref):
    wide = D_MODEL
    sub = lax.broadcasted_iota(jnp.int32, (8, 1), 0)
    col = lax.broadcasted_iota(jnp.int32, (1, wide), 1)
    pad = jnp.zeros((8, wide - KB), F32)
    for h in range(HEADS):
        acc = jnp.concatenate([d_ref[h, 0:8, :], pad], axis=1)
        for qh in range(1, QB // 8):
            a = jnp.concatenate([d_ref[h, 8 * qh:8 * qh + 8, :], pad], axis=1)
            acc = acc + pltpu.roll(a, wide - 8 * qh, 1)
        for r in range(1, 8):
            acc = jnp.where(sub == r, pltpu.roll(acc, wide - r, 1), acc)
        vec = jnp.sum(acc, axis=0, keepdims=True)
        far = (col <= PADK - MAX_REL) | (col > KB)
        tail = jnp.sum(jnp.where(far, vec, 0.0), axis=-1, keepdims=True)
        o_ref[h:h + 1, :] = jnp.where(col == wide - 1, tail, vec)


def _attn_bwd(P, att, datt, ex, rinv, dP):
    S = P.shape[0]
    nb = S // QB

    def body(q_ref, att_ref, datt_ref, ex_ref, rinv_ref, p_hbm, dp_hbm, out_ref, dbias_ref,
             kp, vp, dq_ring, dk_ring, dv_ring, db_ref, sem):
        g = pl.program_id(0)

        _load_keys(g, nb, p_hbm, kp, vp, sem)

        @pl.when(g == 0)
        def _():
            db_ref[...] = jnp.zeros_like(db_ref)
            dk_ring[...] = jnp.zeros_like(dk_ring)
            dv_ring[...] = jnp.zeros_like(dv_ring)

        s_new = g % 3
        s_mid = (g + 2) % 3
        s_old = (g + 1) % 3

        @pl.when(g < nb)
        def _():
            start = pl.multiple_of(g * QB, QB)
            lane = lax.broadcasted_iota(jnp.int32, (1, 128), 1)
            for p in range(HEADS // 2):
                cols = slice(128 * p, 128 * (p + 1))
                qp = q_ref[:, cols] * SCALE
                op = att_ref[:, cols].astype(F32)
                dop = datt_ref[:, cols]
                kpair = kp[pl.ds(start, KB), cols]
                vpair = vp[pl.ds(start, KB), cols]
                dqs = []
                dk_acc = jnp.zeros((KB, 128), F32)
                dv_acc = jnp.zeros((KB, 128), F32)
                for e in range(2):
                    h = 2 * p + e
                    lm = (lane < 64) if e == 0 else (lane >= 64)
                    qm = jnp.where(lm, qp, jnp.zeros_like(qp))
                    dom = jnp.where(lm, dop, jnp.zeros_like(dop))
                    exh = ex_ref[:, KB * h:KB * (h + 1)]
                    rinv = rinv_ref[:, h:h + 1]
                    domf = dom.astype(F32)
                    dp = _dot_nt(dom, vpair)
                    delta = jnp.sum(domf * op, axis=-1, keepdims=True)
                    dsb = exh * ((dp - delta) * rinv).astype(BF16)
                    db_ref[h] += dsb.astype(F32)
                    dqs.append(_dot(dsb, kpair) * SCALE)
                    dk_acc = dk_acc + _dot_tn(dsb, qm)
                    dv_acc = dv_acc + _dot_tn(exh, (domf * rinv).astype(BF16))
                dq_ring[s_new, :, cols] = jnp.where(lane < 64, dqs[0], dqs[1])
                dk_ring[s_old, :, cols] += dk_acc[0:QB]
                dk_ring[s_mid, :, cols] += dk_acc[QB:2 * QB]
                dk_ring[s_new, :, cols] = dk_acc[2 * QB:3 * QB]
                dv_ring[s_old, :, cols] += dv_acc[0:QB]
                dv_ring[s_mid, :, cols] += dv_acc[QB:2 * QB]
                dv_ring[s_new, :, cols] = dv_acc[2 * QB:3 * QB]

        @pl.when(g >= 2)
        def _():
            out_ref[:, 0:D_ATT] = dq_ring[s_old].astype(BF16)
            out_ref[:, D_ATT:2 * D_ATT] = dk_ring[s_old].astype(BF16)
            out_ref[:, 2 * D_ATT:3 * D_ATT] = dv_ring[s_old].astype(BF16)

        @pl.when(g == nb + 1)
        def _():
            _fold_diagonals(db_ref, dbias_ref)

    qblk = lambda w: pl.BlockSpec((QB, w), lambda g: (jnp.minimum(g, nb - 1), 0))
    return pl.pallas_call(
        body, name="attn_bwd", grid=(nb + 2,),
        out_shape=[jax.ShapeDtypeStruct((S, IN_COLS), BF16), jax.ShapeDtypeStruct((HEADS, D_MODEL), F32)],
        in_specs=[qblk(D_ATT), qblk(D_ATT), qblk(D_ATT), qblk(HEADS * KB), qblk(HEADS), ANY, ANY],
        out_specs=[pl.BlockSpec((QB, 3 * D_ATT), lambda g: (jnp.maximum(g - 2, 0), GATE_COLS // (3 * D_ATT))),
                   pl.BlockSpec((HEADS, D_MODEL), lambda g: (0, 0))],
        input_output_aliases={6: 0},
        scratch_shapes=[pltpu.VMEM((S + PADK, D_ATT), BF16), pltpu.VMEM((S + PADK, D_ATT), BF16),
                        pltpu.VMEM((3, QB, D_ATT), F32), pltpu.VMEM((3, QB, D_ATT), F32),
                        pltpu.VMEM((3, QB, D_ATT), F32), pltpu.VMEM((HEADS, QB, KB), F32),
                        pltpu.SemaphoreType.DMA((2, S // (KEY_GROUP * QB)))],
        compiler_params=_params(1),
    )(P, att, datt, ex, rinv, P, dP)


def _dp_block(j):
    return (j + GATE_COLS // W_BLK) % N_DEV


def _in_proj_bwd(x, norm_g, dx2, dP, w_in_g):
    S = x.shape[0]
    ts = 512
    nt = S // ts
    ring = 3

    def body(x_ref, g_ref, dx2_ref, dp_hbm, w_ref, gx_ref, dn_ref, dp_buf, sem):
        i = pl.program_id(0)

        def fetch(j):
            slot = j % ring
            rows = pl.ds(pl.multiple_of(j * ts, ts), ts)
            return pltpu.make_async_copy(dp_hbm.at[rows, :], dp_buf.at[slot], sem.at[slot])

        @pl.when(i == 0)
        def _():
            dn_ref[...] = jnp.zeros_like(dn_ref)
            for j in range(min(ring - 1, nt)):
                fetch(j).start()

        @pl.when(i + ring - 1 < nt)
        def _():
            fetch(i + ring - 1).start()

        fetch(i).wait()
        dh = _dot_nt(dp_buf[i % ring], w_ref[...])
        xf = x_ref[...]
        r = lax.rsqrt(jnp.mean(xf * xf, axis=-1, keepdims=True) + EPS)
        xn = xf * r
        dn_ref[0:1, :] += jnp.sum(dh * xn, axis=0, keepdims=True)
        dhg = dh * g_ref[...]
        gx_ref[...] = dx2_ref[...] + r * (dhg - xn * jnp.mean(dhg * xn, axis=-1, keepdims=True))

    tile = lambda w: pl.BlockSpec((ts, w), lambda i: (i, 0))
    return pl.pallas_call(
        body, name="in_proj_bwd", grid=(nt,),
        out_shape=[jax.ShapeDtypeStruct((S, D_MODEL), F32), jax.ShapeDtypeStruct((8, D_MODEL), F32)],
        in_specs=[tile(D_MODEL), pl.BlockSpec((1, D_MODEL), lambda i: (0, 0)), tile(D_MODEL), ANY,
                  pl.BlockSpec((D_MODEL, IN_COLS), lambda i: (0, 0))],
        out_specs=[tile(D_MODEL), pl.BlockSpec((8, D_MODEL), lambda i: (0, 0))],
        scratch_shapes=[pltpu.VMEM((ring, ts, IN_COLS), BF16), pltpu.SemaphoreType.DMA((ring,))],
        compiler_params=_params(1),
    )(x, norm_g, dx2, dP, w_in_g)


SCATTER_MASKS = ((3, 4, 5, 2, 7, 6, 1, 0), (5, 2, 3, 4, 7, 6, 1, 0))


def _w_in_grad_scatter(ht, dP, d_proj, d_wo, pack, order):
    S = ht.shape[1]
    ts = min(S, 2048)
    nt = S // ts
    n_steps = 8

    def body(order_ref, ht_ref, d_ref, proj_hbm, wo_hbm, pack_hbm, g_ref, rproj, rwo, rpack,
             acc, stage, rsib, rici, d2d_send, d2d_recv, ici_send, ici_recv, small_send, small_recv, local_sems):
        k, i = pl.program_id(0), pl.program_id(1)
        x, y, c = _mesh_pos()
        my = _flat((x, y, c))
        sibling = (x, y, 1 - c)
        owners = [(x ^ (1 - c), y ^ c, c), (x ^ c, y ^ (1 - c), c), (1 - x, 1 - y, c)]
        peers = [sibling, (1 - x, y, c), (x, 1 - y, c), (1 - x, 1 - y, c),
                 (1 - x, y, 1 - c), (x, 1 - y, 1 - c), (1 - x, 1 - y, 1 - c)]
        small = ((proj_hbm, rproj, True), (wo_hbm, rwo, True), (pack_hbm, rpack, False))
        n_small = len(small)

        def small_copy(kk, a, receive=False):
            src, dst, per_peer = small[a]
            slot = _flat(peers[kk]) if receive else my
            return pltpu.make_async_remote_copy(
                src_ref=src.at[_flat(peers[kk])] if per_peer else src, dst_ref=dst.at[slot],
                send_sem=small_send.at[kk, a], recv_sem=small_recv.at[kk, a],
                device_id=peers[kk], device_id_type=MESH)

        def to_sibling(t):
            return pltpu.make_async_remote_copy(
                src_ref=stage.at[0], dst_ref=rsib.at[t % 2], send_sem=d2d_send.at[t], recv_sem=d2d_recv.at[t],
                device_id=sibling, device_id_type=MESH)

        def to_owner(t):
            return pltpu.make_async_remote_copy(
                src_ref=stage.at[1], dst_ref=rici.at[t], send_sem=ici_send.at[t], recv_sem=ici_recv.at[t],
                device_id=owners[t], device_id_type=MESH)

        own_small = [pltpu.make_async_copy(src.at[my] if per_peer else src, dst.at[my], local_sems.at[a])
                     for a, (src, dst, per_peer) in enumerate(small)]

        @pl.when((k == 0) & (i == 0))
        def _():
            for cp in own_small:
                cp.start()
            for kk in range(len(peers)):
                for a in range(n_small):
                    small_copy(kk, a).start()

        @pl.when(i == 0)
        def _():
            acc[...] = jnp.zeros_like(acc)

        acc[...] += _dot(ht_ref[...], d_ref[...])

        @pl.when(i == nt - 1)
        def _():
            for s in range(n_steps):
                @pl.when(k == s)
                def _():
                    t = s // 2
                    if s % 2 == 0:
                        if t >= 1:
                            to_sibling(t - 1).wait_send()
                        stage[0] = acc[...].astype(BF16)
                        to_sibling(t).start()
                    elif t < 3:
                        if t >= 1:
                            to_owner(t - 1).wait_send()
                        to_sibling(t).wait_recv()
                        stage[1] = (acc[...] + rsib[t % 2].astype(F32)).astype(BF16)
                        to_owner(t).start()
                    else:
                        to_sibling(t).wait_recv()
                        total = acc[...] + rsib[t % 2].astype(F32)
                        for j in range(3):
                            to_owner(j).wait_recv()
                            total = total + rici[j].astype(F32)
                        g_ref[...] = total
                        to_owner(2).wait_send()
                        to_sibling(3).wait_send()
                        for q in range(len(peers)):
                            for a in range(n_small):
                                small_copy(q, a).wait_send()
                                small_copy(q, a, receive=True).wait_recv()
                        for cp in own_small:
                            cp.wait()

    blk = (D_MODEL, W_BLK)
    grid_spec = pltpu.PrefetchScalarGridSpec(
        num_scalar_prefetch=1, grid=(n_steps, nt),
        in_specs=[pl.BlockSpec((D_MODEL, ts), lambda k, i, o: (0, i)),
                  pl.BlockSpec((ts, W_BLK), lambda k, i, o: (i, _dp_block(o[k]))),
                  ANY, ANY, ANY],
        out_specs=[pl.BlockSpec(blk, lambda k, i, o: (0, 0)), ANY, ANY, ANY],
        scratch_shapes=[pltpu.VMEM(blk, F32), pltpu.VMEM((2,) + blk, BF16),
                        pltpu.VMEM((2,) + blk, BF16), pltpu.VMEM((3,) + blk, BF16),
                        pltpu.SemaphoreType.DMA((4,)), pltpu.SemaphoreType.DMA((4,)),
                        pltpu.SemaphoreType.DMA((3,)), pltpu.SemaphoreType.DMA((3,)),
                        pltpu.SemaphoreType.DMA((7, 3)), pltpu.SemaphoreType.DMA((7, 3)),
                        pltpu.SemaphoreType.DMA((3,))])
    return pl.pallas_call(
        body, name="w_in_grad_scatter", grid_spec=grid_spec,
        out_shape=[jax.ShapeDtypeStruct(blk, F32),
                   jax.ShapeDtypeStruct(d_proj.shape, BF16), jax.ShapeDtypeStruct(d_wo.shape, BF16),
                   jax.ShapeDtypeStruct((N_DEV,) + pack.shape, F32)],
        compiler_params=_params(2),
    )(order, ht, dP, d_proj, d_wo, pack)


def _adamw(w, g, m, v):
    m = ADAM_B1 * m + (1.0 - ADAM_B1) * g
    v = ADAM_B2 * v + (1.0 - ADAM_B2) * (g * g)
    m_hat = m / (1.0 - ADAM_B1 ** ADAM_STEP)
    v_hat = v / (1.0 - ADAM_B2 ** ADAM_STEP)
    delta = -ADAM_LR * (m_hat / (jnp.sqrt(v_hat) + ADAM_EPS) + ADAM_WD * w)
    return delta, m, v


def _sum_adamw(parts, w, m, v, name):
    R, C = w.shape
    n = parts.shape[0]
    tr = min(R, 256)

    def body(p_ref, w_ref, m_ref, v_ref, g_ref, d_ref, nm_ref, nv_ref):
        g = p_ref[0].astype(F32)
        for s in range(1, n):
            g = g + p_ref[s].astype(F32)
        g_ref[...] = g
        d_ref[...], nm_ref[...], nv_ref[...] = _adamw(w_ref[...], g, m_ref[...], v_ref[...])

    tile = pl.BlockSpec((tr, C), lambda i: (i, 0))
    return pl.pallas_call(
        body, name=name, grid=(R // tr,),
        out_shape=[jax.ShapeDtypeStruct((R, C), F32)] * 4,
        in_specs=[pl.BlockSpec((n, tr, C), lambda i: (0, i, 0)), tile, tile, tile],
        out_specs=[tile] * 4,
        compiler_params=_params(1),
    )(parts, w, m, v)


def _adamw_mid(r_proj, r_wo, params):
    def body(rp_ref, rw_ref, *refs):
        ins, outs = refs[:9], refs[9:]

        def total(part):
            g = part(0).astype(F32)
            for s in range(1, N_DEV):
                g = g + part(s).astype(F32)
            return g

        grads = (total(lambda s: rp_ref[s, :, 0:128]), total(lambda s: rp_ref[s, :, 128:256]),
                 total(lambda s: rw_ref[s]))
        for n, g in enumerate(grads):
            w, m, v = (r[...] for r in ins[3 * n:3 * n + 3])
            outs[4 * n][...] = g
            outs[4 * n + 1][...], outs[4 * n + 2][...], outs[4 * n + 3][...] = _adamw(w, g, m, v)

    return pl.pallas_call(
        body, name="adamw_mid",
        out_shape=[jax.ShapeDtypeStruct(params[3 * n].shape, F32) for n in range(3) for _ in range(4)],
        compiler_params=pltpu.CompilerParams(vmem_limit_bytes=VMEM_LIMIT),
    )(r_proj, r_wo, *params)


def _adamw_small(r_pack, params):
    wide = 384

    def body(p_ref, *refs):
        ins, loss_ref, outs = refs[:15], refs[15], refs[16:]
        tot = p_ref[0]
        for s in range(1, N_DEV):
            tot = tot + p_ref[s]
        me = _flat(_mesh_pos())
        loss_ref[...] = jnp.sum(tot[2:3, :], axis=-1, keepdims=True)
        mine = pltpu.roll(tot[0:8, 0:D_CONV], (D_CONV - 64 * me) % D_CONV, 1)
        col = lax.broadcasted_iota(jnp.int32, (D_MODEL, wide), 0)
        idx = lax.broadcasted_iota(jnp.int32, (D_MODEL, wide), 1)
        near = (idx > MAX_REL - CHUNK) & (idx < 2 * MAX_REL) & (col == PADK + MAX_REL - idx)
        far = (idx == 2 * MAX_REL) & (col == D_MODEL - 1)
        perm = jnp.where(near | far, 1.0, 0.0).astype(F32)
        g_rel = jnp.dot(tot[8:16], perm, precision=lax.Precision.HIGHEST, preferred_element_type=F32)
        grads = (tot[0:1], tot[1:2], mine[3:6, 0:64], tot[6:7, 0:D_CONV], g_rel[:, 0:N_REL])
        for n, g in enumerate(grads):
            w, m, v = (r[...] for r in ins[3 * n:3 * n + 3])
            outs[4 * n][...] = g
            outs[4 * n + 1][...], outs[4 * n + 2][...], outs[4 * n + 3][...] = _adamw(w, g, m, v)

    return pl.pallas_call(
        body, name="adamw_small",
        out_shape=[jax.ShapeDtypeStruct((1, 1), F32)]
        + [jax.ShapeDtypeStruct(params[3 * n].shape, F32) for n in range(5) for _ in range(4)],
    )(r_pack, *params)


def _pad_row(a, width=D_MODEL):
    a = a.reshape(-1, a.shape[-1])
    return jnp.pad(a, ((0, 0), (0, width - a.shape[-1])))


def kernel(x, norm_g, w_in, rel_bias, w_att_out, conv_w, conv_b, w_conv_out, w_out, final_norm_g, loss_target, m_norm_g, m_w_in, m_rel_bias, m_w_att_out, m_conv_w, m_conv_b, m_w_conv_out, m_w_out, m_final_norm_g, v_norm_g, v_w_in, v_rel_bias, v_w_att_out, v_conv_w, v_conv_b, v_w_conv_out, v_w_out, v_final_norm_g):
    S = x.shape[1]
    x2d = x.reshape(S, D_MODEL)
    tgt = loss_target.reshape(S, D_MODEL)
    me = 4 * lax.axis_index("x") + 2 * lax.axis_index("y") + lax.axis_index("c")
    row = lambda a: a.reshape(1, D_MODEL)

    proj_sh = jnp.concatenate([w_att_out[0], w_conv_out[0]], axis=1).astype(BF16)
    cw_sh = jnp.pad(conv_w[0], ((0, 5), (0, 64)))
    P, ht, w_in_g, proj_g, w_out_g, cw_g = _gather_in_proj(
        x2d, norm_g, w_in[0].astype(BF16), [proj_sh, w_out[0].astype(BF16), cw_sh],
        me ^ _by_core(GATHER_MASKS))

    bias_tab = _bias_table(rel_bias[0])
    att, ex, rinv = _attn_fwd(P, bias_tab)
    dx2, dP, datt, d_wo, d_proj, sm1, sm2 = _token_local(
        x2d, tgt, P, att, proj_g, w_out_g.reshape(D_MODEL, D_MODEL), cw_g, conv_b, row(final_norm_g))
    dP, dbias = _attn_bwd(P, att, datt, ex, rinv, dP)
    grad_x, dnorm = _in_proj_bwd(x2d, norm_g, dx2, dP, w_in_g)

    pack = jnp.concatenate([dnorm[0:1], sm1[0:2], _pad_row(sm2[0:4]), jnp.zeros((1, D_MODEL), F32), dbias],
                           axis=0)
    g_win_sum, r_proj, r_wo, r_pack = _w_in_grad_scatter(
        ht, dP, d_proj, d_wo.reshape(N_DEV, 128, D_MODEL), pack, me ^ _by_core(SCATTER_MASKS))

    res = {"w_in": _sum_adamw(g_win_sum[None], w_in[0], m_w_in[0], v_w_in[0], "adamw_w_in")}
    mid = _adamw_mid(r_proj, r_wo, (w_att_out[0], m_w_att_out[0], v_w_att_out[0],
                                    w_conv_out[0], m_w_conv_out[0], v_w_conv_out[0],
                                    w_out[0], m_w_out[0], v_w_out[0]))
    for n, name in enumerate(("w_att_out", "w_conv_out", "w_out")):
        res[name] = mid[4 * n:4 * n + 4]
    small = _adamw_small(r_pack, (norm_g, m_norm_g, v_norm_g,
                                  row(final_norm_g), row(m_final_norm_g), row(v_final_norm_g),
                                  conv_w[0], m_conv_w[0], v_conv_w[0], conv_b, m_conv_b, v_conv_b,
                                  rel_bias[0], m_rel_bias[0], v_rel_bias[0]))
    loss = small[0].reshape(())
    for n, name in enumerate(("norm_g", "final_norm_g", "conv_w", "conv_b", "rel_bias")):
        res[name] = small[1 + 4 * n:5 + 4 * n]

    leading = {"norm_g": (1, D_MODEL), "final_norm_g": (D_MODEL,), "conv_b": (1, D_CONV)}
    outs = []
    for kind in range(4):
        for name in ("norm_g", "w_in", "rel_bias", "w_att_out", "conv_w", "conv_b", "w_conv_out", "w_out",
                     "final_norm_g"):
            a = res[name][kind]
            outs.append(a.reshape(leading[name]) if name in leading else a[None])
    return (loss, grad_x.reshape(1, S, D_MODEL), *outs)
```

```python
import jax
import jax.numpy as jnp
from jax import lax
from jax.experimental import pallas as pl
from jax.experimental.pallas import tpu as pltpu

F32 = jnp.float32
BF16 = jnp.bfloat16

D_MODEL = 1024
CHUNK = 64
N_LEFT = 8
HEADS = 8
D_ATT = 512
D_CONV = 512
MAX_REL = 128
N_REL = 2 * MAX_REL + 1
IN_COLS = 6144
EPS = 1e-6
NEG_BIG = -1e30
N_DEV = 8
W_BLK = IN_COLS // N_DEV
QB = 4 * CHUNK
KB = QB + N_LEFT * CHUNK
PADK = N_LEFT * CHUNK
SCALE = 64 ** -0.5
LOG2E = 1.4426950408889634
GATE_COLS = IN_COLS - 3 * D_ATT

ADAM_LR = 0.001
ADAM_B1 = 0.9
ADAM_B2 = 0.999
ADAM_EPS = 1e-08
ADAM_WD = 0.01
ADAM_STEP = 10

VMEM_LIMIT = 56 * 1024 * 1024

MESH = pl.DeviceIdType.MESH
ANY = pl.BlockSpec(memory_space=pl.ANY)


def _params(n_grid, vmem_limit=VMEM_LIMIT):
    return pltpu.CompilerParams(dimension_semantics=("arbitrary",) * n_grid,
                                vmem_limit_bytes=vmem_limit)


def _dot(a, b):
    return jnp.dot(a, b, preferred_element_type=F32)


def _dot_nt(a, b):
    return lax.dot_general(a, b, (((1,), (1,)), ((), ())), preferred_element_type=F32)


def _dot_tn(a, b):
    return lax.dot_general(a, b, (((0,), (0,)), ((), ())), preferred_element_type=F32)


def _sigmoid(z):
    return 0.5 * jnp.tanh(0.5 * z) + 0.5


def _mesh_pos():
    return lax.axis_index("x"), lax.axis_index("y"), lax.axis_index("c")


def _flat(p):
    return 4 * p[0] + 2 * p[1] + p[2]


def _by_core(masks):
    m0, m1 = (jnp.array(m, jnp.int32) for m in masks)
    return jnp.where(lax.axis_index("c") == 0, m0, m1)


GATHER_MASKS = ((0, 1, 4, 3, 2, 5, 6, 7), (0, 1, 2, 5, 4, 3, 6, 7))


def _gather_in_proj(x, norm_g, w_sh, smalls, order):
    S = x.shape[0]
    ts = 1024
    nt = S // ts
    n_small = len(smalls)
    n_steps = N_DEV

    def body(order_ref, x_ref, g_ref, w_hbm, *rest):
        small_in = rest[:n_small]
        p_ref, ht_ref, wg_hbm = rest[n_small:n_small + 3]
        small_out = rest[n_small + 3:2 * n_small + 3]
        (wbuf, hbuf, own_sem, send_sems, recv_sems, out_sems,
         small_send, small_recv, small_local) = rest[2 * n_small + 3:]
        k, i = pl.program_id(0), pl.program_id(1)
        x_, y_, c_ = _mesh_pos()
        me, sibling = (x_, y_, c_), (x_, y_, 1 - c_)
        my = _flat(me)
        chips = [(x_ ^ (1 - c_), y_ ^ c_), (x_ ^ c_, y_ ^ (1 - c_)), (1 - x_, 1 - y_)]
        peers = [sibling] + [(*chip, c_) for chip in chips] + [(*chip, 1 - c_) for chip in chips]

        def wcopy(sem, block, to, from_input=False):
            dst = wbuf.at[_flat(block)]
            return pltpu.make_async_remote_copy(
                src_ref=w_hbm if from_input else dst, dst_ref=dst,
                send_sem=send_sems.at[sem], recv_sem=recv_sems.at[sem], device_id=to, device_id_type=MESH)

        def small_copy(q, a, receive=False):
            slot = _flat(peers[q]) if receive else my
            return pltpu.make_async_remote_copy(
                src_ref=small_in[a], dst_ref=small_out[a].at[slot],
                send_sem=small_send.at[q, a], recv_sem=small_recv.at[q, a],
                device_id=peers[q], device_id_type=MESH)

        def keep(step, block):
            col = pl.multiple_of(_dp_block(_flat(block)) * W_BLK, 128)
            return pltpu.make_async_copy(wbuf.at[_flat(block)], wg_hbm.at[:, pl.ds(col, W_BLK)], out_sems.at[step])

        own = pltpu.make_async_copy(w_hbm, wbuf.at[my], own_sem)
        small_own = [pltpu.make_async_copy(small_in[a], small_out[a].at[my], small_local.at[a])
                     for a in range(n_small)]
        passed_on = [(*chips[1], 1 - c_), (*chips[0], 1 - c_), (*chips[2], 1 - c_)]
        arrivals = [me, sibling]
        for j in range(3):
            arrivals += [(*chips[j], c_), passed_on[j]]

        @pl.when(i == 0)
        def _():
            for kk in range(n_steps):
                @pl.when(k == kk)
                def _():
                    j = kk // 2 - 1
                    if kk == 0:
                        own.start()
                        wcopy(0, me, sibling, True).start()
                        wcopy(1, me, (*chips[0], c_), True).start()
                        own.wait()
                    elif kk == 1:
                        wcopy(0, sibling, me).wait_recv()
                        wcopy(2, me, (*chips[1], c_), True).start()
                    elif kk % 2 == 0:
                        wcopy(1 + j, (*chips[j], c_), me).wait_recv()
                        wcopy(4 + j, (*chips[j], c_), sibling).start()
                        if kk == 2:
                            wcopy(3, me, (*chips[2], c_), True).start()
                    else:
                        wcopy(4 + j, passed_on[j], me).wait_recv()
                        if kk == 3:
                            for cp in small_own:
                                cp.start()
                            for q in range(len(peers)):
                                for a in range(n_small):
                                    small_copy(q, a).start()
                    keep(kk, arrivals[kk]).start()

        row0 = pl.multiple_of(i * ts, ts)

        @pl.when(k == 0)
        def _():
            xf = x_ref[...]
            r = lax.rsqrt(jnp.mean(xf * xf, axis=-1, keepdims=True) + EPS)
            hf = (xf * r) * g_ref[...]
            hbuf[pl.ds(row0, ts), :] = hf.astype(BF16)
            ht_ref[...] = hf.astype(BF16).T

        p_ref[...] = _dot(hbuf[pl.ds(row0, ts), :], wbuf[order_ref[k]]).astype(BF16)

        @pl.when((k == n_steps - 1) & (i == nt - 1))
        def _():
            wcopy(0, me, sibling, True).wait_send()
            for j, chip in enumerate(chips):
                wcopy(1 + j, me, (*chip, c_), True).wait_send()
                wcopy(4 + j, (*chip, c_), sibling).wait_send()
            for kk in range(n_steps):
                keep(kk, arrivals[kk]).wait()
            for cp in small_own:
                cp.wait()
            for q in range(len(peers)):
                for a in range(n_small):
                    small_copy(q, a).wait_send()
                    small_copy(q, a, receive=True).wait_recv()

    first_pass = lambda k, i: jnp.where(k == 0, i, nt - 1)
    grid_spec = pltpu.PrefetchScalarGridSpec(
        num_scalar_prefetch=1, grid=(n_steps, nt),
        in_specs=[pl.BlockSpec((ts, D_MODEL), lambda k, i, o: (first_pass(k, i), 0)),
                  pl.BlockSpec((1, D_MODEL), lambda k, i, o: (0, 0)), ANY] + [ANY] * n_small,
        out_specs=[pl.BlockSpec((ts, W_BLK), lambda k, i, o: (i, o[k])),
                   pl.BlockSpec((D_MODEL, ts), lambda k, i, o: (0, first_pass(k, i))), ANY] + [ANY] * n_small,
        scratch_shapes=[pltpu.VMEM((N_DEV, D_MODEL, W_BLK), BF16), pltpu.VMEM((S, D_MODEL), BF16),
                        pltpu.SemaphoreType.DMA, pltpu.SemaphoreType.DMA((7,)), pltpu.SemaphoreType.DMA((7,)),
                        pltpu.SemaphoreType.DMA((n_steps,)),
                        pltpu.SemaphoreType.DMA((7, n_small)), pltpu.SemaphoreType.DMA((7, n_small)),
                        pltpu.SemaphoreType.DMA((n_small,))])
    return pl.pallas_call(
        body, name="gather_in_proj", grid_spec=grid_spec,
        out_shape=[jax.ShapeDtypeStruct((S, IN_COLS), BF16), jax.ShapeDtypeStruct((D_MODEL, S), BF16),
                   jax.ShapeDtypeStruct((D_MODEL, IN_COLS), BF16)]
        + [jax.ShapeDtypeStruct((N_DEV,) + s.shape, s.dtype) for s in smalls],
        compiler_params=_params(2),
    )(order, x, norm_g, w_sh, *smalls)


def _bias_table(rel_bias):
    wide = 1024
    n_rel = 384

    def body(r_ref, o_ref):
        col = lax.broadcasted_iota(jnp.int32, (1, wide), 1)
        k_minus_q = jnp.where(col < KB, col, col - wide)
        idx = jnp.clip(PADK - k_minus_q, -MAX_REL, MAX_REL) + MAX_REL
        pick = jnp.where(lax.broadcasted_iota(jnp.int32, (n_rel, wide), 0) == idx, 1.0, 0.0).astype(F32)
        f = jnp.dot(r_ref[...], pick, precision=lax.Precision.HIGHEST, preferred_element_type=F32) * LOG2E
        kcol = lax.broadcasted_iota(jnp.int32, (1, KB), 1)
        kc = kcol >> 6
        sub = lax.broadcasted_iota(jnp.int32, (8, 1), 0)
        for h in range(HEADS):
            f8 = jnp.broadcast_to(f[h:h + 1, :], (8, wide))
            base = f8
            for r in range(1, 8):
                base = jnp.where(sub == r, pltpu.roll(f8, r, 1), base)
            for qh in range(0, QB // 8, 2):
                rows = jnp.concatenate(
                    [(pltpu.roll(base, 8 * q, 1) if q else base)[:, 0:KB] for q in (qh, qh + 1)], axis=0)
                qc = (8 * qh) // CHUNK
                band = (kc >= qc) & (kc <= qc + N_LEFT)
                o_ref[h, 8 * qh:8 * qh + 16, :] = jnp.where(band, rows, NEG_BIG).astype(BF16)

    return pl.pallas_call(
        body, name="bias_table",
        out_shape=jax.ShapeDtypeStruct((HEADS, QB, KB), BF16),
        compiler_params=pltpu.CompilerParams(vmem_limit_bytes=VMEM_LIMIT),
    )(jnp.pad(rel_bias, ((0, 0), (0, n_rel - N_REL))))


KEY_GROUP = 4


def _load_keys(g, nb, p_hbm, kp, vp, sem):
    rows = KEY_GROUP * QB
    n_groups = p_hbm.shape[0] // rows

    def copies(c):
        src = pl.ds(c * rows, rows)
        dst = pl.ds(PADK + c * rows, rows)
        return (pltpu.make_async_copy(p_hbm.at[src, D_ATT:2 * D_ATT], kp.at[dst, :], sem.at[0, c]),
                pltpu.make_async_copy(p_hbm.at[src, 2 * D_ATT:3 * D_ATT], vp.at[dst, :], sem.at[1, c]))

    @pl.when(g == 0)
    def _():
        kp[0:PADK, :] = jnp.zeros((PADK, D_ATT), BF16)
        vp[0:PADK, :] = jnp.zeros((PADK, D_ATT), BF16)
        for c in range(n_groups):
            for cp in copies(c):
                cp.start()

    @pl.when((g % KEY_GROUP == 0) & (g < nb))
    def _():
        for cp in copies(g // KEY_GROUP):
            cp.wait()


def _attn_fwd(P, bias_tab):
    S = P.shape[0]
    nb = S // QB

    def body(q_ref, p_hbm, bias_ref, o_ref, ex_ref, rinv_ref, kp, vp, tab, sem):
        g = pl.program_id(0)
        _load_keys(g, nb, p_hbm, kp, vp, sem)

        @pl.when(g * QB <= PADK)
        def _():
            kcol = lax.broadcasted_iota(jnp.int32, (1, KB), 1)
            for h in range(HEADS):
                tab[h] = jnp.where(kcol + g * QB >= PADK, bias_ref[h], NEG_BIG).astype(BF16)

        start = pl.multiple_of(g * QB, QB)
        lane = lax.broadcasted_iota(jnp.int32, (1, 128), 1)
        half = lambda h: (lane < 64) if h % 2 == 0 else (lane >= 64)
        pair = lambda h: slice(128 * (h // 2), 128 * (h // 2 + 1))

        def scores(h):
            qp = q_ref[:, pair(h)] * SCALE
            qm = jnp.where(half(h), qp, jnp.zeros_like(qp))
            return (_dot_nt(qm, kp[pl.ds(start, KB), pair(h)]) * LOG2E).astype(BF16) + tab[h]

        def numerators(h, s):
            ex = jnp.exp2(s - jnp.max(s, axis=-1, keepdims=True))
            ex_ref[:, KB * h:KB * (h + 1)] = ex
            return ex

        def weighted_values(h, ex):
            vpair = vp[pl.ds(start, KB), pair(h)]
            o = _dot(ex, jnp.where(half(h), vpair, jnp.ones_like(vpair)))
            rinv = 1.0 / pltpu.roll(o, 64, 1)
            rinv_ref[:, h:h + 1] = rinv[:, 0:1] if h % 2 == 0 else 1.0 / o[:, 0:1]
            return o * rinv

        outs = []
        s_ahead = {0: scores(0), 1: scores(1)}
        ex_ahead = {0: numerators(0, s_ahead.pop(0))}
        for h in range(HEADS):
            if h + 2 < HEADS:
                s_ahead[h + 2] = scores(h + 2)
            if h + 1 < HEADS:
                ex_ahead[h + 1] = numerators(h + 1, s_ahead.pop(h + 1))
            outs.append(weighted_values(h, ex_ahead.pop(h)))
            if h % 2 == 1:
                o_ref[:, pair(h)] = jnp.where(lane < 64, outs[h - 1], outs[h]).astype(BF16)

    return pl.pallas_call(
        body, name="attn_fwd", grid=(nb,),
        out_shape=[jax.ShapeDtypeStruct((S, D_ATT), BF16), jax.ShapeDtypeStruct((S, HEADS * KB), BF16),
                   jax.ShapeDtypeStruct((S, HEADS), F32)],
        in_specs=[pl.BlockSpec((QB, D_ATT), lambda g: (g, 0)), ANY,
                  pl.BlockSpec((HEADS, QB, KB), lambda g: (0, 0, 0))],
        out_specs=[pl.BlockSpec((QB, D_ATT), lambda g: (g, 0)),
                   pl.BlockSpec((QB, HEADS * KB), lambda g: (g, 0)),
                   pl.BlockSpec((QB, HEADS), lambda g: (g, 0))],
        scratch_shapes=[pltpu.VMEM((S + PADK, D_ATT), BF16), pltpu.VMEM((S + PADK, D_ATT), BF16),
                        pltpu.VMEM((HEADS, QB, KB), BF16),
                        pltpu.SemaphoreType.DMA((2, S // (KEY_GROUP * QB)))],
        compiler_params=_params(1),
    )(P, P, bias_tab)


def _token_local(x, tgt, P, att, proj_g, w_out, cw_g, conv_b, final_g):
    S = x.shape[0]
    ts = 256
    nt = S // ts
    hb = 16

    def body(x_ref, t_ref, s1_ref, s2_ref, s3_ref, h1_ref, h2_ref, att_ref,
             pg_ref, wo_ref, cwg_ref, cb_ref, g2_ref,
             dx2_ref, dg_ref, datt_ref, dwo_ref, dproj_ref, sm1_ref, sm2_ref,
             carry, wao_ref, wco_ref, cw_ref, dwo_acc, dwao_acc, dwco_acc):
        i = pl.program_id(0)
        t = nt - 1 - i

        @pl.when(i == 0)
        def _():
            dwo_acc[...] = jnp.zeros_like(dwo_acc)
            dwao_acc[...] = jnp.zeros_like(dwao_acc)
            dwco_acc[...] = jnp.zeros_like(dwco_acc)
            lane = lax.broadcasted_iota(jnp.int32, (1, 128), 1)
            for j in range(N_DEV):
                wao_ref[:, 128 * j:128 * (j + 1)] = pg_ref[j, :, 0:128]
                wco_ref[:, 128 * j:128 * (j + 1)] = pg_ref[j, :, 128:256]
            for p in range(N_DEV // 2):
                cw_ref[:, 128 * p:128 * (p + 1)] = jnp.where(
                    lane < 64, cwg_ref[2 * p], pltpu.roll(cwg_ref[2 * p + 1], 64, 1))
            sm1_ref[...] = jnp.zeros_like(sm1_ref)
            sm2_ref[...] = jnp.zeros_like(sm2_ref)
            carry[...] = jnp.zeros_like(carry)

        za = s1_ref[:, 0:512]
        gb = s1_ref[:, 512:1024]
        gc = s1_ref[:, 1024:1536].astype(F32)
        u = s2_ref[:, 0:512].astype(F32)
        zc = s2_ref[:, 512:1024]
        ga = jnp.concatenate([s2_ref[:, 1024:1536], s3_ref[:, 0:512]], axis=1)
        gv = s3_ref[:, 512:1536]
        att = att_ref[...]

        sa = _sigmoid(za)
        silu_a = za * sa
        att_g = att * silu_a
        y_att = _dot(att_g, wao_ref[...])

        cu = gc * u
        keep = jnp.where(t > 0, 1.0, 0.0).astype(F32)
        hcu = (h1_ref[hb - 8:hb, 1024:1536].astype(F32) * h2_ref[hb - 8:hb, 0:512].astype(F32)) * keep
        cu_ext = jnp.concatenate([hcu, cu], axis=0)
        cu_m1 = pltpu.roll(cu_ext, 1, 0)[8:]
        cu_m2 = pltpu.roll(cu_ext, 2, 0)[8:]
        w0, w1, w2 = cw_ref[0:1, :], cw_ref[1:2, :], cw_ref[2:3, :]
        vconv = w0 * cu_m2 + w1 * cu_m1 + w2 * cu + cb_ref[...]
        vcb = vconv.astype(BF16)
        sc = _sigmoid(zc)
        silu_c = zc * sc
        cg = gb * vcb * silu_c
        sga = _sigmoid(ga)
        sgv = _sigmoid(gv)
        y_conv = _dot(cg, wco_ref[...])

        yab, ycb = y_att.astype(BF16), y_conv.astype(BF16)
        m = sga * yab + sgv * ycb
        x2 = x_ref[...] + _dot(m, wo_ref[...])
        r2 = lax.rsqrt(jnp.mean(x2 * x2, axis=-1, keepdims=True) + EPS)
        xn2 = x2 * r2
        g2 = g2_ref[...]
        err = xn2 * g2 - t_ref[...]
        sm1_ref[1:2, :] += jnp.sum(err * err, axis=0, keepdims=True) * (0.5 / D_MODEL)

        dy = err * (1.0 / D_MODEL)
        sm1_ref[0:1, :] += jnp.sum(dy * xn2, axis=0, keepdims=True)
        dxn = dy * g2
        dx2 = r2 * (dxn - xn2 * jnp.mean(dxn * xn2, axis=-1, keepdims=True))
        dx2_ref[...] = dx2
        dx2b = dx2.astype(BF16)
        dwo_acc[...] += _dot_tn(m, dx2b)
        dm = _dot_nt(dx2b, wo_ref[...])
        dmb = dm.astype(BF16)
        dya = dmb * sga
        dyc = dmb * sgv
        dg_ref[:, 2560:3584] = dmb * yab * (sga * (1.0 - sga))
        dg_ref[:, 3584:4608] = dmb * ycb * (sgv * (1.0 - sgv))
        dwao_acc[...] += _dot_tn(att_g, dya)
        dwco_acc[...] += _dot_tn(cg, dyc)
        datt_g = _dot_nt(dya, wao_ref[...])
        dcg = _dot_nt(dyc, wco_ref[...])
        dagb, dcgb = datt_g.astype(BF16), dcg.astype(BF16)
        datt_ref[...] = dagb * silu_a
        dg_ref[:, 0:512] = dagb * att * (sa + silu_a * (1.0 - sa))
        dg_ref[:, 512:1024] = dcgb * vcb * silu_c
        dg_ref[:, 2048:2560] = dcgb * gb * vcb * (sc + silu_c * (1.0 - sc))
        dv = dcg * (gb * silu_c).astype(F32)
        sm2_ref[3:4, :] += jnp.sum(dv, axis=0, keepdims=True)
        sm2_ref[0:1, :] += jnp.sum(dv * cu_m2, axis=0, keepdims=True)
        sm2_ref[1:2, :] += jnp.sum(dv * cu_m1, axis=0, keepdims=True)
        sm2_ref[2:3, :] += jnp.sum(dv * cu, axis=0, keepdims=True)
        dv_ext = jnp.concatenate([dv, carry[...]], axis=0)
        dv_p1 = pltpu.roll(dv_ext, ts + 7, 0)[0:ts]
        dv_p2 = pltpu.roll(dv_ext, ts + 6, 0)[0:ts]
        dcu = w2 * dv + w1 * dv_p1 + w0 * dv_p2
        carry[...] = dv[0:8, :]
        dg_ref[:, 1024:1536] = (dcu * u).astype(BF16)
        dg_ref[:, 1536:2048] = (dcu * gc).astype(BF16)

        @pl.when(i == nt - 1)
        def _():
            dwo_ref[...] = dwo_acc[...].astype(BF16)
            for j in range(N_DEV):
                dproj_ref[j, :, 0:128] = dwao_acc[:, 128 * j:128 * (j + 1)].astype(BF16)
                dproj_ref[j, :, 128:256] = dwco_acc[:, 128 * j:128 * (j + 1)].astype(BF16)

    tile = lambda w: pl.BlockSpec((ts, w), lambda i: (nt - 1 - i, 0))
    seg = lambda c: pl.BlockSpec((ts, 1536), lambda i: (nt - 1 - i, c))
    halo = lambda c: pl.BlockSpec((hb, 1536), lambda i: (jnp.maximum((nt - 1 - i) * (ts // hb) - 1, 0), c))
    full = lambda a: pl.BlockSpec(a.shape, lambda i: (0,) * a.ndim)
    acc = lambda r, c: pl.BlockSpec((r, c), lambda i: (0, 0))
    return pl.pallas_call(
        body, name="token_local", grid=(nt,),
        out_shape=[jax.ShapeDtypeStruct((S, D_MODEL), F32), jax.ShapeDtypeStruct((S, IN_COLS), BF16),
                   jax.ShapeDtypeStruct((S, D_ATT), BF16), jax.ShapeDtypeStruct((D_MODEL, D_MODEL), BF16),
                   jax.ShapeDtypeStruct(proj_g.shape, BF16),
                   jax.ShapeDtypeStruct((8, D_MODEL), F32), jax.ShapeDtypeStruct((8, D_CONV), F32)],
        in_specs=[tile(D_MODEL), tile(D_MODEL), seg(1), seg(2), seg(3), halo(1), halo(2), tile(D_ATT),
                  full(proj_g), full(w_out), full(cw_g), full(conv_b), full(final_g)],
        out_specs=[tile(D_MODEL), tile(GATE_COLS), tile(D_ATT), acc(D_MODEL, D_MODEL), full(proj_g),
                   acc(8, D_MODEL), acc(8, D_CONV)],
        scratch_shapes=[pltpu.VMEM((8, D_CONV), F32),
                        pltpu.VMEM((D_ATT, D_MODEL), BF16), pltpu.VMEM((D_CONV, D_MODEL), BF16),
                        pltpu.VMEM((8, D_CONV), F32), pltpu.VMEM((D_MODEL, D_MODEL), F32),
                        pltpu.VMEM((D_ATT, D_MODEL), F32), pltpu.VMEM((D_CONV, D_MODEL), F32)],
        compiler_params=_params(1),
    )(x, tgt, P, P, P, P, P, att, proj_g, w_out, cw_g, conv_b, final_g)


def _fold_diagonals(d_ref, o_ref):
    wide = D_MODEL
    sub = lax.broadcasted_iota(jnp.int32, (8, 1), 0)
    col = lax.broadcasted_iota(jnp.int32, (1, wide), 1)
    pad = jnp.zeros((8, wide - KB), F32)
    for h in range(HEADS):
        acc = jnp.concatenate([d_ref[h, 0:8, :], pad], axis=1)
        for qh in range(1, QB // 8):
            a = jnp.concatenate([d_ref[h, 8 * qh:8 * qh + 8, :], pad], axis=1)
            acc = acc + pltpu.roll(a, wide - 8 * qh, 1)
        for r in range(1, 8):
            acc = jnp.where(sub == r, pltpu.roll(acc, wide - r, 1), acc)
        vec = jnp.sum(acc, axis=0, keepdims=True)
        far = (col <= PADK - MAX_REL) | (col > KB)
        tail = jnp.sum(jnp.where(far, vec, 0.0), axis=-1, keepdims=True)
        o_ref[h:h + 1, :] = jnp.where(col == wide - 1, tail, vec)


def _attn_bwd(P, att, datt, ex, rinv, dP):
    S = P.shape[0]
    nb = S // QB

    def body(q_ref, att_ref, datt_ref, ex_ref, rinv_ref, p_hbm, dp_hbm, out_ref, dbias_ref,
             kp, vp, dq_ring, dk_ring, dv_ring, db_ref, sem):
        g = pl.program_id(0)

        _load_keys(g, nb, p_hbm, kp, vp, sem)

        @pl.when(g == 0)
        def _():
            db_ref[...] = jnp.zeros_like(db_ref)
            dk_ring[...] = jnp.zeros_like(dk_ring)
            dv_ring[...] = jnp.zeros_like(dv_ring)

        s_new = g % 3
        s_mid = (g + 2) % 3
        s_old = (g + 1) % 3

        @pl.when(g < nb)
        def _():
            start = pl.multiple_of(g * QB, QB)
            lane = lax.broadcasted_iota(jnp.int32, (1, 128), 1)
            for p in range(HEADS // 2):
                cols = slice(128 * p, 128 * (p + 1))
                qp = q_ref[:, cols] * SCALE
                op = att_ref[:, cols].astype(F32)
                dop = datt_ref[:, cols]
                kpair = kp[pl.ds(start, KB), cols]
                vpair = vp[pl.ds(start, KB), cols]
                dqs = []
                dk_acc = jnp.zeros((KB, 128), F32)
                dv_acc = jnp.zeros((KB, 128), F32)
                for e in range(2):
                    h = 2 * p + e
                    lm = (lane < 64) if e == 0 else (lane >= 64)
                    qm = jnp.where(lm, qp, jnp.zeros_like(qp))
                    dom = jnp.where(lm, dop, jnp.zeros_like(dop))
                    exh = ex_ref[:, KB * h:KB * (h + 1)]
                    rinv = rinv_ref[:, h:h + 1]
                    domf = dom.astype(F32)
                    dp = _dot_nt(dom, vpair)
                    delta = jnp.sum(domf * op, axis=-1, keepdims=True)
                    dsb = exh * ((dp - delta) * rinv).astype(BF16)
                    db_ref[h] += dsb.astype(F32)
                    dqs.append(_dot(dsb, kpair) * SCALE)
                    dk_acc = dk_acc + _dot_tn(dsb, qm)
                    dv_acc = dv_acc + _dot_tn(exh, (domf * rinv).astype(BF16))
                dq_ring[s_new, :, cols] = jnp.where(lane < 64, dqs[0], dqs[1])
                dk_ring[s_old, :, cols] += dk_acc[0:QB]
                dk_ring[s_mid, :, cols] += dk_acc[QB:2 * QB]
                dk_ring[s_new, :, cols] = dk_acc[2 * QB:3 * QB]
                dv_ring[s_old, :, cols] += dv_acc[0:QB]
                dv_ring[s_mid, :, cols] += dv_acc[QB:2 * QB]
                dv_ring[s_new, :, cols] = dv_acc[2 * QB:3 * QB]

        @pl.when(g >= 2)
        def _():
            out_ref[:, 0:D_ATT] = dq_ring[s_old].astype(BF16)
            out_ref[:, D_ATT:2 * D_ATT] = dk_ring[s_old].astype(BF16)
            out_ref[:, 2 * D_ATT:3 * D_ATT] = dv_ring[s_old].astype(BF16)

        @pl.when(g == nb + 1)
        def _():
            _fold_diagonals(db_ref, dbias_ref)

    qblk = lambda w: pl.BlockSpec((QB, w), lambda g: (jnp.minimum(g, nb - 1), 0))
    return pl.pallas_call(
        body, name="attn_bwd", grid=(nb + 2,),
        out_shape=[jax.ShapeDtypeStruct((S, IN_COLS), BF16), jax.ShapeDtypeStruct((HEADS, D_MODEL), F32)],
        in_specs=[qblk(D_ATT), qblk(D_ATT), qblk(D_ATT), qblk(HEADS * KB), qblk(HEADS), ANY, ANY],
        out_specs=[pl.BlockSpec((QB, 3 * D_ATT), lambda g: (jnp.maximum(g - 2, 0), GATE_COLS // (3 * D_ATT))),
                   pl.BlockSpec((HEADS, D_MODEL), lambda g: (0, 0))],
        input_output_aliases={6: 0},
        scratch_shapes=[pltpu.VMEM((S + PADK, D_ATT), BF16), pltpu.VMEM((S + PADK, D_ATT), BF16),
                        pltpu.VMEM((3, QB, D_ATT), F32), pltpu.VMEM((3, QB, D_ATT), F32),
                        pltpu.VMEM((3, QB, D_ATT), F32), pltpu.VMEM((HEADS, QB, KB), F32),
                        pltpu.SemaphoreType.DMA((2, S // (KEY_GROUP * QB)))],
        compiler_params=_params(1),
    )(P, att, datt, ex, rinv, P, dP)


def _dp_block(j):
    return (j + GATE_COLS // W_BLK) % N_DEV


def _in_proj_bwd(x, norm_g, dx2, dP, w_in_g):
    S = x.shape[0]
    ts = 512

    def body(x_ref, g_ref, dx2_ref, dp_ref, w_ref, gx_ref, dn_ref):
        @pl.when(pl.program_id(0) == 0)
        def _():
            dn_ref[...] = jnp.zeros_like(dn_ref)

        dh = _dot_nt(dp_ref[...], w_ref[...])
        xf = x_ref[...]
        r = lax.rsqrt(jnp.mean(xf * xf, axis=-1, keepdims=True) + EPS)
        xn = xf * r
        dn_ref[0:1, :] += jnp.sum(dh * xn, axis=0, keepdims=True)
        dhg = dh * g_ref[...]
        gx_ref[...] = dx2_ref[...] + r * (dhg - xn * jnp.mean(dhg * xn, axis=-1, keepdims=True))

    tile = lambda w: pl.BlockSpec((ts, w), lambda i: (i, 0))
    return pl.pallas_call(
        body, name="in_proj_bwd", grid=(S // ts,),
        out_shape=[jax.ShapeDtypeStruct((S, D_MODEL), F32), jax.ShapeDtypeStruct((8, D_MODEL), F32)],
        in_specs=[tile(D_MODEL), pl.BlockSpec((1, D_MODEL), lambda i: (0, 0)), tile(D_MODEL),
                  tile(IN_COLS),
                  pl.BlockSpec((D_MODEL, IN_COLS), lambda i: (0, 0))],
        out_specs=[tile(D_MODEL), pl.BlockSpec((8, D_MODEL), lambda i: (0, 0))],
        compiler_params=_params(1),
    )(x, norm_g, dx2, dP, w_in_g)


SCATTER_MASKS = ((3, 4, 5, 2, 7, 6, 1, 0), (5, 2, 3, 4, 7, 6, 1, 0))


def _w_in_grad_scatter(ht, dP, d_proj, d_wo, pack, order):
    S = ht.shape[1]
    ts = min(S, 2048)
    nt = S // ts
    n_steps = 8

    def body(order_ref, ht_ref, d_ref, proj_hbm, wo_hbm, pack_hbm, g_ref, rproj, rwo, rpack,
             acc, stage, rsib, rici, d2d_send, d2d_recv, ici_send, ici_recv, small_send, small_recv, local_sems):
        k, i = pl.program_id(0), pl.program_id(1)
        x, y, c = _mesh_pos()
        my = _flat((x, y, c))
        sibling = (x, y, 1 - c)
        owners = [(x ^ (1 - c), y ^ c, c), (x ^ c, y ^ (1 - c), c), (1 - x, 1 - y, c)]
        peers = [sibling, (1 - x, y, c), (x, 1 - y, c), (1 - x, 1 - y, c),
                 (1 - x, y, 1 - c), (x, 1 - y, 1 - c), (1 - x, 1 - y, 1 - c)]
        small = ((proj_hbm, rproj, True), (wo_hbm, rwo, True), (pack_hbm, rpack, False))
        n_small = len(small)

        def small_copy(kk, a, receive=False):
            src, dst, per_peer = small[a]
            slot = _flat(peers[kk]) if receive else my
            return pltpu.make_async_remote_copy(
                src_ref=src.at[_flat(peers[kk])] if per_peer else src, dst_ref=dst.at[slot],
                send_sem=small_send.at[kk, a], recv_sem=small_recv.at[kk, a],
                device_id=peers[kk], device_id_type=MESH)

        def to_sibling(t):
            return pltpu.make_async_remote_copy(
                src_ref=stage.at[0], dst_ref=rsib.at[t % 2], send_sem=d2d_send.at[t], recv_sem=d2d_recv.at[t],
                device_id=sibling, device_id_type=MESH)

        def to_owner(t):
            return pltpu.make_async_remote_copy(
                src_ref=stage.at[1], dst_ref=rici.at[t], send_sem=ici_send.at[t], recv_sem=ici_recv.at[t],
                device_id=owners[t], device_id_type=MESH)

        own_small = [pltpu.make_async_copy(src.at[my] if per_peer else src, dst.at[my], local_sems.at[a])
                     for a, (src, dst, per_peer) in enumerate(small)]

        @pl.when((k == 0) & (i == 0))
        def _():
            for cp in own_small:
                cp.start()
            for kk in range(len(peers)):
                for a in range(n_small):
                    small_copy(kk, a).start()

        @pl.when(i == 0)
        def _():
            acc[...] = jnp.zeros_like(acc)

        acc[...] += _dot(ht_ref[...], d_ref[...])

        @pl.when(i == nt - 1)
        def _():
            for s in range(n_steps):
                @pl.when(k == s)
                def _():
                    t = s // 2
                    if s % 2 == 0:
                        if t >= 1:
                            to_sibling(t - 1).wait_send()
                        stage[0] = acc[...].astype(BF16)
                        to_sibling(t).start()
                    elif t < 3:
                        if t >= 1:
                            to_owner(t - 1).wait_send()
                        to_sibling(t).wait_recv()
                        stage[1] = (acc[...] + rsib[t % 2].astype(F32)).astype(BF16)
                        to_owner(t).start()
                    else:
                        to_sibling(t).wait_recv()
                        total = acc[...] + rsib[t % 2].astype(F32)
                        for j in range(3):
                            to_owner(j).wait_recv()
                            total = total + rici[j].astype(F32)
                        g_ref[...] = total
                        to_owner(2).wait_send()
                        to_sibling(3).wait_send()
                        for q in range(len(peers)):
                            for a in range(n_small):
                                small_copy(q, a).wait_send()
                                small_copy(q, a, receive=True).wait_recv()
                        for cp in own_small:
                            cp.wait()

    blk = (D_MODEL, W_BLK)
    grid_spec = pltpu.PrefetchScalarGridSpec(
        num_scalar_prefetch=1, grid=(n_steps, nt),
        in_specs=[pl.BlockSpec((D_MODEL, ts), lambda k, i, o: (0, i)),
                  pl.BlockSpec((ts, W_BLK), lambda k, i, o: (i, _dp_block(o[k]))),
                  ANY, ANY, ANY],
        out_specs=[pl.BlockSpec(blk, lambda k, i, o: (0, 0)), ANY, ANY, ANY],
        scratch_shapes=[pltpu.VMEM(blk, F32), pltpu.VMEM((2,) + blk, BF16),
                        pltpu.VMEM((2,) + blk, BF16), pltpu.VMEM((3,) + blk, BF16),
                        pltpu.SemaphoreType.DMA((4,)), pltpu.SemaphoreType.DMA((4,)),
                        pltpu.SemaphoreType.DMA((3,)), pltpu.SemaphoreType.DMA((3,)),
                        pltpu.SemaphoreType.DMA((7, 3)), pltpu.SemaphoreType.DMA((7, 3)),
                        pltpu.SemaphoreType.DMA((3,))])
    return pl.pallas_call(
        body, name="w_in_grad_scatter", grid_spec=grid_spec,
        out_shape=[jax.ShapeDtypeStruct(blk, F32),
                   jax.ShapeDtypeStruct(d_proj.shape, BF16), jax.ShapeDtypeStruct(d_wo.shape, BF16),
                   jax.ShapeDtypeStruct((N_DEV,) + pack.shape, F32)],
        compiler_params=_params(2),
    )(order, ht, dP, d_proj, d_wo, pack)


def _adamw(w, g, m, v):
    m = ADAM_B1 * m + (1.0 - ADAM_B1) * g
    v = ADAM_B2 * v + (1.0 - ADAM_B2) * (g * g)
    m_hat = m / (1.0 - ADAM_B1 ** ADAM_STEP)
    v_hat = v / (1.0 - ADAM_B2 ** ADAM_STEP)
    delta = -ADAM_LR * (m_hat / (jnp.sqrt(v_hat) + ADAM_EPS) + ADAM_WD * w)
    return delta, m, v


def _adamw_all(g_w_in, w_in_params, r_proj, r_wo, mid_params, r_pack, small_params):
    R, C = g_w_in.shape
    tr = min(R, 256)
    wide = 384

    def update(params, n, g, outs):
        w, m, v = (r[...] for r in params[3 * n:3 * n + 3])
        outs[4 * n][...] = g
        outs[4 * n + 1][...], outs[4 * n + 2][...], outs[4 * n + 3][...] = _adamw(w, g, m, v)

    def body(*refs):
        g_ref, w_ref, m_ref, v_ref, rp_ref, rw_ref = refs[:6]
        mid_in, pack_ref, small_in, outs = refs[6:15], refs[15], refs[16:31], refs[31:]
        win_out, mid_out, loss_ref, small_out = outs[0:4], outs[4:16], outs[16], outs[17:]
        g = g_ref[...]
        win_out[0][...] = g
        win_out[1][...], win_out[2][...], win_out[3][...] = _adamw(w_ref[...], g, m_ref[...], v_ref[...])

        @pl.when(pl.program_id(0) == 0)
        def _():
            def total(part):
                g = part(0).astype(F32)
                for s in range(1, N_DEV):
                    g = g + part(s).astype(F32)
                return g

            for n, g in enumerate((total(lambda s: rp_ref[s, :, 0:128]), total(lambda s: rp_ref[s, :, 128:256]),
                                   total(lambda s: rw_ref[s]))):
                update(mid_in, n, g, mid_out)

            tot = total(lambda s: pack_ref[s])
            me = _flat(_mesh_pos())
            loss_ref[...] = jnp.sum(tot[2:3, :], axis=-1, keepdims=True)
            mine = pltpu.roll(tot[0:8, 0:D_CONV], (D_CONV - 64 * me) % D_CONV, 1)
            col = lax.broadcasted_iota(jnp.int32, (D_MODEL, wide), 0)
            idx = lax.broadcasted_iota(jnp.int32, (D_MODEL, wide), 1)
            near = (idx > MAX_REL - CHUNK) & (idx < 2 * MAX_REL) & (col == PADK + MAX_REL - idx)
            far = (idx == 2 * MAX_REL) & (col == D_MODEL - 1)
            perm = jnp.where(near | far, 1.0, 0.0).astype(F32)
            g_rel = jnp.dot(tot[8:16], perm, precision=lax.Precision.HIGHEST, preferred_element_type=F32)
            for n, g in enumerate((tot[0:1], tot[1:2], mine[3:6, 0:64], tot[6:7, 0:D_CONV], g_rel[:, 0:N_REL])):
                update(small_in, n, g, small_out)

    tile = pl.BlockSpec((tr, C), lambda i: (i, 0))
    full = lambda a: pl.BlockSpec(a.shape, lambda i: (0,) * len(a.shape))
    whole = [r_proj, r_wo, *mid_params, r_pack, *small_params]
    out_shape = ([jax.ShapeDtypeStruct((R, C), F32)] * 4
                 + [jax.ShapeDtypeStruct(mid_params[3 * n].shape, F32) for n in range(3) for _ in range(4)]
                 + [jax.ShapeDtypeStruct((1, 1), F32)]
                 + [jax.ShapeDtypeStruct(small_params[3 * n].shape, F32) for n in range(5) for _ in range(4)])
    return pl.pallas_call(
        body, name="adamw_all", grid=(R // tr,),
        out_shape=out_shape,
        in_specs=[tile] * 4 + [full(a) for a in whole],
        out_specs=[tile] * 4 + [full(o) for o in out_shape[4:]],
        compiler_params=_params(1),
    )(g_w_in, *w_in_params, *whole)


def _pad_row(a, width=D_MODEL):
    a = a.reshape(-1, a.shape[-1])
    return jnp.pad(a, ((0, 0), (0, width - a.shape[-1])))


def kernel(x, norm_g, w_in, rel_bias, w_att_out, conv_w, conv_b, w_conv_out, w_out, final_norm_g, loss_target, m_norm_g, m_w_in, m_rel_bias, m_w_att_out, m_conv_w, m_conv_b, m_w_conv_out, m_w_out, m_final_norm_g, v_norm_g, v_w_in, v_rel_bias, v_w_att_out, v_conv_w, v_conv_b, v_w_conv_out, v_w_out, v_final_norm_g):
    S = x.shape[1]
    x2d = x.reshape(S, D_MODEL)
    tgt = loss_target.reshape(S, D_MODEL)
    me = 4 * lax.axis_index("x") + 2 * lax.axis_index("y") + lax.axis_index("c")
    row = lambda a: a.reshape(1, D_MODEL)

    proj_sh = jnp.concatenate([w_att_out[0], w_conv_out[0]], axis=1).astype(BF16)
    cw_sh = jnp.pad(conv_w[0], ((0, 5), (0, 64)))
    P, ht, w_in_g, proj_g, w_out_g, cw_g = _gather_in_proj(
        x2d, norm_g, w_in[0].astype(BF16), [proj_sh, w_out[0].astype(BF16), cw_sh],
        me ^ _by_core(GATHER_MASKS))

    bias_tab = _bias_table(rel_bias[0])
    att, ex, rinv = _attn_fwd(P, bias_tab)
    dx2, dP, datt, d_wo, d_proj, sm1, sm2 = _token_local(
        x2d, tgt, P, att, proj_g, w_out_g.reshape(D_MODEL, D_MODEL), cw_g, conv_b, row(final_norm_g))
    dP, dbias = _attn_bwd(P, att, datt, ex, rinv, dP)
    grad_x, dnorm = _in_proj_bwd(x2d, norm_g, dx2, dP, w_in_g)

    pack = jnp.concatenate([dnorm[0:1], sm1[0:2], _pad_row(sm2[0:4]), jnp.zeros((1, D_MODEL), F32), dbias],
                           axis=0)
    g_win_sum, r_proj, r_wo, r_pack = _w_in_grad_scatter(
        ht, dP, d_proj, d_wo.reshape(N_DEV, 128, D_MODEL), pack, me ^ _by_core(SCATTER_MASKS))

    upd = _adamw_all(g_win_sum, (w_in[0], m_w_in[0], v_w_in[0]), r_proj, r_wo,
                     (w_att_out[0], m_w_att_out[0], v_w_att_out[0], w_conv_out[0], m_w_conv_out[0], v_w_conv_out[0],
                      w_out[0], m_w_out[0], v_w_out[0]),
                     r_pack,
                     (norm_g, m_norm_g, v_norm_g, row(final_norm_g), row(m_final_norm_g), row(v_final_norm_g),
                      conv_w[0], m_conv_w[0], v_conv_w[0], conv_b, m_conv_b, v_conv_b,
                      rel_bias[0], m_rel_bias[0], v_rel_bias[0]))
    res = {}
    for n, name in enumerate(("w_in", "w_att_out", "w_conv_out", "w_out")):
        res[name] = upd[4 * n:4 * n + 4]
    loss = upd[16].reshape(())
    for n, name in enumerate(("norm_g", "final_norm_g", "conv_w", "conv_b", "rel_bias")):
        res[name] = upd[17 + 4 * n:21 + 4 * n]

    leading = {"norm_g": (1, D_MODEL), "final_norm_g": (D_MODEL,), "conv_b": (1, D_CONV)}
    outs = []
    for kind in range(4):
        for name in ("norm_g", "w_in", "rel_bias", "w_att_out", "conv_w", "conv_b", "w_conv_out", "w_out",
                     "final_norm_g"):
            a = res[name][kind]
            outs.append(a.reshape(leading[name]) if name in leading else a[None])
    return (loss, grad_x.reshape(1, S, D_MODEL), *outs)
```

```python
import jax
import jax.numpy as jnp
from jax import lax
from jax.experimental import pallas as pl
from jax.experimental.pallas import tpu as pltpu

F32 = jnp.float32
BF16 = jnp.bfloat16

D_MODEL = 1024
CHUNK = 64
N_LEFT = 8
HEADS = 8
D_ATT = 512
D_CONV = 512
MAX_REL = 128
N_REL = 2 * MAX_REL + 1
IN_COLS = 6144
EPS = 1e-6
NEG_BIG = -1e30
N_DEV = 8
W_BLK = IN_COLS // N_DEV
QB = 4 * CHUNK
KB = QB + N_LEFT * CHUNK
PADK = N_LEFT * CHUNK
SCALE = 64 ** -0.5
LOG2E = 1.4426950408889634
GATE_COLS = IN_COLS - 3 * D_ATT

ADAM_LR = 0.001
ADAM_B1 = 0.9
ADAM_B2 = 0.999
ADAM_EPS = 1e-08
ADAM_WD = 0.01
ADAM_STEP = 10

VMEM_LIMIT = 56 * 1024 * 1024

MESH = pl.DeviceIdType.MESH
ANY = pl.BlockSpec(memory_space=pl.ANY)


def _params(n_grid, vmem_limit=VMEM_LIMIT):
    return pltpu.CompilerParams(dimension_semantics=("arbitrary",) * n_grid,
                                vmem_limit_bytes=vmem_limit)


def _dot(a, b):
    return jnp.dot(a, b, preferred_element_type=F32)


def _dot_nt(a, b):
    return lax.dot_general(a, b, (((1,), (1,)), ((), ())), preferred_element_type=F32)


def _dot_tn(a, b):
    return lax.dot_general(a, b, (((0,), (0,)), ((), ())), preferred_element_type=F32)


def _sigmoid(z):
    return 0.5 * jnp.tanh(0.5 * z) + 0.5


def _mesh_pos():
    return lax.axis_index("x"), lax.axis_index("y"), lax.axis_index("c")


def _flat(p):
    return 4 * p[0] + 2 * p[1] + p[2]


def _by_core(masks):
    m0, m1 = (jnp.array(m, jnp.int32) for m in masks)
    return jnp.where(lax.axis_index("c") == 0, m0, m1)


GATHER_MASKS = ((0, 1, 4, 3, 2, 5, 6, 7), (0, 1, 2, 5, 4, 3, 6, 7))


def _gather_in_proj(x, norm_g, w_sh, smalls, order):
    S = x.shape[0]
    ts = 1024
    nt = S // ts
    n_small = len(smalls)
    n_steps = N_DEV

    def body(order_ref, x_ref, g_ref, w_hbm, *rest):
        small_in = rest[:n_small]
        p_ref, ht_ref, wg_hbm = rest[n_small:n_small + 3]
        small_out = rest[n_small + 3:2 * n_small + 3]
        (wbuf, hbuf, own_sem, send_sems, recv_sems, out_sems,
         small_send, small_recv, small_local) = rest[2 * n_small + 3:]
        k, i = pl.program_id(0), pl.program_id(1)
        x_, y_, c_ = _mesh_pos()
        me, sibling = (x_, y_, c_), (x_, y_, 1 - c_)
        my = _flat(me)
        chips = [(x_ ^ (1 - c_), y_ ^ c_), (x_ ^ c_, y_ ^ (1 - c_)), (1 - x_, 1 - y_)]
        peers = [sibling] + [(*chip, c_) for chip in chips] + [(*chip, 1 - c_) for chip in chips]

        def wcopy(sem, block, to, from_input=False):
            dst = wbuf.at[_flat(block)]
            return pltpu.make_async_remote_copy(
                src_ref=w_hbm if from_input else dst, dst_ref=dst,
                send_sem=send_sems.at[sem], recv_sem=recv_sems.at[sem], device_id=to, device_id_type=MESH)

        def small_copy(q, a, receive=False):
            slot = _flat(peers[q]) if receive else my
            return pltpu.make_async_remote_copy(
                src_ref=small_in[a], dst_ref=small_out[a].at[slot],
                send_sem=small_send.at[q, a], recv_sem=small_recv.at[q, a],
                device_id=peers[q], device_id_type=MESH)

        def keep(step, block):
            col = pl.multiple_of(_dp_block(_flat(block)) * W_BLK, 128)
            return pltpu.make_async_copy(wbuf.at[_flat(block)], wg_hbm.at[:, pl.ds(col, W_BLK)], out_sems.at[step])

        own = pltpu.make_async_copy(w_hbm, wbuf.at[my], own_sem)
        small_own = [pltpu.make_async_copy(small_in[a], small_out[a].at[my], small_local.at[a])
                     for a in range(n_small)]
        passed_on = [(*chips[1], 1 - c_), (*chips[0], 1 - c_), (*chips[2], 1 - c_)]
        arrivals = [me, sibling]
        for j in range(3):
            arrivals += [(*chips[j], c_), passed_on[j]]

        @pl.when(i == 0)
        def _():
            for kk in range(n_steps):
                @pl.when(k == kk)
                def _():
                    j = kk // 2 - 1
                    if kk == 0:
                        own.start()
                        wcopy(0, me, sibling, True).start()
                        wcopy(1, me, (*chips[0], c_), True).start()
                        own.wait()
                    elif kk == 1:
                        wcopy(0, sibling, me).wait_recv()
                        wcopy(2, me, (*chips[1], c_), True).start()
                    elif kk % 2 == 0:
                        wcopy(1 + j, (*chips[j], c_), me).wait_recv()
                        wcopy(4 + j, (*chips[j], c_), sibling).start()
                        if kk == 2:
                            wcopy(3, me, (*chips[2], c_), True).start()
                    else:
                        wcopy(4 + j, passed_on[j], me).wait_recv()
                        if kk == 3:
                            for cp in small_own:
                                cp.start()
                            for q in range(len(peers)):
                                for a in range(n_small):
                                    small_copy(q, a).start()
                    keep(kk, arrivals[kk]).start()

        row0 = pl.multiple_of(i * ts, ts)

        @pl.when(k == 0)
        def _():
            xf = x_ref[...]
            r = lax.rsqrt(jnp.mean(xf * xf, axis=-1, keepdims=True) + EPS)
            hf = (xf * r) * g_ref[...]
            hbuf[pl.ds(row0, ts), :] = hf.astype(BF16)
            ht_ref[...] = hf.astype(BF16).T

        p_ref[...] = _dot(hbuf[pl.ds(row0, ts), :], wbuf[order_ref[k]]).astype(BF16)

        @pl.when((k == n_steps - 1) & (i == nt - 1))
        def _():
            wcopy(0, me, sibling, True).wait_send()
            for j, chip in enumerate(chips):
                wcopy(1 + j, me, (*chip, c_), True).wait_send()
                wcopy(4 + j, (*chip, c_), sibling).wait_send()
            for kk in range(n_steps):
                keep(kk, arrivals[kk]).wait()
            for cp in small_own:
                cp.wait()
            for q in range(len(peers)):
                for a in range(n_small):
                    small_copy(q, a).wait_send()
                    small_copy(q, a, receive=True).wait_recv()

    first_pass = lambda k, i: jnp.where(k == 0, i, nt - 1)
    grid_spec = pltpu.PrefetchScalarGridSpec(
        num_scalar_prefetch=1, grid=(n_steps, nt),
        in_specs=[pl.BlockSpec((ts, D_MODEL), lambda k, i, o: (first_pass(k, i), 0)),
                  pl.BlockSpec((1, D_MODEL), lambda k, i, o: (0, 0)), ANY] + [ANY] * n_small,
        out_specs=[pl.BlockSpec((ts, W_BLK), lambda k, i, o: (i, o[k])),
                   pl.BlockSpec((D_MODEL, ts), lambda k, i, o: (0, first_pass(k, i))), ANY] + [ANY] * n_small,
        scratch_shapes=[pltpu.VMEM((N_DEV, D_MODEL, W_BLK), BF16), pltpu.VMEM((S, D_MODEL), BF16),
                        pltpu.SemaphoreType.DMA, pltpu.SemaphoreType.DMA((7,)), pltpu.SemaphoreType.DMA((7,)),
                        pltpu.SemaphoreType.DMA((n_steps,)),
                        pltpu.SemaphoreType.DMA((7, n_small)), pltpu.SemaphoreType.DMA((7, n_small)),
                        pltpu.SemaphoreType.DMA((n_small,))])
    return pl.pallas_call(
        body, name="gather_in_proj", grid_spec=grid_spec,
        out_shape=[jax.ShapeDtypeStruct((S, IN_COLS), BF16), jax.ShapeDtypeStruct((D_MODEL, S), BF16),
                   jax.ShapeDtypeStruct((D_MODEL, IN_COLS), BF16)]
        + [jax.ShapeDtypeStruct((N_DEV,) + s.shape, s.dtype) for s in smalls],
        compiler_params=_params(2),
    )(order, x, norm_g, w_sh, *smalls)


def _bias_table(rel_bias):
    wide = 1024
    n_rel = 384

    def body(r_ref, o_ref):
        col = lax.broadcasted_iota(jnp.int32, (1, wide), 1)
        k_minus_q = jnp.where(col < KB, col, col - wide)
        idx = jnp.clip(PADK - k_minus_q, -MAX_REL, MAX_REL) + MAX_REL
        pick = jnp.where(lax.broadcasted_iota(jnp.int32, (n_rel, wide), 0) == idx, 1.0, 0.0).astype(F32)
        f = jnp.dot(r_ref[...], pick, precision=lax.Precision.HIGHEST, preferred_element_type=F32) * LOG2E
        kcol = lax.broadcasted_iota(jnp.int32, (1, KB), 1)
        kc = kcol >> 6
        sub = lax.broadcasted_iota(jnp.int32, (8, 1), 0)
        for h in range(HEADS):
            f8 = jnp.broadcast_to(f[h:h + 1, :], (8, wide))
            base = f8
            for r in range(1, 8):
                base = jnp.where(sub == r, pltpu.roll(f8, r, 1), base)
            for qh in range(0, QB // 8, 2):
                rows = jnp.concatenate(
                    [(pltpu.roll(base, 8 * q, 1) if q else base)[:, 0:KB] for q in (qh, qh + 1)], axis=0)
                qc = (8 * qh) // CHUNK
                band = (kc >= qc) & (kc <= qc + N_LEFT)
                o_ref[h, 8 * qh:8 * qh + 16, :] = jnp.where(band, rows, NEG_BIG).astype(BF16)

    return pl.pallas_call(
        body, name="bias_table",
        out_shape=jax.ShapeDtypeStruct((HEADS, QB, KB), BF16),
        compiler_params=pltpu.CompilerParams(vmem_limit_bytes=VMEM_LIMIT),
    )(jnp.pad(rel_bias, ((0, 0), (0, n_rel - N_REL))))


KEY_GROUP = 4


def _load_keys(g, nb, p_hbm, kp, vp, sem):
    rows = KEY_GROUP * QB
    n_groups = p_hbm.shape[0] // rows

    def copies(c):
        src = pl.ds(c * rows, rows)
        dst = pl.ds(PADK + c * rows, rows)
        return (pltpu.make_async_copy(p_hbm.at[src, D_ATT:2 * D_ATT], kp.at[dst, :], sem.at[0, c]),
                pltpu.make_async_copy(p_hbm.at[src, 2 * D_ATT:3 * D_ATT], vp.at[dst, :], sem.at[1, c]))

    @pl.when(g == 0)
    def _():
        kp[0:PADK, :] = jnp.zeros((PADK, D_ATT), BF16)
        vp[0:PADK, :] = jnp.zeros((PADK, D_ATT), BF16)
        for c in range(n_groups):
            for cp in copies(c):
                cp.start()

    @pl.when((g % KEY_GROUP == 0) & (g < nb))
    def _():
        for cp in copies(g // KEY_GROUP):
            cp.wait()


def _attn_fwd(P, bias_tab):
    S = P.shape[0]
    nb = S // QB

    def body(q_ref, p_hbm, bias_ref, o_ref, ex_ref, rinv_ref, kp, vp, tab, sem):
        g = pl.program_id(0)
        _load_keys(g, nb, p_hbm, kp, vp, sem)

        @pl.when(g * QB <= PADK)
        def _():
            kcol = lax.broadcasted_iota(jnp.int32, (1, KB), 1)
            for h in range(HEADS):
                tab[h] = jnp.where(kcol + g * QB >= PADK, bias_ref[h], NEG_BIG).astype(BF16)

        start = pl.multiple_of(g * QB, QB)
        lane = lax.broadcasted_iota(jnp.int32, (1, 128), 1)
        half = lambda h: (lane < 64) if h % 2 == 0 else (lane >= 64)
        pair = lambda h: slice(128 * (h // 2), 128 * (h // 2 + 1))

        def scores(h):
            qp = q_ref[:, pair(h)] * SCALE
            qm = jnp.where(half(h), qp, jnp.zeros_like(qp))
            return (_dot_nt(qm, kp[pl.ds(start, KB), pair(h)]) * LOG2E).astype(BF16) + tab[h]

        def numerators(h, s):
            ex = jnp.exp2(s - jnp.max(s, axis=-1, keepdims=True))
            ex_ref[:, KB * h:KB * (h + 1)] = ex
            return ex

        def weighted_values(h, ex):
            vpair = vp[pl.ds(start, KB), pair(h)]
            o = _dot(ex, jnp.where(half(h), vpair, jnp.ones_like(vpair)))
            rinv = 1.0 / pltpu.roll(o, 64, 1)
            rinv_ref[:, h:h + 1] = rinv[:, 0:1] if h % 2 == 0 else 1.0 / o[:, 0:1]
            return o * rinv

        outs = []
        s_ahead = {0: scores(0), 1: scores(1)}
        ex_ahead = {0: numerators(0, s_ahead.pop(0))}
        for h in range(HEADS):
            if h + 2 < HEADS:
                s_ahead[h + 2] = scores(h + 2)
            if h + 1 < HEADS:
                ex_ahead[h + 1] = numerators(h + 1, s_ahead.pop(h + 1))
            outs.append(weighted_values(h, ex_ahead.pop(h)))
            if h % 2 == 1:
                o_ref[:, pair(h)] = jnp.where(lane < 64, outs[h - 1], outs[h]).astype(BF16)

    return pl.pallas_call(
        body, name="attn_fwd", grid=(nb,),
        out_shape=[jax.ShapeDtypeStruct((S, D_ATT), BF16), jax.ShapeDtypeStruct((S, HEADS * KB), BF16),
                   jax.ShapeDtypeStruct((S, HEADS), F32)],
        in_specs=[pl.BlockSpec((QB, D_ATT), lambda g: (g, 0)), ANY,
                  pl.BlockSpec((HEADS, QB, KB), lambda g: (0, 0, 0))],
        out_specs=[pl.BlockSpec((QB, D_ATT), lambda g: (g, 0)),
                   pl.BlockSpec((QB, HEADS * KB), lambda g: (g, 0)),
                   pl.BlockSpec((QB, HEADS), lambda g: (g, 0))],
        scratch_shapes=[pltpu.VMEM((S + PADK, D_ATT), BF16), pltpu.VMEM((S + PADK, D_ATT), BF16),
                        pltpu.VMEM((HEADS, QB, KB), BF16),
                        pltpu.SemaphoreType.DMA((2, S // (KEY_GROUP * QB)))],
        compiler_params=_params(1),
    )(P, P, bias_tab)


def _token_local(x, tgt, P, att, proj_g, w_out, cw_g, conv_b, final_g):
    S = x.shape[0]
    ts = 256
    nt = S // ts
    hb = 16

    def body(x_ref, t_ref, s1_ref, s2_ref, s3_ref, h1_ref, h2_ref, att_ref,
             pg_ref, wo_ref, cwg_ref, cb_ref, g2_ref,
             dx2_ref, dg_ref, datt_ref, dwo_ref, dproj_ref, sm1_ref, sm2_ref,
             carry, wao_ref, wco_ref, cw_ref, dwo_acc, dwao_acc, dwco_acc):
        i = pl.program_id(0)
        t = nt - 1 - i

        @pl.when(i == 0)
        def _():
            dwo_acc[...] = jnp.zeros_like(dwo_acc)
            dwao_acc[...] = jnp.zeros_like(dwao_acc)
            dwco_acc[...] = jnp.zeros_like(dwco_acc)
            lane = lax.broadcasted_iota(jnp.int32, (1, 128), 1)
            for j in range(N_DEV):
                wao_ref[:, 128 * j:128 * (j + 1)] = pg_ref[j, :, 0:128]
                wco_ref[:, 128 * j:128 * (j + 1)] = pg_ref[j, :, 128:256]
            for p in range(N_DEV // 2):
                cw_ref[:, 128 * p:128 * (p + 1)] = jnp.where(
                    lane < 64, cwg_ref[2 * p], pltpu.roll(cwg_ref[2 * p + 1], 64, 1))
            sm1_ref[...] = jnp.zeros_like(sm1_ref)
            sm2_ref[...] = jnp.zeros_like(sm2_ref)
            carry[...] = jnp.zeros_like(carry)

        za = s1_ref[:, 0:512]
        gb = s1_ref[:, 512:1024]
        gc = s1_ref[:, 1024:1536].astype(F32)
        u = s2_ref[:, 0:512].astype(F32)
        zc = s2_ref[:, 512:1024]
        ga = jnp.concatenate([s2_ref[:, 1024:1536], s3_ref[:, 0:512]], axis=1)
        gv = s3_ref[:, 512:1536]
        att = att_ref[...]

        sa = _sigmoid(za)
        silu_a = za * sa
        att_g = att * silu_a
        y_att = _dot(att_g, wao_ref[...])

        cu = gc * u
        keep = jnp.where(t > 0, 1.0, 0.0).astype(F32)
        hcu = (h1_ref[hb - 8:hb, 1024:1536].astype(F32) * h2_ref[hb - 8:hb, 0:512].astype(F32)) * keep
        cu_ext = jnp.concatenate([hcu, cu], axis=0)
        cu_m1 = pltpu.roll(cu_ext, 1, 0)[8:]
        cu_m2 = pltpu.roll(cu_ext, 2, 0)[8:]
        w0, w1, w2 = cw_ref[0:1, :], cw_ref[1:2, :], cw_ref[2:3, :]
        vconv = w0 * cu_m2 + w1 * cu_m1 + w2 * cu + cb_ref[...]
        vcb = vconv.astype(BF16)
        sc = _sigmoid(zc)
        silu_c = zc * sc
        cg = gb * vcb * silu_c
        sga = _sigmoid(ga)
        sgv = _sigmoid(gv)
        y_conv = _dot(cg, wco_ref[...])

        yab, ycb = y_att.astype(BF16), y_conv.astype(BF16)
        m = sga * yab + sgv * ycb
        x2 = x_ref[...] + _dot(m, wo_ref[...])
        r2 = lax.rsqrt(jnp.mean(x2 * x2, axis=-1, keepdims=True) + EPS)
        xn2 = x2 * r2
        g2 = g2_ref[...]
        err = xn2 * g2 - t_ref[...]
        sm1_ref[1:2, :] += jnp.sum(err * err, axis=0, keepdims=True) * (0.5 / D_MODEL)

        dy = err * (1.0 / D_MODEL)
        sm1_ref[0:1, :] += jnp.sum(dy * xn2, axis=0, keepdims=True)
        dxn = dy * g2
        dx2 = r2 * (dxn - xn2 * jnp.mean(dxn * xn2, axis=-1, keepdims=True))
        dx2_ref[...] = dx2
        dx2b = dx2.astype(BF16)
        dwo_acc[...] += _dot_tn(m, dx2b)
        dm = _dot_nt(dx2b, wo_ref[...])
        dmb = dm.astype(BF16)
        dya = dmb * sga
        dyc = dmb * sgv
        dg_ref[:, 2560:3584] = dmb * yab * (sga * (1.0 - sga))
        dg_ref[:, 3584:4608] = dmb * ycb * (sgv * (1.0 - sgv))
        dwao_acc[...] += _dot_tn(att_g, dya)
        dwco_acc[...] += _dot_tn(cg, dyc)
        datt_g = _dot_nt(dya, wao_ref[...])
        dcg = _dot_nt(dyc, wco_ref[...])
        dagb, dcgb = datt_g.astype(BF16), dcg.astype(BF16)
        datt_ref[...] = dagb * silu_a
        dg_ref[:, 0:512] = dagb * att * (sa + silu_a * (1.0 - sa))
        dg_ref[:, 512:1024] = dcgb * vcb * silu_c
        dg_ref[:, 2048:2560] = dcgb * gb * vcb * (sc + silu_c * (1.0 - sc))
        dv = dcg * (gb * silu_c).astype(F32)
        sm2_ref[3:4, :] += jnp.sum(dv, axis=0, keepdims=True)
        sm2_ref[0:1, :] += jnp.sum(dv * cu_m2, axis=0, keepdims=True)
        sm2_ref[1:2, :] += jnp.sum(dv * cu_m1, axis=0, keepdims=True)
        sm2_ref[2:3, :] += jnp.sum(dv * cu, axis=0, keepdims=True)
        dv_ext = jnp.concatenate([dv, carry[...]], axis=0)
        dv_p1 = pltpu.roll(dv_ext, ts + 7, 0)[0:ts]
        dv_p2 = pltpu.roll(dv_ext, ts + 6, 0)[0:ts]
        dcu = w2 * dv + w1 * dv_p1 + w0 * dv_p2
        carry[...] = dv[0:8, :]
        dg_ref[:, 1024:1536] = (dcu * u).astype(BF16)
        dg_ref[:, 1536:2048] = (dcu * gc).astype(BF16)

        @pl.when(i == nt - 1)
        def _():
            dwo_ref[...] = dwo_acc[...].astype(BF16)
            for j in range(N_DEV):
                dproj_ref[j, :, 0:128] = dwao_acc[:, 128 * j:128 * (j + 1)].astype(BF16)
                dproj_ref[j, :, 128:256] = dwco_acc[:, 128 * j:128 * (j + 1)].astype(BF16)

    tile = lambda w: pl.BlockSpec((ts, w), lambda i: (nt - 1 - i, 0))
    seg = lambda c: pl.BlockSpec((ts, 1536), lambda i: (nt - 1 - i, c))
    halo = lambda c: pl.BlockSpec((hb, 1536), lambda i: (jnp.maximum((nt - 1 - i) * (ts // hb) - 1, 0), c))
    full = lambda a: pl.BlockSpec(a.shape, lambda i: (0,) * a.ndim)
    acc = lambda r, c: pl.BlockSpec((r, c), lambda i: (0, 0))
    return pl.pallas_call(
        body, name="token_local", grid=(nt,),
        out_shape=[jax.ShapeDtypeStruct((S, D_MODEL), F32), jax.ShapeDtypeStruct((S, IN_COLS), BF16),
                   jax.ShapeDtypeStruct((S, D_ATT), BF16), jax.ShapeDtypeStruct((D_MODEL, D_MODEL), BF16),
                   jax.ShapeDtypeStruct(proj_g.shape, BF16),
                   jax.ShapeDtypeStruct((8, D_MODEL), F32), jax.ShapeDtypeStruct((8, D_CONV), F32)],
        in_specs=[tile(D_MODEL), tile(D_MODEL), seg(1), seg(2), seg(3), halo(1), halo(2), tile(D_ATT),
                  full(proj_g), full(w_out), full(cw_g), full(conv_b), full(final_g)],
        out_specs=[tile(D_MODEL), tile(GATE_COLS), tile(D_ATT), acc(D_MODEL, D_MODEL), full(proj_g),
                   acc(8, D_MODEL), acc(8, D_CONV)],
        scratch_shapes=[pltpu.VMEM((8, D_CONV), F32),
                        pltpu.VMEM((D_ATT, D_MODEL), BF16), pltpu.VMEM((D_CONV, D_MODEL), BF16),
                        pltpu.VMEM((8, D_CONV), F32), pltpu.VMEM((D_MODEL, D_MODEL), F32),
                        pltpu.VMEM((D_ATT, D_MODEL), F32), pltpu.VMEM((D_CONV, D_MODEL), F32)],
        compiler_params=_params(1),
    )(x, tgt, P, P, P, P, P, att, proj_g, w_out, cw_g, conv_b, final_g)


def _fold_diagonals(d_ref, o_ref):
    wide = D_MODEL
    sub = lax.broadcasted_iota(jnp.int32, (8, 1), 0)
    col = lax.broadcasted_iota(jnp.int32, (1, wide), 1)
    pad = jnp.zeros((8, wide - KB), F32)
    for h in range(HEADS):
        acc = jnp.concatenate([d_ref[h, 0:8, :], pad], axis=1)
        for qh in range(1, QB // 8):
            a = jnp.concatenate([d_ref[h, 8 * qh:8 * qh + 8, :], pad], axis=1)
            acc = acc + pltpu.roll(a, wide - 8 * qh, 1)
        for r in range(1, 8):
            acc = jnp.where(sub == r, pltpu.roll(acc, wide - r, 1), acc)
        vec = jnp.sum(acc, axis=0, keepdims=True)
        far = (col <= PADK - MAX_REL) | (col > KB)
        tail = jnp.sum(jnp.where(far, vec, 0.0), axis=-1, keepdims=True)
        o_ref[h:h + 1, :] = jnp.where(col == wide - 1, tail, vec)


def _attn_bwd(P, att, datt, ex, rinv, dP):
    S = P.shape[0]
    nb = S // QB

    def body(q_ref, att_ref, datt_ref, ex_ref, rinv_ref, p_hbm, dp_hbm, out_ref, dbias_ref,
             kp, vp, dq_ring, dk_ring, dv_ring, db_ref, sem):
        g = pl.program_id(0)

        _load_keys(g, nb, p_hbm, kp, vp, sem)

        @pl.when(g == 0)
        def _():
            db_ref[...] = jnp.zeros_like(db_ref)
            dk_ring[...] = jnp.zeros_like(dk_ring)
            dv_ring[...] = jnp.zeros_like(dv_ring)

        s_new = g % 3
        s_mid = (g + 2) % 3
        s_old = (g + 1) % 3

        @pl.when(g < nb)
        def _():
            start = pl.multiple_of(g * QB, QB)
            lane = lax.broadcasted_iota(jnp.int32, (1, 128), 1)
            for p in range(HEADS // 2):
                cols = slice(128 * p, 128 * (p + 1))
                qp = q_ref[:, cols] * SCALE
                op = att_ref[:, cols].astype(F32)
                dop = datt_ref[:, cols]
                kpair = kp[pl.ds(start, KB), cols]
                vpair = vp[pl.ds(start, KB), cols]
                dqs = []
                dk_acc = jnp.zeros((KB, 128), F32)
                dv_acc = jnp.zeros((KB, 128), F32)
                for e in range(2):
                    h = 2 * p + e
                    lm = (lane < 64) if e == 0 else (lane >= 64)
                    qm = jnp.where(lm, qp, jnp.zeros_like(qp))
                    dom = jnp.where(lm, dop, jnp.zeros_like(dop))
                    exh = ex_ref[:, KB * h:KB * (h + 1)]
                    rinv = rinv_ref[:, h:h + 1]
                    domf = dom.astype(F32)
                    dp = _dot_nt(dom, vpair)
                    delta = jnp.sum(domf * op, axis=-1, keepdims=True)
                    dsb = exh * ((dp - delta) * rinv).astype(BF16)
                    db_ref[h] += dsb.astype(F32)
                    dqs.append(_dot(dsb, kpair) * SCALE)
                    dk_acc = dk_acc + _dot_tn(dsb, qm)
                    dv_acc = dv_acc + _dot_tn(exh, (domf * rinv).astype(BF16))
                dq_ring[s_new, :, cols] = jnp.where(lane < 64, dqs[0], dqs[1])
                dk_ring[s_old, :, cols] += dk_acc[0:QB]
                dk_ring[s_mid, :, cols] += dk_acc[QB:2 * QB]
                dk_ring[s_new, :, cols] = dk_acc[2 * QB:3 * QB]
                dv_ring[s_old, :, cols] += dv_acc[0:QB]
                dv_ring[s_mid, :, cols] += dv_acc[QB:2 * QB]
                dv_ring[s_new, :, cols] = dv_acc[2 * QB:3 * QB]

        @pl.when(g >= 2)
        def _():
            out_ref[:, 0:D_ATT] = dq_ring[s_old].astype(BF16)
            out_ref[:, D_ATT:2 * D_ATT] = dk_ring[s_old].astype(BF16)
            out_ref[:, 2 * D_ATT:3 * D_ATT] = dv_ring[s_old].astype(BF16)

        @pl.when(g == nb + 1)
        def _():
            _fold_diagonals(db_ref, dbias_ref)

    qblk = lambda w: pl.BlockSpec((QB, w), lambda g: (jnp.minimum(g, nb - 1), 0))
    return pl.pallas_call(
        body, name="attn_bwd", grid=(nb + 2,),
        out_shape=[jax.ShapeDtypeStruct((S, IN_COLS), BF16), jax.ShapeDtypeStruct((HEADS, D_MODEL), F32)],
        in_specs=[qblk(D_ATT), qblk(D_ATT), qblk(D_ATT), qblk(HEADS * KB), qblk(HEADS), ANY, ANY],
        out_specs=[pl.BlockSpec((QB, 3 * D_ATT), lambda g: (jnp.maximum(g - 2, 0), GATE_COLS // (3 * D_ATT))),
                   pl.BlockSpec((HEADS, D_MODEL), lambda g: (0, 0))],
        input_output_aliases={6: 0},
        scratch_shapes=[pltpu.VMEM((S + PADK, D_ATT), BF16), pltpu.VMEM((S + PADK, D_ATT), BF16),
                        pltpu.VMEM((3, QB, D_ATT), F32), pltpu.VMEM((3, QB, D_ATT), F32),
                        pltpu.VMEM((3, QB, D_ATT), F32), pltpu.VMEM((HEADS, QB, KB), F32),
                        pltpu.SemaphoreType.DMA((2, S // (KEY_GROUP * QB)))],
        compiler_params=_params(1),
    )(P, att, datt, ex, rinv, P, dP)


def _dp_block(j):
    return (j + GATE_COLS // W_BLK) % N_DEV


def _in_proj_bwd(x, norm_g, dx2, dP, w_in_g, d_proj, d_wo):
    S = x.shape[0]
    ts = 512
    nt = S // ts

    def body(x_ref, g_ref, dx2_ref, dp_ref, w_ref, proj_hbm, wo_hbm, gx_ref, dn_ref, rproj, rwo,
             send_sems, recv_sems, local_sems):
        i = pl.program_id(0)
        xx, yy, cc = _mesh_pos()
        my = _flat((xx, yy, cc))
        peers = [(xx, yy, 1 - cc), (1 - xx, yy, cc), (xx, 1 - yy, cc), (1 - xx, 1 - yy, cc),
                 (1 - xx, yy, 1 - cc), (xx, 1 - yy, 1 - cc), (1 - xx, 1 - yy, 1 - cc)]
        pairs = ((proj_hbm, rproj), (wo_hbm, rwo))

        def copy(q, a, receive=False):
            src, dst = pairs[a]
            slot = _flat(peers[q]) if receive else my
            return pltpu.make_async_remote_copy(
                src_ref=src.at[_flat(peers[q])], dst_ref=dst.at[slot],
                send_sem=send_sems.at[q, a], recv_sem=recv_sems.at[q, a],
                device_id=peers[q], device_id_type=MESH)

        own = [pltpu.make_async_copy(src.at[my], dst.at[my], local_sems.at[a]) for a, (src, dst) in enumerate(pairs)]

        @pl.when(i == 0)
        def _():
            dn_ref[...] = jnp.zeros_like(dn_ref)
            for cp in own:
                cp.start()
            for q in range(len(peers)):
                for a in range(2):
                    copy(q, a).start()

        @pl.when(i == nt - 1)
        def _():
            for cp in own:
                cp.wait()
            for q in range(len(peers)):
                for a in range(2):
                    copy(q, a).wait_send()
                    copy(q, a, receive=True).wait_recv()

        dh = _dot_nt(dp_ref[...], w_ref[...])
        xf = x_ref[...]
        r = lax.rsqrt(jnp.mean(xf * xf, axis=-1, keepdims=True) + EPS)
        xn = xf * r
        dn_ref[0:1, :] += jnp.sum(dh * xn, axis=0, keepdims=True)
        dhg = dh * g_ref[...]
        gx_ref[...] = dx2_ref[...] + r * (dhg - xn * jnp.mean(dhg * xn, axis=-1, keepdims=True))

    tile = lambda w: pl.BlockSpec((ts, w), lambda i: (i, 0))
    return pl.pallas_call(
        body, name="in_proj_bwd", grid=(nt,),
        out_shape=[jax.ShapeDtypeStruct((S, D_MODEL), F32), jax.ShapeDtypeStruct((8, D_MODEL), F32),
                   jax.ShapeDtypeStruct(d_proj.shape, BF16), jax.ShapeDtypeStruct(d_wo.shape, BF16)],
        in_specs=[tile(D_MODEL), pl.BlockSpec((1, D_MODEL), lambda i: (0, 0)), tile(D_MODEL),
                  tile(IN_COLS),
                  pl.BlockSpec((D_MODEL, IN_COLS), lambda i: (0, 0)), ANY, ANY],
        out_specs=[tile(D_MODEL), pl.BlockSpec((8, D_MODEL), lambda i: (0, 0)), ANY, ANY],
        scratch_shapes=[pltpu.SemaphoreType.DMA((7, 2)), pltpu.SemaphoreType.DMA((7, 2)),
                        pltpu.SemaphoreType.DMA((2,))],
        compiler_params=_params(1),
    )(x, norm_g, dx2, dP, w_in_g, d_proj, d_wo)


SCATTER_MASKS = ((3, 4, 5, 2, 7, 6, 1, 0), (5, 2, 3, 4, 7, 6, 1, 0))


def _w_in_grad_scatter(ht, dP, pack, order):
    S = ht.shape[1]
    ts = min(S, 2048)
    nt = S // ts
    n_steps = 8

    def body(order_ref, ht_ref, d_ref, pack_hbm, g_ref, rpack,
             acc, stage, rsib, rici, d2d_send, d2d_recv, ici_send, ici_recv, small_send, small_recv, local_sems):
        k, i = pl.program_id(0), pl.program_id(1)
        x, y, c = _mesh_pos()
        my = _flat((x, y, c))
        sibling = (x, y, 1 - c)
        owners = [(x ^ (1 - c), y ^ c, c), (x ^ c, y ^ (1 - c), c), (1 - x, 1 - y, c)]
        peers = [sibling, (1 - x, y, c), (x, 1 - y, c), (1 - x, 1 - y, c),
                 (1 - x, y, 1 - c), (x, 1 - y, 1 - c), (1 - x, 1 - y, 1 - c)]
        small = ((pack_hbm, rpack, False),)
        n_small = len(small)

        def small_copy(kk, a, receive=False):
            src, dst, per_peer = small[a]
            slot = _flat(peers[kk]) if receive else my
            return pltpu.make_async_remote_copy(
                src_ref=src.at[_flat(peers[kk])] if per_peer else src, dst_ref=dst.at[slot],
                send_sem=small_send.at[kk, a], recv_sem=small_recv.at[kk, a],
                device_id=peers[kk], device_id_type=MESH)

        def to_sibling(t):
            return pltpu.make_async_remote_copy(
                src_ref=stage.at[0], dst_ref=rsib.at[t % 2], send_sem=d2d_send.at[t], recv_sem=d2d_recv.at[t],
                device_id=sibling, device_id_type=MESH)

        def to_owner(t):
            return pltpu.make_async_remote_copy(
                src_ref=stage.at[1], dst_ref=rici.at[t], send_sem=ici_send.at[t], recv_sem=ici_recv.at[t],
                device_id=owners[t], device_id_type=MESH)

        own_small = [pltpu.make_async_copy(src.at[my] if per_peer else src, dst.at[my], local_sems.at[a])
                     for a, (src, dst, per_peer) in enumerate(small)]

        @pl.when((k == 0) & (i == 0))
        def _():
            for cp in own_small:
                cp.start()
            for kk in range(len(peers)):
                for a in range(n_small):
                    small_copy(kk, a).start()

        @pl.when(i == 0)
        def _():
            acc[...] = jnp.zeros_like(acc)

        acc[...] += _dot(ht_ref[...], d_ref[...])

        @pl.when(i == nt - 1)
        def _():
            for s in range(n_steps):
                @pl.when(k == s)
                def _():
                    t = s // 2
                    if s % 2 == 0:
                        if t >= 1:
                            to_sibling(t - 1).wait_send()
                        stage[0] = acc[...].astype(BF16)
                        to_sibling(t).start()
                    elif t < 3:
                        if t >= 1:
                            to_owner(t - 1).wait_send()
                        to_sibling(t).wait_recv()
                        stage[1] = (acc[...] + rsib[t % 2].astype(F32)).astype(BF16)
                        to_owner(t).start()
                    else:
                        to_sibling(t).wait_recv()
                        total = acc[...] + rsib[t % 2].astype(F32)
                        for j in range(3):
                            to_owner(j).wait_recv()
                            total = total + rici[j].astype(F32)
                        g_ref[...] = total
                        to_owner(2).wait_send()
                        to_sibling(3).wait_send()
                        for q in range(len(peers)):
                            for a in range(n_small):
                                small_copy(q, a).wait_send()
                                small_copy(q, a, receive=True).wait_recv()
                        for cp in own_small:
                            cp.wait()

    blk = (D_MODEL, W_BLK)
    grid_spec = pltpu.PrefetchScalarGridSpec(
        num_scalar_prefetch=1, grid=(n_steps, nt),
        in_specs=[pl.BlockSpec((D_MODEL, ts), lambda k, i, o: (0, i)),
                  pl.BlockSpec((ts, W_BLK), lambda k, i, o: (i, _dp_block(o[k]))),
                  ANY],
        out_specs=[pl.BlockSpec(blk, lambda k, i, o: (0, 0)), ANY],
        scratch_shapes=[pltpu.VMEM(blk, F32), pltpu.VMEM((2,) + blk, BF16),
                        pltpu.VMEM((2,) + blk, BF16), pltpu.VMEM((3,) + blk, BF16),
                        pltpu.SemaphoreType.DMA((4,)), pltpu.SemaphoreType.DMA((4,)),
                        pltpu.SemaphoreType.DMA((3,)), pltpu.SemaphoreType.DMA((3,)),
                        pltpu.SemaphoreType.DMA((7, 1)), pltpu.SemaphoreType.DMA((7, 1)),
                        pltpu.SemaphoreType.DMA((1,))])
    return pl.pallas_call(
        body, name="w_in_grad_scatter", grid_spec=grid_spec,
        out_shape=[jax.ShapeDtypeStruct(blk, F32), jax.ShapeDtypeStruct((N_DEV,) + pack.shape, F32)],
        compiler_params=_params(2),
    )(order, ht, dP, pack)


def _adamw(w, g, m, v):
    m = ADAM_B1 * m + (1.0 - ADAM_B1) * g
    v = ADAM_B2 * v + (1.0 - ADAM_B2) * (g * g)
    m_hat = m / (1.0 - ADAM_B1 ** ADAM_STEP)
    v_hat = v / (1.0 - ADAM_B2 ** ADAM_STEP)
    delta = -ADAM_LR * (m_hat / (jnp.sqrt(v_hat) + ADAM_EPS) + ADAM_WD * w)
    return delta, m, v


def _sum_adamw(parts, w, m, v, name):
    R, C = w.shape
    n = parts.shape[0]
    tr = min(R, 256)

    def body(p_ref, w_ref, m_ref, v_ref, g_ref, d_ref, nm_ref, nv_ref):
        g = p_ref[0].astype(F32)
        for s in range(1, n):
            g = g + p_ref[s].astype(F32)
        g_ref[...] = g
        d_ref[...], nm_ref[...], nv_ref[...] = _adamw(w_ref[...], g, m_ref[...], v_ref[...])

    tile = pl.BlockSpec((tr, C), lambda i: (i, 0))
    return pl.pallas_call(
        body, name=name, grid=(R // tr,),
        out_shape=[jax.ShapeDtypeStruct((R, C), F32)] * 4,
        in_specs=[pl.BlockSpec((n, tr, C), lambda i: (0, i, 0)), tile, tile, tile],
        out_specs=[tile] * 4,
        compiler_params=_params(1),
    )(parts, w, m, v)


def _adamw_mid(r_proj, r_wo, params):
    def body(rp_ref, rw_ref, *refs):
        ins, outs = refs[:9], refs[9:]

        def total(part):
            g = part(0).astype(F32)
            for s in range(1, N_DEV):
                g = g + part(s).astype(F32)
            return g

        grads = (total(lambda s: rp_ref[s, :, 0:128]), total(lambda s: rp_ref[s, :, 128:256]),
                 total(lambda s: rw_ref[s]))
        for n, g in enumerate(grads):
            w, m, v = (r[...] for r in ins[3 * n:3 * n + 3])
            outs[4 * n][...] = g
            outs[4 * n + 1][...], outs[4 * n + 2][...], outs[4 * n + 3][...] = _adamw(w, g, m, v)

    return pl.pallas_call(
        body, name="adamw_mid",
        out_shape=[jax.ShapeDtypeStruct(params[3 * n].shape, F32) for n in range(3) for _ in range(4)],
        compiler_params=pltpu.CompilerParams(vmem_limit_bytes=VMEM_LIMIT),
    )(r_proj, r_wo, *params)


def _adamw_small(r_pack, params):
    wide = 384

    def body(p_ref, *refs):
        ins, loss_ref, outs = refs[:15], refs[15], refs[16:]
        tot = p_ref[0]
        for s in range(1, N_DEV):
            tot = tot + p_ref[s]
        me = _flat(_mesh_pos())
        loss_ref[...] = jnp.sum(tot[2:3, :], axis=-1, keepdims=True)
        mine = pltpu.roll(tot[0:8, 0:D_CONV], (D_CONV - 64 * me) % D_CONV, 1)
        col = lax.broadcasted_iota(jnp.int32, (D_MODEL, wide), 0)
        idx = lax.broadcasted_iota(jnp.int32, (D_MODEL, wide), 1)
        near = (idx > MAX_REL - CHUNK) & (idx < 2 * MAX_REL) & (col == PADK + MAX_REL - idx)
        far = (idx == 2 * MAX_REL) & (col == D_MODEL - 1)
        perm = jnp.where(near | far, 1.0, 0.0).astype(F32)
        g_rel = jnp.dot(tot[8:16], perm, precision=lax.Precision.HIGHEST, preferred_element_type=F32)
        grads = (tot[0:1], tot[1:2], mine[3:6, 0:64], tot[6:7, 0:D_CONV], g_rel[:, 0:N_REL])
        for n, g in enumerate(grads):
            w, m, v = (r[...] for r in ins[3 * n:3 * n + 3])
            outs[4 * n][...] = g
            outs[4 * n + 1][...], outs[4 * n + 2][...], outs[4 * n + 3][...] = _adamw(w, g, m, v)

    return pl.pallas_call(
        body, name="adamw_small",
        out_shape=[jax.ShapeDtypeStruct((1, 1), F32)]
        + [jax.ShapeDtypeStruct(params[3 * n].shape, F32) for n in range(5) for _ in range(4)],
    )(r_pack, *params)


def _pad_row(a, width=D_MODEL):
    a = a.reshape(-1, a.shape[-1])
    return jnp.pad(a, ((0, 0), (0, width - a.shape[-1])))


def kernel(x, norm_g, w_in, rel_bias, w_att_out, conv_w, conv_b, w_conv_out, w_out, final_norm_g, loss_target, m_norm_g, m_w_in, m_rel_bias, m_w_att_out, m_conv_w, m_conv_b, m_w_conv_out, m_w_out, m_final_norm_g, v_norm_g, v_w_in, v_rel_bias, v_w_att_out, v_conv_w, v_conv_b, v_w_conv_out, v_w_out, v_final_norm_g):
    S = x.shape[1]
    x2d = x.reshape(S, D_MODEL)
    tgt = loss_target.reshape(S, D_MODEL)
    me = 4 * lax.axis_index("x") + 2 * lax.axis_index("y") + lax.axis_index("c")
    row = lambda a: a.reshape(1, D_MODEL)

    proj_sh = jnp.concatenate([w_att_out[0], w_conv_out[0]], axis=1).astype(BF16)
    cw_sh = jnp.pad(conv_w[0], ((0, 5), (0, 64)))
    P, ht, w_in_g, proj_g, w_out_g, cw_g = _gather_in_proj(
        x2d, norm_g, w_in[0].astype(BF16), [proj_sh, w_out[0].astype(BF16), cw_sh],
        me ^ _by_core(GATHER_MASKS))

    bias_tab = _bias_table(rel_bias[0])
    att, ex, rinv = _attn_fwd(P, bias_tab)
    dx2, dP, datt, d_wo, d_proj, sm1, sm2 = _token_local(
        x2d, tgt, P, att, proj_g, w_out_g.reshape(D_MODEL, D_MODEL), cw_g, conv_b, row(final_norm_g))
    dP, dbias = _attn_bwd(P, att, datt, ex, rinv, dP)
    grad_x, dnorm, r_proj, r_wo = _in_proj_bwd(x2d, norm_g, dx2, dP, w_in_g, d_proj,
                                               d_wo.reshape(N_DEV, 128, D_MODEL))

    pack = jnp.concatenate([dnorm[0:1], sm1[0:2], _pad_row(sm2[0:4]), jnp.zeros((1, D_MODEL), F32), dbias],
                           axis=0)
    g_win_sum, r_pack = _w_in_grad_scatter(ht, dP, pack, me ^ _by_core(SCATTER_MASKS))

    res = {"w_in": _sum_adamw(g_win_sum[None], w_in[0], m_w_in[0], v_w_in[0], "adamw_w_in")}
    mid = _adamw_mid(r_proj, r_wo, (w_att_out[0], m_w_att_out[0], v_w_att_out[0],
                                    w_conv_out[0], m_w_conv_out[0], v_w_conv_out[0],
                                    w_out[0], m_w_out[0], v_w_out[0]))
    for n, name in enumerate(("w_att_out", "w_conv_out", "w_out")):
        res[name] = mid[4 * n:4 * n + 4]
    small = _adamw_small(r_pack, (norm_g, m_norm_g, v_norm_g,
                                  row(final_norm_g), row(m_final_norm_g), row(v_final_norm_g),
                                  conv_w[0], m_conv_w[0], v_conv_w[0], conv_b, m_conv_b, v_conv_b,
                                  rel_bias[0], m_rel_bias[0], v_rel_bias[0]))
    loss = small[0].reshape(())
    for n, name in enumerate(("norm_g", "final_norm_g", "conv_w", "conv_b", "rel_bias")):
        res[name] = small[1 + 4 * n:5 + 4 * n]

    leading = {"norm_g": (1, D_MODEL), "final_norm_g": (D_MODEL,), "conv_b": (1, D_CONV)}
    outs = []
    for kind in range(4):
        for name in ("norm_g", "w_in", "rel_bias", "w_att_out", "conv_w", "conv_b", "w_conv_out", "w_out",
                     "final_norm_g"):
            a = res[name][kind]
            outs.append(a.reshape(leading[name]) if name in leading else a[None])
    return (loss, grad_x.reshape(1, S, D_MODEL), *outs)
```
